```python
import math
import jax, jax.numpy as jnp
from jax import lax
import numpy as np

D_MODEL = 1024
BATCH = 4
SEQ = 4096
DEPTH = 2
DEC_BATCH = 128
DEC_SEQ = 4
PAST_LEN = 8192
PAGE_SIZE = 128

N_A_LAYERS = DEPTH // 2
N_B_LAYERS = DEPTH - N_A_LAYERS

M_HEADS = 8
M_DV = D_MODEL // M_HEADS
M_DK = M_DV // 2
M_CHUNK = 64
GATE_SOFTCAP = 15.0
M_EMPTY = -1e30
M_IN = 2 * M_HEADS * M_DK + 2 * M_HEADS * M_DV + 2 * M_HEADS

WINDOW = 128
HEAD_DIM = 64
N_Q_HEADS = D_MODEL // HEAD_DIM
N_KV_HEADS = 2
Q_PER_KV = N_Q_HEADS // N_KV_HEADS
ROPE_THETA = 500000.0
ROT_DIM = HEAD_DIM // 4

N_GROUPS = 4
EXPERTS_PER_GROUP = 4
N_EXPERTS = N_GROUPS * EXPERTS_PER_GROUP
TOP_K_IN_GROUP = 2
D_EXPERT = 256

RMS_EPS = 1e-6

kernel_name = "yoco_mlstm_swa_sink_hmoe_adaln_step"


def rmsnorm(x, g):
    xf = x.astype(jnp.float32)
    y = xf * lax.rsqrt(jnp.mean(xf * xf, axis=-1, keepdims=True) + RMS_EPS)
    return (y * g.astype(jnp.float32)).astype(x.dtype)


def modulate(h, shift, scale):
    return h * (1 + scale[:, None]) + shift[:, None]


def rope_partial(x, pos):
    half = ROT_DIM // 2
    inv = ROPE_THETA ** (-jnp.arange(half, dtype=jnp.float32) / half)
    ang = pos.astype(jnp.float32)[:, None] * inv[None]
    cos = jnp.cos(ang)[:, None, :]
    sin = jnp.sin(ang)[:, None, :]
    xr = x[..., :ROT_DIM].astype(jnp.float32)
    x1, x2 = xr[..., :half], xr[..., half:]
    rot = jnp.concatenate([x1 * cos - x2 * sin, x2 * cos + x1 * sin], axis=-1).astype(x.dtype)
    return jnp.concatenate([rot, x[..., ROT_DIM:]], axis=-1)


def mlstm_chunk(carry, inp):
    C, n, m = carry
    q, k, v, li, lf = inp
    L = q.shape[2]
    b = jnp.cumsum(lf, axis=-1)
    causal = jnp.tril(jnp.ones((L, L), dtype=bool))
    dmat = jnp.where(causal, b[..., :, None] - b[..., None, :] + li[..., None, :], -jnp.inf)
    inter = b + m[..., None]
    m_t = jnp.maximum(inter, jnp.max(dmat, axis=-1))
    w = jnp.exp(dmat - m_t[..., None])
    a = jnp.exp(inter - m_t)
    s = jnp.einsum('bhtk,bhsk->bhts', q, k) * w
    num = a[..., None] * jnp.einsum('bhtk,bhkv->bhtv', q, C) + jnp.einsum('bhts,bhsv->bhtv', s, v)
    den = a * jnp.einsum('bhtk,bhk->bht', q, n) + jnp.sum(s, axis=-1)
    h = num / jnp.maximum(jnp.abs(den), jnp.exp(-m_t))[..., None]
    bL = b[..., -1]
    dl = bL[..., None] - b + li
    m_new = jnp.maximum(bL + m, jnp.max(dl, axis=-1))
    a_st = jnp.exp(bL + m - m_new)
    w_st = jnp.exp(dl - m_new[..., None])
    C_new = a_st[..., None, None] * C + jnp.einsum('bhs,bhsk,bhsv->bhkv', w_st, k, v)
    n_new = a_st[..., None] * n + jnp.einsum('bhs,bhsk->bhk', w_st, k)
    return (C_new, n_new, m_new), h


def mlstm_mixer(u, w_in, b_gates, g_head, w_out, C0, n0, m0):
    B, T, _ = u.shape
    qk = M_HEADS * M_DK
    vd = M_HEADS * M_DV
    proj = u @ w_in
    q, k, v, o, gates = jnp.split(proj, [qk, 2 * qk, 2 * qk + vd, 2 * qk + 2 * vd], axis=-1)
    gates = (gates + b_gates).astype(jnp.float32)
    gates = GATE_SOFTCAP * jnp.tanh(gates / GATE_SOFTCAP)
    li, fpre = jnp.split(gates, 2, axis=-1)
    lf = jax.nn.log_sigmoid(fpre)
    L = math.gcd(T, M_CHUNK)
    nc = T // L

    def to_chunks(a, d):
        return a.astype(jnp.float32).reshape(B, nc, L, M_HEADS, d).transpose(1, 0, 3, 2, 4)

    def gate_chunks(a):
        return a.reshape(B, nc, L, M_HEADS).transpose(1, 0, 3, 2)

    xs = (to_chunks(q, M_DK) * (M_DK ** -0.5), to_chunks(k, M_DK), to_chunks(v, M_DV),
          gate_chunks(li), gate_chunks(lf))
    carry0 = (C0.astype(jnp.float32), n0.astype(jnp.float32), m0.astype(jnp.float32))
    (C, n, m), h = lax.scan(mlstm_chunk, carry0, xs)
    h = h.transpose(1, 0, 3, 2, 4).reshape(B, T, M_HEADS, M_DV)
    h = rmsnorm(h, g_head.reshape(M_HEADS, M_DV)).reshape(B, T, vd).astype(u.dtype)
    h = h * jax.nn.sigmoid(o)
    return h @ w_out, C, n, m


def shared_kv(x, cs, kv_ada_w, kv_ada_b, kv_norm, w_k, w_v, k_norm, pos):
    B, T, _ = x.shape
    shift, scale = jnp.split(cs @ kv_ada_w + kv_ada_b, 2, axis=-1)
    u = modulate(rmsnorm(x, kv_norm), shift, scale)
    k = (u @ w_k).reshape(B, T, N_KV_HEADS, HEAD_DIM)
    v = (u @ w_v).reshape(B, T, N_KV_HEADS, HEAD_DIM)
    k = rope_partial(rmsnorm(k, k_norm), pos)
    return k, v


def sink_attention(q, k, v, mask, sinks):
    s = jnp.einsum('bnqhgd,bnkhd->bnhgqk', q, k).astype(jnp.float32) * (HEAD_DIM ** -0.5)
    s = jnp.where(mask[None, :, None, None], s, -jnp.inf)
    sink = jnp.broadcast_to(sinks.astype(jnp.float32)[None, None, :, :, None, None], s.shape[:-1] + (1,))
    p = jax.nn.softmax(jnp.concatenate([s, sink], axis=-1), axis=-1)[..., :-1]
    return jnp.einsum('bnhgqk,bnkhd->bnqhgd', p.astype(v.dtype), v)


def swa_mixer(u, kb, vb, q_pos_blk, k_pos_blk, w_q, q_norm, sinks, w_o):
    B, T, _ = u.shape
    N, Tq = q_pos_blk.shape
    q = (u @ w_q).reshape(B, T, N_Q_HEADS, HEAD_DIM)
    q = rope_partial(rmsnorm(q, q_norm), q_pos_blk.reshape(T))
    qb = q.reshape(B, N, Tq, N_KV_HEADS, Q_PER_KV, HEAD_DIM)
    d = q_pos_blk[:, :, None] - k_pos_blk[:, None, :]
    mask = (d >= 0) & (d < WINDOW) & (k_pos_blk[:, None, :] >= 0)
    o = sink_attention(qb, kb, vb, mask, sinks.reshape(N_KV_HEADS, Q_PER_KV))
    return o.reshape(B, T, N_Q_HEADS * HEAD_DIM) @ w_o


def hier_moe(u, w_group, b_group, w_expert, b_expert, w_gate_up, w_down):
    B, T, D = u.shape
    xt = u.reshape(B * T, D)
    g_prob = jax.nn.softmax((xt @ w_group + b_group).astype(jnp.float32), axis=-1)
    g_w, g_idx = lax.top_k(g_prob, 1)
    e_logits = (xt @ w_expert + b_expert).astype(jnp.float32).reshape(-1, N_GROUPS, EXPERTS_PER_GROUP)
    e_in = jnp.take_along_axis(e_logits, g_idx[:, :, None], axis=1)[:, 0]
    e_w, e_idx = lax.top_k(jax.nn.softmax(e_in, axis=-1), TOP_K_IN_GROUP)
    w = g_w * e_w / jnp.sum(e_w, axis=-1, keepdims=True)
    gidx = g_idx * EXPERTS_PER_GROUP + e_idx
    gate = jnp.sum(jax.nn.one_hot(gidx, N_EXPERTS, dtype=jnp.float32) * w[..., None], axis=1)
    hu = jnp.einsum('nd,edf->nef', xt, w_gate_up)
    hg, hv = jnp.split(hu, 2, axis=-1)
    act = jax.nn.silu(hg) * hv * gate[..., None].astype(u.dtype)
    return jnp.einsum('nef,efd->nd', act, w_down).reshape(B, T, D)


def trunk(x, c, start, C0, n0, m0, k_buf, v_buf,
          ada_w, ada_b, norm_mix, norm_ffn, a_w_in, a_b_gates, a_head_norm, a_w_out,
          kv_ada_w, kv_ada_b, kv_norm, w_k, w_v, k_norm, b_w_q, b_q_norm, b_sinks, b_w_o,
          moe_w_group, moe_b_group, moe_w_expert, moe_b_expert, moe_w_gate_up, moe_w_down):
    B, T, _ = x.shape
    pos = start + jnp.arange(T, dtype=jnp.int32)
    cs = jax.nn.silu(c)
    new_C, new_n, new_m = [], [], []
    kb = vb = q_pos_blk = k_pos_blk = k_win = v_win = None
    for l in range(DEPTH):
        sh1, sc1, g1, sh2, sc2, g2 = jnp.split(cs @ ada_w[l] + ada_b[l], 6, axis=-1)
        u = modulate(rmsnorm(x, norm_mix[l]), sh1, sc1)
        if l < N_A_LAYERS:
            h, Cl, nl, ml = mlstm_mixer(u, a_w_in[l], a_b_gates[l], a_head_norm[l], a_w_out[l],
                                        C0[l], n0[l], m0[l])
            new_C.append(Cl)
            new_n.append(nl)
            new_m.append(ml)
        else:
            j = l - N_A_LAYERS
            h = swa_mixer(u, kb, vb, q_pos_blk, k_pos_blk, b_w_q[j], b_q_norm[j], b_sinks[j], b_w_o[j])
        x = x + g1[:, None] * h
        u = modulate(rmsnorm(x, norm_ffn[l]), sh2, sc2)
        x = x + g2[:, None] * hier_moe(u, moe_w_group[l], moe_b_group[l], moe_w_expert[l],
                                       moe_b_expert[l], moe_w_gate_up[l], moe_w_down[l])
        if l == N_A_LAYERS - 1:
            k, v = shared_kv(x, cs, kv_ada_w, kv_ada_b, kv_norm, w_k, w_v, k_norm, pos)
            if k_buf is None:
                nb = T // WINDOW
                pad = ((0, 0), (WINDOW, 0), (0, 0), (0, 0))
                kp_ = jnp.pad(k, pad).reshape(B, nb + 1, WINDOW, N_KV_HEADS, HEAD_DIM)
                vp_ = jnp.pad(v, pad).reshape(B, nb + 1, WINDOW, N_KV_HEADS, HEAD_DIM)
                kb = jnp.concatenate([kp_[:, :-1], kp_[:, 1:]], axis=2)
                vb = jnp.concatenate([vp_[:, :-1], vp_[:, 1:]], axis=2)
                q_pos_blk = pos.reshape(nb, WINDOW)
                kpos = (start + jnp.arange(-WINDOW, T, dtype=jnp.int32)).reshape(nb + 1, WINDOW)
                k_pos_blk = jnp.concatenate([kpos[:-1], kpos[1:]], axis=1)
                k_win, v_win = k[:, T - WINDOW:], v[:, T - WINDOW:]
            else:
                W = k_buf.shape[1]
                kc = jnp.concatenate([k_buf.astype(k.dtype), k], axis=1)
                vc = jnp.concatenate([v_buf.astype(v.dtype), v], axis=1)
                kb, vb = kc[:, None], vc[:, None]
                q_pos_blk = pos[None]
                k_pos_blk = (start - W + jnp.arange(W + T, dtype=jnp.int32))[None]
                k_win, v_win = kc[:, -WINDOW:], vc[:, -WINDOW:]
    return x, jnp.stack(new_C), jnp.stack(new_n), jnp.stack(new_m), k_win, v_win


def setup_inputs(seed: int = 0) -> dict:
    key = jax.random.key(seed)
    ks = jax.random.split(key, 40)
    f32 = jnp.float32

    def nrm(i, shape, scale):
        return jax.random.normal(ks[i], shape, f32) * scale

    D = D_MODEL
    qd = N_Q_HEADS * HEAD_DIM
    kvd = N_KV_HEADS * HEAD_DIM
    vd = M_HEADS * M_DV
    b_gates = jnp.concatenate([nrm(20, (N_A_LAYERS, M_HEADS), 0.1),
                               3.0 + nrm(21, (N_A_LAYERS, M_HEADS), 0.5)], axis=-1)
    return {
        "x_prompt": nrm(0, (BATCH, SEQ, D), 1.0),
        "x_sample": nrm(1, (DEC_BATCH, DEC_SEQ, D), 1.0),
        "c_prompt": nrm(2, (BATCH, D), 1.0),
        "c_sample": nrm(3, (DEC_BATCH, D), 1.0),
        "state_c": nrm(4, (N_A_LAYERS, DEC_BATCH, M_HEADS, M_DK, M_DV), 0.5),
        "state_n": nrm(5, (N_A_LAYERS, DEC_BATCH, M_HEADS, M_DK), 0.5),
        "state_m": nrm(6, (N_A_LAYERS, DEC_BATCH, M_HEADS), 1.0),
        "cache_k_win": nrm(7, (DEC_BATCH, WINDOW, N_KV_HEADS, HEAD_DIM), 1.0),
        "cache_v_win": nrm(8, (DEC_BATCH, WINDOW, N_KV_HEADS, HEAD_DIM), 1.0),
        "ada_w": nrm(9, (DEPTH, D, 6 * D), 0.5 * D ** -0.5),
        "ada_b": nrm(10, (DEPTH, 6 * D), 0.02),
        "norm_mix": 1.0 + nrm(11, (DEPTH, D), 0.02),
        "norm_ffn": 1.0 + nrm(12, (DEPTH, D), 0.02),
        "a_w_in": nrm(13, (N_A_LAYERS, D, M_IN), D ** -0.5),
        "a_b_gates": b_gates,
        "a_head_norm": 1.0 + nrm(14, (N_A_LAYERS, vd), 0.02),
        "a_w_out": nrm(15, (N_A_LAYERS, vd, D), vd ** -0.5),
        "kv_ada_w": nrm(16, (D, 2 * D), 0.5 * D ** -0.5),
        "kv_ada_b": nrm(17, (2 * D,), 0.02),
        "kv_norm": 1.0 + nrm(18, (D,), 0.02),
        "w_k": nrm(19, (D, kvd), D ** -0.5),
        "w_v": nrm(22, (D, kvd), D ** -0.5),
        "k_norm": 1.0 + nrm(23, (HEAD_DIM,), 0.02),
        "b_w_q": nrm(24, (N_B_LAYERS, D, qd), D ** -0.5),
        "b_q_norm": 1.0 + nrm(25, (N_B_LAYERS, HEAD_DIM), 0.02),
        "b_sinks": nrm(26, (N_B_LAYERS, N_Q_HEADS), 0.5),
        "b_w_o": nrm(27, (N_B_LAYERS, qd, D), qd ** -0.5),
        "moe_w_group": nrm(28, (DEPTH, D, N_GROUPS), D ** -0.5),
        "moe_b_group": nrm(29, (DEPTH, N_GROUPS), 0.01),
        "moe_w_expert": nrm(30, (DEPTH, D, N_EXPERTS), D ** -0.5),
        "moe_b_expert": nrm(31, (DEPTH, N_EXPERTS), 0.01),
        "moe_w_gate_up": nrm(32, (DEPTH, N_EXPERTS, D, 2 * D_EXPERT), D ** -0.5),
        "moe_w_down": nrm(33, (DEPTH, N_EXPERTS, D_EXPERT, D), D_EXPERT ** -0.5),
    }


def reference(x_prompt, x_sample, c_prompt, c_sample, state_c, state_n, state_m, cache_k_win, cache_v_win,
              ada_w, ada_b, norm_mix, norm_ffn, a_w_in, a_b_gates, a_head_norm, a_w_out,
              kv_ada_w, kv_ada_b, kv_norm, w_k, w_v, k_norm, b_w_q, b_q_norm, b_sinks, b_w_o,
              moe_w_group, moe_b_group, moe_w_expert, moe_b_expert, moe_w_gate_up, moe_w_down):
    weights = (ada_w, ada_b, norm_mix, norm_ffn, a_w_in, a_b_gates, a_head_norm, a_w_out,
               kv_ada_w, kv_ada_b, kv_norm, w_k, w_v, k_norm, b_w_q, b_q_norm, b_sinks, b_w_o,
               moe_w_group, moe_b_group, moe_w_expert, moe_b_expert, moe_w_gate_up, moe_w_down)
    Bp = x_prompt.shape[0]
    C0 = jnp.zeros((N_A_LAYERS, Bp, M_HEADS, M_DK, M_DV), jnp.float32)
    n0 = jnp.zeros((N_A_LAYERS, Bp, M_HEADS, M_DK), jnp.float32)
    m0 = jnp.full((N_A_LAYERS, Bp, M_HEADS), M_EMPTY, jnp.float32)
    y_p, c_p, n_p, m_p, kw_p, vw_p = trunk(x_prompt, c_prompt, 0, C0, n0, m0, None, None, *weights)
    y_s, c_s, n_s, m_s, kw_s, vw_s = trunk(x_sample, c_sample, PAST_LEN, state_c, state_n, state_m,
                                           cache_k_win, cache_v_win, *weights)
    return (y_p, y_s, c_p, n_p, m_p, kw_p, vw_p, c_s, n_s, m_s, kw_s, vw_s)
```

```python
import functools

import jax
import jax.numpy as jnp
from jax import lax
from jax.experimental import pallas as pl
from jax.experimental.pallas import tpu as pltpu

F32 = jnp.float32
BF16 = jnp.bfloat16

D_MODEL = 1024
PAST_LEN = 8192
M_HEADS = 8
M_DK = 64
M_DV = 128
M_CHUNK = 64
GATE_SOFTCAP = 15.0
M_EMPTY = -1e30
WINDOW = 128
HEAD_DIM = 64
N_Q_HEADS = 16
N_KV_HEADS = 2
ROPE_THETA = 500000.0
ROT_DIM = 16
N_GROUPS = 4
EXPERTS_PER_GROUP = 4
N_EXPERTS = 16
D_EXPERT = 256
RMS_EPS = 1e-6

LANES = 128
QK_COLS = 2 * M_HEADS * M_DK
V_COLS = M_HEADS * M_DV
IN_COLS = QK_COLS + 2 * V_COLS + 2 * LANES
SAMPLE_PAD = 16
VMEM_LIMIT = 52 * 1024 * 1024

NT_DIMS = (((1,), (1,)), ((), ()))


def _params(n_axes):
    return pltpu.CompilerParams(dimension_semantics=("arbitrary",) * n_axes,
                                vmem_limit_bytes=VMEM_LIMIT)


def _dot(a, b):
    return jnp.dot(a, b, preferred_element_type=F32)


def _dot_nt(a, b):
    return lax.dot_general(a, b, NT_DIMS, preferred_element_type=F32)


def _split2(x):
    hi = x.astype(BF16)
    lo = (x - hi.astype(F32)).astype(BF16)
    return hi, lo


def _split3(x):
    hi = x.astype(BF16)
    r = x - hi.astype(F32)
    mid = r.astype(BF16)
    lo = (r - mid.astype(F32)).astype(BF16)
    return hi, mid, lo


def _unit_rms(x):
    return x * lax.rsqrt(jnp.mean(x * x, axis=-1, keepdims=True) + RMS_EPS)


def _ada_kernel(c_ref, w_ref, b_ref, o_ref):
    c = c_ref[...]
    cs = (c * jax.nn.sigmoid(c)).astype(BF16)
    o_ref[...] = _dot(cs, w_ref[...].astype(BF16)) + b_ref[...]


def _ada_call(c, w, b):
    g, d, n = w.shape
    r = c.shape[0]
    tn = 1024
    return pl.pallas_call(
        _ada_kernel,
        out_shape=jax.ShapeDtypeStruct((g, r, n), F32),
        grid=(g, n // tn),
        in_specs=[pl.BlockSpec((r, d), lambda i, j: (0, 0)),
                  pl.BlockSpec((None, d, tn), lambda i, j: (i, 0, j)),
                  pl.BlockSpec((None, 1, tn), lambda i, j: (i, 0, j))],
        out_specs=pl.BlockSpec((None, r, tn), lambda i, j: (i, 0, j)),
        compiler_params=_params(2),
        name="ada_mod",
    )(c, w, b)


def _inproj_kernel(x_ref, sh_ref, sc_ref, nw_ref, w_ref, bli_ref, blf_ref,
                   qk_ref, v_ref, o_ref, li_ref, lf_ref):
    u = _unit_rms(x_ref[...]) * nw_ref[...] * (1.0 + sc_ref[...]) + sh_ref[...]
    ub = u.astype(BF16)
    half = QK_COLS // 2
    q = _dot(ub, w_ref[:, 0:half]) * (M_DK ** -0.5)
    qk_ref[:, 0:half] = q.astype(BF16)
    qk_ref[:, half:QK_COLS] = _dot(ub, w_ref[:, half:QK_COLS]).astype(BF16)
    v_ref[...] = _dot(ub, w_ref[:, QK_COLS:QK_COLS + V_COLS]).astype(BF16)
    o_ref[...] = _dot(ub, w_ref[:, QK_COLS + V_COLS:QK_COLS + 2 * V_COLS]).astype(BF16)
    g0 = QK_COLS + 2 * V_COLS
    lane = lax.broadcasted_iota(jnp.int32, (1, LANES), 1)
    live = lane < M_HEADS
    gi = _dot(ub, w_ref[:, g0:g0 + LANES]) + bli_ref[...]
    gf = _dot(ub, w_ref[:, g0 + LANES:g0 + 2 * LANES]) + blf_ref[...]
    li = GATE_SOFTCAP * jnp.tanh(gi / GATE_SOFTCAP)
    fpre = GATE_SOFTCAP * jnp.tanh(gf / GATE_SOFTCAP)
    lf = jnp.minimum(fpre, 0.0) - jnp.log1p(jnp.exp(-jnp.abs(fpre)))
    li_ref[...] = jnp.where(live, li, 0.0)
    lf_ref[...] = jnp.where(live, lf, 0.0)


def _inproj_call(x, mod, mspec, nw, w, bli, blf, tm):
    n = x.shape[0]
    row = lambda i: (i, 0)
    const = lambda i: (0, 0)
    return pl.pallas_call(
        _inproj_kernel,
        out_shape=(jax.ShapeDtypeStruct((n, QK_COLS), BF16),
                   jax.ShapeDtypeStruct((n, V_COLS), BF16),
                   jax.ShapeDtypeStruct((n, V_COLS), BF16),
                   jax.ShapeDtypeStruct((n, LANES), F32),
                   jax.ShapeDtypeStruct((n, LANES), F32)),
        grid=(n // tm,),
        in_specs=[pl.BlockSpec((tm, D_MODEL), row), mspec(0), mspec(1),
                  pl.BlockSpec((1, D_MODEL), const),
                  pl.BlockSpec((D_MODEL, IN_COLS), const),
                  pl.BlockSpec((1, LANES), const), pl.BlockSpec((1, LANES), const)],
        out_specs=(pl.BlockSpec((tm, QK_COLS), row), pl.BlockSpec((tm, V_COLS), row),
                   pl.BlockSpec((tm, V_COLS), row), pl.BlockSpec((tm, LANES), row),
                   pl.BlockSpec((tm, LANES), row)),
        compiler_params=_params(1),
        name="mlstm_inproj",
    )(x, mod, mod, nw, w, bli, blf)


def _mlstm_kernel(L, qk_ref, v_ref, li_ref, lf_ref, c0_ref, n0_ref, m0_ref,
                  h_ref, cout_ref, nout_ref, mout_ref, c2_s, n_s, m_s):
    c = pl.program_id(1)
    nc = pl.num_programs(1)
    npairs = M_HEADS // 2
    hd = M_DK

    lane = lax.broadcasted_iota(jnp.int32, (1, LANES), 1)
    lo128 = lane < hd
    lane256 = lax.broadcasted_iota(jnp.int32, (1, 2 * LANES), 1)
    lo256 = lane256 < LANES
    row128 = lax.broadcasted_iota(jnp.int32, (LANES, 1), 0)
    top = row128 < hd
    blockdiag = (top & lo256) | (jnp.logical_not(top) & jnp.logical_not(lo256))

    @pl.when(c == 0)
    def _():
        m_s[...] = m0_ref[...]
        n_s[...] = n0_ref[...]
        z = jnp.zeros((M_DK, M_DV), F32)
        for j in range(npairs):
            ce = c0_ref[2 * j]
            co = c0_ref[2 * j + 1]
            c2_s[j] = jnp.concatenate([jnp.concatenate([ce, z], 1),
                                       jnp.concatenate([z, co], 1)], 0)

    LI = li_ref[...]
    LF = lf_ref[...]
    rowL = lax.broadcasted_iota(jnp.int32, (L, LANES), 0)

    def prefix(x, op, ident):
        d = 1
        while d < L:
            shifted = pltpu.roll(x, d, axis=0)
            x = op(x, jnp.where(rowL >= d, shifted, ident))
            d *= 2
        return x

    Bc = prefix(LF, jnp.add, 0.0)
    Cm = LI - Bc
    mprev = m_s[...]
    Gc = jnp.maximum(mprev, prefix(Cm, jnp.maximum, -jnp.inf))
    A = jnp.exp(mprev - Gc)
    EXPM = jnp.exp(-(Bc + Gc))
    bL = Bc[L - 1:L, :]
    DL = bL + Cm
    mnew = jnp.maximum(bL + mprev, jnp.max(DL, axis=0, keepdims=True))
    ast = jnp.exp(bL + mprev - mnew)
    WST = jnp.exp(DL - mnew)

    def pad_rows(x, rows):
        if x.shape[0] == rows:
            return x
        return jnp.concatenate([x, jnp.zeros((rows - x.shape[0], x.shape[1]), x.dtype)], 0)

    cm_pad = pad_rows(Cm, hd)
    XT = jnp.concatenate([cm_pad, cm_pad], 0).T
    causal = (lane & (hd - 1)) <= rowL
    J = jnp.where(blockdiag, 1.0, 0.0).astype(BF16)

    def bc(X, h):
        return jnp.broadcast_to(X[:, h:h + 1], X.shape)

    for j in range(npairs):
        he, ho = 2 * j, 2 * j + 1

        def pair128(X):
            return jnp.where(lo128, bc(X, he), bc(X, ho))

        def pair256(X):
            return jnp.concatenate([bc(X, he), bc(X, ho)], 1)

        q128 = qk_ref[:, LANES * j:LANES * (j + 1)]
        k128 = qk_ref[:, QK_COLS // 2 + LANES * j:QK_COLS // 2 + LANES * (j + 1)]
        v256 = v_ref[:, 2 * LANES * j:2 * LANES * (j + 1)]
        zk = jnp.zeros_like(k128)
        zv = jnp.zeros_like(v256)
        K2t = jnp.concatenate([pad_rows(jnp.where(lo128, k128, zk), hd),
                               pad_rows(jnp.where(lo128, zk, k128), hd)], 0)
        V2 = jnp.concatenate([pad_rows(jnp.where(lo256, v256, zv), hd),
                              pad_rows(jnp.where(lo256, zv, v256), hd)], 0)
        S = _dot_nt(q128, K2t)
        crow = jnp.where(lo128, XT[he:he + 1, :], XT[ho:ho + 1, :])
        arg = jnp.where(causal, crow - pair128(Gc), -jnp.inf)
        Sw = (S * jnp.exp(arg)).astype(BF16)
        num_intra = _dot(Sw, V2)
        rowsum = _dot(Sw, J)
        C2 = c2_s[j]
        inter = _dot(q128, C2.astype(BF16))
        npair = n_s[j:j + 1, :]
        Nrow2 = jnp.concatenate(
            [jnp.broadcast_to(jnp.where(lo128, npair, 0.0), (LANES, LANES)),
             jnp.broadcast_to(jnp.where(lo128, 0.0, npair), (LANES, LANES))], 0).astype(BF16)
        qn = _dot_nt(q128, Nrow2)
        a256 = pair256(A)
        num = a256 * inter + num_intra
        den = a256 * qn + rowsum
        h = num / jnp.maximum(jnp.abs(den), pair256(EXPM))
        h_ref[:, 2 * LANES * j:2 * LANES * (j + 1)] = h.astype(h_ref.dtype)

        kw = k128.astype(F32) * pair128(WST)
        n_s[j:j + 1, :] = pair128(ast) * npair + jnp.sum(kw, axis=0, keepdims=True)
        kwT = pad_rows(kw, LANES).T.astype(BF16)
        dC = _dot(kwT, pad_rows(v256, LANES))
        c2_s[j] = pair256(ast) * C2 + jnp.where(blockdiag, dC, 0.0)

    m_s[...] = mnew

    @pl.when(c == nc - 1)
    def _():
        for j in range(npairs):
            C2 = c2_s[j]
            cout_ref[2 * j] = C2[0:M_DK, 0:M_DV]
            cout_ref[2 * j + 1] = C2[M_DK:2 * M_DK, M_DV:2 * M_DV]
        nout_ref[...] = n_s[...]
        mout_ref[...] = mnew


def _mlstm_call(qk, v, li, lf, c0, n0, m0, B, nc, L):
    n = qk.shape[0]
    npairs = M_HEADS // 2
    tok = lambda b, c: (b * nc + c, 0)
    st4 = lambda b, c: (b, 0, 0, 0)
    st3 = lambda b, c: (b, 0, 0)
    return pl.pallas_call(
        functools.partial(_mlstm_kernel, L),
        out_shape=(jax.ShapeDtypeStruct((n, V_COLS), BF16),
                   jax.ShapeDtypeStruct((B, M_HEADS, M_DK, M_DV), F32),
                   jax.ShapeDtypeStruct((B, npairs, LANES), F32),
                   jax.ShapeDtypeStruct((B, 1, LANES), F32)),
        grid=(B, nc),
        in_specs=[pl.BlockSpec((L, QK_COLS), tok), pl.BlockSpec((L, V_COLS), tok),
                  pl.BlockSpec((L, LANES), tok), pl.BlockSpec((L, LANES), tok),
                  pl.BlockSpec((None, M_HEADS, M_DK, M_DV), st4),
                  pl.BlockSpec((None, npairs, LANES), st3),
                  pl.BlockSpec((None, 1, LANES), st3)],
        out_specs=(pl.BlockSpec((L, V_COLS), tok),
                   pl.BlockSpec((None, M_HEADS, M_DK, M_DV), st4),
                   pl.BlockSpec((None, npairs, LANES), st3),
                   pl.BlockSpec((None, 1, LANES), st3)),
        scratch_shapes=[pltpu.VMEM((npairs, 2 * M_DK, 2 * M_DV), F32),
                        pltpu.VMEM((npairs, LANES), F32),
                        pltpu.VMEM((1, LANES), F32)],
        compiler_params=_params(2),
        name="mlstm_chunks",
    )(qk, v, li, lf, c0, n0, m0)


def _route(lg):
    lane = lax.broadcasted_iota(jnp.int32, lg.shape, 1)
    lanef = lane.astype(F32)
    neg = -jnp.inf
    far = float(LANES)
    gm = lane < N_GROUPS
    lgm = jnp.where(gm, lg, neg)
    gmax = jnp.max(lgm, axis=-1, keepdims=True)
    gsum = jnp.sum(jnp.exp(lgm - gmax), axis=-1, keepdims=True)
    g_w = 1.0 / gsum
    gidx = jnp.min(jnp.where(gm & (lg == gmax), lanef, far), axis=-1, keepdims=True)
    first = N_GROUPS + EXPERTS_PER_GROUP * gidx
    sel = (lanef >= first) & (lanef < first + EXPERTS_PER_GROUP)
    l1 = jnp.max(jnp.where(sel, lg, neg), axis=-1, keepdims=True)
    i1 = jnp.min(jnp.where(sel & (lg == l1), lanef, far), axis=-1, keepdims=True)
    sel2 = sel & (lanef != i1)
    l2 = jnp.max(jnp.where(sel2, lg, neg), axis=-1, keepdims=True)
    i2 = jnp.min(jnp.where(sel2 & (lg == l2), lanef, far), axis=-1, keepdims=True)
    r = jnp.exp(l2 - l1)
    w1 = g_w / (1.0 + r)
    w2 = w1 * r
    return jnp.where(lanef == i1, w1, jnp.where(lanef == i2, w2, 0.0))


def _post_kernel(mlstm, *refs):
    if mlstm:
        (h_ref, o_ref, x_ref, g1_ref, sh2_ref, sc2_ref, nf_ref, hn_ref, wout_ref,
         wrh_ref, wrl_ref, br_ref, x1_ref, u2_ref, gate_ref) = refs
        hf = h_ref[...].astype(F32)
        parts = [_unit_rms(hf[:, M_DV * i:M_DV * (i + 1)]) for i in range(M_HEADS)]
        hn = jnp.concatenate(parts, 1) * hn_ref[...]
        hg = (hn * jax.nn.sigmoid(o_ref[...].astype(F32))).astype(BF16)
    else:
        (h_ref, x_ref, g1_ref, sh2_ref, sc2_ref, nf_ref, wout_ref,
         wrh_ref, wrl_ref, br_ref, x1_ref, u2_ref, gate_ref) = refs
        hg = h_ref[...]
    x1 = x_ref[...] + g1_ref[...] * _dot(hg, wout_ref[...])
    x1_ref[...] = x1
    u2 = _unit_rms(x1) * nf_ref[...] * (1.0 + sc2_ref[...]) + sh2_ref[...]
    u2_ref[...] = u2.astype(BF16)
    uh, ul = _split2(u2)
    lg = _dot(uh, wrh_ref[...]) + _dot(ul, wrh_ref[...]) + _dot(uh, wrl_ref[...]) + br_ref[...]
    gate_ref[...] = _route(lg)


def _post_call(mlstm, h, o, x, mod, mspec, nf, hn, wout, wrh, wrl, br, tm):
    n = x.shape[0]
    row = lambda i: (i, 0)
    const = lambda i: (0, 0)
    tok_bf = pl.BlockSpec((tm, D_MODEL), row)
    vec = pl.BlockSpec((1, D_MODEL), const)
    ins = [h] + ([o] if mlstm else []) + [x, mod, mod, mod, nf] + ([hn] if mlstm else []) \
        + [wout, wrh, wrl, br]
    specs = [tok_bf] + ([tok_bf] if mlstm else []) + [tok_bf, mspec(2), mspec(3), mspec(4), vec] \
        + ([vec] if mlstm else []) \
        + [pl.BlockSpec((D_MODEL, D_MODEL), const), pl.BlockSpec((D_MODEL, LANES), const),
           pl.BlockSpec((D_MODEL, LANES), const), pl.BlockSpec((1, LANES), const)]
    return pl.pallas_call(
        functools.partial(_post_kernel, mlstm),
        out_shape=(jax.ShapeDtypeStruct((n, D_MODEL), F32),
                   jax.ShapeDtypeStruct((n, D_MODEL), BF16),
                   jax.ShapeDtypeStruct((n, LANES), F32)),
        grid=(n // tm,),
        in_specs=specs,
        out_specs=(pl.BlockSpec((tm, D_MODEL), row), pl.BlockSpec((tm, D_MODEL), row),
                   pl.BlockSpec((tm, LANES), row)),
        compiler_params=_params(1),
        name="post_mlstm" if mlstm else "post_attn",
    )(*ins)


def _moe_kernel(u_ref, gate_ref, wgu_ref, wd_ref, x1_ref, g2_ref, y_ref, acc_ref):
    e = pl.program_id(1)

    @pl.when(e == 0)
    def _():
        acc_ref[...] = jnp.zeros_like(acc_ref)

    hu = _dot(u_ref[...], wgu_ref[...])
    hg = hu[:, 0:D_EXPERT]
    hv = hu[:, D_EXPERT:2 * D_EXPERT]
    gate = gate_ref[...]
    lane = lax.broadcasted_iota(jnp.int32, gate.shape, 1)
    gcol = jnp.sum(jnp.where(lane == N_GROUPS + e, gate, 0.0), axis=-1, keepdims=True)
    act = hg * jax.nn.sigmoid(hg) * hv * gcol
    acc_ref[...] += _dot(act.astype(BF16), wd_ref[...])

    @pl.when(e == N_EXPERTS - 1)
    def _():
        y_ref[...] = x1_ref[...] + g2_ref[...] * acc_ref[...]


def _moe_call(u2, gate, wgu, wd, x1, mod, mspec, tm):
    n = u2.shape[0]
    row = lambda i, e: (i, 0)
    return pl.pallas_call(
        _moe_kernel,
        out_shape=jax.ShapeDtypeStruct((n, D_MODEL), F32),
        grid=(n // tm, N_EXPERTS),
        in_specs=[pl.BlockSpec((tm, D_MODEL), row), pl.BlockSpec((tm, LANES), row),
                  pl.BlockSpec((None, D_MODEL, 2 * D_EXPERT), lambda i, e: (e, 0, 0)),
                  pl.BlockSpec((None, D_EXPERT, D_MODEL), lambda i, e: (e, 0, 0)),
                  pl.BlockSpec((tm, D_MODEL), row), mspec(5)],
        out_specs=pl.BlockSpec((tm, D_MODEL), row),
        scratch_shapes=[pltpu.VMEM((tm, D_MODEL), F32)],
        compiler_params=_params(2),
        name="moe_dense",
    )(u2, gate, wgu, wd, x1, mod)


def _rope128(x, cos, sa, sb):
    return x * cos + pltpu.roll(x, LANES - ROT_DIM // 2, axis=1) * sa \
        + pltpu.roll(x, ROT_DIM // 2, axis=1) * sb


def _kvq_kernel(x_ref, kvsh_ref, kvsc_ref, sh1_ref, sc1_ref, kvn_ref, nm_ref, wkv_ref, wq_ref,
                kn_ref, qn_ref, cos_ref, sa_ref, sb_ref, g64_ref, gr_ref, gb_ref,
                q_ref, k_ref, v_ref):
    xn = _unit_rms(x_ref[...])
    cos, sa, sb = cos_ref[...], sa_ref[...], sb_ref[...]

    ukv = xn * kvn_ref[...] * (1.0 + kvsc_ref[...]) + kvsh_ref[...]
    kv = _dot(ukv.astype(BF16), wkv_ref[...])
    k = kv[:, 0:LANES]
    v_ref[...] = kv[:, LANES:2 * LANES]
    kh, kl = _split2(k * k)
    ms = _dot(kh, g64_ref[...]) + _dot(kl, g64_ref[...])
    k_ref[...] = _rope128(k * lax.rsqrt(ms + RMS_EPS) * kn_ref[...], cos, sa, sb)

    u1 = xn * nm_ref[...] * (1.0 + sc1_ref[...]) + sh1_ref[...]
    q = _dot(u1.astype(BF16), wq_ref[...])
    qh, ql = _split2(q * q)
    ms16 = _dot(qh, gr_ref[...]) + _dot(ql, gr_ref[...])
    r3 = _split3(lax.rsqrt(ms16 + RMS_EPS))
    rsb = _dot(r3[0], gb_ref[...]) + _dot(r3[1], gb_ref[...]) + _dot(r3[2], gb_ref[...])
    qn = q * rsb * qn_ref[...]
    for i in range(D_MODEL // LANES):
        sl = slice(LANES * i, LANES * (i + 1))
        q_ref[:, sl] = (_rope128(qn[:, sl], cos, sa, sb) * (HEAD_DIM ** -0.5)).astype(BF16)


def _kvq_call(x, mod1, mspec, kvmod, kvspec, kvn, nm, wkv, wq, kn, qn, tabs, tab, g64, gr, gb, tm):
    n = x.shape[0]
    row = lambda i: (i, 0)
    const = lambda i: (0, 0)
    vec = pl.BlockSpec((1, D_MODEL), const)
    return pl.pallas_call(
        _kvq_kernel,
        out_shape=(jax.ShapeDtypeStruct((n, D_MODEL), BF16),
                   jax.ShapeDtypeStruct((n, LANES), F32),
                   jax.ShapeDtypeStruct((n, LANES), F32)),
        grid=(n // tm,),
        in_specs=[pl.BlockSpec((tm, D_MODEL), row), kvspec(0), kvspec(1), mspec(0), mspec(1),
                  vec, vec,
                  pl.BlockSpec((D_MODEL, 2 * LANES), const), pl.BlockSpec((D_MODEL, D_MODEL), const),
                  pl.BlockSpec((1, LANES), const), vec, tab, tab, tab,
                  pl.BlockSpec((LANES, LANES), const), pl.BlockSpec((D_MODEL, LANES), const),
                  pl.BlockSpec((LANES, D_MODEL), const)],
        out_specs=(pl.BlockSpec((tm, D_MODEL), row), pl.BlockSpec((tm, LANES), row),
                   pl.BlockSpec((tm, LANES), row)),
        compiler_params=_params(1),
        name="kv_q_proj",
    )(x, kvmod, kvmod, mod1, mod1, kvn, nm, wkv, wq, kn, qn, *tabs, g64, gr, gb)


def _attn_core(q, kcat, vcat, sinks_ref, first_block, o_ref):
    tq = q.shape[0]
    tk = kcat.shape[0]
    pairs = N_Q_HEADS // N_KV_HEADS // 2
    rows = pairs * tq
    lane = lax.broadcasted_iota(jnp.int32, (1, LANES), 1)
    lo = lane < HEAD_DIM
    kro = pltpu.roll(kcat, HEAD_DIM, axis=1)
    vro = pltpu.roll(vcat, HEAD_DIM, axis=1)
    kj = lax.broadcasted_iota(jnp.int32, (rows, tk), 1)
    qi = lax.broadcasted_iota(jnp.int32, (rows, tk), 0) & (tq - 1)
    valid = (kj > qi) & (kj <= qi + WINDOW)
    if first_block is not None:
        valid = valid & ((kj >= WINDOW) | jnp.logical_not(first_block))
    for g in range(N_KV_HEADS):
        if g == 0:
            ke, ko = jnp.where(lo, kcat, 0.0), jnp.where(lo, 0.0, kro)
            ve, vo = jnp.where(lo, vcat, 0.0), jnp.where(lo, 0.0, vro)
        else:
            ke, ko = jnp.where(lo, kro, 0.0), jnp.where(lo, 0.0, kcat)
            ve, vo = jnp.where(lo, vro, 0.0), jnp.where(lo, 0.0, vcat)
        k2 = jnp.concatenate([ke, ko], 0).astype(BF16)
        v2 = jnp.concatenate([ve, vo], 0).astype(BF16)
        base = g * pairs
        qs = jnp.concatenate([q[:, LANES * (base + p):LANES * (base + p + 1)]
                              for p in range(pairs)], 0)
        s = _dot_nt(qs, k2)
        halves = []
        inv = []
        for par in range(2):
            sp = jnp.where(valid, s[:, par * tk:(par + 1) * tk], -jnp.inf)
            sink = jnp.concatenate(
                [jnp.full((tq, 1), sinks_ref[2 * (base + p) + par], F32) for p in range(pairs)], 0)
            m = jnp.maximum(jnp.max(sp, axis=-1, keepdims=True), sink)
            pexp = jnp.exp(sp - m)
            denom = jnp.sum(pexp, axis=-1, keepdims=True) + jnp.exp(sink - m)
            halves.append(pexp.astype(BF16))
            inv.append(1.0 / denom)
        o = _dot(jnp.concatenate(halves, 1), v2)
        o = o * jnp.where(lo, inv[0], inv[1])
        for p in range(pairs):
            o_ref[:, LANES * (base + p):LANES * (base + p + 1)] = \
                o[tq * p:tq * (p + 1), :].astype(o_ref.dtype)


def _attn_prompt_kernel(sinks_ref, q_ref, kp_ref, kc_ref, vp_ref, vc_ref, o_ref):
    kcat = jnp.concatenate([kp_ref[...], kc_ref[...]], 0)
    vcat = jnp.concatenate([vp_ref[...], vc_ref[...]], 0)
    _attn_core(q_ref[...], kcat, vcat, sinks_ref, pl.program_id(1) == 0, o_ref)


def _attn_sample_kernel(sinks_ref, q_ref, k_ref, v_ref, o_ref):
    _attn_core(q_ref[...], k_ref[...], v_ref[...], sinks_ref, None, o_ref)


def _attn_prompt_call(sinks, q, k, v, B, nb):
    n = q.shape[0]
    cur = lambda b, i: (b * nb + i, 0)
    prev = lambda b, i: (b * nb + jnp.maximum(i - 1, 0), 0)
    kvb = lambda im: pl.BlockSpec((WINDOW, LANES), im)
    return pl.pallas_call(
        _attn_prompt_kernel,
        out_shape=jax.ShapeDtypeStruct((n, D_MODEL), BF16),
        grid=(B, nb),
        in_specs=[pl.BlockSpec(memory_space=pltpu.SMEM),
                  pl.BlockSpec((WINDOW, D_MODEL), cur), kvb(prev), kvb(cur), kvb(prev), kvb(cur)],
        out_specs=pl.BlockSpec((WINDOW, D_MODEL), cur),
        compiler_params=_params(2),
        name="swa_prompt",
    )(sinks, q, k, k, v, v)


def _attn_sample_call(sinks, q, kcat, vcat):
    B = q.shape[0]
    b3 = lambda b: (b, 0, 0)
    return pl.pallas_call(
        _attn_sample_kernel,
        out_shape=jax.ShapeDtypeStruct(q.shape, BF16),
        grid=(B,),
        in_specs=[pl.BlockSpec(memory_space=pltpu.SMEM),
                  pl.BlockSpec((None, SAMPLE_PAD, D_MODEL), b3),
                  pl.BlockSpec((None, 2 * WINDOW, LANES), b3),
                  pl.BlockSpec((None, 2 * WINDOW, LANES), b3)],
        out_specs=pl.BlockSpec((None, SAMPLE_PAD, D_MODEL), b3),
        compiler_params=_params(1),
        name="swa_sample",
    )(sinks, q, kcat, vcat)


def _rope_tables(pos):
    half = ROT_DIM // 2
    inv = ROPE_THETA ** (-jnp.arange(half, dtype=F32) / half)
    ang = pos.astype(F32)[:, None] * inv[None]
    cos, sin = jnp.cos(ang), jnp.sin(ang)
    d = jnp.arange(LANES) % HEAD_DIM
    idx = d % half
    cos_t = jnp.where(d < ROT_DIM, cos[:, idx], 1.0)
    sa = jnp.where(d < half, -sin[:, idx], 0.0)
    sb = jnp.where((d >= half) & (d < ROT_DIM), sin[:, idx], 0.0)
    return cos_t, sa, sb


def _pad_lanes(a, value=0.0):
    return jnp.pad(a, ((0, 0), (0, LANES - a.shape[1])), constant_values=value)


def _prep_weights(ada_w, ada_b, norm_mix, norm_ffn, a_w_in, a_b_gates, a_head_norm, a_w_out,
                  kv_ada_w, kv_ada_b, kv_norm, w_k, w_v, k_norm, b_w_q, b_q_norm, b_sinks, b_w_o,
                  moe_w_group, moe_b_group, moe_w_expert, moe_b_expert, moe_w_gate_up, moe_w_down):
    w = {}
    g0 = QK_COLS + 2 * V_COLS
    win = a_w_in[0]
    w["w_in"] = jnp.concatenate([win[:, :g0], _pad_lanes(win[:, g0:g0 + M_HEADS]),
                                 _pad_lanes(win[:, g0 + M_HEADS:])], 1).astype(BF16)
    w["bli"] = _pad_lanes(a_b_gates[0][None, :M_HEADS])
    w["blf"] = _pad_lanes(a_b_gates[0][None, M_HEADS:])
    w["head_norm"] = a_head_norm[0][None]
    w["w_out"] = a_w_out[0].astype(BF16)
    w["norm_mix"] = [norm_mix[l][None] for l in range(2)]
    w["norm_ffn"] = [norm_ffn[l][None] for l in range(2)]
    w["router"] = []
    for l in range(2):
        wr = _pad_lanes(jnp.concatenate([moe_w_group[l], moe_w_expert[l]], 1))
        hi = wr.astype(BF16)
        lo = (wr - hi.astype(F32)).astype(BF16)
        br = _pad_lanes(jnp.concatenate([moe_b_group[l], moe_b_expert[l]])[None])
        w["router"].append((hi, lo, br))
    w["w_gu"] = [moe_w_gate_up[l].astype(BF16) for l in range(2)]
    w["w_d"] = [moe_w_down[l].astype(BF16) for l in range(2)]
    w["kv_norm"] = kv_norm[None]
    w["w_kv"] = jnp.concatenate([w_k, w_v], 1).astype(BF16)
    w["k_norm"] = jnp.tile(k_norm, N_KV_HEADS)[None]
    w["w_q"] = b_w_q[0].astype(BF16)
    w["q_norm"] = jnp.tile(b_q_norm[0], N_Q_HEADS)[None]
    w["sinks"] = b_sinks[0]
    w["w_o"] = b_w_o[0].astype(BF16)
    lanes = jnp.arange(LANES)
    feat = jnp.arange(D_MODEL)
    w["g64"] = jnp.where((lanes[:, None] // HEAD_DIM) == (lanes[None, :] // HEAD_DIM),
                         1.0 / HEAD_DIM, 0.0).astype(BF16)
    w["gr"] = jnp.where((feat[:, None] // HEAD_DIM) == lanes[None, :], 1.0 / HEAD_DIM, 0.0).astype(BF16)
    w["gb"] = jnp.where(lanes[:, None] == (feat[None, :] // HEAD_DIM), 1.0, 0.0).astype(BF16)
    return w


def _trunk(x2, mods, kvmod, mspec, mspec_moe, w, *, B, T, L, tm, tm_moe, tabs, tabspec, c0, n0, m0,
           cache_k=None, cache_v=None):
    sample = cache_k is not None
    nc = T // L if not sample else 1

    qk, v, o, li, lf = _inproj_call(x2, mods[0], mspec, w["norm_mix"][0], w["w_in"],
                                    w["bli"], w["blf"], tm)
    if sample:
        def padtok(a, value=0.0):
            a = a.reshape(B, T, a.shape[-1])
            a = jnp.pad(a, ((0, 0), (0, L - T), (0, 0)), constant_values=value)
            return a.reshape(B * L, a.shape[-1])
        qk, v, li, lf = padtok(qk), padtok(v), padtok(li, M_EMPTY), padtok(lf)
    h, c_new, n_new, m_new = _mlstm_call(qk, v, li, lf, c0, n0, m0, B, nc, L)
    if sample:
        h = h.reshape(B, L, V_COLS)[:, :T].reshape(B * T, V_COLS)
    rh, rl, br = w["router"][0]
    x1, u2, gate = _post_call(True, h, o, x2, mods[0], mspec, w["norm_ffn"][0], w["head_norm"],
                              w["w_out"], rh, rl, br, tm)
    xa = _moe_call(u2, gate, w["w_gu"][0], w["w_d"][0], x1, mods[0], mspec_moe, tm_moe)

    q, k, vv = _kvq_call(xa, mods[1], mspec, kvmod, mspec, w["kv_norm"], w["norm_mix"][1],
                         w["w_kv"], w["w_q"], w["k_norm"], w["q_norm"], tabs, tabspec,
                         w["g64"], w["gr"], w["gb"], tm)
    if not sample:
        att = _attn_prompt_call(w["sinks"], q, k, vv, B, T // WINDOW)
        k_win = k.reshape(B, T, N_KV_HEADS, HEAD_DIM)[:, T - WINDOW:]
        v_win = vv.reshape(B, T, N_KV_HEADS, HEAD_DIM)[:, T - WINDOW:]
    else:
        zpad = jnp.zeros((B, WINDOW - T, LANES), F32)
        kcat = jnp.concatenate([cache_k.reshape(B, WINDOW, LANES), k.reshape(B, T, LANES), zpad], 1)
        vcat = jnp.concatenate([cache_v.reshape(B, WINDOW, LANES), vv.reshape(B, T, LANES), zpad], 1)
        qp = jnp.pad(q.reshape(B, T, D_MODEL), ((0, 0), (0, SAMPLE_PAD - T), (0, 0)))
        att = _attn_sample_call(w["sinks"], qp, kcat, vcat)[:, :T].reshape(B * T, D_MODEL)
        k_win = kcat[:, T:T + WINDOW].reshape(B, WINDOW, N_KV_HEADS, HEAD_DIM)
        v_win = vcat[:, T:T + WINDOW].reshape(B, WINDOW, N_KV_HEADS, HEAD_DIM)
    rh, rl, br = w["router"][1]
    x3, u4, gate = _post_call(False, att, None, xa, mods[1], mspec, w["norm_ffn"][1], None,
                              w["w_o"], rh, rl, br, tm)
    y = _moe_call(u4, gate, w["w_gu"][1], w["w_d"][1], x3, mods[1], mspec_moe, tm_moe)
    c_out = c_new[None]
    n_out = n_new.reshape(1, B, M_HEADS, M_DK)
    m_out = m_new[:, 0, :M_HEADS][None]
    return y, c_out, n_out, m_out, k_win, v_win


def kernel(x_prompt, x_sample, c_prompt, c_sample, state_c, state_n, state_m, cache_k_win, cache_v_win, ada_w, ada_b, norm_mix, norm_ffn, a_w_in, a_b_gates, a_head_norm, a_w_out, kv_ada_w, kv_ada_b, kv_norm, w_k, w_v, k_norm, b_w_q, b_q_norm, b_sinks, b_w_o, moe_w_group, moe_b_group, moe_w_expert, moe_b_expert, moe_w_gate_up, moe_w_down):
    Bp, Tp, D = x_prompt.shape
    Bs, Ts, _ = x_sample.shape
    w = _prep_weights(ada_w, ada_b, norm_mix, norm_ffn, a_w_in, a_b_gates, a_head_norm, a_w_out,
                      kv_ada_w, kv_ada_b, kv_norm, w_k, w_v, k_norm, b_w_q, b_q_norm, b_sinks, b_w_o,
                      moe_w_group, moe_b_group, moe_w_expert, moe_b_expert, moe_w_gate_up, moe_w_down)

    rows = Bp + Bs
    rpad = -rows % 8
    c_all = jnp.concatenate([c_prompt, c_sample, jnp.zeros((rpad, D), F32)], 0)
    mod = _ada_call(c_all, ada_w, ada_b[:, None, :])
    kvm = _ada_call(c_all, kv_ada_w[None], kv_ada_b[None, None, :])

    tm_p = 512
    tiles_per_seq = Tp // tm_p
    mods_p = [mod[l, :Bp][:, None, :] for l in range(2)]
    kvmod_p = kvm[0, :Bp][:, None, :]

    def mspec_p(col):
        return pl.BlockSpec((None, 1, D_MODEL), lambda i, *_: (i // tiles_per_seq, 0, col))

    tm_moe_p = 1024
    moe_tiles_per_seq = Tp // tm_moe_p

    def mspec_moe_p(col):
        return pl.BlockSpec((None, 1, D_MODEL), lambda i, *_: (i // moe_tiles_per_seq, 0, col))

    tabs_p = _rope_tables(jnp.arange(Tp, dtype=jnp.int32))
    tabspec_p = pl.BlockSpec((tm_p, LANES), lambda i: (i % tiles_per_seq, 0))
    npairs = M_HEADS // 2
    c0 = jnp.zeros((Bp, M_HEADS, M_DK, M_DV), F32)
    n0 = jnp.zeros((Bp, npairs, LANES), F32)
    m0 = jnp.pad(jnp.full((Bp, 1, M_HEADS), M_EMPTY, F32), ((0, 0), (0, 0), (0, LANES - M_HEADS)))
    yp, cp, np_, mp, kwp, vwp = _trunk(
        x_prompt.reshape(Bp * Tp, D), mods_p, kvmod_p, mspec_p, mspec_moe_p, w,
        B=Bp, T=Tp, L=M_CHUNK, tm=tm_p, tm_moe=tm_moe_p, tabs=tabs_p, tabspec=tabspec_p,
        c0=c0, n0=n0, m0=m0)

    ns = Bs * Ts
    mods_s = [jnp.repeat(mod[l, Bp:Bp + Bs], Ts, axis=0)[None] for l in range(2)]
    kvmod_s = jnp.repeat(kvm[0, Bp:Bp + Bs], Ts, axis=0)[None]

    def mspec_s(col):
        return pl.BlockSpec((None, ns, D_MODEL), lambda i, *_: (0, 0, col))

    tabs_s = _rope_tables(PAST_LEN + jnp.arange(Ts, dtype=jnp.int32))
    tabs_s = tuple(jnp.tile(t, (Bs, 1)) for t in tabs_s)
    m0s = jnp.pad(state_m[0][:, None, :], ((0, 0), (0, 0), (0, LANES - M_HEADS)))
    ys, cs, ns_, ms, kws, vws = _trunk(
        x_sample.reshape(ns, D), mods_s, kvmod_s, mspec_s, mspec_s, w,
        B=Bs, T=Ts, L=SAMPLE_PAD, tm=ns, tm_moe=ns, tabs=tabs_s,
        tabspec=pl.BlockSpec((ns, LANES), lambda i: (0, 0)),
        c0=state_c[0], n0=state_n[0].reshape(Bs, npairs, LANES), m0=m0s,
        cache_k=cache_k_win, cache_v=cache_v_win)

    return (yp.reshape(Bp, Tp, D), ys.reshape(Bs, Ts, D), cp, np_, mp, kwp, vwp,
            cs, ns_, ms, kws, vws)
```

```python
import functools

import jax
import jax.numpy as jnp
from jax import lax
from jax.experimental import pallas as pl
from jax.experimental.pallas import tpu as pltpu

F32 = jnp.float32
BF16 = jnp.bfloat16

D_MODEL = 1024
PAST_LEN = 8192
M_HEADS = 8
M_DK = 64
M_DV = 128
M_CHUNK = 64
GATE_SOFTCAP = 15.0
M_EMPTY = -1e30
WINDOW = 128
HEAD_DIM = 64
N_Q_HEADS = 16
N_KV_HEADS = 2
ROPE_THETA = 500000.0
ROT_DIM = 16
N_GROUPS = 4
EXPERTS_PER_GROUP = 4
N_EXPERTS = 16
D_EXPERT = 256
RMS_EPS = 1e-6

LANES = 128
QK_COLS = 2 * M_HEADS * M_DK
V_COLS = M_HEADS * M_DV
IN_COLS = QK_COLS + 2 * V_COLS + 2 * LANES
SAMPLE_PAD = 16
VMEM_LIMIT = 52 * 1024 * 1024

NT_DIMS = (((1,), (1,)), ((), ()))


def _params(n_axes):
    return pltpu.CompilerParams(dimension_semantics=("arbitrary",) * n_axes,
                                vmem_limit_bytes=VMEM_LIMIT)


def _dot(a, b):
    return jnp.dot(a, b, preferred_element_type=F32)


def _dot_nt(a, b):
    return lax.dot_general(a, b, NT_DIMS, preferred_element_type=F32)


def _split2(x):
    hi = x.astype(BF16)
    lo = (x - hi.astype(F32)).astype(BF16)
    return hi, lo


def _split3(x):
    hi = x.astype(BF16)
    r = x - hi.astype(F32)
    mid = r.astype(BF16)
    lo = (r - mid.astype(F32)).astype(BF16)
    return hi, mid, lo


def _unit_rms(x):
    return x * lax.rsqrt(jnp.mean(x * x, axis=-1, keepdims=True) + RMS_EPS)


def _ada_kernel(c_ref, w_ref, b_ref, o_ref):
    c = c_ref[...]
    cs = (c * jax.nn.sigmoid(c)).astype(BF16)
    o_ref[...] = _dot(cs, w_ref[...].astype(BF16)) + b_ref[...]


def _ada_call(c, w, b):
    g, d, n = w.shape
    r = c.shape[0]
    tn = 1024
    return pl.pallas_call(
        _ada_kernel,
        out_shape=jax.ShapeDtypeStruct((g, r, n), F32),
        grid=(g, n // tn),
        in_specs=[pl.BlockSpec((r, d), lambda i, j: (0, 0)),
                  pl.BlockSpec((None, d, tn), lambda i, j: (i, 0, j)),
                  pl.BlockSpec((None, 1, tn), lambda i, j: (i, 0, j))],
        out_specs=pl.BlockSpec((None, r, tn), lambda i, j: (i, 0, j)),
        compiler_params=_params(2),
        name="ada_mod",
    )(c, w, b)


def _inproj_kernel(x_ref, sh_ref, sc_ref, nw_ref, w_ref, bli_ref, blf_ref,
                   qk_ref, v_ref, o_ref, li_ref, lf_ref):
    u = _unit_rms(x_ref[...]) * nw_ref[...] * (1.0 + sc_ref[...]) + sh_ref[...]
    ub = u.astype(BF16)
    half = QK_COLS // 2
    q = _dot(ub, w_ref[:, 0:half]) * (M_DK ** -0.5)
    qk_ref[:, 0:half] = q.astype(BF16)
    qk_ref[:, half:QK_COLS] = _dot(ub, w_ref[:, half:QK_COLS]).astype(BF16)
    v_ref[...] = _dot(ub, w_ref[:, QK_COLS:QK_COLS + V_COLS]).astype(BF16)
    o_ref[...] = _dot(ub, w_ref[:, QK_COLS + V_COLS:QK_COLS + 2 * V_COLS]).astype(BF16)
    g0 = QK_COLS + 2 * V_COLS
    lane = lax.broadcasted_iota(jnp.int32, (1, LANES), 1)
    live = lane < M_HEADS
    gi = _dot(ub, w_ref[:, g0:g0 + LANES]) + bli_ref[...]
    gf = _dot(ub, w_ref[:, g0 + LANES:g0 + 2 * LANES]) + blf_ref[...]
    li = GATE_SOFTCAP * jnp.tanh(gi / GATE_SOFTCAP)
    fpre = GATE_SOFTCAP * jnp.tanh(gf / GATE_SOFTCAP)
    lf = jnp.minimum(fpre, 0.0) - jnp.log1p(jnp.exp(-jnp.abs(fpre)))
    li_ref[...] = jnp.where(live, li, 0.0)
    lf_ref[...] = jnp.where(live, lf, 0.0)


def _inproj_call(x, mod, mspec, nw, w, bli, blf, tm):
    n = x.shape[0]
    row = lambda i: (i, 0)
    const = lambda i: (0, 0)
    return pl.pallas_call(
        _inproj_kernel,
        out_shape=(jax.ShapeDtypeStruct((n, QK_COLS), BF16),
                   jax.ShapeDtypeStruct((n, V_COLS), BF16),
                   jax.ShapeDtypeStruct((n, V_COLS), BF16),
                   jax.ShapeDtypeStruct((n, LANES), F32),
                   jax.ShapeDtypeStruct((n, LANES), F32)),
        grid=(n // tm,),
        in_specs=[pl.BlockSpec((tm, D_MODEL), row), mspec(0), mspec(1),
                  pl.BlockSpec((1, D_MODEL), const),
                  pl.BlockSpec((D_MODEL, IN_COLS), const),
                  pl.BlockSpec((1, LANES), const), pl.BlockSpec((1, LANES), const)],
        out_specs=(pl.BlockSpec((tm, QK_COLS), row), pl.BlockSpec((tm, V_COLS), row),
                   pl.BlockSpec((tm, V_COLS), row), pl.BlockSpec((tm, LANES), row),
                   pl.BlockSpec((tm, LANES), row)),
        compiler_params=_params(1),
        name="mlstm_inproj",
    )(x, mod, mod, nw, w, bli, blf)


def _mlstm_kernel(L, qk_ref, v_ref, li_ref, lf_ref, c0_ref, n0_ref, m0_ref,
                  h_ref, cout_ref, nout_ref, mout_ref, c2_s, n_s, m_s):
    c = pl.program_id(1)
    nc = pl.num_programs(1)
    npairs = M_HEADS // 2
    hd = M_DK

    lane = lax.broadcasted_iota(jnp.int32, (1, LANES), 1)
    lo128 = lane < hd
    lane256 = lax.broadcasted_iota(jnp.int32, (1, 2 * LANES), 1)
    lo256 = lane256 < LANES
    row128 = lax.broadcasted_iota(jnp.int32, (LANES, 1), 0)
    top = row128 < hd
    blockdiag = (top & lo256) | (jnp.logical_not(top) & jnp.logical_not(lo256))

    @pl.when(c == 0)
    def _():
        m_s[...] = m0_ref[...]
        n_s[...] = n0_ref[...]
        z = jnp.zeros((M_DK, M_DV), F32)
        for j in range(npairs):
            ce = c0_ref[2 * j]
            co = c0_ref[2 * j + 1]
            c2_s[j] = jnp.concatenate([jnp.concatenate([ce, z], 1),
                                       jnp.concatenate([z, co], 1)], 0)

    LI = li_ref[...]
    LF = lf_ref[...]
    rowL = lax.broadcasted_iota(jnp.int32, (L, LANES), 0)

    def prefix(x, op, ident):
        d = 1
        while d < L:
            shifted = pltpu.roll(x, d, axis=0)
            x = op(x, jnp.where(rowL >= d, shifted, ident))
            d *= 2
        return x

    Bc = prefix(LF, jnp.add, 0.0)
    Cm = LI - Bc
    mprev = m_s[...]
    Gc = jnp.maximum(mprev, prefix(Cm, jnp.maximum, -jnp.inf))
    A = jnp.exp(mprev - Gc)
    EXPM = jnp.exp(-(Bc + Gc))
    bL = Bc[L - 1:L, :]
    DL = bL + Cm
    mnew = jnp.maximum(bL + mprev, jnp.max(DL, axis=0, keepdims=True))
    ast = jnp.exp(bL + mprev - mnew)
    WST = jnp.exp(DL - mnew)

    def pad_rows(x, rows):
        if x.shape[0] == rows:
            return x
        return jnp.concatenate([x, jnp.zeros((rows - x.shape[0], x.shape[1]), x.dtype)], 0)

    cm_pad = pad_rows(Cm, hd)
    XT = jnp.concatenate([cm_pad, cm_pad], 0).T
    causal = (lane & (hd - 1)) <= rowL
    J = jnp.where(blockdiag, 1.0, 0.0).astype(BF16)

    def bc(X, h):
        return jnp.broadcast_to(X[:, h:h + 1], X.shape)

    for j in range(npairs):
        he, ho = 2 * j, 2 * j + 1

        def pair128(X):
            return jnp.where(lo128, bc(X, he), bc(X, ho))

        def pair256(X):
            return jnp.concatenate([bc(X, he), bc(X, ho)], 1)

        q128 = qk_ref[:, LANES * j:LANES * (j + 1)]
        k128 = qk_ref[:, QK_COLS // 2 + LANES * j:QK_COLS // 2 + LANES * (j + 1)]
        v256 = v_ref[:, 2 * LANES * j:2 * LANES * (j + 1)]
        zk = jnp.zeros_like(k128)
        zv = jnp.zeros_like(v256)
        K2t = jnp.concatenate([pad_rows(jnp.where(lo128, k128, zk), hd),
                               pad_rows(jnp.where(lo128, zk, k128), hd)], 0)
        V2 = jnp.concatenate([pad_rows(jnp.where(lo256, v256, zv), hd),
                              pad_rows(jnp.where(lo256, zv, v256), hd)], 0)
        S = _dot_nt(q128, K2t)
        crow = jnp.where(lo128, XT[he:he + 1, :], XT[ho:ho + 1, :])
        arg = jnp.where(causal, crow - pair128(Gc), -jnp.inf)
        Sw = (S * jnp.exp(arg)).astype(BF16)
        num_intra = _dot(Sw, V2)
        rowsum = _dot(Sw, J)
        C2 = c2_s[j]
        inter = _dot(q128, C2.astype(BF16))
        npair = n_s[j:j + 1, :]
        Nrow2 = jnp.concatenate(
            [jnp.broadcast_to(jnp.where(lo128, npair, 0.0), (LANES, LANES)),
             jnp.broadcast_to(jnp.where(lo128, 0.0, npair), (LANES, LANES))], 0).astype(BF16)
        qn = _dot_nt(q128, Nrow2)
        a256 = pair256(A)
        num = a256 * inter + num_intra
        den = a256 * qn + rowsum
        h = num / jnp.maximum(jnp.abs(den), pair256(EXPM))
        h_ref[:, 2 * LANES * j:2 * LANES * (j + 1)] = h.astype(h_ref.dtype)

        kw = k128.astype(F32) * pair128(WST)
        n_s[j:j + 1, :] = pair128(ast) * npair + jnp.sum(kw, axis=0, keepdims=True)
        kwT = pad_rows(kw, LANES).T.astype(BF16)
        dC = _dot(kwT, pad_rows(v256, LANES))
        c2_s[j] = pair256(ast) * C2 + jnp.where(blockdiag, dC, 0.0)

    m_s[...] = mnew

    @pl.when(c == nc - 1)
    def _():
        for j in range(npairs):
            C2 = c2_s[j]
            cout_ref[2 * j] = C2[0:M_DK, 0:M_DV]
            cout_ref[2 * j + 1] = C2[M_DK:2 * M_DK, M_DV:2 * M_DV]
        nout_ref[...] = n_s[...]
        mout_ref[...] = mnew


def _mlstm_call(qk, v, li, lf, c0, n0, m0, B, nc, L):
    n = qk.shape[0]
    npairs = M_HEADS // 2
    tok = lambda b, c: (b * nc + c, 0)
    st4 = lambda b, c: (b, 0, 0, 0)
    st3 = lambda b, c: (b, 0, 0)
    return pl.pallas_call(
        functools.partial(_mlstm_kernel, L),
        out_shape=(jax.ShapeDtypeStruct((n, V_COLS), BF16),
                   jax.ShapeDtypeStruct((B, M_HEADS, M_DK, M_DV), F32),
                   jax.ShapeDtypeStruct((B, npairs, LANES), F32),
                   jax.ShapeDtypeStruct((B, 1, LANES), F32)),
        grid=(B, nc),
        in_specs=[pl.BlockSpec((L, QK_COLS), tok), pl.BlockSpec((L, V_COLS), tok),
                  pl.BlockSpec((L, LANES), tok), pl.BlockSpec((L, LANES), tok),
                  pl.BlockSpec((None, M_HEADS, M_DK, M_DV), st4),
                  pl.BlockSpec((None, npairs, LANES), st3),
                  pl.BlockSpec((None, 1, LANES), st3)],
        out_specs=(pl.BlockSpec((L, V_COLS), tok),
                   pl.BlockSpec((None, M_HEADS, M_DK, M_DV), st4),
                   pl.BlockSpec((None, npairs, LANES), st3),
                   pl.BlockSpec((None, 1, LANES), st3)),
        scratch_shapes=[pltpu.VMEM((npairs, 2 * M_DK, 2 * M_DV), F32),
                        pltpu.VMEM((npairs, LANES), F32),
                        pltpu.VMEM((1, LANES), F32)],
        compiler_params=_params(2),
        name="mlstm_chunks",
    )(qk, v, li, lf, c0, n0, m0)


def _route(lg):
    lane = lax.broadcasted_iota(jnp.int32, lg.shape, 1)
    lanef = lane.astype(F32)
    neg = -jnp.inf
    far = float(LANES)
    gm = lane < N_GROUPS
    lgm = jnp.where(gm, lg, neg)
    gmax = jnp.max(lgm, axis=-1, keepdims=True)
    gsum = jnp.sum(jnp.exp(lgm - gmax), axis=-1, keepdims=True)
    g_w = 1.0 / gsum
    gidx = jnp.min(jnp.where(gm & (lg == gmax), lanef, far), axis=-1, keepdims=True)
    first = N_GROUPS + EXPERTS_PER_GROUP * gidx
    sel = (lanef >= first) & (lanef < first + EXPERTS_PER_GROUP)
    l1 = jnp.max(jnp.where(sel, lg, neg), axis=-1, keepdims=True)
    i1 = jnp.min(jnp.where(sel & (lg == l1), lanef, far), axis=-1, keepdims=True)
    sel2 = sel & (lanef != i1)
    l2 = jnp.max(jnp.where(sel2, lg, neg), axis=-1, keepdims=True)
    i2 = jnp.min(jnp.where(sel2 & (lg == l2), lanef, far), axis=-1, keepdims=True)
    r = jnp.exp(l2 - l1)
    w1 = g_w / (1.0 + r)
    w2 = w1 * r
    return jnp.where(lanef == i1, w1, jnp.where(lanef == i2, w2, 0.0)), gidx


def _post_kernel(mlstm, routed, *refs):
    refs = list(refs)
    h_ref = refs.pop(0)
    o_ref = refs.pop(0) if mlstm else None
    x_ref, g1_ref, sh2_ref, sc2_ref, nf_ref = refs[:5]
    refs = refs[5:]
    hn_ref = refs.pop(0) if mlstm else None
    wout_ref, wrh_ref, wrl_ref, br_ref, x1_ref = refs[:5]
    outs = refs[5:]
    if mlstm:
        hf = h_ref[...].astype(F32)
        parts = [_unit_rms(hf[:, M_DV * i:M_DV * (i + 1)]) for i in range(M_HEADS)]
        hn = jnp.concatenate(parts, 1) * hn_ref[...]
        hg = (hn * jax.nn.sigmoid(o_ref[...].astype(F32))).astype(BF16)
    else:
        hg = h_ref[...]
    x1 = x_ref[...] + g1_ref[...] * _dot(hg, wout_ref[...])
    x1_ref[...] = x1
    u2 = _unit_rms(x1) * nf_ref[...] * (1.0 + sc2_ref[...]) + sh2_ref[...]
    uh, ul = _split2(u2)
    lg = _dot(uh, wrh_ref[...]) + _dot(ul, wrh_ref[...]) + _dot(uh, wrl_ref[...]) + br_ref[...]
    gate, gidx = _route(lg)
    if not routed:
        u2_ref, gate_ref = outs
        u2_ref[...] = uh
        gate_ref[...] = gate
        return
    pack_ref, gid_ref = outs
    pack_ref[:, 0:D_MODEL] = u2
    pack_ref[:, D_MODEL:PACK_COLS] = gate
    rows = []
    for blk in range(x1.shape[0] // LANES):
        col = jnp.broadcast_to(gidx[LANES * blk:LANES * (blk + 1), :], (LANES, LANES))
        rows.append(col.T[0:1, :])
    gid_ref[...] = jnp.concatenate(rows, 1)


PACK_COLS = D_MODEL + LANES


def _post_call(mlstm, routed, h, o, x, mod, mspec, nf, hn, wout, wrh, wrl, br, tm):
    n = x.shape[0]
    row = lambda i: (i, 0)
    const = lambda i: (0, 0)
    tok = pl.BlockSpec((tm, D_MODEL), row)
    vec = pl.BlockSpec((1, D_MODEL), const)
    ins = [h] + ([o] if mlstm else []) + [x, mod, mod, mod, nf] + ([hn] if mlstm else []) \
        + [wout, wrh, wrl, br]
    specs = [tok] + ([tok] if mlstm else []) + [tok, mspec(2), mspec(3), mspec(4), vec] \
        + ([vec] if mlstm else []) \
        + [pl.BlockSpec((D_MODEL, D_MODEL), const), pl.BlockSpec((D_MODEL, LANES), const),
           pl.BlockSpec((D_MODEL, LANES), const), pl.BlockSpec((1, LANES), const)]
    if routed:
        out_shape = (jax.ShapeDtypeStruct((n, D_MODEL), F32),
                     jax.ShapeDtypeStruct((n, PACK_COLS), F32),
                     jax.ShapeDtypeStruct((n // tm, 1, tm), F32))
        out_specs = (tok, pl.BlockSpec((tm, PACK_COLS), row),
                     pl.BlockSpec((None, 1, tm), lambda i: (i, 0, 0)))
    else:
        out_shape = (jax.ShapeDtypeStruct((n, D_MODEL), F32),
                     jax.ShapeDtypeStruct((n, D_MODEL), BF16),
                     jax.ShapeDtypeStruct((n, LANES), F32))
        out_specs = (tok, tok, pl.BlockSpec((tm, LANES), row))
    return pl.pallas_call(
        functools.partial(_post_kernel, mlstm, routed),
        out_shape=out_shape,
        grid=(n // tm,),
        in_specs=specs,
        out_specs=out_specs,
        compiler_params=_params(1),
        name="post_mlstm" if mlstm else "post_attn",
    )(*ins)


ITEM_ROWS = 8


def _sort_meta_kernel(tm, gid_ref, pos_ref, items_ref):
    ntiles = gid_ref.shape[0]
    gid = gid_ref[...]
    r = lax.broadcasted_iota(jnp.int32, (tm, tm), 0)
    c = lax.broadcasted_iota(jnp.int32, (tm, tm), 1)
    before = jnp.where(r < c, 1.0, 0.0).astype(BF16)
    trow = lax.broadcasted_iota(jnp.int32, (ntiles, LANES), 0)
    k = lax.broadcasted_iota(jnp.int32, (1, LANES), 1).astype(F32)
    zero11 = jnp.zeros((1, 1), F32)
    pos = jnp.zeros((ntiles, tm), F32)
    start, nitems = zero11, zero11
    grp = jnp.zeros((1, LANES), F32)
    tile = jnp.zeros((1, LANES), F32)
    valid = jnp.zeros((1, LANES), F32)
    for g in range(N_GROUPS):
        mask = jnp.where(gid == float(g), 1.0, 0.0)
        incl = jnp.broadcast_to(jnp.sum(mask, axis=1, keepdims=True), (ntiles, LANES))
        cnt = incl
        d = 1
        while d < ntiles:
            incl = incl + jnp.where(trow >= d, pltpu.roll(incl, d, axis=0), 0.0)
            d *= 2
        tilebase = (incl - cnt)[:, 0:1]
        total = incl[ntiles - 1:ntiles, 0:1]
        local = _dot(mask.astype(BF16), before)
        pos = pos + mask * (start + tilebase + local)
        end = start + total
        ft = jnp.floor(start * (1.0 / tm))
        lt = jnp.floor((end - 1.0) * (1.0 / tm))
        ni = jnp.where(total > 0.0, lt - ft + 1.0, 0.0)
        inside = (k >= nitems) & (k < nitems + ni)
        grp = grp + jnp.where(inside, float(g), 0.0)
        tile = tile + jnp.where(inside, ft + (k - nitems), 0.0)
        valid = valid + jnp.where(inside, 1.0, 0.0)
        nitems = nitems + ni
        start = end
    live = valid > 0.0
    tile = jnp.where(live, tile, float(ntiles - 1))
    grp = jnp.where(live, grp, jnp.max(grp, axis=1, keepdims=True))
    prev = pltpu.roll(tile, 1, axis=1)
    nxt = pltpu.roll(tile, LANES - 1, axis=1)
    first = jnp.where(live & ((k == 0.0) | (tile != prev)), 1.0, 0.0)
    last = jnp.where(live & ((k == nitems - 1.0) | (tile != nxt)), 1.0, 0.0)
    pos_ref[...] = pos.astype(jnp.int32)
    table = jnp.concatenate([tile, grp, valid, first, last,
                             jnp.zeros((ITEM_ROWS - 5, LANES), F32)], 0)
    items_ref[...] = table.astype(jnp.int32)


def _sort_meta_call(gid, tm):
    ntiles = gid.shape[0]
    assert ntiles + N_GROUPS - 1 <= LANES
    return pl.pallas_call(
        functools.partial(_sort_meta_kernel, tm),
        out_shape=(jax.ShapeDtypeStruct((ntiles, tm), jnp.int32),
                   jax.ShapeDtypeStruct((ITEM_ROWS, LANES), jnp.int32)),
        compiler_params=_params(0),
        name="moe_sort_meta",
    )(gid)


def _scatter_kernel(tm, pos_ref, pack_ref, out_hbm, sem):
    def row_copy(r, p):
        return pltpu.make_async_copy(pack_ref.at[pl.ds(r, 1), :], out_hbm.at[pl.ds(p, 1), :], sem)

    def issue(r, carry):
        row_copy(r, pos_ref[0, r]).start()
        return carry

    lax.fori_loop(0, tm, issue, 0, unroll=8)
    pltpu.make_async_copy(pack_ref, out_hbm.at[pl.ds(0, tm), :], sem).wait()


def _scatter_call(pack, pos3, tm):
    n = pack.shape[0]
    return pl.pallas_call(
        functools.partial(_scatter_kernel, tm),
        out_shape=jax.ShapeDtypeStruct(pack.shape, pack.dtype),
        grid=(n // tm,),
        in_specs=[pl.BlockSpec((None, 1, tm), lambda i: (i, 0, 0), memory_space=pltpu.SMEM),
                  pl.BlockSpec((tm, PACK_COLS), lambda i: (i, 0))],
        out_specs=pl.BlockSpec(memory_space=pl.ANY),
        scratch_shapes=[pltpu.SemaphoreType.DMA(())],
        compiler_params=_params(1),
        name="moe_scatter",
    )(pos3, pack)


def _gather_rows(tm, pos_ref, src_hbm, buf, sem):
    def issue(r, carry):
        pltpu.make_async_copy(src_hbm.at[pl.ds(pos_ref[0, r], 1), :],
                              buf.at[pl.ds(r, 1), :], sem).start()
        return carry

    lax.fori_loop(0, tm, issue, 0, unroll=8)
    pltpu.make_async_copy(src_hbm.at[pl.ds(0, tm), :], buf, sem).wait()


def _moe_routed_kernel(items_ref, pack_ref, wgu_ref, wd_ref, out_ref):
    kk = pl.program_id(0)
    grp = items_ref[1, kk]

    @pl.when(items_ref[3, kk] == 1)
    def _():
        out_ref[...] = jnp.zeros_like(out_ref)

    @pl.when(items_ref[2, kk] == 1)
    def _():
        u = pack_ref[:, 0:D_MODEL].astype(BF16)
        gate = pack_ref[:, D_MODEL:PACK_COLS]
        lane = lax.broadcasted_iota(jnp.int32, gate.shape, 1)
        first_lane = N_GROUPS + EXPERTS_PER_GROUP * grp
        for e in range(EXPERTS_PER_GROUP):
            gcol = jnp.sum(jnp.where(lane == first_lane + e, gate, 0.0), axis=-1, keepdims=True)
            hu = _dot(u, wgu_ref[e])
            hg = hu[:, 0:D_EXPERT]
            act = hg * jax.nn.sigmoid(hg) * hu[:, D_EXPERT:2 * D_EXPERT] * gcol
            out_ref[...] += _dot(act.astype(BF16), wd_ref[e])


def _moe_routed_call(items, sorted_pack, wgu, wd, tm):
    n = sorted_pack.shape[0]
    n_items = n // tm + N_GROUPS - 1
    wgu4 = wgu.reshape(N_GROUPS, EXPERTS_PER_GROUP, D_MODEL, 2 * D_EXPERT)
    wd4 = wd.reshape(N_GROUPS, EXPERTS_PER_GROUP, D_EXPERT, D_MODEL)
    grid_spec = pltpu.PrefetchScalarGridSpec(
        num_scalar_prefetch=1,
        grid=(n_items,),
        in_specs=[pl.BlockSpec((tm, PACK_COLS), lambda k, it: (it[0, k], 0)),
                  pl.BlockSpec((None, EXPERTS_PER_GROUP, D_MODEL, 2 * D_EXPERT),
                               lambda k, it: (it[1, k], 0, 0, 0)),
                  pl.BlockSpec((None, EXPERTS_PER_GROUP, D_EXPERT, D_MODEL),
                               lambda k, it: (it[1, k], 0, 0, 0))],
        out_specs=pl.BlockSpec((tm, D_MODEL), lambda k, it: (it[0, k], 0)))
    return pl.pallas_call(
        _moe_routed_kernel,
        out_shape=jax.ShapeDtypeStruct((n, D_MODEL), F32),
        grid_spec=grid_spec,
        compiler_params=_params(1),
        name="moe_routed",
    )(items, sorted_pack, wgu4, wd4)


def _resid_gather_kernel(tm, pos_ref, x_ref, g2_ref, ys_hbm, y_ref, buf, sem):
    _gather_rows(tm, pos_ref, ys_hbm, buf, sem)
    y_ref[...] = x_ref[...] + g2_ref[...] * buf[...]


def _resid_gather_call(x, pos3, ys, mod, mspec, tm):
    n = x.shape[0]
    row = lambda i: (i, 0)
    return pl.pallas_call(
        functools.partial(_resid_gather_kernel, tm),
        out_shape=jax.ShapeDtypeStruct((n, D_MODEL), F32),
        grid=(n // tm,),
        in_specs=[pl.BlockSpec((None, 1, tm), lambda i: (i, 0, 0), memory_space=pltpu.SMEM),
                  pl.BlockSpec((tm, D_MODEL), row), mspec(5),
                  pl.BlockSpec(memory_space=pl.ANY)],
        out_specs=pl.BlockSpec((tm, D_MODEL), row),
        scratch_shapes=[pltpu.VMEM((tm, D_MODEL), F32), pltpu.SemaphoreType.DMA(())],
        compiler_params=_params(1),
        name="moe_unsort_resid",
    )(pos3, x, mod, ys)


def _moe_kernel(u_ref, gate_ref, wgu_ref, wd_ref, x1_ref, g2_ref, y_ref, acc_ref):
    e = pl.program_id(1)

    @pl.when(e == 0)
    def _():
        acc_ref[...] = jnp.zeros_like(acc_ref)

    hu = _dot(u_ref[...], wgu_ref[...])
    hg = hu[:, 0:D_EXPERT]
    hv = hu[:, D_EXPERT:2 * D_EXPERT]
    gate = gate_ref[...]
    lane = lax.broadcasted_iota(jnp.int32, gate.shape, 1)
    gcol = jnp.sum(jnp.where(lane == N_GROUPS + e, gate, 0.0), axis=-1, keepdims=True)
    act = hg * jax.nn.sigmoid(hg) * hv * gcol
    acc_ref[...] += _dot(act.astype(BF16), wd_ref[...])

    @pl.when(e == N_EXPERTS - 1)
    def _():
        y_ref[...] = x1_ref[...] + g2_ref[...] * acc_ref[...]


def _moe_call(u2, gate, wgu, wd, x1, mod, mspec, tm):
    n = u2.shape[0]
    row = lambda i, e: (i, 0)
    return pl.pallas_call(
        _moe_kernel,
        out_shape=jax.ShapeDtypeStruct((n, D_MODEL), F32),
        grid=(n // tm, N_EXPERTS),
        in_specs=[pl.BlockSpec((tm, D_MODEL), row), pl.BlockSpec((tm, LANES), row),
                  pl.BlockSpec((None, D_MODEL, 2 * D_EXPERT), lambda i, e: (e, 0, 0)),
                  pl.BlockSpec((None, D_EXPERT, D_MODEL), lambda i, e: (e, 0, 0)),
                  pl.BlockSpec((tm, D_MODEL), row), mspec(5)],
        out_specs=pl.BlockSpec((tm, D_MODEL), row),
        scratch_shapes=[pltpu.VMEM((tm, D_MODEL), F32)],
        compiler_params=_params(2),
        name="moe_dense",
    )(u2, gate, wgu, wd, x1, mod)


def _rope128(x, cos, sa, sb):
    return x * cos + pltpu.roll(x, LANES - ROT_DIM // 2, axis=1) * sa \
        + pltpu.roll(x, ROT_DIM // 2, axis=1) * sb


def _kvq_kernel(gather_tm, *refs):
    if gather_tm:
        (pos_ref, x_ref, g2_ref, ys_hbm, kvsh_ref, kvsc_ref, sh1_ref, sc1_ref, kvn_ref, nm_ref,
         wkv_ref, wq_ref, kn_ref, qn_ref, cos_ref, sa_ref, sb_ref, g64_ref, gr_ref, gb_ref,
         q_ref, k_ref, v_ref, xa_ref, buf, sem) = refs
        _gather_rows(gather_tm, pos_ref, ys_hbm, buf, sem)
        x = x_ref[...] + g2_ref[...] * buf[...]
        xa_ref[...] = x
    else:
        (x_ref, kvsh_ref, kvsc_ref, sh1_ref, sc1_ref, kvn_ref, nm_ref,
         wkv_ref, wq_ref, kn_ref, qn_ref, cos_ref, sa_ref, sb_ref, g64_ref, gr_ref, gb_ref,
         q_ref, k_ref, v_ref) = refs
        x = x_ref[...]
    xn = _unit_rms(x)
    cos, sa, sb = cos_ref[...], sa_ref[...], sb_ref[...]

    ukv = xn * kvn_ref[...] * (1.0 + kvsc_ref[...]) + kvsh_ref[...]
    kv = _dot(ukv.astype(BF16), wkv_ref[...])
    k = kv[:, 0:LANES]
    v_ref[...] = kv[:, LANES:2 * LANES]
    kh, kl = _split2(k * k)
    ms = _dot(kh, g64_ref[...]) + _dot(kl, g64_ref[...])
    k_ref[...] = _rope128(k * lax.rsqrt(ms + RMS_EPS) * kn_ref[...], cos, sa, sb)

    u1 = xn * nm_ref[...] * (1.0 + sc1_ref[...]) + sh1_ref[...]
    q = _dot(u1.astype(BF16), wq_ref[...])
    qh, ql = _split2(q * q)
    ms16 = _dot(qh, gr_ref[...]) + _dot(ql, gr_ref[...])
    r3 = _split3(lax.rsqrt(ms16 + RMS_EPS))
    rsb = _dot(r3[0], gb_ref[...]) + _dot(r3[1], gb_ref[...]) + _dot(r3[2], gb_ref[...])
    qn = q * rsb * qn_ref[...]
    for i in range(D_MODEL // LANES):
        sl = slice(LANES * i, LANES * (i + 1))
        q_ref[:, sl] = (_rope128(qn[:, sl], cos, sa, sb) * (HEAD_DIM ** -0.5)).astype(BF16)


def _kvq_call(x, routed, mod0, mod1, mspec, kvmod, kvspec, kvn, nm, wkv, wq, kn, qn, tabs, tab,
              g64, gr, gb, tm):
    n = x.shape[0]
    row = lambda i: (i, 0)
    const = lambda i: (0, 0)
    vec = pl.BlockSpec((1, D_MODEL), const)
    tok = pl.BlockSpec((tm, D_MODEL), row)
    small = pl.BlockSpec((tm, LANES), row)
    ins = [x, kvmod, kvmod, mod1, mod1, kvn, nm, wkv, wq, kn, qn, *tabs, g64, gr, gb]
    specs = [tok, kvspec(0), kvspec(1), mspec(0), mspec(1), vec, vec,
             pl.BlockSpec((D_MODEL, 2 * LANES), const), pl.BlockSpec((D_MODEL, D_MODEL), const),
             pl.BlockSpec((1, LANES), const), vec, tab, tab, tab,
             pl.BlockSpec((LANES, LANES), const), pl.BlockSpec((D_MODEL, LANES), const),
             pl.BlockSpec((LANES, D_MODEL), const)]
    out_shape = [jax.ShapeDtypeStruct((n, D_MODEL), BF16), jax.ShapeDtypeStruct((n, LANES), F32),
                 jax.ShapeDtypeStruct((n, LANES), F32)]
    out_specs = [tok, small, small]
    scratch = []
    if routed is not None:
        pos3, ys = routed
        ins = [pos3, x, mod0, ys] + ins[1:]
        specs = [pl.BlockSpec((None, 1, tm), lambda i: (i, 0, 0), memory_space=pltpu.SMEM),
                 tok, mspec(5), pl.BlockSpec(memory_space=pl.ANY)] + specs[1:]
        out_shape.append(jax.ShapeDtypeStruct((n, D_MODEL), F32))
        out_specs.append(tok)
        scratch = [pltpu.VMEM((tm, D_MODEL), F32), pltpu.SemaphoreType.DMA(())]
    return pl.pallas_call(
        functools.partial(_kvq_kernel, tm if routed is not None else 0),
        out_shape=tuple(out_shape),
        grid=(n // tm,),
        in_specs=specs,
        out_specs=tuple(out_specs),
        scratch_shapes=scratch,
        compiler_params=_params(1),
        name="kv_q_proj",
    )(*ins)


def _attn_core(q, kcat, vcat, sinks_ref, first_block, o_ref):
    tq = q.shape[0]
    tk = kcat.shape[0]
    pairs = N_Q_HEADS // N_KV_HEADS // 2
    rows = pairs * tq
    lane = lax.broadcasted_iota(jnp.int32, (1, LANES), 1)
    lo = lane < HEAD_DIM
    kro = pltpu.roll(kcat, HEAD_DIM, axis=1)
    vro = pltpu.roll(vcat, HEAD_DIM, axis=1)
    kj = lax.broadcasted_iota(jnp.int32, (rows, tk), 1)
    qi = lax.broadcasted_iota(jnp.int32, (rows, tk), 0) & (tq - 1)
    valid = (kj > qi) & (kj <= qi + WINDOW)
    if first_block is not None:
        valid = valid & ((kj >= WINDOW) | jnp.logical_not(first_block))
    for g in range(N_KV_HEADS):
        if g == 0:
            ke, ko = jnp.where(lo, kcat, 0.0), jnp.where(lo, 0.0, kro)
            ve, vo = jnp.where(lo, vcat, 0.0), jnp.where(lo, 0.0, vro)
        else:
            ke, ko = jnp.where(lo, kro, 0.0), jnp.where(lo, 0.0, kcat)
            ve, vo = jnp.where(lo, vro, 0.0), jnp.where(lo, 0.0, vcat)
        k2 = jnp.concatenate([ke, ko], 0).astype(BF16)
        v2 = jnp.concatenate([ve, vo], 0).astype(BF16)
        base = g * pairs
        qs = jnp.concatenate([q[:, LANES * (base + p):LANES * (base + p + 1)]
                              for p in range(pairs)], 0)
        s = _dot_nt(qs, k2)
        halves = []
        inv = []
        for par in range(2):
            sp = jnp.where(valid, s[:, par * tk:(par + 1) * tk], -jnp.inf)
            sink = jnp.concatenate(
                [jnp.full((tq, 1), sinks_ref[2 * (base + p) + par], F32) for p in range(pairs)], 0)
            m = jnp.maximum(jnp.max(sp, axis=-1, keepdims=True), sink)
            pexp = jnp.exp(sp - m)
            denom = jnp.sum(pexp, axis=-1, keepdims=True) + jnp.exp(sink - m)
            halves.append(pexp.astype(BF16))
            inv.append(1.0 / denom)
        o = _dot(jnp.concatenate(halves, 1), v2)
        o = o * jnp.where(lo, inv[0], inv[1])
        for p in range(pairs):
            o_ref[:, LANES * (base + p):LANES * (base + p + 1)] = \
                o[tq * p:tq * (p + 1), :].astype(o_ref.dtype)


def _attn_prompt_kernel(sinks_ref, q_ref, kp_ref, kc_ref, vp_ref, vc_ref, o_ref):
    kcat = jnp.concatenate([kp_ref[...], kc_ref[...]], 0)
    vcat = jnp.concatenate([vp_ref[...], vc_ref[...]], 0)
    _attn_core(q_ref[...], kcat, vcat, sinks_ref, pl.program_id(1) == 0, o_ref)


def _attn_sample_kernel(sinks_ref, q_ref, k_ref, v_ref, o_ref):
    _attn_core(q_ref[...], k_ref[...], v_ref[...], sinks_ref, None, o_ref)


def _attn_prompt_call(sinks, q, k, v, B, nb):
    n = q.shape[0]
    cur = lambda b, i: (b * nb + i, 0)
    prev = lambda b, i: (b * nb + jnp.maximum(i - 1, 0), 0)
    kvb = lambda im: pl.BlockSpec((WINDOW, LANES), im)
    return pl.pallas_call(
        _attn_prompt_kernel,
        out_shape=jax.ShapeDtypeStruct((n, D_MODEL), BF16),
        grid=(B, nb),
        in_specs=[pl.BlockSpec(memory_space=pltpu.SMEM),
                  pl.BlockSpec((WINDOW, D_MODEL), cur), kvb(prev), kvb(cur), kvb(prev), kvb(cur)],
        out_specs=pl.BlockSpec((WINDOW, D_MODEL), cur),
        compiler_params=_params(2),
        name="swa_prompt",
    )(sinks, q, k, k, v, v)


def _attn_sample_call(sinks, q, kcat, vcat):
    B = q.shape[0]
    b3 = lambda b: (b, 0, 0)
    return pl.pallas_call(
        _attn_sample_kernel,
        out_shape=jax.ShapeDtypeStruct(q.shape, BF16),
        grid=(B,),
        in_specs=[pl.BlockSpec(memory_space=pltpu.SMEM),
                  pl.BlockSpec((None, SAMPLE_PAD, D_MODEL), b3),
                  pl.BlockSpec((None, 2 * WINDOW, LANES), b3),
                  pl.BlockSpec((None, 2 * WINDOW, LANES), b3)],
        out_specs=pl.BlockSpec((None, SAMPLE_PAD, D_MODEL), b3),
        compiler_params=_params(1),
        name="swa_sample",
    )(sinks, q, kcat, vcat)


def _rope_tables(pos):
    half = ROT_DIM // 2
    inv = ROPE_THETA ** (-jnp.arange(half, dtype=F32) / half)
    ang = pos.astype(F32)[:, None] * inv[None]
    cos, sin = jnp.cos(ang), jnp.sin(ang)
    d = jnp.arange(LANES) % HEAD_DIM
    idx = d % half
    cos_t = jnp.where(d < ROT_DIM, cos[:, idx], 1.0)
    sa = jnp.where(d < half, -sin[:, idx], 0.0)
    sb = jnp.where((d >= half) & (d < ROT_DIM), sin[:, idx], 0.0)
    return cos_t, sa, sb


def _pad_lanes(a, value=0.0):
    return jnp.pad(a, ((0, 0), (0, LANES - a.shape[1])), constant_values=value)


def _prep_weights(ada_w, ada_b, norm_mix, norm_ffn, a_w_in, a_b_gates, a_head_norm, a_w_out,
                  kv_ada_w, kv_ada_b, kv_norm, w_k, w_v, k_norm, b_w_q, b_q_norm, b_sinks, b_w_o,
                  moe_w_group, moe_b_group, moe_w_expert, moe_b_expert, moe_w_gate_up, moe_w_down):
    w = {}
    g0 = QK_COLS + 2 * V_COLS
    win = a_w_in[0]
    w["w_in"] = jnp.concatenate([win[:, :g0], _pad_lanes(win[:, g0:g0 + M_HEADS]),
                                 _pad_lanes(win[:, g0 + M_HEADS:])], 1).astype(BF16)
    w["bli"] = _pad_lanes(a_b_gates[0][None, :M_HEADS])
    w["blf"] = _pad_lanes(a_b_gates[0][None, M_HEADS:])
    w["head_norm"] = a_head_norm[0][None]
    w["w_out"] = a_w_out[0].astype(BF16)
    w["norm_mix"] = [norm_mix[l][None] for l in range(2)]
    w["norm_ffn"] = [norm_ffn[l][None] for l in range(2)]
    w["router"] = []
    for l in range(2):
        wr = _pad_lanes(jnp.concatenate([moe_w_group[l], moe_w_expert[l]], 1))
        hi = wr.astype(BF16)
        lo = (wr - hi.astype(F32)).astype(BF16)
        br = _pad_lanes(jnp.concatenate([moe_b_group[l], moe_b_expert[l]])[None])
        w["router"].append((hi, lo, br))
    w["w_gu"] = [moe_w_gate_up[l].astype(BF16) for l in range(2)]
    w["w_d"] = [moe_w_down[l].astype(BF16) for l in range(2)]
    w["kv_norm"] = kv_norm[None]
    w["w_kv"] = jnp.concatenate([w_k, w_v], 1).astype(BF16)
    w["k_norm"] = jnp.tile(k_norm, N_KV_HEADS)[None]
    w["w_q"] = b_w_q[0].astype(BF16)
    w["q_norm"] = jnp.tile(b_q_norm[0], N_Q_HEADS)[None]
    w["sinks"] = b_sinks[0]
    w["w_o"] = b_w_o[0].astype(BF16)
    lanes = jnp.arange(LANES)
    feat = jnp.arange(D_MODEL)
    w["g64"] = jnp.where((lanes[:, None] // HEAD_DIM) == (lanes[None, :] // HEAD_DIM),
                         1.0 / HEAD_DIM, 0.0).astype(BF16)
    w["gr"] = jnp.where((feat[:, None] // HEAD_DIM) == lanes[None, :], 1.0 / HEAD_DIM, 0.0).astype(BF16)
    w["gb"] = jnp.where(lanes[:, None] == (feat[None, :] // HEAD_DIM), 1.0, 0.0).astype(BF16)
    return w


def _trunk(x2, mods, kvmod, mspec, mspec_moe, w, *, B, T, L, tm, tm_moe, tabs, tabspec, c0, n0, m0,
           cache_k=None, cache_v=None):
    sample = cache_k is not None
    nc = T // L if not sample else 1

    qk, v, o, li, lf = _inproj_call(x2, mods[0], mspec, w["norm_mix"][0], w["w_in"],
                                    w["bli"], w["blf"], tm)
    if sample:
        def padtok(a, value=0.0):
            a = a.reshape(B, T, a.shape[-1])
            a = jnp.pad(a, ((0, 0), (0, L - T), (0, 0)), constant_values=value)
            return a.reshape(B * L, a.shape[-1])
        qk, v, li, lf = padtok(qk), padtok(v), padtok(li, M_EMPTY), padtok(lf)
    h, c_new, n_new, m_new = _mlstm_call(qk, v, li, lf, c0, n0, m0, B, nc, L)
    if sample:
        h = h.reshape(B, L, V_COLS)[:, :T].reshape(B * T, V_COLS)
    routed = not sample
    ntiles = (B * T) // tm

    def routed_moe(pack, gid, layer):
        pos, items = _sort_meta_call(gid.reshape(ntiles, tm), tm)
        pos3 = pos.reshape(ntiles, 1, tm)
        ys = _moe_routed_call(items, _scatter_call(pack, pos3, tm), w["w_gu"][layer], w["w_d"][layer], tm)
        return pos3, ys

    rh, rl, br = w["router"][0]
    post0 = _post_call(True, routed, h, o, x2, mods[0], mspec, w["norm_ffn"][0], w["head_norm"],
                       w["w_out"], rh, rl, br, tm)
    kvq_args = (mods[0], mods[1], mspec, kvmod, mspec, w["kv_norm"], w["norm_mix"][1],
                w["w_kv"], w["w_q"], w["k_norm"], w["q_norm"], tabs, tabspec,
                w["g64"], w["gr"], w["gb"], tm)

    if routed:
        x1, pack, gid = post0
        q, k, vv, xa = _kvq_call(x1, routed_moe(pack, gid, 0), *kvq_args)
    else:
        x1, u2, gate = post0
        xa = _moe_call(u2, gate, w["w_gu"][0], w["w_d"][0], x1, mods[0], mspec_moe, tm_moe)
        q, k, vv = _kvq_call(xa, None, *kvq_args)
    if not sample:
        att = _attn_prompt_call(w["sinks"], q, k, vv, B, T // WINDOW)
        k_win = k.reshape(B, T, N_KV_HEADS, HEAD_DIM)[:, T - WINDOW:]
        v_win = vv.reshape(B, T, N_KV_HEADS, HEAD_DIM)[:, T - WINDOW:]
    else:
        zpad = jnp.zeros((B, WINDOW - T, LANES), F32)
        kcat = jnp.concatenate([cache_k.reshape(B, WINDOW, LANES), k.reshape(B, T, LANES), zpad], 1)
        vcat = jnp.concatenate([cache_v.reshape(B, WINDOW, LANES), vv.reshape(B, T, LANES), zpad], 1)
        qp = jnp.pad(q.reshape(B, T, D_MODEL), ((0, 0), (0, SAMPLE_PAD - T), (0, 0)))
        att = _attn_sample_call(w["sinks"], qp, kcat, vcat)[:, :T].reshape(B * T, D_MODEL)
        k_win = kcat[:, T:T + WINDOW].reshape(B, WINDOW, N_KV_HEADS, HEAD_DIM)
        v_win = vcat[:, T:T + WINDOW].reshape(B, WINDOW, N_KV_HEADS, HEAD_DIM)
    rh, rl, br = w["router"][1]
    post1 = _post_call(False, routed, att, None, xa, mods[1], mspec, w["norm_ffn"][1], None,
                       w["w_o"], rh, rl, br, tm)
    if routed:
        x3, pack, gid = post1
        pos3, ys = routed_moe(pack, gid, 1)
        y = _resid_gather_call(x3, pos3, ys, mods[1], mspec, tm)
    else:
        x3, u4, gate = post1
        y = _moe_call(u4, gate, w["w_gu"][1], w["w_d"][1], x3, mods[1], mspec_moe, tm_moe)
    c_out = c_new[None]
    n_out = n_new.reshape(1, B, M_HEADS, M_DK)
    m_out = m_new[:, 0, :M_HEADS][None]
    return y, c_out, n_out, m_out, k_win, v_win


def kernel(x_prompt, x_sample, c_prompt, c_sample, state_c, state_n, state_m, cache_k_win, cache_v_win, ada_w, ada_b, norm_mix, norm_ffn, a_w_in, a_b_gates, a_head_norm, a_w_out, kv_ada_w, kv_ada_b, kv_norm, w_k, w_v, k_norm, b_w_q, b_q_norm, b_sinks, b_w_o, moe_w_group, moe_b_group, moe_w_expert, moe_b_expert, moe_w_gate_up, moe_w_down):
    Bp, Tp, D = x_prompt.shape
    Bs, Ts, _ = x_sample.shape
    w = _prep_weights(ada_w, ada_b, norm_mix, norm_ffn, a_w_in, a_b_gates, a_head_norm, a_w_out,
                      kv_ada_w, kv_ada_b, kv_norm, w_k, w_v, k_norm, b_w_q, b_q_norm, b_sinks, b_w_o,
                      moe_w_group, moe_b_group, moe_w_expert, moe_b_expert, moe_w_gate_up, moe_w_down)

    rows = Bp + Bs
    rpad = -rows % 8
    c_all = jnp.concatenate([c_prompt, c_sample, jnp.zeros((rpad, D), F32)], 0)
    mod = _ada_call(c_all, ada_w, ada_b[:, None, :])
    kvm = _ada_call(c_all, kv_ada_w[None], kv_ada_b[None, None, :])

    tm_p = 512
    tiles_per_seq = Tp // tm_p
    mods_p = [mod[l, :Bp][:, None, :] for l in range(2)]
    kvmod_p = kvm[0, :Bp][:, None, :]

    def mspec_p(col):
        return pl.BlockSpec((None, 1, D_MODEL), lambda i, *_: (i // tiles_per_seq, 0, col))

    tm_moe_p = 1024
    moe_tiles_per_seq = Tp // tm_moe_p

    def mspec_moe_p(col):
        return pl.BlockSpec((None, 1, D_MODEL), lambda i, *_: (i // moe_tiles_per_seq, 0, col))

    tabs_p = _rope_tables(jnp.arange(Tp, dtype=jnp.int32))
    tabspec_p = pl.BlockSpec((tm_p, LANES), lambda i: (i % tiles_per_seq, 0))
    npairs = M_HEADS // 2
    c0 = jnp.zeros((Bp, M_HEADS, M_DK, M_DV), F32)
    n0 = jnp.zeros((Bp, npairs, LANES), F32)
    m0 = jnp.pad(jnp.full((Bp, 1, M_HEADS), M_EMPTY, F32), ((0, 0), (0, 0), (0, LANES - M_HEADS)))
    yp, cp, np_, mp, kwp, vwp = _trunk(
        x_prompt.reshape(Bp * Tp, D), mods_p, kvmod_p, mspec_p, mspec_moe_p, w,
        B=Bp, T=Tp, L=M_CHUNK, tm=tm_p, tm_moe=tm_moe_p, tabs=tabs_p, tabspec=tabspec_p,
        c0=c0, n0=n0, m0=m0)

    ns = Bs * Ts
    mods_s = [jnp.repeat(mod[l, Bp:Bp + Bs], Ts, axis=0)[None] for l in range(2)]
    kvmod_s = jnp.repeat(kvm[0, Bp:Bp + Bs], Ts, axis=0)[None]

    def mspec_s(col):
        return pl.BlockSpec((None, ns, D_MODEL), lambda i, *_: (0, 0, col))

    tabs_s = _rope_tables(PAST_LEN + jnp.arange(Ts, dtype=jnp.int32))
    tabs_s = tuple(jnp.tile(t, (Bs, 1)) for t in tabs_s)
    m0s = jnp.pad(state_m[0][:, None, :], ((0, 0), (0, 0), (0, LANES - M_HEADS)))
    ys, cs, ns_, ms, kws, vws = _trunk(
        x_sample.reshape(ns, D), mods_s, kvmod_s, mspec_s, mspec_s, w,
        B=Bs, T=Ts, L=SAMPLE_PAD, tm=ns, tm_moe=ns, tabs=tabs_s,
        tabspec=pl.BlockSpec((ns, LANES), lambda i: (0, 0)),
        c0=state_c[0], n0=state_n[0].reshape(Bs, npairs, LANES), m0=m0s,
        cache_k=cache_k_win, cache_v=cache_v_win)

    return (yp.reshape(Bp, Tp, D), ys.reshape(Bs, Ts, D), cp, np_, mp, kwp, vwp,
            cs, ns_, ms, kws, vws)
```

```python
import functools

import jax
import jax.numpy as jnp
from jax import lax
from jax.experimental import pallas as pl
from jax.experimental.pallas import tpu as pltpu

F32 = jnp.float32
BF16 = jnp.bfloat16

D_MODEL = 1024
PAST_LEN = 8192
M_HEADS = 8
M_DK = 64
M_DV = 128
M_CHUNK = 64
GATE_SOFTCAP = 15.0
M_EMPTY = -1e30
WINDOW = 128
HEAD_DIM = 64
N_Q_HEADS = 16
N_KV_HEADS = 2
ROPE_THETA = 500000.0
ROT_DIM = 16
N_GROUPS = 4
EXPERTS_PER_GROUP = 4
N_EXPERTS = 16
D_EXPERT = 256
RMS_EPS = 1e-6

LANES = 128
QK_COLS = 2 * M_HEADS * M_DK
V_COLS = M_HEADS * M_DV
IN_COLS = QK_COLS + 2 * V_COLS + 2 * LANES
SAMPLE_PAD = 16
VMEM_LIMIT = 52 * 1024 * 1024

NT_DIMS = (((1,), (1,)), ((), ()))


def _params(n_axes):
    return pltpu.CompilerParams(dimension_semantics=("arbitrary",) * n_axes,
                                vmem_limit_bytes=VMEM_LIMIT)


def _dot(a, b):
    return jnp.dot(a, b, preferred_element_type=F32)


def _dot_nt(a, b):
    return lax.dot_general(a, b, NT_DIMS, preferred_element_type=F32)


def _split2(x):
    hi = x.astype(BF16)
    lo = (x - hi.astype(F32)).astype(BF16)
    return hi, lo


def _split3(x):
    hi = x.astype(BF16)
    r = x - hi.astype(F32)
    mid = r.astype(BF16)
    lo = (r - mid.astype(F32)).astype(BF16)
    return hi, mid, lo


def _unit_rms(x):
    return x * lax.rsqrt(jnp.mean(x * x, axis=-1, keepdims=True) + RMS_EPS)


def _ada_kernel(c_ref, w_ref, b_ref, o_ref):
    c = c_ref[...]
    cs = (c * jax.nn.sigmoid(c)).astype(BF16)
    o_ref[...] = _dot(cs, w_ref[...].astype(BF16)) + b_ref[...]


def _ada_call(c, w, b):
    g, d, n = w.shape
    r = c.shape[0]
    tn = 1024
    return pl.pallas_call(
        _ada_kernel,
        out_shape=jax.ShapeDtypeStruct((g, r, n), F32),
        grid=(g, n // tn),
        in_specs=[pl.BlockSpec((r, d), lambda i, j: (0, 0)),
                  pl.BlockSpec((None, d, tn), lambda i, j: (i, 0, j)),
                  pl.BlockSpec((None, 1, tn), lambda i, j: (i, 0, j))],
        out_specs=pl.BlockSpec((None, r, tn), lambda i, j: (i, 0, j)),
        compiler_params=_params(2),
        name="ada_mod",
    )(c, w, b)


def _inproj_kernel(x_ref, sh_ref, sc_ref, nw_ref, w_ref, bli_ref, blf_ref,
                   qk_ref, v_ref, o_ref, li_ref, lf_ref):
    u = _unit_rms(x_ref[...]) * nw_ref[...] * (1.0 + sc_ref[...]) + sh_ref[...]
    ub = u.astype(BF16)
    half = QK_COLS // 2
    q = _dot(ub, w_ref[:, 0:half]) * (M_DK ** -0.5)
    qk_ref[:, 0:half] = q.astype(BF16)
    qk_ref[:, half:QK_COLS] = _dot(ub, w_ref[:, half:QK_COLS]).astype(BF16)
    v_ref[...] = _dot(ub, w_ref[:, QK_COLS:QK_COLS + V_COLS]).astype(BF16)
    o_ref[...] = _dot(ub, w_ref[:, QK_COLS + V_COLS:QK_COLS + 2 * V_COLS]).astype(BF16)
    g0 = QK_COLS + 2 * V_COLS
    lane = lax.broadcasted_iota(jnp.int32, (1, LANES), 1)
    live = lane < M_HEADS
    gi = _dot(ub, w_ref[:, g0:g0 + LANES]) + bli_ref[...]
    gf = _dot(ub, w_ref[:, g0 + LANES:g0 + 2 * LANES]) + blf_ref[...]
    li = GATE_SOFTCAP * jnp.tanh(gi / GATE_SOFTCAP)
    fpre = GATE_SOFTCAP * jnp.tanh(gf / GATE_SOFTCAP)
    lf = jnp.minimum(fpre, 0.0) - jnp.log1p(jnp.exp(-jnp.abs(fpre)))
    li_ref[...] = jnp.where(live, li, 0.0)
    lf_ref[...] = jnp.where(live, lf, 0.0)


def _inproj_call(x, mod, mspec, nw, w, bli, blf, tm):
    n = x.shape[0]
    row = lambda i: (i, 0)
    const = lambda i: (0, 0)
    return pl.pallas_call(
        _inproj_kernel,
        out_shape=(jax.ShapeDtypeStruct((n, QK_COLS), BF16),
                   jax.ShapeDtypeStruct((n, V_COLS), BF16),
                   jax.ShapeDtypeStruct((n, V_COLS), BF16),
                   jax.ShapeDtypeStruct((n, LANES), F32),
                   jax.ShapeDtypeStruct((n, LANES), F32)),
        grid=(n // tm,),
        in_specs=[pl.BlockSpec((tm, D_MODEL), row), mspec(0), mspec(1),
                  pl.BlockSpec((1, D_MODEL), const),
                  pl.BlockSpec((D_MODEL, IN_COLS), const),
                  pl.BlockSpec((1, LANES), const), pl.BlockSpec((1, LANES), const)],
        out_specs=(pl.BlockSpec((tm, QK_COLS), row), pl.BlockSpec((tm, V_COLS), row),
                   pl.BlockSpec((tm, V_COLS), row), pl.BlockSpec((tm, LANES), row),
                   pl.BlockSpec((tm, LANES), row)),
        compiler_params=_params(1),
        name="mlstm_inproj",
    )(x, mod, mod, nw, w, bli, blf)


def _mlstm_kernel(L, qk_ref, v_ref, li_ref, lf_ref, c0_ref, n0_ref, m0_ref,
                  h_ref, cout_ref, nout_ref, mout_ref, c2_s, n_s, m_s):
    c = pl.program_id(1)
    nc = pl.num_programs(1)
    npairs = M_HEADS // 2
    hd = M_DK

    lane = lax.broadcasted_iota(jnp.int32, (1, LANES), 1)
    lo128 = lane < hd
    lane256 = lax.broadcasted_iota(jnp.int32, (1, 2 * LANES), 1)
    lo256 = lane256 < LANES
    row128 = lax.broadcasted_iota(jnp.int32, (LANES, 1), 0)
    top = row128 < hd
    blockdiag = (top & lo256) | (jnp.logical_not(top) & jnp.logical_not(lo256))

    @pl.when(c == 0)
    def _():
        m_s[...] = m0_ref[...]
        n_s[...] = n0_ref[...]
        z = jnp.zeros((M_DK, M_DV), F32)
        for j in range(npairs):
            ce = c0_ref[2 * j]
            co = c0_ref[2 * j + 1]
            c2_s[j] = jnp.concatenate([jnp.concatenate([ce, z], 1),
                                       jnp.concatenate([z, co], 1)], 0)

    LI = li_ref[...]
    LF = lf_ref[...]
    rowL = lax.broadcasted_iota(jnp.int32, (L, LANES), 0)

    def prefix(x, op, ident):
        d = 1
        while d < L:
            shifted = pltpu.roll(x, d, axis=0)
            x = op(x, jnp.where(rowL >= d, shifted, ident))
            d *= 2
        return x

    Bc = prefix(LF, jnp.add, 0.0)
    Cm = LI - Bc
    mprev = m_s[...]
    Gc = jnp.maximum(mprev, prefix(Cm, jnp.maximum, -jnp.inf))
    A = jnp.exp(mprev - Gc)
    EXPM = jnp.exp(-(Bc + Gc))
    bL = Bc[L - 1:L, :]
    DL = bL + Cm
    mnew = jnp.maximum(bL + mprev, jnp.max(DL, axis=0, keepdims=True))
    ast = jnp.exp(bL + mprev - mnew)
    WST = jnp.exp(DL - mnew)

    def pad_rows(x, rows):
        if x.shape[0] == rows:
            return x
        return jnp.concatenate([x, jnp.zeros((rows - x.shape[0], x.shape[1]), x.dtype)], 0)

    cm_pad = pad_rows(Cm, hd)
    XT = jnp.concatenate([cm_pad, cm_pad], 0).T
    causal = (lane & (hd - 1)) <= rowL
    J = jnp.where(blockdiag, 1.0, 0.0).astype(BF16)

    def bc(X, h):
        return jnp.broadcast_to(X[:, h:h + 1], X.shape)

    for j in range(npairs):
        he, ho = 2 * j, 2 * j + 1

        def pair128(X):
            return jnp.where(lo128, bc(X, he), bc(X, ho))

        def pair256(X):
            return jnp.concatenate([bc(X, he), bc(X, ho)], 1)

        q128 = qk_ref[:, LANES * j:LANES * (j + 1)]
        k128 = qk_ref[:, QK_COLS // 2 + LANES * j:QK_COLS // 2 + LANES * (j + 1)]
        v256 = v_ref[:, 2 * LANES * j:2 * LANES * (j + 1)]
        zk = jnp.zeros_like(k128)
        zv = jnp.zeros_like(v256)
        K2t = jnp.concatenate([pad_rows(jnp.where(lo128, k128, zk), hd),
                               pad_rows(jnp.where(lo128, zk, k128), hd)], 0)
        V2 = jnp.concatenate([pad_rows(jnp.where(lo256, v256, zv), hd),
                              pad_rows(jnp.where(lo256, zv, v256), hd)], 0)
        S = _dot_nt(q128, K2t)
        crow = jnp.where(lo128, XT[he:he + 1, :], XT[ho:ho + 1, :])
        arg = jnp.where(causal, crow - pair128(Gc), -jnp.inf)
        Sw = (S * jnp.exp(arg)).astype(BF16)
        num_intra = _dot(Sw, V2)
        rowsum = _dot(Sw, J)
        C2 = c2_s[j]
        inter = _dot(q128, C2.astype(BF16))
        npair = n_s[j:j + 1, :]
        Nrow2 = jnp.concatenate(
            [jnp.broadcast_to(jnp.where(lo128, npair, 0.0), (LANES, LANES)),
             jnp.broadcast_to(jnp.where(lo128, 0.0, npair), (LANES, LANES))], 0).astype(BF16)
        qn = _dot_nt(q128, Nrow2)
        a256 = pair256(A)
        num = a256 * inter + num_intra
        den = a256 * qn + rowsum
        h = num / jnp.maximum(jnp.abs(den), pair256(EXPM))
        h_ref[:, 2 * LANES * j:2 * LANES * (j + 1)] = h.astype(h_ref.dtype)

        kw = k128.astype(F32) * pair128(WST)
        n_s[j:j + 1, :] = pair128(ast) * npair + jnp.sum(kw, axis=0, keepdims=True)
        kwT = pad_rows(kw, LANES).T.astype(BF16)
        dC = _dot(kwT, pad_rows(v256, LANES))
        c2_s[j] = pair256(ast) * C2 + jnp.where(blockdiag, dC, 0.0)

    m_s[...] = mnew

    @pl.when(c == nc - 1)
    def _():
        for j in range(npairs):
            C2 = c2_s[j]
            cout_ref[2 * j] = C2[0:M_DK, 0:M_DV]
            cout_ref[2 * j + 1] = C2[M_DK:2 * M_DK, M_DV:2 * M_DV]
        nout_ref[...] = n_s[...]
        mout_ref[...] = mnew


def _mlstm_call(qk, v, li, lf, c0, n0, m0, B, nc, L):
    n = qk.shape[0]
    npairs = M_HEADS // 2
    tok = lambda b, c: (b * nc + c, 0)
    st4 = lambda b, c: (b, 0, 0, 0)
    st3 = lambda b, c: (b, 0, 0)
    return pl.pallas_call(
        functools.partial(_mlstm_kernel, L),
        out_shape=(jax.ShapeDtypeStruct((n, V_COLS), BF16),
                   jax.ShapeDtypeStruct((B, M_HEADS, M_DK, M_DV), F32),
                   jax.ShapeDtypeStruct((B, npairs, LANES), F32),
                   jax.ShapeDtypeStruct((B, 1, LANES), F32)),
        grid=(B, nc),
        in_specs=[pl.BlockSpec((L, QK_COLS), tok), pl.BlockSpec((L, V_COLS), tok),
                  pl.BlockSpec((L, LANES), tok), pl.BlockSpec((L, LANES), tok),
                  pl.BlockSpec((None, M_HEADS, M_DK, M_DV), st4),
                  pl.BlockSpec((None, npairs, LANES), st3),
                  pl.BlockSpec((None, 1, LANES), st3)],
        out_specs=(pl.BlockSpec((L, V_COLS), tok),
                   pl.BlockSpec((None, M_HEADS, M_DK, M_DV), st4),
                   pl.BlockSpec((None, npairs, LANES), st3),
                   pl.BlockSpec((None, 1, LANES), st3)),
        scratch_shapes=[pltpu.VMEM((npairs, 2 * M_DK, 2 * M_DV), F32),
                        pltpu.VMEM((npairs, LANES), F32),
                        pltpu.VMEM((1, LANES), F32)],
        compiler_params=_params(2),
        name="mlstm_chunks",
    )(qk, v, li, lf, c0, n0, m0)


def _route(lg):
    lane = lax.broadcasted_iota(jnp.int32, lg.shape, 1)
    lanef = lane.astype(F32)
    neg = -jnp.inf
    far = float(LANES)
    gm = lane < N_GROUPS
    lgm = jnp.where(gm, lg, neg)
    gmax = jnp.max(lgm, axis=-1, keepdims=True)
    gsum = jnp.sum(jnp.exp(lgm - gmax), axis=-1, keepdims=True)
    g_w = 1.0 / gsum
    gidx = jnp.min(jnp.where(gm & (lg == gmax), lanef, far), axis=-1, keepdims=True)
    first = N_GROUPS + EXPERTS_PER_GROUP * gidx
    sel = (lanef >= first) & (lanef < first + EXPERTS_PER_GROUP)
    l1 = jnp.max(jnp.where(sel, lg, neg), axis=-1, keepdims=True)
    i1 = jnp.min(jnp.where(sel & (lg == l1), lanef, far), axis=-1, keepdims=True)
    sel2 = sel & (lanef != i1)
    l2 = jnp.max(jnp.where(sel2, lg, neg), axis=-1, keepdims=True)
    i2 = jnp.min(jnp.where(sel2 & (lg == l2), lanef, far), axis=-1, keepdims=True)
    r = jnp.exp(l2 - l1)
    w1 = g_w / (1.0 + r)
    w2 = w1 * r
    return jnp.where(lanef == i1, w1, jnp.where(lanef == i2, w2, 0.0)), gidx


def _post_kernel(mlstm, routed, *refs):
    refs = list(refs)
    h_ref = refs.pop(0)
    o_ref = refs.pop(0) if mlstm else None
    x_ref, g1_ref, sh2_ref, sc2_ref, nf_ref = refs[:5]
    refs = refs[5:]
    hn_ref = refs.pop(0) if mlstm else None
    wout_ref, wrh_ref, wrl_ref, br_ref = refs[:4]
    refs = refs[4:]
    tri_ref = refs.pop(0) if routed else None
    x1_ref, u2_ref, gate_ref = refs[:3]
    outs = refs[3:]
    if mlstm:
        hf = h_ref[...].astype(F32)
        parts = [_unit_rms(hf[:, M_DV * i:M_DV * (i + 1)]) for i in range(M_HEADS)]
        hn = jnp.concatenate(parts, 1) * hn_ref[...]
        hg = (hn * jax.nn.sigmoid(o_ref[...].astype(F32))).astype(BF16)
    else:
        hg = h_ref[...]
    x1 = x_ref[...] + g1_ref[...] * _dot(hg, wout_ref[...])
    x1_ref[...] = x1
    u2 = _unit_rms(x1) * nf_ref[...] * (1.0 + sc2_ref[...]) + sh2_ref[...]
    uh, ul = _split2(u2)
    lg = _dot(uh, wrh_ref[...]) + _dot(ul, wrh_ref[...]) + _dot(uh, wrl_ref[...]) + br_ref[...]
    gate, gidx = _route(lg)
    u2_ref[...] = uh
    gate_ref[...] = gate
    if not routed:
        return
    gid_ref, col_ref = outs
    rows = []
    for blk in range(x1.shape[0] // LANES):
        col = jnp.broadcast_to(gidx[LANES * blk:LANES * (blk + 1), :], (LANES, LANES))
        rows.append(col.T[0:1, :])
    gid_ref[...] = jnp.concatenate(rows, 1)
    lanef = lax.broadcasted_iota(jnp.int32, gate.shape, 1).astype(F32)
    onehot = jnp.where(lanef == gidx, 1.0, 0.0)
    earlier = _dot(tri_ref[...], onehot.astype(BF16))
    rank = jnp.sum(onehot * earlier, axis=-1, keepdims=True)
    col_ref[...] = jnp.where(lanef == 0.0, gidx, jnp.where(lanef == 1.0, rank, 0.0))


def _post_call(mlstm, routed, h, o, x, mod, mspec, nf, hn, wout, wrh, wrl, br, tm):
    n = x.shape[0]
    row = lambda i: (i, 0)
    const = lambda i: (0, 0)
    tok = pl.BlockSpec((tm, D_MODEL), row)
    small = pl.BlockSpec((tm, LANES), row)
    vec = pl.BlockSpec((1, D_MODEL), const)
    ins = [h] + ([o] if mlstm else []) + [x, mod, mod, mod, nf] + ([hn] if mlstm else []) \
        + [wout, wrh, wrl, br]
    specs = [tok] + ([tok] if mlstm else []) + [tok, mspec(2), mspec(3), mspec(4), vec] \
        + ([vec] if mlstm else []) \
        + [pl.BlockSpec((D_MODEL, D_MODEL), const), pl.BlockSpec((D_MODEL, LANES), const),
           pl.BlockSpec((D_MODEL, LANES), const), pl.BlockSpec((1, LANES), const)]
    out_shape = [jax.ShapeDtypeStruct((n, D_MODEL), F32), jax.ShapeDtypeStruct((n, D_MODEL), BF16),
                 jax.ShapeDtypeStruct((n, LANES), F32)]
    out_specs = [tok, tok, small]
    if routed:
        t = jnp.arange(tm)
        ins.append((t[None, :] < t[:, None]).astype(BF16))
        specs.append(pl.BlockSpec((tm, tm), const))
        out_shape += [jax.ShapeDtypeStruct((n // tm, 1, tm), F32), jax.ShapeDtypeStruct((n, LANES), F32)]
        out_specs += [pl.BlockSpec((None, 1, tm), lambda i: (i, 0, 0)), small]
    out_shape, out_specs = tuple(out_shape), tuple(out_specs)
    return pl.pallas_call(
        functools.partial(_post_kernel, mlstm, routed),
        out_shape=out_shape,
        grid=(n // tm,),
        in_specs=specs,
        out_specs=out_specs,
        compiler_params=_params(1),
        name="post_mlstm" if mlstm else "post_attn",
    )(*ins)


ITEM_ROWS = 8
RUN_ALIGN = 16
RUN_SIZES = (512, 256, 128, 64, 32, 16)
RUN_SRC, RUN_DST, RUN_LEN, RUN_TOTAL = 0, N_GROUPS, 2 * N_GROUPS, 3 * N_GROUPS


def _stage_rows(tm):
    assert N_GROUPS * (RUN_ALIGN - 1) <= LANES
    return tm + LANES


def _sorted_rows(n, tm):
    rows = n + N_GROUPS * (n // tm) * RUN_ALIGN + tm
    return -(-rows // tm) * tm


def _sort_meta_kernel(tm, nts, gid_ref, q_ref, runs_ref, items_ref):
    ntiles = gid_ref.shape[0]
    gid = gid_ref[...]
    r = lax.broadcasted_iota(jnp.int32, (tm, tm), 0)
    c = lax.broadcasted_iota(jnp.int32, (tm, tm), 1)
    before = jnp.where(r < c, 1.0, 0.0).astype(BF16)
    trow = lax.broadcasted_iota(jnp.int32, (ntiles, LANES), 0)
    lane = lax.broadcasted_iota(jnp.int32, (ntiles, LANES), 1)
    k = lax.broadcasted_iota(jnp.int32, (1, LANES), 1).astype(F32)
    zero11 = jnp.zeros((1, 1), F32)
    q = jnp.zeros((ntiles, tm), F32)
    runs = jnp.zeros((ntiles, LANES), F32)
    src = jnp.zeros((ntiles, 1), F32)
    start, nitems = zero11, zero11
    grp = jnp.zeros((1, LANES), F32)
    tile = jnp.zeros((1, LANES), F32)
    valid = jnp.zeros((1, LANES), F32)
    for g in range(N_GROUPS):
        mask = jnp.where(gid == float(g), 1.0, 0.0)
        cnt = jnp.sum(mask, axis=1, keepdims=True)
        padded = jnp.floor((cnt + (RUN_ALIGN - 1.0)) * (1.0 / RUN_ALIGN)) * RUN_ALIGN
        incl = jnp.broadcast_to(padded, (ntiles, LANES))
        d = 1
        while d < ntiles:
            incl = incl + jnp.where(trow >= d, pltpu.roll(incl, d, axis=0), 0.0)
            d *= 2
        total = incl[ntiles - 1:ntiles, 0:1]
        dst = start + incl[:, 0:1] - padded
        q = q + mask * (src + _dot(mask.astype(BF16), before))
        runs = runs + jnp.where(lane == RUN_SRC + g, src, 0.0) + jnp.where(lane == RUN_DST + g, dst, 0.0) \
            + jnp.where(lane == RUN_LEN + g, padded, 0.0)
        src = src + padded
        end = start + total
        ft = jnp.floor(start * (1.0 / tm))
        lt = jnp.floor((end - 1.0) * (1.0 / tm))
        ni = jnp.where(total > 0.0, lt - ft + 1.0, 0.0)
        inside = (k >= nitems) & (k < nitems + ni)
        grp = grp + jnp.where(inside, float(g), 0.0)
        tile = tile + jnp.where(inside, ft + (k - nitems), 0.0)
        valid = valid + jnp.where(inside, 1.0, 0.0)
        nitems = nitems + ni
        start = end
    runs = runs + jnp.where(lane == RUN_TOTAL, start, 0.0)
    live = valid > 0.0
    prev = pltpu.roll(tile, 1, axis=1)
    nxt = pltpu.roll(tile, LANES - 1, axis=1)
    first = live & ((k == 0.0) | (tile != prev))
    last = live & ((k == nitems - 1.0) | (tile != nxt))
    spare = jnp.floor((start - 1.0) * (1.0 / tm)) + 1.0 + (k - nitems)
    fill = jnp.logical_not(live) & (spare <= nts - 1.0)
    tile = jnp.where(live, tile, jnp.minimum(spare, nts - 1.0))
    grp = jnp.where(live, grp, jnp.max(grp, axis=1, keepdims=True))
    flag = lambda m: jnp.where(m, 1.0, 0.0)
    q_ref[...] = q
    runs_ref[...] = runs.astype(jnp.int32)
    table = jnp.concatenate([tile, grp, valid, flag(first | fill), flag(last | fill),
                             jnp.zeros((ITEM_ROWS - 5, LANES), F32)], 0)
    items_ref[...] = table.astype(jnp.int32)


def _sort_meta_call(gid, tm):
    ntiles = gid.shape[0]
    nts = _sorted_rows(ntiles * tm, tm) // tm
    assert nts + N_GROUPS - 1 <= LANES and tm <= RUN_SIZES[0]
    return pl.pallas_call(
        functools.partial(_sort_meta_kernel, tm, nts),
        out_shape=(jax.ShapeDtypeStruct((ntiles, tm), F32),
                   jax.ShapeDtypeStruct((ntiles, LANES), jnp.int32),
                   jax.ShapeDtypeStruct((ITEM_ROWS, LANES), jnp.int32)),
        compiler_params=_params(0),
        name="moe_sort_meta",
    )(gid)


def _run_copies(runs_ref, i, tile_ref, sorted_hbm, sem, to_sorted):
    pieces = []
    for g in range(N_GROUPS):
        src = runs_ref[i, RUN_SRC + g]
        dst = runs_ref[i, RUN_DST + g]
        length = runs_ref[i, RUN_LEN + g]
        for s in RUN_SIZES:
            def build(src=src, dst=dst, length=length, s=s):
                off = length & (-2 * s)
                a = tile_ref.at[pl.ds(pl.multiple_of(src + off, RUN_ALIGN), s), :]
                b = sorted_hbm.at[pl.ds(pl.multiple_of(dst + off, RUN_ALIGN), s), :]
                return pltpu.make_async_copy(a, b, sem) if to_sorted else pltpu.make_async_copy(b, a, sem)
            pieces.append(((length & s) != 0, build))
    return pieces


def _start_then_wait(pieces):
    for pred, build in pieces:
        pl.when(pred)(lambda build=build: build().start())
    for pred, build in pieces:
        pl.when(pred)(lambda build=build: build().wait())


def _scatter_kernel(tm, nts, ntiles, runs_ref, q_ref, u_ref, g_ref, su_hbm, sg_hbm,
                    stu, stg, sem_u, sem_g):
    i = pl.program_id(0)
    r = lax.broadcasted_iota(jnp.int32, (_stage_rows(tm), tm), 0).astype(F32)
    perm = jnp.where(q_ref[...] == r, 1.0, 0.0).astype(BF16)
    stu[...] = _dot(perm, u_ref[...]).astype(BF16)
    g3 = _split3(g_ref[...])
    stg[...] = _dot(perm, g3[0]) + _dot(perm, g3[1]) + _dot(perm, g3[2])
    _start_then_wait(_run_copies(runs_ref, i, stu, su_hbm, sem_u, True)
                     + _run_copies(runs_ref, i, stg, sg_hbm, sem_g, True))

    @pl.when(i == ntiles - 1)
    def _():
        stu[...] = jnp.zeros_like(stu)
        stg[...] = jnp.zeros_like(stg)
        total = runs_ref[i, RUN_TOTAL]
        tail = nts * tm - total
        nfull = tail // tm
        pieces = []
        for tile_ref, hbm, sem in ((stu, su_hbm, sem_u), (stg, sg_hbm, sem_g)):
            for j in range(nts - ntiles):
                def full(j=j, tile_ref=tile_ref, hbm=hbm, sem=sem):
                    dst = pl.multiple_of(total + j * tm, RUN_ALIGN)
                    return pltpu.make_async_copy(tile_ref.at[pl.ds(0, tm), :], hbm.at[pl.ds(dst, tm), :], sem)
                pieces.append((j < nfull, full))
            rem = tail - nfull * tm
            for s in RUN_SIZES:
                if s >= tm:
                    continue
                def part(s=s, tile_ref=tile_ref, hbm=hbm, sem=sem):
                    dst = pl.multiple_of(total + nfull * tm + (rem & (-2 * s)), RUN_ALIGN)
                    return pltpu.make_async_copy(tile_ref.at[pl.ds(0, s), :], hbm.at[pl.ds(dst, s), :], sem)
                pieces.append(((rem & s) != 0, part))
        _start_then_wait(pieces)


def _scatter_call(runs, q3, u2, gate, tm):
    n = u2.shape[0]
    ns = _sorted_rows(n, tm)
    grid_spec = pltpu.PrefetchScalarGridSpec(
        num_scalar_prefetch=1,
        grid=(n // tm,),
        in_specs=[pl.BlockSpec((None, 1, tm), lambda i, *_: (i, 0, 0)),
                  pl.BlockSpec((tm, D_MODEL), lambda i, *_: (i, 0)),
                  pl.BlockSpec((tm, LANES), lambda i, *_: (i, 0))],
        out_specs=(pl.BlockSpec(memory_space=pl.ANY), pl.BlockSpec(memory_space=pl.ANY)),
        scratch_shapes=[pltpu.VMEM((_stage_rows(tm), D_MODEL), BF16), pltpu.VMEM((_stage_rows(tm), LANES), F32),
                        pltpu.SemaphoreType.DMA(()), pltpu.SemaphoreType.DMA(())])
    return pl.pallas_call(
        functools.partial(_scatter_kernel, tm, ns // tm, n // tm),
        out_shape=(jax.ShapeDtypeStruct((ns, D_MODEL), BF16), jax.ShapeDtypeStruct((ns, LANES), F32)),
        grid_spec=grid_spec,
        compiler_params=_params(1),
        name="moe_scatter",
    )(runs, q3, u2, gate)


def _ungroup(tm, runs_ref, col_ref, ys_hbm, stage, sem):
    i = pl.program_id(0)

    @pl.when(i == 0)
    def _():
        stage[...] = jnp.zeros_like(stage)

    _start_then_wait(_run_copies(runs_ref, i, stage, ys_hbm, sem, False))
    col = col_ref[...]
    gidx = col[:, 0:1]
    src = [runs_ref[i, RUN_SRC + g].astype(F32) for g in range(N_GROUPS)]
    first = src[N_GROUPS - 1]
    for g in range(N_GROUPS - 2, -1, -1):
        first = jnp.where(gidx == float(g), src[g], first)
    q = first + col[:, 1:2]
    lanes = lax.broadcasted_iota(jnp.int32, (tm, _stage_rows(tm)), 1).astype(F32)
    perm_t = jnp.where(q == lanes, 1.0, 0.0).astype(BF16)
    return _dot(perm_t, stage[...])


def _moe_routed_kernel(items_ref, su_ref, sg_ref, wgu_ref, wd_ref, out_ref, acc_ref):
    kk = pl.program_id(0)
    grp = items_ref[1, kk]

    @pl.when(items_ref[3, kk] == 1)
    def _():
        acc_ref[...] = jnp.zeros_like(acc_ref)

    @pl.when(items_ref[2, kk] == 1)
    def _():
        u = su_ref[...]
        gate = sg_ref[...]
        lane = lax.broadcasted_iota(jnp.int32, gate.shape, 1)
        first_lane = N_GROUPS + EXPERTS_PER_GROUP * grp
        for e in range(EXPERTS_PER_GROUP):
            gcol = jnp.sum(jnp.where(lane == first_lane + e, gate, 0.0), axis=-1, keepdims=True)
            hu = _dot(u, wgu_ref[e])
            hg = hu[:, 0:D_EXPERT]
            act = hg * jax.nn.sigmoid(hg) * hu[:, D_EXPERT:2 * D_EXPERT] * gcol
            acc_ref[...] += _dot(act.astype(BF16), wd_ref[e])

    @pl.when(items_ref[4, kk] == 1)
    def _():
        out_ref[...] = acc_ref[...].astype(out_ref.dtype)


def _moe_routed_call(items, su, sg, wgu, wd, tm):
    ns = su.shape[0]
    n_items = ns // tm + N_GROUPS - 1
    wgu4 = wgu.reshape(N_GROUPS, EXPERTS_PER_GROUP, D_MODEL, 2 * D_EXPERT)
    wd4 = wd.reshape(N_GROUPS, EXPERTS_PER_GROUP, D_EXPERT, D_MODEL)
    grid_spec = pltpu.PrefetchScalarGridSpec(
        num_scalar_prefetch=1,
        grid=(n_items,),
        in_specs=[pl.BlockSpec((tm, D_MODEL), lambda k, it: (it[0, k], 0)),
                  pl.BlockSpec((tm, LANES), lambda k, it: (it[0, k], 0)),
                  pl.BlockSpec((None, EXPERTS_PER_GROUP, D_MODEL, 2 * D_EXPERT),
                               lambda k, it: (it[1, k], 0, 0, 0)),
                  pl.BlockSpec((None, EXPERTS_PER_GROUP, D_EXPERT, D_MODEL),
                               lambda k, it: (it[1, k], 0, 0, 0))],
        out_specs=pl.BlockSpec((tm, D_MODEL), lambda k, it: (it[0, k], 0)),
        scratch_shapes=[pltpu.VMEM((tm, D_MODEL), F32)])
    return pl.pallas_call(
        _moe_routed_kernel,
        out_shape=jax.ShapeDtypeStruct((ns, D_MODEL), BF16),
        grid_spec=grid_spec,
        compiler_params=_params(1),
        name="moe_routed",
    )(items, su, sg, wgu4, wd4)


def _resid_gather_kernel(tm, runs_ref, x_ref, g2_ref, col_ref, ys_hbm, y_ref, stage, sem):
    y_ref[...] = x_ref[...] + g2_ref[...] * _ungroup(tm, runs_ref, col_ref, ys_hbm, stage, sem)


def _resid_gather_call(x, runs, col, ys, mod, mspec, tm):
    n = x.shape[0]
    row = lambda i, *_: (i, 0)
    grid_spec = pltpu.PrefetchScalarGridSpec(
        num_scalar_prefetch=1,
        grid=(n // tm,),
        in_specs=[pl.BlockSpec((tm, D_MODEL), row), mspec(5), pl.BlockSpec((tm, LANES), row),
                  pl.BlockSpec(memory_space=pl.ANY)],
        out_specs=pl.BlockSpec((tm, D_MODEL), row),
        scratch_shapes=[pltpu.VMEM((_stage_rows(tm), D_MODEL), BF16), pltpu.SemaphoreType.DMA(())])
    return pl.pallas_call(
        functools.partial(_resid_gather_kernel, tm),
        out_shape=jax.ShapeDtypeStruct((n, D_MODEL), F32),
        grid_spec=grid_spec,
        compiler_params=_params(1),
        name="moe_unsort_resid",
    )(runs, x, mod, col, ys)


def _moe_kernel(u_ref, gate_ref, wgu_ref, wd_ref, x1_ref, g2_ref, y_ref, acc_ref):
    e = pl.program_id(1)

    @pl.when(e == 0)
    def _():
        acc_ref[...] = jnp.zeros_like(acc_ref)

    hu = _dot(u_ref[...], wgu_ref[...])
    hg = hu[:, 0:D_EXPERT]
    hv = hu[:, D_EXPERT:2 * D_EXPERT]
    gate = gate_ref[...]
    lane = lax.broadcasted_iota(jnp.int32, gate.shape, 1)
    gcol = jnp.sum(jnp.where(lane == N_GROUPS + e, gate, 0.0), axis=-1, keepdims=True)
    act = hg * jax.nn.sigmoid(hg) * hv * gcol
    acc_ref[...] += _dot(act.astype(BF16), wd_ref[...])

    @pl.when(e == N_EXPERTS - 1)
    def _():
        y_ref[...] = x1_ref[...] + g2_ref[...] * acc_ref[...]


def _moe_call(u2, gate, wgu, wd, x1, mod, mspec, tm):
    n = u2.shape[0]
    row = lambda i, e: (i, 0)
    return pl.pallas_call(
        _moe_kernel,
        out_shape=jax.ShapeDtypeStruct((n, D_MODEL), F32),
        grid=(n // tm, N_EXPERTS),
        in_specs=[pl.BlockSpec((tm, D_MODEL), row), pl.BlockSpec((tm, LANES), row),
                  pl.BlockSpec((None, D_MODEL, 2 * D_EXPERT), lambda i, e: (e, 0, 0)),
                  pl.BlockSpec((None, D_EXPERT, D_MODEL), lambda i, e: (e, 0, 0)),
                  pl.BlockSpec((tm, D_MODEL), row), mspec(5)],
        out_specs=pl.BlockSpec((tm, D_MODEL), row),
        scratch_shapes=[pltpu.VMEM((tm, D_MODEL), F32)],
        compiler_params=_params(2),
        name="moe_dense",
    )(u2, gate, wgu, wd, x1, mod)


def _rope128(x, cos, sa, sb):
    return x * cos + pltpu.roll(x, LANES - ROT_DIM // 2, axis=1) * sa \
        + pltpu.roll(x, ROT_DIM // 2, axis=1) * sb


def _kvq_kernel(gather_tm, *refs):
    if gather_tm:
        (runs_ref, x_ref, g2_ref, col_ref, ys_hbm, kvsh_ref, kvsc_ref, sh1_ref, sc1_ref, kvn_ref,
         nm_ref, wkv_ref, wq_ref, kn_ref, qn_ref, cos_ref, sa_ref, sb_ref, g64_ref, gr_ref, gb_ref,
         q_ref, k_ref, v_ref, xa_ref, stage, sem) = refs
        x = x_ref[...] + g2_ref[...] * _ungroup(gather_tm, runs_ref, col_ref, ys_hbm, stage, sem)
        xa_ref[...] = x
    else:
        (x_ref, kvsh_ref, kvsc_ref, sh1_ref, sc1_ref, kvn_ref, nm_ref,
         wkv_ref, wq_ref, kn_ref, qn_ref, cos_ref, sa_ref, sb_ref, g64_ref, gr_ref, gb_ref,
         q_ref, k_ref, v_ref) = refs
        x = x_ref[...]
    xn = _unit_rms(x)
    cos, sa, sb = cos_ref[...], sa_ref[...], sb_ref[...]

    ukv = xn * kvn_ref[...] * (1.0 + kvsc_ref[...]) + kvsh_ref[...]
    kv = _dot(ukv.astype(BF16), wkv_ref[...])
    k = kv[:, 0:LANES]
    v_ref[...] = kv[:, LANES:2 * LANES]
    kh, kl = _split2(k * k)
    ms = _dot(kh, g64_ref[...]) + _dot(kl, g64_ref[...])
    k_ref[...] = _rope128(k * lax.rsqrt(ms + RMS_EPS) * kn_ref[...], cos, sa, sb)

    u1 = xn * nm_ref[...] * (1.0 + sc1_ref[...]) + sh1_ref[...]
    q = _dot(u1.astype(BF16), wq_ref[...])
    qh, ql = _split2(q * q)
    ms16 = _dot(qh, gr_ref[...]) + _dot(ql, gr_ref[...])
    r3 = _split3(lax.rsqrt(ms16 + RMS_EPS))
    rsb = _dot(r3[0], gb_ref[...]) + _dot(r3[1], gb_ref[...]) + _dot(r3[2], gb_ref[...])
    qn = q * rsb * qn_ref[...]
    for i in range(D_MODEL // LANES):
        sl = slice(LANES * i, LANES * (i + 1))
        q_ref[:, sl] = (_rope128(qn[:, sl], cos, sa, sb) * (HEAD_DIM ** -0.5)).astype(BF16)


def _kvq_call(x, routed, mod0, mod1, mspec, kvmod, kvspec, kvn, nm, wkv, wq, kn, qn, tabs, tab,
              g64, gr, gb, tm):
    n = x.shape[0]
    row = lambda i, *_: (i, 0)
    const = lambda i, *_: (0, 0)
    vec = pl.BlockSpec((1, D_MODEL), const)
    tok = pl.BlockSpec((tm, D_MODEL), row)
    small = pl.BlockSpec((tm, LANES), row)
    ins = [x, kvmod, kvmod, mod1, mod1, kvn, nm, wkv, wq, kn, qn, *tabs, g64, gr, gb]
    specs = [tok, kvspec(0), kvspec(1), mspec(0), mspec(1), vec, vec,
             pl.BlockSpec((D_MODEL, 2 * LANES), const), pl.BlockSpec((D_MODEL, D_MODEL), const),
             pl.BlockSpec((1, LANES), const), vec, tab, tab, tab,
             pl.BlockSpec((LANES, LANES), const), pl.BlockSpec((D_MODEL, LANES), const),
             pl.BlockSpec((LANES, D_MODEL), const)]
    out_shape = [jax.ShapeDtypeStruct((n, D_MODEL), BF16), jax.ShapeDtypeStruct((n, LANES), F32),
                 jax.ShapeDtypeStruct((n, LANES), F32)]
    out_specs = [tok, small, small]
    scratch = []
    nprefetch = 0
    if routed is not None:
        runs, col, ys = routed
        nprefetch = 1
        ins = [runs, x, mod0, col, ys] + ins[1:]
        specs = [tok, mspec(5), small, pl.BlockSpec(memory_space=pl.ANY)] + specs[1:]
        out_shape.append(jax.ShapeDtypeStruct((n, D_MODEL), F32))
        out_specs.append(tok)
        scratch = [pltpu.VMEM((_stage_rows(tm), D_MODEL), BF16), pltpu.SemaphoreType.DMA(())]
    grid_spec = pltpu.PrefetchScalarGridSpec(
        num_scalar_prefetch=nprefetch, grid=(n // tm,), in_specs=specs,
        out_specs=tuple(out_specs), scratch_shapes=scratch)
    return pl.pallas_call(
        functools.partial(_kvq_kernel, tm if routed is not None else 0),
        out_shape=tuple(out_shape),
        grid_spec=grid_spec,
        compiler_params=_params(1),
        name="kv_q_proj",
    )(*ins)


def _attn_core(q, kcat, vcat, sinks_ref, first_block, o_ref):
    tq = q.shape[0]
    tk = kcat.shape[0]
    pairs = N_Q_HEADS // N_KV_HEADS // 2
    rows = pairs * tq
    lane = lax.broadcasted_iota(jnp.int32, (1, LANES), 1)
    lo = lane < HEAD_DIM
    kro = pltpu.roll(kcat, HEAD_DIM, axis=1)
    vro = pltpu.roll(vcat, HEAD_DIM, axis=1)
    kj = lax.broadcasted_iota(jnp.int32, (rows, tk), 1)
    qi = lax.broadcasted_iota(jnp.int32, (rows, tk), 0) & (tq - 1)
    valid = (kj > qi) & (kj <= qi + WINDOW)
    if first_block is not None:
        valid = valid & ((kj >= WINDOW) | jnp.logical_not(first_block))
    for g in range(N_KV_HEADS):
        if g == 0:
            ke, ko = jnp.where(lo, kcat, 0.0), jnp.where(lo, 0.0, kro)
            ve, vo = jnp.where(lo, vcat, 0.0), jnp.where(lo, 0.0, vro)
        else:
            ke, ko = jnp.where(lo, kro, 0.0), jnp.where(lo, 0.0, kcat)
            ve, vo = jnp.where(lo, vro, 0.0), jnp.where(lo, 0.0, vcat)
        k2 = jnp.concatenate([ke, ko], 0).astype(BF16)
        v2 = jnp.concatenate([ve, vo], 0).astype(BF16)
        base = g * pairs
        qs = jnp.concatenate([q[:, LANES * (base + p):LANES * (base + p + 1)]
                              for p in range(pairs)], 0)
        s = _dot_nt(qs, k2)
        halves = []
        inv = []
        for par in range(2):
            sp = jnp.where(valid, s[:, par * tk:(par + 1) * tk], -jnp.inf)
            sink = jnp.concatenate(
                [jnp.full((tq, 1), sinks_ref[2 * (base + p) + par], F32) for p in range(pairs)], 0)
            m = jnp.maximum(jnp.max(sp, axis=-1, keepdims=True), sink)
            pexp = jnp.exp(sp - m)
            denom = jnp.sum(pexp, axis=-1, keepdims=True) + jnp.exp(sink - m)
            halves.append(pexp.astype(BF16))
            inv.append(1.0 / denom)
        o = _dot(jnp.concatenate(halves, 1), v2)
        o = o * jnp.where(lo, inv[0], inv[1])
        for p in range(pairs):
            o_ref[:, LANES * (base + p):LANES * (base + p + 1)] = \
                o[tq * p:tq * (p + 1), :].astype(o_ref.dtype)


def _attn_prompt_kernel(sinks_ref, q_ref, kp_ref, kc_ref, vp_ref, vc_ref, o_ref):
    kcat = jnp.concatenate([kp_ref[...], kc_ref[...]], 0)
    vcat = jnp.concatenate([vp_ref[...], vc_ref[...]], 0)
    _attn_core(q_ref[...], kcat, vcat, sinks_ref, pl.program_id(1) == 0, o_ref)


def _attn_sample_kernel(sinks_ref, q_ref, k_ref, v_ref, o_ref):
    _attn_core(q_ref[...], k_ref[...], v_ref[...], sinks_ref, None, o_ref)


def _attn_prompt_call(sinks, q, k, v, B, nb):
    n = q.shape[0]
    cur = lambda b, i: (b * nb + i, 0)
    prev = lambda b, i: (b * nb + jnp.maximum(i - 1, 0), 0)
    kvb = lambda im: pl.BlockSpec((WINDOW, LANES), im)
    return pl.pallas_call(
        _attn_prompt_kernel,
        out_shape=jax.ShapeDtypeStruct((n, D_MODEL), BF16),
        grid=(B, nb),
        in_specs=[pl.BlockSpec(memory_space=pltpu.SMEM),
                  pl.BlockSpec((WINDOW, D_MODEL), cur), kvb(prev), kvb(cur), kvb(prev), kvb(cur)],
        out_specs=pl.BlockSpec((WINDOW, D_MODEL), cur),
        compiler_params=_params(2),
        name="swa_prompt",
    )(sinks, q, k, k, v, v)


def _attn_sample_call(sinks, q, kcat, vcat):
    B = q.shape[0]
    b3 = lambda b: (b, 0, 0)
    return pl.pallas_call(
        _attn_sample_kernel,
        out_shape=jax.ShapeDtypeStruct(q.shape, BF16),
        grid=(B,),
        in_specs=[pl.BlockSpec(memory_space=pltpu.SMEM),
                  pl.BlockSpec((None, SAMPLE_PAD, D_MODEL), b3),
                  pl.BlockSpec((None, 2 * WINDOW, LANES), b3),
                  pl.BlockSpec((None, 2 * WINDOW, LANES), b3)],
        out_specs=pl.BlockSpec((None, SAMPLE_PAD, D_MODEL), b3),
        compiler_params=_params(1),
        name="swa_sample",
    )(sinks, q, kcat, vcat)


def _rope_tables(pos):
    half = ROT_DIM // 2
    inv = ROPE_THETA ** (-jnp.arange(half, dtype=F32) / half)
    ang = pos.astype(F32)[:, None] * inv[None]
    cos, sin = jnp.cos(ang), jnp.sin(ang)
    d = jnp.arange(LANES) % HEAD_DIM
    idx = d % half
    cos_t = jnp.where(d < ROT_DIM, cos[:, idx], 1.0)
    sa = jnp.where(d < half, -sin[:, idx], 0.0)
    sb = jnp.where((d >= half) & (d < ROT_DIM), sin[:, idx], 0.0)
    return cos_t, sa, sb


def _pad_lanes(a, value=0.0):
    return jnp.pad(a, ((0, 0), (0, LANES - a.shape[1])), constant_values=value)


def _prep_weights(ada_w, ada_b, norm_mix, norm_ffn, a_w_in, a_b_gates, a_head_norm, a_w_out,
                  kv_ada_w, kv_ada_b, kv_norm, w_k, w_v, k_norm, b_w_q, b_q_norm, b_sinks, b_w_o,
                  moe_w_group, moe_b_group, moe_w_expert, moe_b_expert, moe_w_gate_up, moe_w_down):
    w = {}
    g0 = QK_COLS + 2 * V_COLS
    win = a_w_in[0]
    w["w_in"] = jnp.concatenate([win[:, :g0], _pad_lanes(win[:, g0:g0 + M_HEADS]),
                                 _pad_lanes(win[:, g0 + M_HEADS:])], 1).astype(BF16)
    w["bli"] = _pad_lanes(a_b_gates[0][None, :M_HEADS])
    w["blf"] = _pad_lanes(a_b_gates[0][None, M_HEADS:])
    w["head_norm"] = a_head_norm[0][None]
    w["w_out"] = a_w_out[0].astype(BF16)
    w["norm_mix"] = [norm_mix[l][None] for l in range(2)]
    w["norm_ffn"] = [norm_ffn[l][None] for l in range(2)]
    w["router"] = []
    for l in range(2):
        wr = _pad_lanes(jnp.concatenate([moe_w_group[l], moe_w_expert[l]], 1))
        hi = wr.astype(BF16)
        lo = (wr - hi.astype(F32)).astype(BF16)
        br = _pad_lanes(jnp.concatenate([moe_b_group[l], moe_b_expert[l]])[None])
        w["router"].append((hi, lo, br))
    w["w_gu"] = [moe_w_gate_up[l].astype(BF16) for l in range(2)]
    w["w_d"] = [moe_w_down[l].astype(BF16) for l in range(2)]
    w["kv_norm"] = kv_norm[None]
    w["w_kv"] = jnp.concatenate([w_k, w_v], 1).astype(BF16)
    w["k_norm"] = jnp.tile(k_norm, N_KV_HEADS)[None]
    w["w_q"] = b_w_q[0].astype(BF16)
    w["q_norm"] = jnp.tile(b_q_norm[0], N_Q_HEADS)[None]
    w["sinks"] = b_sinks[0]
    w["w_o"] = b_w_o[0].astype(BF16)
    lanes = jnp.arange(LANES)
    feat = jnp.arange(D_MODEL)
    w["g64"] = jnp.where((lanes[:, None] // HEAD_DIM) == (lanes[None, :] // HEAD_DIM),
                         1.0 / HEAD_DIM, 0.0).astype(BF16)
    w["gr"] = jnp.where((feat[:, None] // HEAD_DIM) == lanes[None, :], 1.0 / HEAD_DIM, 0.0).astype(BF16)
    w["gb"] = jnp.where(lanes[:, None] == (feat[None, :] // HEAD_DIM), 1.0, 0.0).astype(BF16)
    return w


def _trunk(x2, mods, kvmod, mspec, mspec_moe, w, *, B, T, L, tm, tm_moe, tabs, tabspec, c0, n0, m0,
           cache_k=None, cache_v=None):
    sample = cache_k is not None
    nc = T // L if not sample else 1

    qk, v, o, li, lf = _inproj_call(x2, mods[0], mspec, w["norm_mix"][0], w["w_in"],
                                    w["bli"], w["blf"], tm)
    if sample:
        def padtok(a, value=0.0):
            a = a.reshape(B, T, a.shape[-1])
            a = jnp.pad(a, ((0, 0), (0, L - T), (0, 0)), constant_values=value)
            return a.reshape(B * L, a.shape[-1])
        qk, v, li, lf = padtok(qk), padtok(v), padtok(li, M_EMPTY), padtok(lf)
    h, c_new, n_new, m_new = _mlstm_call(qk, v, li, lf, c0, n0, m0, B, nc, L)
    if sample:
        h = h.reshape(B, L, V_COLS)[:, :T].reshape(B * T, V_COLS)
    routed = not sample
    ntiles = (B * T) // tm

    def routed_moe(u2, gate, gid, col, layer):
        q, runs, items = _sort_meta_call(gid.reshape(ntiles, tm), tm)
        su, sg = _scatter_call(runs, q.reshape(ntiles, 1, tm), u2, gate, tm)
        ys = _moe_routed_call(items, su, sg, w["w_gu"][layer], w["w_d"][layer], tm)
        return runs, col, ys

    rh, rl, br = w["router"][0]
    post0 = _post_call(True, routed, h, o, x2, mods[0], mspec, w["norm_ffn"][0], w["head_norm"],
                       w["w_out"], rh, rl, br, tm)
    kvq_args = (mods[0], mods[1], mspec, kvmod, mspec, w["kv_norm"], w["norm_mix"][1],
                w["w_kv"], w["w_q"], w["k_norm"], w["q_norm"], tabs, tabspec,
                w["g64"], w["gr"], w["gb"], tm)

    if routed:
        x1, u2, gate, gid, col = post0
        q, k, vv, xa = _kvq_call(x1, routed_moe(u2, gate, gid, col, 0), *kvq_args)
    else:
        x1, u2, gate = post0
        xa = _moe_call(u2, gate, w["w_gu"][0], w["w_d"][0], x1, mods[0], mspec_moe, tm_moe)
        q, k, vv = _kvq_call(xa, None, *kvq_args)
    if not sample:
        att = _attn_prompt_call(w["sinks"], q, k, vv, B, T // WINDOW)
        k_win = k.reshape(B, T, N_KV_HEADS, HEAD_DIM)[:, T - WINDOW:]
        v_win = vv.reshape(B, T, N_KV_HEADS, HEAD_DIM)[:, T - WINDOW:]
    else:
        zpad = jnp.zeros((B, WINDOW - T, LANES), F32)
        kcat = jnp.concatenate([cache_k.reshape(B, WINDOW, LANES), k.reshape(B, T, LANES), zpad], 1)
        vcat = jnp.concatenate([cache_v.reshape(B, WINDOW, LANES), vv.reshape(B, T, LANES), zpad], 1)
        qp = jnp.pad(q.reshape(B, T, D_MODEL), ((0, 0), (0, SAMPLE_PAD - T), (0, 0)))
        att = _attn_sample_call(w["sinks"], qp, kcat, vcat)[:, :T].reshape(B * T, D_MODEL)
        k_win = kcat[:, T:T + WINDOW].reshape(B, WINDOW, N_KV_HEADS, HEAD_DIM)
        v_win = vcat[:, T:T + WINDOW].reshape(B, WINDOW, N_KV_HEADS, HEAD_DIM)
    rh, rl, br = w["router"][1]
    post1 = _post_call(False, routed, att, None, xa, mods[1], mspec, w["norm_ffn"][1], None,
                       w["w_o"], rh, rl, br, tm)
    if routed:
        x3, u4, gate, gid, col = post1
        runs, col, ys = routed_moe(u4, gate, gid, col, 1)
        y = _resid_gather_call(x3, runs, col, ys, mods[1], mspec, tm)
    else:
        x3, u4, gate = post1
        y = _moe_call(u4, gate, w["w_gu"][1], w["w_d"][1], x3, mods[1], mspec_moe, tm_moe)
    c_out = c_new[None]
    n_out = n_new.reshape(1, B, M_HEADS, M_DK)
    m_out = m_new[:, 0, :M_HEADS][None]
    return y, c_out, n_out, m_out, k_win, v_win


def kernel(x_prompt, x_sample, c_prompt, c_sample, state_c, state_n, state_m, cache_k_win, cache_v_win, ada_w, ada_b, norm_mix, norm_ffn, a_w_in, a_b_gates, a_head_norm, a_w_out, kv_ada_w, kv_ada_b, kv_norm, w_k, w_v, k_norm, b_w_q, b_q_norm, b_sinks, b_w_o, moe_w_group, moe_b_group, moe_w_expert, moe_b_expert, moe_w_gate_up, moe_w_down):
    Bp, Tp, D = x_prompt.shape
    Bs, Ts, _ = x_sample.shape
    w = _prep_weights(ada_w, ada_b, norm_mix, norm_ffn, a_w_in, a_b_gates, a_head_norm, a_w_out,
                      kv_ada_w, kv_ada_b, kv_norm, w_k, w_v, k_norm, b_w_q, b_q_norm, b_sinks, b_w_o,
                      moe_w_group, moe_b_group, moe_w_expert, moe_b_expert, moe_w_gate_up, moe_w_down)

    rows = Bp + Bs
    rpad = -rows % 8
    c_all = jnp.concatenate([c_prompt, c_sample, jnp.zeros((rpad, D), F32)], 0)
    mod = _ada_call(c_all, ada_w, ada_b[:, None, :])
    kvm = _ada_call(c_all, kv_ada_w[None], kv_ada_b[None, None, :])

    tm_p = 512
    tiles_per_seq = Tp // tm_p
    mods_p = [mod[l, :Bp][:, None, :] for l in range(2)]
    kvmod_p = kvm[0, :Bp][:, None, :]

    def mspec_p(col):
        return pl.BlockSpec((None, 1, D_MODEL), lambda i, *_: (i // tiles_per_seq, 0, col))

    tm_moe_p = 1024
    moe_tiles_per_seq = Tp // tm_moe_p

    def mspec_moe_p(col):
        return pl.BlockSpec((None, 1, D_MODEL), lambda i, *_: (i // moe_tiles_per_seq, 0, col))

    tabs_p = _rope_tables(jnp.arange(Tp, dtype=jnp.int32))
    tabspec_p = pl.BlockSpec((tm_p, LANES), lambda i, *_: (i % tiles_per_seq, 0))
    npairs = M_HEADS // 2
    c0 = jnp.zeros((Bp, M_HEADS, M_DK, M_DV), F32)
    n0 = jnp.zeros((Bp, npairs, LANES), F32)
    m0 = jnp.pad(jnp.full((Bp, 1, M_HEADS), M_EMPTY, F32), ((0, 0), (0, 0), (0, LANES - M_HEADS)))
    yp, cp, np_, mp, kwp, vwp = _trunk(
        x_prompt.reshape(Bp * Tp, D), mods_p, kvmod_p, mspec_p, mspec_moe_p, w,
        B=Bp, T=Tp, L=M_CHUNK, tm=tm_p, tm_moe=tm_moe_p, tabs=tabs_p, tabspec=tabspec_p,
        c0=c0, n0=n0, m0=m0)

    ns = Bs * Ts
    mods_s = [jnp.repeat(mod[l, Bp:Bp + Bs], Ts, axis=0)[None] for l in range(2)]
    kvmod_s = jnp.repeat(kvm[0, Bp:Bp + Bs], Ts, axis=0)[None]

    def mspec_s(col):
        return pl.BlockSpec((None, ns, D_MODEL), lambda i, *_: (0, 0, col))

    tabs_s = _rope_tables(PAST_LEN + jnp.arange(Ts, dtype=jnp.int32))
    tabs_s = tuple(jnp.tile(t, (Bs, 1)) for t in tabs_s)
    m0s = jnp.pad(state_m[0][:, None, :], ((0, 0), (0, 0), (0, LANES - M_HEADS)))
    ys, cs, ns_, ms, kws, vws = _trunk(
        x_sample.reshape(ns, D), mods_s, kvmod_s, mspec_s, mspec_s, w,
        B=Bs, T=Ts, L=SAMPLE_PAD, tm=ns, tm_moe=ns, tabs=tabs_s,
        tabspec=pl.BlockSpec((ns, LANES), lambda i, *_: (0, 0)),
        c0=state_c[0], n0=state_n[0].reshape(Bs, npairs, LANES), m0=m0s,
        cache_k=cache_k_win, cache_v=cache_v_win)

    return (yp.reshape(Bp, Tp, D), ys.reshape(Bs, Ts, D), cp, np_, mp, kwp, vwp,
            cs, ns_, ms, kws, vws)
```

```python
import functools

import jax
import jax.numpy as jnp
from jax import lax
from jax.experimental import pallas as pl
from jax.experimental.pallas import tpu as pltpu

F32 = jnp.float32
BF16 = jnp.bfloat16

D_MODEL = 1024
PAST_LEN = 8192
M_HEADS = 8
M_DK = 64
M_DV = 128
M_CHUNK = 64
GATE_SOFTCAP = 15.0
M_EMPTY = -1e30
WINDOW = 128
HEAD_DIM = 64
N_Q_HEADS = 16
N_KV_HEADS = 2
ROPE_THETA = 500000.0
ROT_DIM = 16
N_GROUPS = 4
EXPERTS_PER_GROUP = 4
N_EXPERTS = 16
D_EXPERT = 256
RMS_EPS = 1e-6

LANES = 128
QK_COLS = 2 * M_HEADS * M_DK
V_COLS = M_HEADS * M_DV
IN_COLS = QK_COLS + 2 * V_COLS + 2 * LANES
SAMPLE_PAD = 16
VMEM_LIMIT = 52 * 1024 * 1024

NT_DIMS = (((1,), (1,)), ((), ()))


def _params(n_axes):
    return pltpu.CompilerParams(dimension_semantics=("arbitrary",) * n_axes,
                                vmem_limit_bytes=VMEM_LIMIT)


def _dot(a, b):
    return jnp.dot(a, b, preferred_element_type=F32)


def _dot_nt(a, b):
    return lax.dot_general(a, b, NT_DIMS, preferred_element_type=F32)


def _split2(x):
    hi = x.astype(BF16)
    lo = (x - hi.astype(F32)).astype(BF16)
    return hi, lo


def _split3(x):
    hi = x.astype(BF16)
    r = x - hi.astype(F32)
    mid = r.astype(BF16)
    lo = (r - mid.astype(F32)).astype(BF16)
    return hi, mid, lo


def _unit_rms(x):
    return x * lax.rsqrt(jnp.mean(x * x, axis=-1, keepdims=True) + RMS_EPS)


def _ada_kernel(c_ref, w_ref, b_ref, o_ref):
    c = c_ref[...]
    cs = (c * jax.nn.sigmoid(c)).astype(BF16)
    o_ref[...] = _dot(cs, w_ref[...].astype(BF16)) + b_ref[...]


def _ada_call(c, w, b):
    g, d, n = w.shape
    r = c.shape[0]
    tn = 1024
    return pl.pallas_call(
        _ada_kernel,
        out_shape=jax.ShapeDtypeStruct((g, r, n), F32),
        grid=(g, n // tn),
        in_specs=[pl.BlockSpec((r, d), lambda i, j: (0, 0)),
                  pl.BlockSpec((None, d, tn), lambda i, j: (i, 0, j)),
                  pl.BlockSpec((None, 1, tn), lambda i, j: (i, 0, j))],
        out_specs=pl.BlockSpec((None, r, tn), lambda i, j: (i, 0, j)),
        compiler_params=_params(2),
        name="ada_mod",
    )(c, w, b)


def _inproj_kernel(x_ref, sh_ref, sc_ref, nw_ref, w_ref, bli_ref, blf_ref,
                   qk_ref, v_ref, o_ref, li_ref, lf_ref):
    u = _unit_rms(x_ref[...]) * nw_ref[...] * (1.0 + sc_ref[...]) + sh_ref[...]
    ub = u.astype(BF16)
    half = QK_COLS // 2
    q = _dot(ub, w_ref[:, 0:half]) * (M_DK ** -0.5)
    qk_ref[:, 0:half] = q.astype(BF16)
    qk_ref[:, half:QK_COLS] = _dot(ub, w_ref[:, half:QK_COLS]).astype(BF16)
    v_ref[...] = _dot(ub, w_ref[:, QK_COLS:QK_COLS + V_COLS]).astype(BF16)
    o_ref[...] = _dot(ub, w_ref[:, QK_COLS + V_COLS:QK_COLS + 2 * V_COLS]).astype(BF16)
    g0 = QK_COLS + 2 * V_COLS
    lane = lax.broadcasted_iota(jnp.int32, (1, LANES), 1)
    live = lane < M_HEADS
    gi = _dot(ub, w_ref[:, g0:g0 + LANES]) + bli_ref[...]
    gf = _dot(ub, w_ref[:, g0 + LANES:g0 + 2 * LANES]) + blf_ref[...]
    li = GATE_SOFTCAP * jnp.tanh(gi / GATE_SOFTCAP)
    fpre = GATE_SOFTCAP * jnp.tanh(gf / GATE_SOFTCAP)
    lf = jnp.minimum(fpre, 0.0) - jnp.log1p(jnp.exp(-jnp.abs(fpre)))
    li_ref[...] = jnp.where(live, li, 0.0)
    lf_ref[...] = jnp.where(live, lf, 0.0)


def _inproj_call(x, mod, mspec, nw, w, bli, blf, tm):
    n = x.shape[0]
    row = lambda i: (i, 0)
    const = lambda i: (0, 0)
    return pl.pallas_call(
        _inproj_kernel,
        out_shape=(jax.ShapeDtypeStruct((n, QK_COLS), BF16),
                   jax.ShapeDtypeStruct((n, V_COLS), BF16),
                   jax.ShapeDtypeStruct((n, V_COLS), BF16),
                   jax.ShapeDtypeStruct((n, LANES), F32),
                   jax.ShapeDtypeStruct((n, LANES), F32)),
        grid=(n // tm,),
        in_specs=[pl.BlockSpec((tm, D_MODEL), row), mspec(0), mspec(1),
                  pl.BlockSpec((1, D_MODEL), const),
                  pl.BlockSpec((D_MODEL, IN_COLS), const),
                  pl.BlockSpec((1, LANES), const), pl.BlockSpec((1, LANES), const)],
        out_specs=(pl.BlockSpec((tm, QK_COLS), row), pl.BlockSpec((tm, V_COLS), row),
                   pl.BlockSpec((tm, V_COLS), row), pl.BlockSpec((tm, LANES), row),
                   pl.BlockSpec((tm, LANES), row)),
        compiler_params=_params(1),
        name="mlstm_inproj",
    )(x, mod, mod, nw, w, bli, blf)


def _mlstm_kernel(L, qk_ref, v_ref, li_ref, lf_ref, c0_ref, n0_ref, m0_ref,
                  h_ref, cout_ref, nout_ref, mout_ref, c2_s, n_s, m_s):
    c = pl.program_id(1)
    nc = pl.num_programs(1)
    npairs = M_HEADS // 2
    hd = M_DK

    lane = lax.broadcasted_iota(jnp.int32, (1, LANES), 1)
    lo128 = lane < hd
    lane256 = lax.broadcasted_iota(jnp.int32, (1, 2 * LANES), 1)
    lo256 = lane256 < LANES
    row128 = lax.broadcasted_iota(jnp.int32, (LANES, 1), 0)
    top = row128 < hd
    blockdiag = (top & lo256) | (jnp.logical_not(top) & jnp.logical_not(lo256))

    @pl.when(c == 0)
    def _():
        m_s[...] = m0_ref[...]
        n_s[...] = n0_ref[...]
        z = jnp.zeros((M_DK, M_DV), F32)
        for j in range(npairs):
            ce = c0_ref[2 * j]
            co = c0_ref[2 * j + 1]
            c2_s[j] = jnp.concatenate([jnp.concatenate([ce, z], 1),
                                       jnp.concatenate([z, co], 1)], 0)

    LI = li_ref[...]
    LF = lf_ref[...]
    rowL = lax.broadcasted_iota(jnp.int32, (L, LANES), 0)

    def prefix(x, op, ident):
        d = 1
        while d < L:
            shifted = pltpu.roll(x, d, axis=0)
            x = op(x, jnp.where(rowL >= d, shifted, ident))
            d *= 2
        return x

    Bc = prefix(LF, jnp.add, 0.0)
    Cm = LI - Bc
    mprev = m_s[...]
    Gc = jnp.maximum(mprev, prefix(Cm, jnp.maximum, -jnp.inf))
    A = jnp.exp(mprev - Gc)
    EXPM = jnp.exp(-(Bc + Gc))
    bL = Bc[L - 1:L, :]
    DL = bL + Cm
    mnew = jnp.maximum(bL + mprev, jnp.max(DL, axis=0, keepdims=True))
    ast = jnp.exp(bL + mprev - mnew)
    WST = jnp.exp(DL - mnew)

    def pad_rows(x, rows):
        if x.shape[0] == rows:
            return x
        return jnp.concatenate([x, jnp.zeros((rows - x.shape[0], x.shape[1]), x.dtype)], 0)

    cm_pad = pad_rows(Cm, hd)
    XT = jnp.concatenate([cm_pad, cm_pad], 0).T
    causal = (lane & (hd - 1)) <= rowL
    J = jnp.where(blockdiag, 1.0, 0.0).astype(BF16)

    def bc(X, h):
        return jnp.broadcast_to(X[:, h:h + 1], X.shape)

    for j in range(npairs):
        he, ho = 2 * j, 2 * j + 1

        def pair128(X):
            return jnp.where(lo128, bc(X, he), bc(X, ho))

        def pair256(X):
            return jnp.concatenate([bc(X, he), bc(X, ho)], 1)

        q128 = qk_ref[:, LANES * j:LANES * (j + 1)]
        k128 = qk_ref[:, QK_COLS // 2 + LANES * j:QK_COLS // 2 + LANES * (j + 1)]
        v256 = v_ref[:, 2 * LANES * j:2 * LANES * (j + 1)]
        zk = jnp.zeros_like(k128)
        zv = jnp.zeros_like(v256)
        K2t = jnp.concatenate([pad_rows(jnp.where(lo128, k128, zk), hd),
                               pad_rows(jnp.where(lo128, zk, k128), hd)], 0)
        V2 = jnp.concatenate([pad_rows(jnp.where(lo256, v256, zv), hd),
                              pad_rows(jnp.where(lo256, zv, v256), hd)], 0)
        S = _dot_nt(q128, K2t)
        crow = jnp.where(lo128, XT[he:he + 1, :], XT[ho:ho + 1, :])
        arg = jnp.where(causal, crow - pair128(Gc), -jnp.inf)
        Sw = (S * jnp.exp(arg)).astype(BF16)
        num_intra = _dot(Sw, V2)
        rowsum = _dot(Sw, J)
        C2 = c2_s[j]
        inter = _dot(q128, C2.astype(BF16))
        npair = n_s[j:j + 1, :]
        Nrow2 = jnp.concatenate(
            [jnp.broadcast_to(jnp.where(lo128, npair, 0.0), (LANES, LANES)),
             jnp.broadcast_to(jnp.where(lo128, 0.0, npair), (LANES, LANES))], 0).astype(BF16)
        qn = _dot_nt(q128, Nrow2)
        a256 = pair256(A)
        num = a256 * inter + num_intra
        den = a256 * qn + rowsum
        h = num / jnp.maximum(jnp.abs(den), pair256(EXPM))
        h_ref[:, 2 * LANES * j:2 * LANES * (j + 1)] = h.astype(h_ref.dtype)

        kw = k128.astype(F32) * pair128(WST)
        n_s[j:j + 1, :] = pair128(ast) * npair + jnp.sum(kw, axis=0, keepdims=True)
        kwT = pad_rows(kw, LANES).T.astype(BF16)
        dC = _dot(kwT, pad_rows(v256, LANES))
        c2_s[j] = pair256(ast) * C2 + jnp.where(blockdiag, dC, 0.0)

    m_s[...] = mnew

    @pl.when(c == nc - 1)
    def _():
        for j in range(npairs):
            C2 = c2_s[j]
            cout_ref[2 * j] = C2[0:M_DK, 0:M_DV]
            cout_ref[2 * j + 1] = C2[M_DK:2 * M_DK, M_DV:2 * M_DV]
        nout_ref[...] = n_s[...]
        mout_ref[...] = mnew


def _mlstm_call(qk, v, li, lf, c0, n0, m0, B, nc, L):
    n = qk.shape[0]
    npairs = M_HEADS // 2
    tok = lambda b, c: (b * nc + c, 0)
    st4 = lambda b, c: (b, 0, 0, 0)
    st3 = lambda b, c: (b, 0, 0)
    return pl.pallas_call(
        functools.partial(_mlstm_kernel, L),
        out_shape=(jax.ShapeDtypeStruct((n, V_COLS), BF16),
                   jax.ShapeDtypeStruct((B, M_HEADS, M_DK, M_DV), F32),
                   jax.ShapeDtypeStruct((B, npairs, LANES), F32),
                   jax.ShapeDtypeStruct((B, 1, LANES), F32)),
        grid=(B, nc),
        in_specs=[pl.BlockSpec((L, QK_COLS), tok), pl.BlockSpec((L, V_COLS), tok),
                  pl.BlockSpec((L, LANES), tok), pl.BlockSpec((L, LANES), tok),
                  pl.BlockSpec((None, M_HEADS, M_DK, M_DV), st4),
                  pl.BlockSpec((None, npairs, LANES), st3),
                  pl.BlockSpec((None, 1, LANES), st3)],
        out_specs=(pl.BlockSpec((L, V_COLS), tok),
                   pl.BlockSpec((None, M_HEADS, M_DK, M_DV), st4),
                   pl.BlockSpec((None, npairs, LANES), st3),
                   pl.BlockSpec((None, 1, LANES), st3)),
        scratch_shapes=[pltpu.VMEM((npairs, 2 * M_DK, 2 * M_DV), F32),
                        pltpu.VMEM((npairs, LANES), F32),
                        pltpu.VMEM((1, LANES), F32)],
        compiler_params=_params(2),
        name="mlstm_chunks",
    )(qk, v, li, lf, c0, n0, m0)


def _route(lg):
    lane = lax.broadcasted_iota(jnp.int32, lg.shape, 1)
    lanef = lane.astype(F32)
    neg = -jnp.inf
    far = float(LANES)
    gm = lane < N_GROUPS
    lgm = jnp.where(gm, lg, neg)
    gmax = jnp.max(lgm, axis=-1, keepdims=True)
    gsum = jnp.sum(jnp.exp(lgm - gmax), axis=-1, keepdims=True)
    g_w = 1.0 / gsum
    gidx = jnp.min(jnp.where(gm & (lg == gmax), lanef, far), axis=-1, keepdims=True)
    first = N_GROUPS + EXPERTS_PER_GROUP * gidx
    sel = (lanef >= first) & (lanef < first + EXPERTS_PER_GROUP)
    l1 = jnp.max(jnp.where(sel, lg, neg), axis=-1, keepdims=True)
    i1 = jnp.min(jnp.where(sel & (lg == l1), lanef, far), axis=-1, keepdims=True)
    sel2 = sel & (lanef != i1)
    l2 = jnp.max(jnp.where(sel2, lg, neg), axis=-1, keepdims=True)
    i2 = jnp.min(jnp.where(sel2 & (lg == l2), lanef, far), axis=-1, keepdims=True)
    r = jnp.exp(l2 - l1)
    w1 = g_w / (1.0 + r)
    w2 = w1 * r
    return jnp.where(lanef == i1, w1, jnp.where(lanef == i2, w2, 0.0)), gidx


def _post_kernel(mlstm, routed, *refs):
    refs = list(refs)
    h_ref = refs.pop(0)
    o_ref = refs.pop(0) if mlstm else None
    x_ref, g1_ref, sh2_ref, sc2_ref, nf_ref = refs[:5]
    refs = refs[5:]
    hn_ref = refs.pop(0) if mlstm else None
    wout_ref, wrh_ref, wrl_ref, br_ref = refs[:4]
    refs = refs[4:]
    tri_ref = refs.pop(0) if routed else None
    x1_ref, u2_ref, gate_ref = refs[:3]
    outs = refs[3:]
    if mlstm:
        hf = h_ref[...].astype(F32)
        parts = [_unit_rms(hf[:, M_DV * i:M_DV * (i + 1)]) for i in range(M_HEADS)]
        hn = jnp.concatenate(parts, 1) * hn_ref[...]
        hg = (hn * jax.nn.sigmoid(o_ref[...].astype(F32))).astype(BF16)
    else:
        hg = h_ref[...]
    x1 = x_ref[...] + g1_ref[...] * _dot(hg, wout_ref[...])
    x1_ref[...] = x1
    u2 = _unit_rms(x1) * nf_ref[...] * (1.0 + sc2_ref[...]) + sh2_ref[...]
    uh, ul = _split2(u2)
    lg = _dot(uh, wrh_ref[...]) + _dot(ul, wrh_ref[...]) + _dot(uh, wrl_ref[...]) + br_ref[...]
    gate, gidx = _route(lg)
    u2_ref[...] = uh
    gate_ref[...] = gate
    if not routed:
        return
    gid_ref, col_ref = outs
    rows = []
    for blk in range(x1.shape[0] // LANES):
        col = jnp.broadcast_to(gidx[LANES * blk:LANES * (blk + 1), :], (LANES, LANES))
        rows.append(col.T[0:1, :])
    gid_ref[...] = jnp.concatenate(rows, 1)
    lanef = lax.broadcasted_iota(jnp.int32, gate.shape, 1).astype(F32)
    onehot = jnp.where(lanef == gidx, 1.0, 0.0)
    earlier = _dot(tri_ref[...], onehot.astype(BF16))
    rank = jnp.sum(onehot * earlier, axis=-1, keepdims=True)
    col_ref[...] = jnp.where(lanef == 0.0, gidx, jnp.where(lanef == 1.0, rank, 0.0))


def _post_call(mlstm, routed, h, o, x, mod, mspec, nf, hn, wout, wrh, wrl, br, tm):
    n = x.shape[0]
    row = lambda i: (i, 0)
    const = lambda i: (0, 0)
    tok = pl.BlockSpec((tm, D_MODEL), row)
    small = pl.BlockSpec((tm, LANES), row)
    vec = pl.BlockSpec((1, D_MODEL), const)
    ins = [h] + ([o] if mlstm else []) + [x, mod, mod, mod, nf] + ([hn] if mlstm else []) \
        + [wout, wrh, wrl, br]
    specs = [tok] + ([tok] if mlstm else []) + [tok, mspec(2), mspec(3), mspec(4), vec] \
        + ([vec] if mlstm else []) \
        + [pl.BlockSpec((D_MODEL, D_MODEL), const), pl.BlockSpec((D_MODEL, LANES), const),
           pl.BlockSpec((D_MODEL, LANES), const), pl.BlockSpec((1, LANES), const)]
    out_shape = [jax.ShapeDtypeStruct((n, D_MODEL), F32), jax.ShapeDtypeStruct((n, D_MODEL), BF16),
                 jax.ShapeDtypeStruct((n, LANES), F32)]
    out_specs = [tok, tok, small]
    if routed:
        t = jnp.arange(tm)
        ins.append((t[None, :] < t[:, None]).astype(BF16))
        specs.append(pl.BlockSpec((tm, tm), const))
        out_shape += [jax.ShapeDtypeStruct((n // tm, 1, tm), F32), jax.ShapeDtypeStruct((n, LANES), F32)]
        out_specs += [pl.BlockSpec((None, 1, tm), lambda i: (i, 0, 0)), small]
    out_shape, out_specs = tuple(out_shape), tuple(out_specs)
    return pl.pallas_call(
        functools.partial(_post_kernel, mlstm, routed),
        out_shape=out_shape,
        grid=(n // tm,),
        in_specs=specs,
        out_specs=out_specs,
        compiler_params=_params(1),
        name="post_mlstm" if mlstm else "post_attn",
    )(*ins)


ITEM_ROWS = 8
RUN_ALIGN = 16
RUN_SIZES = (512, 256, 128, 64, 32, 16)
RUN_SRC, RUN_DST, RUN_LEN, RUN_TOTAL = 0, N_GROUPS, 2 * N_GROUPS, 3 * N_GROUPS


def _stage_rows(tm):
    assert N_GROUPS * (RUN_ALIGN - 1) <= LANES
    return tm + LANES


def _sorted_rows(n, tm):
    rows = n + N_GROUPS * (n // tm) * RUN_ALIGN + tm
    return -(-rows // tm) * tm


def _sort_meta_kernel(tm, nts, gid_ref, q_ref, runs_ref, items_ref):
    ntiles = gid_ref.shape[0]
    gid = gid_ref[...]
    r = lax.broadcasted_iota(jnp.int32, (tm, tm), 0)
    c = lax.broadcasted_iota(jnp.int32, (tm, tm), 1)
    before = jnp.where(r < c, 1.0, 0.0).astype(BF16)
    trow = lax.broadcasted_iota(jnp.int32, (ntiles, LANES), 0)
    lane = lax.broadcasted_iota(jnp.int32, (ntiles, LANES), 1)
    k = lax.broadcasted_iota(jnp.int32, (1, LANES), 1).astype(F32)
    zero11 = jnp.zeros((1, 1), F32)
    q = jnp.zeros((ntiles, tm), F32)
    runs = jnp.zeros((ntiles, LANES), F32)
    src = jnp.zeros((ntiles, 1), F32)
    start, nitems = zero11, zero11
    grp = jnp.zeros((1, LANES), F32)
    tile = jnp.zeros((1, LANES), F32)
    valid = jnp.zeros((1, LANES), F32)
    for g in range(N_GROUPS):
        mask = jnp.where(gid == float(g), 1.0, 0.0)
        cnt = jnp.sum(mask, axis=1, keepdims=True)
        padded = jnp.floor((cnt + (RUN_ALIGN - 1.0)) * (1.0 / RUN_ALIGN)) * RUN_ALIGN
        incl = jnp.broadcast_to(padded, (ntiles, LANES))
        d = 1
        while d < ntiles:
            incl = incl + jnp.where(trow >= d, pltpu.roll(incl, d, axis=0), 0.0)
            d *= 2
        total = incl[ntiles - 1:ntiles, 0:1]
        dst = start + incl[:, 0:1] - padded
        q = q + mask * (src + _dot(mask.astype(BF16), before))
        runs = runs + jnp.where(lane == RUN_SRC + g, src, 0.0) + jnp.where(lane == RUN_DST + g, dst, 0.0) \
            + jnp.where(lane == RUN_LEN + g, padded, 0.0)
        src = src + padded
        end = start + total
        ft = jnp.floor(start * (1.0 / tm))
        lt = jnp.floor((end - 1.0) * (1.0 / tm))
        ni = jnp.where(total > 0.0, lt - ft + 1.0, 0.0)
        inside = (k >= nitems) & (k < nitems + ni)
        grp = grp + jnp.where(inside, float(g), 0.0)
        tile = tile + jnp.where(inside, ft + (k - nitems), 0.0)
        valid = valid + jnp.where(inside, 1.0, 0.0)
        nitems = nitems + ni
        start = end
    runs = runs + jnp.where(lane == RUN_TOTAL, start, 0.0)
    live = valid > 0.0
    prev = pltpu.roll(tile, 1, axis=1)
    nxt = pltpu.roll(tile, LANES - 1, axis=1)
    first = live & ((k == 0.0) | (tile != prev))
    last = live & ((k == nitems - 1.0) | (tile != nxt))
    spare = jnp.floor((start - 1.0) * (1.0 / tm)) + 1.0 + (k - nitems)
    fill = jnp.logical_not(live) & (spare <= nts - 1.0)
    tile = jnp.where(live, tile, jnp.minimum(spare, nts - 1.0))
    grp = jnp.where(live, grp, jnp.max(grp, axis=1, keepdims=True))
    flag = lambda m: jnp.where(m, 1.0, 0.0)
    q_ref[...] = q
    runs_ref[...] = runs.astype(jnp.int32)
    table = jnp.concatenate([tile, grp, valid, flag(first | fill), flag(last | fill),
                             jnp.zeros((ITEM_ROWS - 5, LANES), F32)], 0)
    items_ref[...] = table.astype(jnp.int32)


def _sort_meta_call(gid, tm):
    ntiles = gid.shape[0]
    nts = _sorted_rows(ntiles * tm, tm) // tm
    assert nts + N_GROUPS - 1 <= LANES and tm <= RUN_SIZES[0]
    return pl.pallas_call(
        functools.partial(_sort_meta_kernel, tm, nts),
        out_shape=(jax.ShapeDtypeStruct((ntiles, tm), F32),
                   jax.ShapeDtypeStruct((ntiles, LANES), jnp.int32),
                   jax.ShapeDtypeStruct((ITEM_ROWS, LANES), jnp.int32)),
        compiler_params=_params(0),
        name="moe_sort_meta",
    )(gid)


def _run_copies(runs_ref, i, tile_ref, sorted_hbm, sem, to_sorted):
    pieces = []
    for g in range(N_GROUPS):
        src = runs_ref[i, RUN_SRC + g]
        dst = runs_ref[i, RUN_DST + g]
        length = runs_ref[i, RUN_LEN + g]
        for s in RUN_SIZES:
            def build(src=src, dst=dst, length=length, s=s):
                off = length & (-2 * s)
                a = tile_ref.at[pl.ds(pl.multiple_of(src + off, RUN_ALIGN), s), :]
                b = sorted_hbm.at[pl.ds(pl.multiple_of(dst + off, RUN_ALIGN), s), :]
                return pltpu.make_async_copy(a, b, sem) if to_sorted else pltpu.make_async_copy(b, a, sem)
            pieces.append(((length & s) != 0, build))
    return pieces


def _start(pieces):
    for pred, build in pieces:
        pl.when(pred)(lambda build=build: build().start())


def _wait(pieces):
    for pred, build in pieces:
        pl.when(pred)(lambda build=build: build().wait())


def _start_then_wait(pieces):
    _start(pieces)
    _wait(pieces)


def _scatter_kernel(tm, nts, ntiles, runs_ref, q_ref, u_ref, g_ref, su_hbm, sg_hbm,
                    stu, stg, sem_u, sem_g):
    i = pl.program_id(0)
    slot = i % 2

    def copies(tile, slot):
        return (_run_copies(runs_ref, tile, stu.at[slot], su_hbm, sem_u.at[slot], True)
                + _run_copies(runs_ref, tile, stg.at[slot], sg_hbm, sem_g.at[slot], True))

    r = lax.broadcasted_iota(jnp.int32, (_stage_rows(tm), tm), 0).astype(F32)
    perm = jnp.where(q_ref[...] == r, 1.0, 0.0).astype(BF16)
    stu[slot] = _dot(perm, u_ref[...]).astype(BF16)
    g3 = _split3(g_ref[...])
    stg[slot] = _dot(perm, g3[0]) + _dot(perm, g3[1]) + _dot(perm, g3[2])
    _start(copies(i, slot))

    @pl.when(i > 0)
    def _():
        _wait(copies(i - 1, 1 - slot))

    @pl.when(i == ntiles - 1)
    def _():
        _wait(copies(i, slot))
        stu[slot] = jnp.zeros(stu.shape[1:], stu.dtype)
        stg[slot] = jnp.zeros(stg.shape[1:], stg.dtype)
        total = runs_ref[i, RUN_TOTAL]
        tail = nts * tm - total
        nfull = tail // tm
        pieces = []
        for tile_ref, hbm, sem in ((stu.at[slot], su_hbm, sem_u.at[slot]),
                                   (stg.at[slot], sg_hbm, sem_g.at[slot])):
            for j in range(nts - ntiles):
                def full(j=j, tile_ref=tile_ref, hbm=hbm, sem=sem):
                    dst = pl.multiple_of(total + j * tm, RUN_ALIGN)
                    return pltpu.make_async_copy(tile_ref.at[pl.ds(0, tm), :], hbm.at[pl.ds(dst, tm), :], sem)
                pieces.append((j < nfull, full))
            rem = tail - nfull * tm
            for s in RUN_SIZES:
                if s >= tm:
                    continue
                def part(s=s, tile_ref=tile_ref, hbm=hbm, sem=sem):
                    dst = pl.multiple_of(total + nfull * tm + (rem & (-2 * s)), RUN_ALIGN)
                    return pltpu.make_async_copy(tile_ref.at[pl.ds(0, s), :], hbm.at[pl.ds(dst, s), :], sem)
                pieces.append(((rem & s) != 0, part))
        _start_then_wait(pieces)


def _scatter_call(runs, q3, u2, gate, tm):
    n = u2.shape[0]
    ns = _sorted_rows(n, tm)
    grid_spec = pltpu.PrefetchScalarGridSpec(
        num_scalar_prefetch=1,
        grid=(n // tm,),
        in_specs=[pl.BlockSpec((None, 1, tm), lambda i, *_: (i, 0, 0)),
                  pl.BlockSpec((tm, D_MODEL), lambda i, *_: (i, 0)),
                  pl.BlockSpec((tm, LANES), lambda i, *_: (i, 0))],
        out_specs=(pl.BlockSpec(memory_space=pl.ANY), pl.BlockSpec(memory_space=pl.ANY)),
        scratch_shapes=[pltpu.VMEM((2, _stage_rows(tm), D_MODEL), BF16),
                        pltpu.VMEM((2, _stage_rows(tm), LANES), F32),
                        pltpu.SemaphoreType.DMA((2,)), pltpu.SemaphoreType.DMA((2,))])
    return pl.pallas_call(
        functools.partial(_scatter_kernel, tm, ns // tm, n // tm),
        out_shape=(jax.ShapeDtypeStruct((ns, D_MODEL), BF16), jax.ShapeDtypeStruct((ns, LANES), F32)),
        grid_spec=grid_spec,
        compiler_params=_params(1),
        name="moe_scatter",
    )(runs, q3, u2, gate)


def _ungroup(tm, runs_ref, col_ref, ys_hbm, stage, sem):
    i = pl.program_id(0)
    slot = i % 2

    def copies(tile, slot):
        return _run_copies(runs_ref, tile, stage.at[slot], ys_hbm, sem.at[slot], False)

    @pl.when(i == 0)
    def _():
        stage[...] = jnp.zeros_like(stage)
        _start(copies(i, slot))

    @pl.when(i + 1 < pl.num_programs(0))
    def _():
        _start(copies(i + 1, 1 - slot))

    _wait(copies(i, slot))
    col = col_ref[...]
    gidx = col[:, 0:1]
    src = [runs_ref[i, RUN_SRC + g].astype(F32) for g in range(N_GROUPS)]
    first = src[N_GROUPS - 1]
    for g in range(N_GROUPS - 2, -1, -1):
        first = jnp.where(gidx == float(g), src[g], first)
    q = first + col[:, 1:2]
    lanes = lax.broadcasted_iota(jnp.int32, (tm, _stage_rows(tm)), 1).astype(F32)
    perm_t = jnp.where(q == lanes, 1.0, 0.0).astype(BF16)
    return _dot(perm_t, stage[slot])


def _moe_routed_kernel(items_ref, su_ref, sg_ref, wgu_ref, wd_ref, out_ref, acc_ref):
    kk = pl.program_id(0)
    grp = items_ref[1, kk]

    @pl.when(items_ref[3, kk] == 1)
    def _():
        acc_ref[...] = jnp.zeros_like(acc_ref)

    @pl.when(items_ref[2, kk] == 1)
    def _():
        u = su_ref[...]
        gate = sg_ref[...]
        lane = lax.broadcasted_iota(jnp.int32, gate.shape, 1)
        first_lane = N_GROUPS + EXPERTS_PER_GROUP * grp
        for e in range(EXPERTS_PER_GROUP):
            gcol = jnp.sum(jnp.where(lane == first_lane + e, gate, 0.0), axis=-1, keepdims=True)
            hu = _dot(u, wgu_ref[e])
            hg = hu[:, 0:D_EXPERT]
            act = hg * jax.nn.sigmoid(hg) * hu[:, D_EXPERT:2 * D_EXPERT] * gcol
            acc_ref[...] += _dot(act.astype(BF16), wd_ref[e])

    @pl.when(items_ref[4, kk] == 1)
    def _():
        out_ref[...] = acc_ref[...].astype(out_ref.dtype)


def _moe_routed_call(items, su, sg, wgu, wd, tm):
    ns = su.shape[0]
    n_items = ns // tm + N_GROUPS - 1
    wgu4 = wgu.reshape(N_GROUPS, EXPERTS_PER_GROUP, D_MODEL, 2 * D_EXPERT)
    wd4 = wd.reshape(N_GROUPS, EXPERTS_PER_GROUP, D_EXPERT, D_MODEL)
    grid_spec = pltpu.PrefetchScalarGridSpec(
        num_scalar_prefetch=1,
        grid=(n_items,),
        in_specs=[pl.BlockSpec((tm, D_MODEL), lambda k, it: (it[0, k], 0)),
                  pl.BlockSpec((tm, LANES), lambda k, it: (it[0, k], 0)),
                  pl.BlockSpec((None, EXPERTS_PER_GROUP, D_MODEL, 2 * D_EXPERT),
                               lambda k, it: (it[1, k], 0, 0, 0)),
                  pl.BlockSpec((None, EXPERTS_PER_GROUP, D_EXPERT, D_MODEL),
                               lambda k, it: (it[1, k], 0, 0, 0))],
        out_specs=pl.BlockSpec((tm, D_MODEL), lambda k, it: (it[0, k], 0)),
        scratch_shapes=[pltpu.VMEM((tm, D_MODEL), F32)])
    return pl.pallas_call(
        _moe_routed_kernel,
        out_shape=jax.ShapeDtypeStruct((ns, D_MODEL), BF16),
        grid_spec=grid_spec,
        compiler_params=_params(1),
        name="moe_routed",
    )(items, su, sg, wgu4, wd4)


def _resid_gather_kernel(tm, runs_ref, x_ref, g2_ref, col_ref, ys_hbm, y_ref, stage, sem):
    y_ref[...] = x_ref[...] + g2_ref[...] * _ungroup(tm, runs_ref, col_ref, ys_hbm, stage, sem)


def _resid_gather_call(x, runs, col, ys, mod, mspec, tm):
    n = x.shape[0]
    row = lambda i, *_: (i, 0)
    grid_spec = pltpu.PrefetchScalarGridSpec(
        num_scalar_prefetch=1,
        grid=(n // tm,),
        in_specs=[pl.BlockSpec((tm, D_MODEL), row), mspec(5), pl.BlockSpec((tm, LANES), row),
                  pl.BlockSpec(memory_space=pl.ANY)],
        out_specs=pl.BlockSpec((tm, D_MODEL), row),
        scratch_shapes=[pltpu.VMEM((2, _stage_rows(tm), D_MODEL), BF16), pltpu.SemaphoreType.DMA((2,))])
    return pl.pallas_call(
        functools.partial(_resid_gather_kernel, tm),
        out_shape=jax.ShapeDtypeStruct((n, D_MODEL), F32),
        grid_spec=grid_spec,
        compiler_params=_params(1),
        name="moe_unsort_resid",
    )(runs, x, mod, col, ys)


def _moe_kernel(u_ref, gate_ref, wgu_ref, wd_ref, x1_ref, g2_ref, y_ref, acc_ref):
    e = pl.program_id(1)

    @pl.when(e == 0)
    def _():
        acc_ref[...] = jnp.zeros_like(acc_ref)

    hu = _dot(u_ref[...], wgu_ref[...])
    hg = hu[:, 0:D_EXPERT]
    hv = hu[:, D_EXPERT:2 * D_EXPERT]
    gate = gate_ref[...]
    lane = lax.broadcasted_iota(jnp.int32, gate.shape, 1)
    gcol = jnp.sum(jnp.where(lane == N_GROUPS + e, gate, 0.0), axis=-1, keepdims=True)
    act = hg * jax.nn.sigmoid(hg) * hv * gcol
    acc_ref[...] += _dot(act.astype(BF16), wd_ref[...])

    @pl.when(e == N_EXPERTS - 1)
    def _():
        y_ref[...] = x1_ref[...] + g2_ref[...] * acc_ref[...]


def _moe_call(u2, gate, wgu, wd, x1, mod, mspec, tm):
    n = u2.shape[0]
    row = lambda i, e: (i, 0)
    return pl.pallas_call(
        _moe_kernel,
        out_shape=jax.ShapeDtypeStruct((n, D_MODEL), F32),
        grid=(n // tm, N_EXPERTS),
        in_specs=[pl.BlockSpec((tm, D_MODEL), row), pl.BlockSpec((tm, LANES), row),
                  pl.BlockSpec((None, D_MODEL, 2 * D_EXPERT), lambda i, e: (e, 0, 0)),
                  pl.BlockSpec((None, D_EXPERT, D_MODEL), lambda i, e: (e, 0, 0)),
                  pl.BlockSpec((tm, D_MODEL), row), mspec(5)],
        out_specs=pl.BlockSpec((tm, D_MODEL), row),
        scratch_shapes=[pltpu.VMEM((tm, D_MODEL), F32)],
        compiler_params=_params(2),
        name="moe_dense",
    )(u2, gate, wgu, wd, x1, mod)


def _rope128(x, cos, sa, sb):
    return x * cos + pltpu.roll(x, LANES - ROT_DIM // 2, axis=1) * sa \
        + pltpu.roll(x, ROT_DIM // 2, axis=1) * sb


def _kvq_kernel(gather_tm, *refs):
    if gather_tm:
        (runs_ref, x_ref, g2_ref, col_ref, ys_hbm, kvsh_ref, kvsc_ref, sh1_ref, sc1_ref, kvn_ref,
         nm_ref, wkv_ref, wq_ref, kn_ref, qn_ref, cos_ref, sa_ref, sb_ref, g64_ref, gr_ref, gb_ref,
         q_ref, k_ref, v_ref, xa_ref, stage, sem) = refs
        x = x_ref[...] + g2_ref[...] * _ungroup(gather_tm, runs_ref, col_ref, ys_hbm, stage, sem)
        xa_ref[...] = x
    else:
        (x_ref, kvsh_ref, kvsc_ref, sh1_ref, sc1_ref, kvn_ref, nm_ref,
         wkv_ref, wq_ref, kn_ref, qn_ref, cos_ref, sa_ref, sb_ref, g64_ref, gr_ref, gb_ref,
         q_ref, k_ref, v_ref) = refs
        x = x_ref[...]
    xn = _unit_rms(x)
    cos, sa, sb = cos_ref[...], sa_ref[...], sb_ref[...]

    ukv = xn * kvn_ref[...] * (1.0 + kvsc_ref[...]) + kvsh_ref[...]
    kv = _dot(ukv.astype(BF16), wkv_ref[...])
    k = kv[:, 0:LANES]
    v_ref[...] = kv[:, LANES:2 * LANES]
    kh, kl = _split2(k * k)
    ms = _dot(kh, g64_ref[...]) + _dot(kl, g64_ref[...])
    k_ref[...] = _rope128(k * lax.rsqrt(ms + RMS_EPS) * kn_ref[...], cos, sa, sb)

    u1 = xn * nm_ref[...] * (1.0 + sc1_ref[...]) + sh1_ref[...]
    q = _dot(u1.astype(BF16), wq_ref[...])
    qh, ql = _split2(q * q)
    ms16 = _dot(qh, gr_ref[...]) + _dot(ql, gr_ref[...])
    r3 = _split3(lax.rsqrt(ms16 + RMS_EPS))
    rsb = _dot(r3[0], gb_ref[...]) + _dot(r3[1], gb_ref[...]) + _dot(r3[2], gb_ref[...])
    qn = q * rsb * qn_ref[...]
    for i in range(D_MODEL // LANES):
        sl = slice(LANES * i, LANES * (i + 1))
        q_ref[:, sl] = (_rope128(qn[:, sl], cos, sa, sb) * (HEAD_DIM ** -0.5)).astype(BF16)


def _kvq_call(x, routed, mod0, mod1, mspec, kvmod, kvspec, kvn, nm, wkv, wq, kn, qn, tabs, tab,
              g64, gr, gb, tm):
    n = x.shape[0]
    row = lambda i, *_: (i, 0)
    const = lambda i, *_: (0, 0)
    vec = pl.BlockSpec((1, D_MODEL), const)
    tok = pl.BlockSpec((tm, D_MODEL), row)
    small = pl.BlockSpec((tm, LANES), row)
    ins = [x, kvmod, kvmod, mod1, mod1, kvn, nm, wkv, wq, kn, qn, *tabs, g64, gr, gb]
    specs = [tok, kvspec(0), kvspec(1), mspec(0), mspec(1), vec, vec,
             pl.BlockSpec((D_MODEL, 2 * LANES), const), pl.BlockSpec((D_MODEL, D_MODEL), const),
             pl.BlockSpec((1, LANES), const), vec, tab, tab, tab,
             pl.BlockSpec((LANES, LANES), const), pl.BlockSpec((D_MODEL, LANES), const),
             pl.BlockSpec((LANES, D_MODEL), const)]
    out_shape = [jax.ShapeDtypeStruct((n, D_MODEL), BF16), jax.ShapeDtypeStruct((n, LANES), F32),
                 jax.ShapeDtypeStruct((n, LANES), F32)]
    out_specs = [tok, small, small]
    scratch = []
    nprefetch = 0
    if routed is not None:
        runs, col, ys = routed
        nprefetch = 1
        ins = [runs, x, mod0, col, ys] + ins[1:]
        specs = [tok, mspec(5), small, pl.BlockSpec(memory_space=pl.ANY)] + specs[1:]
        out_shape.append(jax.ShapeDtypeStruct((n, D_MODEL), F32))
        out_specs.append(tok)
        scratch = [pltpu.VMEM((2, _stage_rows(tm), D_MODEL), BF16), pltpu.SemaphoreType.DMA((2,))]
    grid_spec = pltpu.PrefetchScalarGridSpec(
        num_scalar_prefetch=nprefetch, grid=(n // tm,), in_specs=specs,
        out_specs=tuple(out_specs), scratch_shapes=scratch)
    return pl.pallas_call(
        functools.partial(_kvq_kernel, tm if routed is not None else 0),
        out_shape=tuple(out_shape),
        grid_spec=grid_spec,
        compiler_params=_params(1),
        name="kv_q_proj",
    )(*ins)


def _attn_core(q, kcat, vcat, sinks_ref, first_block, o_ref):
    tq = q.shape[0]
    tk = kcat.shape[0]
    pairs = N_Q_HEADS // N_KV_HEADS // 2
    rows = pairs * tq
    lane = lax.broadcasted_iota(jnp.int32, (1, LANES), 1)
    lo = lane < HEAD_DIM
    kro = pltpu.roll(kcat, HEAD_DIM, axis=1)
    vro = pltpu.roll(vcat, HEAD_DIM, axis=1)
    kj = lax.broadcasted_iota(jnp.int32, (rows, tk), 1)
    qi = lax.broadcasted_iota(jnp.int32, (rows, tk), 0) & (tq - 1)
    valid = (kj > qi) & (kj <= qi + WINDOW)
    if first_block is not None:
        valid = valid & ((kj >= WINDOW) | jnp.logical_not(first_block))
    for g in range(N_KV_HEADS):
        if g == 0:
            ke, ko = jnp.where(lo, kcat, 0.0), jnp.where(lo, 0.0, kro)
            ve, vo = jnp.where(lo, vcat, 0.0), jnp.where(lo, 0.0, vro)
        else:
            ke, ko = jnp.where(lo, kro, 0.0), jnp.where(lo, 0.0, kcat)
            ve, vo = jnp.where(lo, vro, 0.0), jnp.where(lo, 0.0, vcat)
        k2 = jnp.concatenate([ke, ko], 0).astype(BF16)
        v2 = jnp.concatenate([ve, vo], 0).astype(BF16)
        base = g * pairs
        qs = jnp.concatenate([q[:, LANES * (base + p):LANES * (base + p + 1)]
                              for p in range(pairs)], 0)
        s = _dot_nt(qs, k2)
        halves = []
        inv = []
        for par in range(2):
            sp = jnp.where(valid, s[:, par * tk:(par + 1) * tk], -jnp.inf)
            sink = jnp.concatenate(
                [jnp.full((tq, 1), sinks_ref[2 * (base + p) + par], F32) for p in range(pairs)], 0)
            m = jnp.maximum(jnp.max(sp, axis=-1, keepdims=True), sink)
            pexp = jnp.exp(sp - m)
            denom = jnp.sum(pexp, axis=-1, keepdims=True) + jnp.exp(sink - m)
            halves.append(pexp.astype(BF16))
            inv.append(1.0 / denom)
        o = _dot(jnp.concatenate(halves, 1), v2)
        o = o * jnp.where(lo, inv[0], inv[1])
        for p in range(pairs):
            o_ref[:, LANES * (base + p):LANES * (base + p + 1)] = \
                o[tq * p:tq * (p + 1), :].astype(o_ref.dtype)


def _attn_prompt_kernel(sinks_ref, q_ref, kp_ref, kc_ref, vp_ref, vc_ref, o_ref):
    kcat = jnp.concatenate([kp_ref[...], kc_ref[...]], 0)
    vcat = jnp.concatenate([vp_ref[...], vc_ref[...]], 0)
    _attn_core(q_ref[...], kcat, vcat, sinks_ref, pl.program_id(1) == 0, o_ref)


def _attn_sample_kernel(sinks_ref, q_ref, k_ref, v_ref, o_ref):
    _attn_core(q_ref[...], k_ref[...], v_ref[...], sinks_ref, None, o_ref)


def _attn_prompt_call(sinks, q, k, v, B, nb):
    n = q.shape[0]
    cur = lambda b, i: (b * nb + i, 0)
    prev = lambda b, i: (b * nb + jnp.maximum(i - 1, 0), 0)
    kvb = lambda im: pl.BlockSpec((WINDOW, LANES), im)
    return pl.pallas_call(
        _attn_prompt_kernel,
        out_shape=jax.ShapeDtypeStruct((n, D_MODEL), BF16),
        grid=(B, nb),
        in_specs=[pl.BlockSpec(memory_space=pltpu.SMEM),
                  pl.BlockSpec((WINDOW, D_MODEL), cur), kvb(prev), kvb(cur), kvb(prev), kvb(cur)],
        out_specs=pl.BlockSpec((WINDOW, D_MODEL), cur),
        compiler_params=_params(2),
        name="swa_prompt",
    )(sinks, q, k, k, v, v)


def _attn_sample_call(sinks, q, kcat, vcat):
    B = q.shape[0]
    b3 = lambda b: (b, 0, 0)
    return pl.pallas_call(
        _attn_sample_kernel,
        out_shape=jax.ShapeDtypeStruct(q.shape, BF16),
        grid=(B,),
        in_specs=[pl.BlockSpec(memory_space=pltpu.SMEM),
                  pl.BlockSpec((None, SAMPLE_PAD, D_MODEL), b3),
                  pl.BlockSpec((None, 2 * WINDOW, LANES), b3),
                  pl.BlockSpec((None, 2 * WINDOW, LANES), b3)],
        out_specs=pl.BlockSpec((None, SAMPLE_PAD, D_MODEL), b3),
        compiler_params=_params(1),
        name="swa_sample",
    )(sinks, q, kcat, vcat)


def _rope_tables(pos):
    half = ROT_DIM // 2
    inv = ROPE_THETA ** (-jnp.arange(half, dtype=F32) / half)
    ang = pos.astype(F32)[:, None] * inv[None]
    cos, sin = jnp.cos(ang), jnp.sin(ang)
    d = jnp.arange(LANES) % HEAD_DIM
    idx = d % half
    cos_t = jnp.where(d < ROT_DIM, cos[:, idx], 1.0)
    sa = jnp.where(d < half, -sin[:, idx], 0.0)
    sb = jnp.where((d >= half) & (d < ROT_DIM), sin[:, idx], 0.0)
    return cos_t, sa, sb


def _pad_lanes(a, value=0.0):
    return jnp.pad(a, ((0, 0), (0, LANES - a.shape[1])), constant_values=value)


def _prep_weights(ada_w, ada_b, norm_mix, norm_ffn, a_w_in, a_b_gates, a_head_norm, a_w_out,
                  kv_ada_w, kv_ada_b, kv_norm, w_k, w_v, k_norm, b_w_q, b_q_norm, b_sinks, b_w_o,
                  moe_w_group, moe_b_group, moe_w_expert, moe_b_expert, moe_w_gate_up, moe_w_down):
    w = {}
    g0 = QK_COLS + 2 * V_COLS
    win = a_w_in[0]
    w["w_in"] = jnp.concatenate([win[:, :g0], _pad_lanes(win[:, g0:g0 + M_HEADS]),
                                 _pad_lanes(win[:, g0 + M_HEADS:])], 1).astype(BF16)
    w["bli"] = _pad_lanes(a_b_gates[0][None, :M_HEADS])
    w["blf"] = _pad_lanes(a_b_gates[0][None, M_HEADS:])
    w["head_norm"] = a_head_norm[0][None]
    w["w_out"] = a_w_out[0].astype(BF16)
    w["norm_mix"] = [norm_mix[l][None] for l in range(2)]
    w["norm_ffn"] = [norm_ffn[l][None] for l in range(2)]
    w["router"] = []
    for l in range(2):
        wr = _pad_lanes(jnp.concatenate([moe_w_group[l], moe_w_expert[l]], 1))
        hi = wr.astype(BF16)
        lo = (wr - hi.astype(F32)).astype(BF16)
        br = _pad_lanes(jnp.concatenate([moe_b_group[l], moe_b_expert[l]])[None])
        w["router"].append((hi, lo, br))
    w["w_gu"] = [moe_w_gate_up[l].astype(BF16) for l in range(2)]
    w["w_d"] = [moe_w_down[l].astype(BF16) for l in range(2)]
    w["kv_norm"] = kv_norm[None]
    w["w_kv"] = jnp.concatenate([w_k, w_v], 1).astype(BF16)
    w["k_norm"] = jnp.tile(k_norm, N_KV_HEADS)[None]
    w["w_q"] = b_w_q[0].astype(BF16)
    w["q_norm"] = jnp.tile(b_q_norm[0], N_Q_HEADS)[None]
    w["sinks"] = b_sinks[0]
    w["w_o"] = b_w_o[0].astype(BF16)
    lanes = jnp.arange(LANES)
    feat = jnp.arange(D_MODEL)
    w["g64"] = jnp.where((lanes[:, None] // HEAD_DIM) == (lanes[None, :] // HEAD_DIM),
                         1.0 / HEAD_DIM, 0.0).astype(BF16)
    w["gr"] = jnp.where((feat[:, None] // HEAD_DIM) == lanes[None, :], 1.0 / HEAD_DIM, 0.0).astype(BF16)
    w["gb"] = jnp.where(lanes[:, None] == (feat[None, :] // HEAD_DIM), 1.0, 0.0).astype(BF16)
    return w


def _trunk(x2, mods, kvmod, mspec, mspec_moe, w, *, B, T, L, tm, tm_moe, tabs, tabspec, c0, n0, m0,
           cache_k=None, cache_v=None):
    sample = cache_k is not None
    nc = T // L if not sample else 1

    qk, v, o, li, lf = _inproj_call(x2, mods[0], mspec, w["norm_mix"][0], w["w_in"],
                                    w["bli"], w["blf"], tm)
    if sample:
        def padtok(a, value=0.0):
            a = a.reshape(B, T, a.shape[-1])
            a = jnp.pad(a, ((0, 0), (0, L - T), (0, 0)), constant_values=value)
            return a.reshape(B * L, a.shape[-1])
        qk, v, li, lf = padtok(qk), padtok(v), padtok(li, M_EMPTY), padtok(lf)
    h, c_new, n_new, m_new = _mlstm_call(qk, v, li, lf, c0, n0, m0, B, nc, L)
    if sample:
        h = h.reshape(B, L, V_COLS)[:, :T].reshape(B * T, V_COLS)
    routed = not sample
    ntiles = (B * T) // tm

    def routed_moe(u2, gate, gid, col, layer):
        q, runs, items = _sort_meta_call(gid.reshape(ntiles, tm), tm)
        su, sg = _scatter_call(runs, q.reshape(ntiles, 1, tm), u2, gate, tm)
        ys = _moe_routed_call(items, su, sg, w["w_gu"][layer], w["w_d"][layer], tm)
        return runs, col, ys

    rh, rl, br = w["router"][0]
    post0 = _post_call(True, routed, h, o, x2, mods[0], mspec, w["norm_ffn"][0], w["head_norm"],
                       w["w_out"], rh, rl, br, tm)
    kvq_args = (mods[0], mods[1], mspec, kvmod, mspec, w["kv_norm"], w["norm_mix"][1],
                w["w_kv"], w["w_q"], w["k_norm"], w["q_norm"], tabs, tabspec,
                w["g64"], w["gr"], w["gb"], tm)

    if routed:
        x1, u2, gate, gid, col = post0
        q, k, vv, xa = _kvq_call(x1, routed_moe(u2, gate, gid, col, 0), *kvq_args)
    else:
        x1, u2, gate = post0
        xa = _moe_call(u2, gate, w["w_gu"][0], w["w_d"][0], x1, mods[0], mspec_moe, tm_moe)
        q, k, vv = _kvq_call(xa, None, *kvq_args)
    if not sample:
        att = _attn_prompt_call(w["sinks"], q, k, vv, B, T // WINDOW)
        k_win = k.reshape(B, T, N_KV_HEADS, HEAD_DIM)[:, T - WINDOW:]
        v_win = vv.reshape(B, T, N_KV_HEADS, HEAD_DIM)[:, T - WINDOW:]
    else:
        zpad = jnp.zeros((B, WINDOW - T, LANES), F32)
        kcat = jnp.concatenate([cache_k.reshape(B, WINDOW, LANES), k.reshape(B, T, LANES), zpad], 1)
        vcat = jnp.concatenate([cache_v.reshape(B, WINDOW, LANES), vv.reshape(B, T, LANES), zpad], 1)
        qp = jnp.pad(q.reshape(B, T, D_MODEL), ((0, 0), (0, SAMPLE_PAD - T), (0, 0)))
        att = _attn_sample_call(w["sinks"], qp, kcat, vcat)[:, :T].reshape(B * T, D_MODEL)
        k_win = kcat[:, T:T + WINDOW].reshape(B, WINDOW, N_KV_HEADS, HEAD_DIM)
        v_win = vcat[:, T:T + WINDOW].reshape(B, WINDOW, N_KV_HEADS, HEAD_DIM)
    rh, rl, br = w["router"][1]
    post1 = _post_call(False, routed, att, None, xa, mods[1], mspec, w["norm_ffn"][1], None,
                       w["w_o"], rh, rl, br, tm)
    if routed:
        x3, u4, gate, gid, col = post1
        runs, col, ys = routed_moe(u4, gate, gid, col, 1)
        y = _resid_gather_call(x3, runs, col, ys, mods[1], mspec, tm)
    else:
        x3, u4, gate = post1
        y = _moe_call(u4, gate, w["w_gu"][1], w["w_d"][1], x3, mods[1], mspec_moe, tm_moe)
    c_out = c_new[None]
    n_out = n_new.reshape(1, B, M_HEADS, M_DK)
    m_out = m_new[:, 0, :M_HEADS][None]
    return y, c_out, n_out, m_out, k_win, v_win


def kernel(x_prompt, x_sample, c_prompt, c_sample, state_c, state_n, state_m, cache_k_win, cache_v_win, ada_w, ada_b, norm_mix, norm_ffn, a_w_in, a_b_gates, a_head_norm, a_w_out, kv_ada_w, kv_ada_b, kv_norm, w_k, w_v, k_norm, b_w_q, b_q_norm, b_sinks, b_w_o, moe_w_group, moe_b_group, moe_w_expert, moe_b_expert, moe_w_gate_up, moe_w_down):
    Bp, Tp, D = x_prompt.shape
    Bs, Ts, _ = x_sample.shape
    w = _prep_weights(ada_w, ada_b, norm_mix, norm_ffn, a_w_in, a_b_gates, a_head_norm, a_w_out,
                      kv_ada_w, kv_ada_b, kv_norm, w_k, w_v, k_norm, b_w_q, b_q_norm, b_sinks, b_w_o,
                      moe_w_group, moe_b_group, moe_w_expert, moe_b_expert, moe_w_gate_up, moe_w_down)

    rows = Bp + Bs
    rpad = -rows % 8
    c_all = jnp.concatenate([c_prompt, c_sample, jnp.zeros((rpad, D), F32)], 0)
    mod = _ada_call(c_all, ada_w, ada_b[:, None, :])
    kvm = _ada_call(c_all, kv_ada_w[None], kv_ada_b[None, None, :])

    tm_p = 512
    tiles_per_seq = Tp // tm_p
    mods_p = [mod[l, :Bp][:, None, :] for l in range(2)]
    kvmod_p = kvm[0, :Bp][:, None, :]

    def mspec_p(col):
        return pl.BlockSpec((None, 1, D_MODEL), lambda i, *_: (i // tiles_per_seq, 0, col))

    tm_moe_p = 1024
    moe_tiles_per_seq = Tp // tm_moe_p

    def mspec_moe_p(col):
        return pl.BlockSpec((None, 1, D_MODEL), lambda i, *_: (i // moe_tiles_per_seq, 0, col))

    tabs_p = _rope_tables(jnp.arange(Tp, dtype=jnp.int32))
    tabspec_p = pl.BlockSpec((tm_p, LANES), lambda i, *_: (i % tiles_per_seq, 0))
    npairs = M_HEADS // 2
    c0 = jnp.zeros((Bp, M_HEADS, M_DK, M_DV), F32)
    n0 = jnp.zeros((Bp, npairs, LANES), F32)
    m0 = jnp.pad(jnp.full((Bp, 1, M_HEADS), M_EMPTY, F32), ((0, 0), (0, 0), (0, LANES - M_HEADS)))
    yp, cp, np_, mp, kwp, vwp = _trunk(
        x_prompt.reshape(Bp * Tp, D), mods_p, kvmod_p, mspec_p, mspec_moe_p, w,
        B=Bp, T=Tp, L=M_CHUNK, tm=tm_p, tm_moe=tm_moe_p, tabs=tabs_p, tabspec=tabspec_p,
        c0=c0, n0=n0, m0=m0)

    ns = Bs * Ts
    mods_s = [jnp.repeat(mod[l, Bp:Bp + Bs], Ts, axis=0)[None] for l in range(2)]
    kvmod_s = jnp.repeat(kvm[0, Bp:Bp + Bs], Ts, axis=0)[None]

    def mspec_s(col):
        return pl.BlockSpec((None, ns, D_MODEL), lambda i, *_: (0, 0, col))

    tabs_s = _rope_tables(PAST_LEN + jnp.arange(Ts, dtype=jnp.int32))
    tabs_s = tuple(jnp.tile(t, (Bs, 1)) for t in tabs_s)
    m0s = jnp.pad(state_m[0][:, None, :], ((0, 0), (0, 0), (0, LANES - M_HEADS)))
    ys, cs, ns_, ms, kws, vws = _trunk(
        x_sample.reshape(ns, D), mods_s, kvmod_s, mspec_s, mspec_s, w,
        B=Bs, T=Ts, L=SAMPLE_PAD, tm=ns, tm_moe=ns, tabs=tabs_s,
        tabspec=pl.BlockSpec((ns, LANES), lambda i, *_: (0, 0)),
        c0=state_c[0], n0=state_n[0].reshape(Bs, npairs, LANES), m0=m0s,
        cache_k=cache_k_win, cache_v=cache_v_win)

    return (yp.reshape(Bp, Tp, D), ys.reshape(Bs, Ts, D), cp, np_, mp, kwp, vwp,
            cs, ns_, ms, kws, vws)
```

```python
import functools

import jax
import jax.numpy as jnp
from jax import lax
from jax.experimental import pallas as pl
from jax.experimental.pallas import tpu as pltpu

F32 = jnp.float32
BF16 = jnp.bfloat16

D_MODEL = 1024
PAST_LEN = 8192
M_HEADS = 8
M_DK = 64
M_DV = 128
M_CHUNK = 64
GATE_SOFTCAP = 15.0
M_EMPTY = -1e30
WINDOW = 128
HEAD_DIM = 64
N_Q_HEADS = 16
N_KV_HEADS = 2
ROPE_THETA = 500000.0
ROT_DIM = 16
N_GROUPS = 4
EXPERTS_PER_GROUP = 4
N_EXPERTS = 16
D_EXPERT = 256
RMS_EPS = 1e-6

LANES = 128
QK_COLS = 2 * M_HEADS * M_DK
V_COLS = M_HEADS * M_DV
IN_COLS = QK_COLS + 2 * V_COLS + 2 * LANES
SAMPLE_PAD = 16
VMEM_LIMIT = 52 * 1024 * 1024

NT_DIMS = (((1,), (1,)), ((), ()))


def _params(n_axes):
    return pltpu.CompilerParams(dimension_semantics=("arbitrary",) * n_axes,
                                vmem_limit_bytes=VMEM_LIMIT)


def _dot(a, b):
    return jnp.dot(a, b, preferred_element_type=F32)


def _dot_nt(a, b):
    return lax.dot_general(a, b, NT_DIMS, preferred_element_type=F32)


def _split2(x):
    hi = x.astype(BF16)
    lo = (x - hi.astype(F32)).astype(BF16)
    return hi, lo


def _split3(x):
    hi = x.astype(BF16)
    r = x - hi.astype(F32)
    mid = r.astype(BF16)
    lo = (r - mid.astype(F32)).astype(BF16)
    return hi, mid, lo


def _unit_rms(x):
    return x * lax.rsqrt(jnp.mean(x * x, axis=-1, keepdims=True) + RMS_EPS)


def _ada_kernel(c_ref, w_ref, b_ref, o_ref):
    c = c_ref[...]
    cs = (c * jax.nn.sigmoid(c)).astype(BF16)
    o_ref[...] = _dot(cs, w_ref[...].astype(BF16)) + b_ref[...]


def _ada_call(c, w, b):
    g, d, n = w.shape
    r = c.shape[0]
    tn = 1024
    return pl.pallas_call(
        _ada_kernel,
        out_shape=jax.ShapeDtypeStruct((g, r, n), F32),
        grid=(g, n // tn),
        in_specs=[pl.BlockSpec((r, d), lambda i, j: (0, 0)),
                  pl.BlockSpec((None, d, tn), lambda i, j: (i, 0, j)),
                  pl.BlockSpec((None, 1, tn), lambda i, j: (i, 0, j))],
        out_specs=pl.BlockSpec((None, r, tn), lambda i, j: (i, 0, j)),
        compiler_params=_params(2),
        name="ada_mod",
    )(c, w, b)


def _inproj_kernel(x_ref, sh_ref, sc_ref, nw_ref, w_ref, bli_ref, blf_ref,
                   qk_ref, v_ref, o_ref, li_ref, lf_ref):
    u = _unit_rms(x_ref[...]) * nw_ref[...] * (1.0 + sc_ref[...]) + sh_ref[...]
    ub = u.astype(BF16)
    half = QK_COLS // 2
    q = _dot(ub, w_ref[:, 0:half]) * (M_DK ** -0.5)
    qk_ref[:, 0:half] = q.astype(BF16)
    qk_ref[:, half:QK_COLS] = _dot(ub, w_ref[:, half:QK_COLS]).astype(BF16)
    v_ref[...] = _dot(ub, w_ref[:, QK_COLS:QK_COLS + V_COLS]).astype(BF16)
    o_ref[...] = _dot(ub, w_ref[:, QK_COLS + V_COLS:QK_COLS + 2 * V_COLS]).astype(BF16)
    g0 = QK_COLS + 2 * V_COLS
    lane = lax.broadcasted_iota(jnp.int32, (1, LANES), 1)
    live = lane < M_HEADS
    gi = _dot(ub, w_ref[:, g0:g0 + LANES]) + bli_ref[...]
    gf = _dot(ub, w_ref[:, g0 + LANES:g0 + 2 * LANES]) + blf_ref[...]
    li = GATE_SOFTCAP * jnp.tanh(gi / GATE_SOFTCAP)
    fpre = GATE_SOFTCAP * jnp.tanh(gf / GATE_SOFTCAP)
    lf = jnp.minimum(fpre, 0.0) - jnp.log1p(jnp.exp(-jnp.abs(fpre)))
    li_ref[...] = jnp.where(live, li, 0.0)
    lf_ref[...] = jnp.where(live, lf, 0.0)


def _inproj_call(x, mod, mspec, nw, w, bli, blf, tm):
    n = x.shape[0]
    row = lambda i: (i, 0)
    const = lambda i: (0, 0)
    return pl.pallas_call(
        _inproj_kernel,
        out_shape=(jax.ShapeDtypeStruct((n, QK_COLS), BF16),
                   jax.ShapeDtypeStruct((n, V_COLS), BF16),
                   jax.ShapeDtypeStruct((n, V_COLS), BF16),
                   jax.ShapeDtypeStruct((n, LANES), F32),
                   jax.ShapeDtypeStruct((n, LANES), F32)),
        grid=(n // tm,),
        in_specs=[pl.BlockSpec((tm, D_MODEL), row), mspec(0), mspec(1),
                  pl.BlockSpec((1, D_MODEL), const),
                  pl.BlockSpec((D_MODEL, IN_COLS), const),
                  pl.BlockSpec((1, LANES), const), pl.BlockSpec((1, LANES), const)],
        out_specs=(pl.BlockSpec((tm, QK_COLS), row), pl.BlockSpec((tm, V_COLS), row),
                   pl.BlockSpec((tm, V_COLS), row), pl.BlockSpec((tm, LANES), row),
                   pl.BlockSpec((tm, LANES), row)),
        compiler_params=_params(1),
        name="mlstm_inproj",
    )(x, mod, mod, nw, w, bli, blf)


def _mlstm_kernel(L, qk_ref, v_ref, li_ref, lf_ref, c0_ref, n0_ref, m0_ref,
                  h_ref, cout_ref, nout_ref, mout_ref, c2_s, n_s, m_s):
    c = pl.program_id(1)
    nc = pl.num_programs(1)
    npairs = M_HEADS // 2
    hd = M_DK

    lane = lax.broadcasted_iota(jnp.int32, (1, LANES), 1)
    lo128 = lane < hd
    lane256 = lax.broadcasted_iota(jnp.int32, (1, 2 * LANES), 1)
    lo256 = lane256 < LANES
    row128 = lax.broadcasted_iota(jnp.int32, (LANES, 1), 0)
    top = row128 < hd
    blockdiag = (top & lo256) | (jnp.logical_not(top) & jnp.logical_not(lo256))

    @pl.when(c == 0)
    def _():
        m_s[...] = m0_ref[...]
        n_s[...] = n0_ref[...]
        z = jnp.zeros((M_DK, M_DV), F32)
        for j in range(npairs):
            ce = c0_ref[2 * j]
            co = c0_ref[2 * j + 1]
            c2_s[j] = jnp.concatenate([jnp.concatenate([ce, z], 1),
                                       jnp.concatenate([z, co], 1)], 0)

    LI = li_ref[...]
    LF = lf_ref[...]
    rowL = lax.broadcasted_iota(jnp.int32, (L, LANES), 0)

    def prefix(x, op, ident):
        d = 1
        while d < L:
            shifted = pltpu.roll(x, d, axis=0)
            x = op(x, jnp.where(rowL >= d, shifted, ident))
            d *= 2
        return x

    Bc = prefix(LF, jnp.add, 0.0)
    Cm = LI - Bc
    mprev = m_s[...]
    Gc = jnp.maximum(mprev, prefix(Cm, jnp.maximum, -jnp.inf))
    A = jnp.exp(mprev - Gc)
    EXPM = jnp.exp(-(Bc + Gc))
    bL = Bc[L - 1:L, :]
    DL = bL + Cm
    mnew = jnp.maximum(bL + mprev, jnp.max(DL, axis=0, keepdims=True))
    ast = jnp.exp(bL + mprev - mnew)
    WST = jnp.exp(DL - mnew)

    def pad_rows(x, rows):
        if x.shape[0] == rows:
            return x
        return jnp.concatenate([x, jnp.zeros((rows - x.shape[0], x.shape[1]), x.dtype)], 0)

    cm_pad = pad_rows(Cm, hd)
    XT = jnp.concatenate([cm_pad, cm_pad], 0).T
    causal = (lane & (hd - 1)) <= rowL
    J = jnp.where(blockdiag, 1.0, 0.0).astype(BF16)

    def bc(X, h):
        return jnp.broadcast_to(X[:, h:h + 1], X.shape)

    for j in range(npairs):
        he, ho = 2 * j, 2 * j + 1

        def pair128(X):
            return jnp.where(lo128, bc(X, he), bc(X, ho))

        def pair256(X):
            return jnp.concatenate([bc(X, he), bc(X, ho)], 1)

        q128 = qk_ref[:, LANES * j:LANES * (j + 1)]
        k128 = qk_ref[:, QK_COLS // 2 + LANES * j:QK_COLS // 2 + LANES * (j + 1)]
        v256 = v_ref[:, 2 * LANES * j:2 * LANES * (j + 1)]
        zk = jnp.zeros_like(k128)
        zv = jnp.zeros_like(v256)
        K2t = jnp.concatenate([pad_rows(jnp.where(lo128, k128, zk), hd),
                               pad_rows(jnp.where(lo128, zk, k128), hd)], 0)
        V2 = jnp.concatenate([pad_rows(jnp.where(lo256, v256, zv), hd),
                              pad_rows(jnp.where(lo256, zv, v256), hd)], 0)
        S = _dot_nt(q128, K2t)
        crow = jnp.where(lo128, XT[he:he + 1, :], XT[ho:ho + 1, :])
        arg = jnp.where(causal, crow - pair128(Gc), -jnp.inf)
        Sw = (S * jnp.exp(arg)).astype(BF16)
        num_intra = _dot(Sw, V2)
        rowsum = _dot(Sw, J)
        C2 = c2_s[j]
        inter = _dot(q128, C2.astype(BF16))
        npair = n_s[j:j + 1, :]
        Nrow2 = jnp.concatenate(
            [jnp.broadcast_to(jnp.where(lo128, npair, 0.0), (LANES, LANES)),
             jnp.broadcast_to(jnp.where(lo128, 0.0, npair), (LANES, LANES))], 0).astype(BF16)
        qn = _dot_nt(q128, Nrow2)
        a256 = pair256(A)
        num = a256 * inter + num_intra
        den = a256 * qn + rowsum
        h = num / jnp.maximum(jnp.abs(den), pair256(EXPM))
        h_ref[:, 2 * LANES * j:2 * LANES * (j + 1)] = h.astype(h_ref.dtype)

        kw = k128.astype(F32) * pair128(WST)
        n_s[j:j + 1, :] = pair128(ast) * npair + jnp.sum(kw, axis=0, keepdims=True)
        kwT = pad_rows(kw, LANES).T.astype(BF16)
        dC = _dot(kwT, pad_rows(v256, LANES))
        c2_s[j] = pair256(ast) * C2 + jnp.where(blockdiag, dC, 0.0)

    m_s[...] = mnew

    @pl.when(c == nc - 1)
    def _():
        for j in range(npairs):
            C2 = c2_s[j]
            cout_ref[2 * j] = C2[0:M_DK, 0:M_DV]
            cout_ref[2 * j + 1] = C2[M_DK:2 * M_DK, M_DV:2 * M_DV]
        nout_ref[...] = n_s[...]
        mout_ref[...] = mnew


def _mlstm_call(qk, v, li, lf, c0, n0, m0, B, nc, L):
    n = qk.shape[0]
    npairs = M_HEADS // 2
    tok = lambda b, c: (b * nc + c, 0)
    st4 = lambda b, c: (b, 0, 0, 0)
    st3 = lambda b, c: (b, 0, 0)
    return pl.pallas_call(
        functools.partial(_mlstm_kernel, L),
        out_shape=(jax.ShapeDtypeStruct((n, V_COLS), BF16),
                   jax.ShapeDtypeStruct((B, M_HEADS, M_DK, M_DV), F32),
                   jax.ShapeDtypeStruct((B, npairs, LANES), F32),
                   jax.ShapeDtypeStruct((B, 1, LANES), F32)),
        grid=(B, nc),
        in_specs=[pl.BlockSpec((L, QK_COLS), tok), pl.BlockSpec((L, V_COLS), tok),
                  pl.BlockSpec((L, LANES), tok), pl.BlockSpec((L, LANES), tok),
                  pl.BlockSpec((None, M_HEADS, M_DK, M_DV), st4),
                  pl.BlockSpec((None, npairs, LANES), st3),
                  pl.BlockSpec((None, 1, LANES), st3)],
        out_specs=(pl.BlockSpec((L, V_COLS), tok),
                   pl.BlockSpec((None, M_HEADS, M_DK, M_DV), st4),
                   pl.BlockSpec((None, npairs, LANES), st3),
                   pl.BlockSpec((None, 1, LANES), st3)),
        scratch_shapes=[pltpu.VMEM((npairs, 2 * M_DK, 2 * M_DV), F32),
                        pltpu.VMEM((npairs, LANES), F32),
                        pltpu.VMEM((1, LANES), F32)],
        compiler_params=_params(2),
        name="mlstm_chunks",
    )(qk, v, li, lf, c0, n0, m0)


def _route(lg):
    lane = lax.broadcasted_iota(jnp.int32, lg.shape, 1)
    lanef = lane.astype(F32)
    neg = -jnp.inf
    far = float(LANES)
    gm = lane < N_GROUPS
    lgm = jnp.where(gm, lg, neg)
    gmax = jnp.max(lgm, axis=-1, keepdims=True)
    gsum = jnp.sum(jnp.exp(lgm - gmax), axis=-1, keepdims=True)
    g_w = 1.0 / gsum
    gidx = jnp.min(jnp.where(gm & (lg == gmax), lanef, far), axis=-1, keepdims=True)
    first = N_GROUPS + EXPERTS_PER_GROUP * gidx
    sel = (lanef >= first) & (lanef < first + EXPERTS_PER_GROUP)
    l1 = jnp.max(jnp.where(sel, lg, neg), axis=-1, keepdims=True)
    i1 = jnp.min(jnp.where(sel & (lg == l1), lanef, far), axis=-1, keepdims=True)
    sel2 = sel & (lanef != i1)
    l2 = jnp.max(jnp.where(sel2, lg, neg), axis=-1, keepdims=True)
    i2 = jnp.min(jnp.where(sel2 & (lg == l2), lanef, far), axis=-1, keepdims=True)
    r = jnp.exp(l2 - l1)
    w1 = g_w / (1.0 + r)
    w2 = w1 * r
    return jnp.where(lanef == i1, w1, jnp.where(lanef == i2, w2, 0.0)), gidx


def _post_kernel(mlstm, routed, *refs):
    refs = list(refs)
    h_ref = refs.pop(0)
    o_ref = refs.pop(0) if mlstm else None
    x_ref, g1_ref, sh2_ref, sc2_ref, nf_ref = refs[:5]
    refs = refs[5:]
    hn_ref = refs.pop(0) if mlstm else None
    wout_ref, wrh_ref, wrl_ref, br_ref = refs[:4]
    refs = refs[4:]
    tri_ref = refs.pop(0) if routed else None
    x1_ref, u2_ref, gate_ref = refs[:3]
    outs = refs[3:]
    if mlstm:
        hf = h_ref[...].astype(F32)
        parts = [_unit_rms(hf[:, M_DV * i:M_DV * (i + 1)]) for i in range(M_HEADS)]
        hn = jnp.concatenate(parts, 1) * hn_ref[...]
        hg = (hn * jax.nn.sigmoid(o_ref[...].astype(F32))).astype(BF16)
    else:
        hg = h_ref[...]
    x1 = x_ref[...] + g1_ref[...] * _dot(hg, wout_ref[...])
    x1_ref[...] = x1
    u2 = _unit_rms(x1) * nf_ref[...] * (1.0 + sc2_ref[...]) + sh2_ref[...]
    uh, ul = _split2(u2)
    lg = _dot(uh, wrh_ref[...]) + _dot(ul, wrh_ref[...]) + _dot(uh, wrl_ref[...]) + br_ref[...]
    gate, gidx = _route(lg)
    u2_ref[...] = uh
    gate_ref[...] = gate
    if not routed:
        return
    gid_ref, col_ref = outs
    rows = []
    for blk in range(x1.shape[0] // LANES):
        col = jnp.broadcast_to(gidx[LANES * blk:LANES * (blk + 1), :], (LANES, LANES))
        rows.append(col.T[0:1, :])
    gid_ref[...] = jnp.concatenate(rows, 1)
    lanef = lax.broadcasted_iota(jnp.int32, gate.shape, 1).astype(F32)
    onehot = jnp.where(lanef == gidx, 1.0, 0.0)
    earlier = _dot(tri_ref[...], onehot.astype(BF16))
    rank = jnp.sum(onehot * earlier, axis=-1, keepdims=True)
    col_ref[...] = jnp.where(lanef == 0.0, gidx, jnp.where(lanef == 1.0, rank, 0.0))


def _post_call(mlstm, routed, h, o, x, mod, mspec, nf, hn, wout, wrh, wrl, br, tm):
    n = x.shape[0]
    row = lambda i: (i, 0)
    const = lambda i: (0, 0)
    tok = pl.BlockSpec((tm, D_MODEL), row)
    small = pl.BlockSpec((tm, LANES), row)
    vec = pl.BlockSpec((1, D_MODEL), const)
    ins = [h] + ([o] if mlstm else []) + [x, mod, mod, mod, nf] + ([hn] if mlstm else []) \
        + [wout, wrh, wrl, br]
    specs = [tok] + ([tok] if mlstm else []) + [tok, mspec(2), mspec(3), mspec(4), vec] \
        + ([vec] if mlstm else []) \
        + [pl.BlockSpec((D_MODEL, D_MODEL), const), pl.BlockSpec((D_MODEL, LANES), const),
           pl.BlockSpec((D_MODEL, LANES), const), pl.BlockSpec((1, LANES), const)]
    out_shape = [jax.ShapeDtypeStruct((n, D_MODEL), F32), jax.ShapeDtypeStruct((n, D_MODEL), BF16),
                 jax.ShapeDtypeStruct((n, LANES), F32)]
    out_specs = [tok, tok, small]
    if routed:
        t = jnp.arange(tm)
        ins.append((t[None, :] < t[:, None]).astype(BF16))
        specs.append(pl.BlockSpec((tm, tm), const))
        out_shape += [jax.ShapeDtypeStruct((n // tm, 1, tm), F32), jax.ShapeDtypeStruct((n, LANES), F32)]
        out_specs += [pl.BlockSpec((None, 1, tm), lambda i: (i, 0, 0)), small]
    out_shape, out_specs = tuple(out_shape), tuple(out_specs)
    return pl.pallas_call(
        functools.partial(_post_kernel, mlstm, routed),
        out_shape=out_shape,
        grid=(n // tm,),
        in_specs=specs,
        out_specs=out_specs,
        compiler_params=_params(1),
        name="post_mlstm" if mlstm else "post_attn",
    )(*ins)


ITEM_ROWS = 8
RUN_ALIGN = 16
RUN_SIZES = (512, 256, 128, 64, 32, 16)
RUN_SRC, RUN_DST, RUN_LEN, RUN_TOTAL = 0, N_GROUPS, 2 * N_GROUPS, 3 * N_GROUPS


def _stage_rows(tm):
    assert N_GROUPS * (RUN_ALIGN - 1) <= LANES
    return tm + LANES


def _sorted_rows(n, tm):
    rows = n + N_GROUPS * (n // tm) * RUN_ALIGN + tm
    return -(-rows // tm) * tm


def _sort_meta_kernel(tm, nts, gid_ref, q_ref, runs_ref, items_ref):
    ntiles = gid_ref.shape[0]
    gid = gid_ref[...]
    r = lax.broadcasted_iota(jnp.int32, (tm, tm), 0)
    c = lax.broadcasted_iota(jnp.int32, (tm, tm), 1)
    before = jnp.where(r < c, 1.0, 0.0).astype(BF16)
    trow = lax.broadcasted_iota(jnp.int32, (ntiles, LANES), 0)
    lane = lax.broadcasted_iota(jnp.int32, (ntiles, LANES), 1)
    k = lax.broadcasted_iota(jnp.int32, (1, LANES), 1).astype(F32)
    zero11 = jnp.zeros((1, 1), F32)
    q = jnp.zeros((ntiles, tm), F32)
    runs = jnp.zeros((ntiles, LANES), F32)
    src = jnp.zeros((ntiles, 1), F32)
    start, nitems = zero11, zero11
    grp = jnp.zeros((1, LANES), F32)
    tile = jnp.zeros((1, LANES), F32)
    valid = jnp.zeros((1, LANES), F32)
    for g in range(N_GROUPS):
        mask = jnp.where(gid == float(g), 1.0, 0.0)
        cnt = jnp.sum(mask, axis=1, keepdims=True)
        padded = jnp.floor((cnt + (RUN_ALIGN - 1.0)) * (1.0 / RUN_ALIGN)) * RUN_ALIGN
        incl = jnp.broadcast_to(padded, (ntiles, LANES))
        d = 1
        while d < ntiles:
            incl = incl + jnp.where(trow >= d, pltpu.roll(incl, d, axis=0), 0.0)
            d *= 2
        total = incl[ntiles - 1:ntiles, 0:1]
        dst = start + incl[:, 0:1] - padded
        q = q + mask * (src + _dot(mask.astype(BF16), before))
        runs = runs + jnp.where(lane == RUN_SRC + g, src, 0.0) + jnp.where(lane == RUN_DST + g, dst, 0.0) \
            + jnp.where(lane == RUN_LEN + g, padded, 0.0)
        src = src + padded
        end = start + total
        ft = jnp.floor(start * (1.0 / tm))
        lt = jnp.floor((end - 1.0) * (1.0 / tm))
        ni = jnp.where(total > 0.0, lt - ft + 1.0, 0.0)
        inside = (k >= nitems) & (k < nitems + ni)
        grp = grp + jnp.where(inside, float(g), 0.0)
        tile = tile + jnp.where(inside, ft + (k - nitems), 0.0)
        valid = valid + jnp.where(inside, 1.0, 0.0)
        nitems = nitems + ni
        start = end
    runs = runs + jnp.where(lane == RUN_TOTAL, start, 0.0)
    live = valid > 0.0
    prev = pltpu.roll(tile, 1, axis=1)
    nxt = pltpu.roll(tile, LANES - 1, axis=1)
    first = live & ((k == 0.0) | (tile != prev))
    last = live & ((k == nitems - 1.0) | (tile != nxt))
    spare = jnp.floor((start - 1.0) * (1.0 / tm)) + 1.0 + (k - nitems)
    fill = jnp.logical_not(live) & (spare <= nts - 1.0)
    tile = jnp.where(live, tile, jnp.minimum(spare, nts - 1.0))
    grp = jnp.where(live, grp, jnp.max(grp, axis=1, keepdims=True))
    flag = lambda m: jnp.where(m, 1.0, 0.0)
    q_ref[...] = q
    runs_ref[...] = runs.astype(jnp.int32)
    table = jnp.concatenate([tile, grp, valid, flag(first | fill), flag(last | fill),
                             jnp.zeros((ITEM_ROWS - 5, LANES), F32)], 0)
    items_ref[...] = table.astype(jnp.int32)


def _sort_meta_call(gid, tm):
    ntiles = gid.shape[0]
    nts = _sorted_rows(ntiles * tm, tm) // tm
    assert nts + N_GROUPS - 1 <= LANES and tm <= RUN_SIZES[0]
    return pl.pallas_call(
        functools.partial(_sort_meta_kernel, tm, nts),
        out_shape=(jax.ShapeDtypeStruct((ntiles, tm), F32),
                   jax.ShapeDtypeStruct((ntiles, LANES), jnp.int32),
                   jax.ShapeDtypeStruct((ITEM_ROWS, LANES), jnp.int32)),
        compiler_params=_params(0),
        name="moe_sort_meta",
    )(gid)


def _run_copies(runs_ref, i, tile_ref, sorted_hbm, sem, to_sorted):
    pieces = []
    for g in range(N_GROUPS):
        src = runs_ref[i, RUN_SRC + g]
        dst = runs_ref[i, RUN_DST + g]
        length = runs_ref[i, RUN_LEN + g]
        for s in RUN_SIZES:
            def build(src=src, dst=dst, length=length, s=s):
                off = length & (-2 * s)
                a = tile_ref.at[pl.ds(pl.multiple_of(src + off, RUN_ALIGN), s), :]
                b = sorted_hbm.at[pl.ds(pl.multiple_of(dst + off, RUN_ALIGN), s), :]
                return pltpu.make_async_copy(a, b, sem) if to_sorted else pltpu.make_async_copy(b, a, sem)
            pieces.append(((length & s) != 0, build))
    return pieces


def _start(pieces):
    for pred, build in pieces:
        pl.when(pred)(lambda build=build: build().start())


def _wait(pieces):
    for pred, build in pieces:
        pl.when(pred)(lambda build=build: build().wait())


def _start_then_wait(pieces):
    _start(pieces)
    _wait(pieces)


def _scatter_kernel(tm, nts, ntiles, runs_ref, q_ref, u_ref, g_ref, su_hbm, sg_hbm,
                    stu, stg, sem_u, sem_g):
    i = pl.program_id(0)
    slot = i % 2

    def copies(tile, slot):
        return (_run_copies(runs_ref, tile, stu.at[slot], su_hbm, sem_u.at[slot], True)
                + _run_copies(runs_ref, tile, stg.at[slot], sg_hbm, sem_g.at[slot], True))

    r = lax.broadcasted_iota(jnp.int32, (_stage_rows(tm), tm), 0).astype(F32)
    perm = jnp.where(q_ref[...] == r, 1.0, 0.0).astype(BF16)
    stu[slot] = _dot(perm, u_ref[...]).astype(BF16)
    g3 = _split3(g_ref[...])
    stg[slot] = _dot(perm, g3[0]) + _dot(perm, g3[1]) + _dot(perm, g3[2])
    _start(copies(i, slot))

    @pl.when(i > 0)
    def _():
        _wait(copies(i - 1, 1 - slot))

    @pl.when(i == ntiles - 1)
    def _():
        _wait(copies(i, slot))
        stu[slot] = jnp.zeros(stu.shape[1:], stu.dtype)
        stg[slot] = jnp.zeros(stg.shape[1:], stg.dtype)
        total = runs_ref[i, RUN_TOTAL]
        tail = nts * tm - total
        nfull = tail // tm
        pieces = []
        for tile_ref, hbm, sem in ((stu.at[slot], su_hbm, sem_u.at[slot]),
                                   (stg.at[slot], sg_hbm, sem_g.at[slot])):
            for j in range(nts - ntiles):
                def full(j=j, tile_ref=tile_ref, hbm=hbm, sem=sem):
                    dst = pl.multiple_of(total + j * tm, RUN_ALIGN)
                    return pltpu.make_async_copy(tile_ref.at[pl.ds(0, tm), :], hbm.at[pl.ds(dst, tm), :], sem)
                pieces.append((j < nfull, full))
            rem = tail - nfull * tm
            for s in RUN_SIZES:
                if s >= tm:
                    continue
                def part(s=s, tile_ref=tile_ref, hbm=hbm, sem=sem):
                    dst = pl.multiple_of(total + nfull * tm + (rem & (-2 * s)), RUN_ALIGN)
                    return pltpu.make_async_copy(tile_ref.at[pl.ds(0, s), :], hbm.at[pl.ds(dst, s), :], sem)
                pieces.append(((rem & s) != 0, part))
        _start_then_wait(pieces)


def _scatter_call(runs, q3, u2, gate, tm):
    n = u2.shape[0]
    ns = _sorted_rows(n, tm)
    grid_spec = pltpu.PrefetchScalarGridSpec(
        num_scalar_prefetch=1,
        grid=(n // tm,),
        in_specs=[pl.BlockSpec((None, 1, tm), lambda i, *_: (i, 0, 0)),
                  pl.BlockSpec((tm, D_MODEL), lambda i, *_: (i, 0)),
                  pl.BlockSpec((tm, LANES), lambda i, *_: (i, 0))],
        out_specs=(pl.BlockSpec(memory_space=pl.ANY), pl.BlockSpec(memory_space=pl.ANY)),
        scratch_shapes=[pltpu.VMEM((2, _stage_rows(tm), D_MODEL), BF16),
                        pltpu.VMEM((2, _stage_rows(tm), LANES), F32),
                        pltpu.SemaphoreType.DMA((2,)), pltpu.SemaphoreType.DMA((2,))])
    return pl.pallas_call(
        functools.partial(_scatter_kernel, tm, ns // tm, n // tm),
        out_shape=(jax.ShapeDtypeStruct((ns, D_MODEL), BF16), jax.ShapeDtypeStruct((ns, LANES), F32)),
        grid_spec=grid_spec,
        compiler_params=_params(1),
        name="moe_scatter",
    )(runs, q3, u2, gate)


def _ungroup(tm, runs_ref, col_ref, ys_hbm, stage, sem):
    i = pl.program_id(0)
    slot = i % 2

    def copies(tile, slot):
        return _run_copies(runs_ref, tile, stage.at[slot], ys_hbm, sem.at[slot], False)

    @pl.when(i == 0)
    def _():
        stage[...] = jnp.zeros_like(stage)
        _start(copies(i, slot))

    @pl.when(i + 1 < pl.num_programs(0))
    def _():
        _start(copies(i + 1, 1 - slot))

    _wait(copies(i, slot))
    col = col_ref[...]
    gidx = col[:, 0:1]
    src = [runs_ref[i, RUN_SRC + g].astype(F32) for g in range(N_GROUPS)]
    first = src[N_GROUPS - 1]
    for g in range(N_GROUPS - 2, -1, -1):
        first = jnp.where(gidx == float(g), src[g], first)
    q = first + col[:, 1:2]
    lanes = lax.broadcasted_iota(jnp.int32, (tm, _stage_rows(tm)), 1).astype(F32)
    perm_t = jnp.where(q == lanes, 1.0, 0.0).astype(BF16)
    return _dot(perm_t, stage[slot])


def _moe_routed_kernel(items_ref, su_ref, sg_ref, wgu_ref, wd_ref, out_ref, acc_ref):
    kk = pl.program_id(0)
    grp = items_ref[1, kk]

    @pl.when(items_ref[3, kk] == 1)
    def _():
        acc_ref[...] = jnp.zeros_like(acc_ref)

    @pl.when(items_ref[2, kk] == 1)
    def _():
        u = su_ref[...]
        gate = sg_ref[...]
        lane = lax.broadcasted_iota(jnp.int32, gate.shape, 1)
        first_lane = N_GROUPS + EXPERTS_PER_GROUP * grp
        for e in range(EXPERTS_PER_GROUP):
            gcol = jnp.sum(jnp.where(lane == first_lane + e, gate, 0.0), axis=-1, keepdims=True)
            hu = _dot(u, wgu_ref[e])
            hg = hu[:, 0:D_EXPERT]
            act = hg * jax.nn.sigmoid(hg) * hu[:, D_EXPERT:2 * D_EXPERT] * gcol
            acc_ref[...] += _dot(act.astype(BF16), wd_ref[e])

    @pl.when(items_ref[4, kk] == 1)
    def _():
        out_ref[...] = acc_ref[...].astype(out_ref.dtype)


def _moe_routed_call(items, su, sg, wgu, wd, tm):
    ns = su.shape[0]
    n_items = ns // tm + N_GROUPS - 1
    wgu4 = wgu.reshape(N_GROUPS, EXPERTS_PER_GROUP, D_MODEL, 2 * D_EXPERT)
    wd4 = wd.reshape(N_GROUPS, EXPERTS_PER_GROUP, D_EXPERT, D_MODEL)
    grid_spec = pltpu.PrefetchScalarGridSpec(
        num_scalar_prefetch=1,
        grid=(n_items,),
        in_specs=[pl.BlockSpec((tm, D_MODEL), lambda k, it: (it[0, k], 0)),
                  pl.BlockSpec((tm, LANES), lambda k, it: (it[0, k], 0)),
                  pl.BlockSpec((None, EXPERTS_PER_GROUP, D_MODEL, 2 * D_EXPERT),
                               lambda k, it: (it[1, k], 0, 0, 0)),
                  pl.BlockSpec((None, EXPERTS_PER_GROUP, D_EXPERT, D_MODEL),
                               lambda k, it: (it[1, k], 0, 0, 0))],
        out_specs=pl.BlockSpec((tm, D_MODEL), lambda k, it: (it[0, k], 0)),
        scratch_shapes=[pltpu.VMEM((tm, D_MODEL), F32)])
    return pl.pallas_call(
        _moe_routed_kernel,
        out_shape=jax.ShapeDtypeStruct((ns, D_MODEL), BF16),
        grid_spec=grid_spec,
        compiler_params=_params(1),
        name="moe_routed",
    )(items, su, sg, wgu4, wd4)


def _resid_gather_kernel(tm, runs_ref, x_ref, g2_ref, col_ref, ys_hbm, y_ref, stage, sem):
    y_ref[...] = x_ref[...] + g2_ref[...] * _ungroup(tm, runs_ref, col_ref, ys_hbm, stage, sem)


def _resid_gather_call(x, runs, col, ys, mod, mspec, tm):
    n = x.shape[0]
    row = lambda i, *_: (i, 0)
    grid_spec = pltpu.PrefetchScalarGridSpec(
        num_scalar_prefetch=1,
        grid=(n // tm,),
        in_specs=[pl.BlockSpec((tm, D_MODEL), row), mspec(5), pl.BlockSpec((tm, LANES), row),
                  pl.BlockSpec(memory_space=pl.ANY)],
        out_specs=pl.BlockSpec((tm, D_MODEL), row),
        scratch_shapes=[pltpu.VMEM((2, _stage_rows(tm), D_MODEL), BF16), pltpu.SemaphoreType.DMA((2,))])
    return pl.pallas_call(
        functools.partial(_resid_gather_kernel, tm),
        out_shape=jax.ShapeDtypeStruct((n, D_MODEL), F32),
        grid_spec=grid_spec,
        compiler_params=_params(1),
        name="moe_unsort_resid",
    )(runs, x, mod, col, ys)


def _moe_kernel(u_ref, gate_ref, wgu_ref, wd_ref, x1_ref, g2_ref, y_ref, acc_ref):
    e = pl.program_id(1)

    @pl.when(e == 0)
    def _():
        acc_ref[...] = jnp.zeros_like(acc_ref)

    hu = _dot(u_ref[...], wgu_ref[...])
    hg = hu[:, 0:D_EXPERT]
    hv = hu[:, D_EXPERT:2 * D_EXPERT]
    gate = gate_ref[...]
    lane = lax.broadcasted_iota(jnp.int32, gate.shape, 1)
    gcol = jnp.sum(jnp.where(lane == N_GROUPS + e, gate, 0.0), axis=-1, keepdims=True)
    act = hg * jax.nn.sigmoid(hg) * hv * gcol
    acc_ref[...] += _dot(act.astype(BF16), wd_ref[...])

    @pl.when(e == N_EXPERTS - 1)
    def _():
        y_ref[...] = x1_ref[...] + g2_ref[...] * acc_ref[...]


def _moe_call(u2, gate, wgu, wd, x1, mod, mspec, tm):
    n = u2.shape[0]
    row = lambda i, e: (i, 0)
    return pl.pallas_call(
        _moe_kernel,
        out_shape=jax.ShapeDtypeStruct((n, D_MODEL), F32),
        grid=(n // tm, N_EXPERTS),
        in_specs=[pl.BlockSpec((tm, D_MODEL), row), pl.BlockSpec((tm, LANES), row),
                  pl.BlockSpec((None, D_MODEL, 2 * D_EXPERT), lambda i, e: (e, 0, 0)),
                  pl.BlockSpec((None, D_EXPERT, D_MODEL), lambda i, e: (e, 0, 0)),
                  pl.BlockSpec((tm, D_MODEL), row), mspec(5)],
        out_specs=pl.BlockSpec((tm, D_MODEL), row),
        scratch_shapes=[pltpu.VMEM((tm, D_MODEL), F32)],
        compiler_params=_params(2),
        name="moe_dense",
    )(u2, gate, wgu, wd, x1, mod)


def _rope128(x, cos, sa, sb):
    return x * cos + pltpu.roll(x, LANES - ROT_DIM // 2, axis=1) * sa \
        + pltpu.roll(x, ROT_DIM // 2, axis=1) * sb


def _kvq_kernel(gather_tm, *refs):
    if gather_tm:
        (runs_ref, x_ref, g2_ref, col_ref, ys_hbm, kvsh_ref, kvsc_ref, sh1_ref, sc1_ref, kvn_ref,
         nm_ref, wkv_ref, wq_ref, kn_ref, qn_ref, cos_ref, sa_ref, sb_ref, g64_ref, gr_ref, gb_ref,
         q_ref, k_ref, v_ref, xa_ref, stage, sem) = refs
        x = x_ref[...] + g2_ref[...] * _ungroup(gather_tm, runs_ref, col_ref, ys_hbm, stage, sem)
        xa_ref[...] = x
    else:
        (x_ref, kvsh_ref, kvsc_ref, sh1_ref, sc1_ref, kvn_ref, nm_ref,
         wkv_ref, wq_ref, kn_ref, qn_ref, cos_ref, sa_ref, sb_ref, g64_ref, gr_ref, gb_ref,
         q_ref, k_ref, v_ref) = refs
        x = x_ref[...]
    xn = _unit_rms(x)
    cos, sa, sb = cos_ref[...], sa_ref[...], sb_ref[...]

    ukv = xn * kvn_ref[...] * (1.0 + kvsc_ref[...]) + kvsh_ref[...]
    kv = _dot(ukv.astype(BF16), wkv_ref[...])
    k = kv[:, 0:LANES]
    v_ref[...] = kv[:, LANES:2 * LANES]
    kh, kl = _split2(k * k)
    ms = _dot(kh, g64_ref[...]) + _dot(kl, g64_ref[...])
    k_ref[...] = _rope128(k * lax.rsqrt(ms + RMS_EPS) * kn_ref[...], cos, sa, sb)

    u1 = xn * nm_ref[...] * (1.0 + sc1_ref[...]) + sh1_ref[...]
    q = _dot(u1.astype(BF16), wq_ref[...])
    qh, ql = _split2(q * q)
    ms16 = _dot(qh, gr_ref[...]) + _dot(ql, gr_ref[...])
    r3 = _split3(lax.rsqrt(ms16 + RMS_EPS))
    rsb = _dot(r3[0], gb_ref[...]) + _dot(r3[1], gb_ref[...]) + _dot(r3[2], gb_ref[...])
    qn = q * rsb * qn_ref[...]
    for i in range(D_MODEL // LANES):
        sl = slice(LANES * i, LANES * (i + 1))
        q_ref[:, sl] = (_rope128(qn[:, sl], cos, sa, sb) * (HEAD_DIM ** -0.5)).astype(BF16)


def _kvq_call(x, routed, mod0, mod1, mspec, kvmod, kvspec, kvn, nm, wkv, wq, kn, qn, tabs, tab,
              g64, gr, gb, tm):
    n = x.shape[0]
    row = lambda i, *_: (i, 0)
    const = lambda i, *_: (0, 0)
    vec = pl.BlockSpec((1, D_MODEL), const)
    tok = pl.BlockSpec((tm, D_MODEL), row)
    small = pl.BlockSpec((tm, LANES), row)
    ins = [x, kvmod, kvmod, mod1, mod1, kvn, nm, wkv, wq, kn, qn, *tabs, g64, gr, gb]
    specs = [tok, kvspec(0), kvspec(1), mspec(0), mspec(1), vec, vec,
             pl.BlockSpec((D_MODEL, 2 * LANES), const), pl.BlockSpec((D_MODEL, D_MODEL), const),
             pl.BlockSpec((1, LANES), const), vec, tab, tab, tab,
             pl.BlockSpec((LANES, LANES), const), pl.BlockSpec((D_MODEL, LANES), const),
             pl.BlockSpec((LANES, D_MODEL), const)]
    out_shape = [jax.ShapeDtypeStruct((n, D_MODEL), BF16), jax.ShapeDtypeStruct((n, LANES), F32),
                 jax.ShapeDtypeStruct((n, LANES), F32)]
    out_specs = [tok, small, small]
    scratch = []
    nprefetch = 0
    if routed is not None:
        runs, col, ys = routed
        nprefetch = 1
        ins = [runs, x, mod0, col, ys] + ins[1:]
        specs = [tok, mspec(5), small, pl.BlockSpec(memory_space=pl.ANY)] + specs[1:]
        out_shape.append(jax.ShapeDtypeStruct((n, D_MODEL), F32))
        out_specs.append(tok)
        scratch = [pltpu.VMEM((2, _stage_rows(tm), D_MODEL), BF16), pltpu.SemaphoreType.DMA((2,))]
    grid_spec = pltpu.PrefetchScalarGridSpec(
        num_scalar_prefetch=nprefetch, grid=(n // tm,), in_specs=specs,
        out_specs=tuple(out_specs), scratch_shapes=scratch)
    return pl.pallas_call(
        functools.partial(_kvq_kernel, tm if routed is not None else 0),
        out_shape=tuple(out_shape),
        grid_spec=grid_spec,
        compiler_params=_params(1),
        name="kv_q_proj",
    )(*ins)


def _attn_core(q, kcat, vcat, bias, sinks_ref, o_ref):
    tk = kcat.shape[0]
    pairs = N_Q_HEADS // N_KV_HEADS // 2
    lane = lax.broadcasted_iota(jnp.int32, (1, LANES), 1)
    lo = lane < HEAD_DIM
    kro = pltpu.roll(kcat, HEAD_DIM, axis=1)
    vro = pltpu.roll(vcat, HEAD_DIM, axis=1)
    one_e = jnp.broadcast_to(jnp.where(lo, 1.0, 0.0), (tk, LANES))
    one_o = 1.0 - one_e
    for g in range(N_KV_HEADS):
        if g == 0:
            ke, ko = jnp.where(lo, kcat, 0.0), jnp.where(lo, 0.0, kro)
            ve, vo = jnp.where(lo, vcat, 0.0), jnp.where(lo, 0.0, vro)
        else:
            ke, ko = jnp.where(lo, kro, 0.0), jnp.where(lo, 0.0, kcat)
            ve, vo = jnp.where(lo, vro, 0.0), jnp.where(lo, 0.0, vcat)
        k2 = jnp.concatenate([ke, ko], 0).astype(BF16)
        v2 = jnp.concatenate([jnp.concatenate([ve, one_e], 1),
                              jnp.concatenate([vo, one_o], 1)], 0).astype(BF16)
        for p in range(pairs):
            hp = g * pairs + p
            s = _dot_nt(q[:, LANES * hp:LANES * (hp + 1)], k2)
            halves, corr = [], []
            for par in range(2):
                sp = s[:, par * tk:(par + 1) * tk] + bias
                sink = sinks_ref[2 * hp + par]
                m = jnp.maximum(jnp.max(sp, axis=-1, keepdims=True), sink)
                halves.append(jnp.exp(sp - m).astype(BF16))
                corr.append(jnp.exp(sink - m))
            o2 = _dot(jnp.concatenate(halves, 1), v2)
            den = o2[:, LANES:2 * LANES] + jnp.where(lo, corr[0], corr[1])
            o_ref[:, LANES * hp:LANES * (hp + 1)] = (o2[:, 0:LANES] / den).astype(o_ref.dtype)


def _attn_prompt_kernel(sinks_ref, q_ref, kp_ref, kc_ref, vp_ref, vc_ref, bias_ref, o_ref):
    kcat = jnp.concatenate([kp_ref[...], kc_ref[...]], 0)
    vcat = jnp.concatenate([vp_ref[...], vc_ref[...]], 0)
    _attn_core(q_ref[...], kcat, vcat, bias_ref[...], sinks_ref, o_ref)


def _attn_sample_kernel(sinks_ref, q_ref, k_ref, v_ref, bias_ref, o_ref):
    _attn_core(q_ref[...], k_ref[...], v_ref[...], bias_ref[...], sinks_ref, o_ref)


def _window_bias(tq, first):
    qi = jnp.arange(tq)[:, None]
    kj = jnp.arange(2 * WINDOW)[None, :]
    ok = (kj > qi) & (kj <= qi + WINDOW)
    if first:
        ok = ok & (kj >= WINDOW)
    return jnp.where(ok, 0.0, -jnp.inf).astype(F32)


def _attn_prompt_call(sinks, q, k, v, B, nb):
    n = q.shape[0]
    cur = lambda b, i: (b * nb + i, 0)
    prev = lambda b, i: (b * nb + jnp.maximum(i - 1, 0), 0)
    kvb = lambda im: pl.BlockSpec((WINDOW, LANES), im)
    bias = jnp.stack([_window_bias(WINDOW, True), _window_bias(WINDOW, False)])
    return pl.pallas_call(
        _attn_prompt_kernel,
        out_shape=jax.ShapeDtypeStruct((n, D_MODEL), BF16),
        grid=(B, nb),
        in_specs=[pl.BlockSpec(memory_space=pltpu.SMEM),
                  pl.BlockSpec((WINDOW, D_MODEL), cur), kvb(prev), kvb(cur), kvb(prev), kvb(cur),
                  pl.BlockSpec((None, WINDOW, 2 * WINDOW), lambda b, i: (jnp.minimum(i, 1), 0, 0))],
        out_specs=pl.BlockSpec((WINDOW, D_MODEL), cur),
        compiler_params=_params(2),
        name="swa_prompt",
    )(sinks, q, k, k, v, v, bias)


def _attn_sample_call(sinks, q, kcat, vcat):
    B = q.shape[0]
    b3 = lambda b: (b, 0, 0)
    return pl.pallas_call(
        _attn_sample_kernel,
        out_shape=jax.ShapeDtypeStruct(q.shape, BF16),
        grid=(B,),
        in_specs=[pl.BlockSpec(memory_space=pltpu.SMEM),
                  pl.BlockSpec((None, SAMPLE_PAD, D_MODEL), b3),
                  pl.BlockSpec((None, 2 * WINDOW, LANES), b3),
                  pl.BlockSpec((None, 2 * WINDOW, LANES), b3),
                  pl.BlockSpec((SAMPLE_PAD, 2 * WINDOW), lambda b: (0, 0))],
        out_specs=pl.BlockSpec((None, SAMPLE_PAD, D_MODEL), b3),
        compiler_params=_params(1),
        name="swa_sample",
    )(sinks, q, kcat, vcat, _window_bias(SAMPLE_PAD, False))


def _rope_tables(pos):
    half = ROT_DIM // 2
    inv = ROPE_THETA ** (-jnp.arange(half, dtype=F32) / half)
    ang = pos.astype(F32)[:, None] * inv[None]
    cos, sin = jnp.cos(ang), jnp.sin(ang)
    d = jnp.arange(LANES) % HEAD_DIM
    idx = d % half
    cos_t = jnp.where(d < ROT_DIM, cos[:, idx], 1.0)
    sa = jnp.where(d < half, -sin[:, idx], 0.0)
    sb = jnp.where((d >= half) & (d < ROT_DIM), sin[:, idx], 0.0)
    return cos_t, sa, sb


def _pad_lanes(a, value=0.0):
    return jnp.pad(a, ((0, 0), (0, LANES - a.shape[1])), constant_values=value)


def _prep_weights(ada_w, ada_b, norm_mix, norm_ffn, a_w_in, a_b_gates, a_head_norm, a_w_out,
                  kv_ada_w, kv_ada_b, kv_norm, w_k, w_v, k_norm, b_w_q, b_q_norm, b_sinks, b_w_o,
                  moe_w_group, moe_b_group, moe_w_expert, moe_b_expert, moe_w_gate_up, moe_w_down):
    w = {}
    g0 = QK_COLS + 2 * V_COLS
    win = a_w_in[0]
    w["w_in"] = jnp.concatenate([win[:, :g0], _pad_lanes(win[:, g0:g0 + M_HEADS]),
                                 _pad_lanes(win[:, g0 + M_HEADS:])], 1).astype(BF16)
    w["bli"] = _pad_lanes(a_b_gates[0][None, :M_HEADS])
    w["blf"] = _pad_lanes(a_b_gates[0][None, M_HEADS:])
    w["head_norm"] = a_head_norm[0][None]
    w["w_out"] = a_w_out[0].astype(BF16)
    w["norm_mix"] = [norm_mix[l][None] for l in range(2)]
    w["norm_ffn"] = [norm_ffn[l][None] for l in range(2)]
    w["router"] = []
    for l in range(2):
        wr = _pad_lanes(jnp.concatenate([moe_w_group[l], moe_w_expert[l]], 1))
        hi = wr.astype(BF16)
        lo = (wr - hi.astype(F32)).astype(BF16)
        br = _pad_lanes(jnp.concatenate([moe_b_group[l], moe_b_expert[l]])[None])
        w["router"].append((hi, lo, br))
    w["w_gu"] = [moe_w_gate_up[l].astype(BF16) for l in range(2)]
    w["w_d"] = [moe_w_down[l].astype(BF16) for l in range(2)]
    w["kv_norm"] = kv_norm[None]
    w["w_kv"] = jnp.concatenate([w_k, w_v], 1).astype(BF16)
    w["k_norm"] = jnp.tile(k_norm, N_KV_HEADS)[None]
    w["w_q"] = b_w_q[0].astype(BF16)
    w["q_norm"] = jnp.tile(b_q_norm[0], N_Q_HEADS)[None]
    w["sinks"] = b_sinks[0]
    w["w_o"] = b_w_o[0].astype(BF16)
    lanes = jnp.arange(LANES)
    feat = jnp.arange(D_MODEL)
    w["g64"] = jnp.where((lanes[:, None] // HEAD_DIM) == (lanes[None, :] // HEAD_DIM),
                         1.0 / HEAD_DIM, 0.0).astype(BF16)
    w["gr"] = jnp.where((feat[:, None] // HEAD_DIM) == lanes[None, :], 1.0 / HEAD_DIM, 0.0).astype(BF16)
    w["gb"] = jnp.where(lanes[:, None] == (feat[None, :] // HEAD_DIM), 1.0, 0.0).astype(BF16)
    return w


def _trunk(x2, mods, kvmod, mspec, mspec_moe, w, *, B, T, L, tm, tm_moe, tabs, tabspec, c0, n0, m0,
           cache_k=None, cache_v=None):
    sample = cache_k is not None
    nc = T // L if not sample else 1

    qk, v, o, li, lf = _inproj_call(x2, mods[0], mspec, w["norm_mix"][0], w["w_in"],
                                    w["bli"], w["blf"], tm)
    if sample:
        def padtok(a, value=0.0):
            a = a.reshape(B, T, a.shape[-1])
            a = jnp.pad(a, ((0, 0), (0, L - T), (0, 0)), constant_values=value)
            return a.reshape(B * L, a.shape[-1])
        qk, v, li, lf = padtok(qk), padtok(v), padtok(li, M_EMPTY), padtok(lf)
    h, c_new, n_new, m_new = _mlstm_call(qk, v, li, lf, c0, n0, m0, B, nc, L)
    if sample:
        h = h.reshape(B, L, V_COLS)[:, :T].reshape(B * T, V_COLS)
    routed = not sample
    ntiles = (B * T) // tm

    def routed_moe(u2, gate, gid, col, layer):
        q, runs, items = _sort_meta_call(gid.reshape(ntiles, tm), tm)
        su, sg = _scatter_call(runs, q.reshape(ntiles, 1, tm), u2, gate, tm)
        ys = _moe_routed_call(items, su, sg, w["w_gu"][layer], w["w_d"][layer], tm)
        return runs, col, ys

    rh, rl, br = w["router"][0]
    post0 = _post_call(True, routed, h, o, x2, mods[0], mspec, w["norm_ffn"][0], w["head_norm"],
                       w["w_out"], rh, rl, br, tm)
    kvq_args = (mods[0], mods[1], mspec, kvmod, mspec, w["kv_norm"], w["norm_mix"][1],
                w["w_kv"], w["w_q"], w["k_norm"], w["q_norm"], tabs, tabspec,
                w["g64"], w["gr"], w["gb"], tm)

    if routed:
        x1, u2, gate, gid, col = post0
        q, k, vv, xa = _kvq_call(x1, routed_moe(u2, gate, gid, col, 0), *kvq_args)
    else:
        x1, u2, gate = post0
        xa = _moe_call(u2, gate, w["w_gu"][0], w["w_d"][0], x1, mods[0], mspec_moe, tm_moe)
        q, k, vv = _kvq_call(xa, None, *kvq_args)
    if not sample:
        att = _attn_prompt_call(w["sinks"], q, k, vv, B, T // WINDOW)
        k_win = k.reshape(B, T, N_KV_HEADS, HEAD_DIM)[:, T - WINDOW:]
        v_win = vv.reshape(B, T, N_KV_HEADS, HEAD_DIM)[:, T - WINDOW:]
    else:
        zpad = jnp.zeros((B, WINDOW - T, LANES), F32)
        kcat = jnp.concatenate([cache_k.reshape(B, WINDOW, LANES), k.reshape(B, T, LANES), zpad], 1)
        vcat = jnp.concatenate([cache_v.reshape(B, WINDOW, LANES), vv.reshape(B, T, LANES), zpad], 1)
        qp = jnp.pad(q.reshape(B, T, D_MODEL), ((0, 0), (0, SAMPLE_PAD - T), (0, 0)))
        att = _attn_sample_call(w["sinks"], qp, kcat, vcat)[:, :T].reshape(B * T, D_MODEL)
        k_win = kcat[:, T:T + WINDOW].reshape(B, WINDOW, N_KV_HEADS, HEAD_DIM)
        v_win = vcat[:, T:T + WINDOW].reshape(B, WINDOW, N_KV_HEADS, HEAD_DIM)
    rh, rl, br = w["router"][1]
    post1 = _post_call(False, routed, att, None, xa, mods[1], mspec, w["norm_ffn"][1], None,
                       w["w_o"], rh, rl, br, tm)
    if routed:
        x3, u4, gate, gid, col = post1
        runs, col, ys = routed_moe(u4, gate, gid, col, 1)
        y = _resid_gather_call(x3, runs, col, ys, mods[1], mspec, tm)
    else:
        x3, u4, gate = post1
        y = _moe_call(u4, gate, w["w_gu"][1], w["w_d"][1], x3, mods[1], mspec_moe, tm_moe)
    c_out = c_new[None]
    n_out = n_new.reshape(1, B, M_HEADS, M_DK)
    m_out = m_new[:, 0, :M_HEADS][None]
    return y, c_out, n_out, m_out, k_win, v_win


def kernel(x_prompt, x_sample, c_prompt, c_sample, state_c, state_n, state_m, cache_k_win, cache_v_win, ada_w, ada_b, norm_mix, norm_ffn, a_w_in, a_b_gates, a_head_norm, a_w_out, kv_ada_w, kv_ada_b, kv_norm, w_k, w_v, k_norm, b_w_q, b_q_norm, b_sinks, b_w_o, moe_w_group, moe_b_group, moe_w_expert, moe_b_expert, moe_w_gate_up, moe_w_down):
    Bp, Tp, D = x_prompt.shape
    Bs, Ts, _ = x_sample.shape
    w = _prep_weights(ada_w, ada_b, norm_mix, norm_ffn, a_w_in, a_b_gates, a_head_norm, a_w_out,
                      kv_ada_w, kv_ada_b, kv_norm, w_k, w_v, k_norm, b_w_q, b_q_norm, b_sinks, b_w_o,
                      moe_w_group, moe_b_group, moe_w_expert, moe_b_expert, moe_w_gate_up, moe_w_down)

    rows = Bp + Bs
    rpad = -rows % 8
    c_all = jnp.concatenate([c_prompt, c_sample, jnp.zeros((rpad, D), F32)], 0)
    mod = _ada_call(c_all, ada_w, ada_b[:, None, :])
    kvm = _ada_call(c_all, kv_ada_w[None], kv_ada_b[None, None, :])

    tm_p = 512
    tiles_per_seq = Tp // tm_p
    mods_p = [mod[l, :Bp][:, None, :] for l in range(2)]
    kvmod_p = kvm[0, :Bp][:, None, :]

    def mspec_p(col):
        return pl.BlockSpec((None, 1, D_MODEL), lambda i, *_: (i // tiles_per_seq, 0, col))

    tm_moe_p = 1024
    moe_tiles_per_seq = Tp // tm_moe_p

    def mspec_moe_p(col):
        return pl.BlockSpec((None, 1, D_MODEL), lambda i, *_: (i // moe_tiles_per_seq, 0, col))

    tabs_p = _rope_tables(jnp.arange(Tp, dtype=jnp.int32))
    tabspec_p = pl.BlockSpec((tm_p, LANES), lambda i, *_: (i % tiles_per_seq, 0))
    npairs = M_HEADS // 2
    c0 = jnp.zeros((Bp, M_HEADS, M_DK, M_DV), F32)
    n0 = jnp.zeros((Bp, npairs, LANES), F32)
    m0 = jnp.pad(jnp.full((Bp, 1, M_HEADS), M_EMPTY, F32), ((0, 0), (0, 0), (0, LANES - M_HEADS)))
    yp, cp, np_, mp, kwp, vwp = _trunk(
        x_prompt.reshape(Bp * Tp, D), mods_p, kvmod_p, mspec_p, mspec_moe_p, w,
        B=Bp, T=Tp, L=M_CHUNK, tm=tm_p, tm_moe=tm_moe_p, tabs=tabs_p, tabspec=tabspec_p,
        c0=c0, n0=n0, m0=m0)

    ns = Bs * Ts
    mods_s = [jnp.repeat(mod[l, Bp:Bp + Bs], Ts, axis=0)[None] for l in range(2)]
    kvmod_s = jnp.repeat(kvm[0, Bp:Bp + Bs], Ts, axis=0)[None]

    def mspec_s(col):
        return pl.BlockSpec((None, ns, D_MODEL), lambda i, *_: (0, 0, col))

    tabs_s = _rope_tables(PAST_LEN + jnp.arange(Ts, dtype=jnp.int32))
    tabs_s = tuple(jnp.tile(t, (Bs, 1)) for t in tabs_s)
    m0s = jnp.pad(state_m[0][:, None, :], ((0, 0), (0, 0), (0, LANES - M_HEADS)))
    ys, cs, ns_, ms, kws, vws = _trunk(
        x_sample.reshape(ns, D), mods_s, kvmod_s, mspec_s, mspec_s, w,
        B=Bs, T=Ts, L=SAMPLE_PAD, tm=ns, tm_moe=ns, tabs=tabs_s,
        tabspec=pl.BlockSpec((ns, LANES), lambda i, *_: (0, 0)),
        c0=state_c[0], n0=state_n[0].reshape(Bs, npairs, LANES), m0=m0s,
        cache_k=cache_k_win, cache_v=cache_v_win)

    return (yp.reshape(Bp, Tp, D), ys.reshape(Bs, Ts, D), cp, np_, mp, kwp, vwp,
            cs, ns_, ms, kws, vws)
```

```python
import functools

import jax
import jax.numpy as jnp
from jax import lax
from jax.experimental import pallas as pl
from jax.experimental.pallas import tpu as pltpu

F32 = jnp.float32
BF16 = jnp.bfloat16

D_MODEL = 1024
PAST_LEN = 8192
M_HEADS = 8
M_DK = 64
M_DV = 128
M_CHUNK = 64
GATE_SOFTCAP = 15.0
M_EMPTY = -1e30
WINDOW = 128
HEAD_DIM = 64
N_Q_HEADS = 16
N_KV_HEADS = 2
ROPE_THETA = 500000.0
ROT_DIM = 16
N_GROUPS = 4
EXPERTS_PER_GROUP = 4
N_EXPERTS = 16
D_EXPERT = 256
RMS_EPS = 1e-6

LANES = 128
QK_COLS = 2 * M_HEADS * M_DK
V_COLS = M_HEADS * M_DV
IN_COLS = QK_COLS + 2 * V_COLS + 2 * LANES
SAMPLE_PAD = 16
VMEM_LIMIT = 52 * 1024 * 1024

NT_DIMS = (((1,), (1,)), ((), ()))


def _params(n_axes):
    return pltpu.CompilerParams(dimension_semantics=("arbitrary",) * n_axes,
                                vmem_limit_bytes=VMEM_LIMIT)


def _dot(a, b):
    return jnp.dot(a, b, preferred_element_type=F32)


def _dot_nt(a, b):
    return lax.dot_general(a, b, NT_DIMS, preferred_element_type=F32)


def _split2(x):
    hi = x.astype(BF16)
    lo = (x - hi.astype(F32)).astype(BF16)
    return hi, lo


def _split3(x):
    hi = x.astype(BF16)
    r = x - hi.astype(F32)
    mid = r.astype(BF16)
    lo = (r - mid.astype(F32)).astype(BF16)
    return hi, mid, lo


def _unit_rms(x):
    return x * lax.rsqrt(jnp.mean(x * x, axis=-1, keepdims=True) + RMS_EPS)


def _ada_kernel(c_ref, w_ref, b_ref, o_ref):
    c = c_ref[...]
    cs = (c * jax.nn.sigmoid(c)).astype(BF16)
    o_ref[...] = _dot(cs, w_ref[...].astype(BF16)) + b_ref[...]


def _ada_call(c, w, b):
    g, d, n = w.shape
    r = c.shape[0]
    tn = 1024
    return pl.pallas_call(
        _ada_kernel,
        out_shape=jax.ShapeDtypeStruct((g, r, n), F32),
        grid=(g, n // tn),
        in_specs=[pl.BlockSpec((r, d), lambda i, j: (0, 0)),
                  pl.BlockSpec((None, d, tn), lambda i, j: (i, 0, j)),
                  pl.BlockSpec((None, 1, tn), lambda i, j: (i, 0, j))],
        out_specs=pl.BlockSpec((None, r, tn), lambda i, j: (i, 0, j)),
        compiler_params=_params(2),
        name="ada_mod",
    )(c, w, b)


def _inproj_kernel(x_ref, sh_ref, sc_ref, nw_ref, w_ref, bli_ref, blf_ref,
                   qk_ref, v_ref, o_ref, li_ref, lf_ref):
    u = _unit_rms(x_ref[...]) * nw_ref[...] * (1.0 + sc_ref[...]) + sh_ref[...]
    ub = u.astype(BF16)
    half = QK_COLS // 2
    q = _dot(ub, w_ref[:, 0:half]) * (M_DK ** -0.5)
    qk_ref[:, 0:half] = q.astype(BF16)
    qk_ref[:, half:QK_COLS] = _dot(ub, w_ref[:, half:QK_COLS]).astype(BF16)
    v_ref[...] = _dot(ub, w_ref[:, QK_COLS:QK_COLS + V_COLS]).astype(BF16)
    o_ref[...] = _dot(ub, w_ref[:, QK_COLS + V_COLS:QK_COLS + 2 * V_COLS]).astype(BF16)
    g0 = QK_COLS + 2 * V_COLS
    lane = lax.broadcasted_iota(jnp.int32, (1, LANES), 1)
    live = lane < M_HEADS
    gi = _dot(ub, w_ref[:, g0:g0 + LANES]) + bli_ref[...]
    gf = _dot(ub, w_ref[:, g0 + LANES:g0 + 2 * LANES]) + blf_ref[...]
    li = GATE_SOFTCAP * jnp.tanh(gi / GATE_SOFTCAP)
    fpre = GATE_SOFTCAP * jnp.tanh(gf / GATE_SOFTCAP)
    lf = jnp.minimum(fpre, 0.0) - jnp.log1p(jnp.exp(-jnp.abs(fpre)))
    li_ref[...] = jnp.where(live, li, 0.0)
    lf_ref[...] = jnp.where(live, lf, 0.0)


def _inproj_call(x, mod, mspec, nw, w, bli, blf, tm):
    n = x.shape[0]
    row = lambda i: (i, 0)
    const = lambda i: (0, 0)
    return pl.pallas_call(
        _inproj_kernel,
        out_shape=(jax.ShapeDtypeStruct((n, QK_COLS), BF16),
                   jax.ShapeDtypeStruct((n, V_COLS), BF16),
                   jax.ShapeDtypeStruct((n, V_COLS), BF16),
                   jax.ShapeDtypeStruct((n, LANES), F32),
                   jax.ShapeDtypeStruct((n, LANES), F32)),
        grid=(n // tm,),
        in_specs=[pl.BlockSpec((tm, D_MODEL), row), mspec(0), mspec(1),
                  pl.BlockSpec((1, D_MODEL), const),
                  pl.BlockSpec((D_MODEL, IN_COLS), const),
                  pl.BlockSpec((1, LANES), const), pl.BlockSpec((1, LANES), const)],
        out_specs=(pl.BlockSpec((tm, QK_COLS), row), pl.BlockSpec((tm, V_COLS), row),
                   pl.BlockSpec((tm, V_COLS), row), pl.BlockSpec((tm, LANES), row),
                   pl.BlockSpec((tm, LANES), row)),
        compiler_params=_params(1),
        name="mlstm_inproj",
    )(x, mod, mod, nw, w, bli, blf)


def _mlstm_kernel(L, nseq, *refs):
    c = pl.program_id(1)
    seqs = [[r.at[b] for r in refs] for b in range(nseq)]

    @pl.when(c == 0)
    def _():
        for s in seqs:
            _mlstm_load_state(*s[4:7], *s[11:14])

    for s in seqs:
        _mlstm_seq_step(L, *s[0:4], s[7], *s[11:14])

    @pl.when(c == pl.num_programs(1) - 1)
    def _():
        for s in seqs:
            _mlstm_store_state(*s[8:14])


def _mlstm_load_state(c0_ref, n0_ref, m0_ref, c2_s, n_s, m_s):
    m_s[...] = m0_ref[...]
    n_s[...] = n0_ref[...]
    z = jnp.zeros((M_DK, M_DV), F32)
    for j in range(M_HEADS // 2):
        c2_s[j] = jnp.concatenate([jnp.concatenate([c0_ref[2 * j], z], 1),
                                   jnp.concatenate([z, c0_ref[2 * j + 1]], 1)], 0)


def _mlstm_seq_step(L, qk_ref, v_ref, li_ref, lf_ref, h_ref, c2_s, n_s, m_s):
    npairs = M_HEADS // 2
    hd = M_DK

    lane = lax.broadcasted_iota(jnp.int32, (1, LANES), 1)
    lo128 = lane < hd
    lane256 = lax.broadcasted_iota(jnp.int32, (1, 2 * LANES), 1)
    lo256 = lane256 < LANES
    row128 = lax.broadcasted_iota(jnp.int32, (LANES, 1), 0)
    top = row128 < hd
    blockdiag = (top & lo256) | (jnp.logical_not(top) & jnp.logical_not(lo256))

    LI = li_ref[...]
    LF = lf_ref[...]
    rowL = lax.broadcasted_iota(jnp.int32, (L, LANES), 0)

    def prefix(x, op, ident):
        d = 1
        while d < L:
            shifted = pltpu.roll(x, d, axis=0)
            x = op(x, jnp.where(rowL >= d, shifted, ident))
            d *= 2
        return x

    Bc = prefix(LF, jnp.add, 0.0)
    Cm = LI - Bc
    mprev = m_s[...]
    Gc = jnp.maximum(mprev, prefix(Cm, jnp.maximum, -jnp.inf))
    A = jnp.exp(mprev - Gc)
    EXPM = jnp.exp(-(Bc + Gc))
    bL = Bc[L - 1:L, :]
    DL = bL + Cm
    mnew = jnp.maximum(bL + mprev, jnp.max(DL, axis=0, keepdims=True))
    ast = jnp.exp(bL + mprev - mnew)
    WST = jnp.exp(DL - mnew)

    def pad_rows(x, rows):
        if x.shape[0] == rows:
            return x
        return jnp.concatenate([x, jnp.zeros((rows - x.shape[0], x.shape[1]), x.dtype)], 0)

    cm_pad = pad_rows(Cm, hd)
    XT = jnp.concatenate([cm_pad, cm_pad], 0).T
    causal = (lane & (hd - 1)) <= rowL
    J = jnp.where(blockdiag, 1.0, 0.0).astype(BF16)

    def bc(X, h):
        return jnp.broadcast_to(X[:, h:h + 1], X.shape)

    for j in range(npairs):
        he, ho = 2 * j, 2 * j + 1

        def pair128(X):
            return jnp.where(lo128, bc(X, he), bc(X, ho))

        def pair256(X):
            return jnp.concatenate([bc(X, he), bc(X, ho)], 1)

        q128 = qk_ref[:, LANES * j:LANES * (j + 1)]
        k128 = qk_ref[:, QK_COLS // 2 + LANES * j:QK_COLS // 2 + LANES * (j + 1)]
        v256 = v_ref[:, 2 * LANES * j:2 * LANES * (j + 1)]
        zk = jnp.zeros_like(k128)
        zv = jnp.zeros_like(v256)
        K2t = jnp.concatenate([pad_rows(jnp.where(lo128, k128, zk), hd),
                               pad_rows(jnp.where(lo128, zk, k128), hd)], 0)
        V2 = jnp.concatenate([pad_rows(jnp.where(lo256, v256, zv), hd),
                              pad_rows(jnp.where(lo256, zv, v256), hd)], 0)
        S = _dot_nt(q128, K2t)
        crow = jnp.where(lo128, XT[he:he + 1, :], XT[ho:ho + 1, :])
        arg = jnp.where(causal, crow - pair128(Gc), -jnp.inf)
        Sw = (S * jnp.exp(arg)).astype(BF16)
        num_intra = _dot(Sw, V2)
        rowsum = _dot(Sw, J)
        C2 = c2_s[j]
        inter = _dot(q128, C2.astype(BF16))
        npair = n_s[j:j + 1, :]
        Nrow2 = jnp.concatenate(
            [jnp.broadcast_to(jnp.where(lo128, npair, 0.0), (LANES, LANES)),
             jnp.broadcast_to(jnp.where(lo128, 0.0, npair), (LANES, LANES))], 0).astype(BF16)
        qn = _dot_nt(q128, Nrow2)
        a256 = pair256(A)
        num = a256 * inter + num_intra
        den = a256 * qn + rowsum
        h = num / jnp.maximum(jnp.abs(den), pair256(EXPM))
        h_ref[:, 2 * LANES * j:2 * LANES * (j + 1)] = h.astype(h_ref.dtype)

        kw = k128.astype(F32) * pair128(WST)
        n_s[j:j + 1, :] = pair128(ast) * npair + jnp.sum(kw, axis=0, keepdims=True)
        kwT = pad_rows(kw, LANES).T.astype(BF16)
        dC = _dot(kwT, pad_rows(v256, LANES))
        c2_s[j] = pair256(ast) * C2 + jnp.where(blockdiag, dC, 0.0)

    m_s[...] = mnew


def _mlstm_store_state(cout_ref, nout_ref, mout_ref, c2_s, n_s, m_s):
    for j in range(M_HEADS // 2):
        C2 = c2_s[j]
        cout_ref[2 * j] = C2[0:M_DK, 0:M_DV]
        cout_ref[2 * j + 1] = C2[M_DK:2 * M_DK, M_DV:2 * M_DV]
    nout_ref[...] = n_s[...]
    mout_ref[...] = m_s[...]


MLSTM_SEQS_PER_STEP = 4


def _mlstm_call(qk, v, li, lf, c0, n0, m0, B, nc, L):
    npairs = M_HEADS // 2
    nseq = MLSTM_SEQS_PER_STEP
    T = nc * L
    tok3 = lambda a: a.reshape(B, T, a.shape[-1])
    tok = lambda b, c: (b, c, 0)
    st4 = lambda b, c: (b, 0, 0, 0)
    st3 = lambda b, c: (b, 0, 0)
    h, c_new, n_new, m_new = pl.pallas_call(
        functools.partial(_mlstm_kernel, L, nseq),
        out_shape=(jax.ShapeDtypeStruct((B, T, V_COLS), BF16),
                   jax.ShapeDtypeStruct((B, M_HEADS, M_DK, M_DV), F32),
                   jax.ShapeDtypeStruct((B, npairs, LANES), F32),
                   jax.ShapeDtypeStruct((B, 1, LANES), F32)),
        grid=(B // nseq, nc),
        in_specs=[pl.BlockSpec((nseq, L, QK_COLS), tok), pl.BlockSpec((nseq, L, V_COLS), tok),
                  pl.BlockSpec((nseq, L, LANES), tok), pl.BlockSpec((nseq, L, LANES), tok),
                  pl.BlockSpec((nseq, M_HEADS, M_DK, M_DV), st4),
                  pl.BlockSpec((nseq, npairs, LANES), st3),
                  pl.BlockSpec((nseq, 1, LANES), st3)],
        out_specs=(pl.BlockSpec((nseq, L, V_COLS), tok),
                   pl.BlockSpec((nseq, M_HEADS, M_DK, M_DV), st4),
                   pl.BlockSpec((nseq, npairs, LANES), st3),
                   pl.BlockSpec((nseq, 1, LANES), st3)),
        scratch_shapes=[pltpu.VMEM((nseq, npairs, 2 * M_DK, 2 * M_DV), F32),
                        pltpu.VMEM((nseq, npairs, LANES), F32),
                        pltpu.VMEM((nseq, 1, LANES), F32)],
        compiler_params=_params(2),
        name="mlstm_chunks",
    )(tok3(qk), tok3(v), tok3(li), tok3(lf), c0, n0, m0)
    return h.reshape(B * T, V_COLS), c_new, n_new, m_new


def _route(lg):
    lane = lax.broadcasted_iota(jnp.int32, lg.shape, 1)
    lanef = lane.astype(F32)
    neg = -jnp.inf
    far = float(LANES)
    gm = lane < N_GROUPS
    lgm = jnp.where(gm, lg, neg)
    gmax = jnp.max(lgm, axis=-1, keepdims=True)
    gsum = jnp.sum(jnp.exp(lgm - gmax), axis=-1, keepdims=True)
    g_w = 1.0 / gsum
    gidx = jnp.min(jnp.where(gm & (lg == gmax), lanef, far), axis=-1, keepdims=True)
    first = N_GROUPS + EXPERTS_PER_GROUP * gidx
    sel = (lanef >= first) & (lanef < first + EXPERTS_PER_GROUP)
    l1 = jnp.max(jnp.where(sel, lg, neg), axis=-1, keepdims=True)
    i1 = jnp.min(jnp.where(sel & (lg == l1), lanef, far), axis=-1, keepdims=True)
    sel2 = sel & (lanef != i1)
    l2 = jnp.max(jnp.where(sel2, lg, neg), axis=-1, keepdims=True)
    i2 = jnp.min(jnp.where(sel2 & (lg == l2), lanef, far), axis=-1, keepdims=True)
    r = jnp.exp(l2 - l1)
    w1 = g_w / (1.0 + r)
    w2 = w1 * r
    return jnp.where(lanef == i1, w1, jnp.where(lanef == i2, w2, 0.0)), gidx


def _post_kernel(mlstm, routed, *refs):
    refs = list(refs)
    h_ref = refs.pop(0)
    o_ref = refs.pop(0) if mlstm else None
    x_ref, g1_ref, sh2_ref, sc2_ref, nf_ref = refs[:5]
    refs = refs[5:]
    hn_ref = refs.pop(0) if mlstm else None
    wout_ref, wrh_ref, wrl_ref, br_ref = refs[:4]
    refs = refs[4:]
    tri_ref = refs.pop(0) if routed else None
    x1_ref, u2_ref, gate_ref = refs[:3]
    outs = refs[3:]
    if mlstm:
        hf = h_ref[...].astype(F32)
        parts = [_unit_rms(hf[:, M_DV * i:M_DV * (i + 1)]) for i in range(M_HEADS)]
        hn = jnp.concatenate(parts, 1) * hn_ref[...]
        hg = (hn * jax.nn.sigmoid(o_ref[...].astype(F32))).astype(BF16)
    else:
        hg = h_ref[...]
    x1 = x_ref[...] + g1_ref[...] * _dot(hg, wout_ref[...])
    x1_ref[...] = x1
    u2 = _unit_rms(x1) * nf_ref[...] * (1.0 + sc2_ref[...]) + sh2_ref[...]
    uh, ul = _split2(u2)
    lg = _dot(uh, wrh_ref[...]) + _dot(ul, wrh_ref[...]) + _dot(uh, wrl_ref[...]) + br_ref[...]
    gate, gidx = _route(lg)
    u2_ref[...] = uh
    gate_ref[...] = gate
    if not routed:
        return
    gid_ref, col_ref = outs
    rows = []
    for blk in range(x1.shape[0] // LANES):
        col = jnp.broadcast_to(gidx[LANES * blk:LANES * (blk + 1), :], (LANES, LANES))
        rows.append(col.T[0:1, :])
    gid_ref[...] = jnp.concatenate(rows, 1)
    lanef = lax.broadcasted_iota(jnp.int32, gate.shape, 1).astype(F32)
    onehot = jnp.where(lanef == gidx, 1.0, 0.0)
    earlier = _dot(tri_ref[...], onehot.astype(BF16))
    rank = jnp.sum(onehot * earlier, axis=-1, keepdims=True)
    col_ref[...] = jnp.where(lanef == 0.0, gidx, jnp.where(lanef == 1.0, rank, 0.0))


def _post_call(mlstm, routed, h, o, x, mod, mspec, nf, hn, wout, wrh, wrl, br, tm):
    n = x.shape[0]
    row = lambda i: (i, 0)
    const = lambda i: (0, 0)
    tok = pl.BlockSpec((tm, D_MODEL), row)
    small = pl.BlockSpec((tm, LANES), row)
    vec = pl.BlockSpec((1, D_MODEL), const)
    ins = [h] + ([o] if mlstm else []) + [x, mod, mod, mod, nf] + ([hn] if mlstm else []) \
        + [wout, wrh, wrl, br]
    specs = [tok] + ([tok] if mlstm else []) + [tok, mspec(2), mspec(3), mspec(4), vec] \
        + ([vec] if mlstm else []) \
        + [pl.BlockSpec((D_MODEL, D_MODEL), const), pl.BlockSpec((D_MODEL, LANES), const),
           pl.BlockSpec((D_MODEL, LANES), const), pl.BlockSpec((1, LANES), const)]
    out_shape = [jax.ShapeDtypeStruct((n, D_MODEL), F32), jax.ShapeDtypeStruct((n, D_MODEL), BF16),
                 jax.ShapeDtypeStruct((n, LANES), F32)]
    out_specs = [tok, tok, small]
    if routed:
        t = jnp.arange(tm)
        ins.append((t[None, :] < t[:, None]).astype(BF16))
        specs.append(pl.BlockSpec((tm, tm), const))
        out_shape += [jax.ShapeDtypeStruct((n // tm, 1, tm), F32), jax.ShapeDtypeStruct((n, LANES), F32)]
        out_specs += [pl.BlockSpec((None, 1, tm), lambda i: (i, 0, 0)), small]
    out_shape, out_specs = tuple(out_shape), tuple(out_specs)
    return pl.pallas_call(
        functools.partial(_post_kernel, mlstm, routed),
        out_shape=out_shape,
        grid=(n // tm,),
        in_specs=specs,
        out_specs=out_specs,
        compiler_params=_params(1),
        name="post_mlstm" if mlstm else "post_attn",
    )(*ins)


ITEM_ROWS = 8
RUN_ALIGN = 16
RUN_SIZES = (512, 256, 128, 64, 32, 16)
RUN_SRC, RUN_DST, RUN_LEN, RUN_TOTAL = 0, N_GROUPS, 2 * N_GROUPS, 3 * N_GROUPS


def _stage_rows(tm):
    assert N_GROUPS * (RUN_ALIGN - 1) <= LANES
    return tm + LANES


def _sorted_rows(n, tm):
    rows = n + N_GROUPS * (n // tm) * RUN_ALIGN + tm
    return -(-rows // tm) * tm


def _sort_meta_kernel(tm, nts, gid_ref, q_ref, runs_ref, items_ref):
    ntiles = gid_ref.shape[0]
    gid = gid_ref[...]
    r = lax.broadcasted_iota(jnp.int32, (tm, tm), 0)
    c = lax.broadcasted_iota(jnp.int32, (tm, tm), 1)
    before = jnp.where(r < c, 1.0, 0.0).astype(BF16)
    trow = lax.broadcasted_iota(jnp.int32, (ntiles, LANES), 0)
    lane = lax.broadcasted_iota(jnp.int32, (ntiles, LANES), 1)
    k = lax.broadcasted_iota(jnp.int32, (1, LANES), 1).astype(F32)
    zero11 = jnp.zeros((1, 1), F32)
    q = jnp.zeros((ntiles, tm), F32)
    runs = jnp.zeros((ntiles, LANES), F32)
    src = jnp.zeros((ntiles, 1), F32)
    start, nitems = zero11, zero11
    grp = jnp.zeros((1, LANES), F32)
    tile = jnp.zeros((1, LANES), F32)
    valid = jnp.zeros((1, LANES), F32)
    for g in range(N_GROUPS):
        mask = jnp.where(gid == float(g), 1.0, 0.0)
        cnt = jnp.sum(mask, axis=1, keepdims=True)
        padded = jnp.floor((cnt + (RUN_ALIGN - 1.0)) * (1.0 / RUN_ALIGN)) * RUN_ALIGN
        incl = jnp.broadcast_to(padded, (ntiles, LANES))
        d = 1
        while d < ntiles:
            incl = incl + jnp.where(trow >= d, pltpu.roll(incl, d, axis=0), 0.0)
            d *= 2
        total = incl[ntiles - 1:ntiles, 0:1]
        dst = start + incl[:, 0:1] - padded
        q = q + mask * (src + _dot(mask.astype(BF16), before))
        runs = runs + jnp.where(lane == RUN_SRC + g, src, 0.0) + jnp.where(lane == RUN_DST + g, dst, 0.0) \
            + jnp.where(lane == RUN_LEN + g, padded, 0.0)
        src = src + padded
        end = start + total
        ft = jnp.floor(start * (1.0 / tm))
        lt = jnp.floor((end - 1.0) * (1.0 / tm))
        ni = jnp.where(total > 0.0, lt - ft + 1.0, 0.0)
        inside = (k >= nitems) & (k < nitems + ni)
        grp = grp + jnp.where(inside, float(g), 0.0)
        tile = tile + jnp.where(inside, ft + (k - nitems), 0.0)
        valid = valid + jnp.where(inside, 1.0, 0.0)
        nitems = nitems + ni
        start = end
    runs = runs + jnp.where(lane == RUN_TOTAL, start, 0.0)
    live = valid > 0.0
    prev = pltpu.roll(tile, 1, axis=1)
    nxt = pltpu.roll(tile, LANES - 1, axis=1)
    first = live & ((k == 0.0) | (tile != prev))
    last = live & ((k == nitems - 1.0) | (tile != nxt))
    spare = jnp.floor((start - 1.0) * (1.0 / tm)) + 1.0 + (k - nitems)
    fill = jnp.logical_not(live) & (spare <= nts - 1.0)
    tile = jnp.where(live, tile, jnp.minimum(spare, nts - 1.0))
    grp = jnp.where(live, grp, jnp.max(grp, axis=1, keepdims=True))
    flag = lambda m: jnp.where(m, 1.0, 0.0)
    q_ref[...] = q
    runs_ref[...] = runs.astype(jnp.int32)
    table = jnp.concatenate([tile, grp, valid, flag(first | fill), flag(last | fill),
                             jnp.zeros((ITEM_ROWS - 5, LANES), F32)], 0)
    items_ref[...] = table.astype(jnp.int32)


def _sort_meta_call(gid, tm):
    ntiles = gid.shape[0]
    nts = _sorted_rows(ntiles * tm, tm) // tm
    assert nts + N_GROUPS - 1 <= LANES and tm <= RUN_SIZES[0]
    return pl.pallas_call(
        functools.partial(_sort_meta_kernel, tm, nts),
        out_shape=(jax.ShapeDtypeStruct((ntiles, tm), F32),
                   jax.ShapeDtypeStruct((ntiles, LANES), jnp.int32),
                   jax.ShapeDtypeStruct((ITEM_ROWS, LANES), jnp.int32)),
        compiler_params=_params(0),
        name="moe_sort_meta",
    )(gid)


def _run_copies(runs_ref, i, tile_ref, sorted_hbm, sem, to_sorted):
    pieces = []
    for g in range(N_GROUPS):
        src = runs_ref[i, RUN_SRC + g]
        dst = runs_ref[i, RUN_DST + g]
        length = runs_ref[i, RUN_LEN + g]
        for s in RUN_SIZES:
            def build(src=src, dst=dst, length=length, s=s):
                off = length & (-2 * s)
                a = tile_ref.at[pl.ds(pl.multiple_of(src + off, RUN_ALIGN), s), :]
                b = sorted_hbm.at[pl.ds(pl.multiple_of(dst + off, RUN_ALIGN), s), :]
                return pltpu.make_async_copy(a, b, sem) if to_sorted else pltpu.make_async_copy(b, a, sem)
            pieces.append(((length & s) != 0, build))
    return pieces


def _start(pieces):
    for pred, build in pieces:
        pl.when(pred)(lambda build=build: build().start())


def _wait(pieces):
    for pred, build in pieces:
        pl.when(pred)(lambda build=build: build().wait())


def _start_then_wait(pieces):
    _start(pieces)
    _wait(pieces)


def _scatter_kernel(tm, nts, ntiles, runs_ref, q_ref, u_ref, g_ref, su_hbm, sg_hbm,
                    stu, stg, sem_u, sem_g):
    i = pl.program_id(0)
    slot = i % 2

    def copies(tile, slot):
        return (_run_copies(runs_ref, tile, stu.at[slot], su_hbm, sem_u.at[slot], True)
                + _run_copies(runs_ref, tile, stg.at[slot], sg_hbm, sem_g.at[slot], True))

    r = lax.broadcasted_iota(jnp.int32, (_stage_rows(tm), tm), 0).astype(F32)
    perm = jnp.where(q_ref[...] == r, 1.0, 0.0).astype(BF16)
    stu[slot] = _dot(perm, u_ref[...]).astype(BF16)
    g3 = _split3(g_ref[...])
    stg[slot] = _dot(perm, g3[0]) + _dot(perm, g3[1]) + _dot(perm, g3[2])
    _start(copies(i, slot))

    @pl.when(i > 0)
    def _():
        _wait(copies(i - 1, 1 - slot))

    @pl.when(i == ntiles - 1)
    def _():
        _wait(copies(i, slot))
        stu[slot] = jnp.zeros(stu.shape[1:], stu.dtype)
        stg[slot] = jnp.zeros(stg.shape[1:], stg.dtype)
        total = runs_ref[i, RUN_TOTAL]
        tail = nts * tm - total
        nfull = tail // tm
        pieces = []
        for tile_ref, hbm, sem in ((stu.at[slot], su_hbm, sem_u.at[slot]),
                                   (stg.at[slot], sg_hbm, sem_g.at[slot])):
            for j in range(nts - ntiles):
                def full(j=j, tile_ref=tile_ref, hbm=hbm, sem=sem):
                    dst = pl.multiple_of(total + j * tm, RUN_ALIGN)
                    return pltpu.make_async_copy(tile_ref.at[pl.ds(0, tm), :], hbm.at[pl.ds(dst, tm), :], sem)
                pieces.append((j < nfull, full))
            rem = tail - nfull * tm
            for s in RUN_SIZES:
                if s >= tm:
                    continue
                def part(s=s, tile_ref=tile_ref, hbm=hbm, sem=sem):
                    dst = pl.multiple_of(total + nfull * tm + (rem & (-2 * s)), RUN_ALIGN)
                    return pltpu.make_async_copy(tile_ref.at[pl.ds(0, s), :], hbm.at[pl.ds(dst, s), :], sem)
                pieces.append(((rem & s) != 0, part))
        _start_then_wait(pieces)


def _scatter_call(runs, q3, u2, gate, tm):
    n = u2.shape[0]
    ns = _sorted_rows(n, tm)
    grid_spec = pltpu.PrefetchScalarGridSpec(
        num_scalar_prefetch=1,
        grid=(n // tm,),
        in_specs=[pl.BlockSpec((None, 1, tm), lambda i, *_: (i, 0, 0)),
                  pl.BlockSpec((tm, D_MODEL), lambda i, *_: (i, 0)),
                  pl.BlockSpec((tm, LANES), lambda i, *_: (i, 0))],
        out_specs=(pl.BlockSpec(memory_space=pl.ANY), pl.BlockSpec(memory_space=pl.ANY)),
        scratch_shapes=[pltpu.VMEM((2, _stage_rows(tm), D_MODEL), BF16),
                        pltpu.VMEM((2, _stage_rows(tm), LANES), F32),
                        pltpu.SemaphoreType.DMA((2,)), pltpu.SemaphoreType.DMA((2,))])
    return pl.pallas_call(
        functools.partial(_scatter_kernel, tm, ns // tm, n // tm),
        out_shape=(jax.ShapeDtypeStruct((ns, D_MODEL), BF16), jax.ShapeDtypeStruct((ns, LANES), F32)),
        grid_spec=grid_spec,
        compiler_params=_params(1),
        name="moe_scatter",
    )(runs, q3, u2, gate)


def _ungroup(tm, runs_ref, col_ref, ys_hbm, stage, sem):
    i = pl.program_id(0)
    slot = i % 2

    def copies(tile, slot):
        return _run_copies(runs_ref, tile, stage.at[slot], ys_hbm, sem.at[slot], False)

    @pl.when(i == 0)
    def _():
        stage[...] = jnp.zeros_like(stage)
        _start(copies(i, slot))

    @pl.when(i + 1 < pl.num_programs(0))
    def _():
        _start(copies(i + 1, 1 - slot))

    _wait(copies(i, slot))
    col = col_ref[...]
    gidx = col[:, 0:1]
    src = [runs_ref[i, RUN_SRC + g].astype(F32) for g in range(N_GROUPS)]
    first = src[N_GROUPS - 1]
    for g in range(N_GROUPS - 2, -1, -1):
        first = jnp.where(gidx == float(g), src[g], first)
    q = first + col[:, 1:2]
    lanes = lax.broadcasted_iota(jnp.int32, (tm, _stage_rows(tm)), 1).astype(F32)
    perm_t = jnp.where(q == lanes, 1.0, 0.0).astype(BF16)
    return _dot(perm_t, stage[slot])


def _moe_routed_kernel(items_ref, su_ref, sg_ref, wgu_ref, wd_ref, out_ref, acc_ref):
    kk = pl.program_id(0)
    grp = items_ref[1, kk]

    @pl.when(items_ref[3, kk] == 1)
    def _():
        acc_ref[...] = jnp.zeros_like(acc_ref)

    @pl.when(items_ref[2, kk] == 1)
    def _():
        u = su_ref[...]
        gate = sg_ref[...]
        lane = lax.broadcasted_iota(jnp.int32, gate.shape, 1)
        first_lane = N_GROUPS + EXPERTS_PER_GROUP * grp
        for e in range(EXPERTS_PER_GROUP):
            gcol = jnp.sum(jnp.where(lane == first_lane + e, gate, 0.0), axis=-1, keepdims=True)
            hu = _dot(u, wgu_ref[e])
            hg = hu[:, 0:D_EXPERT]
            act = hg * jax.nn.sigmoid(hg) * hu[:, D_EXPERT:2 * D_EXPERT] * gcol
            acc_ref[...] += _dot(act.astype(BF16), wd_ref[e])

    @pl.when(items_ref[4, kk] == 1)
    def _():
        out_ref[...] = acc_ref[...].astype(out_ref.dtype)


def _moe_routed_call(items, su, sg, wgu, wd, tm):
    ns = su.shape[0]
    n_items = ns // tm + N_GROUPS - 1
    wgu4 = wgu.reshape(N_GROUPS, EXPERTS_PER_GROUP, D_MODEL, 2 * D_EXPERT)
    wd4 = wd.reshape(N_GROUPS, EXPERTS_PER_GROUP, D_EXPERT, D_MODEL)
    grid_spec = pltpu.PrefetchScalarGridSpec(
        num_scalar_prefetch=1,
        grid=(n_items,),
        in_specs=[pl.BlockSpec((tm, D_MODEL), lambda k, it: (it[0, k], 0)),
                  pl.BlockSpec((tm, LANES), lambda k, it: (it[0, k], 0)),
                  pl.BlockSpec((None, EXPERTS_PER_GROUP, D_MODEL, 2 * D_EXPERT),
                               lambda k, it: (it[1, k], 0, 0, 0)),
                  pl.BlockSpec((None, EXPERTS_PER_GROUP, D_EXPERT, D_MODEL),
                               lambda k, it: (it[1, k], 0, 0, 0))],
        out_specs=pl.BlockSpec((tm, D_MODEL), lambda k, it: (it[0, k], 0)),
        scratch_shapes=[pltpu.VMEM((tm, D_MODEL), F32)])
    return pl.pallas_call(
        _moe_routed_kernel,
        out_shape=jax.ShapeDtypeStruct((ns, D_MODEL), BF16),
        grid_spec=grid_spec,
        compiler_params=_params(1),
        name="moe_routed",
    )(items, su, sg, wgu4, wd4)


def _resid_gather_kernel(tm, runs_ref, x_ref, g2_ref, col_ref, ys_hbm, y_ref, stage, sem):
    y_ref[...] = x_ref[...] + g2_ref[...] * _ungroup(tm, runs_ref, col_ref, ys_hbm, stage, sem)


def _resid_gather_call(x, runs, col, ys, mod, mspec, tm):
    n = x.shape[0]
    row = lambda i, *_: (i, 0)
    grid_spec = pltpu.PrefetchScalarGridSpec(
        num_scalar_prefetch=1,
        grid=(n // tm,),
        in_specs=[pl.BlockSpec((tm, D_MODEL), row), mspec(5), pl.BlockSpec((tm, LANES), row),
                  pl.BlockSpec(memory_space=pl.ANY)],
        out_specs=pl.BlockSpec((tm, D_MODEL), row),
        scratch_shapes=[pltpu.VMEM((2, _stage_rows(tm), D_MODEL), BF16), pltpu.SemaphoreType.DMA((2,))])
    return pl.pallas_call(
        functools.partial(_resid_gather_kernel, tm),
        out_shape=jax.ShapeDtypeStruct((n, D_MODEL), F32),
        grid_spec=grid_spec,
        compiler_params=_params(1),
        name="moe_unsort_resid",
    )(runs, x, mod, col, ys)


def _moe_kernel(u_ref, gate_ref, wgu_ref, wd_ref, x1_ref, g2_ref, y_ref, acc_ref):
    e = pl.program_id(1)

    @pl.when(e == 0)
    def _():
        acc_ref[...] = jnp.zeros_like(acc_ref)

    hu = _dot(u_ref[...], wgu_ref[...])
    hg = hu[:, 0:D_EXPERT]
    hv = hu[:, D_EXPERT:2 * D_EXPERT]
    gate = gate_ref[...]
    lane = lax.broadcasted_iota(jnp.int32, gate.shape, 1)
    gcol = jnp.sum(jnp.where(lane == N_GROUPS + e, gate, 0.0), axis=-1, keepdims=True)
    act = hg * jax.nn.sigmoid(hg) * hv * gcol
    acc_ref[...] += _dot(act.astype(BF16), wd_ref[...])

    @pl.when(e == N_EXPERTS - 1)
    def _():
        y_ref[...] = x1_ref[...] + g2_ref[...] * acc_ref[...]


def _moe_call(u2, gate, wgu, wd, x1, mod, mspec, tm):
    n = u2.shape[0]
    row = lambda i, e: (i, 0)
    return pl.pallas_call(
        _moe_kernel,
        out_shape=jax.ShapeDtypeStruct((n, D_MODEL), F32),
        grid=(n // tm, N_EXPERTS),
        in_specs=[pl.BlockSpec((tm, D_MODEL), row), pl.BlockSpec((tm, LANES), row),
                  pl.BlockSpec((None, D_MODEL, 2 * D_EXPERT), lambda i, e: (e, 0, 0)),
                  pl.BlockSpec((None, D_EXPERT, D_MODEL), lambda i, e: (e, 0, 0)),
                  pl.BlockSpec((tm, D_MODEL), row), mspec(5)],
        out_specs=pl.BlockSpec((tm, D_MODEL), row),
        scratch_shapes=[pltpu.VMEM((tm, D_MODEL), F32)],
        compiler_params=_params(2),
        name="moe_dense",
    )(u2, gate, wgu, wd, x1, mod)


def _rope128(x, cos, sa, sb):
    return x * cos + pltpu.roll(x, LANES - ROT_DIM // 2, axis=1) * sa \
        + pltpu.roll(x, ROT_DIM // 2, axis=1) * sb


def _kvq_kernel(gather_tm, *refs):
    if gather_tm:
        (runs_ref, x_ref, g2_ref, col_ref, ys_hbm, kvsh_ref, kvsc_ref, sh1_ref, sc1_ref, kvn_ref,
         nm_ref, wkv_ref, wq_ref, kn_ref, qn_ref, cos_ref, sa_ref, sb_ref, g64_ref, gr_ref, gb_ref,
         q_ref, k_ref, v_ref, xa_ref, stage, sem) = refs
        x = x_ref[...] + g2_ref[...] * _ungroup(gather_tm, runs_ref, col_ref, ys_hbm, stage, sem)
        xa_ref[...] = x
    else:
        (x_ref, kvsh_ref, kvsc_ref, sh1_ref, sc1_ref, kvn_ref, nm_ref,
         wkv_ref, wq_ref, kn_ref, qn_ref, cos_ref, sa_ref, sb_ref, g64_ref, gr_ref, gb_ref,
         q_ref, k_ref, v_ref) = refs
        x = x_ref[...]
    xn = _unit_rms(x)
    cos, sa, sb = cos_ref[...], sa_ref[...], sb_ref[...]

    ukv = xn * kvn_ref[...] * (1.0 + kvsc_ref[...]) + kvsh_ref[...]
    kv = _dot(ukv.astype(BF16), wkv_ref[...])
    k = kv[:, 0:LANES]
    v_ref[...] = kv[:, LANES:2 * LANES]
    kh, kl = _split2(k * k)
    ms = _dot(kh, g64_ref[...]) + _dot(kl, g64_ref[...])
    k_ref[...] = _rope128(k * lax.rsqrt(ms + RMS_EPS) * kn_ref[...], cos, sa, sb)

    u1 = xn * nm_ref[...] * (1.0 + sc1_ref[...]) + sh1_ref[...]
    q = _dot(u1.astype(BF16), wq_ref[...])
    qh, ql = _split2(q * q)
    ms16 = _dot(qh, gr_ref[...]) + _dot(ql, gr_ref[...])
    r3 = _split3(lax.rsqrt(ms16 + RMS_EPS))
    rsb = _dot(r3[0], gb_ref[...]) + _dot(r3[1], gb_ref[...]) + _dot(r3[2], gb_ref[...])
    qn = q * rsb * qn_ref[...]
    for i in range(D_MODEL // LANES):
        sl = slice(LANES * i, LANES * (i + 1))
        q_ref[:, sl] = (_rope128(qn[:, sl], cos, sa, sb) * (HEAD_DIM ** -0.5)).astype(BF16)


def _kvq_call(x, routed, mod0, mod1, mspec, kvmod, kvspec, kvn, nm, wkv, wq, kn, qn, tabs, tab,
              g64, gr, gb, tm):
    n = x.shape[0]
    row = lambda i, *_: (i, 0)
    const = lambda i, *_: (0, 0)
    vec = pl.BlockSpec((1, D_MODEL), const)
    tok = pl.BlockSpec((tm, D_MODEL), row)
    small = pl.BlockSpec((tm, LANES), row)
    ins = [x, kvmod, kvmod, mod1, mod1, kvn, nm, wkv, wq, kn, qn, *tabs, g64, gr, gb]
    specs = [tok, kvspec(0), kvspec(1), mspec(0), mspec(1), vec, vec,
             pl.BlockSpec((D_MODEL, 2 * LANES), const), pl.BlockSpec((D_MODEL, D_MODEL), const),
             pl.BlockSpec((1, LANES), const), vec, tab, tab, tab,
             pl.BlockSpec((LANES, LANES), const), pl.BlockSpec((D_MODEL, LANES), const),
             pl.BlockSpec((LANES, D_MODEL), const)]
    out_shape = [jax.ShapeDtypeStruct((n, D_MODEL), BF16), jax.ShapeDtypeStruct((n, LANES), F32),
                 jax.ShapeDtypeStruct((n, LANES), F32)]
    out_specs = [tok, small, small]
    scratch = []
    nprefetch = 0
    if routed is not None:
        runs, col, ys = routed
        nprefetch = 1
        ins = [runs, x, mod0, col, ys] + ins[1:]
        specs = [tok, mspec(5), small, pl.BlockSpec(memory_space=pl.ANY)] + specs[1:]
        out_shape.append(jax.ShapeDtypeStruct((n, D_MODEL), F32))
        out_specs.append(tok)
        scratch = [pltpu.VMEM((2, _stage_rows(tm), D_MODEL), BF16), pltpu.SemaphoreType.DMA((2,))]
    grid_spec = pltpu.PrefetchScalarGridSpec(
        num_scalar_prefetch=nprefetch, grid=(n // tm,), in_specs=specs,
        out_specs=tuple(out_specs), scratch_shapes=scratch)
    return pl.pallas_call(
        functools.partial(_kvq_kernel, tm if routed is not None else 0),
        out_shape=tuple(out_shape),
        grid_spec=grid_spec,
        compiler_params=_params(1),
        name="kv_q_proj",
    )(*ins)


def _attn_core(q, kcat, vcat, bias, sinks_ref, o_ref):
    tk = kcat.shape[0]
    pairs = N_Q_HEADS // N_KV_HEADS // 2
    lane = lax.broadcasted_iota(jnp.int32, (1, LANES), 1)
    lo = lane < HEAD_DIM
    kro = pltpu.roll(kcat, HEAD_DIM, axis=1)
    vro = pltpu.roll(vcat, HEAD_DIM, axis=1)
    one_e = jnp.broadcast_to(jnp.where(lo, 1.0, 0.0), (tk, LANES))
    one_o = 1.0 - one_e
    for g in range(N_KV_HEADS):
        if g == 0:
            ke, ko = jnp.where(lo, kcat, 0.0), jnp.where(lo, 0.0, kro)
            ve, vo = jnp.where(lo, vcat, 0.0), jnp.where(lo, 0.0, vro)
        else:
            ke, ko = jnp.where(lo, kro, 0.0), jnp.where(lo, 0.0, kcat)
            ve, vo = jnp.where(lo, vro, 0.0), jnp.where(lo, 0.0, vcat)
        k2 = jnp.concatenate([ke, ko], 0).astype(BF16)
        v2 = jnp.concatenate([jnp.concatenate([ve, one_e], 1),
                              jnp.concatenate([vo, one_o], 1)], 0).astype(BF16)
        for p in range(pairs):
            hp = g * pairs + p
            s = _dot_nt(q[:, LANES * hp:LANES * (hp + 1)], k2)
            halves, corr = [], []
            for par in range(2):
                sp = s[:, par * tk:(par + 1) * tk] + bias
                sink = sinks_ref[2 * hp + par]
                m = jnp.maximum(jnp.max(sp, axis=-1, keepdims=True), sink)
                halves.append(jnp.exp(sp - m).astype(BF16))
                corr.append(jnp.exp(sink - m))
            o2 = _dot(jnp.concatenate(halves, 1), v2)
            den = o2[:, LANES:2 * LANES] + jnp.where(lo, corr[0], corr[1])
            o_ref[:, LANES * hp:LANES * (hp + 1)] = (o2[:, 0:LANES] / den).astype(o_ref.dtype)


def _attn_prompt_kernel(sinks_ref, q_ref, kp_ref, kc_ref, vp_ref, vc_ref, bias_ref, o_ref):
    kcat = jnp.concatenate([kp_ref[...], kc_ref[...]], 0)
    vcat = jnp.concatenate([vp_ref[...], vc_ref[...]], 0)
    _attn_core(q_ref[...], kcat, vcat, bias_ref[...], sinks_ref, o_ref)


ATTN_SEQS_PER_STEP = 4


def _attn_sample_kernel(sinks_ref, q_ref, k_ref, v_ref, bias_ref, o_ref):
    for b in range(ATTN_SEQS_PER_STEP):
        _attn_core(q_ref[b], k_ref[b], v_ref[b], bias_ref[...], sinks_ref, o_ref.at[b])


def _window_bias(tq, first):
    qi = jnp.arange(tq)[:, None]
    kj = jnp.arange(2 * WINDOW)[None, :]
    ok = (kj > qi) & (kj <= qi + WINDOW)
    if first:
        ok = ok & (kj >= WINDOW)
    return jnp.where(ok, 0.0, -jnp.inf).astype(F32)


def _attn_prompt_call(sinks, q, k, v, B, nb):
    n = q.shape[0]
    cur = lambda b, i: (b * nb + i, 0)
    prev = lambda b, i: (b * nb + jnp.maximum(i - 1, 0), 0)
    kvb = lambda im: pl.BlockSpec((WINDOW, LANES), im)
    bias = jnp.stack([_window_bias(WINDOW, True), _window_bias(WINDOW, False)])
    return pl.pallas_call(
        _attn_prompt_kernel,
        out_shape=jax.ShapeDtypeStruct((n, D_MODEL), BF16),
        grid=(B, nb),
        in_specs=[pl.BlockSpec(memory_space=pltpu.SMEM),
                  pl.BlockSpec((WINDOW, D_MODEL), cur), kvb(prev), kvb(cur), kvb(prev), kvb(cur),
                  pl.BlockSpec((None, WINDOW, 2 * WINDOW), lambda b, i: (jnp.minimum(i, 1), 0, 0))],
        out_specs=pl.BlockSpec((WINDOW, D_MODEL), cur),
        compiler_params=_params(2),
        name="swa_prompt",
    )(sinks, q, k, k, v, v, bias)


def _attn_sample_call(sinks, q, kcat, vcat):
    B = q.shape[0]
    b3 = lambda b: (b, 0, 0)
    return pl.pallas_call(
        _attn_sample_kernel,
        out_shape=jax.ShapeDtypeStruct(q.shape, BF16),
        grid=(B // ATTN_SEQS_PER_STEP,),
        in_specs=[pl.BlockSpec(memory_space=pltpu.SMEM),
                  pl.BlockSpec((ATTN_SEQS_PER_STEP, SAMPLE_PAD, D_MODEL), b3),
                  pl.BlockSpec((ATTN_SEQS_PER_STEP, 2 * WINDOW, LANES), b3),
                  pl.BlockSpec((ATTN_SEQS_PER_STEP, 2 * WINDOW, LANES), b3),
                  pl.BlockSpec((SAMPLE_PAD, 2 * WINDOW), lambda b: (0, 0))],
        out_specs=pl.BlockSpec((ATTN_SEQS_PER_STEP, SAMPLE_PAD, D_MODEL), b3),
        compiler_params=_params(1),
        name="swa_sample",
    )(sinks, q, kcat, vcat, _window_bias(SAMPLE_PAD, False))


def _rope_tables(pos):
    half = ROT_DIM // 2
    inv = ROPE_THETA ** (-jnp.arange(half, dtype=F32) / half)
    ang = pos.astype(F32)[:, None] * inv[None]
    cos, sin = jnp.cos(ang), jnp.sin(ang)
    d = jnp.arange(LANES) % HEAD_DIM
    idx = d % half
    cos_t = jnp.where(d < ROT_DIM, cos[:, idx], 1.0)
    sa = jnp.where(d < half, -sin[:, idx], 0.0)
    sb = jnp.where((d >= half) & (d < ROT_DIM), sin[:, idx], 0.0)
    return cos_t, sa, sb


def _pad_lanes(a, value=0.0):
    return jnp.pad(a, ((0, 0), (0, LANES - a.shape[1])), constant_values=value)


def _prep_weights(ada_w, ada_b, norm_mix, norm_ffn, a_w_in, a_b_gates, a_head_norm, a_w_out,
                  kv_ada_w, kv_ada_b, kv_norm, w_k, w_v, k_norm, b_w_q, b_q_norm, b_sinks, b_w_o,
                  moe_w_group, moe_b_group, moe_w_expert, moe_b_expert, moe_w_gate_up, moe_w_down):
    w = {}
    g0 = QK_COLS + 2 * V_COLS
    win = a_w_in[0]
    w["w_in"] = jnp.concatenate([win[:, :g0], _pad_lanes(win[:, g0:g0 + M_HEADS]),
                                 _pad_lanes(win[:, g0 + M_HEADS:])], 1).astype(BF16)
    w["bli"] = _pad_lanes(a_b_gates[0][None, :M_HEADS])
    w["blf"] = _pad_lanes(a_b_gates[0][None, M_HEADS:])
    w["head_norm"] = a_head_norm[0][None]
    w["w_out"] = a_w_out[0].astype(BF16)
    w["norm_mix"] = [norm_mix[l][None] for l in range(2)]
    w["norm_ffn"] = [norm_ffn[l][None] for l in range(2)]
    w["router"] = []
    for l in range(2):
        wr = _pad_lanes(jnp.concatenate([moe_w_group[l], moe_w_expert[l]], 1))
        hi = wr.astype(BF16)
        lo = (wr - hi.astype(F32)).astype(BF16)
        br = _pad_lanes(jnp.concatenate([moe_b_group[l], moe_b_expert[l]])[None])
        w["router"].append((hi, lo, br))
    w["w_gu"] = [moe_w_gate_up[l].astype(BF16) for l in range(2)]
    w["w_d"] = [moe_w_down[l].astype(BF16) for l in range(2)]
    w["kv_norm"] = kv_norm[None]
    w["w_kv"] = jnp.concatenate([w_k, w_v], 1).astype(BF16)
    w["k_norm"] = jnp.tile(k_norm, N_KV_HEADS)[None]
    w["w_q"] = b_w_q[0].astype(BF16)
    w["q_norm"] = jnp.tile(b_q_norm[0], N_Q_HEADS)[None]
    w["sinks"] = b_sinks[0]
    w["w_o"] = b_w_o[0].astype(BF16)
    lanes = jnp.arange(LANES)
    feat = jnp.arange(D_MODEL)
    w["g64"] = jnp.where((lanes[:, None] // HEAD_DIM) == (lanes[None, :] // HEAD_DIM),
                         1.0 / HEAD_DIM, 0.0).astype(BF16)
    w["gr"] = jnp.where((feat[:, None] // HEAD_DIM) == lanes[None, :], 1.0 / HEAD_DIM, 0.0).astype(BF16)
    w["gb"] = jnp.where(lanes[:, None] == (feat[None, :] // HEAD_DIM), 1.0, 0.0).astype(BF16)
    return w


def _trunk(x2, mods, kvmod, mspec, mspec_moe, w, *, B, T, L, tm, tm_moe, tabs, tabspec, c0, n0, m0,
           cache_k=None, cache_v=None):
    sample = cache_k is not None
    nc = T // L if not sample else 1

    qk, v, o, li, lf = _inproj_call(x2, mods[0], mspec, w["norm_mix"][0], w["w_in"],
                                    w["bli"], w["blf"], tm)
    if sample:
        def padtok(a, value=0.0):
            a = a.reshape(B, T, a.shape[-1])
            a = jnp.pad(a, ((0, 0), (0, L - T), (0, 0)), constant_values=value)
            return a.reshape(B * L, a.shape[-1])
        qk, v, li, lf = padtok(qk), padtok(v), padtok(li, M_EMPTY), padtok(lf)
    h, c_new, n_new, m_new = _mlstm_call(qk, v, li, lf, c0, n0, m0, B, nc, L)
    if sample:
        h = h.reshape(B, L, V_COLS)[:, :T].reshape(B * T, V_COLS)
    routed = not sample
    ntiles = (B * T) // tm

    def routed_moe(u2, gate, gid, col, layer):
        q, runs, items = _sort_meta_call(gid.reshape(ntiles, tm), tm)
        su, sg = _scatter_call(runs, q.reshape(ntiles, 1, tm), u2, gate, tm)
        ys = _moe_routed_call(items, su, sg, w["w_gu"][layer], w["w_d"][layer], tm)
        return runs, col, ys

    rh, rl, br = w["router"][0]
    post0 = _post_call(True, routed, h, o, x2, mods[0], mspec, w["norm_ffn"][0], w["head_norm"],
                       w["w_out"], rh, rl, br, tm)
    kvq_args = (mods[0], mods[1], mspec, kvmod, mspec, w["kv_norm"], w["norm_mix"][1],
                w["w_kv"], w["w_q"], w["k_norm"], w["q_norm"], tabs, tabspec,
                w["g64"], w["gr"], w["gb"], tm)

    if routed:
        x1, u2, gate, gid, col = post0
        q, k, vv, xa = _kvq_call(x1, routed_moe(u2, gate, gid, col, 0), *kvq_args)
    else:
        x1, u2, gate = post0
        xa = _moe_call(u2, gate, w["w_gu"][0], w["w_d"][0], x1, mods[0], mspec_moe, tm_moe)
        q, k, vv = _kvq_call(xa, None, *kvq_args)
    if not sample:
        att = _attn_prompt_call(w["sinks"], q, k, vv, B, T // WINDOW)
        k_win = k.reshape(B, T, N_KV_HEADS, HEAD_DIM)[:, T - WINDOW:]
        v_win = vv.reshape(B, T, N_KV_HEADS, HEAD_DIM)[:, T - WINDOW:]
    else:
        zpad = jnp.zeros((B, WINDOW - T, LANES), F32)
        kcat = jnp.concatenate([cache_k.reshape(B, WINDOW, LANES), k.reshape(B, T, LANES), zpad], 1)
        vcat = jnp.concatenate([cache_v.reshape(B, WINDOW, LANES), vv.reshape(B, T, LANES), zpad], 1)
        qp = jnp.pad(q.reshape(B, T, D_MODEL), ((0, 0), (0, SAMPLE_PAD - T), (0, 0)))
        att = _attn_sample_call(w["sinks"], qp, kcat, vcat)[:, :T].reshape(B * T, D_MODEL)
        k_win = kcat[:, T:T + WINDOW].reshape(B, WINDOW, N_KV_HEADS, HEAD_DIM)
        v_win = vcat[:, T:T + WINDOW].reshape(B, WINDOW, N_KV_HEADS, HEAD_DIM)
    rh, rl, br = w["router"][1]
    post1 = _post_call(False, routed, att, None, xa, mods[1], mspec, w["norm_ffn"][1], None,
                       w["w_o"], rh, rl, br, tm)
    if routed:
        x3, u4, gate, gid, col = post1
        runs, col, ys = routed_moe(u4, gate, gid, col, 1)
        y = _resid_gather_call(x3, runs, col, ys, mods[1], mspec, tm)
    else:
        x3, u4, gate = post1
        y = _moe_call(u4, gate, w["w_gu"][1], w["w_d"][1], x3, mods[1], mspec_moe, tm_moe)
    c_out = c_new[None]
    n_out = n_new.reshape(1, B, M_HEADS, M_DK)
    m_out = m_new[:, 0, :M_HEADS][None]
    return y, c_out, n_out, m_out, k_win, v_win


def kernel(x_prompt, x_sample, c_prompt, c_sample, state_c, state_n, state_m, cache_k_win, cache_v_win, ada_w, ada_b, norm_mix, norm_ffn, a_w_in, a_b_gates, a_head_norm, a_w_out, kv_ada_w, kv_ada_b, kv_norm, w_k, w_v, k_norm, b_w_q, b_q_norm, b_sinks, b_w_o, moe_w_group, moe_b_group, moe_w_expert, moe_b_expert, moe_w_gate_up, moe_w_down):
    Bp, Tp, D = x_prompt.shape
    Bs, Ts, _ = x_sample.shape
    w = _prep_weights(ada_w, ada_b, norm_mix, norm_ffn, a_w_in, a_b_gates, a_head_norm, a_w_out,
                      kv_ada_w, kv_ada_b, kv_norm, w_k, w_v, k_norm, b_w_q, b_q_norm, b_sinks, b_w_o,
                      moe_w_group, moe_b_group, moe_w_expert, moe_b_expert, moe_w_gate_up, moe_w_down)

    rows = Bp + Bs
    rpad = -rows % 8
    c_all = jnp.concatenate([c_prompt, c_sample, jnp.zeros((rpad, D), F32)], 0)
    mod = _ada_call(c_all, ada_w, ada_b[:, None, :])
    kvm = _ada_call(c_all, kv_ada_w[None], kv_ada_b[None, None, :])

    tm_p = 512
    tiles_per_seq = Tp // tm_p
    mods_p = [mod[l, :Bp][:, None, :] for l in range(2)]
    kvmod_p = kvm[0, :Bp][:, None, :]

    def mspec_p(col):
        return pl.BlockSpec((None, 1, D_MODEL), lambda i, *_: (i // tiles_per_seq, 0, col))

    tm_moe_p = 1024
    moe_tiles_per_seq = Tp // tm_moe_p

    def mspec_moe_p(col):
        return pl.BlockSpec((None, 1, D_MODEL), lambda i, *_: (i // moe_tiles_per_seq, 0, col))

    tabs_p = _rope_tables(jnp.arange(Tp, dtype=jnp.int32))
    tabspec_p = pl.BlockSpec((tm_p, LANES), lambda i, *_: (i % tiles_per_seq, 0))
    npairs = M_HEADS // 2
    c0 = jnp.zeros((Bp, M_HEADS, M_DK, M_DV), F32)
    n0 = jnp.zeros((Bp, npairs, LANES), F32)
    m0 = jnp.pad(jnp.full((Bp, 1, M_HEADS), M_EMPTY, F32), ((0, 0), (0, 0), (0, LANES - M_HEADS)))
    yp, cp, np_, mp, kwp, vwp = _trunk(
        x_prompt.reshape(Bp * Tp, D), mods_p, kvmod_p, mspec_p, mspec_moe_p, w,
        B=Bp, T=Tp, L=M_CHUNK, tm=tm_p, tm_moe=tm_moe_p, tabs=tabs_p, tabspec=tabspec_p,
        c0=c0, n0=n0, m0=m0)

    ns = Bs * Ts
    mods_s = [jnp.repeat(mod[l, Bp:Bp + Bs], Ts, axis=0)[None] for l in range(2)]
    kvmod_s = jnp.repeat(kvm[0, Bp:Bp + Bs], Ts, axis=0)[None]

    def mspec_s(col):
        return pl.BlockSpec((None, ns, D_MODEL), lambda i, *_: (0, 0, col))

    tabs_s = _rope_tables(PAST_LEN + jnp.arange(Ts, dtype=jnp.int32))
    tabs_s = tuple(jnp.tile(t, (Bs, 1)) for t in tabs_s)
    m0s = jnp.pad(state_m[0][:, None, :], ((0, 0), (0, 0), (0, LANES - M_HEADS)))
    ys, cs, ns_, ms, kws, vws = _trunk(
        x_sample.reshape(ns, D), mods_s, kvmod_s, mspec_s, mspec_s, w,
        B=Bs, T=Ts, L=SAMPLE_PAD, tm=ns, tm_moe=ns, tabs=tabs_s,
        tabspec=pl.BlockSpec((ns, LANES), lambda i, *_: (0, 0)),
        c0=state_c[0], n0=state_n[0].reshape(Bs, npairs, LANES), m0=m0s,
        cache_k=cache_k_win, cache_v=cache_v_win)

    return (yp.reshape(Bp, Tp, D), ys.reshape(Bs, Ts, D), cp, np_, mp, kwp, vwp,
            cs, ns_, ms, kws, vws)
```

```python
import functools

import jax
import jax.numpy as jnp
from jax import lax
from jax.experimental import pallas as pl
from jax.experimental.pallas import tpu as pltpu

F32 = jnp.float32
BF16 = jnp.bfloat16

D_MODEL = 1024
PAST_LEN = 8192
M_HEADS = 8
M_DK = 64
M_DV = 128
M_CHUNK = 64
GATE_SOFTCAP = 15.0
M_EMPTY = -1e30
WINDOW = 128
HEAD_DIM = 64
N_Q_HEADS = 16
N_KV_HEADS = 2
ROPE_THETA = 500000.0
ROT_DIM = 16
N_GROUPS = 4
EXPERTS_PER_GROUP = 4
N_EXPERTS = 16
D_EXPERT = 256
RMS_EPS = 1e-6

LANES = 128
QK_COLS = 2 * M_HEADS * M_DK
V_COLS = M_HEADS * M_DV
IN_COLS = QK_COLS + 2 * V_COLS + 2 * LANES
SAMPLE_PAD = 16
VMEM_LIMIT = 52 * 1024 * 1024

NT_DIMS = (((1,), (1,)), ((), ()))


def _params(n_axes):
    return pltpu.CompilerParams(dimension_semantics=("arbitrary",) * n_axes,
                                vmem_limit_bytes=VMEM_LIMIT)


def _dot(a, b):
    return jnp.dot(a, b, preferred_element_type=F32)


def _dot_nt(a, b):
    return lax.dot_general(a, b, NT_DIMS, preferred_element_type=F32)


def _split2(x):
    hi = x.astype(BF16)
    lo = (x - hi.astype(F32)).astype(BF16)
    return hi, lo


def _split3(x):
    hi = x.astype(BF16)
    r = x - hi.astype(F32)
    mid = r.astype(BF16)
    lo = (r - mid.astype(F32)).astype(BF16)
    return hi, mid, lo


def _unit_rms(x):
    return x * lax.rsqrt(jnp.mean(x * x, axis=-1, keepdims=True) + RMS_EPS)


def _ada_kernel(c_ref, w_ref, b_ref, o_ref):
    c = c_ref[...]
    cs = (c * jax.nn.sigmoid(c)).astype(BF16)
    o_ref[...] = _dot(cs, w_ref[...].astype(BF16)) + b_ref[...]


def _ada_call(c, w, b):
    g, d, n = w.shape
    r = c.shape[0]
    tn = 1024
    return pl.pallas_call(
        _ada_kernel,
        out_shape=jax.ShapeDtypeStruct((g, r, n), F32),
        grid=(g, n // tn),
        in_specs=[pl.BlockSpec((r, d), lambda i, j: (0, 0)),
                  pl.BlockSpec((None, d, tn), lambda i, j: (i, 0, j)),
                  pl.BlockSpec((None, 1, tn), lambda i, j: (i, 0, j))],
        out_specs=pl.BlockSpec((None, r, tn), lambda i, j: (i, 0, j)),
        compiler_params=_params(2),
        name="ada_mod",
    )(c, w, b)


def _mod_rows(ref, rep_ref):
    if rep_ref is None:
        return ref[...]
    hi, mid, lo = _split3(ref[...])
    rep = rep_ref[...]
    return _dot(rep, hi) + _dot(rep, mid) + _dot(rep, lo)


def _inproj_kernel(expand, x_ref, sh_ref, sc_ref, nw_ref, w_ref, wg_ref, bli_ref, blf_ref, *rest):
    rep_ref = rest[0] if expand else None
    qk_ref, v_ref, o_ref, li_ref, lf_ref, wb_ref = rest[1 if expand else 0:]

    @pl.when(pl.program_id(0) == 0)
    def _():
        wb_ref[...] = w_ref[...].astype(BF16)

    u = _unit_rms(x_ref[...]) * nw_ref[...] * (1.0 + _mod_rows(sc_ref, rep_ref)) \
        + _mod_rows(sh_ref, rep_ref)
    ub = u.astype(BF16)
    half = QK_COLS // 2
    q = _dot(ub, wb_ref[:, 0:half]) * (M_DK ** -0.5)
    qk_ref[:, 0:half] = q.astype(BF16)
    qk_ref[:, half:QK_COLS] = _dot(ub, wb_ref[:, half:QK_COLS]).astype(BF16)
    v_ref[...] = _dot(ub, wb_ref[:, QK_COLS:QK_COLS + V_COLS]).astype(BF16)
    o_ref[...] = _dot(ub, wb_ref[:, QK_COLS + V_COLS:QK_COLS + 2 * V_COLS]).astype(BF16)
    lane = lax.broadcasted_iota(jnp.int32, (1, LANES), 1)
    live = lane < M_HEADS
    wg = wg_ref[...].astype(BF16)
    gi = _dot(ub, wg[:, 0:LANES]) + bli_ref[...]
    gf = _dot(ub, wg[:, LANES:2 * LANES]) + blf_ref[...]
    li = GATE_SOFTCAP * jnp.tanh(gi / GATE_SOFTCAP)
    fpre = GATE_SOFTCAP * jnp.tanh(gf / GATE_SOFTCAP)
    lf = jnp.minimum(fpre, 0.0) - jnp.log1p(jnp.exp(-jnp.abs(fpre)))
    li_ref[...] = jnp.where(live, li, 0.0)
    lf_ref[...] = jnp.where(live, lf, 0.0)


def _inproj_call(x, mod, mspec, rep, nw, w_in, wg, bli, blf, tm):
    n = x.shape[0]
    row = lambda i: (i, 0)
    const = lambda i: (0, 0)
    main_cols = QK_COLS + 2 * V_COLS
    expand = rep is not None
    return pl.pallas_call(
        functools.partial(_inproj_kernel, expand),
        out_shape=(jax.ShapeDtypeStruct((n, QK_COLS), BF16),
                   jax.ShapeDtypeStruct((n, V_COLS), BF16),
                   jax.ShapeDtypeStruct((n, V_COLS), BF16),
                   jax.ShapeDtypeStruct((n, LANES), F32),
                   jax.ShapeDtypeStruct((n, LANES), F32)),
        grid=(n // tm,),
        in_specs=[pl.BlockSpec((tm, D_MODEL), row), mspec(0), mspec(1),
                  pl.BlockSpec((1, D_MODEL), const),
                  pl.BlockSpec((None, D_MODEL, main_cols), lambda i: (0, 0, 0),
                               pipeline_mode=pl.Buffered(1)),
                  pl.BlockSpec((D_MODEL, 2 * LANES), const),
                  pl.BlockSpec((1, LANES), const), pl.BlockSpec((1, LANES), const)]
        + ([pl.BlockSpec(rep.shape, const)] if expand else []),
        out_specs=(pl.BlockSpec((tm, QK_COLS), row), pl.BlockSpec((tm, V_COLS), row),
                   pl.BlockSpec((tm, V_COLS), row), pl.BlockSpec((tm, LANES), row),
                   pl.BlockSpec((tm, LANES), row)),
        scratch_shapes=[pltpu.VMEM((D_MODEL, main_cols), BF16)],
        compiler_params=_params(1),
        name="mlstm_inproj",
    )(x, mod, mod, nw, w_in, wg, bli, blf, *([rep] if expand else []))


def _mlstm_kernel(L, nseq, *refs):
    c = pl.program_id(1)
    seqs = [[r.at[b] for r in refs] for b in range(nseq)]

    @pl.when(c == 0)
    def _():
        for s in seqs:
            _mlstm_load_state(*s[4:7], *s[11:14])

    for s in seqs:
        _mlstm_seq_step(L, *s[0:4], s[7], *s[11:14])

    @pl.when(c == pl.num_programs(1) - 1)
    def _():
        for s in seqs:
            _mlstm_store_state(*s[8:14])


def _mlstm_load_state(c0_ref, n0_ref, m0_ref, c2_s, n_s, m_s):
    m_s[...] = m0_ref[...]
    n_s[...] = n0_ref[...]
    z = jnp.zeros((M_DK, M_DV), F32)
    for j in range(M_HEADS // 2):
        c2_s[j] = jnp.concatenate([jnp.concatenate([c0_ref[2 * j], z], 1),
                                   jnp.concatenate([z, c0_ref[2 * j + 1]], 1)], 0)


def _mlstm_seq_step(L, qk_ref, v_ref, li_ref, lf_ref, h_ref, c2_s, n_s, m_s):
    npairs = M_HEADS // 2
    hd = M_DK

    lane = lax.broadcasted_iota(jnp.int32, (1, LANES), 1)
    lo128 = lane < hd
    lane256 = lax.broadcasted_iota(jnp.int32, (1, 2 * LANES), 1)
    lo256 = lane256 < LANES
    row128 = lax.broadcasted_iota(jnp.int32, (LANES, 1), 0)
    top = row128 < hd
    blockdiag = (top & lo256) | (jnp.logical_not(top) & jnp.logical_not(lo256))

    LI = li_ref[...]
    LF = lf_ref[...]
    rowL = lax.broadcasted_iota(jnp.int32, (L, LANES), 0)

    def prefix(x, op, ident):
        d = 1
        while d < L:
            shifted = pltpu.roll(x, d, axis=0)
            x = op(x, jnp.where(rowL >= d, shifted, ident))
            d *= 2
        return x

    Bc = prefix(LF, jnp.add, 0.0)
    Cm = LI - Bc
    mprev = m_s[...]
    Gc = jnp.maximum(mprev, prefix(Cm, jnp.maximum, -jnp.inf))
    A = jnp.exp(mprev - Gc)
    EXPM = jnp.exp(-(Bc + Gc))
    bL = Bc[L - 1:L, :]
    DL = bL + Cm
    mnew = jnp.maximum(bL + mprev, jnp.max(DL, axis=0, keepdims=True))
    ast = jnp.exp(bL + mprev - mnew)
    WST = jnp.exp(DL - mnew)

    def pad_rows(x, rows):
        if x.shape[0] == rows:
            return x
        return jnp.concatenate([x, jnp.zeros((rows - x.shape[0], x.shape[1]), x.dtype)], 0)

    cm_pad = pad_rows(Cm, hd)
    XT = jnp.concatenate([cm_pad, cm_pad], 0).T
    causal = (lane & (hd - 1)) <= rowL
    J = jnp.where(blockdiag, 1.0, 0.0).astype(BF16)

    def bc(X, h):
        return jnp.broadcast_to(X[:, h:h + 1], X.shape)

    for j in range(npairs):
        he, ho = 2 * j, 2 * j + 1

        def pair128(X):
            return jnp.where(lo128, bc(X, he), bc(X, ho))

        def pair256(X):
            return jnp.concatenate([bc(X, he), bc(X, ho)], 1)

        q128 = qk_ref[:, LANES * j:LANES * (j + 1)]
        k128 = qk_ref[:, QK_COLS // 2 + LANES * j:QK_COLS // 2 + LANES * (j + 1)]
        v256 = v_ref[:, 2 * LANES * j:2 * LANES * (j + 1)]
        zk = jnp.zeros_like(k128)
        zv = jnp.zeros_like(v256)
        K2t = jnp.concatenate([pad_rows(jnp.where(lo128, k128, zk), hd),
                               pad_rows(jnp.where(lo128, zk, k128), hd)], 0)
        V2 = jnp.concatenate([pad_rows(jnp.where(lo256, v256, zv), hd),
                              pad_rows(jnp.where(lo256, zv, v256), hd)], 0)
        S = _dot_nt(q128, K2t)
        crow = jnp.where(lo128, XT[he:he + 1, :], XT[ho:ho + 1, :])
        arg = jnp.where(causal, crow - pair128(Gc), -jnp.inf)
        Sw = (S * jnp.exp(arg)).astype(BF16)
        num_intra = _dot(Sw, V2)
        rowsum = _dot(Sw, J)
        C2 = c2_s[j]
        inter = _dot(q128, C2.astype(BF16))
        npair = n_s[j:j + 1, :]
        Nrow2 = jnp.concatenate(
            [jnp.broadcast_to(jnp.where(lo128, npair, 0.0), (LANES, LANES)),
             jnp.broadcast_to(jnp.where(lo128, 0.0, npair), (LANES, LANES))], 0).astype(BF16)
        qn = _dot_nt(q128, Nrow2)
        a256 = pair256(A)
        num = a256 * inter + num_intra
        den = a256 * qn + rowsum
        h = num / jnp.maximum(jnp.abs(den), pair256(EXPM))
        h_ref[:, 2 * LANES * j:2 * LANES * (j + 1)] = h.astype(h_ref.dtype)

        kw = k128.astype(F32) * pair128(WST)
        n_s[j:j + 1, :] = pair128(ast) * npair + jnp.sum(kw, axis=0, keepdims=True)
        kwT = pad_rows(kw, LANES).T.astype(BF16)
        dC = _dot(kwT, pad_rows(v256, LANES))
        c2_s[j] = pair256(ast) * C2 + jnp.where(blockdiag, dC, 0.0)

    m_s[...] = mnew


def _mlstm_store_state(cout_ref, nout_ref, mout_ref, c2_s, n_s, m_s):
    for j in range(M_HEADS // 2):
        C2 = c2_s[j]
        cout_ref[2 * j] = C2[0:M_DK, 0:M_DV]
        cout_ref[2 * j + 1] = C2[M_DK:2 * M_DK, M_DV:2 * M_DV]
    nout_ref[...] = n_s[...]
    mout_ref[...] = m_s[...]


MLSTM_SEQS_PER_STEP = 4


def _mlstm_call(qk, v, li, lf, c0, n0, m0, B, nc, L):
    npairs = M_HEADS // 2
    nseq = MLSTM_SEQS_PER_STEP
    T = nc * L
    tok3 = lambda a: a.reshape(B, T, a.shape[-1])
    tok = lambda b, c: (b, c, 0)
    st4 = lambda b, c: (b, 0, 0, 0)
    st3 = lambda b, c: (b, 0, 0)
    h, c_new, n_new, m_new = pl.pallas_call(
        functools.partial(_mlstm_kernel, L, nseq),
        out_shape=(jax.ShapeDtypeStruct((B, T, V_COLS), BF16),
                   jax.ShapeDtypeStruct((B, M_HEADS, M_DK, M_DV), F32),
                   jax.ShapeDtypeStruct((B, npairs, LANES), F32),
                   jax.ShapeDtypeStruct((B, 1, LANES), F32)),
        grid=(B // nseq, nc),
        in_specs=[pl.BlockSpec((nseq, L, QK_COLS), tok), pl.BlockSpec((nseq, L, V_COLS), tok),
                  pl.BlockSpec((nseq, L, LANES), tok), pl.BlockSpec((nseq, L, LANES), tok),
                  pl.BlockSpec((nseq, M_HEADS, M_DK, M_DV), st4),
                  pl.BlockSpec((nseq, npairs, LANES), st3),
                  pl.BlockSpec((nseq, 1, LANES), st3)],
        out_specs=(pl.BlockSpec((nseq, L, V_COLS), tok),
                   pl.BlockSpec((nseq, M_HEADS, M_DK, M_DV), st4),
                   pl.BlockSpec((nseq, npairs, LANES), st3),
                   pl.BlockSpec((nseq, 1, LANES), st3)),
        scratch_shapes=[pltpu.VMEM((nseq, npairs, 2 * M_DK, 2 * M_DV), F32),
                        pltpu.VMEM((nseq, npairs, LANES), F32),
                        pltpu.VMEM((nseq, 1, LANES), F32)],
        compiler_params=_params(2),
        name="mlstm_chunks",
    )(tok3(qk), tok3(v), tok3(li), tok3(lf), c0, n0, m0)
    return h.reshape(B * T, V_COLS), c_new, n_new, m_new


def _route(lg):
    lane = lax.broadcasted_iota(jnp.int32, lg.shape, 1)
    lanef = lane.astype(F32)
    neg = -jnp.inf
    far = float(LANES)
    gm = lane < N_GROUPS
    lgm = jnp.where(gm, lg, neg)
    gmax = jnp.max(lgm, axis=-1, keepdims=True)
    gsum = jnp.sum(jnp.exp(lgm - gmax), axis=-1, keepdims=True)
    g_w = 1.0 / gsum
    gidx = jnp.min(jnp.where(gm & (lg == gmax), lanef, far), axis=-1, keepdims=True)
    first = N_GROUPS + EXPERTS_PER_GROUP * gidx
    sel = (lanef >= first) & (lanef < first + EXPERTS_PER_GROUP)
    l1 = jnp.max(jnp.where(sel, lg, neg), axis=-1, keepdims=True)
    i1 = jnp.min(jnp.where(sel & (lg == l1), lanef, far), axis=-1, keepdims=True)
    sel2 = sel & (lanef != i1)
    l2 = jnp.max(jnp.where(sel2, lg, neg), axis=-1, keepdims=True)
    i2 = jnp.min(jnp.where(sel2 & (lg == l2), lanef, far), axis=-1, keepdims=True)
    r = jnp.exp(l2 - l1)
    w1 = g_w / (1.0 + r)
    w2 = w1 * r
    return jnp.where(lanef == i1, w1, jnp.where(lanef == i2, w2, 0.0)), gidx


def _post_kernel(mlstm, routed, expand, *refs):
    refs = list(refs)
    h_ref = refs.pop(0)
    o_ref = refs.pop(0) if mlstm else None
    x_ref, g1_ref, sh2_ref, sc2_ref, nf_ref = refs[:5]
    refs = refs[5:]
    hn_ref = refs.pop(0) if mlstm else None
    wout_ref, wrh_ref, wrl_ref, br_ref = refs[:4]
    refs = refs[4:]
    tri_ref = refs.pop(0) if routed else None
    rep_ref = refs.pop(0) if expand else None
    x1_ref, u2_ref, gate_ref = refs[:3]
    outs = refs[3:]
    if mlstm:
        hf = h_ref[...].astype(F32)
        parts = [_unit_rms(hf[:, M_DV * i:M_DV * (i + 1)]) for i in range(M_HEADS)]
        hn = jnp.concatenate(parts, 1) * hn_ref[...]
        hg = (hn * jax.nn.sigmoid(o_ref[...].astype(F32))).astype(BF16)
    else:
        hg = h_ref[...]
    x1 = x_ref[...] + _mod_rows(g1_ref, rep_ref) * _dot(hg, wout_ref[...])
    x1_ref[...] = x1
    u2 = _unit_rms(x1) * nf_ref[...] * (1.0 + _mod_rows(sc2_ref, rep_ref)) + _mod_rows(sh2_ref, rep_ref)
    uh, ul = _split2(u2)
    lg = _dot(uh, wrh_ref[...]) + _dot(ul, wrh_ref[...]) + _dot(uh, wrl_ref[...]) + br_ref[...]
    gate, gidx = _route(lg)
    u2_ref[...] = uh
    gate_ref[...] = gate
    if not routed:
        return
    gid_ref, col_ref = outs
    rows = []
    for blk in range(x1.shape[0] // LANES):
        col = jnp.broadcast_to(gidx[LANES * blk:LANES * (blk + 1), :], (LANES, LANES))
        rows.append(col.T[0:1, :])
    gid_ref[...] = jnp.concatenate(rows, 1)
    lanef = lax.broadcasted_iota(jnp.int32, gate.shape, 1).astype(F32)
    onehot = jnp.where(lanef == gidx, 1.0, 0.0)
    earlier = _dot(tri_ref[...], onehot.astype(BF16))
    rank = jnp.sum(onehot * earlier, axis=-1, keepdims=True)
    col_ref[...] = jnp.where(lanef == 0.0, gidx, jnp.where(lanef == 1.0, rank, 0.0))


def _post_call(mlstm, routed, h, o, x, mod, mspec, rep, nf, hn, wout, wrh, wrl, br, tm):
    n = x.shape[0]
    expand = rep is not None
    row = lambda i: (i, 0)
    const = lambda i: (0, 0)
    tok = pl.BlockSpec((tm, D_MODEL), row)
    small = pl.BlockSpec((tm, LANES), row)
    vec = pl.BlockSpec((1, D_MODEL), const)
    ins = [h] + ([o] if mlstm else []) + [x, mod, mod, mod, nf] + ([hn] if mlstm else []) \
        + [wout, wrh, wrl, br]
    specs = [tok] + ([tok] if mlstm else []) + [tok, mspec(2), mspec(3), mspec(4), vec] \
        + ([vec] if mlstm else []) \
        + [pl.BlockSpec((D_MODEL, D_MODEL), const), pl.BlockSpec((D_MODEL, LANES), const),
           pl.BlockSpec((D_MODEL, LANES), const), pl.BlockSpec((1, LANES), const)]
    out_shape = [jax.ShapeDtypeStruct((n, D_MODEL), F32), jax.ShapeDtypeStruct((n, D_MODEL), BF16),
                 jax.ShapeDtypeStruct((n, LANES), F32)]
    out_specs = [tok, tok, small]
    if routed:
        t = jnp.arange(tm)
        ins.append((t[None, :] < t[:, None]).astype(BF16))
        specs.append(pl.BlockSpec((tm, tm), const))
        out_shape += [jax.ShapeDtypeStruct((n // tm, 1, tm), F32), jax.ShapeDtypeStruct((n, LANES), F32)]
        out_specs += [pl.BlockSpec((None, 1, tm), lambda i: (i, 0, 0)), small]
    if expand:
        ins.append(rep)
        specs.append(pl.BlockSpec(rep.shape, const))
    out_shape, out_specs = tuple(out_shape), tuple(out_specs)
    return pl.pallas_call(
        functools.partial(_post_kernel, mlstm, routed, expand),
        out_shape=out_shape,
        grid=(n // tm,),
        in_specs=specs,
        out_specs=out_specs,
        compiler_params=_params(1),
        name="post_mlstm" if mlstm else "post_attn",
    )(*ins)


ITEM_ROWS = 8
RUN_ALIGN = 16
RUN_SIZES = (512, 256, 128, 64, 32, 16)
RUN_SRC, RUN_DST, RUN_LEN, RUN_TOTAL = 0, N_GROUPS, 2 * N_GROUPS, 3 * N_GROUPS


def _stage_rows(tm):
    assert N_GROUPS * (RUN_ALIGN - 1) <= LANES
    return tm + LANES


def _sorted_rows(n, tm):
    rows = n + N_GROUPS * (n // tm) * RUN_ALIGN + tm
    return -(-rows // tm) * tm


def _sort_meta_kernel(tm, nts, gid_ref, q_ref, runs_ref, items_ref):
    ntiles = gid_ref.shape[0]
    gid = gid_ref[...]
    r = lax.broadcasted_iota(jnp.int32, (tm, tm), 0)
    c = lax.broadcasted_iota(jnp.int32, (tm, tm), 1)
    before = jnp.where(r < c, 1.0, 0.0).astype(BF16)
    trow = lax.broadcasted_iota(jnp.int32, (ntiles, LANES), 0)
    lane = lax.broadcasted_iota(jnp.int32, (ntiles, LANES), 1)
    k = lax.broadcasted_iota(jnp.int32, (1, LANES), 1).astype(F32)
    zero11 = jnp.zeros((1, 1), F32)
    q = jnp.zeros((ntiles, tm), F32)
    runs = jnp.zeros((ntiles, LANES), F32)
    src = jnp.zeros((ntiles, 1), F32)
    start, nitems = zero11, zero11
    grp = jnp.zeros((1, LANES), F32)
    tile = jnp.zeros((1, LANES), F32)
    valid = jnp.zeros((1, LANES), F32)
    for g in range(N_GROUPS):
        mask = jnp.where(gid == float(g), 1.0, 0.0)
        cnt = jnp.sum(mask, axis=1, keepdims=True)
        padded = jnp.floor((cnt + (RUN_ALIGN - 1.0)) * (1.0 / RUN_ALIGN)) * RUN_ALIGN
        incl = jnp.broadcast_to(padded, (ntiles, LANES))
        d = 1
        while d < ntiles:
            incl = incl + jnp.where(trow >= d, pltpu.roll(incl, d, axis=0), 0.0)
            d *= 2
        total = incl[ntiles - 1:ntiles, 0:1]
        dst = start + incl[:, 0:1] - padded
        q = q + mask * (src + _dot(mask.astype(BF16), before))
        runs = runs + jnp.where(lane == RUN_SRC + g, src, 0.0) + jnp.where(lane == RUN_DST + g, dst, 0.0) \
            + jnp.where(lane == RUN_LEN + g, padded, 0.0)
        src = src + padded
        end = start + total
        ft = jnp.floor(start * (1.0 / tm))
        lt = jnp.floor((end - 1.0) * (1.0 / tm))
        ni = jnp.where(total > 0.0, lt - ft + 1.0, 0.0)
        inside = (k >= nitems) & (k < nitems + ni)
        grp = grp + jnp.where(inside, float(g), 0.0)
        tile = tile + jnp.where(inside, ft + (k - nitems), 0.0)
        valid = valid + jnp.where(inside, 1.0, 0.0)
        nitems = nitems + ni
        start = end
    runs = runs + jnp.where(lane == RUN_TOTAL, start, 0.0)
    live = valid > 0.0
    prev = pltpu.roll(tile, 1, axis=1)
    nxt = pltpu.roll(tile, LANES - 1, axis=1)
    first = live & ((k == 0.0) | (tile != prev))
    last = live & ((k == nitems - 1.0) | (tile != nxt))
    spare = jnp.floor((start - 1.0) * (1.0 / tm)) + 1.0 + (k - nitems)
    fill = jnp.logical_not(live) & (spare <= nts - 1.0)
    tile = jnp.where(live, tile, jnp.minimum(spare, nts - 1.0))
    grp = jnp.where(live, grp, jnp.max(grp, axis=1, keepdims=True))
    flag = lambda m: jnp.where(m, 1.0, 0.0)
    q_ref[...] = q
    runs_ref[...] = runs.astype(jnp.int32)
    table = jnp.concatenate([tile, grp, valid, flag(first | fill), flag(last | fill),
                             jnp.zeros((ITEM_ROWS - 5, LANES), F32)], 0)
    items_ref[...] = table.astype(jnp.int32)


def _sort_meta_call(gid, tm):
    ntiles = gid.shape[0]
    nts = _sorted_rows(ntiles * tm, tm) // tm
    assert nts + N_GROUPS - 1 <= LANES and tm <= RUN_SIZES[0]
    return pl.pallas_call(
        functools.partial(_sort_meta_kernel, tm, nts),
        out_shape=(jax.ShapeDtypeStruct((ntiles, tm), F32),
                   jax.ShapeDtypeStruct((ntiles, LANES), jnp.int32),
                   jax.ShapeDtypeStruct((ITEM_ROWS, LANES), jnp.int32)),
        compiler_params=_params(0),
        name="moe_sort_meta",
    )(gid)


def _run_copies(runs_ref, i, tile_ref, sorted_hbm, sem, to_sorted):
    pieces = []
    for g in range(N_GROUPS):
        src = runs_ref[i, RUN_SRC + g]
        dst = runs_ref[i, RUN_DST + g]
        length = runs_ref[i, RUN_LEN + g]
        for s in RUN_SIZES:
            def build(src=src, dst=dst, length=length, s=s):
                off = length & (-2 * s)
                a = tile_ref.at[pl.ds(pl.multiple_of(src + off, RUN_ALIGN), s), :]
                b = sorted_hbm.at[pl.ds(pl.multiple_of(dst + off, RUN_ALIGN), s), :]
                return pltpu.make_async_copy(a, b, sem) if to_sorted else pltpu.make_async_copy(b, a, sem)
            pieces.append(((length & s) != 0, build))
    return pieces


def _start(pieces):
    for pred, build in pieces:
        pl.when(pred)(lambda build=build: build().start())


def _wait(pieces):
    for pred, build in pieces:
        pl.when(pred)(lambda build=build: build().wait())


def _start_then_wait(pieces):
    _start(pieces)
    _wait(pieces)


def _scatter_kernel(tm, nts, ntiles, runs_ref, q_ref, u_ref, g_ref, su_hbm, sg_hbm,
                    stu, stg, sem_u, sem_g):
    i = pl.program_id(0)
    slot = i % 2

    def copies(tile, slot):
        return (_run_copies(runs_ref, tile, stu.at[slot], su_hbm, sem_u.at[slot], True)
                + _run_copies(runs_ref, tile, stg.at[slot], sg_hbm, sem_g.at[slot], True))

    r = lax.broadcasted_iota(jnp.int32, (_stage_rows(tm), tm), 0).astype(F32)
    perm = jnp.where(q_ref[...] == r, 1.0, 0.0).astype(BF16)
    stu[slot] = _dot(perm, u_ref[...]).astype(BF16)
    g3 = _split3(g_ref[...])
    stg[slot] = _dot(perm, g3[0]) + _dot(perm, g3[1]) + _dot(perm, g3[2])
    _start(copies(i, slot))

    @pl.when(i > 0)
    def _():
        _wait(copies(i - 1, 1 - slot))

    @pl.when(i == ntiles - 1)
    def _():
        _wait(copies(i, slot))
        stu[slot] = jnp.zeros(stu.shape[1:], stu.dtype)
        stg[slot] = jnp.zeros(stg.shape[1:], stg.dtype)
        total = runs_ref[i, RUN_TOTAL]
        tail = nts * tm - total
        nfull = tail // tm
        pieces = []
        for tile_ref, hbm, sem in ((stu.at[slot], su_hbm, sem_u.at[slot]),
                                   (stg.at[slot], sg_hbm, sem_g.at[slot])):
            for j in range(nts - ntiles):
                def full(j=j, tile_ref=tile_ref, hbm=hbm, sem=sem):
                    dst = pl.multiple_of(total + j * tm, RUN_ALIGN)
                    return pltpu.make_async_copy(tile_ref.at[pl.ds(0, tm), :], hbm.at[pl.ds(dst, tm), :], sem)
                pieces.append((j < nfull, full))
            rem = tail - nfull * tm
            for s in RUN_SIZES:
                if s >= tm:
                    continue
                def part(s=s, tile_ref=tile_ref, hbm=hbm, sem=sem):
                    dst = pl.multiple_of(total + nfull * tm + (rem & (-2 * s)), RUN_ALIGN)
                    return pltpu.make_async_copy(tile_ref.at[pl.ds(0, s), :], hbm.at[pl.ds(dst, s), :], sem)
                pieces.append(((rem & s) != 0, part))
        _start_then_wait(pieces)


def _scatter_call(runs, q3, u2, gate, tm):
    n = u2.shape[0]
    ns = _sorted_rows(n, tm)
    grid_spec = pltpu.PrefetchScalarGridSpec(
        num_scalar_prefetch=1,
        grid=(n // tm,),
        in_specs=[pl.BlockSpec((None, 1, tm), lambda i, *_: (i, 0, 0)),
                  pl.BlockSpec((tm, D_MODEL), lambda i, *_: (i, 0)),
                  pl.BlockSpec((tm, LANES), lambda i, *_: (i, 0))],
        out_specs=(pl.BlockSpec(memory_space=pl.ANY), pl.BlockSpec(memory_space=pl.ANY)),
        scratch_shapes=[pltpu.VMEM((2, _stage_rows(tm), D_MODEL), BF16),
                        pltpu.VMEM((2, _stage_rows(tm), LANES), F32),
                        pltpu.SemaphoreType.DMA((2,)), pltpu.SemaphoreType.DMA((2,))])
    return pl.pallas_call(
        functools.partial(_scatter_kernel, tm, ns // tm, n // tm),
        out_shape=(jax.ShapeDtypeStruct((ns, D_MODEL), BF16), jax.ShapeDtypeStruct((ns, LANES), F32)),
        grid_spec=grid_spec,
        compiler_params=_params(1),
        name="moe_scatter",
    )(runs, q3, u2, gate)


def _ungroup(tm, runs_ref, col_ref, ys_hbm, stage, sem):
    i = pl.program_id(0)
    slot = i % 2

    def copies(tile, slot):
        return _run_copies(runs_ref, tile, stage.at[slot], ys_hbm, sem.at[slot], False)

    @pl.when(i == 0)
    def _():
        stage[...] = jnp.zeros_like(stage)
        _start(copies(i, slot))

    @pl.when(i + 1 < pl.num_programs(0))
    def _():
        _start(copies(i + 1, 1 - slot))

    _wait(copies(i, slot))
    col = col_ref[...]
    gidx = col[:, 0:1]
    src = [runs_ref[i, RUN_SRC + g].astype(F32) for g in range(N_GROUPS)]
    first = src[N_GROUPS - 1]
    for g in range(N_GROUPS - 2, -1, -1):
        first = jnp.where(gidx == float(g), src[g], first)
    q = first + col[:, 1:2]
    lanes = lax.broadcasted_iota(jnp.int32, (tm, _stage_rows(tm)), 1).astype(F32)
    perm_t = jnp.where(q == lanes, 1.0, 0.0).astype(BF16)
    return _dot(perm_t, stage[slot])


def _moe_routed_kernel(items_ref, su_ref, sg_ref, wgu_ref, wd_ref, out_ref, acc_ref, wgu_b, wd_b):
    kk = pl.program_id(0)
    grp = items_ref[1, kk]

    @pl.when((kk == 0) | (grp != items_ref[1, jnp.maximum(kk - 1, 0)]))
    def _():
        for e in range(EXPERTS_PER_GROUP):
            wgu_b[e] = wgu_ref[e].astype(BF16)
            wd_b[e] = wd_ref[e].astype(BF16)

    @pl.when(items_ref[3, kk] == 1)
    def _():
        acc_ref[...] = jnp.zeros_like(acc_ref)

    @pl.when(items_ref[2, kk] == 1)
    def _():
        u = su_ref[...]
        gate = sg_ref[...]
        lane = lax.broadcasted_iota(jnp.int32, gate.shape, 1)
        first_lane = N_GROUPS + EXPERTS_PER_GROUP * grp
        for e in range(EXPERTS_PER_GROUP):
            gcol = jnp.sum(jnp.where(lane == first_lane + e, gate, 0.0), axis=-1, keepdims=True)
            hu = _dot(u, wgu_b[e])
            hg = hu[:, 0:D_EXPERT]
            act = hg * jax.nn.sigmoid(hg) * hu[:, D_EXPERT:2 * D_EXPERT] * gcol
            acc_ref[...] += _dot(act.astype(BF16), wd_b[e])

    @pl.when(items_ref[4, kk] == 1)
    def _():
        out_ref[...] = acc_ref[...].astype(out_ref.dtype)


def _moe_routed_call(items, su, sg, wgu, wd, layer, tm):
    ns = su.shape[0]
    n_items = ns // tm + N_GROUPS - 1
    depth = wgu.shape[0]
    wgu5 = wgu.reshape(depth, N_GROUPS, EXPERTS_PER_GROUP, D_MODEL, 2 * D_EXPERT)
    wd5 = wd.reshape(depth, N_GROUPS, EXPERTS_PER_GROUP, D_EXPERT, D_MODEL)
    grid_spec = pltpu.PrefetchScalarGridSpec(
        num_scalar_prefetch=1,
        grid=(n_items,),
        in_specs=[pl.BlockSpec((tm, D_MODEL), lambda k, it: (it[0, k], 0)),
                  pl.BlockSpec((tm, LANES), lambda k, it: (it[0, k], 0)),
                  pl.BlockSpec((None, None, EXPERTS_PER_GROUP, D_MODEL, 2 * D_EXPERT),
                               lambda k, it: (layer, it[1, k], 0, 0, 0)),
                  pl.BlockSpec((None, None, EXPERTS_PER_GROUP, D_EXPERT, D_MODEL),
                               lambda k, it: (layer, it[1, k], 0, 0, 0))],
        out_specs=pl.BlockSpec((tm, D_MODEL), lambda k, it: (it[0, k], 0)),
        scratch_shapes=[pltpu.VMEM((tm, D_MODEL), F32),
                        pltpu.VMEM((EXPERTS_PER_GROUP, D_MODEL, 2 * D_EXPERT), BF16),
                        pltpu.VMEM((EXPERTS_PER_GROUP, D_EXPERT, D_MODEL), BF16)])
    return pl.pallas_call(
        _moe_routed_kernel,
        out_shape=jax.ShapeDtypeStruct((ns, D_MODEL), BF16),
        grid_spec=grid_spec,
        compiler_params=_params(1),
        name="moe_routed",
    )(items, su, sg, wgu5, wd5)


def _resid_gather_kernel(tm, runs_ref, x_ref, g2_ref, col_ref, ys_hbm, y_ref, stage, sem):
    y_ref[...] = x_ref[...] + g2_ref[...] * _ungroup(tm, runs_ref, col_ref, ys_hbm, stage, sem)


def _resid_gather_call(x, runs, col, ys, mod, mspec, tm):
    n = x.shape[0]
    row = lambda i, *_: (i, 0)
    grid_spec = pltpu.PrefetchScalarGridSpec(
        num_scalar_prefetch=1,
        grid=(n // tm,),
        in_specs=[pl.BlockSpec((tm, D_MODEL), row), mspec(5), pl.BlockSpec((tm, LANES), row),
                  pl.BlockSpec(memory_space=pl.ANY)],
        out_specs=pl.BlockSpec((tm, D_MODEL), row),
        scratch_shapes=[pltpu.VMEM((2, _stage_rows(tm), D_MODEL), BF16), pltpu.SemaphoreType.DMA((2,))])
    return pl.pallas_call(
        functools.partial(_resid_gather_kernel, tm),
        out_shape=jax.ShapeDtypeStruct((n, D_MODEL), F32),
        grid_spec=grid_spec,
        compiler_params=_params(1),
        name="moe_unsort_resid",
    )(runs, x, mod, col, ys)


def _moe_kernel(expand, u_ref, gate_ref, wgu_ref, wd_ref, x1_ref, g2_ref, *rest):
    rep_ref = rest[0] if expand else None
    y_ref, acc_ref = rest[1 if expand else 0:]
    e = pl.program_id(1)

    @pl.when(e == 0)
    def _():
        acc_ref[...] = jnp.zeros_like(acc_ref)

    hu = _dot(u_ref[...], wgu_ref[...].astype(BF16))
    hg = hu[:, 0:D_EXPERT]
    hv = hu[:, D_EXPERT:2 * D_EXPERT]
    gate = gate_ref[...]
    lane = lax.broadcasted_iota(jnp.int32, gate.shape, 1)
    gcol = jnp.sum(jnp.where(lane == N_GROUPS + e, gate, 0.0), axis=-1, keepdims=True)
    act = hg * jax.nn.sigmoid(hg) * hv * gcol
    acc_ref[...] += _dot(act.astype(BF16), wd_ref[...].astype(BF16))

    @pl.when(e == N_EXPERTS - 1)
    def _():
        y_ref[...] = x1_ref[...] + _mod_rows(g2_ref, rep_ref) * acc_ref[...]


def _moe_call(u2, gate, wgu, wd, layer, x1, mod, mspec, rep, tm):
    n = u2.shape[0]
    row = lambda i, e: (i, 0)
    expand = rep is not None
    return pl.pallas_call(
        functools.partial(_moe_kernel, expand),
        out_shape=jax.ShapeDtypeStruct((n, D_MODEL), F32),
        grid=(n // tm, N_EXPERTS),
        in_specs=[pl.BlockSpec((tm, D_MODEL), row), pl.BlockSpec((tm, LANES), row),
                  pl.BlockSpec((None, None, D_MODEL, 2 * D_EXPERT), lambda i, e: (layer, e, 0, 0)),
                  pl.BlockSpec((None, None, D_EXPERT, D_MODEL), lambda i, e: (layer, e, 0, 0)),
                  pl.BlockSpec((tm, D_MODEL), row), mspec(5)]
        + ([pl.BlockSpec(rep.shape, lambda i, e: (0, 0))] if expand else []),
        out_specs=pl.BlockSpec((tm, D_MODEL), row),
        scratch_shapes=[pltpu.VMEM((tm, D_MODEL), F32)],
        compiler_params=_params(2),
        name="moe_dense",
    )(u2, gate, wgu, wd, x1, mod, *([rep] if expand else []))


def _rope128(x, cos, sa, sb):
    return x * cos + pltpu.roll(x, LANES - ROT_DIM // 2, axis=1) * sa \
        + pltpu.roll(x, ROT_DIM // 2, axis=1) * sb


def _kvq_kernel(gather_tm, *refs):
    if gather_tm:
        (runs_ref, x_ref, g2_ref, col_ref, ys_hbm, kvsh_ref, kvsc_ref, sh1_ref, sc1_ref, kvn_ref,
         nm_ref, wkv_ref, wq_ref, kn_ref, qn_ref, cos_ref, sa_ref, sb_ref, g64_ref, gr_ref, gb_ref,
         q_ref, k_ref, v_ref, xa_ref, stage, sem) = refs
        x = x_ref[...] + g2_ref[...] * _ungroup(gather_tm, runs_ref, col_ref, ys_hbm, stage, sem)
        xa_ref[...] = x
        rep_ref = None
    else:
        (x_ref, kvsh_ref, kvsc_ref, sh1_ref, sc1_ref, kvn_ref, nm_ref,
         wkv_ref, wq_ref, kn_ref, qn_ref, cos_ref, sa_ref, sb_ref, g64_ref, gr_ref, gb_ref,
         rep_ref, q_ref, k_ref, v_ref) = refs
        x = x_ref[...]
    xn = _unit_rms(x)
    cos, sa, sb = cos_ref[...], sa_ref[...], sb_ref[...]

    ukv = xn * kvn_ref[...] * (1.0 + _mod_rows(kvsc_ref, rep_ref)) + _mod_rows(kvsh_ref, rep_ref)
    kv = _dot(ukv.astype(BF16), wkv_ref[...])
    k = kv[:, 0:LANES]
    v_ref[...] = kv[:, LANES:2 * LANES]
    kh, kl = _split2(k * k)
    ms = _dot(kh, g64_ref[...]) + _dot(kl, g64_ref[...])
    k_ref[...] = _rope128(k * lax.rsqrt(ms + RMS_EPS) * kn_ref[...], cos, sa, sb)

    u1 = xn * nm_ref[...] * (1.0 + _mod_rows(sc1_ref, rep_ref)) + _mod_rows(sh1_ref, rep_ref)
    q = _dot(u1.astype(BF16), wq_ref[...])
    qh, ql = _split2(q * q)
    ms16 = _dot(qh, gr_ref[...]) + _dot(ql, gr_ref[...])
    r3 = _split3(lax.rsqrt(ms16 + RMS_EPS))
    rsb = _dot(r3[0], gb_ref[...]) + _dot(r3[1], gb_ref[...]) + _dot(r3[2], gb_ref[...])
    qn = q * rsb * qn_ref[...]
    for i in range(D_MODEL // LANES):
        sl = slice(LANES * i, LANES * (i + 1))
        q_ref[:, sl] = (_rope128(qn[:, sl], cos, sa, sb) * (HEAD_DIM ** -0.5)).astype(BF16)


def _kvq_call(x, routed, mod0, mod1, mspec, kvmod, kvspec, rep, kvn, nm, wkv, wq, kn, qn, tabs, tab,
              g64, gr, gb, tm):
    n = x.shape[0]
    row = lambda i, *_: (i, 0)
    const = lambda i, *_: (0, 0)
    vec = pl.BlockSpec((1, D_MODEL), const)
    tok = pl.BlockSpec((tm, D_MODEL), row)
    small = pl.BlockSpec((tm, LANES), row)
    ins = [x, kvmod, kvmod, mod1, mod1, kvn, nm, wkv, wq, kn, qn, *tabs, g64, gr, gb]
    specs = [tok, kvspec(0), kvspec(1), mspec(0), mspec(1), vec, vec,
             pl.BlockSpec((D_MODEL, 2 * LANES), const), pl.BlockSpec((D_MODEL, D_MODEL), const),
             pl.BlockSpec((1, LANES), const), vec, tab, tab, tab,
             pl.BlockSpec((LANES, LANES), const), pl.BlockSpec((D_MODEL, LANES), const),
             pl.BlockSpec((LANES, D_MODEL), const)]
    out_shape = [jax.ShapeDtypeStruct((n, D_MODEL), BF16), jax.ShapeDtypeStruct((n, LANES), F32),
                 jax.ShapeDtypeStruct((n, LANES), F32)]
    out_specs = [tok, small, small]
    scratch = []
    nprefetch = 0
    if routed is not None:
        runs, col, ys = routed
        nprefetch = 1
        ins = [runs, x, mod0, col, ys] + ins[1:]
        specs = [tok, mspec(5), small, pl.BlockSpec(memory_space=pl.ANY)] + specs[1:]
        out_shape.append(jax.ShapeDtypeStruct((n, D_MODEL), F32))
        out_specs.append(tok)
        scratch = [pltpu.VMEM((2, _stage_rows(tm), D_MODEL), BF16), pltpu.SemaphoreType.DMA((2,))]
    else:
        ins.append(rep)
        specs.append(pl.BlockSpec(rep.shape, const))
    grid_spec = pltpu.PrefetchScalarGridSpec(
        num_scalar_prefetch=nprefetch, grid=(n // tm,), in_specs=specs,
        out_specs=tuple(out_specs), scratch_shapes=scratch)
    return pl.pallas_call(
        functools.partial(_kvq_kernel, tm if routed is not None else 0),
        out_shape=tuple(out_shape),
        grid_spec=grid_spec,
        compiler_params=_params(1),
        name="kv_q_proj",
    )(*ins)


def _attn_core(q, kcat, vcat, bias, sinks_ref, o_ref):
    tk = kcat.shape[0]
    pairs = N_Q_HEADS // N_KV_HEADS // 2
    lane = lax.broadcasted_iota(jnp.int32, (1, LANES), 1)
    lo = lane < HEAD_DIM
    kro = pltpu.roll(kcat, HEAD_DIM, axis=1)
    vro = pltpu.roll(vcat, HEAD_DIM, axis=1)
    one_e = jnp.broadcast_to(jnp.where(lo, 1.0, 0.0), (tk, LANES))
    one_o = 1.0 - one_e
    for g in range(N_KV_HEADS):
        if g == 0:
            ke, ko = jnp.where(lo, kcat, 0.0), jnp.where(lo, 0.0, kro)
            ve, vo = jnp.where(lo, vcat, 0.0), jnp.where(lo, 0.0, vro)
        else:
            ke, ko = jnp.where(lo, kro, 0.0), jnp.where(lo, 0.0, kcat)
            ve, vo = jnp.where(lo, vro, 0.0), jnp.where(lo, 0.0, vcat)
        k2 = jnp.concatenate([ke, ko], 0).astype(BF16)
        v2 = jnp.concatenate([jnp.concatenate([ve, one_e], 1),
                              jnp.concatenate([vo, one_o], 1)], 0).astype(BF16)
        for p in range(pairs):
            hp = g * pairs + p
            s = _dot_nt(q[:, LANES * hp:LANES * (hp + 1)], k2)
            halves, corr = [], []
            for par in range(2):
                sp = s[:, par * tk:(par + 1) * tk] + bias
                sink = sinks_ref[2 * hp + par]
                m = jnp.maximum(jnp.max(sp, axis=-1, keepdims=True), sink)
                halves.append(jnp.exp(sp - m).astype(BF16))
                corr.append(jnp.exp(sink - m))
            o2 = _dot(jnp.concatenate(halves, 1), v2)
            den = o2[:, LANES:2 * LANES] + jnp.where(lo, corr[0], corr[1])
            o_ref[:, LANES * hp:LANES * (hp + 1)] = (o2[:, 0:LANES] / den).astype(o_ref.dtype)


def _attn_prompt_kernel(sinks_ref, q_ref, kp_ref, kc_ref, vp_ref, vc_ref, bias_ref, o_ref):
    kcat = jnp.concatenate([kp_ref[...], kc_ref[...]], 0)
    vcat = jnp.concatenate([vp_ref[...], vc_ref[...]], 0)
    _attn_core(q_ref[...], kcat, vcat, bias_ref[...], sinks_ref, o_ref)


ATTN_SEQS_PER_STEP = 4


def _attn_sample_kernel(sinks_ref, q_ref, kc_ref, kn_ref, vc_ref, vn_ref, bias_ref, o_ref):
    fill = jnp.zeros((WINDOW - SAMPLE_PAD, LANES), F32)
    for b in range(ATTN_SEQS_PER_STEP):
        kcat = jnp.concatenate([kc_ref[b], kn_ref[b], fill], 0)
        vcat = jnp.concatenate([vc_ref[b], vn_ref[b], fill], 0)
        _attn_core(q_ref[b], kcat, vcat, bias_ref[...], sinks_ref, o_ref.at[b])


def _window_bias(tq, first):
    qi = jnp.arange(tq)[:, None]
    kj = jnp.arange(2 * WINDOW)[None, :]
    ok = (kj > qi) & (kj <= qi + WINDOW)
    if first:
        ok = ok & (kj >= WINDOW)
    return jnp.where(ok, 0.0, -jnp.inf).astype(F32)


def _attn_prompt_call(sinks, q, k, v, B, nb):
    n = q.shape[0]
    cur = lambda b, i: (b * nb + i, 0)
    prev = lambda b, i: (b * nb + jnp.maximum(i - 1, 0), 0)
    kvb = lambda im: pl.BlockSpec((WINDOW, LANES), im)
    bias = jnp.stack([_window_bias(WINDOW, True), _window_bias(WINDOW, False)])
    return pl.pallas_call(
        _attn_prompt_kernel,
        out_shape=jax.ShapeDtypeStruct((n, D_MODEL), BF16),
        grid=(B, nb),
        in_specs=[pl.BlockSpec(memory_space=pltpu.SMEM),
                  pl.BlockSpec((WINDOW, D_MODEL), cur), kvb(prev), kvb(cur), kvb(prev), kvb(cur),
                  pl.BlockSpec((None, WINDOW, 2 * WINDOW), lambda b, i: (jnp.minimum(i, 1), 0, 0))],
        out_specs=pl.BlockSpec((WINDOW, D_MODEL), cur),
        compiler_params=_params(2),
        name="swa_prompt",
    )(sinks, q, k, k, v, v, bias)


def _attn_sample_call(sinks, q, kcache, knew, vcache, vnew):
    B = q.shape[0]
    nseq = ATTN_SEQS_PER_STEP
    b3 = lambda b: (b, 0, 0)
    cache = pl.BlockSpec((nseq, WINDOW, LANES), b3)
    new = pl.BlockSpec((nseq, SAMPLE_PAD, LANES), b3)
    return pl.pallas_call(
        _attn_sample_kernel,
        out_shape=jax.ShapeDtypeStruct(q.shape, BF16),
        grid=(B // nseq,),
        in_specs=[pl.BlockSpec(memory_space=pltpu.SMEM),
                  pl.BlockSpec((nseq, SAMPLE_PAD, D_MODEL), b3), cache, new, cache, new,
                  pl.BlockSpec((SAMPLE_PAD, 2 * WINDOW), lambda b: (0, 0))],
        out_specs=pl.BlockSpec((nseq, SAMPLE_PAD, D_MODEL), b3),
        compiler_params=_params(1),
        name="swa_sample",
    )(sinks, q, kcache, knew, vcache, vnew, _window_bias(SAMPLE_PAD, False))


def _rope_tables(pos):
    half = ROT_DIM // 2
    inv = ROPE_THETA ** (-jnp.arange(half, dtype=F32) / half)
    ang = pos.astype(F32)[:, None] * inv[None]
    cos, sin = jnp.cos(ang), jnp.sin(ang)
    d = jnp.arange(LANES) % HEAD_DIM
    idx = d % half
    cos_t = jnp.where(d < ROT_DIM, cos[:, idx], 1.0)
    sa = jnp.where(d < half, -sin[:, idx], 0.0)
    sb = jnp.where((d >= half) & (d < ROT_DIM), sin[:, idx], 0.0)
    return cos_t, sa, sb


def _pad_lanes(a, value=0.0):
    return jnp.pad(a, ((0, 0), (0, LANES - a.shape[1])), constant_values=value)


def _prep_weights(ada_w, ada_b, norm_mix, norm_ffn, a_w_in, a_b_gates, a_head_norm, a_w_out,
                  kv_ada_w, kv_ada_b, kv_norm, w_k, w_v, k_norm, b_w_q, b_q_norm, b_sinks, b_w_o,
                  moe_w_group, moe_b_group, moe_w_expert, moe_b_expert, moe_w_gate_up, moe_w_down):
    w = {}
    g0 = QK_COLS + 2 * V_COLS
    w["w_in"] = a_w_in
    w["w_gates"] = jnp.concatenate([_pad_lanes(a_w_in[0, :, g0:g0 + M_HEADS]),
                                    _pad_lanes(a_w_in[0, :, g0 + M_HEADS:])], 1)
    w["bli"] = _pad_lanes(a_b_gates[0][None, :M_HEADS])
    w["blf"] = _pad_lanes(a_b_gates[0][None, M_HEADS:])
    w["head_norm"] = a_head_norm[0][None]
    w["w_out"] = a_w_out[0].astype(BF16)
    w["norm_mix"] = [norm_mix[l][None] for l in range(2)]
    w["norm_ffn"] = [norm_ffn[l][None] for l in range(2)]
    w["router"] = []
    for l in range(2):
        wr = _pad_lanes(jnp.concatenate([moe_w_group[l], moe_w_expert[l]], 1))
        hi = wr.astype(BF16)
        lo = (wr - hi.astype(F32)).astype(BF16)
        br = _pad_lanes(jnp.concatenate([moe_b_group[l], moe_b_expert[l]])[None])
        w["router"].append((hi, lo, br))
    w["w_gu"] = moe_w_gate_up
    w["w_d"] = moe_w_down
    w["kv_norm"] = kv_norm[None]
    w["w_kv"] = jnp.concatenate([w_k, w_v], 1).astype(BF16)
    w["k_norm"] = jnp.tile(k_norm, N_KV_HEADS)[None]
    w["w_q"] = b_w_q[0].astype(BF16)
    w["q_norm"] = jnp.tile(b_q_norm[0], N_Q_HEADS)[None]
    w["sinks"] = b_sinks[0]
    w["w_o"] = b_w_o[0].astype(BF16)
    lanes = jnp.arange(LANES)
    feat = jnp.arange(D_MODEL)
    w["g64"] = jnp.where((lanes[:, None] // HEAD_DIM) == (lanes[None, :] // HEAD_DIM),
                         1.0 / HEAD_DIM, 0.0).astype(BF16)
    w["gr"] = jnp.where((feat[:, None] // HEAD_DIM) == lanes[None, :], 1.0 / HEAD_DIM, 0.0).astype(BF16)
    w["gb"] = jnp.where(lanes[:, None] == (feat[None, :] // HEAD_DIM), 1.0, 0.0).astype(BF16)
    return w


def _trunk(x2, mods, kvmod, mspec, mspec_moe, rep, w, *, B, T, L, tm, tm_moe, tabs, tabspec, c0, n0, m0,
           cache_k=None, cache_v=None):
    sample = cache_k is not None
    nc = T // L if not sample else 1

    qk, v, o, li, lf = _inproj_call(x2, mods[0], mspec, rep, w["norm_mix"][0], w["w_in"],
                                    w["w_gates"], w["bli"], w["blf"], tm)
    if sample:
        def padtok(a, value=0.0):
            a = a.reshape(B, T, a.shape[-1])
            a = jnp.pad(a, ((0, 0), (0, L - T), (0, 0)), constant_values=value)
            return a.reshape(B * L, a.shape[-1])
        qk, v, li, lf = padtok(qk), padtok(v), padtok(li, M_EMPTY), padtok(lf)
    h, c_new, n_new, m_new = _mlstm_call(qk, v, li, lf, c0, n0, m0, B, nc, L)
    if sample:
        h = h.reshape(B, L, V_COLS)[:, :T].reshape(B * T, V_COLS)
    routed = not sample
    ntiles = (B * T) // tm

    def routed_moe(u2, gate, gid, col, layer):
        q, runs, items = _sort_meta_call(gid.reshape(ntiles, tm), tm)
        su, sg = _scatter_call(runs, q.reshape(ntiles, 1, tm), u2, gate, tm)
        ys = _moe_routed_call(items, su, sg, w["w_gu"], w["w_d"], layer, tm)
        return runs, col, ys

    rh, rl, br = w["router"][0]
    post0 = _post_call(True, routed, h, o, x2, mods[0], mspec, rep, w["norm_ffn"][0], w["head_norm"],
                       w["w_out"], rh, rl, br, tm)
    kvq_args = (mods[0], mods[1], mspec, kvmod, mspec, rep, w["kv_norm"], w["norm_mix"][1],
                w["w_kv"], w["w_q"], w["k_norm"], w["q_norm"], tabs, tabspec,
                w["g64"], w["gr"], w["gb"], tm)

    if routed:
        x1, u2, gate, gid, col = post0
        q, k, vv, xa = _kvq_call(x1, routed_moe(u2, gate, gid, col, 0), *kvq_args)
    else:
        x1, u2, gate = post0
        xa = _moe_call(u2, gate, w["w_gu"], w["w_d"], 0, x1, mods[0], mspec_moe, rep, tm_moe)
        q, k, vv = _kvq_call(xa, None, *kvq_args)
    if not sample:
        att = _attn_prompt_call(w["sinks"], q, k, vv, B, T // WINDOW)
        k_win = k.reshape(B, T, LANES)[:, T - WINDOW:].reshape(B, WINDOW, N_KV_HEADS, HEAD_DIM)
        v_win = vv.reshape(B, T, LANES)[:, T - WINDOW:].reshape(B, WINDOW, N_KV_HEADS, HEAD_DIM)
    else:
        def padseq(a):
            return jnp.pad(a.reshape(B, T, a.shape[-1]), ((0, 0), (0, SAMPLE_PAD - T), (0, 0)))
        kc = cache_k.reshape(B, WINDOW, LANES)
        vc = cache_v.reshape(B, WINDOW, LANES)
        att = _attn_sample_call(w["sinks"], padseq(q), kc, padseq(k), vc, padseq(vv))
        att = att[:, :T].reshape(B * T, D_MODEL)
        k_win = jnp.concatenate([kc[:, T:], k.reshape(B, T, LANES)], 1)
        v_win = jnp.concatenate([vc[:, T:], vv.reshape(B, T, LANES)], 1)
        k_win = k_win.reshape(B, WINDOW, N_KV_HEADS, HEAD_DIM)
        v_win = v_win.reshape(B, WINDOW, N_KV_HEADS, HEAD_DIM)
    rh, rl, br = w["router"][1]
    post1 = _post_call(False, routed, att, None, xa, mods[1], mspec, rep, w["norm_ffn"][1], None,
                       w["w_o"], rh, rl, br, tm)
    if routed:
        x3, u4, gate, gid, col = post1
        runs, col, ys = routed_moe(u4, gate, gid, col, 1)
        y = _resid_gather_call(x3, runs, col, ys, mods[1], mspec, tm)
    else:
        x3, u4, gate = post1
        y = _moe_call(u4, gate, w["w_gu"], w["w_d"], 1, x3, mods[1], mspec_moe, rep, tm_moe)
    c_out = c_new[None]
    n_out = n_new.reshape(1, B, M_HEADS, M_DK)
    m_out = m_new[:, 0, :M_HEADS][None]
    return y, c_out, n_out, m_out, k_win, v_win


def kernel(x_prompt, x_sample, c_prompt, c_sample, state_c, state_n, state_m, cache_k_win, cache_v_win, ada_w, ada_b, norm_mix, norm_ffn, a_w_in, a_b_gates, a_head_norm, a_w_out, kv_ada_w, kv_ada_b, kv_norm, w_k, w_v, k_norm, b_w_q, b_q_norm, b_sinks, b_w_o, moe_w_group, moe_b_group, moe_w_expert, moe_b_expert, moe_w_gate_up, moe_w_down):
    Bp, Tp, D = x_prompt.shape
    Bs, Ts, _ = x_sample.shape
    w = _prep_weights(ada_w, ada_b, norm_mix, norm_ffn, a_w_in, a_b_gates, a_head_norm, a_w_out,
                      kv_ada_w, kv_ada_b, kv_norm, w_k, w_v, k_norm, b_w_q, b_q_norm, b_sinks, b_w_o,
                      moe_w_group, moe_b_group, moe_w_expert, moe_b_expert, moe_w_gate_up, moe_w_down)

    rows = Bp + Bs
    rpad = -rows % 8
    c_all = jnp.concatenate([c_prompt, c_sample, jnp.zeros((rpad, D), F32)], 0)
    mod = _ada_call(c_all, ada_w, ada_b[:, None, :])
    kvm = _ada_call(c_all, kv_ada_w[None], kv_ada_b[None, None, :])

    tm_p = 512
    tiles_per_seq = Tp // tm_p
    mods_p = [mod[l, :Bp][:, None, :] for l in range(2)]
    kvmod_p = kvm[0, :Bp][:, None, :]

    def mspec_p(col):
        return pl.BlockSpec((None, 1, D_MODEL), lambda i, *_: (i // tiles_per_seq, 0, col))

    tm_moe_p = 1024
    moe_tiles_per_seq = Tp // tm_moe_p

    def mspec_moe_p(col):
        return pl.BlockSpec((None, 1, D_MODEL), lambda i, *_: (i // moe_tiles_per_seq, 0, col))

    tabs_p = _rope_tables(jnp.arange(Tp, dtype=jnp.int32))
    tabspec_p = pl.BlockSpec((tm_p, LANES), lambda i, *_: (i % tiles_per_seq, 0))
    npairs = M_HEADS // 2
    c0 = jnp.zeros((Bp, M_HEADS, M_DK, M_DV), F32)
    n0 = jnp.zeros((Bp, npairs, LANES), F32)
    m0 = jnp.pad(jnp.full((Bp, 1, M_HEADS), M_EMPTY, F32), ((0, 0), (0, 0), (0, LANES - M_HEADS)))
    yp, cp, np_, mp, kwp, vwp = _trunk(
        x_prompt.reshape(Bp * Tp, D), mods_p, kvmod_p, mspec_p, mspec_moe_p, None, w,
        B=Bp, T=Tp, L=M_CHUNK, tm=tm_p, tm_moe=tm_moe_p, tabs=tabs_p, tabspec=tabspec_p,
        c0=c0, n0=n0, m0=m0)

    ns = Bs * Ts
    mods_s = [mod[l, Bp:Bp + Bs][None] for l in range(2)]
    kvmod_s = kvm[0, Bp:Bp + Bs][None]
    rep = (jnp.arange(ns)[:, None] // Ts == jnp.arange(Bs)[None, :]).astype(BF16)

    def mspec_s(col):
        return pl.BlockSpec((None, Bs, D_MODEL), lambda i, *_: (0, 0, col))

    tabs_s = _rope_tables(PAST_LEN + jnp.arange(Ts, dtype=jnp.int32))
    tabs_s = tuple(jnp.tile(t, (Bs, 1)) for t in tabs_s)
    m0s = jnp.pad(state_m[0][:, None, :], ((0, 0), (0, 0), (0, LANES - M_HEADS)))
    ys, cs, ns_, ms, kws, vws = _trunk(
        x_sample.reshape(ns, D), mods_s, kvmod_s, mspec_s, mspec_s, rep, w,
        B=Bs, T=Ts, L=SAMPLE_PAD, tm=ns, tm_moe=ns, tabs=tabs_s,
        tabspec=pl.BlockSpec((ns, LANES), lambda i, *_: (0, 0)),
        c0=state_c[0], n0=state_n[0].reshape(Bs, npairs, LANES), m0=m0s,
        cache_k=cache_k_win, cache_v=cache_v_win)

    return (yp.reshape(Bp, Tp, D), ys.reshape(Bs, Ts, D), cp, np_, mp, kwp, vwp,
            cs, ns_, ms, kws, vws)
```

```python
import functools

import jax
import jax.numpy as jnp
from jax import lax
from jax.experimental import pallas as pl
from jax.experimental.pallas import tpu as pltpu

F32 = jnp.float32
BF16 = jnp.bfloat16

D_MODEL = 1024
PAST_LEN = 8192
M_HEADS = 8
M_DK = 64
M_DV = 128
M_CHUNK = 64
GATE_SOFTCAP = 15.0
M_EMPTY = -1e30
WINDOW = 128
HEAD_DIM = 64
N_Q_HEADS = 16
N_KV_HEADS = 2
ROPE_THETA = 500000.0
ROT_DIM = 16
N_GROUPS = 4
EXPERTS_PER_GROUP = 4
N_EXPERTS = 16
D_EXPERT = 256
RMS_EPS = 1e-6

LANES = 128
QK_COLS = 2 * M_HEADS * M_DK
V_COLS = M_HEADS * M_DV
IN_COLS = QK_COLS + 2 * V_COLS + 2 * LANES
SAMPLE_PAD = 16
VMEM_LIMIT = 52 * 1024 * 1024

NT_DIMS = (((1,), (1,)), ((), ()))


def _params(n_axes):
    return pltpu.CompilerParams(dimension_semantics=("arbitrary",) * n_axes,
                                vmem_limit_bytes=VMEM_LIMIT)


def _dot(a, b):
    return jnp.dot(a, b, preferred_element_type=F32)


def _dot_nt(a, b):
    return lax.dot_general(a, b, NT_DIMS, preferred_element_type=F32)


def _split2(x):
    hi = x.astype(BF16)
    lo = (x - hi.astype(F32)).astype(BF16)
    return hi, lo


def _split3(x):
    hi = x.astype(BF16)
    r = x - hi.astype(F32)
    mid = r.astype(BF16)
    lo = (r - mid.astype(F32)).astype(BF16)
    return hi, mid, lo


def _unit_rms(x):
    return x * lax.rsqrt(jnp.mean(x * x, axis=-1, keepdims=True) + RMS_EPS)


def _ada_kernel(c_ref, w_ref, b_ref, o_ref):
    c = c_ref[...]
    cs = (c * jax.nn.sigmoid(c)).astype(BF16)
    o_ref[...] = _dot(cs, w_ref[...].astype(BF16)) + b_ref[...]


def _ada_call(c, w, b):
    g, d, n = w.shape
    r = c.shape[0]
    tn = 1024
    return pl.pallas_call(
        _ada_kernel,
        out_shape=jax.ShapeDtypeStruct((g, r, n), F32),
        grid=(g, n // tn),
        in_specs=[pl.BlockSpec((r, d), lambda i, j: (0, 0)),
                  pl.BlockSpec((None, d, tn), lambda i, j: (i, 0, j)),
                  pl.BlockSpec((None, 1, tn), lambda i, j: (i, 0, j))],
        out_specs=pl.BlockSpec((None, r, tn), lambda i, j: (i, 0, j)),
        compiler_params=_params(2),
        name="ada_mod",
    )(c, w, b)


def _mod_rows(ref, rep_ref):
    if rep_ref is None:
        return ref[...]
    hi, mid, lo = _split3(ref[...])
    rep = rep_ref[...]
    return _dot(rep, hi) + _dot(rep, mid) + _dot(rep, lo)


def _inproj_kernel(expand, x_ref, sh_ref, sc_ref, nw_ref, w_ref, wg_ref, bli_ref, blf_ref, *rest):
    rep_ref = rest[0] if expand else None
    qk_ref, v_ref, o_ref, li_ref, lf_ref, wb_ref = rest[1 if expand else 0:]

    @pl.when(pl.program_id(0) == 0)
    def _():
        wb_ref[...] = w_ref[...].astype(BF16)

    u = _unit_rms(x_ref[...]) * nw_ref[...] * (1.0 + _mod_rows(sc_ref, rep_ref)) \
        + _mod_rows(sh_ref, rep_ref)
    ub = u.astype(BF16)
    half = QK_COLS // 2
    q = _dot(ub, wb_ref[:, 0:half]) * (M_DK ** -0.5)
    qk_ref[:, 0:half] = q.astype(BF16)
    qk_ref[:, half:QK_COLS] = _dot(ub, wb_ref[:, half:QK_COLS]).astype(BF16)
    v_ref[...] = _dot(ub, wb_ref[:, QK_COLS:QK_COLS + V_COLS]).astype(BF16)
    o_ref[...] = _dot(ub, wb_ref[:, QK_COLS + V_COLS:QK_COLS + 2 * V_COLS]).astype(BF16)
    lane = lax.broadcasted_iota(jnp.int32, (1, LANES), 1)
    live = lane < M_HEADS
    wg = wg_ref[...].astype(BF16)
    gi = _dot(ub, wg[:, 0:LANES]) + bli_ref[...]
    gf = _dot(ub, wg[:, LANES:2 * LANES]) + blf_ref[...]
    li = GATE_SOFTCAP * jnp.tanh(gi / GATE_SOFTCAP)
    fpre = GATE_SOFTCAP * jnp.tanh(gf / GATE_SOFTCAP)
    lf = jnp.minimum(fpre, 0.0) - jnp.log1p(jnp.exp(-jnp.abs(fpre)))
    li_ref[...] = jnp.where(live, li, 0.0)
    lf_ref[...] = jnp.where(live, lf, 0.0)


def _inproj_call(x, mod, mspec, rep, nw, w_in, wg, bli, blf, tm):
    n = x.shape[0]
    row = lambda i: (i, 0)
    const = lambda i: (0, 0)
    main_cols = QK_COLS + 2 * V_COLS
    expand = rep is not None
    return pl.pallas_call(
        functools.partial(_inproj_kernel, expand),
        out_shape=(jax.ShapeDtypeStruct((n, QK_COLS), BF16),
                   jax.ShapeDtypeStruct((n, V_COLS), BF16),
                   jax.ShapeDtypeStruct((n, V_COLS), BF16),
                   jax.ShapeDtypeStruct((n, LANES), F32),
                   jax.ShapeDtypeStruct((n, LANES), F32)),
        grid=(n // tm,),
        in_specs=[pl.BlockSpec((tm, D_MODEL), row), mspec(0), mspec(1),
                  pl.BlockSpec((1, D_MODEL), const),
                  pl.BlockSpec((None, D_MODEL, main_cols), lambda i: (0, 0, 0),
                               pipeline_mode=pl.Buffered(1)),
                  pl.BlockSpec((D_MODEL, 2 * LANES), const),
                  pl.BlockSpec((1, LANES), const), pl.BlockSpec((1, LANES), const)]
        + ([pl.BlockSpec(rep.shape, const)] if expand else []),
        out_specs=(pl.BlockSpec((tm, QK_COLS), row), pl.BlockSpec((tm, V_COLS), row),
                   pl.BlockSpec((tm, V_COLS), row), pl.BlockSpec((tm, LANES), row),
                   pl.BlockSpec((tm, LANES), row)),
        scratch_shapes=[pltpu.VMEM((D_MODEL, main_cols), BF16)],
        compiler_params=_params(1),
        name="mlstm_inproj",
    )(x, mod, mod, nw, w_in, wg, bli, blf, *([rep] if expand else []))


def _mlstm_kernel(L, nseq, *refs):
    c = pl.program_id(1)
    seqs = [[r.at[b] for r in refs] for b in range(nseq)]

    @pl.when(c == 0)
    def _():
        for s in seqs:
            _mlstm_load_state(*s[4:7], *s[11:14])

    for s in seqs:
        _mlstm_seq_step(L, *s[0:4], s[7], *s[11:14])

    @pl.when(c == pl.num_programs(1) - 1)
    def _():
        for s in seqs:
            _mlstm_store_state(*s[8:14])


def _mlstm_load_state(c0_ref, n0_ref, m0_ref, c2_s, n_s, m_s):
    m_s[...] = m0_ref[...]
    n_s[...] = n0_ref[...]
    z = jnp.zeros((M_DK, M_DV), F32)
    for j in range(M_HEADS // 2):
        c2_s[j] = jnp.concatenate([jnp.concatenate([c0_ref[2 * j], z], 1),
                                   jnp.concatenate([z, c0_ref[2 * j + 1]], 1)], 0)


def _mlstm_seq_step(L, qk_ref, v_ref, li_ref, lf_ref, h_ref, c2_s, n_s, m_s):
    npairs = M_HEADS // 2
    hd = M_DK

    lane = lax.broadcasted_iota(jnp.int32, (1, LANES), 1)
    lo128 = lane < hd
    lane256 = lax.broadcasted_iota(jnp.int32, (1, 2 * LANES), 1)
    lo256 = lane256 < LANES
    row128 = lax.broadcasted_iota(jnp.int32, (LANES, 1), 0)
    top = row128 < hd
    blockdiag = (top & lo256) | (jnp.logical_not(top) & jnp.logical_not(lo256))

    LI = li_ref[...]
    LF = lf_ref[...]
    rowL = lax.broadcasted_iota(jnp.int32, (L, LANES), 0)

    def prefix(x, op, ident):
        d = 1
        while d < L:
            shifted = pltpu.roll(x, d, axis=0)
            x = op(x, jnp.where(rowL >= d, shifted, ident))
            d *= 2
        return x

    Bc = prefix(LF, jnp.add, 0.0)
    Cm = LI - Bc
    mprev = m_s[...]
    Gc = jnp.maximum(mprev, prefix(Cm, jnp.maximum, -jnp.inf))
    A = jnp.exp(mprev - Gc)
    EXPM = jnp.exp(-(Bc + Gc))
    bL = Bc[L - 1:L, :]
    DL = bL + Cm
    mnew = jnp.maximum(bL + mprev, jnp.max(DL, axis=0, keepdims=True))
    ast = jnp.exp(bL + mprev - mnew)
    WST = jnp.exp(DL - mnew)

    def pad_rows(x, rows):
        if x.shape[0] == rows:
            return x
        return jnp.concatenate([x, jnp.zeros((rows - x.shape[0], x.shape[1]), x.dtype)], 0)

    cm_pad = pad_rows(Cm, hd)
    XT = jnp.concatenate([cm_pad, cm_pad], 0).T
    causal = (lane & (hd - 1)) <= rowL
    J = jnp.where(blockdiag, 1.0, 0.0).astype(BF16)

    def bc(X, h):
        return jnp.broadcast_to(X[:, h:h + 1], X.shape)

    for j in range(npairs):
        he, ho = 2 * j, 2 * j + 1

        def pair128(X):
            return jnp.where(lo128, bc(X, he), bc(X, ho))

        def pair256(X):
            return jnp.concatenate([bc(X, he), bc(X, ho)], 1)

        q128 = qk_ref[:, LANES * j:LANES * (j + 1)]
        k128 = qk_ref[:, QK_COLS // 2 + LANES * j:QK_COLS // 2 + LANES * (j + 1)]
        v256 = v_ref[:, 2 * LANES * j:2 * LANES * (j + 1)]
        zk = jnp.zeros_like(k128)
        zv = jnp.zeros_like(v256)
        K2t = jnp.concatenate([pad_rows(jnp.where(lo128, k128, zk), hd),
                               pad_rows(jnp.where(lo128, zk, k128), hd)], 0)
        V2 = jnp.concatenate([pad_rows(jnp.where(lo256, v256, zv), hd),
                              pad_rows(jnp.where(lo256, zv, v256), hd)], 0)
        S = _dot_nt(q128, K2t)
        crow = jnp.where(lo128, XT[he:he + 1, :], XT[ho:ho + 1, :])
        arg = jnp.where(causal, crow - pair128(Gc), -jnp.inf)
        Sw = (S * jnp.exp(arg)).astype(BF16)
        num_intra = _dot(Sw, V2)
        rowsum = _dot(Sw, J)
        C2 = c2_s[j]
        inter = _dot(q128, C2.astype(BF16))
        npair = n_s[j:j + 1, :]
        Nrow2 = jnp.concatenate(
            [jnp.broadcast_to(jnp.where(lo128, npair, 0.0), (LANES, LANES)),
             jnp.broadcast_to(jnp.where(lo128, 0.0, npair), (LANES, LANES))], 0).astype(BF16)
        qn = _dot_nt(q128, Nrow2)
        a256 = pair256(A)
        num = a256 * inter + num_intra
        den = a256 * qn + rowsum
        h = num / jnp.maximum(jnp.abs(den), pair256(EXPM))
        h_ref[:, 2 * LANES * j:2 * LANES * (j + 1)] = h.astype(h_ref.dtype)

        kw = k128.astype(F32) * pair128(WST)
        n_s[j:j + 1, :] = pair128(ast) * npair + jnp.sum(kw, axis=0, keepdims=True)
        kwT = pad_rows(kw, LANES).T.astype(BF16)
        dC = _dot(kwT, pad_rows(v256, LANES))
        c2_s[j] = pair256(ast) * C2 + jnp.where(blockdiag, dC, 0.0)

    m_s[...] = mnew


def _mlstm_store_state(cout_ref, nout_ref, mout_ref, c2_s, n_s, m_s):
    for j in range(M_HEADS // 2):
        C2 = c2_s[j]
        cout_ref[2 * j] = C2[0:M_DK, 0:M_DV]
        cout_ref[2 * j + 1] = C2[M_DK:2 * M_DK, M_DV:2 * M_DV]
    nout_ref[...] = n_s[...]
    mout_ref[...] = m_s[...]


MLSTM_SEQS_PER_STEP = 4


def _mlstm_call(qk, v, li, lf, c0, n0, m0, B, nc, L):
    npairs = M_HEADS // 2
    nseq = MLSTM_SEQS_PER_STEP
    T = nc * L
    tok3 = lambda a: a.reshape(B, T, a.shape[-1])
    tok = lambda b, c: (b, c, 0)
    st4 = lambda b, c: (b, 0, 0, 0)
    st3 = lambda b, c: (b, 0, 0)
    h, c_new, n_new, m_new = pl.pallas_call(
        functools.partial(_mlstm_kernel, L, nseq),
        out_shape=(jax.ShapeDtypeStruct((B, T, V_COLS), BF16),
                   jax.ShapeDtypeStruct((B, M_HEADS, M_DK, M_DV), F32),
                   jax.ShapeDtypeStruct((B, npairs, LANES), F32),
                   jax.ShapeDtypeStruct((B, 1, LANES), F32)),
        grid=(B // nseq, nc),
        in_specs=[pl.BlockSpec((nseq, L, QK_COLS), tok), pl.BlockSpec((nseq, L, V_COLS), tok),
                  pl.BlockSpec((nseq, L, LANES), tok), pl.BlockSpec((nseq, L, LANES), tok),
                  pl.BlockSpec((nseq, M_HEADS, M_DK, M_DV), st4),
                  pl.BlockSpec((nseq, npairs, LANES), st3),
                  pl.BlockSpec((nseq, 1, LANES), st3)],
        out_specs=(pl.BlockSpec((nseq, L, V_COLS), tok),
                   pl.BlockSpec((nseq, M_HEADS, M_DK, M_DV), st4),
                   pl.BlockSpec((nseq, npairs, LANES), st3),
                   pl.BlockSpec((nseq, 1, LANES), st3)),
        scratch_shapes=[pltpu.VMEM((nseq, npairs, 2 * M_DK, 2 * M_DV), F32),
                        pltpu.VMEM((nseq, npairs, LANES), F32),
                        pltpu.VMEM((nseq, 1, LANES), F32)],
        compiler_params=_params(2),
        name="mlstm_chunks",
    )(tok3(qk), tok3(v), tok3(li), tok3(lf), c0, n0, m0)
    return h.reshape(B * T, V_COLS), c_new, n_new, m_new


def _route(lg):
    lane = lax.broadcasted_iota(jnp.int32, lg.shape, 1)
    lanef = lane.astype(F32)
    neg = -jnp.inf
    far = float(LANES)
    gm = lane < N_GROUPS
    lgm = jnp.where(gm, lg, neg)
    gmax = jnp.max(lgm, axis=-1, keepdims=True)
    gsum = jnp.sum(jnp.exp(lgm - gmax), axis=-1, keepdims=True)
    g_w = 1.0 / gsum
    gidx = jnp.min(jnp.where(gm & (lg == gmax), lanef, far), axis=-1, keepdims=True)
    first = N_GROUPS + EXPERTS_PER_GROUP * gidx
    sel = (lanef >= first) & (lanef < first + EXPERTS_PER_GROUP)
    l1 = jnp.max(jnp.where(sel, lg, neg), axis=-1, keepdims=True)
    i1 = jnp.min(jnp.where(sel & (lg == l1), lanef, far), axis=-1, keepdims=True)
    sel2 = sel & (lanef != i1)
    l2 = jnp.max(jnp.where(sel2, lg, neg), axis=-1, keepdims=True)
    i2 = jnp.min(jnp.where(sel2 & (lg == l2), lanef, far), axis=-1, keepdims=True)
    r = jnp.exp(l2 - l1)
    w1 = g_w / (1.0 + r)
    w2 = w1 * r
    return jnp.where(lanef == i1, w1, jnp.where(lanef == i2, w2, 0.0)), gidx


def _post_kernel(mlstm, routed, expand, *refs):
    refs = list(refs)
    h_ref = refs.pop(0)
    o_ref = refs.pop(0) if mlstm else None
    x_ref, g1_ref, sh2_ref, sc2_ref, nf_ref = refs[:5]
    refs = refs[5:]
    hn_ref = refs.pop(0) if mlstm else None
    wout_ref, wrh_ref, wrl_ref, br_ref = refs[:4]
    refs = refs[4:]
    tri_ref = refs.pop(0) if routed else None
    rep_ref = refs.pop(0) if expand else None
    x1_ref, u2_ref, gate_ref = refs[:3]
    outs = refs[3:]
    if mlstm:
        hf = h_ref[...].astype(F32)
        parts = [_unit_rms(hf[:, M_DV * i:M_DV * (i + 1)]) for i in range(M_HEADS)]
        hn = jnp.concatenate(parts, 1) * hn_ref[...]
        hg = (hn * jax.nn.sigmoid(o_ref[...].astype(F32))).astype(BF16)
    else:
        hg = h_ref[...]
    x1 = x_ref[...] + _mod_rows(g1_ref, rep_ref) * _dot(hg, wout_ref[...])
    x1_ref[...] = x1
    u2 = _unit_rms(x1) * nf_ref[...] * (1.0 + _mod_rows(sc2_ref, rep_ref)) + _mod_rows(sh2_ref, rep_ref)
    uh, ul = _split2(u2)
    lg = _dot(uh, wrh_ref[...]) + _dot(ul, wrh_ref[...]) + _dot(uh, wrl_ref[...]) + br_ref[...]
    gate, gidx = _route(lg)
    u2_ref[...] = uh
    gate_ref[...] = gate
    if not routed:
        return
    gid_ref, col_ref = outs
    rows = []
    for blk in range(x1.shape[0] // LANES):
        col = jnp.broadcast_to(gidx[LANES * blk:LANES * (blk + 1), :], (LANES, LANES))
        rows.append(col.T[0:1, :])
    gid_ref[...] = jnp.concatenate(rows, 1)
    lanef = lax.broadcasted_iota(jnp.int32, gate.shape, 1).astype(F32)
    onehot = jnp.where(lanef == gidx, 1.0, 0.0)
    earlier = _dot(tri_ref[...], onehot.astype(BF16))
    rank = jnp.sum(onehot * earlier, axis=-1, keepdims=True)
    col_ref[...] = jnp.where(lanef == 0.0, gidx, jnp.where(lanef == 1.0, rank, 0.0))


def _post_call(mlstm, routed, h, o, x, mod, mspec, rep, nf, hn, wout, wrh, wrl, br, tm):
    n = x.shape[0]
    expand = rep is not None
    row = lambda i: (i, 0)
    const = lambda i: (0, 0)
    tok = pl.BlockSpec((tm, D_MODEL), row)
    small = pl.BlockSpec((tm, LANES), row)
    vec = pl.BlockSpec((1, D_MODEL), const)
    ins = [h] + ([o] if mlstm else []) + [x, mod, mod, mod, nf] + ([hn] if mlstm else []) \
        + [wout, wrh, wrl, br]
    specs = [tok] + ([tok] if mlstm else []) + [tok, mspec(2), mspec(3), mspec(4), vec] \
        + ([vec] if mlstm else []) \
        + [pl.BlockSpec((D_MODEL, D_MODEL), const), pl.BlockSpec((D_MODEL, LANES), const),
           pl.BlockSpec((D_MODEL, LANES), const), pl.BlockSpec((1, LANES), const)]
    out_shape = [jax.ShapeDtypeStruct((n, D_MODEL), F32), jax.ShapeDtypeStruct((n, D_MODEL), BF16),
                 jax.ShapeDtypeStruct((n, LANES), F32)]
    out_specs = [tok, tok, small]
    if routed:
        t = jnp.arange(tm)
        ins.append((t[None, :] < t[:, None]).astype(BF16))
        specs.append(pl.BlockSpec((tm, tm), const))
        out_shape += [jax.ShapeDtypeStruct((n // tm, 1, tm), F32), jax.ShapeDtypeStruct((n, LANES), F32)]
        out_specs += [pl.BlockSpec((None, 1, tm), lambda i: (i, 0, 0)), small]
    if expand:
        ins.append(rep)
        specs.append(pl.BlockSpec(rep.shape, const))
    out_shape, out_specs = tuple(out_shape), tuple(out_specs)
    return pl.pallas_call(
        functools.partial(_post_kernel, mlstm, routed, expand),
        out_shape=out_shape,
        grid=(n // tm,),
        in_specs=specs,
        out_specs=out_specs,
        compiler_params=_params(1),
        name="post_mlstm" if mlstm else "post_attn",
    )(*ins)


ITEM_ROWS = 8
RUN_ALIGN = 16
RUN_SIZES = (512, 256, 128, 64, 32, 16)
RUN_SRC, RUN_DST, RUN_LEN, RUN_TOTAL = 0, N_GROUPS, 2 * N_GROUPS, 3 * N_GROUPS


def _stage_rows(tm):
    assert N_GROUPS * (RUN_ALIGN - 1) <= LANES
    return tm + LANES


def _sorted_rows(n, tm):
    rows = n + N_GROUPS * (n // tm) * RUN_ALIGN + tm
    return -(-rows // tm) * tm


def _sort_meta_kernel(tm, nts, gid_ref, q_ref, runs_ref, items_ref):
    ntiles = gid_ref.shape[0]
    gid = gid_ref[...]
    r = lax.broadcasted_iota(jnp.int32, (tm, tm), 0)
    c = lax.broadcasted_iota(jnp.int32, (tm, tm), 1)
    before = jnp.where(r < c, 1.0, 0.0).astype(BF16)
    trow = lax.broadcasted_iota(jnp.int32, (ntiles, LANES), 0)
    lane = lax.broadcasted_iota(jnp.int32, (ntiles, LANES), 1)
    k = lax.broadcasted_iota(jnp.int32, (1, LANES), 1).astype(F32)
    zero11 = jnp.zeros((1, 1), F32)
    q = jnp.zeros((ntiles, tm), F32)
    runs = jnp.zeros((ntiles, LANES), F32)
    src = jnp.zeros((ntiles, 1), F32)
    start, nitems = zero11, zero11
    grp = jnp.zeros((1, LANES), F32)
    tile = jnp.zeros((1, LANES), F32)
    valid = jnp.zeros((1, LANES), F32)
    for g in range(N_GROUPS):
        mask = jnp.where(gid == float(g), 1.0, 0.0)
        cnt = jnp.sum(mask, axis=1, keepdims=True)
        padded = jnp.floor((cnt + (RUN_ALIGN - 1.0)) * (1.0 / RUN_ALIGN)) * RUN_ALIGN
        incl = jnp.broadcast_to(padded, (ntiles, LANES))
        d = 1
        while d < ntiles:
            incl = incl + jnp.where(trow >= d, pltpu.roll(incl, d, axis=0), 0.0)
            d *= 2
        total = incl[ntiles - 1:ntiles, 0:1]
        dst = start + incl[:, 0:1] - padded
        q = q + mask * (src + _dot(mask.astype(BF16), before))
        runs = runs + jnp.where(lane == RUN_SRC + g, src, 0.0) + jnp.where(lane == RUN_DST + g, dst, 0.0) \
            + jnp.where(lane == RUN_LEN + g, padded, 0.0)
        src = src + padded
        end = start + total
        ft = jnp.floor(start * (1.0 / tm))
        lt = jnp.floor((end - 1.0) * (1.0 / tm))
        ni = jnp.where(total > 0.0, lt - ft + 1.0, 0.0)
        inside = (k >= nitems) & (k < nitems + ni)
        grp = grp + jnp.where(inside, float(g), 0.0)
        tile = tile + jnp.where(inside, ft + (k - nitems), 0.0)
        valid = valid + jnp.where(inside, 1.0, 0.0)
        nitems = nitems + ni
        start = end
    runs = runs + jnp.where(lane == RUN_TOTAL, start, 0.0)
    live = valid > 0.0
    prev = pltpu.roll(tile, 1, axis=1)
    nxt = pltpu.roll(tile, LANES - 1, axis=1)
    first = live & ((k == 0.0) | (tile != prev))
    last = live & ((k == nitems - 1.0) | (tile != nxt))
    spare = jnp.floor((start - 1.0) * (1.0 / tm)) + 1.0 + (k - nitems)
    fill = jnp.logical_not(live) & (spare <= nts - 1.0)
    tile = jnp.where(live, tile, jnp.minimum(spare, nts - 1.0))
    grp = jnp.where(live, grp, jnp.max(grp, axis=1, keepdims=True))
    flag = lambda m: jnp.where(m, 1.0, 0.0)
    q_ref[...] = q
    runs_ref[...] = runs.astype(jnp.int32)
    table = jnp.concatenate([tile, grp, valid, flag(first | fill), flag(last | fill),
                             jnp.zeros((ITEM_ROWS - 5, LANES), F32)], 0)
    items_ref[...] = table.astype(jnp.int32)


def _sort_meta_call(gid, tm):
    ntiles = gid.shape[0]
    nts = _sorted_rows(ntiles * tm, tm) // tm
    assert nts + N_GROUPS - 1 <= LANES and tm <= RUN_SIZES[0]
    return pl.pallas_call(
        functools.partial(_sort_meta_kernel, tm, nts),
        out_shape=(jax.ShapeDtypeStruct((ntiles, tm), F32),
                   jax.ShapeDtypeStruct((ntiles, LANES), jnp.int32),
                   jax.ShapeDtypeStruct((ITEM_ROWS, LANES), jnp.int32)),
        compiler_params=_params(0),
        name="moe_sort_meta",
    )(gid)


def _run_copies(runs_ref, i, tile_ref, sorted_hbm, sem, to_sorted):
    pieces = []
    for g in range(N_GROUPS):
        src = runs_ref[i, RUN_SRC + g]
        dst = runs_ref[i, RUN_DST + g]
        length = runs_ref[i, RUN_LEN + g]
        for s in RUN_SIZES:
            def build(src=src, dst=dst, length=length, s=s):
                off = length & (-2 * s)
                a = tile_ref.at[pl.ds(pl.multiple_of(src + off, RUN_ALIGN), s), :]
                b = sorted_hbm.at[pl.ds(pl.multiple_of(dst + off, RUN_ALIGN), s), :]
                return pltpu.make_async_copy(a, b, sem) if to_sorted else pltpu.make_async_copy(b, a, sem)
            pieces.append(((length & s) != 0, build))
    return pieces


def _start(pieces):
    for pred, build in pieces:
        pl.when(pred)(lambda build=build: build().start())


def _wait(pieces):
    for pred, build in pieces:
        pl.when(pred)(lambda build=build: build().wait())


def _start_then_wait(pieces):
    _start(pieces)
    _wait(pieces)


def _scatter_kernel(tm, nts, ntiles, runs_ref, q_ref, u_ref, g_ref, su_hbm, sg_hbm,
                    stu, stg, sem_u, sem_g):
    i = pl.program_id(0)
    slot = i % 2

    def copies(tile, slot):
        return (_run_copies(runs_ref, tile, stu.at[slot], su_hbm, sem_u.at[slot], True)
                + _run_copies(runs_ref, tile, stg.at[slot], sg_hbm, sem_g.at[slot], True))

    r = lax.broadcasted_iota(jnp.int32, (_stage_rows(tm), tm), 0).astype(F32)
    perm = jnp.where(q_ref[...] == r, 1.0, 0.0).astype(BF16)
    stu[slot] = _dot(perm, u_ref[...]).astype(BF16)
    g3 = _split3(g_ref[...])
    stg[slot] = _dot(perm, g3[0]) + _dot(perm, g3[1]) + _dot(perm, g3[2])
    _start(copies(i, slot))

    @pl.when(i > 0)
    def _():
        _wait(copies(i - 1, 1 - slot))

    @pl.when(i == ntiles - 1)
    def _():
        _wait(copies(i, slot))
        stu[slot] = jnp.zeros(stu.shape[1:], stu.dtype)
        stg[slot] = jnp.zeros(stg.shape[1:], stg.dtype)
        total = runs_ref[i, RUN_TOTAL]
        tail = nts * tm - total
        nfull = tail // tm
        pieces = []
        for tile_ref, hbm, sem in ((stu.at[slot], su_hbm, sem_u.at[slot]),
                                   (stg.at[slot], sg_hbm, sem_g.at[slot])):
            for j in range(nts - ntiles):
                def full(j=j, tile_ref=tile_ref, hbm=hbm, sem=sem):
                    dst = pl.multiple_of(total + j * tm, RUN_ALIGN)
                    return pltpu.make_async_copy(tile_ref.at[pl.ds(0, tm), :], hbm.at[pl.ds(dst, tm), :], sem)
                pieces.append((j < nfull, full))
            rem = tail - nfull * tm
            for s in RUN_SIZES:
                if s >= tm:
                    continue
                def part(s=s, tile_ref=tile_ref, hbm=hbm, sem=sem):
                    dst = pl.multiple_of(total + nfull * tm + (rem & (-2 * s)), RUN_ALIGN)
                    return pltpu.make_async_copy(tile_ref.at[pl.ds(0, s), :], hbm.at[pl.ds(dst, s), :], sem)
                pieces.append(((rem & s) != 0, part))
        _start_then_wait(pieces)


def _scatter_call(runs, q3, u2, gate, tm):
    n = u2.shape[0]
    ns = _sorted_rows(n, tm)
    grid_spec = pltpu.PrefetchScalarGridSpec(
        num_scalar_prefetch=1,
        grid=(n // tm,),
        in_specs=[pl.BlockSpec((None, 1, tm), lambda i, *_: (i, 0, 0)),
                  pl.BlockSpec((tm, D_MODEL), lambda i, *_: (i, 0)),
                  pl.BlockSpec((tm, LANES), lambda i, *_: (i, 0))],
        out_specs=(pl.BlockSpec(memory_space=pl.ANY), pl.BlockSpec(memory_space=pl.ANY)),
        scratch_shapes=[pltpu.VMEM((2, _stage_rows(tm), D_MODEL), BF16),
                        pltpu.VMEM((2, _stage_rows(tm), LANES), F32),
                        pltpu.SemaphoreType.DMA((2,)), pltpu.SemaphoreType.DMA((2,))])
    return pl.pallas_call(
        functools.partial(_scatter_kernel, tm, ns // tm, n // tm),
        out_shape=(jax.ShapeDtypeStruct((ns, D_MODEL), BF16), jax.ShapeDtypeStruct((ns, LANES), F32)),
        grid_spec=grid_spec,
        compiler_params=_params(1),
        name="moe_scatter",
    )(runs, q3, u2, gate)


def _ungroup(tm, runs_ref, col_ref, ys_hbm, stage, sem):
    i = pl.program_id(0)
    slot = i % 2

    def copies(tile, slot):
        return _run_copies(runs_ref, tile, stage.at[slot], ys_hbm, sem.at[slot], False)

    @pl.when(i == 0)
    def _():
        stage[...] = jnp.zeros_like(stage)
        _start(copies(i, slot))

    @pl.when(i + 1 < pl.num_programs(0))
    def _():
        _start(copies(i + 1, 1 - slot))

    _wait(copies(i, slot))
    col = col_ref[...]
    gidx = col[:, 0:1]
    src = [runs_ref[i, RUN_SRC + g].astype(F32) for g in range(N_GROUPS)]
    first = src[N_GROUPS - 1]
    for g in range(N_GROUPS - 2, -1, -1):
        first = jnp.where(gidx == float(g), src[g], first)
    q = first + col[:, 1:2]
    lanes = lax.broadcasted_iota(jnp.int32, (tm, _stage_rows(tm)), 1).astype(F32)
    perm_t = jnp.where(q == lanes, 1.0, 0.0).astype(BF16)
    return _dot(perm_t, stage[slot])


def _moe_routed_kernel(items_ref, su_ref, sg_ref, wgu_ref, wd_ref, out_ref, acc_ref, wgu_b, wd_b):
    kk = pl.program_id(0)
    grp = items_ref[1, kk]

    @pl.when((kk == 0) | (grp != items_ref[1, jnp.maximum(kk - 1, 0)]))
    def _():
        for e in range(EXPERTS_PER_GROUP):
            wgu_b[e] = wgu_ref[e].astype(BF16)
            wd_b[e] = wd_ref[e].astype(BF16)

    @pl.when(items_ref[3, kk] == 1)
    def _():
        acc_ref[...] = jnp.zeros_like(acc_ref)

    @pl.when(items_ref[2, kk] == 1)
    def _():
        u = su_ref[...]
        gate = sg_ref[...]
        lane = lax.broadcasted_iota(jnp.int32, gate.shape, 1)
        first_lane = N_GROUPS + EXPERTS_PER_GROUP * grp
        for e in range(EXPERTS_PER_GROUP):
            gcol = jnp.sum(jnp.where(lane == first_lane + e, gate, 0.0), axis=-1, keepdims=True)
            hu = _dot(u, wgu_b[e])
            hg = hu[:, 0:D_EXPERT]
            act = hg * jax.nn.sigmoid(hg) * hu[:, D_EXPERT:2 * D_EXPERT] * gcol
            acc_ref[...] += _dot(act.astype(BF16), wd_b[e])

    @pl.when(items_ref[4, kk] == 1)
    def _():
        out_ref[...] = acc_ref[...].astype(out_ref.dtype)


def _moe_routed_call(items, su, sg, wgu, wd, layer, tm):
    ns = su.shape[0]
    n_items = ns // tm + N_GROUPS - 1
    depth = wgu.shape[0]
    wgu5 = wgu.reshape(depth, N_GROUPS, EXPERTS_PER_GROUP, D_MODEL, 2 * D_EXPERT)
    wd5 = wd.reshape(depth, N_GROUPS, EXPERTS_PER_GROUP, D_EXPERT, D_MODEL)
    grid_spec = pltpu.PrefetchScalarGridSpec(
        num_scalar_prefetch=1,
        grid=(n_items,),
        in_specs=[pl.BlockSpec((tm, D_MODEL), lambda k, it: (it[0, k], 0)),
                  pl.BlockSpec((tm, LANES), lambda k, it: (it[0, k], 0)),
                  pl.BlockSpec((None, None, EXPERTS_PER_GROUP, D_MODEL, 2 * D_EXPERT),
                               lambda k, it: (layer, it[1, k], 0, 0, 0)),
                  pl.BlockSpec((None, None, EXPERTS_PER_GROUP, D_EXPERT, D_MODEL),
                               lambda k, it: (layer, it[1, k], 0, 0, 0))],
        out_specs=pl.BlockSpec((tm, D_MODEL), lambda k, it: (it[0, k], 0)),
        scratch_shapes=[pltpu.VMEM((tm, D_MODEL), F32),
                        pltpu.VMEM((EXPERTS_PER_GROUP, D_MODEL, 2 * D_EXPERT), BF16),
                        pltpu.VMEM((EXPERTS_PER_GROUP, D_EXPERT, D_MODEL), BF16)])
    return pl.pallas_call(
        _moe_routed_kernel,
        out_shape=jax.ShapeDtypeStruct((ns, D_MODEL), BF16),
        grid_spec=grid_spec,
        compiler_params=_params(1),
        name="moe_routed",
    )(items, su, sg, wgu5, wd5)


def _resid_gather_kernel(tm, runs_ref, x_ref, g2_ref, col_ref, ys_hbm, y_ref, stage, sem):
    y_ref[...] = x_ref[...] + g2_ref[...] * _ungroup(tm, runs_ref, col_ref, ys_hbm, stage, sem)


def _resid_gather_call(x, runs, col, ys, mod, mspec, tm):
    n = x.shape[0]
    row = lambda i, *_: (i, 0)
    grid_spec = pltpu.PrefetchScalarGridSpec(
        num_scalar_prefetch=1,
        grid=(n // tm,),
        in_specs=[pl.BlockSpec((tm, D_MODEL), row), mspec(5), pl.BlockSpec((tm, LANES), row),
                  pl.BlockSpec(memory_space=pl.ANY)],
        out_specs=pl.BlockSpec((tm, D_MODEL), row),
        scratch_shapes=[pltpu.VMEM((2, _stage_rows(tm), D_MODEL), BF16), pltpu.SemaphoreType.DMA((2,))])
    return pl.pallas_call(
        functools.partial(_resid_gather_kernel, tm),
        out_shape=jax.ShapeDtypeStruct((n, D_MODEL), F32),
        grid_spec=grid_spec,
        compiler_params=_params(1),
        name="moe_unsort_resid",
    )(runs, x, mod, col, ys)


def _moe_kernel(expand, u_ref, gate_ref, wgu_ref, wd_ref, x1_ref, g2_ref, *rest):
    rep_ref = rest[0] if expand else None
    y_ref, acc_ref = rest[1 if expand else 0:]
    e = pl.program_id(1)

    @pl.when(e == 0)
    def _():
        acc_ref[...] = jnp.zeros_like(acc_ref)

    hu = _dot(u_ref[...], wgu_ref[...].astype(BF16))
    hg = hu[:, 0:D_EXPERT]
    hv = hu[:, D_EXPERT:2 * D_EXPERT]
    gate = gate_ref[...]
    lane = lax.broadcasted_iota(jnp.int32, gate.shape, 1)
    gcol = jnp.sum(jnp.where(lane == N_GROUPS + e, gate, 0.0), axis=-1, keepdims=True)
    act = hg * jax.nn.sigmoid(hg) * hv * gcol
    acc_ref[...] += _dot(act.astype(BF16), wd_ref[...].astype(BF16))

    @pl.when(e == N_EXPERTS - 1)
    def _():
        y_ref[...] = x1_ref[...] + _mod_rows(g2_ref, rep_ref) * acc_ref[...]


def _moe_call(u2, gate, wgu, wd, layer, x1, mod, mspec, rep, tm):
    n = u2.shape[0]
    row = lambda i, e: (i, 0)
    expand = rep is not None
    return pl.pallas_call(
        functools.partial(_moe_kernel, expand),
        out_shape=jax.ShapeDtypeStruct((n, D_MODEL), F32),
        grid=(n // tm, N_EXPERTS),
        in_specs=[pl.BlockSpec((tm, D_MODEL), row), pl.BlockSpec((tm, LANES), row),
                  pl.BlockSpec((None, None, D_MODEL, 2 * D_EXPERT), lambda i, e: (layer, e, 0, 0)),
                  pl.BlockSpec((None, None, D_EXPERT, D_MODEL), lambda i, e: (layer, e, 0, 0)),
                  pl.BlockSpec((tm, D_MODEL), row), mspec(5)]
        + ([pl.BlockSpec(rep.shape, lambda i, e: (0, 0))] if expand else []),
        out_specs=pl.BlockSpec((tm, D_MODEL), row),
        scratch_shapes=[pltpu.VMEM((tm, D_MODEL), F32)],
        compiler_params=_params(2),
        name="moe_dense",
    )(u2, gate, wgu, wd, x1, mod, *([rep] if expand else []))


def _rope128(x, cos, sa, sb):
    return x * cos + pltpu.roll(x, LANES - ROT_DIM // 2, axis=1) * sa \
        + pltpu.roll(x, ROT_DIM // 2, axis=1) * sb


def _kvq_kernel(gather_tm, *refs):
    if gather_tm:
        (runs_ref, x_ref, g2_ref, col_ref, ys_hbm, kvsh_ref, kvsc_ref, sh1_ref, sc1_ref, kvn_ref,
         nm_ref, wkv_ref, wq_ref, kn_ref, qn_ref, cos_ref, sa_ref, sb_ref, g64_ref, gr_ref, gb_ref,
         q_ref, k_ref, v_ref, xa_ref, stage, sem) = refs
        x = x_ref[...] + g2_ref[...] * _ungroup(gather_tm, runs_ref, col_ref, ys_hbm, stage, sem)
        xa_ref[...] = x
        rep_ref = None
    else:
        (x_ref, kvsh_ref, kvsc_ref, sh1_ref, sc1_ref, kvn_ref, nm_ref,
         wkv_ref, wq_ref, kn_ref, qn_ref, cos_ref, sa_ref, sb_ref, g64_ref, gr_ref, gb_ref,
         rep_ref, q_ref, k_ref, v_ref) = refs
        x = x_ref[...]
    xn = _unit_rms(x)
    cos, sa, sb = cos_ref[...], sa_ref[...], sb_ref[...]

    ukv = xn * kvn_ref[...] * (1.0 + _mod_rows(kvsc_ref, rep_ref)) + _mod_rows(kvsh_ref, rep_ref)
    kv = _dot(ukv.astype(BF16), wkv_ref[...])
    k = kv[:, 0:LANES]
    v_ref[...] = kv[:, LANES:2 * LANES]
    kh, kl = _split2(k * k)
    ms = _dot(kh, g64_ref[...]) + _dot(kl, g64_ref[...])
    k_ref[...] = _rope128(k * lax.rsqrt(ms + RMS_EPS) * kn_ref[...], cos, sa, sb)

    u1 = xn * nm_ref[...] * (1.0 + _mod_rows(sc1_ref, rep_ref)) + _mod_rows(sh1_ref, rep_ref)
    q = _dot(u1.astype(BF16), wq_ref[...])
    qh, ql = _split2(q * q)
    ms16 = _dot(qh, gr_ref[...]) + _dot(ql, gr_ref[...])
    r3 = _split3(lax.rsqrt(ms16 + RMS_EPS))
    rsb = _dot(r3[0], gb_ref[...]) + _dot(r3[1], gb_ref[...]) + _dot(r3[2], gb_ref[...])
    qn = q * rsb * qn_ref[...]
    for i in range(D_MODEL // LANES):
        sl = slice(LANES * i, LANES * (i + 1))
        q_ref[:, sl] = (_rope128(qn[:, sl], cos, sa, sb) * (HEAD_DIM ** -0.5)).astype(BF16)


def _kvq_call(x, routed, mod0, mod1, mspec, kvmod, kvspec, rep, kvn, nm, wkv, wq, kn, qn, tabs, tab,
              g64, gr, gb, tm):
    n = x.shape[0]
    row = lambda i, *_: (i, 0)
    const = lambda i, *_: (0, 0)
    vec = pl.BlockSpec((1, D_MODEL), const)
    tok = pl.BlockSpec((tm, D_MODEL), row)
    small = pl.BlockSpec((tm, LANES), row)
    ins = [x, kvmod, kvmod, mod1, mod1, kvn, nm, wkv, wq, kn, qn, *tabs, g64, gr, gb]
    specs = [tok, kvspec(0), kvspec(1), mspec(0), mspec(1), vec, vec,
             pl.BlockSpec((D_MODEL, 2 * LANES), const), pl.BlockSpec((D_MODEL, D_MODEL), const),
             pl.BlockSpec((1, LANES), const), vec, tab, tab, tab,
             pl.BlockSpec((LANES, LANES), const), pl.BlockSpec((D_MODEL, LANES), const),
             pl.BlockSpec((LANES, D_MODEL), const)]
    out_shape = [jax.ShapeDtypeStruct((n, D_MODEL), BF16), jax.ShapeDtypeStruct((n, LANES), F32),
                 jax.ShapeDtypeStruct((n, LANES), F32)]
    out_specs = [tok, small, small]
    scratch = []
    nprefetch = 0
    if routed is not None:
        runs, col, ys = routed
        nprefetch = 1
        ins = [runs, x, mod0, col, ys] + ins[1:]
        specs = [tok, mspec(5), small, pl.BlockSpec(memory_space=pl.ANY)] + specs[1:]
        out_shape.append(jax.ShapeDtypeStruct((n, D_MODEL), F32))
        out_specs.append(tok)
        scratch = [pltpu.VMEM((2, _stage_rows(tm), D_MODEL), BF16), pltpu.SemaphoreType.DMA((2,))]
    else:
        ins.append(rep)
        specs.append(pl.BlockSpec(rep.shape, const))
    grid_spec = pltpu.PrefetchScalarGridSpec(
        num_scalar_prefetch=nprefetch, grid=(n // tm,), in_specs=specs,
        out_specs=tuple(out_specs), scratch_shapes=scratch)
    return pl.pallas_call(
        functools.partial(_kvq_kernel, tm if routed is not None else 0),
        out_shape=tuple(out_shape),
        grid_spec=grid_spec,
        compiler_params=_params(1),
        name="kv_q_proj",
    )(*ins)


def _attn_core(q, kcat, vcat, bias, sinks_ref, o_ref):
    tk = kcat.shape[0]
    pairs = N_Q_HEADS // N_KV_HEADS // 2
    lane = lax.broadcasted_iota(jnp.int32, (1, LANES), 1)
    lo = lane < HEAD_DIM
    kro = pltpu.roll(kcat, HEAD_DIM, axis=1)
    vro = pltpu.roll(vcat, HEAD_DIM, axis=1)
    one_e = jnp.broadcast_to(jnp.where(lo, 1.0, 0.0), (tk, LANES))
    one_o = 1.0 - one_e
    for g in range(N_KV_HEADS):
        if g == 0:
            ke, ko = jnp.where(lo, kcat, 0.0), jnp.where(lo, 0.0, kro)
            ve, vo = jnp.where(lo, vcat, 0.0), jnp.where(lo, 0.0, vro)
        else:
            ke, ko = jnp.where(lo, kro, 0.0), jnp.where(lo, 0.0, kcat)
            ve, vo = jnp.where(lo, vro, 0.0), jnp.where(lo, 0.0, vcat)
        k2 = jnp.concatenate([ke, ko], 0).astype(BF16)
        v2 = jnp.concatenate([jnp.concatenate([ve, one_e], 1),
                              jnp.concatenate([vo, one_o], 1)], 0).astype(BF16)
        for p in range(pairs):
            hp = g * pairs + p
            s = _dot_nt(q[:, LANES * hp:LANES * (hp + 1)], k2)
            halves, corr = [], []
            for par in range(2):
                sp = s[:, par * tk:(par + 1) * tk] + bias
                sink = sinks_ref[2 * hp + par]
                m = jnp.maximum(jnp.max(sp, axis=-1, keepdims=True), sink)
                halves.append(jnp.exp(sp - m).astype(BF16))
                corr.append(jnp.exp(sink - m))
            o2 = _dot(jnp.concatenate(halves, 1), v2)
            den = o2[:, LANES:2 * LANES] + jnp.where(lo, corr[0], corr[1])
            o_ref[:, LANES * hp:LANES * (hp + 1)] = (o2[:, 0:LANES] / den).astype(o_ref.dtype)


def _attn_prompt_kernel(sinks_ref, q_ref, kp_ref, kc_ref, vp_ref, vc_ref, bias_ref, o_ref):
    kcat = jnp.concatenate([kp_ref[...], kc_ref[...]], 0)
    vcat = jnp.concatenate([vp_ref[...], vc_ref[...]], 0)
    _attn_core(q_ref[...], kcat, vcat, bias_ref[...], sinks_ref, o_ref)


ATTN_SEQS_PER_STEP = 4


def _attn_sample_kernel(sinks_ref, q_ref, kc_ref, kn_ref, vc_ref, vn_ref, bias_ref, o_ref, kbuf, vbuf):
    @pl.when(pl.program_id(0) == 0)
    def _():
        kbuf[...] = jnp.zeros_like(kbuf)
        vbuf[...] = jnp.zeros_like(vbuf)

    for b in range(ATTN_SEQS_PER_STEP):
        kbuf[b, 0:WINDOW, :] = kc_ref[b]
        kbuf[b, WINDOW:WINDOW + SAMPLE_PAD, :] = kn_ref[b]
        vbuf[b, 0:WINDOW, :] = vc_ref[b]
        vbuf[b, WINDOW:WINDOW + SAMPLE_PAD, :] = vn_ref[b]
    for b in range(ATTN_SEQS_PER_STEP):
        _attn_core(q_ref[b], kbuf[b], vbuf[b], bias_ref[...], sinks_ref, o_ref.at[b])


def _window_bias(tq, first):
    qi = jnp.arange(tq)[:, None]
    kj = jnp.arange(2 * WINDOW)[None, :]
    ok = (kj > qi) & (kj <= qi + WINDOW)
    if first:
        ok = ok & (kj >= WINDOW)
    return jnp.where(ok, 0.0, -jnp.inf).astype(F32)


def _attn_prompt_call(sinks, q, k, v, B, nb):
    n = q.shape[0]
    cur = lambda b, i: (b * nb + i, 0)
    prev = lambda b, i: (b * nb + jnp.maximum(i - 1, 0), 0)
    kvb = lambda im: pl.BlockSpec((WINDOW, LANES), im)
    bias = jnp.stack([_window_bias(WINDOW, True), _window_bias(WINDOW, False)])
    return pl.pallas_call(
        _attn_prompt_kernel,
        out_shape=jax.ShapeDtypeStruct((n, D_MODEL), BF16),
        grid=(B, nb),
        in_specs=[pl.BlockSpec(memory_space=pltpu.SMEM),
                  pl.BlockSpec((WINDOW, D_MODEL), cur), kvb(prev), kvb(cur), kvb(prev), kvb(cur),
                  pl.BlockSpec((None, WINDOW, 2 * WINDOW), lambda b, i: (jnp.minimum(i, 1), 0, 0))],
        out_specs=pl.BlockSpec((WINDOW, D_MODEL), cur),
        compiler_params=_params(2),
        name="swa_prompt",
    )(sinks, q, k, k, v, v, bias)


def _attn_sample_call(sinks, q, kcache, knew, vcache, vnew):
    B = q.shape[0]
    nseq = ATTN_SEQS_PER_STEP
    b3 = lambda b: (b, 0, 0)
    cache = pl.BlockSpec((nseq, WINDOW, LANES), b3)
    new = pl.BlockSpec((nseq, SAMPLE_PAD, LANES), b3)
    return pl.pallas_call(
        _attn_sample_kernel,
        out_shape=jax.ShapeDtypeStruct(q.shape, BF16),
        grid=(B // nseq,),
        in_specs=[pl.BlockSpec(memory_space=pltpu.SMEM),
                  pl.BlockSpec((nseq, SAMPLE_PAD, D_MODEL), b3), cache, new, cache, new,
                  pl.BlockSpec((SAMPLE_PAD, 2 * WINDOW), lambda b: (0, 0))],
        out_specs=pl.BlockSpec((nseq, SAMPLE_PAD, D_MODEL), b3),
        scratch_shapes=[pltpu.VMEM((nseq, 2 * WINDOW, LANES), F32),
                        pltpu.VMEM((nseq, 2 * WINDOW, LANES), F32)],
        compiler_params=_params(1),
        name="swa_sample",
    )(sinks, q, kcache, knew, vcache, vnew, _window_bias(SAMPLE_PAD, False))


def _rope_tables(pos):
    half = ROT_DIM // 2
    inv = ROPE_THETA ** (-jnp.arange(half, dtype=F32) / half)
    ang = pos.astype(F32)[:, None] * inv[None]
    cos, sin = jnp.cos(ang), jnp.sin(ang)
    d = jnp.arange(LANES) % HEAD_DIM
    idx = d % half
    cos_t = jnp.where(d < ROT_DIM, cos[:, idx], 1.0)
    sa = jnp.where(d < half, -sin[:, idx], 0.0)
    sb = jnp.where((d >= half) & (d < ROT_DIM), sin[:, idx], 0.0)
    return cos_t, sa, sb


def _pad_lanes(a, value=0.0):
    return jnp.pad(a, ((0, 0), (0, LANES - a.shape[1])), constant_values=value)


def _prep_weights(ada_w, ada_b, norm_mix, norm_ffn, a_w_in, a_b_gates, a_head_norm, a_w_out,
                  kv_ada_w, kv_ada_b, kv_norm, w_k, w_v, k_norm, b_w_q, b_q_norm, b_sinks, b_w_o,
                  moe_w_group, moe_b_group, moe_w_expert, moe_b_expert, moe_w_gate_up, moe_w_down):
    w = {}
    g0 = QK_COLS + 2 * V_COLS
    w["w_in"] = a_w_in
    w["w_gates"] = jnp.concatenate([_pad_lanes(a_w_in[0, :, g0:g0 + M_HEADS]),
                                    _pad_lanes(a_w_in[0, :, g0 + M_HEADS:])], 1)
    w["bli"] = _pad_lanes(a_b_gates[0][None, :M_HEADS])
    w["blf"] = _pad_lanes(a_b_gates[0][None, M_HEADS:])
    w["head_norm"] = a_head_norm[0][None]
    w["w_out"] = a_w_out[0].astype(BF16)
    w["norm_mix"] = [norm_mix[l][None] for l in range(2)]
    w["norm_ffn"] = [norm_ffn[l][None] for l in range(2)]
    w["router"] = []
    for l in range(2):
        wr = _pad_lanes(jnp.concatenate([moe_w_group[l], moe_w_expert[l]], 1))
        hi = wr.astype(BF16)
        lo = (wr - hi.astype(F32)).astype(BF16)
        br = _pad_lanes(jnp.concatenate([moe_b_group[l], moe_b_expert[l]])[None])
        w["router"].append((hi, lo, br))
    w["w_gu"] = moe_w_gate_up
    w["w_d"] = moe_w_down
    w["kv_norm"] = kv_norm[None]
    w["w_kv"] = jnp.concatenate([w_k, w_v], 1).astype(BF16)
    w["k_norm"] = jnp.tile(k_norm, N_KV_HEADS)[None]
    w["w_q"] = b_w_q[0].astype(BF16)
    w["q_norm"] = jnp.tile(b_q_norm[0], N_Q_HEADS)[None]
    w["sinks"] = b_sinks[0]
    w["w_o"] = b_w_o[0].astype(BF16)
    lanes = jnp.arange(LANES)
    feat = jnp.arange(D_MODEL)
    w["g64"] = jnp.where((lanes[:, None] // HEAD_DIM) == (lanes[None, :] // HEAD_DIM),
                         1.0 / HEAD_DIM, 0.0).astype(BF16)
    w["gr"] = jnp.where((feat[:, None] // HEAD_DIM) == lanes[None, :], 1.0 / HEAD_DIM, 0.0).astype(BF16)
    w["gb"] = jnp.where(lanes[:, None] == (feat[None, :] // HEAD_DIM), 1.0, 0.0).astype(BF16)
    return w


def _trunk(x2, mods, kvmod, mspec, mspec_moe, rep, w, *, B, T, L, tm, tm_moe, tabs, tabspec, c0, n0, m0,
           cache_k=None, cache_v=None):
    sample = cache_k is not None
    nc = T // L if not sample else 1

    qk, v, o, li, lf = _inproj_call(x2, mods[0], mspec, rep, w["norm_mix"][0], w["w_in"],
                                    w["w_gates"], w["bli"], w["blf"], tm)
    if sample:
        def padtok(a, value=0.0):
            a = a.reshape(B, T, a.shape[-1])
            a = jnp.pad(a, ((0, 0), (0, L - T), (0, 0)), constant_values=value)
            return a.reshape(B * L, a.shape[-1])
        qk, v, li, lf = padtok(qk), padtok(v), padtok(li, M_EMPTY), padtok(lf)
    h, c_new, n_new, m_new = _mlstm_call(qk, v, li, lf, c0, n0, m0, B, nc, L)
    if sample:
        h = h.reshape(B, L, V_COLS)[:, :T].reshape(B * T, V_COLS)
    routed = not sample
    ntiles = (B * T) // tm

    def routed_moe(u2, gate, gid, col, layer):
        q, runs, items = _sort_meta_call(gid.reshape(ntiles, tm), tm)
        su, sg = _scatter_call(runs, q.reshape(ntiles, 1, tm), u2, gate, tm)
        ys = _moe_routed_call(items, su, sg, w["w_gu"], w["w_d"], layer, tm)
        return runs, col, ys

    rh, rl, br = w["router"][0]
    post0 = _post_call(True, routed, h, o, x2, mods[0], mspec, rep, w["norm_ffn"][0], w["head_norm"],
                       w["w_out"], rh, rl, br, tm)
    kvq_args = (mods[0], mods[1], mspec, kvmod, mspec, rep, w["kv_norm"], w["norm_mix"][1],
                w["w_kv"], w["w_q"], w["k_norm"], w["q_norm"], tabs, tabspec,
                w["g64"], w["gr"], w["gb"], tm)

    if routed:
        x1, u2, gate, gid, col = post0
        q, k, vv, xa = _kvq_call(x1, routed_moe(u2, gate, gid, col, 0), *kvq_args)
    else:
        x1, u2, gate = post0
        xa = _moe_call(u2, gate, w["w_gu"], w["w_d"], 0, x1, mods[0], mspec_moe, rep, tm_moe)
        q, k, vv = _kvq_call(xa, None, *kvq_args)
    if not sample:
        att = _attn_prompt_call(w["sinks"], q, k, vv, B, T // WINDOW)
        k_win = k.reshape(B, T, LANES)[:, T - WINDOW:].reshape(B, WINDOW, N_KV_HEADS, HEAD_DIM)
        v_win = vv.reshape(B, T, LANES)[:, T - WINDOW:].reshape(B, WINDOW, N_KV_HEADS, HEAD_DIM)
    else:
        def padseq(a):
            return jnp.pad(a.reshape(B, T, a.shape[-1]), ((0, 0), (0, SAMPLE_PAD - T), (0, 0)))
        kc = cache_k.reshape(B, WINDOW, LANES)
        vc = cache_v.reshape(B, WINDOW, LANES)
        att = _attn_sample_call(w["sinks"], padseq(q), kc, padseq(k), vc, padseq(vv))
        att = att[:, :T].reshape(B * T, D_MODEL)
        k_win = jnp.concatenate([kc[:, T:], k.reshape(B, T, LANES)], 1)
        v_win = jnp.concatenate([vc[:, T:], vv.reshape(B, T, LANES)], 1)
        k_win = k_win.reshape(B, WINDOW, N_KV_HEADS, HEAD_DIM)
        v_win = v_win.reshape(B, WINDOW, N_KV_HEADS, HEAD_DIM)
    rh, rl, br = w["router"][1]
    post1 = _post_call(False, routed, att, None, xa, mods[1], mspec, rep, w["norm_ffn"][1], None,
                       w["w_o"], rh, rl, br, tm)
    if routed:
        x3, u4, gate, gid, col = post1
        runs, col, ys = routed_moe(u4, gate, gid, col, 1)
        y = _resid_gather_call(x3, runs, col, ys, mods[1], mspec, tm)
    else:
        x3, u4, gate = post1
        y = _moe_call(u4, gate, w["w_gu"], w["w_d"], 1, x3, mods[1], mspec_moe, rep, tm_moe)
    c_out = c_new[None]
    n_out = n_new.reshape(1, B, M_HEADS, M_DK)
    m_out = m_new[:, 0, :M_HEADS][None]
    return y, c_out, n_out, m_out, k_win, v_win


def kernel(x_prompt, x_sample, c_prompt, c_sample, state_c, state_n, state_m, cache_k_win, cache_v_win, ada_w, ada_b, norm_mix, norm_ffn, a_w_in, a_b_gates, a_head_norm, a_w_out, kv_ada_w, kv_ada_b, kv_norm, w_k, w_v, k_norm, b_w_q, b_q_norm, b_sinks, b_w_o, moe_w_group, moe_b_group, moe_w_expert, moe_b_expert, moe_w_gate_up, moe_w_down):
    Bp, Tp, D = x_prompt.shape
    Bs, Ts, _ = x_sample.shape
    w = _prep_weights(ada_w, ada_b, norm_mix, norm_ffn, a_w_in, a_b_gates, a_head_norm, a_w_out,
                      kv_ada_w, kv_ada_b, kv_norm, w_k, w_v, k_norm, b_w_q, b_q_norm, b_sinks, b_w_o,
                      moe_w_group, moe_b_group, moe_w_expert, moe_b_expert, moe_w_gate_up, moe_w_down)

    rows = Bp + Bs
    rpad = -rows % 8
    c_all = jnp.concatenate([c_prompt, c_sample, jnp.zeros((rpad, D), F32)], 0)
    mod = _ada_call(c_all, ada_w, ada_b[:, None, :])
    kvm = _ada_call(c_all, kv_ada_w[None], kv_ada_b[None, None, :])

    tm_p = 512
    tiles_per_seq = Tp // tm_p
    mods_p = [mod[l, :Bp][:, None, :] for l in range(2)]
    kvmod_p = kvm[0, :Bp][:, None, :]

    def mspec_p(col):
        return pl.BlockSpec((None, 1, D_MODEL), lambda i, *_: (i // tiles_per_seq, 0, col))

    tm_moe_p = 1024
    moe_tiles_per_seq = Tp // tm_moe_p

    def mspec_moe_p(col):
        return pl.BlockSpec((None, 1, D_MODEL), lambda i, *_: (i // moe_tiles_per_seq, 0, col))

    tabs_p = _rope_tables(jnp.arange(Tp, dtype=jnp.int32))
    tabspec_p = pl.BlockSpec((tm_p, LANES), lambda i, *_: (i % tiles_per_seq, 0))
    npairs = M_HEADS // 2
    c0 = jnp.zeros((Bp, M_HEADS, M_DK, M_DV), F32)
    n0 = jnp.zeros((Bp, npairs, LANES), F32)
    m0 = jnp.pad(jnp.full((Bp, 1, M_HEADS), M_EMPTY, F32), ((0, 0), (0, 0), (0, LANES - M_HEADS)))
    yp, cp, np_, mp, kwp, vwp = _trunk(
        x_prompt.reshape(Bp * Tp, D), mods_p, kvmod_p, mspec_p, mspec_moe_p, None, w,
        B=Bp, T=Tp, L=M_CHUNK, tm=tm_p, tm_moe=tm_moe_p, tabs=tabs_p, tabspec=tabspec_p,
        c0=c0, n0=n0, m0=m0)

    ns = Bs * Ts
    mods_s = [mod[l, Bp:Bp + Bs][None] for l in range(2)]
    kvmod_s = kvm[0, Bp:Bp + Bs][None]
    rep = (jnp.arange(ns)[:, None] // Ts == jnp.arange(Bs)[None, :]).astype(BF16)

    def mspec_s(col):
        return pl.BlockSpec((None, Bs, D_MODEL), lambda i, *_: (0, 0, col))

    tabs_s = _rope_tables(PAST_LEN + jnp.arange(Ts, dtype=jnp.int32))
    tabs_s = tuple(jnp.tile(t, (Bs, 1)) for t in tabs_s)
    m0s = jnp.pad(state_m[0][:, None, :], ((0, 0), (0, 0), (0, LANES - M_HEADS)))
    ys, cs, ns_, ms, kws, vws = _trunk(
        x_sample.reshape(ns, D), mods_s, kvmod_s, mspec_s, mspec_s, rep, w,
        B=Bs, T=Ts, L=SAMPLE_PAD, tm=ns, tm_moe=ns, tabs=tabs_s,
        tabspec=pl.BlockSpec((ns, LANES), lambda i, *_: (0, 0)),
        c0=state_c[0], n0=state_n[0].reshape(Bs, npairs, LANES), m0=m0s,
        cache_k=cache_k_win, cache_v=cache_v_win)

    return (yp.reshape(Bp, Tp, D), ys.reshape(Bs, Ts, D), cp, np_, mp, kwp, vwp,
            cs, ns_, ms, kws, vws)
```

```python
import functools

import jax
import jax.numpy as jnp
from jax import lax
from jax.experimental import pallas as pl
from jax.experimental.pallas import tpu as pltpu

F32 = jnp.float32
BF16 = jnp.bfloat16

D_MODEL = 1024
PAST_LEN = 8192
M_HEADS = 8
M_DK = 64
M_DV = 128
M_CHUNK = 64
GATE_SOFTCAP = 15.0
M_EMPTY = -1e30
WINDOW = 128
HEAD_DIM = 64
N_Q_HEADS = 16
N_KV_HEADS = 2
ROPE_THETA = 500000.0
ROT_DIM = 16
N_GROUPS = 4
EXPERTS_PER_GROUP = 4
N_EXPERTS = 16
D_EXPERT = 256
RMS_EPS = 1e-6

LANES = 128
QK_COLS = 2 * M_HEADS * M_DK
V_COLS = M_HEADS * M_DV
IN_COLS = QK_COLS + 2 * V_COLS + 2 * LANES
SAMPLE_PAD = 16
VMEM_LIMIT = 52 * 1024 * 1024

NT_DIMS = (((1,), (1,)), ((), ()))


def _params(n_axes):
    return pltpu.CompilerParams(dimension_semantics=("arbitrary",) * n_axes,
                                vmem_limit_bytes=VMEM_LIMIT)


def _dot(a, b):
    return jnp.dot(a, b, preferred_element_type=F32)


def _dot_nt(a, b):
    return lax.dot_general(a, b, NT_DIMS, preferred_element_type=F32)


def _split2(x):
    hi = x.astype(BF16)
    lo = (x - hi.astype(F32)).astype(BF16)
    return hi, lo


def _split3(x):
    hi = x.astype(BF16)
    r = x - hi.astype(F32)
    mid = r.astype(BF16)
    lo = (r - mid.astype(F32)).astype(BF16)
    return hi, mid, lo


def _unit_rms(x):
    return x * lax.rsqrt(jnp.mean(x * x, axis=-1, keepdims=True) + RMS_EPS)


def _ada_kernel(c_ref, w_ref, b_ref, o_ref):
    c = c_ref[...]
    cs = (c * jax.nn.sigmoid(c)).astype(BF16)
    o_ref[...] = _dot(cs, w_ref[...].astype(BF16)) + b_ref[...]


def _ada_call(c, w, b):
    g, d, n = w.shape
    r = c.shape[0]
    tn = 1024
    return pl.pallas_call(
        _ada_kernel,
        out_shape=jax.ShapeDtypeStruct((g, r, n), F32),
        grid=(g, n // tn),
        in_specs=[pl.BlockSpec((r, d), lambda i, j: (0, 0)),
                  pl.BlockSpec((None, d, tn), lambda i, j: (i, 0, j)),
                  pl.BlockSpec((None, 1, tn), lambda i, j: (i, 0, j))],
        out_specs=pl.BlockSpec((None, r, tn), lambda i, j: (i, 0, j)),
        compiler_params=_params(2),
        name="ada_mod",
    )(c, w, b)


def _mod_rows(ref, rep_ref):
    if rep_ref is None:
        return ref[...]
    hi, mid, lo = _split3(ref[...])
    rep = rep_ref[...]
    return _dot(rep, hi) + _dot(rep, mid) + _dot(rep, lo)


def _inproj_kernel(expand, x_ref, sh_ref, sc_ref, nw_ref, w_ref, wg_ref, bli_ref, blf_ref, *rest):
    rep_ref = rest[0] if expand else None
    qk_ref, v_ref, o_ref, li_ref, lf_ref, wb_ref = rest[1 if expand else 0:]

    @pl.when(pl.program_id(0) == 0)
    def _():
        wb_ref[...] = w_ref[...].astype(BF16)

    u = _unit_rms(x_ref[...]) * nw_ref[...] * (1.0 + _mod_rows(sc_ref, rep_ref)) \
        + _mod_rows(sh_ref, rep_ref)
    ub = u.astype(BF16)
    half = QK_COLS // 2
    q = _dot(ub, wb_ref[:, 0:half]) * (M_DK ** -0.5)
    qk_ref[:, 0:half] = q.astype(BF16)
    qk_ref[:, half:QK_COLS] = _dot(ub, wb_ref[:, half:QK_COLS]).astype(BF16)
    v_ref[...] = _dot(ub, wb_ref[:, QK_COLS:QK_COLS + V_COLS]).astype(BF16)
    o_ref[...] = _dot(ub, wb_ref[:, QK_COLS + V_COLS:QK_COLS + 2 * V_COLS]).astype(BF16)
    lane = lax.broadcasted_iota(jnp.int32, (1, LANES), 1)
    live = lane < M_HEADS
    wg = wg_ref[...].astype(BF16)
    gi = _dot(ub, wg[:, 0:LANES]) + bli_ref[...]
    gf = _dot(ub, wg[:, LANES:2 * LANES]) + blf_ref[...]
    li = GATE_SOFTCAP * jnp.tanh(gi / GATE_SOFTCAP)
    fpre = GATE_SOFTCAP * jnp.tanh(gf / GATE_SOFTCAP)
    lf = jnp.minimum(fpre, 0.0) - jnp.log1p(jnp.exp(-jnp.abs(fpre)))
    li_ref[...] = jnp.where(live, li, 0.0)
    lf_ref[...] = jnp.where(live, lf, 0.0)


def _inproj_call(x, mod, mspec, rep, nw, w_in, wg, bli, blf, tm):
    n = x.shape[0]
    row = lambda i: (i, 0)
    const = lambda i: (0, 0)
    main_cols = QK_COLS + 2 * V_COLS
    expand = rep is not None
    return pl.pallas_call(
        functools.partial(_inproj_kernel, expand),
        out_shape=(jax.ShapeDtypeStruct((n, QK_COLS), BF16),
                   jax.ShapeDtypeStruct((n, V_COLS), BF16),
                   jax.ShapeDtypeStruct((n, V_COLS), BF16),
                   jax.ShapeDtypeStruct((n, LANES), F32),
                   jax.ShapeDtypeStruct((n, LANES), F32)),
        grid=(n // tm,),
        in_specs=[pl.BlockSpec((tm, D_MODEL), row), mspec(0), mspec(1),
                  pl.BlockSpec((1, D_MODEL), const),
                  pl.BlockSpec((None, D_MODEL, main_cols), lambda i: (0, 0, 0),
                               pipeline_mode=pl.Buffered(1)),
                  pl.BlockSpec((D_MODEL, 2 * LANES), const),
                  pl.BlockSpec((1, LANES), const), pl.BlockSpec((1, LANES), const)]
        + ([pl.BlockSpec(rep.shape, const)] if expand else []),
        out_specs=(pl.BlockSpec((tm, QK_COLS), row), pl.BlockSpec((tm, V_COLS), row),
                   pl.BlockSpec((tm, V_COLS), row), pl.BlockSpec((tm, LANES), row),
                   pl.BlockSpec((tm, LANES), row)),
        scratch_shapes=[pltpu.VMEM((D_MODEL, main_cols), BF16)],
        compiler_params=_params(1),
        name="mlstm_inproj",
    )(x, mod, mod, nw, w_in, wg, bli, blf, *([rep] if expand else []))


def _mlstm_kernel(L, nseq, scale_q, *refs):
    c = pl.program_id(1)
    seqs = [[r.at[b] for r in refs] for b in range(nseq)]

    @pl.when(c == 0)
    def _():
        for s in seqs:
            _mlstm_load_state(*s[4:7], *s[11:14])

    for s in seqs:
        _mlstm_seq_step(L, scale_q, *s[0:4], s[7], *s[11:14])

    @pl.when(c == pl.num_programs(1) - 1)
    def _():
        for s in seqs:
            _mlstm_store_state(*s[8:14])


def _mlstm_load_state(c0_ref, n0_ref, m0_ref, c2_s, n_s, m_s):
    m_s[...] = m0_ref[...]
    n_s[...] = n0_ref[...]
    z = jnp.zeros((M_DK, M_DV), F32)
    for j in range(M_HEADS // 2):
        c2_s[j] = jnp.concatenate([jnp.concatenate([c0_ref[2 * j], z], 1),
                                   jnp.concatenate([z, c0_ref[2 * j + 1]], 1)], 0)


def _mlstm_seq_step(L, scale_q, qk_ref, v_ref, li_ref, lf_ref, h_ref, c2_s, n_s, m_s):
    npairs = M_HEADS // 2
    hd = M_DK

    lane = lax.broadcasted_iota(jnp.int32, (1, LANES), 1)
    lo128 = lane < hd
    lane256 = lax.broadcasted_iota(jnp.int32, (1, 2 * LANES), 1)
    lo256 = lane256 < LANES
    row128 = lax.broadcasted_iota(jnp.int32, (LANES, 1), 0)
    top = row128 < hd
    blockdiag = (top & lo256) | (jnp.logical_not(top) & jnp.logical_not(lo256))

    LI = li_ref[...]
    LF = lf_ref[...]
    rowL = lax.broadcasted_iota(jnp.int32, (L, LANES), 0)

    def prefix(x, op, ident):
        d = 1
        while d < L:
            shifted = pltpu.roll(x, d, axis=0)
            x = op(x, jnp.where(rowL >= d, shifted, ident))
            d *= 2
        return x

    Bc = prefix(LF, jnp.add, 0.0)
    Cm = LI - Bc
    mprev = m_s[...]
    Gc = jnp.maximum(mprev, prefix(Cm, jnp.maximum, -jnp.inf))
    A = jnp.exp(mprev - Gc)
    bL = Bc[L - 1:L, :]
    DL = bL + Cm
    mnew = jnp.maximum(bL + mprev, jnp.max(DL, axis=0, keepdims=True))
    ast = jnp.exp(bL + mprev - mnew)
    WST = jnp.exp(DL - mnew)

    def pad_rows(x, rows):
        if x.shape[0] == rows:
            return x
        return jnp.concatenate([x, jnp.zeros((rows - x.shape[0], x.shape[1]), x.dtype)], 0)

    cm_pad = pad_rows(Cm, hd)
    XT = jnp.concatenate([cm_pad, cm_pad], 0).T
    causal = (lane & (hd - 1)) <= rowL
    J = jnp.where(blockdiag, 1.0, 0.0).astype(BF16)

    EXPM = jnp.exp(-(Bc + Gc))

    def bc(X, h):
        return jnp.broadcast_to(X[:, h:h + 1], X.shape)

    for j in range(npairs):
        he, ho = 2 * j, 2 * j + 1
        p128 = slice(LANES * j, LANES * (j + 1))
        p256 = slice(2 * LANES * j, 2 * LANES * (j + 1))

        def pair128(X):
            return jnp.where(lo128, bc(X, he), bc(X, ho))

        def pair256(X):
            return jnp.concatenate([bc(X, he), bc(X, ho)], 1)

        q128 = qk_ref[:, LANES * j:LANES * (j + 1)]
        k128 = qk_ref[:, QK_COLS // 2 + LANES * j:QK_COLS // 2 + LANES * (j + 1)]
        v256 = v_ref[:, 2 * LANES * j:2 * LANES * (j + 1)]
        zk = jnp.zeros_like(k128)
        zv = jnp.zeros_like(v256)
        K2t = jnp.concatenate([pad_rows(jnp.where(lo128, k128, zk), hd),
                               pad_rows(jnp.where(lo128, zk, k128), hd)], 0)
        V2 = jnp.concatenate([pad_rows(jnp.where(lo256, v256, zv), hd),
                              pad_rows(jnp.where(lo256, zv, v256), hd)], 0)
        S = _dot_nt(q128, K2t)
        crow = jnp.where(lo128, XT[he:he + 1, :], XT[ho:ho + 1, :])
        arg = jnp.where(causal, crow - pair128(Gc), -jnp.inf)
        Sw = (S * jnp.exp(arg)).astype(BF16)
        num_intra = _dot(Sw, V2)
        rowsum = _dot(Sw, J)
        C2 = c2_s[j]
        npair = n_s[j:j + 1, :]
        Nrow2 = jnp.concatenate(
            [jnp.broadcast_to(jnp.where(lo128, npair, 0.0), (LANES, LANES)),
             jnp.broadcast_to(jnp.where(lo128, 0.0, npair), (LANES, LANES))], 0).astype(BF16)
        if scale_q:
            qa = (q128.astype(F32) * pair128(A)).astype(BF16)
            num = _dot(qa, C2.astype(BF16)) + num_intra
            den = _dot_nt(qa, Nrow2) + rowsum
        else:
            a256 = pair256(A)
            num = a256 * _dot(q128, C2.astype(BF16)) + num_intra
            den = a256 * _dot_nt(q128, Nrow2) + rowsum
        h = num / jnp.maximum(jnp.abs(den), pair256(EXPM))
        h_ref[:, p256] = h.astype(h_ref.dtype)

        kw = k128.astype(F32) * pair128(WST)
        n_s[j:j + 1, :] = pair128(ast) * npair + jnp.sum(kw, axis=0, keepdims=True)
        kwT = pad_rows(kw, LANES).T.astype(BF16)
        dC = _dot(kwT, pad_rows(v256, LANES))
        c2_s[j] = pair256(ast) * C2 + jnp.where(blockdiag, dC, 0.0)

    m_s[...] = mnew


def _mlstm_store_state(cout_ref, nout_ref, mout_ref, c2_s, n_s, m_s):
    for j in range(M_HEADS // 2):
        C2 = c2_s[j]
        cout_ref[2 * j] = C2[0:M_DK, 0:M_DV]
        cout_ref[2 * j + 1] = C2[M_DK:2 * M_DK, M_DV:2 * M_DV]
    nout_ref[...] = n_s[...]
    mout_ref[...] = m_s[...]


MLSTM_SEQS_PER_STEP = 4


def _mlstm_call(qk, v, li, lf, c0, n0, m0, B, nc, L):
    npairs = M_HEADS // 2
    nseq = min(B, MLSTM_SEQS_PER_STEP)
    T = nc * L
    tok3 = lambda a: a.reshape(B, T, a.shape[-1])
    tok = lambda b, c: (b, c, 0)
    st4 = lambda b, c: (b, 0, 0, 0)
    st3 = lambda b, c: (b, 0, 0)
    h, c_new, n_new, m_new = pl.pallas_call(
        functools.partial(_mlstm_kernel, L, nseq, nc == 1),
        out_shape=(jax.ShapeDtypeStruct((B, T, V_COLS), BF16),
                   jax.ShapeDtypeStruct((B, M_HEADS, M_DK, M_DV), F32),
                   jax.ShapeDtypeStruct((B, npairs, LANES), F32),
                   jax.ShapeDtypeStruct((B, 1, LANES), F32)),
        grid=(B // nseq, nc),
        in_specs=[pl.BlockSpec((nseq, L, QK_COLS), tok), pl.BlockSpec((nseq, L, V_COLS), tok),
                  pl.BlockSpec((nseq, L, LANES), tok), pl.BlockSpec((nseq, L, LANES), tok),
                  pl.BlockSpec((nseq, M_HEADS, M_DK, M_DV), st4),
                  pl.BlockSpec((nseq, npairs, LANES), st3),
                  pl.BlockSpec((nseq, 1, LANES), st3)],
        out_specs=(pl.BlockSpec((nseq, L, V_COLS), tok),
                   pl.BlockSpec((nseq, M_HEADS, M_DK, M_DV), st4),
                   pl.BlockSpec((nseq, npairs, LANES), st3),
                   pl.BlockSpec((nseq, 1, LANES), st3)),
        scratch_shapes=[pltpu.VMEM((nseq, npairs, 2 * M_DK, 2 * M_DV), F32),
                        pltpu.VMEM((nseq, npairs, LANES), F32),
                        pltpu.VMEM((nseq, 1, LANES), F32)],
        compiler_params=_params(2),
        name="mlstm_chunks",
    )(tok3(qk), tok3(v), tok3(li), tok3(lf), c0, n0, m0)
    return h.reshape(B * T, V_COLS), c_new, n_new, m_new


def _route(lg):
    lane = lax.broadcasted_iota(jnp.int32, lg.shape, 1)
    lanef = lane.astype(F32)
    neg = -jnp.inf
    far = float(LANES)
    gm = lane < N_GROUPS
    lgm = jnp.where(gm, lg, neg)
    gmax = jnp.max(lgm, axis=-1, keepdims=True)
    gsum = jnp.sum(jnp.exp(lgm - gmax), axis=-1, keepdims=True)
    g_w = 1.0 / gsum
    gidx = jnp.min(jnp.where(gm & (lg == gmax), lanef, far), axis=-1, keepdims=True)
    first = N_GROUPS + EXPERTS_PER_GROUP * gidx
    sel = (lanef >= first) & (lanef < first + EXPERTS_PER_GROUP)
    l1 = jnp.max(jnp.where(sel, lg, neg), axis=-1, keepdims=True)
    i1 = jnp.min(jnp.where(sel & (lg == l1), lanef, far), axis=-1, keepdims=True)
    sel2 = sel & (lanef != i1)
    l2 = jnp.max(jnp.where(sel2, lg, neg), axis=-1, keepdims=True)
    i2 = jnp.min(jnp.where(sel2 & (lg == l2), lanef, far), axis=-1, keepdims=True)
    r = jnp.exp(l2 - l1)
    w1 = g_w / (1.0 + r)
    w2 = w1 * r
    return jnp.where(lanef == i1, w1, jnp.where(lanef == i2, w2, 0.0)), gidx


def _post_kernel(mlstm, routed, expand, *refs):
    refs = list(refs)
    h_ref = refs.pop(0)
    o_ref = refs.pop(0) if mlstm else None
    x_ref, g1_ref, sh2_ref, sc2_ref, nf_ref = refs[:5]
    refs = refs[5:]
    hn_ref = refs.pop(0) if mlstm else None
    wout_ref, wrh_ref, wrl_ref, br_ref = refs[:4]
    refs = refs[4:]
    tri_ref = refs.pop(0) if routed else None
    rep_ref = refs.pop(0) if expand else None
    x1_ref, u2_ref, gate_ref = refs[:3]
    outs = refs[3:]
    if mlstm:
        hf = h_ref[...].astype(F32)
        parts = [_unit_rms(hf[:, M_DV * i:M_DV * (i + 1)]) for i in range(M_HEADS)]
        hn = jnp.concatenate(parts, 1) * hn_ref[...]
        hg = (hn * jax.nn.sigmoid(o_ref[...].astype(F32))).astype(BF16)
    else:
        hg = h_ref[...]
    x1 = x_ref[...] + _mod_rows(g1_ref, rep_ref) * _dot(hg, wout_ref[...])
    x1_ref[...] = x1
    u2 = _unit_rms(x1) * nf_ref[...] * (1.0 + _mod_rows(sc2_ref, rep_ref)) + _mod_rows(sh2_ref, rep_ref)
    uh, ul = _split2(u2)
    lg = _dot(uh, wrh_ref[...]) + _dot(ul, wrh_ref[...]) + _dot(uh, wrl_ref[...]) + br_ref[...]
    gate, gidx = _route(lg)
    u2_ref[...] = uh
    gate_ref[...] = gate
    if not routed:
        return
    gid_ref, col_ref = outs
    rows = []
    for blk in range(x1.shape[0] // LANES):
        col = jnp.broadcast_to(gidx[LANES * blk:LANES * (blk + 1), :], (LANES, LANES))
        rows.append(col.T[0:1, :])
    gid_ref[...] = jnp.concatenate(rows, 1)
    lanef = lax.broadcasted_iota(jnp.int32, gate.shape, 1).astype(F32)
    onehot = jnp.where(lanef == gidx, 1.0, 0.0)
    earlier = _dot(tri_ref[...], onehot.astype(BF16))
    rank = jnp.sum(onehot * earlier, axis=-1, keepdims=True)
    col_ref[...] = jnp.where(lanef == 0.0, gidx, jnp.where(lanef == 1.0, rank, 0.0))


def _post_call(mlstm, routed, h, o, x, mod, mspec, rep, nf, hn, wout, wrh, wrl, br, tm):
    n = x.shape[0]
    expand = rep is not None
    row = lambda i: (i, 0)
    const = lambda i: (0, 0)
    tok = pl.BlockSpec((tm, D_MODEL), row)
    small = pl.BlockSpec((tm, LANES), row)
    vec = pl.BlockSpec((1, D_MODEL), const)
    ins = [h] + ([o] if mlstm else []) + [x, mod, mod, mod, nf] + ([hn] if mlstm else []) \
        + [wout, wrh, wrl, br]
    specs = [tok] + ([tok] if mlstm else []) + [tok, mspec(2), mspec(3), mspec(4), vec] \
        + ([vec] if mlstm else []) \
        + [pl.BlockSpec((D_MODEL, D_MODEL), const), pl.BlockSpec((D_MODEL, LANES), const),
           pl.BlockSpec((D_MODEL, LANES), const), pl.BlockSpec((1, LANES), const)]
    out_shape = [jax.ShapeDtypeStruct((n, D_MODEL), F32), jax.ShapeDtypeStruct((n, D_MODEL), BF16),
                 jax.ShapeDtypeStruct((n, LANES), F32)]
    out_specs = [tok, tok, small]
    if routed:
        t = jnp.arange(tm)
        ins.append((t[None, :] < t[:, None]).astype(BF16))
        specs.append(pl.BlockSpec((tm, tm), const))
        out_shape += [jax.ShapeDtypeStruct((n // tm, 1, tm), F32), jax.ShapeDtypeStruct((n, LANES), F32)]
        out_specs += [pl.BlockSpec((None, 1, tm), lambda i: (i, 0, 0)), small]
    if expand:
        ins.append(rep)
        specs.append(pl.BlockSpec(rep.shape, const))
    out_shape, out_specs = tuple(out_shape), tuple(out_specs)
    return pl.pallas_call(
        functools.partial(_post_kernel, mlstm, routed, expand),
        out_shape=out_shape,
        grid=(n // tm,),
        in_specs=specs,
        out_specs=out_specs,
        compiler_params=_params(1),
        name="post_mlstm" if mlstm else "post_attn",
    )(*ins)


ITEM_ROWS = 8
RUN_ALIGN = 16
RUN_SIZES = (512, 256, 128, 64, 32, 16)
RUN_SRC, RUN_DST, RUN_LEN, RUN_TOTAL = 0, N_GROUPS, 2 * N_GROUPS, 3 * N_GROUPS


def _stage_rows(tm):
    assert N_GROUPS * (RUN_ALIGN - 1) <= LANES
    return tm + LANES


def _sorted_rows(n, tm):
    rows = n + N_GROUPS * (n // tm) * RUN_ALIGN + tm
    return -(-rows // tm) * tm


def _sort_meta_kernel(tm, nts, gid_ref, q_ref, runs_ref, items_ref):
    ntiles = gid_ref.shape[0]
    gid = gid_ref[...]
    r = lax.broadcasted_iota(jnp.int32, (tm, tm), 0)
    c = lax.broadcasted_iota(jnp.int32, (tm, tm), 1)
    before = jnp.where(r < c, 1.0, 0.0).astype(BF16)
    trow = lax.broadcasted_iota(jnp.int32, (ntiles, LANES), 0)
    lane = lax.broadcasted_iota(jnp.int32, (ntiles, LANES), 1)
    k = lax.broadcasted_iota(jnp.int32, (1, LANES), 1).astype(F32)
    zero11 = jnp.zeros((1, 1), F32)
    q = jnp.zeros((ntiles, tm), F32)
    runs = jnp.zeros((ntiles, LANES), F32)
    src = jnp.zeros((ntiles, 1), F32)
    start, nitems = zero11, zero11
    grp = jnp.zeros((1, LANES), F32)
    tile = jnp.zeros((1, LANES), F32)
    valid = jnp.zeros((1, LANES), F32)
    for g in range(N_GROUPS):
        mask = jnp.where(gid == float(g), 1.0, 0.0)
        cnt = jnp.sum(mask, axis=1, keepdims=True)
        padded = jnp.floor((cnt + (RUN_ALIGN - 1.0)) * (1.0 / RUN_ALIGN)) * RUN_ALIGN
        incl = jnp.broadcast_to(padded, (ntiles, LANES))
        d = 1
        while d < ntiles:
            incl = incl + jnp.where(trow >= d, pltpu.roll(incl, d, axis=0), 0.0)
            d *= 2
        total = incl[ntiles - 1:ntiles, 0:1]
        dst = start + incl[:, 0:1] - padded
        q = q + mask * (src + _dot(mask.astype(BF16), before))
        runs = runs + jnp.where(lane == RUN_SRC + g, src, 0.0) + jnp.where(lane == RUN_DST + g, dst, 0.0) \
            + jnp.where(lane == RUN_LEN + g, padded, 0.0)
        src = src + padded
        end = start + total
        ft = jnp.floor(start * (1.0 / tm))
        lt = jnp.floor((end - 1.0) * (1.0 / tm))
        ni = jnp.where(total > 0.0, lt - ft + 1.0, 0.0)
        inside = (k >= nitems) & (k < nitems + ni)
        grp = grp + jnp.where(inside, float(g), 0.0)
        tile = tile + jnp.where(inside, ft + (k - nitems), 0.0)
        valid = valid + jnp.where(inside, 1.0, 0.0)
        nitems = nitems + ni
        start = end
    runs = runs + jnp.where(lane == RUN_TOTAL, start, 0.0)
    live = valid > 0.0
    prev = pltpu.roll(tile, 1, axis=1)
    nxt = pltpu.roll(tile, LANES - 1, axis=1)
    first = live & ((k == 0.0) | (tile != prev))
    last = live & ((k == nitems - 1.0) | (tile != nxt))
    spare = jnp.floor((start - 1.0) * (1.0 / tm)) + 1.0 + (k - nitems)
    fill = jnp.logical_not(live) & (spare <= nts - 1.0)
    tile = jnp.where(live, tile, jnp.minimum(spare, nts - 1.0))
    grp = jnp.where(live, grp, jnp.max(grp, axis=1, keepdims=True))
    flag = lambda m: jnp.where(m, 1.0, 0.0)
    q_ref[...] = q
    runs_ref[...] = runs.astype(jnp.int32)
    table = jnp.concatenate([tile, grp, valid, flag(first | fill), flag(last | fill),
                             jnp.zeros((ITEM_ROWS - 5, LANES), F32)], 0)
    items_ref[...] = table.astype(jnp.int32)


def _sort_meta_call(gid, tm):
    ntiles = gid.shape[0]
    nts = _sorted_rows(ntiles * tm, tm) // tm
    assert nts + N_GROUPS - 1 <= LANES and tm <= RUN_SIZES[0]
    return pl.pallas_call(
        functools.partial(_sort_meta_kernel, tm, nts),
        out_shape=(jax.ShapeDtypeStruct((ntiles, tm), F32),
                   jax.ShapeDtypeStruct((ntiles, LANES), jnp.int32),
                   jax.ShapeDtypeStruct((ITEM_ROWS, LANES), jnp.int32)),
        compiler_params=_params(0),
        name="moe_sort_meta",
    )(gid)


def _run_copies(runs_ref, i, tile_ref, sorted_hbm, sem, to_sorted):
    pieces = []
    for g in range(N_GROUPS):
        src = runs_ref[i, RUN_SRC + g]
        dst = runs_ref[i, RUN_DST + g]
        length = runs_ref[i, RUN_LEN + g]
        for s in RUN_SIZES:
            def build(src=src, dst=dst, length=length, s=s):
                off = length & (-2 * s)
                a = tile_ref.at[pl.ds(pl.multiple_of(src + off, RUN_ALIGN), s), :]
                b = sorted_hbm.at[pl.ds(pl.multiple_of(dst + off, RUN_ALIGN), s), :]
                return pltpu.make_async_copy(a, b, sem) if to_sorted else pltpu.make_async_copy(b, a, sem)
            pieces.append(((length & s) != 0, build))
    return pieces


def _start(pieces):
    for pred, build in pieces:
        pl.when(pred)(lambda build=build: build().start())


def _wait(pieces):
    for pred, build in pieces:
        pl.when(pred)(lambda build=build: build().wait())


def _start_then_wait(pieces):
    _start(pieces)
    _wait(pieces)


def _scatter_kernel(tm, nts, ntiles, runs_ref, q_ref, u_ref, g_ref, su_hbm, sg_hbm,
                    stu, stg, sem_u, sem_g):
    i = pl.program_id(0)
    slot = i % 2

    def copies(tile, slot):
        return (_run_copies(runs_ref, tile, stu.at[slot], su_hbm, sem_u.at[slot], True)
                + _run_copies(runs_ref, tile, stg.at[slot], sg_hbm, sem_g.at[slot], True))

    r = lax.broadcasted_iota(jnp.int32, (_stage_rows(tm), tm), 0).astype(F32)
    perm = jnp.where(q_ref[...] == r, 1.0, 0.0).astype(BF16)
    stu[slot] = _dot(perm, u_ref[...]).astype(BF16)
    g3 = _split3(g_ref[...])
    stg[slot] = _dot(perm, g3[0]) + _dot(perm, g3[1]) + _dot(perm, g3[2])
    _start(copies(i, slot))

    @pl.when(i > 0)
    def _():
        _wait(copies(i - 1, 1 - slot))

    @pl.when(i == ntiles - 1)
    def _():
        _wait(copies(i, slot))
        stu[slot] = jnp.zeros(stu.shape[1:], stu.dtype)
        stg[slot] = jnp.zeros(stg.shape[1:], stg.dtype)
        total = runs_ref[i, RUN_TOTAL]
        tail = nts * tm - total
        nfull = tail // tm
        pieces = []
        for tile_ref, hbm, sem in ((stu.at[slot], su_hbm, sem_u.at[slot]),
                                   (stg.at[slot], sg_hbm, sem_g.at[slot])):
            for j in range(nts - ntiles):
                def full(j=j, tile_ref=tile_ref, hbm=hbm, sem=sem):
                    dst = pl.multiple_of(total + j * tm, RUN_ALIGN)
                    return pltpu.make_async_copy(tile_ref.at[pl.ds(0, tm), :], hbm.at[pl.ds(dst, tm), :], sem)
                pieces.append((j < nfull, full))
            rem = tail - nfull * tm
            for s in RUN_SIZES:
                if s >= tm:
                    continue
                def part(s=s, tile_ref=tile_ref, hbm=hbm, sem=sem):
                    dst = pl.multiple_of(total + nfull * tm + (rem & (-2 * s)), RUN_ALIGN)
                    return pltpu.make_async_copy(tile_ref.at[pl.ds(0, s), :], hbm.at[pl.ds(dst, s), :], sem)
                pieces.append(((rem & s) != 0, part))
        _start_then_wait(pieces)


def _scatter_call(runs, q3, u2, gate, tm):
    n = u2.shape[0]
    ns = _sorted_rows(n, tm)
    grid_spec = pltpu.PrefetchScalarGridSpec(
        num_scalar_prefetch=1,
        grid=(n // tm,),
        in_specs=[pl.BlockSpec((None, 1, tm), lambda i, *_: (i, 0, 0)),
                  pl.BlockSpec((tm, D_MODEL), lambda i, *_: (i, 0)),
                  pl.BlockSpec((tm, LANES), lambda i, *_: (i, 0))],
        out_specs=(pl.BlockSpec(memory_space=pl.ANY), pl.BlockSpec(memory_space=pl.ANY)),
        scratch_shapes=[pltpu.VMEM((2, _stage_rows(tm), D_MODEL), BF16),
                        pltpu.VMEM((2, _stage_rows(tm), LANES), F32),
                        pltpu.SemaphoreType.DMA((2,)), pltpu.SemaphoreType.DMA((2,))])
    return pl.pallas_call(
        functools.partial(_scatter_kernel, tm, ns // tm, n // tm),
        out_shape=(jax.ShapeDtypeStruct((ns, D_MODEL), BF16), jax.ShapeDtypeStruct((ns, LANES), F32)),
        grid_spec=grid_spec,
        compiler_params=_params(1),
        name="moe_scatter",
    )(runs, q3, u2, gate)


def _ungroup(tm, runs_ref, col_ref, ys_hbm, stage, sem):
    i = pl.program_id(0)
    slot = i % 2

    def copies(tile, slot):
        return _run_copies(runs_ref, tile, stage.at[slot], ys_hbm, sem.at[slot], False)

    @pl.when(i == 0)
    def _():
        stage[...] = jnp.zeros_like(stage)
        _start(copies(i, slot))

    @pl.when(i + 1 < pl.num_programs(0))
    def _():
        _start(copies(i + 1, 1 - slot))

    _wait(copies(i, slot))
    col = col_ref[...]
    gidx = col[:, 0:1]
    src = [runs_ref[i, RUN_SRC + g].astype(F32) for g in range(N_GROUPS)]
    first = src[N_GROUPS - 1]
    for g in range(N_GROUPS - 2, -1, -1):
        first = jnp.where(gidx == float(g), src[g], first)
    q = first + col[:, 1:2]
    lanes = lax.broadcasted_iota(jnp.int32, (tm, _stage_rows(tm)), 1).astype(F32)
    perm_t = jnp.where(q == lanes, 1.0, 0.0).astype(BF16)
    return _dot(perm_t, stage[slot])


def _moe_routed_kernel(items_ref, su_ref, sg_ref, wgu_ref, wd_ref, out_ref, acc_ref, wgu_b, wd_b):
    kk = pl.program_id(0)
    grp = items_ref[1, kk]

    @pl.when((kk == 0) | (grp != items_ref[1, jnp.maximum(kk - 1, 0)]))
    def _():
        for e in range(EXPERTS_PER_GROUP):
            wgu_b[e] = wgu_ref[e].astype(BF16)
            wd_b[e] = wd_ref[e].astype(BF16)

    @pl.when(items_ref[3, kk] == 1)
    def _():
        acc_ref[...] = jnp.zeros_like(acc_ref)

    @pl.when(items_ref[2, kk] == 1)
    def _():
        u = su_ref[...]
        gate = sg_ref[...]
        lane = lax.broadcasted_iota(jnp.int32, gate.shape, 1)
        first_lane = N_GROUPS + EXPERTS_PER_GROUP * grp
        acts = []
        for e in range(EXPERTS_PER_GROUP):
            gcol = jnp.sum(jnp.where(lane == first_lane + e, gate, 0.0), axis=-1, keepdims=True)
            hu = _dot(u, wgu_b[e])
            hg = hu[:, 0:D_EXPERT]
            acts.append((hg * jax.nn.sigmoid(hg) * hu[:, D_EXPERT:2 * D_EXPERT] * gcol).astype(BF16))
        wd_all = wd_b[...].reshape(EXPERTS_PER_GROUP * D_EXPERT, D_MODEL)
        acc_ref[...] += _dot(jnp.concatenate(acts, 1), wd_all)

    @pl.when(items_ref[4, kk] == 1)
    def _():
        out_ref[...] = acc_ref[...].astype(out_ref.dtype)


def _moe_routed_call(items, su, sg, wgu, wd, layer, tm):
    ns = su.shape[0]
    n_items = ns // tm + N_GROUPS - 1
    depth = wgu.shape[0]
    wgu5 = wgu.reshape(depth, N_GROUPS, EXPERTS_PER_GROUP, D_MODEL, 2 * D_EXPERT)
    wd5 = wd.reshape(depth, N_GROUPS, EXPERTS_PER_GROUP, D_EXPERT, D_MODEL)
    grid_spec = pltpu.PrefetchScalarGridSpec(
        num_scalar_prefetch=1,
        grid=(n_items,),
        in_specs=[pl.BlockSpec((tm, D_MODEL), lambda k, it: (it[0, k], 0)),
                  pl.BlockSpec((tm, LANES), lambda k, it: (it[0, k], 0)),
                  pl.BlockSpec((None, None, EXPERTS_PER_GROUP, D_MODEL, 2 * D_EXPERT),
                               lambda k, it: (layer, it[1, k], 0, 0, 0)),
                  pl.BlockSpec((None, None, EXPERTS_PER_GROUP, D_EXPERT, D_MODEL),
                               lambda k, it: (layer, it[1, k], 0, 0, 0))],
        out_specs=pl.BlockSpec((tm, D_MODEL), lambda k, it: (it[0, k], 0)),
        scratch_shapes=[pltpu.VMEM((tm, D_MODEL), F32),
                        pltpu.VMEM((EXPERTS_PER_GROUP, D_MODEL, 2 * D_EXPERT), BF16),
                        pltpu.VMEM((EXPERTS_PER_GROUP, D_EXPERT, D_MODEL), BF16)])
    return pl.pallas_call(
        _moe_routed_kernel,
        out_shape=jax.ShapeDtypeStruct((ns, D_MODEL), BF16),
        grid_spec=grid_spec,
        compiler_params=_params(1),
        name="moe_routed",
    )(items, su, sg, wgu5, wd5)


def _resid_gather_kernel(tm, runs_ref, x_ref, g2_ref, col_ref, ys_hbm, y_ref, stage, sem):
    y_ref[...] = x_ref[...] + g2_ref[...] * _ungroup(tm, runs_ref, col_ref, ys_hbm, stage, sem)


def _resid_gather_call(x, runs, col, ys, mod, mspec, tm):
    n = x.shape[0]
    row = lambda i, *_: (i, 0)
    grid_spec = pltpu.PrefetchScalarGridSpec(
        num_scalar_prefetch=1,
        grid=(n // tm,),
        in_specs=[pl.BlockSpec((tm, D_MODEL), row), mspec(5), pl.BlockSpec((tm, LANES), row),
                  pl.BlockSpec(memory_space=pl.ANY)],
        out_specs=pl.BlockSpec((tm, D_MODEL), row),
        scratch_shapes=[pltpu.VMEM((2, _stage_rows(tm), D_MODEL), BF16), pltpu.SemaphoreType.DMA((2,))])
    return pl.pallas_call(
        functools.partial(_resid_gather_kernel, tm),
        out_shape=jax.ShapeDtypeStruct((n, D_MODEL), F32),
        grid_spec=grid_spec,
        compiler_params=_params(1),
        name="moe_unsort_resid",
    )(runs, x, mod, col, ys)


def _moe_kernel(expand, u_ref, gate_ref, wgu_ref, wd_ref, x1_ref, g2_ref, *rest):
    rep_ref = rest[0] if expand else None
    y_ref, acc_ref = rest[1 if expand else 0:]
    e = pl.program_id(1)

    @pl.when(e == 0)
    def _():
        acc_ref[...] = jnp.zeros_like(acc_ref)

    hu = _dot(u_ref[...], wgu_ref[...].astype(BF16))
    hg = hu[:, 0:D_EXPERT]
    hv = hu[:, D_EXPERT:2 * D_EXPERT]
    gate = gate_ref[...]
    lane = lax.broadcasted_iota(jnp.int32, gate.shape, 1)
    gcol = jnp.sum(jnp.where(lane == N_GROUPS + e, gate, 0.0), axis=-1, keepdims=True)
    act = hg * jax.nn.sigmoid(hg) * hv * gcol
    acc_ref[...] += _dot(act.astype(BF16), wd_ref[...].astype(BF16))

    @pl.when(e == N_EXPERTS - 1)
    def _():
        y_ref[...] = x1_ref[...] + _mod_rows(g2_ref, rep_ref) * acc_ref[...]


def _moe_call(u2, gate, wgu, wd, layer, x1, mod, mspec, rep, tm):
    n = u2.shape[0]
    row = lambda i, e: (i, 0)
    expand = rep is not None
    return pl.pallas_call(
        functools.partial(_moe_kernel, expand),
        out_shape=jax.ShapeDtypeStruct((n, D_MODEL), F32),
        grid=(n // tm, N_EXPERTS),
        in_specs=[pl.BlockSpec((tm, D_MODEL), row), pl.BlockSpec((tm, LANES), row),
                  pl.BlockSpec((None, None, D_MODEL, 2 * D_EXPERT), lambda i, e: (layer, e, 0, 0)),
                  pl.BlockSpec((None, None, D_EXPERT, D_MODEL), lambda i, e: (layer, e, 0, 0)),
                  pl.BlockSpec((tm, D_MODEL), row), mspec(5)]
        + ([pl.BlockSpec(rep.shape, lambda i, e: (0, 0))] if expand else []),
        out_specs=pl.BlockSpec((tm, D_MODEL), row),
        scratch_shapes=[pltpu.VMEM((tm, D_MODEL), F32)],
        compiler_params=_params(2),
        name="moe_dense",
    )(u2, gate, wgu, wd, x1, mod, *([rep] if expand else []))


def _rope128(x, cos, sa, sb):
    return x * cos + pltpu.roll(x, LANES - ROT_DIM // 2, axis=1) * sa \
        + pltpu.roll(x, ROT_DIM // 2, axis=1) * sb


def _kvq_kernel(gather_tm, *refs):
    if gather_tm:
        (runs_ref, x_ref, g2_ref, col_ref, ys_hbm, kvsh_ref, kvsc_ref, sh1_ref, sc1_ref, kvn_ref,
         nm_ref, wkv_ref, wq_ref, kn_ref, qn_ref, cos_ref, sa_ref, sb_ref, g64_ref, gr_ref, gb_ref,
         q_ref, k_ref, v_ref, xa_ref, stage, sem) = refs
        x = x_ref[...] + g2_ref[...] * _ungroup(gather_tm, runs_ref, col_ref, ys_hbm, stage, sem)
        xa_ref[...] = x
        rep_ref = None
    else:
        (x_ref, kvsh_ref, kvsc_ref, sh1_ref, sc1_ref, kvn_ref, nm_ref,
         wkv_ref, wq_ref, kn_ref, qn_ref, cos_ref, sa_ref, sb_ref, g64_ref, gr_ref, gb_ref,
         rep_ref, q_ref, k_ref, v_ref) = refs
        x = x_ref[...]
    xn = _unit_rms(x)
    cos, sa, sb = cos_ref[...], sa_ref[...], sb_ref[...]

    ukv = xn * kvn_ref[...] * (1.0 + _mod_rows(kvsc_ref, rep_ref)) + _mod_rows(kvsh_ref, rep_ref)
    kv = _dot(ukv.astype(BF16), wkv_ref[...])
    k = kv[:, 0:LANES]
    v_ref[...] = kv[:, LANES:2 * LANES]
    kh, kl = _split2(k * k)
    ms = _dot(kh, g64_ref[...]) + _dot(kl, g64_ref[...])
    k_ref[...] = _rope128(k * lax.rsqrt(ms + RMS_EPS) * kn_ref[...], cos, sa, sb)

    u1 = xn * nm_ref[...] * (1.0 + _mod_rows(sc1_ref, rep_ref)) + _mod_rows(sh1_ref, rep_ref)
    q = _dot(u1.astype(BF16), wq_ref[...])
    ms16 = _dot((q * q).astype(BF16), gr_ref[...])
    rh, rl = _split2(lax.rsqrt(ms16 + RMS_EPS))
    rsb = _dot(rh, gb_ref[...]) + _dot(rl, gb_ref[...])
    qn = q * rsb * qn_ref[...]
    for i in range(D_MODEL // LANES):
        sl = slice(LANES * i, LANES * (i + 1))
        q_ref[:, sl] = (_rope128(qn[:, sl], cos, sa, sb) * (HEAD_DIM ** -0.5)).astype(BF16)


def _kvq_call(x, routed, mod0, mod1, mspec, kvmod, kvspec, rep, kvn, nm, wkv, wq, kn, qn, tabs, tab,
              g64, gr, gb, tm):
    n = x.shape[0]
    row = lambda i, *_: (i, 0)
    const = lambda i, *_: (0, 0)
    vec = pl.BlockSpec((1, D_MODEL), const)
    tok = pl.BlockSpec((tm, D_MODEL), row)
    small = pl.BlockSpec((tm, LANES), row)
    ins = [x, kvmod, kvmod, mod1, mod1, kvn, nm, wkv, wq, kn, qn, *tabs, g64, gr, gb]
    specs = [tok, kvspec(0), kvspec(1), mspec(0), mspec(1), vec, vec,
             pl.BlockSpec((D_MODEL, 2 * LANES), const), pl.BlockSpec((D_MODEL, D_MODEL), const),
             pl.BlockSpec((1, LANES), const), vec, tab, tab, tab,
             pl.BlockSpec((LANES, LANES), const), pl.BlockSpec((D_MODEL, LANES), const),
             pl.BlockSpec((LANES, D_MODEL), const)]
    out_shape = [jax.ShapeDtypeStruct((n, D_MODEL), BF16), jax.ShapeDtypeStruct((n, LANES), F32),
                 jax.ShapeDtypeStruct((n, LANES), F32)]
    out_specs = [tok, small, small]
    scratch = []
    nprefetch = 0
    if routed is not None:
        runs, col, ys = routed
        nprefetch = 1
        ins = [runs, x, mod0, col, ys] + ins[1:]
        specs = [tok, mspec(5), small, pl.BlockSpec(memory_space=pl.ANY)] + specs[1:]
        out_shape.append(jax.ShapeDtypeStruct((n, D_MODEL), F32))
        out_specs.append(tok)
        scratch = [pltpu.VMEM((2, _stage_rows(tm), D_MODEL), BF16), pltpu.SemaphoreType.DMA((2,))]
    else:
        ins.append(rep)
        specs.append(pl.BlockSpec(rep.shape, const))
    grid_spec = pltpu.PrefetchScalarGridSpec(
        num_scalar_prefetch=nprefetch, grid=(n // tm,), in_specs=specs,
        out_specs=tuple(out_specs), scratch_shapes=scratch)
    return pl.pallas_call(
        functools.partial(_kvq_kernel, tm if routed is not None else 0),
        out_shape=tuple(out_shape),
        grid_spec=grid_spec,
        compiler_params=_params(1),
        name="kv_q_proj",
    )(*ins)


def _attn_core(q, kcat, vcat, bias, sinks_ref, o_ref):
    tk = kcat.shape[0]
    pairs = N_Q_HEADS // N_KV_HEADS // 2
    lane = lax.broadcasted_iota(jnp.int32, (1, LANES), 1)
    lo = lane < HEAD_DIM
    kro = pltpu.roll(kcat, HEAD_DIM, axis=1)
    vro = pltpu.roll(vcat, HEAD_DIM, axis=1)
    one_e = jnp.broadcast_to(jnp.where(lo, 1.0, 0.0), (tk, LANES))
    one_o = 1.0 - one_e
    for g in range(N_KV_HEADS):
        if g == 0:
            ke, ko = jnp.where(lo, kcat, 0.0), jnp.where(lo, 0.0, kro)
            ve, vo = jnp.where(lo, vcat, 0.0), jnp.where(lo, 0.0, vro)
        else:
            ke, ko = jnp.where(lo, kro, 0.0), jnp.where(lo, 0.0, kcat)
            ve, vo = jnp.where(lo, vro, 0.0), jnp.where(lo, 0.0, vcat)
        k2 = jnp.concatenate([ke, ko], 0).astype(BF16)
        v2 = jnp.concatenate([jnp.concatenate([ve, one_e], 1),
                              jnp.concatenate([vo, one_o], 1)], 0).astype(BF16)
        for p in range(pairs):
            hp = g * pairs + p
            s = _dot_nt(q[:, LANES * hp:LANES * (hp + 1)], k2)
            halves, corr = [], []
            for par in range(2):
                sp = s[:, par * tk:(par + 1) * tk] + bias
                sink = sinks_ref[2 * hp + par]
                m = jnp.maximum(jnp.max(sp, axis=-1, keepdims=True), sink)
                halves.append(jnp.exp(sp - m).astype(BF16))
                corr.append(jnp.exp(sink - m))
            o2 = _dot(jnp.concatenate(halves, 1), v2)
            den = o2[:, LANES:2 * LANES] + jnp.where(lo, corr[0], corr[1])
            o_ref[:, LANES * hp:LANES * (hp + 1)] = (o2[:, 0:LANES] / den).astype(o_ref.dtype)


def _attn_prompt_kernel(sinks_ref, q_ref, kp_ref, kc_ref, vp_ref, vc_ref, bias_ref, o_ref):
    kcat = jnp.concatenate([kp_ref[...], kc_ref[...]], 0)
    vcat = jnp.concatenate([vp_ref[...], vc_ref[...]], 0)
    _attn_core(q_ref[...], kcat, vcat, bias_ref[...], sinks_ref, o_ref)


ATTN_SEQS_PER_STEP = 8


def _attn_sample_kernel(sinks_ref, q_ref, kc_ref, kn_ref, vc_ref, vn_ref, bias_ref, o_ref, kbuf, vbuf):
    @pl.when(pl.program_id(0) == 0)
    def _():
        kbuf[...] = jnp.zeros_like(kbuf)
        vbuf[...] = jnp.zeros_like(vbuf)

    for b in range(ATTN_SEQS_PER_STEP):
        kbuf[b, 0:WINDOW, :] = kc_ref[b]
        kbuf[b, WINDOW:WINDOW + SAMPLE_PAD, :] = kn_ref[b]
        vbuf[b, 0:WINDOW, :] = vc_ref[b]
        vbuf[b, WINDOW:WINDOW + SAMPLE_PAD, :] = vn_ref[b]
    for b in range(ATTN_SEQS_PER_STEP):
        _attn_core(q_ref[b], kbuf[b], vbuf[b], bias_ref[...], sinks_ref, o_ref.at[b])


def _window_bias(tq, first):
    qi = jnp.arange(tq)[:, None]
    kj = jnp.arange(2 * WINDOW)[None, :]
    ok = (kj > qi) & (kj <= qi + WINDOW)
    if first:
        ok = ok & (kj >= WINDOW)
    return jnp.where(ok, 0.0, -jnp.inf).astype(F32)


def _attn_prompt_call(sinks, q, k, v, B, nb):
    n = q.shape[0]
    cur = lambda b, i: (b * nb + i, 0)
    prev = lambda b, i: (b * nb + jnp.maximum(i - 1, 0), 0)
    kvb = lambda im: pl.BlockSpec((WINDOW, LANES), im)
    bias = jnp.stack([_window_bias(WINDOW, True), _window_bias(WINDOW, False)])
    return pl.pallas_call(
        _attn_prompt_kernel,
        out_shape=jax.ShapeDtypeStruct((n, D_MODEL), BF16),
        grid=(B, nb),
        in_specs=[pl.BlockSpec(memory_space=pltpu.SMEM),
                  pl.BlockSpec((WINDOW, D_MODEL), cur), kvb(prev), kvb(cur), kvb(prev), kvb(cur),
                  pl.BlockSpec((None, WINDOW, 2 * WINDOW), lambda b, i: (jnp.minimum(i, 1), 0, 0))],
        out_specs=pl.BlockSpec((WINDOW, D_MODEL), cur),
        compiler_params=_params(2),
        name="swa_prompt",
    )(sinks, q, k, k, v, v, bias)


def _attn_sample_call(sinks, q, kcache, knew, vcache, vnew):
    B = q.shape[0]
    nseq = ATTN_SEQS_PER_STEP
    b3 = lambda b: (b, 0, 0)
    cache = pl.BlockSpec((nseq, WINDOW, LANES), b3)
    new = pl.BlockSpec((nseq, SAMPLE_PAD, LANES), b3)
    return pl.pallas_call(
        _attn_sample_kernel,
        out_shape=jax.ShapeDtypeStruct(q.shape, BF16),
        grid=(B // nseq,),
        in_specs=[pl.BlockSpec(memory_space=pltpu.SMEM),
                  pl.BlockSpec((nseq, SAMPLE_PAD, D_MODEL), b3), cache, new, cache, new,
                  pl.BlockSpec((SAMPLE_PAD, 2 * WINDOW), lambda b: (0, 0))],
        out_specs=pl.BlockSpec((nseq, SAMPLE_PAD, D_MODEL), b3),
        scratch_shapes=[pltpu.VMEM((nseq, 2 * WINDOW, LANES), F32),
                        pltpu.VMEM((nseq, 2 * WINDOW, LANES), F32)],
        compiler_params=_params(1),
        name="swa_sample",
    )(sinks, q, kcache, knew, vcache, vnew, _window_bias(SAMPLE_PAD, False))


def _rope_tables(pos):
    half = ROT_DIM // 2
    inv = ROPE_THETA ** (-jnp.arange(half, dtype=F32) / half)
    ang = pos.astype(F32)[:, None] * inv[None]
    cos, sin = jnp.cos(ang), jnp.sin(ang)
    d = jnp.arange(LANES) % HEAD_DIM
    idx = d % half
    cos_t = jnp.where(d < ROT_DIM, cos[:, idx], 1.0)
    sa = jnp.where(d < half, -sin[:, idx], 0.0)
    sb = jnp.where((d >= half) & (d < ROT_DIM), sin[:, idx], 0.0)
    return cos_t, sa, sb


def _pad_lanes(a, value=0.0):
    return jnp.pad(a, ((0, 0), (0, LANES - a.shape[1])), constant_values=value)


def _prep_weights(ada_w, ada_b, norm_mix, norm_ffn, a_w_in, a_b_gates, a_head_norm, a_w_out,
                  kv_ada_w, kv_ada_b, kv_norm, w_k, w_v, k_norm, b_w_q, b_q_norm, b_sinks, b_w_o,
                  moe_w_group, moe_b_group, moe_w_expert, moe_b_expert, moe_w_gate_up, moe_w_down):
    w = {}
    g0 = QK_COLS + 2 * V_COLS
    w["w_in"] = a_w_in
    w["w_gates"] = jnp.concatenate([_pad_lanes(a_w_in[0, :, g0:g0 + M_HEADS]),
                                    _pad_lanes(a_w_in[0, :, g0 + M_HEADS:])], 1)
    w["bli"] = _pad_lanes(a_b_gates[0][None, :M_HEADS])
    w["blf"] = _pad_lanes(a_b_gates[0][None, M_HEADS:])
    w["head_norm"] = a_head_norm[0][None]
    w["w_out"] = a_w_out[0].astype(BF16)
    w["norm_mix"] = [norm_mix[l][None] for l in range(2)]
    w["norm_ffn"] = [norm_ffn[l][None] for l in range(2)]
    w["router"] = []
    for l in range(2):
        wr = _pad_lanes(jnp.concatenate([moe_w_group[l], moe_w_expert[l]], 1))
        hi = wr.astype(BF16)
        lo = (wr - hi.astype(F32)).astype(BF16)
        br = _pad_lanes(jnp.concatenate([moe_b_group[l], moe_b_expert[l]])[None])
        w["router"].append((hi, lo, br))
    w["w_gu"] = moe_w_gate_up
    w["w_d"] = moe_w_down
    w["kv_norm"] = kv_norm[None]
    w["w_kv"] = jnp.concatenate([w_k, w_v], 1).astype(BF16)
    w["k_norm"] = jnp.tile(k_norm, N_KV_HEADS)[None]
    w["w_q"] = b_w_q[0].astype(BF16)
    w["q_norm"] = jnp.tile(b_q_norm[0], N_Q_HEADS)[None]
    w["sinks"] = b_sinks[0]
    w["w_o"] = b_w_o[0].astype(BF16)
    lanes = jnp.arange(LANES)
    feat = jnp.arange(D_MODEL)
    w["g64"] = jnp.where((lanes[:, None] // HEAD_DIM) == (lanes[None, :] // HEAD_DIM),
                         1.0 / HEAD_DIM, 0.0).astype(BF16)
    w["gr"] = jnp.where((feat[:, None] // HEAD_DIM) == lanes[None, :], 1.0 / HEAD_DIM, 0.0).astype(BF16)
    w["gb"] = jnp.where(lanes[:, None] == (feat[None, :] // HEAD_DIM), 1.0, 0.0).astype(BF16)
    return w


def _trunk(x2, mods, kvmod, mspec, mspec_moe, rep, w, *, B, T, L, tm, tm_moe, tabs, tabspec, c0, n0, m0,
           cache_k=None, cache_v=None):
    sample = cache_k is not None
    nc = T // L if not sample else 1

    qk, v, o, li, lf = _inproj_call(x2, mods[0], mspec, rep, w["norm_mix"][0], w["w_in"],
                                    w["w_gates"], w["bli"], w["blf"], tm)
    if sample:
        def padtok(a, value=0.0):
            a = a.reshape(B, T, a.shape[-1])
            a = jnp.pad(a, ((0, 0), (0, L - T), (0, 0)), constant_values=value)
            return a.reshape(B * L, a.shape[-1])
        qk, v, li, lf = padtok(qk), padtok(v), padtok(li, M_EMPTY), padtok(lf)
    h, c_new, n_new, m_new = _mlstm_call(qk, v, li, lf, c0, n0, m0, B, nc, L)
    if sample:
        h = h.reshape(B, L, V_COLS)[:, :T].reshape(B * T, V_COLS)
    routed = not sample
    ntiles = (B * T) // tm

    def routed_moe(u2, gate, gid, col, layer):
        q, runs, items = _sort_meta_call(gid.reshape(ntiles, tm), tm)
        su, sg = _scatter_call(runs, q.reshape(ntiles, 1, tm), u2, gate, tm)
        ys = _moe_routed_call(items, su, sg, w["w_gu"], w["w_d"], layer, tm)
        return runs, col, ys

    rh, rl, br = w["router"][0]
    post0 = _post_call(True, routed, h, o, x2, mods[0], mspec, rep, w["norm_ffn"][0], w["head_norm"],
                       w["w_out"], rh, rl, br, tm)
    kvq_args = (mods[0], mods[1], mspec, kvmod, mspec, rep, w["kv_norm"], w["norm_mix"][1],
                w["w_kv"], w["w_q"], w["k_norm"], w["q_norm"], tabs, tabspec,
                w["g64"], w["gr"], w["gb"], tm)

    if routed:
        x1, u2, gate, gid, col = post0
        q, k, vv, xa = _kvq_call(x1, routed_moe(u2, gate, gid, col, 0), *kvq_args)
    else:
        x1, u2, gate = post0
        xa = _moe_call(u2, gate, w["w_gu"], w["w_d"], 0, x1, mods[0], mspec_moe, rep, tm_moe)
        q, k, vv = _kvq_call(xa, None, *kvq_args)
    if not sample:
        att = _attn_prompt_call(w["sinks"], q, k, vv, B, T // WINDOW)
        k_win = k.reshape(B, T, LANES)[:, T - WINDOW:].reshape(B, WINDOW, N_KV_HEADS, HEAD_DIM)
        v_win = vv.reshape(B, T, LANES)[:, T - WINDOW:].reshape(B, WINDOW, N_KV_HEADS, HEAD_DIM)
    else:
        def padseq(a):
            return jnp.pad(a.reshape(B, T, a.shape[-1]), ((0, 0), (0, SAMPLE_PAD - T), (0, 0)))
        kc = cache_k.reshape(B, WINDOW, LANES)
        vc = cache_v.reshape(B, WINDOW, LANES)
        att = _attn_sample_call(w["sinks"], padseq(q), kc, padseq(k), vc, padseq(vv))
        att = att[:, :T].reshape(B * T, D_MODEL)
        k_win = jnp.concatenate([kc[:, T:], k.reshape(B, T, LANES)], 1)
        v_win = jnp.concatenate([vc[:, T:], vv.reshape(B, T, LANES)], 1)
        k_win = k_win.reshape(B, WINDOW, N_KV_HEADS, HEAD_DIM)
        v_win = v_win.reshape(B, WINDOW, N_KV_HEADS, HEAD_DIM)
    rh, rl, br = w["router"][1]
    post1 = _post_call(False, routed, att, None, xa, mods[1], mspec, rep, w["norm_ffn"][1], None,
                       w["w_o"], rh, rl, br, tm)
    if routed:
        x3, u4, gate, gid, col = post1
        runs, col, ys = routed_moe(u4, gate, gid, col, 1)
        y = _resid_gather_call(x3, runs, col, ys, mods[1], mspec, tm)
    else:
        x3, u4, gate = post1
        y = _moe_call(u4, gate, w["w_gu"], w["w_d"], 1, x3, mods[1], mspec_moe, rep, tm_moe)
    c_out = c_new[None]
    n_out = n_new.reshape(1, B, M_HEADS, M_DK)
    m_out = m_new[:, 0, :M_HEADS][None]
    return y, c_out, n_out, m_out, k_win, v_win


def kernel(x_prompt, x_sample, c_prompt, c_sample, state_c, state_n, state_m, cache_k_win, cache_v_win, ada_w, ada_b, norm_mix, norm_ffn, a_w_in, a_b_gates, a_head_norm, a_w_out, kv_ada_w, kv_ada_b, kv_norm, w_k, w_v, k_norm, b_w_q, b_q_norm, b_sinks, b_w_o, moe_w_group, moe_b_group, moe_w_expert, moe_b_expert, moe_w_gate_up, moe_w_down):
    Bp, Tp, D = x_prompt.shape
    Bs, Ts, _ = x_sample.shape
    w = _prep_weights(ada_w, ada_b, norm_mix, norm_ffn, a_w_in, a_b_gates, a_head_norm, a_w_out,
                      kv_ada_w, kv_ada_b, kv_norm, w_k, w_v, k_norm, b_w_q, b_q_norm, b_sinks, b_w_o,
                      moe_w_group, moe_b_group, moe_w_expert, moe_b_expert, moe_w_gate_up, moe_w_down)

    rows = Bp + Bs
    rpad = -rows % 8
    c_all = jnp.concatenate([c_prompt, c_sample, jnp.zeros((rpad, D), F32)], 0)
    mod = _ada_call(c_all, ada_w, ada_b[:, None, :])
    kvm = _ada_call(c_all, kv_ada_w[None], kv_ada_b[None, None, :])

    tm_p = 512
    tiles_per_seq = Tp // tm_p
    mods_p = [mod[l, :Bp][:, None, :] for l in range(2)]
    kvmod_p = kvm[0, :Bp][:, None, :]

    def mspec_p(col):
        return pl.BlockSpec((None, 1, D_MODEL), lambda i, *_: (i // tiles_per_seq, 0, col))

    tm_moe_p = 1024
    moe_tiles_per_seq = Tp // tm_moe_p

    def mspec_moe_p(col):
        return pl.BlockSpec((None, 1, D_MODEL), lambda i, *_: (i // moe_tiles_per_seq, 0, col))

    tabs_p = _rope_tables(jnp.arange(Tp, dtype=jnp.int32))
    tabspec_p = pl.BlockSpec((tm_p, LANES), lambda i, *_: (i % tiles_per_seq, 0))
    npairs = M_HEADS // 2
    c0 = jnp.zeros((Bp, M_HEADS, M_DK, M_DV), F32)
    n0 = jnp.zeros((Bp, npairs, LANES), F32)
    m0 = jnp.pad(jnp.full((Bp, 1, M_HEADS), M_EMPTY, F32), ((0, 0), (0, 0), (0, LANES - M_HEADS)))
    yp, cp, np_, mp, kwp, vwp = _trunk(
        x_prompt.reshape(Bp * Tp, D), mods_p, kvmod_p, mspec_p, mspec_moe_p, None, w,
        B=Bp, T=Tp, L=M_CHUNK, tm=tm_p, tm_moe=tm_moe_p, tabs=tabs_p, tabspec=tabspec_p,
        c0=c0, n0=n0, m0=m0)

    ns = Bs * Ts
    mods_s = [mod[l, Bp:Bp + Bs][None] for l in range(2)]
    kvmod_s = kvm[0, Bp:Bp + Bs][None]
    rep = (jnp.arange(ns)[:, None] // Ts == jnp.arange(Bs)[None, :]).astype(BF16)

    def mspec_s(col):
        return pl.BlockSpec((None, Bs, D_MODEL), lambda i, *_: (0, 0, col))

    tabs_s = _rope_tables(PAST_LEN + jnp.arange(Ts, dtype=jnp.int32))
    tabs_s = tuple(jnp.tile(t, (Bs, 1)) for t in tabs_s)
    m0s = jnp.pad(state_m[0][:, None, :], ((0, 0), (0, 0), (0, LANES - M_HEADS)))
    ys, cs, ns_, ms, kws, vws = _trunk(
        x_sample.reshape(ns, D), mods_s, kvmod_s, mspec_s, mspec_s, rep, w,
        B=Bs, T=Ts, L=SAMPLE_PAD, tm=ns, tm_moe=ns, tabs=tabs_s,
        tabspec=pl.BlockSpec((ns, LANES), lambda i, *_: (0, 0)),
        c0=state_c[0], n0=state_n[0].reshape(Bs, npairs, LANES), m0=m0s,
        cache_k=cache_k_win, cache_v=cache_v_win)

    return (yp.reshape(Bp, Tp, D), ys.reshape(Bs, Ts, D), cp, np_, mp, kwp, vwp,
            cs, ns_, ms, kws, vws)
```

```python
import functools

import jax
import jax.numpy as jnp
from jax import lax
from jax.experimental import pallas as pl
from jax.experimental.pallas import tpu as pltpu

F32 = jnp.float32
BF16 = jnp.bfloat16

D_MODEL = 1024
PAST_LEN = 8192
M_HEADS = 8
M_DK = 64
M_DV = 128
M_CHUNK = 128
GATE_SOFTCAP = 15.0
M_EMPTY = -1e30
WINDOW = 128
HEAD_DIM = 64
N_Q_HEADS = 16
N_KV_HEADS = 2
ROPE_THETA = 500000.0
ROT_DIM = 16
N_GROUPS = 4
EXPERTS_PER_GROUP = 4
N_EXPERTS = 16
D_EXPERT = 256
RMS_EPS = 1e-6

LANES = 128
QK_COLS = 2 * M_HEADS * M_DK
V_COLS = M_HEADS * M_DV
IN_COLS = QK_COLS + 2 * V_COLS + 2 * LANES
SAMPLE_PAD = 16
VMEM_LIMIT = 52 * 1024 * 1024

NT_DIMS = (((1,), (1,)), ((), ()))


def _params(n_axes):
    return pltpu.CompilerParams(dimension_semantics=("arbitrary",) * n_axes,
                                vmem_limit_bytes=VMEM_LIMIT)


def _dot(a, b):
    return jnp.dot(a, b, preferred_element_type=F32)


def _dot_nt(a, b):
    return lax.dot_general(a, b, NT_DIMS, preferred_element_type=F32)


def _split2(x):
    hi = x.astype(BF16)
    lo = (x - hi.astype(F32)).astype(BF16)
    return hi, lo


def _split3(x):
    hi = x.astype(BF16)
    r = x - hi.astype(F32)
    mid = r.astype(BF16)
    lo = (r - mid.astype(F32)).astype(BF16)
    return hi, mid, lo


def _unit_rms(x):
    return x * lax.rsqrt(jnp.mean(x * x, axis=-1, keepdims=True) + RMS_EPS)


def _ada_kernel(c_ref, w_ref, b_ref, o_ref):
    c = c_ref[...]
    cs = (c * jax.nn.sigmoid(c)).astype(BF16)
    o_ref[...] = _dot(cs, w_ref[...].astype(BF16)) + b_ref[...]


def _ada_call(c, w, b):
    g, d, n = w.shape
    r = c.shape[0]
    tn = 1024
    return pl.pallas_call(
        _ada_kernel,
        out_shape=jax.ShapeDtypeStruct((g, r, n), F32),
        grid=(g, n // tn),
        in_specs=[pl.BlockSpec((r, d), lambda i, j: (0, 0)),
                  pl.BlockSpec((None, d, tn), lambda i, j: (i, 0, j)),
                  pl.BlockSpec((None, 1, tn), lambda i, j: (i, 0, j))],
        out_specs=pl.BlockSpec((None, r, tn), lambda i, j: (i, 0, j)),
        compiler_params=_params(2),
        name="ada_mod",
    )(c, w, b)


def _mod_rows(ref, rep_ref):
    if rep_ref is None:
        return ref[...]
    hi, mid, lo = _split3(ref[...])
    rep = rep_ref[...]
    return _dot(rep, hi) + _dot(rep, mid) + _dot(rep, lo)


def _inproj_kernel(expand, x_ref, sh_ref, sc_ref, nw_ref, w_ref, wg_ref, bli_ref, blf_ref, *rest):
    rep_ref = rest[0] if expand else None
    qk_ref, v_ref, o_ref, li_ref, lf_ref, wb_ref = rest[1 if expand else 0:]

    @pl.when(pl.program_id(0) == 0)
    def _():
        wb_ref[...] = w_ref[...].astype(BF16)

    u = _unit_rms(x_ref[...]) * nw_ref[...] * (1.0 + _mod_rows(sc_ref, rep_ref)) \
        + _mod_rows(sh_ref, rep_ref)
    ub = u.astype(BF16)
    half = QK_COLS // 2
    q = _dot(ub, wb_ref[:, 0:half]) * (M_DK ** -0.5)
    qk_ref[:, 0:half] = q.astype(BF16)
    qk_ref[:, half:QK_COLS] = _dot(ub, wb_ref[:, half:QK_COLS]).astype(BF16)
    v_ref[...] = _dot(ub, wb_ref[:, QK_COLS:QK_COLS + V_COLS]).astype(BF16)
    o_ref[...] = _dot(ub, wb_ref[:, QK_COLS + V_COLS:QK_COLS + 2 * V_COLS]).astype(BF16)
    lane = lax.broadcasted_iota(jnp.int32, (1, LANES), 1)
    live = lane < M_HEADS
    wg = wg_ref[...].astype(BF16)
    gi = _dot(ub, wg[:, 0:LANES]) + bli_ref[...]
    gf = _dot(ub, wg[:, LANES:2 * LANES]) + blf_ref[...]
    li = GATE_SOFTCAP * jnp.tanh(gi / GATE_SOFTCAP)
    fpre = GATE_SOFTCAP * jnp.tanh(gf / GATE_SOFTCAP)
    lf = jnp.minimum(fpre, 0.0) - jnp.log1p(jnp.exp(-jnp.abs(fpre)))
    li_ref[...] = jnp.where(live, li, 0.0)
    lf_ref[...] = jnp.where(live, lf, 0.0)


def _inproj_call(x, mod, mspec, rep, nw, w_in, wg, bli, blf, tm):
    n = x.shape[0]
    row = lambda i: (i, 0)
    const = lambda i: (0, 0)
    main_cols = QK_COLS + 2 * V_COLS
    expand = rep is not None
    return pl.pallas_call(
        functools.partial(_inproj_kernel, expand),
        out_shape=(jax.ShapeDtypeStruct((n, QK_COLS), BF16),
                   jax.ShapeDtypeStruct((n, V_COLS), BF16),
                   jax.ShapeDtypeStruct((n, V_COLS), BF16),
                   jax.ShapeDtypeStruct((n, LANES), F32),
                   jax.ShapeDtypeStruct((n, LANES), F32)),
        grid=(n // tm,),
        in_specs=[pl.BlockSpec((tm, D_MODEL), row), mspec(0), mspec(1),
                  pl.BlockSpec((1, D_MODEL), const),
                  pl.BlockSpec((None, D_MODEL, main_cols), lambda i: (0, 0, 0),
                               pipeline_mode=pl.Buffered(1)),
                  pl.BlockSpec((D_MODEL, 2 * LANES), const),
                  pl.BlockSpec((1, LANES), const), pl.BlockSpec((1, LANES), const)]
        + ([pl.BlockSpec(rep.shape, const)] if expand else []),
        out_specs=(pl.BlockSpec((tm, QK_COLS), row), pl.BlockSpec((tm, V_COLS), row),
                   pl.BlockSpec((tm, V_COLS), row), pl.BlockSpec((tm, LANES), row),
                   pl.BlockSpec((tm, LANES), row)),
        scratch_shapes=[pltpu.VMEM((D_MODEL, main_cols), BF16)],
        compiler_params=_params(1),
        name="mlstm_inproj",
    )(x, mod, mod, nw, w_in, wg, bli, blf, *([rep] if expand else []))


def _mlstm_kernel(L, nseq, scale_q, *refs):
    c = pl.program_id(1)
    seqs = [[r.at[b] for r in refs] for b in range(nseq)]

    @pl.when(c == 0)
    def _():
        for s in seqs:
            _mlstm_load_state(*s[4:7], *s[11:14])

    for s in seqs:
        _mlstm_seq_step(L, scale_q, *s[0:4], s[7], *s[11:14])

    @pl.when(c == pl.num_programs(1) - 1)
    def _():
        for s in seqs:
            _mlstm_store_state(*s[8:14])


def _mlstm_load_state(c0_ref, n0_ref, m0_ref, c2_s, n_s, m_s):
    m_s[...] = m0_ref[...]
    n_s[...] = n0_ref[...]
    z = jnp.zeros((M_DK, M_DV), F32)
    for j in range(M_HEADS // 2):
        c2_s[j] = jnp.concatenate([jnp.concatenate([c0_ref[2 * j], z], 1),
                                   jnp.concatenate([z, c0_ref[2 * j + 1]], 1)], 0)


def _mlstm_seq_step(L, scale_q, qk_ref, v_ref, li_ref, lf_ref, h_ref, c2_s, n_s, m_s):
    npairs = M_HEADS // 2
    hd = M_DK

    lane = lax.broadcasted_iota(jnp.int32, (1, LANES), 1)
    lo128 = lane < hd
    lane256 = lax.broadcasted_iota(jnp.int32, (1, 2 * LANES), 1)
    lo256 = lane256 < LANES
    row128 = lax.broadcasted_iota(jnp.int32, (LANES, 1), 0)
    top = row128 < hd
    blockdiag = (top & lo256) | (jnp.logical_not(top) & jnp.logical_not(lo256))

    LI = li_ref[...]
    LF = lf_ref[...]
    rowL = lax.broadcasted_iota(jnp.int32, (L, LANES), 0)

    def prefix(x, op, ident):
        d = 1
        while d < L:
            shifted = pltpu.roll(x, d, axis=0)
            x = op(x, jnp.where(rowL >= d, shifted, ident))
            d *= 2
        return x

    Bc = prefix(LF, jnp.add, 0.0)
    Cm = LI - Bc
    mprev = m_s[...]
    Gc = jnp.maximum(mprev, prefix(Cm, jnp.maximum, -jnp.inf))
    A = jnp.exp(mprev - Gc)
    bL = Bc[L - 1:L, :]
    DL = bL + Cm
    mnew = jnp.maximum(bL + mprev, jnp.max(DL, axis=0, keepdims=True))
    ast = jnp.exp(bL + mprev - mnew)
    WST = jnp.exp(DL - mnew)

    def pad_rows(x, rows):
        if x.shape[0] == rows:
            return x
        return jnp.concatenate([x, jnp.zeros((rows - x.shape[0], x.shape[1]), x.dtype)], 0)

    kw_lanes = max(L, hd)
    assert kw_lanes in (hd, LANES) and L <= LANES
    wide = kw_lanes == LANES
    lane_s = lax.broadcasted_iota(jnp.int32, (1, 2 * kw_lanes), 1)
    row_s = lax.broadcasted_iota(jnp.int32, (2 * kw_lanes, 1), 0)
    cm_pad = pad_rows(Cm, kw_lanes)
    if wide:
        XT = cm_pad.T
    else:
        XT = jnp.concatenate([cm_pad, cm_pad], 0).T
    causal = (lane_s & (kw_lanes - 1)) <= lax.broadcasted_iota(jnp.int32, (L, 2 * kw_lanes), 0)
    top_s = row_s < kw_lanes
    J = jnp.where((top_s & lo256) | (jnp.logical_not(top_s) & jnp.logical_not(lo256)),
                  1.0, 0.0).astype(BF16)

    EXPM = jnp.exp(-(Bc + Gc))

    def bc(X, h):
        return jnp.broadcast_to(X[:, h:h + 1], X.shape)

    for j in range(npairs):
        he, ho = 2 * j, 2 * j + 1
        p128 = slice(LANES * j, LANES * (j + 1))
        p256 = slice(2 * LANES * j, 2 * LANES * (j + 1))

        def pair128(X):
            return jnp.where(lo128, bc(X, he), bc(X, ho))

        def pair256(X):
            return jnp.concatenate([bc(X, he), bc(X, ho)], 1)

        q128 = qk_ref[:, LANES * j:LANES * (j + 1)]
        k128 = qk_ref[:, QK_COLS // 2 + LANES * j:QK_COLS // 2 + LANES * (j + 1)]
        v256 = v_ref[:, 2 * LANES * j:2 * LANES * (j + 1)]
        zk = jnp.zeros_like(k128)
        zv = jnp.zeros_like(v256)
        K2t = jnp.concatenate([pad_rows(jnp.where(lo128, k128, zk), kw_lanes),
                               pad_rows(jnp.where(lo128, zk, k128), kw_lanes)], 0)
        V2 = jnp.concatenate([pad_rows(jnp.where(lo256, v256, zv), kw_lanes),
                              pad_rows(jnp.where(lo256, zv, v256), kw_lanes)], 0)
        S = _dot_nt(q128, K2t)
        if wide:
            crow = jnp.concatenate([XT[he:he + 1, :], XT[ho:ho + 1, :]], 1)
            arg = jnp.where(causal, crow - pair256(Gc), -jnp.inf)
        else:
            crow = jnp.where(lo128, XT[he:he + 1, :], XT[ho:ho + 1, :])
            arg = jnp.where(causal, crow - pair128(Gc), -jnp.inf)
        Sw = (S * jnp.exp(arg)).astype(BF16)
        num_intra = _dot(Sw, V2)
        rowsum = _dot(Sw, J)
        C2 = c2_s[j]
        npair = n_s[j:j + 1, :]
        Nrow2 = jnp.concatenate(
            [jnp.broadcast_to(jnp.where(lo128, npair, 0.0), (LANES, LANES)),
             jnp.broadcast_to(jnp.where(lo128, 0.0, npair), (LANES, LANES))], 0).astype(BF16)
        if scale_q:
            qa = (q128.astype(F32) * pair128(A)).astype(BF16)
            num = _dot(qa, C2.astype(BF16)) + num_intra
            den = _dot_nt(qa, Nrow2) + rowsum
        else:
            a256 = pair256(A)
            num = a256 * _dot(q128, C2.astype(BF16)) + num_intra
            den = a256 * _dot_nt(q128, Nrow2) + rowsum
        h = num / jnp.maximum(jnp.abs(den), pair256(EXPM))
        h_ref[:, p256] = h.astype(h_ref.dtype)

        kw = k128.astype(F32) * pair128(WST)
        n_s[j:j + 1, :] = pair128(ast) * npair + jnp.sum(kw, axis=0, keepdims=True)
        kwT = pad_rows(kw, LANES).T.astype(BF16)
        dC = _dot(kwT, pad_rows(v256, LANES))
        c2_s[j] = pair256(ast) * C2 + jnp.where(blockdiag, dC, 0.0)

    m_s[...] = mnew


def _mlstm_store_state(cout_ref, nout_ref, mout_ref, c2_s, n_s, m_s):
    for j in range(M_HEADS // 2):
        C2 = c2_s[j]
        cout_ref[2 * j] = C2[0:M_DK, 0:M_DV]
        cout_ref[2 * j + 1] = C2[M_DK:2 * M_DK, M_DV:2 * M_DV]
    nout_ref[...] = n_s[...]
    mout_ref[...] = m_s[...]


MLSTM_SEQS_PER_STEP = 4


def _mlstm_call(qk, v, li, lf, c0, n0, m0, B, nc, L):
    npairs = M_HEADS // 2
    nseq = min(B, MLSTM_SEQS_PER_STEP)
    T = nc * L
    tok3 = lambda a: a.reshape(B, T, a.shape[-1])
    tok = lambda b, c: (b, c, 0)
    st4 = lambda b, c: (b, 0, 0, 0)
    st3 = lambda b, c: (b, 0, 0)
    h, c_new, n_new, m_new = pl.pallas_call(
        functools.partial(_mlstm_kernel, L, nseq, nc == 1),
        out_shape=(jax.ShapeDtypeStruct((B, T, V_COLS), BF16),
                   jax.ShapeDtypeStruct((B, M_HEADS, M_DK, M_DV), F32),
                   jax.ShapeDtypeStruct((B, npairs, LANES), F32),
                   jax.ShapeDtypeStruct((B, 1, LANES), F32)),
        grid=(B // nseq, nc),
        in_specs=[pl.BlockSpec((nseq, L, QK_COLS), tok), pl.BlockSpec((nseq, L, V_COLS), tok),
                  pl.BlockSpec((nseq, L, LANES), tok), pl.BlockSpec((nseq, L, LANES), tok),
                  pl.BlockSpec((nseq, M_HEADS, M_DK, M_DV), st4),
                  pl.BlockSpec((nseq, npairs, LANES), st3),
                  pl.BlockSpec((nseq, 1, LANES), st3)],
        out_specs=(pl.BlockSpec((nseq, L, V_COLS), tok),
                   pl.BlockSpec((nseq, M_HEADS, M_DK, M_DV), st4),
                   pl.BlockSpec((nseq, npairs, LANES), st3),
                   pl.BlockSpec((nseq, 1, LANES), st3)),
        scratch_shapes=[pltpu.VMEM((nseq, npairs, 2 * M_DK, 2 * M_DV), F32),
                        pltpu.VMEM((nseq, npairs, LANES), F32),
                        pltpu.VMEM((nseq, 1, LANES), F32)],
        compiler_params=_params(2),
        name="mlstm_chunks",
    )(tok3(qk), tok3(v), tok3(li), tok3(lf), c0, n0, m0)
    return h.reshape(B * T, V_COLS), c_new, n_new, m_new


def _route(lg):
    lane = lax.broadcasted_iota(jnp.int32, lg.shape, 1)
    lanef = lane.astype(F32)
    neg = -jnp.inf
    far = float(LANES)
    gm = lane < N_GROUPS
    lgm = jnp.where(gm, lg, neg)
    gmax = jnp.max(lgm, axis=-1, keepdims=True)
    gsum = jnp.sum(jnp.exp(lgm - gmax), axis=-1, keepdims=True)
    g_w = 1.0 / gsum
    gidx = jnp.min(jnp.where(gm & (lg == gmax), lanef, far), axis=-1, keepdims=True)
    first = N_GROUPS + EXPERTS_PER_GROUP * gidx
    sel = (lanef >= first) & (lanef < first + EXPERTS_PER_GROUP)
    l1 = jnp.max(jnp.where(sel, lg, neg), axis=-1, keepdims=True)
    i1 = jnp.min(jnp.where(sel & (lg == l1), lanef, far), axis=-1, keepdims=True)
    sel2 = sel & (lanef != i1)
    l2 = jnp.max(jnp.where(sel2, lg, neg), axis=-1, keepdims=True)
    i2 = jnp.min(jnp.where(sel2 & (lg == l2), lanef, far), axis=-1, keepdims=True)
    r = jnp.exp(l2 - l1)
    w1 = g_w / (1.0 + r)
    w2 = w1 * r
    return jnp.where(lanef == i1, w1, jnp.where(lanef == i2, w2, 0.0)), gidx


def _post_kernel(mlstm, routed, expand, *refs):
    refs = list(refs)
    h_ref = refs.pop(0)
    o_ref = refs.pop(0) if mlstm else None
    x_ref, g1_ref, sh2_ref, sc2_ref, nf_ref = refs[:5]
    refs = refs[5:]
    hn_ref = refs.pop(0) if mlstm else None
    wout_ref, wrh_ref, wrl_ref, br_ref = refs[:4]
    refs = refs[4:]
    tri_ref = refs.pop(0) if routed else None
    rep_ref = refs.pop(0) if expand else None
    x1_ref, u2_ref, gate_ref = refs[:3]
    outs = refs[3:]
    if mlstm:
        hf = h_ref[...].astype(F32)
        parts = [_unit_rms(hf[:, M_DV * i:M_DV * (i + 1)]) for i in range(M_HEADS)]
        hn = jnp.concatenate(parts, 1) * hn_ref[...]
        hg = (hn * jax.nn.sigmoid(o_ref[...].astype(F32))).astype(BF16)
    else:
        hg = h_ref[...]
    x1 = x_ref[...] + _mod_rows(g1_ref, rep_ref) * _dot(hg, wout_ref[...])
    x1_ref[...] = x1
    u2 = _unit_rms(x1) * nf_ref[...] * (1.0 + _mod_rows(sc2_ref, rep_ref)) + _mod_rows(sh2_ref, rep_ref)
    uh, ul = _split2(u2)
    lg = _dot(uh, wrh_ref[...]) + _dot(ul, wrh_ref[...]) + _dot(uh, wrl_ref[...]) + br_ref[...]
    gate, gidx = _route(lg)
    u2_ref[...] = uh
    gate_ref[...] = gate
    if not routed:
        return
    gid_ref, col_ref = outs
    rows = []
    for blk in range(x1.shape[0] // LANES):
        col = jnp.broadcast_to(gidx[LANES * blk:LANES * (blk + 1), :], (LANES, LANES))
        rows.append(col.T[0:1, :])
    gid_ref[...] = jnp.concatenate(rows, 1)
    lanef = lax.broadcasted_iota(jnp.int32, gate.shape, 1).astype(F32)
    onehot = jnp.where(lanef == gidx, 1.0, 0.0)
    earlier = _dot(tri_ref[...], onehot.astype(BF16))
    rank = jnp.sum(onehot * earlier, axis=-1, keepdims=True)
    col_ref[...] = jnp.where(lanef == 0.0, gidx, jnp.where(lanef == 1.0, rank, 0.0))


def _post_call(mlstm, routed, h, o, x, mod, mspec, rep, nf, hn, wout, wrh, wrl, br, tm):
    n = x.shape[0]
    expand = rep is not None
    row = lambda i: (i, 0)
    const = lambda i: (0, 0)
    tok = pl.BlockSpec((tm, D_MODEL), row)
    small = pl.BlockSpec((tm, LANES), row)
    vec = pl.BlockSpec((1, D_MODEL), const)
    ins = [h] + ([o] if mlstm else []) + [x, mod, mod, mod, nf] + ([hn] if mlstm else []) \
        + [wout, wrh, wrl, br]
    specs = [tok] + ([tok] if mlstm else []) + [tok, mspec(2), mspec(3), mspec(4), vec] \
        + ([vec] if mlstm else []) \
        + [pl.BlockSpec((D_MODEL, D_MODEL), const), pl.BlockSpec((D_MODEL, LANES), const),
           pl.BlockSpec((D_MODEL, LANES), const), pl.BlockSpec((1, LANES), const)]
    out_shape = [jax.ShapeDtypeStruct((n, D_MODEL), F32), jax.ShapeDtypeStruct((n, D_MODEL), BF16),
                 jax.ShapeDtypeStruct((n, LANES), F32)]
    out_specs = [tok, tok, small]
    if routed:
        t = jnp.arange(tm)
        ins.append((t[None, :] < t[:, None]).astype(BF16))
        specs.append(pl.BlockSpec((tm, tm), const))
        out_shape += [jax.ShapeDtypeStruct((n // tm, 1, tm), F32), jax.ShapeDtypeStruct((n, LANES), F32)]
        out_specs += [pl.BlockSpec((None, 1, tm), lambda i: (i, 0, 0)), small]
    if expand:
        ins.append(rep)
        specs.append(pl.BlockSpec(rep.shape, const))
    out_shape, out_specs = tuple(out_shape), tuple(out_specs)
    return pl.pallas_call(
        functools.partial(_post_kernel, mlstm, routed, expand),
        out_shape=out_shape,
        grid=(n // tm,),
        in_specs=specs,
        out_specs=out_specs,
        compiler_params=_params(1),
        name="post_mlstm" if mlstm else "post_attn",
    )(*ins)


ITEM_ROWS = 8
RUN_ALIGN = 16
RUN_SIZES = (512, 256, 128, 64, 32, 16)
RUN_SRC, RUN_DST, RUN_LEN, RUN_TOTAL = 0, N_GROUPS, 2 * N_GROUPS, 3 * N_GROUPS


def _stage_rows(tm):
    assert N_GROUPS * (RUN_ALIGN - 1) <= LANES
    return tm + LANES


def _sorted_rows(n, tm):
    rows = n + N_GROUPS * (n // tm) * RUN_ALIGN + tm
    return -(-rows // tm) * tm


def _sort_meta_kernel(tm, nts, gid_ref, q_ref, runs_ref, items_ref):
    ntiles = gid_ref.shape[0]
    gid = gid_ref[...]
    r = lax.broadcasted_iota(jnp.int32, (tm, tm), 0)
    c = lax.broadcasted_iota(jnp.int32, (tm, tm), 1)
    before = jnp.where(r < c, 1.0, 0.0).astype(BF16)
    trow = lax.broadcasted_iota(jnp.int32, (ntiles, LANES), 0)
    lane = lax.broadcasted_iota(jnp.int32, (ntiles, LANES), 1)
    k = lax.broadcasted_iota(jnp.int32, (1, LANES), 1).astype(F32)
    zero11 = jnp.zeros((1, 1), F32)
    q = jnp.zeros((ntiles, tm), F32)
    runs = jnp.zeros((ntiles, LANES), F32)
    src = jnp.zeros((ntiles, 1), F32)
    start, nitems = zero11, zero11
    grp = jnp.zeros((1, LANES), F32)
    tile = jnp.zeros((1, LANES), F32)
    valid = jnp.zeros((1, LANES), F32)
    for g in range(N_GROUPS):
        mask = jnp.where(gid == float(g), 1.0, 0.0)
        cnt = jnp.sum(mask, axis=1, keepdims=True)
        padded = jnp.floor((cnt + (RUN_ALIGN - 1.0)) * (1.0 / RUN_ALIGN)) * RUN_ALIGN
        incl = jnp.broadcast_to(padded, (ntiles, LANES))
        d = 1
        while d < ntiles:
            incl = incl + jnp.where(trow >= d, pltpu.roll(incl, d, axis=0), 0.0)
            d *= 2
        total = incl[ntiles - 1:ntiles, 0:1]
        dst = start + incl[:, 0:1] - padded
        q = q + mask * (src + _dot(mask.astype(BF16), before))
        runs = runs + jnp.where(lane == RUN_SRC + g, src, 0.0) + jnp.where(lane == RUN_DST + g, dst, 0.0) \
            + jnp.where(lane == RUN_LEN + g, padded, 0.0)
        src = src + padded
        end = start + total
        ft = jnp.floor(start * (1.0 / tm))
        lt = jnp.floor((end - 1.0) * (1.0 / tm))
        ni = jnp.where(total > 0.0, lt - ft + 1.0, 0.0)
        inside = (k >= nitems) & (k < nitems + ni)
        grp = grp + jnp.where(inside, float(g), 0.0)
        tile = tile + jnp.where(inside, ft + (k - nitems), 0.0)
        valid = valid + jnp.where(inside, 1.0, 0.0)
        nitems = nitems + ni
        start = end
    runs = runs + jnp.where(lane == RUN_TOTAL, start, 0.0)
    live = valid > 0.0
    prev = pltpu.roll(tile, 1, axis=1)
    nxt = pltpu.roll(tile, LANES - 1, axis=1)
    first = live & ((k == 0.0) | (tile != prev))
    last = live & ((k == nitems - 1.0) | (tile != nxt))
    spare = jnp.floor((start - 1.0) * (1.0 / tm)) + 1.0 + (k - nitems)
    fill = jnp.logical_not(live) & (spare <= nts - 1.0)
    tile = jnp.where(live, tile, jnp.minimum(spare, nts - 1.0))
    grp = jnp.where(live, grp, jnp.max(grp, axis=1, keepdims=True))
    flag = lambda m: jnp.where(m, 1.0, 0.0)
    q_ref[...] = q
    runs_ref[...] = runs.astype(jnp.int32)
    table = jnp.concatenate([tile, grp, valid, flag(first | fill), flag(last | fill),
                             jnp.zeros((ITEM_ROWS - 5, LANES), F32)], 0)
    items_ref[...] = table.astype(jnp.int32)


def _sort_meta_call(gid, tm):
    ntiles = gid.shape[0]
    nts = _sorted_rows(ntiles * tm, tm) // tm
    assert nts + N_GROUPS - 1 <= LANES and tm <= RUN_SIZES[0]
    return pl.pallas_call(
        functools.partial(_sort_meta_kernel, tm, nts),
        out_shape=(jax.ShapeDtypeStruct((ntiles, tm), F32),
                   jax.ShapeDtypeStruct((ntiles, LANES), jnp.int32),
                   jax.ShapeDtypeStruct((ITEM_ROWS, LANES), jnp.int32)),
        compiler_params=_params(0),
        name="moe_sort_meta",
    )(gid)


def _run_copies(runs_ref, i, tile_ref, sorted_hbm, sem, to_sorted):
    pieces = []
    for g in range(N_GROUPS):
        src = runs_ref[i, RUN_SRC + g]
        dst = runs_ref[i, RUN_DST + g]
        length = runs_ref[i, RUN_LEN + g]
        for s in RUN_SIZES:
            def build(src=src, dst=dst, length=length, s=s):
                off = length & (-2 * s)
                a = tile_ref.at[pl.ds(pl.multiple_of(src + off, RUN_ALIGN), s), :]
                b = sorted_hbm.at[pl.ds(pl.multiple_of(dst + off, RUN_ALIGN), s), :]
                return pltpu.make_async_copy(a, b, sem) if to_sorted else pltpu.make_async_copy(b, a, sem)
            pieces.append(((length & s) != 0, build))
    return pieces


def _start(pieces):
    for pred, build in pieces:
        pl.when(pred)(lambda build=build: build().start())


def _wait(pieces):
    for pred, build in pieces:
        pl.when(pred)(lambda build=build: build().wait())


def _start_then_wait(pieces):
    _start(pieces)
    _wait(pieces)


def _scatter_kernel(tm, nts, ntiles, runs_ref, q_ref, u_ref, g_ref, su_hbm, sg_hbm,
                    stu, stg, sem_u, sem_g):
    i = pl.program_id(0)
    slot = i % 2

    def copies(tile, slot):
        return (_run_copies(runs_ref, tile, stu.at[slot], su_hbm, sem_u.at[slot], True)
                + _run_copies(runs_ref, tile, stg.at[slot], sg_hbm, sem_g.at[slot], True))

    r = lax.broadcasted_iota(jnp.int32, (_stage_rows(tm), tm), 0).astype(F32)
    perm = jnp.where(q_ref[...] == r, 1.0, 0.0).astype(BF16)
    stu[slot] = _dot(perm, u_ref[...]).astype(BF16)
    g3 = _split3(g_ref[...])
    stg[slot] = _dot(perm, g3[0]) + _dot(perm, g3[1]) + _dot(perm, g3[2])
    _start(copies(i, slot))

    @pl.when(i > 0)
    def _():
        _wait(copies(i - 1, 1 - slot))

    @pl.when(i == ntiles - 1)
    def _():
        _wait(copies(i, slot))
        stu[slot] = jnp.zeros(stu.shape[1:], stu.dtype)
        stg[slot] = jnp.zeros(stg.shape[1:], stg.dtype)
        total = runs_ref[i, RUN_TOTAL]
        tail = nts * tm - total
        nfull = tail // tm
        pieces = []
        for tile_ref, hbm, sem in ((stu.at[slot], su_hbm, sem_u.at[slot]),
                                   (stg.at[slot], sg_hbm, sem_g.at[slot])):
            for j in range(nts - ntiles):
                def full(j=j, tile_ref=tile_ref, hbm=hbm, sem=sem):
                    dst = pl.multiple_of(total + j * tm, RUN_ALIGN)
                    return pltpu.make_async_copy(tile_ref.at[pl.ds(0, tm), :], hbm.at[pl.ds(dst, tm), :], sem)
                pieces.append((j < nfull, full))
            rem = tail - nfull * tm
            for s in RUN_SIZES:
                if s >= tm:
                    continue
                def part(s=s, tile_ref=tile_ref, hbm=hbm, sem=sem):
                    dst = pl.multiple_of(total + nfull * tm + (rem & (-2 * s)), RUN_ALIGN)
                    return pltpu.make_async_copy(tile_ref.at[pl.ds(0, s), :], hbm.at[pl.ds(dst, s), :], sem)
                pieces.append(((rem & s) != 0, part))
        _start_then_wait(pieces)


def _scatter_call(runs, q3, u2, gate, tm):
    n = u2.shape[0]
    ns = _sorted_rows(n, tm)
    grid_spec = pltpu.PrefetchScalarGridSpec(
        num_scalar_prefetch=1,
        grid=(n // tm,),
        in_specs=[pl.BlockSpec((None, 1, tm), lambda i, *_: (i, 0, 0)),
                  pl.BlockSpec((tm, D_MODEL), lambda i, *_: (i, 0)),
                  pl.BlockSpec((tm, LANES), lambda i, *_: (i, 0))],
        out_specs=(pl.BlockSpec(memory_space=pl.ANY), pl.BlockSpec(memory_space=pl.ANY)),
        scratch_shapes=[pltpu.VMEM((2, _stage_rows(tm), D_MODEL), BF16),
                        pltpu.VMEM((2, _stage_rows(tm), LANES), F32),
                        pltpu.SemaphoreType.DMA((2,)), pltpu.SemaphoreType.DMA((2,))])
    return pl.pallas_call(
        functools.partial(_scatter_kernel, tm, ns // tm, n // tm),
        out_shape=(jax.ShapeDtypeStruct((ns, D_MODEL), BF16), jax.ShapeDtypeStruct((ns, LANES), F32)),
        grid_spec=grid_spec,
        compiler_params=_params(1),
        name="moe_scatter",
    )(runs, q3, u2, gate)


def _ungroup(tm, runs_ref, col_ref, ys_hbm, stage, sem):
    i = pl.program_id(0)
    slot = i % 2

    def copies(tile, slot):
        return _run_copies(runs_ref, tile, stage.at[slot], ys_hbm, sem.at[slot], False)

    @pl.when(i == 0)
    def _():
        stage[...] = jnp.zeros_like(stage)
        _start(copies(i, slot))

    @pl.when(i + 1 < pl.num_programs(0))
    def _():
        _start(copies(i + 1, 1 - slot))

    _wait(copies(i, slot))
    col = col_ref[...]
    gidx = col[:, 0:1]
    src = [runs_ref[i, RUN_SRC + g].astype(F32) for g in range(N_GROUPS)]
    first = src[N_GROUPS - 1]
    for g in range(N_GROUPS - 2, -1, -1):
        first = jnp.where(gidx == float(g), src[g], first)
    q = first + col[:, 1:2]
    lanes = lax.broadcasted_iota(jnp.int32, (tm, _stage_rows(tm)), 1).astype(F32)
    perm_t = jnp.where(q == lanes, 1.0, 0.0).astype(BF16)
    return _dot(perm_t, stage[slot])


def _moe_routed_kernel(items_ref, su_ref, sg_ref, wgu_ref, wd_ref, out_ref, acc_ref, wgu_b, wd_b):
    kk = pl.program_id(0)
    grp = items_ref[1, kk]

    @pl.when((kk == 0) | (grp != items_ref[1, jnp.maximum(kk - 1, 0)]))
    def _():
        for e in range(EXPERTS_PER_GROUP):
            wgu_b[e] = wgu_ref[e].astype(BF16)
            wd_b[e] = wd_ref[e].astype(BF16)

    @pl.when(items_ref[3, kk] == 1)
    def _():
        acc_ref[...] = jnp.zeros_like(acc_ref)

    @pl.when(items_ref[2, kk] == 1)
    def _():
        u = su_ref[...]
        gate = sg_ref[...]
        lane = lax.broadcasted_iota(jnp.int32, gate.shape, 1)
        first_lane = N_GROUPS + EXPERTS_PER_GROUP * grp
        acts = []
        for e in range(EXPERTS_PER_GROUP):
            gcol = jnp.sum(jnp.where(lane == first_lane + e, gate, 0.0), axis=-1, keepdims=True)
            hu = _dot(u, wgu_b[e])
            hg = hu[:, 0:D_EXPERT]
            acts.append((hg * jax.nn.sigmoid(hg) * hu[:, D_EXPERT:2 * D_EXPERT] * gcol).astype(BF16))
        wd_all = wd_b[...].reshape(EXPERTS_PER_GROUP * D_EXPERT, D_MODEL)
        acc_ref[...] += _dot(jnp.concatenate(acts, 1), wd_all)

    @pl.when(items_ref[4, kk] == 1)
    def _():
        out_ref[...] = acc_ref[...].astype(out_ref.dtype)


def _moe_routed_call(items, su, sg, wgu, wd, layer, tm):
    ns = su.shape[0]
    n_items = ns // tm + N_GROUPS - 1
    depth = wgu.shape[0]
    wgu5 = wgu.reshape(depth, N_GROUPS, EXPERTS_PER_GROUP, D_MODEL, 2 * D_EXPERT)
    wd5 = wd.reshape(depth, N_GROUPS, EXPERTS_PER_GROUP, D_EXPERT, D_MODEL)
    grid_spec = pltpu.PrefetchScalarGridSpec(
        num_scalar_prefetch=1,
        grid=(n_items,),
        in_specs=[pl.BlockSpec((tm, D_MODEL), lambda k, it: (it[0, k], 0)),
                  pl.BlockSpec((tm, LANES), lambda k, it: (it[0, k], 0)),
                  pl.BlockSpec((None, None, EXPERTS_PER_GROUP, D_MODEL, 2 * D_EXPERT),
                               lambda k, it: (layer, it[1, k], 0, 0, 0)),
                  pl.BlockSpec((None, None, EXPERTS_PER_GROUP, D_EXPERT, D_MODEL),
                               lambda k, it: (layer, it[1, k], 0, 0, 0))],
        out_specs=pl.BlockSpec((tm, D_MODEL), lambda k, it: (it[0, k], 0)),
        scratch_shapes=[pltpu.VMEM((tm, D_MODEL), F32),
                        pltpu.VMEM((EXPERTS_PER_GROUP, D_MODEL, 2 * D_EXPERT), BF16),
                        pltpu.VMEM((EXPERTS_PER_GROUP, D_EXPERT, D_MODEL), BF16)])
    return pl.pallas_call(
        _moe_routed_kernel,
        out_shape=jax.ShapeDtypeStruct((ns, D_MODEL), BF16),
        grid_spec=grid_spec,
        compiler_params=_params(1),
        name="moe_routed",
    )(items, su, sg, wgu5, wd5)


def _resid_gather_kernel(tm, runs_ref, x_ref, g2_ref, col_ref, ys_hbm, y_ref, stage, sem):
    y_ref[...] = x_ref[...] + g2_ref[...] * _ungroup(tm, runs_ref, col_ref, ys_hbm, stage, sem)


def _resid_gather_call(x, runs, col, ys, mod, mspec, tm):
    n = x.shape[0]
    row = lambda i, *_: (i, 0)
    grid_spec = pltpu.PrefetchScalarGridSpec(
        num_scalar_prefetch=1,
        grid=(n // tm,),
        in_specs=[pl.BlockSpec((tm, D_MODEL), row), mspec(5), pl.BlockSpec((tm, LANES), row),
                  pl.BlockSpec(memory_space=pl.ANY)],
        out_specs=pl.BlockSpec((tm, D_MODEL), row),
        scratch_shapes=[pltpu.VMEM((2, _stage_rows(tm), D_MODEL), BF16), pltpu.SemaphoreType.DMA((2,))])
    return pl.pallas_call(
        functools.partial(_resid_gather_kernel, tm),
        out_shape=jax.ShapeDtypeStruct((n, D_MODEL), F32),
        grid_spec=grid_spec,
        compiler_params=_params(1),
        name="moe_unsort_resid",
    )(runs, x, mod, col, ys)


def _moe_kernel(expand, u_ref, gate_ref, wgu_ref, wd_ref, x1_ref, g2_ref, *rest):
    rep_ref = rest[0] if expand else None
    y_ref, acc_ref = rest[1 if expand else 0:]
    e = pl.program_id(1)

    @pl.when(e == 0)
    def _():
        acc_ref[...] = jnp.zeros_like(acc_ref)

    hu = _dot(u_ref[...], wgu_ref[...].astype(BF16))
    hg = hu[:, 0:D_EXPERT]
    hv = hu[:, D_EXPERT:2 * D_EXPERT]
    gate = gate_ref[...]
    lane = lax.broadcasted_iota(jnp.int32, gate.shape, 1)
    gcol = jnp.sum(jnp.where(lane == N_GROUPS + e, gate, 0.0), axis=-1, keepdims=True)
    act = hg * jax.nn.sigmoid(hg) * hv * gcol
    acc_ref[...] += _dot(act.astype(BF16), wd_ref[...].astype(BF16))

    @pl.when(e == N_EXPERTS - 1)
    def _():
        y_ref[...] = x1_ref[...] + _mod_rows(g2_ref, rep_ref) * acc_ref[...]


def _moe_call(u2, gate, wgu, wd, layer, x1, mod, mspec, rep, tm):
    n = u2.shape[0]
    row = lambda i, e: (i, 0)
    expand = rep is not None
    return pl.pallas_call(
        functools.partial(_moe_kernel, expand),
        out_shape=jax.ShapeDtypeStruct((n, D_MODEL), F32),
        grid=(n // tm, N_EXPERTS),
        in_specs=[pl.BlockSpec((tm, D_MODEL), row), pl.BlockSpec((tm, LANES), row),
                  pl.BlockSpec((None, None, D_MODEL, 2 * D_EXPERT), lambda i, e: (layer, e, 0, 0)),
                  pl.BlockSpec((None, None, D_EXPERT, D_MODEL), lambda i, e: (layer, e, 0, 0)),
                  pl.BlockSpec((tm, D_MODEL), row), mspec(5)]
        + ([pl.BlockSpec(rep.shape, lambda i, e: (0, 0))] if expand else []),
        out_specs=pl.BlockSpec((tm, D_MODEL), row),
        scratch_shapes=[pltpu.VMEM((tm, D_MODEL), F32)],
        compiler_params=_params(2),
        name="moe_dense",
    )(u2, gate, wgu, wd, x1, mod, *([rep] if expand else []))


def _rope128(x, cos, sa, sb):
    return x * cos + pltpu.roll(x, LANES - ROT_DIM // 2, axis=1) * sa \
        + pltpu.roll(x, ROT_DIM // 2, axis=1) * sb


def _kvq_kernel(gather_tm, *refs):
    if gather_tm:
        (runs_ref, x_ref, g2_ref, col_ref, ys_hbm, kvsh_ref, kvsc_ref, sh1_ref, sc1_ref, kvn_ref,
         nm_ref, wkv_ref, wq_ref, kn_ref, qn_ref, cos_ref, sa_ref, sb_ref, g64_ref, gr_ref, gb_ref,
         q_ref, k_ref, v_ref, xa_ref, stage, sem) = refs
        x = x_ref[...] + g2_ref[...] * _ungroup(gather_tm, runs_ref, col_ref, ys_hbm, stage, sem)
        xa_ref[...] = x
        rep_ref = None
    else:
        (x_ref, kvsh_ref, kvsc_ref, sh1_ref, sc1_ref, kvn_ref, nm_ref,
         wkv_ref, wq_ref, kn_ref, qn_ref, cos_ref, sa_ref, sb_ref, g64_ref, gr_ref, gb_ref,
         rep_ref, q_ref, k_ref, v_ref) = refs
        x = x_ref[...]
    xn = _unit_rms(x)
    cos, sa, sb = cos_ref[...], sa_ref[...], sb_ref[...]

    ukv = xn * kvn_ref[...] * (1.0 + _mod_rows(kvsc_ref, rep_ref)) + _mod_rows(kvsh_ref, rep_ref)
    kv = _dot(ukv.astype(BF16), wkv_ref[...])
    k = kv[:, 0:LANES]
    v_ref[...] = kv[:, LANES:2 * LANES]
    kh, kl = _split2(k * k)
    ms = _dot(kh, g64_ref[...]) + _dot(kl, g64_ref[...])
    k_ref[...] = _rope128(k * lax.rsqrt(ms + RMS_EPS) * kn_ref[...], cos, sa, sb)

    u1 = xn * nm_ref[...] * (1.0 + _mod_rows(sc1_ref, rep_ref)) + _mod_rows(sh1_ref, rep_ref)
    q = _dot(u1.astype(BF16), wq_ref[...])
    ms16 = _dot((q * q).astype(BF16), gr_ref[...])
    rh, rl = _split2(lax.rsqrt(ms16 + RMS_EPS))
    rsb = _dot(rh, gb_ref[...]) + _dot(rl, gb_ref[...])
    qn = q * rsb * qn_ref[...]
    for i in range(D_MODEL // LANES):
        sl = slice(LANES * i, LANES * (i + 1))
        q_ref[:, sl] = (_rope128(qn[:, sl], cos, sa, sb) * (HEAD_DIM ** -0.5)).astype(BF16)


def _kvq_call(x, routed, mod0, mod1, mspec, kvmod, kvspec, rep, kvn, nm, wkv, wq, kn, qn, tabs, tab,
              g64, gr, gb, tm):
    n = x.shape[0]
    row = lambda i, *_: (i, 0)
    const = lambda i, *_: (0, 0)
    vec = pl.BlockSpec((1, D_MODEL), const)
    tok = pl.BlockSpec((tm, D_MODEL), row)
    small = pl.BlockSpec((tm, LANES), row)
    ins = [x, kvmod, kvmod, mod1, mod1, kvn, nm, wkv, wq, kn, qn, *tabs, g64, gr, gb]
    specs = [tok, kvspec(0), kvspec(1), mspec(0), mspec(1), vec, vec,
             pl.BlockSpec((D_MODEL, 2 * LANES), const), pl.BlockSpec((D_MODEL, D_MODEL), const),
             pl.BlockSpec((1, LANES), const), vec, tab, tab, tab,
             pl.BlockSpec((LANES, LANES), const), pl.BlockSpec((D_MODEL, LANES), const),
             pl.BlockSpec((LANES, D_MODEL), const)]
    out_shape = [jax.ShapeDtypeStruct((n, D_MODEL), BF16), jax.ShapeDtypeStruct((n, LANES), F32),
                 jax.ShapeDtypeStruct((n, LANES), F32)]
    out_specs = [tok, small, small]
    scratch = []
    nprefetch = 0
    if routed is not None:
        runs, col, ys = routed
        nprefetch = 1
        ins = [runs, x, mod0, col, ys] + ins[1:]
        specs = [tok, mspec(5), small, pl.BlockSpec(memory_space=pl.ANY)] + specs[1:]
        out_shape.append(jax.ShapeDtypeStruct((n, D_MODEL), F32))
        out_specs.append(tok)
        scratch = [pltpu.VMEM((2, _stage_rows(tm), D_MODEL), BF16), pltpu.SemaphoreType.DMA((2,))]
    else:
        ins.append(rep)
        specs.append(pl.BlockSpec(rep.shape, const))
    grid_spec = pltpu.PrefetchScalarGridSpec(
        num_scalar_prefetch=nprefetch, grid=(n // tm,), in_specs=specs,
        out_specs=tuple(out_specs), scratch_shapes=scratch)
    return pl.pallas_call(
        functools.partial(_kvq_kernel, tm if routed is not None else 0),
        out_shape=tuple(out_shape),
        grid_spec=grid_spec,
        compiler_params=_params(1),
        name="kv_q_proj",
    )(*ins)


def _attn_core(q, kcat, vcat, bias, sinks_ref, o_ref):
    tk = kcat.shape[0]
    pairs = N_Q_HEADS // N_KV_HEADS // 2
    lane = lax.broadcasted_iota(jnp.int32, (1, LANES), 1)
    lo = lane < HEAD_DIM
    kro = pltpu.roll(kcat, HEAD_DIM, axis=1)
    vro = pltpu.roll(vcat, HEAD_DIM, axis=1)
    one_e = jnp.broadcast_to(jnp.where(lo, 1.0, 0.0), (tk, LANES))
    one_o = 1.0 - one_e
    for g in range(N_KV_HEADS):
        if g == 0:
            ke, ko = jnp.where(lo, kcat, 0.0), jnp.where(lo, 0.0, kro)
            ve, vo = jnp.where(lo, vcat, 0.0), jnp.where(lo, 0.0, vro)
        else:
            ke, ko = jnp.where(lo, kro, 0.0), jnp.where(lo, 0.0, kcat)
            ve, vo = jnp.where(lo, vro, 0.0), jnp.where(lo, 0.0, vcat)
        k2 = jnp.concatenate([ke, ko], 0).astype(BF16)
        v2 = jnp.concatenate([jnp.concatenate([ve, one_e], 1),
                              jnp.concatenate([vo, one_o], 1)], 0).astype(BF16)
        for p in range(pairs):
            hp = g * pairs + p
            s = _dot_nt(q[:, LANES * hp:LANES * (hp + 1)], k2)
            halves, corr = [], []
            for par in range(2):
                sp = s[:, par * tk:(par + 1) * tk] + bias
                sink = sinks_ref[2 * hp + par]
                m = jnp.maximum(jnp.max(sp, axis=-1, keepdims=True), sink)
                halves.append(jnp.exp(sp - m).astype(BF16))
                corr.append(jnp.exp(sink - m))
            o2 = _dot(jnp.concatenate(halves, 1), v2)
            den = o2[:, LANES:2 * LANES] + jnp.where(lo, corr[0], corr[1])
            o_ref[:, LANES * hp:LANES * (hp + 1)] = (o2[:, 0:LANES] / den).astype(o_ref.dtype)


def _attn_prompt_kernel(sinks_ref, q_ref, kp_ref, kc_ref, vp_ref, vc_ref, bias_ref, o_ref):
    kcat = jnp.concatenate([kp_ref[...], kc_ref[...]], 0)
    vcat = jnp.concatenate([vp_ref[...], vc_ref[...]], 0)
    _attn_core(q_ref[...], kcat, vcat, bias_ref[...], sinks_ref, o_ref)


ATTN_SEQS_PER_STEP = 8


def _attn_sample_kernel(sinks_ref, q_ref, kc_ref, kn_ref, vc_ref, vn_ref, bias_ref, o_ref, kbuf, vbuf):
    @pl.when(pl.program_id(0) == 0)
    def _():
        kbuf[...] = jnp.zeros_like(kbuf)
        vbuf[...] = jnp.zeros_like(vbuf)

    for b in range(ATTN_SEQS_PER_STEP):
        kbuf[b, 0:WINDOW, :] = kc_ref[b]
        kbuf[b, WINDOW:WINDOW + SAMPLE_PAD, :] = kn_ref[b]
        vbuf[b, 0:WINDOW, :] = vc_ref[b]
        vbuf[b, WINDOW:WINDOW + SAMPLE_PAD, :] = vn_ref[b]
    for b in range(ATTN_SEQS_PER_STEP):
        _attn_core(q_ref[b], kbuf[b], vbuf[b], bias_ref[...], sinks_ref, o_ref.at[b])


def _window_bias(tq, first):
    qi = jnp.arange(tq)[:, None]
    kj = jnp.arange(2 * WINDOW)[None, :]
    ok = (kj > qi) & (kj <= qi + WINDOW)
    if first:
        ok = ok & (kj >= WINDOW)
    return jnp.where(ok, 0.0, -jnp.inf).astype(F32)


def _attn_prompt_call(sinks, q, k, v, B, nb):
    n = q.shape[0]
    cur = lambda b, i: (b * nb + i, 0)
    prev = lambda b, i: (b * nb + jnp.maximum(i - 1, 0), 0)
    kvb = lambda im: pl.BlockSpec((WINDOW, LANES), im)
    bias = jnp.stack([_window_bias(WINDOW, True), _window_bias(WINDOW, False)])
    return pl.pallas_call(
        _attn_prompt_kernel,
        out_shape=jax.ShapeDtypeStruct((n, D_MODEL), BF16),
        grid=(B, nb),
        in_specs=[pl.BlockSpec(memory_space=pltpu.SMEM),
                  pl.BlockSpec((WINDOW, D_MODEL), cur), kvb(prev), kvb(cur), kvb(prev), kvb(cur),
                  pl.BlockSpec((None, WINDOW, 2 * WINDOW), lambda b, i: (jnp.minimum(i, 1), 0, 0))],
        out_specs=pl.BlockSpec((WINDOW, D_MODEL), cur),
        compiler_params=_params(2),
        name="swa_prompt",
    )(sinks, q, k, k, v, v, bias)


def _attn_sample_call(sinks, q, kcache, knew, vcache, vnew):
    B = q.shape[0]
    nseq = ATTN_SEQS_PER_STEP
    b3 = lambda b: (b, 0, 0)
    cache = pl.BlockSpec((nseq, WINDOW, LANES), b3)
    new = pl.BlockSpec((nseq, SAMPLE_PAD, LANES), b3)
    return pl.pallas_call(
        _attn_sample_kernel,
        out_shape=jax.ShapeDtypeStruct(q.shape, BF16),
        grid=(B // nseq,),
        in_specs=[pl.BlockSpec(memory_space=pltpu.SMEM),
                  pl.BlockSpec((nseq, SAMPLE_PAD, D_MODEL), b3), cache, new, cache, new,
                  pl.BlockSpec((SAMPLE_PAD, 2 * WINDOW), lambda b: (0, 0))],
        out_specs=pl.BlockSpec((nseq, SAMPLE_PAD, D_MODEL), b3),
        scratch_shapes=[pltpu.VMEM((nseq, 2 * WINDOW, LANES), F32),
                        pltpu.VMEM((nseq, 2 * WINDOW, LANES), F32)],
        compiler_params=_params(1),
        name="swa_sample",
    )(sinks, q, kcache, knew, vcache, vnew, _window_bias(SAMPLE_PAD, False))


def _rope_tables(pos):
    half = ROT_DIM // 2
    inv = ROPE_THETA ** (-jnp.arange(half, dtype=F32) / half)
    ang = pos.astype(F32)[:, None] * inv[None]
    cos, sin = jnp.cos(ang), jnp.sin(ang)
    d = jnp.arange(LANES) % HEAD_DIM
    idx = d % half
    cos_t = jnp.where(d < ROT_DIM, cos[:, idx], 1.0)
    sa = jnp.where(d < half, -sin[:, idx], 0.0)
    sb = jnp.where((d >= half) & (d < ROT_DIM), sin[:, idx], 0.0)
    return cos_t, sa, sb


def _pad_lanes(a, value=0.0):
    return jnp.pad(a, ((0, 0), (0, LANES - a.shape[1])), constant_values=value)


def _prep_weights(ada_w, ada_b, norm_mix, norm_ffn, a_w_in, a_b_gates, a_head_norm, a_w_out,
                  kv_ada_w, kv_ada_b, kv_norm, w_k, w_v, k_norm, b_w_q, b_q_norm, b_sinks, b_w_o,
                  moe_w_group, moe_b_group, moe_w_expert, moe_b_expert, moe_w_gate_up, moe_w_down):
    w = {}
    g0 = QK_COLS + 2 * V_COLS
    w["w_in"] = a_w_in
    w["w_gates"] = jnp.concatenate([_pad_lanes(a_w_in[0, :, g0:g0 + M_HEADS]),
                                    _pad_lanes(a_w_in[0, :, g0 + M_HEADS:])], 1)
    w["bli"] = _pad_lanes(a_b_gates[0][None, :M_HEADS])
    w["blf"] = _pad_lanes(a_b_gates[0][None, M_HEADS:])
    w["head_norm"] = a_head_norm[0][None]
    w["w_out"] = a_w_out[0].astype(BF16)
    w["norm_mix"] = [norm_mix[l][None] for l in range(2)]
    w["norm_ffn"] = [norm_ffn[l][None] for l in range(2)]
    w["router"] = []
    for l in range(2):
        wr = _pad_lanes(jnp.concatenate([moe_w_group[l], moe_w_expert[l]], 1))
        hi = wr.astype(BF16)
        lo = (wr - hi.astype(F32)).astype(BF16)
        br = _pad_lanes(jnp.concatenate([moe_b_group[l], moe_b_expert[l]])[None])
        w["router"].append((hi, lo, br))
    w["w_gu"] = moe_w_gate_up
    w["w_d"] = moe_w_down
    w["kv_norm"] = kv_norm[None]
    w["w_kv"] = jnp.concatenate([w_k, w_v], 1).astype(BF16)
    w["k_norm"] = jnp.tile(k_norm, N_KV_HEADS)[None]
    w["w_q"] = b_w_q[0].astype(BF16)
    w["q_norm"] = jnp.tile(b_q_norm[0], N_Q_HEADS)[None]
    w["sinks"] = b_sinks[0]
    w["w_o"] = b_w_o[0].astype(BF16)
    lanes = jnp.arange(LANES)
    feat = jnp.arange(D_MODEL)
    w["g64"] = jnp.where((lanes[:, None] // HEAD_DIM) == (lanes[None, :] // HEAD_DIM),
                         1.0 / HEAD_DIM, 0.0).astype(BF16)
    w["gr"] = jnp.where((feat[:, None] // HEAD_DIM) == lanes[None, :], 1.0 / HEAD_DIM, 0.0).astype(BF16)
    w["gb"] = jnp.where(lanes[:, None] == (feat[None, :] // HEAD_DIM), 1.0, 0.0).astype(BF16)
    return w


def _trunk(x2, mods, kvmod, mspec, mspec_moe, rep, w, *, B, T, L, tm, tm_moe, tabs, tabspec, c0, n0, m0,
           cache_k=None, cache_v=None):
    sample = cache_k is not None
    nc = T // L if not sample else 1

    qk, v, o, li, lf = _inproj_call(x2, mods[0], mspec, rep, w["norm_mix"][0], w["w_in"],
                                    w["w_gates"], w["bli"], w["blf"], tm)
    if sample:
        def padtok(a, value=0.0):
            a = a.reshape(B, T, a.shape[-1])
            a = jnp.pad(a, ((0, 0), (0, L - T), (0, 0)), constant_values=value)
            return a.reshape(B * L, a.shape[-1])
        qk, v, li, lf = padtok(qk), padtok(v), padtok(li, M_EMPTY), padtok(lf)
    h, c_new, n_new, m_new = _mlstm_call(qk, v, li, lf, c0, n0, m0, B, nc, L)
    if sample:
        h = h.reshape(B, L, V_COLS)[:, :T].reshape(B * T, V_COLS)
    routed = not sample
    ntiles = (B * T) // tm

    def routed_moe(u2, gate, gid, col, layer):
        q, runs, items = _sort_meta_call(gid.reshape(ntiles, tm), tm)
        su, sg = _scatter_call(runs, q.reshape(ntiles, 1, tm), u2, gate, tm)
        ys = _moe_routed_call(items, su, sg, w["w_gu"], w["w_d"], layer, tm)
        return runs, col, ys

    rh, rl, br = w["router"][0]
    post0 = _post_call(True, routed, h, o, x2, mods[0], mspec, rep, w["norm_ffn"][0], w["head_norm"],
                       w["w_out"], rh, rl, br, tm)
    kvq_args = (mods[0], mods[1], mspec, kvmod, mspec, rep, w["kv_norm"], w["norm_mix"][1],
                w["w_kv"], w["w_q"], w["k_norm"], w["q_norm"], tabs, tabspec,
                w["g64"], w["gr"], w["gb"], tm)

    if routed:
        x1, u2, gate, gid, col = post0
        q, k, vv, xa = _kvq_call(x1, routed_moe(u2, gate, gid, col, 0), *kvq_args)
    else:
        x1, u2, gate = post0
        xa = _moe_call(u2, gate, w["w_gu"], w["w_d"], 0, x1, mods[0], mspec_moe, rep, tm_moe)
        q, k, vv = _kvq_call(xa, None, *kvq_args)
    if not sample:
        att = _attn_prompt_call(w["sinks"], q, k, vv, B, T // WINDOW)
        k_win = k.reshape(B, T, LANES)[:, T - WINDOW:].reshape(B, WINDOW, N_KV_HEADS, HEAD_DIM)
        v_win = vv.reshape(B, T, LANES)[:, T - WINDOW:].reshape(B, WINDOW, N_KV_HEADS, HEAD_DIM)
    else:
        def padseq(a):
            return jnp.pad(a.reshape(B, T, a.shape[-1]), ((0, 0), (0, SAMPLE_PAD - T), (0, 0)))
        kc = cache_k.reshape(B, WINDOW, LANES)
        vc = cache_v.reshape(B, WINDOW, LANES)
        att = _attn_sample_call(w["sinks"], padseq(q), kc, padseq(k), vc, padseq(vv))
        att = att[:, :T].reshape(B * T, D_MODEL)
        k_win = jnp.concatenate([kc[:, T:], k.reshape(B, T, LANES)], 1)
        v_win = jnp.concatenate([vc[:, T:], vv.reshape(B, T, LANES)], 1)
        k_win = k_win.reshape(B, WINDOW, N_KV_HEADS, HEAD_DIM)
        v_win = v_win.reshape(B, WINDOW, N_KV_HEADS, HEAD_DIM)
    rh, rl, br = w["router"][1]
    post1 = _post_call(False, routed, att, None, xa, mods[1], mspec, rep, w["norm_ffn"][1], None,
                       w["w_o"], rh, rl, br, tm)
    if routed:
        x3, u4, gate, gid, col = post1
        runs, col, ys = routed_moe(u4, gate, gid, col, 1)
        y = _resid_gather_call(x3, runs, col, ys, mods[1], mspec, tm)
    else:
        x3, u4, gate = post1
        y = _moe_call(u4, gate, w["w_gu"], w["w_d"], 1, x3, mods[1], mspec_moe, rep, tm_moe)
    c_out = c_new[None]
    n_out = n_new.reshape(1, B, M_HEADS, M_DK)
    m_out = m_new[:, 0, :M_HEADS][None]
    return y, c_out, n_out, m_out, k_win, v_win


def kernel(x_prompt, x_sample, c_prompt, c_sample, state_c, state_n, state_m, cache_k_win, cache_v_win, ada_w, ada_b, norm_mix, norm_ffn, a_w_in, a_b_gates, a_head_norm, a_w_out, kv_ada_w, kv_ada_b, kv_norm, w_k, w_v, k_norm, b_w_q, b_q_norm, b_sinks, b_w_o, moe_w_group, moe_b_group, moe_w_expert, moe_b_expert, moe_w_gate_up, moe_w_down):
    Bp, Tp, D = x_prompt.shape
    Bs, Ts, _ = x_sample.shape
    w = _prep_weights(ada_w, ada_b, norm_mix, norm_ffn, a_w_in, a_b_gates, a_head_norm, a_w_out,
                      kv_ada_w, kv_ada_b, kv_norm, w_k, w_v, k_norm, b_w_q, b_q_norm, b_sinks, b_w_o,
                      moe_w_group, moe_b_group, moe_w_expert, moe_b_expert, moe_w_gate_up, moe_w_down)

    rows = Bp + Bs
    rpad = -rows % 8
    c_all = jnp.concatenate([c_prompt, c_sample, jnp.zeros((rpad, D), F32)], 0)
    mod = _ada_call(c_all, ada_w, ada_b[:, None, :])
    kvm = _ada_call(c_all, kv_ada_w[None], kv_ada_b[None, None, :])

    tm_p = 512
    tiles_per_seq = Tp // tm_p
    mods_p = [mod[l, :Bp][:, None, :] for l in range(2)]
    kvmod_p = kvm[0, :Bp][:, None, :]

    def mspec_p(col):
        return pl.BlockSpec((None, 1, D_MODEL), lambda i, *_: (i // tiles_per_seq, 0, col))

    tm_moe_p = 1024
    moe_tiles_per_seq = Tp // tm_moe_p

    def mspec_moe_p(col):
        return pl.BlockSpec((None, 1, D_MODEL), lambda i, *_: (i // moe_tiles_per_seq, 0, col))

    tabs_p = _rope_tables(jnp.arange(Tp, dtype=jnp.int32))
    tabspec_p = pl.BlockSpec((tm_p, LANES), lambda i, *_: (i % tiles_per_seq, 0))
    npairs = M_HEADS // 2
    c0 = jnp.zeros((Bp, M_HEADS, M_DK, M_DV), F32)
    n0 = jnp.zeros((Bp, npairs, LANES), F32)
    m0 = jnp.pad(jnp.full((Bp, 1, M_HEADS), M_EMPTY, F32), ((0, 0), (0, 0), (0, LANES - M_HEADS)))
    yp, cp, np_, mp, kwp, vwp = _trunk(
        x_prompt.reshape(Bp * Tp, D), mods_p, kvmod_p, mspec_p, mspec_moe_p, None, w,
        B=Bp, T=Tp, L=M_CHUNK, tm=tm_p, tm_moe=tm_moe_p, tabs=tabs_p, tabspec=tabspec_p,
        c0=c0, n0=n0, m0=m0)

    ns = Bs * Ts
    mods_s = [mod[l, Bp:Bp + Bs][None] for l in range(2)]
    kvmod_s = kvm[0, Bp:Bp + Bs][None]
    rep = (jnp.arange(ns)[:, None] // Ts == jnp.arange(Bs)[None, :]).astype(BF16)

    def mspec_s(col):
        return pl.BlockSpec((None, Bs, D_MODEL), lambda i, *_: (0, 0, col))

    tabs_s = _rope_tables(PAST_LEN + jnp.arange(Ts, dtype=jnp.int32))
    tabs_s = tuple(jnp.tile(t, (Bs, 1)) for t in tabs_s)
    m0s = jnp.pad(state_m[0][:, None, :], ((0, 0), (0, 0), (0, LANES - M_HEADS)))
    ys, cs, ns_, ms, kws, vws = _trunk(
        x_sample.reshape(ns, D), mods_s, kvmod_s, mspec_s, mspec_s, rep, w,
        B=Bs, T=Ts, L=SAMPLE_PAD, tm=ns, tm_moe=ns, tabs=tabs_s,
        tabspec=pl.BlockSpec((ns, LANES), lambda i, *_: (0, 0)),
        c0=state_c[0], n0=state_n[0].reshape(Bs, npairs, LANES), m0=m0s,
        cache_k=cache_k_win, cache_v=cache_v_win)

    return (yp.reshape(Bp, Tp, D), ys.reshape(Bs, Ts, D), cp, np_, mp, kwp, vwp,
            cs, ns_, ms, kws, vws)
```

```python
import functools

import jax
import jax.numpy as jnp
from jax import lax
from jax.experimental import pallas as pl
from jax.experimental.pallas import tpu as pltpu

F32 = jnp.float32
BF16 = jnp.bfloat16

D_MODEL = 1024
PAST_LEN = 8192
M_HEADS = 8
M_DK = 64
M_DV = 128
M_CHUNK = 128
GATE_SOFTCAP = 15.0
M_EMPTY = -1e30
WINDOW = 128
HEAD_DIM = 64
N_Q_HEADS = 16
N_KV_HEADS = 2
ROPE_THETA = 500000.0
ROT_DIM = 16
N_GROUPS = 4
EXPERTS_PER_GROUP = 4
N_EXPERTS = 16
D_EXPERT = 256
RMS_EPS = 1e-6

LANES = 128
QK_COLS = 2 * M_HEADS * M_DK
V_COLS = M_HEADS * M_DV
IN_COLS = QK_COLS + 2 * V_COLS + 2 * LANES
SAMPLE_PAD = 16
VMEM_LIMIT = 52 * 1024 * 1024

NT_DIMS = (((1,), (1,)), ((), ()))


def _params(n_axes):
    return pltpu.CompilerParams(dimension_semantics=("arbitrary",) * n_axes,
                                vmem_limit_bytes=VMEM_LIMIT)


def _dot(a, b):
    return jnp.dot(a, b, preferred_element_type=F32)


def _dot_nt(a, b):
    return lax.dot_general(a, b, NT_DIMS, preferred_element_type=F32)


def _split2(x):
    hi = x.astype(BF16)
    lo = (x - hi.astype(F32)).astype(BF16)
    return hi, lo


def _split3(x):
    hi = x.astype(BF16)
    r = x - hi.astype(F32)
    mid = r.astype(BF16)
    lo = (r - mid.astype(F32)).astype(BF16)
    return hi, mid, lo


def _unit_rms(x):
    return x * lax.rsqrt(jnp.mean(x * x, axis=-1, keepdims=True) + RMS_EPS)


def _ada_kernel(c_ref, w_ref, b_ref, o_ref):
    c = c_ref[...]
    cs = (c * jax.nn.sigmoid(c)).astype(BF16)
    o_ref[...] = _dot(cs, w_ref[...].astype(BF16)) + b_ref[...]


def _ada_call(c, w, b):
    g, d, n = w.shape
    r = c.shape[0]
    tn = 1024
    return pl.pallas_call(
        _ada_kernel,
        out_shape=jax.ShapeDtypeStruct((g, r, n), F32),
        grid=(g, n // tn),
        in_specs=[pl.BlockSpec((r, d), lambda i, j: (0, 0)),
                  pl.BlockSpec((None, d, tn), lambda i, j: (i, 0, j)),
                  pl.BlockSpec((None, 1, tn), lambda i, j: (i, 0, j))],
        out_specs=pl.BlockSpec((None, r, tn), lambda i, j: (i, 0, j)),
        compiler_params=_params(2),
        name="ada_mod",
    )(c, w, b)


def _mod_rows(ref, rep_ref):
    if rep_ref is None:
        return ref[...]
    hi, mid, lo = _split3(ref[...])
    rep = rep_ref[...]
    return _dot(rep, hi) + _dot(rep, mid) + _dot(rep, lo)


def _inproj_kernel(expand, x_ref, sh_ref, sc_ref, nw_ref, w_ref, wg_ref, bli_ref, blf_ref, *rest):
    rep_ref = rest[0] if expand else None
    qk_ref, v_ref, o_ref, li_ref, lf_ref, wb_ref = rest[1 if expand else 0:]

    @pl.when(pl.program_id(0) == 0)
    def _():
        wb_ref[...] = w_ref[...].astype(BF16)

    u = _unit_rms(x_ref[...]) * nw_ref[...] * (1.0 + _mod_rows(sc_ref, rep_ref)) \
        + _mod_rows(sh_ref, rep_ref)
    ub = u.astype(BF16)
    half = QK_COLS // 2
    q = _dot(ub, wb_ref[:, 0:half]) * (M_DK ** -0.5)
    qk_ref[:, 0:half] = q.astype(BF16)
    qk_ref[:, half:QK_COLS] = _dot(ub, wb_ref[:, half:QK_COLS]).astype(BF16)
    v_ref[...] = _dot(ub, wb_ref[:, QK_COLS:QK_COLS + V_COLS]).astype(BF16)
    o_ref[...] = _dot(ub, wb_ref[:, QK_COLS + V_COLS:QK_COLS + 2 * V_COLS]).astype(BF16)
    lane = lax.broadcasted_iota(jnp.int32, (1, LANES), 1)
    live = lane < M_HEADS
    wg = wg_ref[...].astype(BF16)
    gi = _dot(ub, wg[:, 0:LANES]) + bli_ref[...]
    gf = _dot(ub, wg[:, LANES:2 * LANES]) + blf_ref[...]
    li = GATE_SOFTCAP * jnp.tanh(gi / GATE_SOFTCAP)
    fpre = GATE_SOFTCAP * jnp.tanh(gf / GATE_SOFTCAP)
    lf = jnp.minimum(fpre, 0.0) - jnp.log1p(jnp.exp(-jnp.abs(fpre)))
    li_ref[...] = jnp.where(live, li, 0.0)
    lf_ref[...] = jnp.where(live, lf, 0.0)


def _inproj_call(x, mod, mspec, rep, nw, w_in, wg, bli, blf, tm):
    n = x.shape[0]
    row = lambda i: (i, 0)
    const = lambda i: (0, 0)
    main_cols = QK_COLS + 2 * V_COLS
    expand = rep is not None
    return pl.pallas_call(
        functools.partial(_inproj_kernel, expand),
        out_shape=(jax.ShapeDtypeStruct((n, QK_COLS), BF16),
                   jax.ShapeDtypeStruct((n, V_COLS), BF16),
                   jax.ShapeDtypeStruct((n, V_COLS), BF16),
                   jax.ShapeDtypeStruct((n, LANES), F32),
                   jax.ShapeDtypeStruct((n, LANES), F32)),
        grid=(n // tm,),
        in_specs=[pl.BlockSpec((tm, D_MODEL), row), mspec(0), mspec(1),
                  pl.BlockSpec((1, D_MODEL), const),
                  pl.BlockSpec((None, D_MODEL, main_cols), lambda i: (0, 0, 0),
                               pipeline_mode=pl.Buffered(1)),
                  pl.BlockSpec((D_MODEL, 2 * LANES), const),
                  pl.BlockSpec((1, LANES), const), pl.BlockSpec((1, LANES), const)]
        + ([pl.BlockSpec(rep.shape, const)] if expand else []),
        out_specs=(pl.BlockSpec((tm, QK_COLS), row), pl.BlockSpec((tm, V_COLS), row),
                   pl.BlockSpec((tm, V_COLS), row), pl.BlockSpec((tm, LANES), row),
                   pl.BlockSpec((tm, LANES), row)),
        scratch_shapes=[pltpu.VMEM((D_MODEL, main_cols), BF16)],
        compiler_params=_params(1),
        name="mlstm_inproj",
    )(x, mod, mod, nw, w_in, wg, bli, blf, *([rep] if expand else []))


def _mlstm_kernel(L, nseq, scale_q, *refs):
    c = pl.program_id(1)
    seqs = [[r.at[b] for r in refs] for b in range(nseq)]

    @pl.when(c == 0)
    def _():
        for s in seqs:
            _mlstm_load_state(*s[4:7], *s[11:14])

    for s in seqs:
        _mlstm_seq_step(L, scale_q, *s[0:4], s[7], *s[11:14])

    @pl.when(c == pl.num_programs(1) - 1)
    def _():
        for s in seqs:
            _mlstm_store_state(*s[8:14])


def _mlstm_load_state(c0_ref, n0_ref, m0_ref, c2_s, n_s, m_s):
    m_s[...] = m0_ref[...]
    n_s[...] = n0_ref[...]
    z = jnp.zeros((M_DK, M_DV), F32)
    for j in range(M_HEADS // 2):
        c2_s[j] = jnp.concatenate([jnp.concatenate([c0_ref[2 * j], z], 1),
                                   jnp.concatenate([z, c0_ref[2 * j + 1]], 1)], 0)


def _mlstm_seq_step(L, scale_q, qk_ref, v_ref, li_ref, lf_ref, h_ref, c2_s, n_s, m_s):
    npairs = M_HEADS // 2
    hd = M_DK

    lane = lax.broadcasted_iota(jnp.int32, (1, LANES), 1)
    lo128 = lane < hd
    lane256 = lax.broadcasted_iota(jnp.int32, (1, 2 * LANES), 1)
    lo256 = lane256 < LANES
    row128 = lax.broadcasted_iota(jnp.int32, (LANES, 1), 0)
    top = row128 < hd
    blockdiag = (top & lo256) | (jnp.logical_not(top) & jnp.logical_not(lo256))

    LI = li_ref[...]
    LF = lf_ref[...]
    rowL = lax.broadcasted_iota(jnp.int32, (L, LANES), 0)

    def prefix(x, op, ident):
        d = 1
        while d < L:
            shifted = pltpu.roll(x, d, axis=0)
            x = op(x, jnp.where(rowL >= d, shifted, ident))
            d *= 2
        return x

    Bc = prefix(LF, jnp.add, 0.0)
    Cm = LI - Bc
    mprev = m_s[...]
    Gc = jnp.maximum(mprev, prefix(Cm, jnp.maximum, -jnp.inf))
    A = jnp.exp(mprev - Gc)
    bL = Bc[L - 1:L, :]
    DL = bL + Cm
    mnew = jnp.maximum(bL + mprev, jnp.max(DL, axis=0, keepdims=True))
    ast = jnp.exp(bL + mprev - mnew)
    WST = jnp.exp(DL - mnew)

    def pad_rows(x, rows):
        if x.shape[0] == rows:
            return x
        return jnp.concatenate([x, jnp.zeros((rows - x.shape[0], x.shape[1]), x.dtype)], 0)

    kw_lanes = max(L, hd)
    assert kw_lanes in (hd, LANES) and L <= LANES
    wide = kw_lanes == LANES
    lane_s = lax.broadcasted_iota(jnp.int32, (1, 2 * kw_lanes), 1)
    row_s = lax.broadcasted_iota(jnp.int32, (2 * kw_lanes, 1), 0)
    cm_pad = pad_rows(Cm, kw_lanes)
    if wide:
        XT = cm_pad.T
    else:
        XT = jnp.concatenate([cm_pad, cm_pad], 0).T
    causal = (lane_s & (kw_lanes - 1)) <= lax.broadcasted_iota(jnp.int32, (L, 2 * kw_lanes), 0)
    top_s = row_s < kw_lanes
    J = jnp.where((top_s & lo256) | (jnp.logical_not(top_s) & jnp.logical_not(lo256)),
                  1.0, 0.0).astype(BF16)

    EXPM = jnp.exp(-(Bc + Gc))

    def bc(X, h):
        return jnp.broadcast_to(X[:, h:h + 1], X.shape)

    for j in range(npairs):
        he, ho = 2 * j, 2 * j + 1
        p128 = slice(LANES * j, LANES * (j + 1))
        p256 = slice(2 * LANES * j, 2 * LANES * (j + 1))

        def pair128(X):
            return jnp.where(lo128, bc(X, he), bc(X, ho))

        def pair256(X):
            return jnp.concatenate([bc(X, he), bc(X, ho)], 1)

        q128 = qk_ref[:, LANES * j:LANES * (j + 1)]
        k128 = qk_ref[:, QK_COLS // 2 + LANES * j:QK_COLS // 2 + LANES * (j + 1)]
        v256 = v_ref[:, 2 * LANES * j:2 * LANES * (j + 1)]
        zk = jnp.zeros_like(k128)
        zv = jnp.zeros_like(v256)
        K2t = jnp.concatenate([pad_rows(jnp.where(lo128, k128, zk), kw_lanes),
                               pad_rows(jnp.where(lo128, zk, k128), kw_lanes)], 0)
        V2 = jnp.concatenate([pad_rows(jnp.where(lo256, v256, zv), kw_lanes),
                              pad_rows(jnp.where(lo256, zv, v256), kw_lanes)], 0)
        S = _dot_nt(q128, K2t)
        if wide:
            crow = jnp.concatenate([XT[he:he + 1, :], XT[ho:ho + 1, :]], 1)
            arg = jnp.where(causal, crow - pair256(Gc), -jnp.inf)
        else:
            crow = jnp.where(lo128, XT[he:he + 1, :], XT[ho:ho + 1, :])
            arg = jnp.where(causal, crow - pair128(Gc), -jnp.inf)
        Sw = (S * jnp.exp(arg)).astype(BF16)
        num_intra = _dot(Sw, V2)
        rowsum = _dot(Sw, J)
        C2 = c2_s[j]
        npair = n_s[j:j + 1, :]
        Nrow2 = jnp.concatenate(
            [jnp.broadcast_to(jnp.where(lo128, npair, 0.0), (LANES, LANES)),
             jnp.broadcast_to(jnp.where(lo128, 0.0, npair), (LANES, LANES))], 0).astype(BF16)
        if scale_q:
            qa = (q128.astype(F32) * pair128(A)).astype(BF16)
            num = _dot(qa, C2.astype(BF16)) + num_intra
            den = _dot_nt(qa, Nrow2) + rowsum
        else:
            a256 = pair256(A)
            num = a256 * _dot(q128, C2.astype(BF16)) + num_intra
            den = a256 * _dot_nt(q128, Nrow2) + rowsum
        h = num / jnp.maximum(jnp.abs(den), pair256(EXPM))
        h_ref[:, p256] = h.astype(h_ref.dtype)

        kw = k128.astype(F32) * pair128(WST)
        n_s[j:j + 1, :] = pair128(ast) * npair + jnp.sum(kw, axis=0, keepdims=True)
        kwT = pad_rows(kw, LANES).T.astype(BF16)
        dC = _dot(kwT, pad_rows(v256, LANES))
        c2_s[j] = pair256(ast) * C2 + jnp.where(blockdiag, dC, 0.0)

    m_s[...] = mnew


def _mlstm_store_state(cout_ref, nout_ref, mout_ref, c2_s, n_s, m_s):
    for j in range(M_HEADS // 2):
        C2 = c2_s[j]
        cout_ref[2 * j] = C2[0:M_DK, 0:M_DV]
        cout_ref[2 * j + 1] = C2[M_DK:2 * M_DK, M_DV:2 * M_DV]
    nout_ref[...] = n_s[...]
    mout_ref[...] = m_s[...]


MLSTM_SEQS_PER_STEP = 4


def _mlstm_call(qk, v, li, lf, c0, n0, m0, B, nc, L):
    npairs = M_HEADS // 2
    nseq = min(B, MLSTM_SEQS_PER_STEP)
    T = nc * L
    tok3 = lambda a: a.reshape(B, T, a.shape[-1])
    tok = lambda b, c: (b, c, 0)
    st4 = lambda b, c: (b, 0, 0, 0)
    st3 = lambda b, c: (b, 0, 0)
    h, c_new, n_new, m_new = pl.pallas_call(
        functools.partial(_mlstm_kernel, L, nseq, nc == 1),
        out_shape=(jax.ShapeDtypeStruct((B, T, V_COLS), BF16),
                   jax.ShapeDtypeStruct((B, M_HEADS, M_DK, M_DV), F32),
                   jax.ShapeDtypeStruct((B, npairs, LANES), F32),
                   jax.ShapeDtypeStruct((B, 1, LANES), F32)),
        grid=(B // nseq, nc),
        in_specs=[pl.BlockSpec((nseq, L, QK_COLS), tok), pl.BlockSpec((nseq, L, V_COLS), tok),
                  pl.BlockSpec((nseq, L, LANES), tok), pl.BlockSpec((nseq, L, LANES), tok),
                  pl.BlockSpec((nseq, M_HEADS, M_DK, M_DV), st4),
                  pl.BlockSpec((nseq, npairs, LANES), st3),
                  pl.BlockSpec((nseq, 1, LANES), st3)],
        out_specs=(pl.BlockSpec((nseq, L, V_COLS), tok),
                   pl.BlockSpec((nseq, M_HEADS, M_DK, M_DV), st4),
                   pl.BlockSpec((nseq, npairs, LANES), st3),
                   pl.BlockSpec((nseq, 1, LANES), st3)),
        scratch_shapes=[pltpu.VMEM((nseq, npairs, 2 * M_DK, 2 * M_DV), F32),
                        pltpu.VMEM((nseq, npairs, LANES), F32),
                        pltpu.VMEM((nseq, 1, LANES), F32)],
        compiler_params=_params(2),
        name="mlstm_chunks",
    )(tok3(qk), tok3(v), tok3(li), tok3(lf), c0, n0, m0)
    return h.reshape(B * T, V_COLS), c_new, n_new, m_new


def _route(lg):
    lane = lax.broadcasted_iota(jnp.int32, lg.shape, 1)
    lanef = lane.astype(F32)
    neg = -jnp.inf
    far = float(LANES)
    gm = lane < N_GROUPS
    lgm = jnp.where(gm, lg, neg)
    gmax = jnp.max(lgm, axis=-1, keepdims=True)
    gsum = jnp.sum(jnp.exp(lgm - gmax), axis=-1, keepdims=True)
    g_w = 1.0 / gsum
    gidx = jnp.min(jnp.where(gm & (lg == gmax), lanef, far), axis=-1, keepdims=True)
    first = N_GROUPS + EXPERTS_PER_GROUP * gidx
    sel = (lanef >= first) & (lanef < first + EXPERTS_PER_GROUP)
    l1 = jnp.max(jnp.where(sel, lg, neg), axis=-1, keepdims=True)
    i1 = jnp.min(jnp.where(sel & (lg == l1), lanef, far), axis=-1, keepdims=True)
    sel2 = sel & (lanef != i1)
    l2 = jnp.max(jnp.where(sel2, lg, neg), axis=-1, keepdims=True)
    i2 = jnp.min(jnp.where(sel2 & (lg == l2), lanef, far), axis=-1, keepdims=True)
    r = jnp.exp(l2 - l1)
    w1 = g_w / (1.0 + r)
    w2 = w1 * r
    return jnp.where(lanef == i1, w1, jnp.where(lanef == i2, w2, 0.0)), gidx


def _post_kernel(mlstm, routed, expand, *refs):
    refs = list(refs)
    h_ref = refs.pop(0)
    o_ref = refs.pop(0) if mlstm else None
    x_ref, g1_ref, sh2_ref, sc2_ref, nf_ref = refs[:5]
    refs = refs[5:]
    hn_ref = refs.pop(0) if mlstm else None
    wout_ref, wrh_ref, wrl_ref, br_ref = refs[:4]
    refs = refs[4:]
    tri_ref = refs.pop(0) if routed else None
    rep_ref = refs.pop(0) if expand else None
    x1_ref, u2_ref, gate_ref = refs[:3]
    outs = refs[3:]
    if mlstm:
        hf = h_ref[...].astype(F32)
        parts = [_unit_rms(hf[:, M_DV * i:M_DV * (i + 1)]) for i in range(M_HEADS)]
        hn = jnp.concatenate(parts, 1) * hn_ref[...]
        hg = (hn * jax.nn.sigmoid(o_ref[...].astype(F32))).astype(BF16)
    else:
        hg = h_ref[...]
    x1 = x_ref[...] + _mod_rows(g1_ref, rep_ref) * _dot(hg, wout_ref[...])
    x1_ref[...] = x1
    u2 = _unit_rms(x1) * nf_ref[...] * (1.0 + _mod_rows(sc2_ref, rep_ref)) + _mod_rows(sh2_ref, rep_ref)
    uh, ul = _split2(u2)
    lg = _dot(uh, wrh_ref[...]) + _dot(ul, wrh_ref[...]) + _dot(uh, wrl_ref[...]) + br_ref[...]
    gate, gidx = _route(lg)
    u2_ref[...] = uh
    gate_ref[...] = gate
    if not routed:
        return
    gid_ref, col_ref = outs
    rows = []
    for blk in range(x1.shape[0] // LANES):
        col = jnp.broadcast_to(gidx[LANES * blk:LANES * (blk + 1), :], (LANES, LANES))
        rows.append(col.T[0:1, :])
    gid_ref[...] = jnp.concatenate(rows, 1)
    lanef = lax.broadcasted_iota(jnp.int32, gate.shape, 1).astype(F32)
    onehot = jnp.where(lanef == gidx, 1.0, 0.0)
    earlier = _dot(tri_ref[...], onehot.astype(BF16))
    rank = jnp.sum(onehot * earlier, axis=-1, keepdims=True)
    col_ref[...] = jnp.where(lanef == 0.0, gidx, jnp.where(lanef == 1.0, rank, 0.0))


def _post_call(mlstm, routed, h, o, x, mod, mspec, rep, nf, hn, wout, wrh, wrl, br, tm):
    n = x.shape[0]
    expand = rep is not None
    row = lambda i: (i, 0)
    const = lambda i: (0, 0)
    tok = pl.BlockSpec((tm, D_MODEL), row)
    small = pl.BlockSpec((tm, LANES), row)
    vec = pl.BlockSpec((1, D_MODEL), const)
    ins = [h] + ([o] if mlstm else []) + [x, mod, mod, mod, nf] + ([hn] if mlstm else []) \
        + [wout, wrh, wrl, br]
    specs = [tok] + ([tok] if mlstm else []) + [tok, mspec(2), mspec(3), mspec(4), vec] \
        + ([vec] if mlstm else []) \
        + [pl.BlockSpec((D_MODEL, D_MODEL), const), pl.BlockSpec((D_MODEL, LANES), const),
           pl.BlockSpec((D_MODEL, LANES), const), pl.BlockSpec((1, LANES), const)]
    out_shape = [jax.ShapeDtypeStruct((n, D_MODEL), F32), jax.ShapeDtypeStruct((n, D_MODEL), BF16),
                 jax.ShapeDtypeStruct((n, LANES), F32)]
    out_specs = [tok, tok, small]
    if routed:
        t = jnp.arange(tm)
        ins.append((t[None, :] < t[:, None]).astype(BF16))
        specs.append(pl.BlockSpec((tm, tm), const))
        out_shape += [jax.ShapeDtypeStruct((n // tm, 1, tm), F32), jax.ShapeDtypeStruct((n, LANES), F32)]
        out_specs += [pl.BlockSpec((None, 1, tm), lambda i: (i, 0, 0)), small]
    if expand:
        ins.append(rep)
        specs.append(pl.BlockSpec(rep.shape, const))
    out_shape, out_specs = tuple(out_shape), tuple(out_specs)
    return pl.pallas_call(
        functools.partial(_post_kernel, mlstm, routed, expand),
        out_shape=out_shape,
        grid=(n // tm,),
        in_specs=specs,
        out_specs=out_specs,
        compiler_params=_params(1),
        name="post_mlstm" if mlstm else "post_attn",
    )(*ins)


ITEM_ROWS = 8
RUN_ALIGN = 16
RUN_SIZES = (512, 256, 128, 64, 32, 16)
RUN_SRC, RUN_DST, RUN_LEN, RUN_TOTAL = 0, N_GROUPS, 2 * N_GROUPS, 3 * N_GROUPS


def _stage_rows(tm):
    assert N_GROUPS * (RUN_ALIGN - 1) <= LANES
    return tm + LANES


def _sorted_rows(n, tm):
    rows = n + N_GROUPS * (n // tm) * RUN_ALIGN + tm
    return -(-rows // tm) * tm


def _sort_meta_kernel(tm, nts, gid_ref, q_ref, runs_ref, items_ref):
    ntiles = gid_ref.shape[0]
    gid = gid_ref[...]
    r = lax.broadcasted_iota(jnp.int32, (tm, tm), 0)
    c = lax.broadcasted_iota(jnp.int32, (tm, tm), 1)
    before = jnp.where(r < c, 1.0, 0.0).astype(BF16)
    trow = lax.broadcasted_iota(jnp.int32, (ntiles, LANES), 0)
    lane = lax.broadcasted_iota(jnp.int32, (ntiles, LANES), 1)
    k = lax.broadcasted_iota(jnp.int32, (1, LANES), 1).astype(F32)
    zero11 = jnp.zeros((1, 1), F32)
    q = jnp.zeros((ntiles, tm), F32)
    runs = jnp.zeros((ntiles, LANES), F32)
    src = jnp.zeros((ntiles, 1), F32)
    start, nitems = zero11, zero11
    grp = jnp.zeros((1, LANES), F32)
    tile = jnp.zeros((1, LANES), F32)
    valid = jnp.zeros((1, LANES), F32)
    for g in range(N_GROUPS):
        mask = jnp.where(gid == float(g), 1.0, 0.0)
        cnt = jnp.sum(mask, axis=1, keepdims=True)
        padded = jnp.floor((cnt + (RUN_ALIGN - 1.0)) * (1.0 / RUN_ALIGN)) * RUN_ALIGN
        incl = jnp.broadcast_to(padded, (ntiles, LANES))
        d = 1
        while d < ntiles:
            incl = incl + jnp.where(trow >= d, pltpu.roll(incl, d, axis=0), 0.0)
            d *= 2
        total = incl[ntiles - 1:ntiles, 0:1]
        dst = start + incl[:, 0:1] - padded
        q = q + mask * (src + _dot(mask.astype(BF16), before))
        runs = runs + jnp.where(lane == RUN_SRC + g, src, 0.0) + jnp.where(lane == RUN_DST + g, dst, 0.0) \
            + jnp.where(lane == RUN_LEN + g, padded, 0.0)
        src = src + padded
        end = start + total
        ft = jnp.floor(start * (1.0 / tm))
        lt = jnp.floor((end - 1.0) * (1.0 / tm))
        ni = jnp.where(total > 0.0, lt - ft + 1.0, 0.0)
        inside = (k >= nitems) & (k < nitems + ni)
        grp = grp + jnp.where(inside, float(g), 0.0)
        tile = tile + jnp.where(inside, ft + (k - nitems), 0.0)
        valid = valid + jnp.where(inside, 1.0, 0.0)
        nitems = nitems + ni
        start = end
    runs = runs + jnp.where(lane == RUN_TOTAL, start, 0.0)
    live = valid > 0.0
    prev = pltpu.roll(tile, 1, axis=1)
    nxt = pltpu.roll(tile, LANES - 1, axis=1)
    first = live & ((k == 0.0) | (tile != prev))
    last = live & ((k == nitems - 1.0) | (tile != nxt))
    spare = jnp.floor((start - 1.0) * (1.0 / tm)) + 1.0 + (k - nitems)
    fill = jnp.logical_not(live) & (spare <= nts - 1.0)
    tile = jnp.where(live, tile, jnp.minimum(spare, nts - 1.0))
    grp = jnp.where(live, grp, jnp.max(grp, axis=1, keepdims=True))
    flag = lambda m: jnp.where(m, 1.0, 0.0)
    q_ref[...] = q
    runs_ref[...] = runs.astype(jnp.int32)
    table = jnp.concatenate([tile, grp, valid, flag(first | fill), flag(last | fill),
                             jnp.zeros((ITEM_ROWS - 5, LANES), F32)], 0)
    items_ref[...] = table.astype(jnp.int32)


def _sort_meta_call(gid, tm):
    ntiles = gid.shape[0]
    nts = _sorted_rows(ntiles * tm, tm) // tm
    assert nts + N_GROUPS - 1 <= LANES and tm <= RUN_SIZES[0]
    return pl.pallas_call(
        functools.partial(_sort_meta_kernel, tm, nts),
        out_shape=(jax.ShapeDtypeStruct((ntiles, tm), F32),
                   jax.ShapeDtypeStruct((ntiles, LANES), jnp.int32),
                   jax.ShapeDtypeStruct((ITEM_ROWS, LANES), jnp.int32)),
        compiler_params=_params(0),
        name="moe_sort_meta",
    )(gid)


def _run_copies(runs_ref, i, tile_ref, sorted_hbm, sem, to_sorted):
    pieces = []
    for g in range(N_GROUPS):
        src = runs_ref[i, RUN_SRC + g]
        dst = runs_ref[i, RUN_DST + g]
        length = runs_ref[i, RUN_LEN + g]
        for s in RUN_SIZES:
            def build(src=src, dst=dst, length=length, s=s):
                off = length & (-2 * s)
                a = tile_ref.at[pl.ds(pl.multiple_of(src + off, RUN_ALIGN), s), :]
                b = sorted_hbm.at[pl.ds(pl.multiple_of(dst + off, RUN_ALIGN), s), :]
                return pltpu.make_async_copy(a, b, sem) if to_sorted else pltpu.make_async_copy(b, a, sem)
            pieces.append(((length & s) != 0, build))
    return pieces


def _start(pieces):
    for pred, build in pieces:
        pl.when(pred)(lambda build=build: build().start())


def _wait(pieces):
    for pred, build in pieces:
        pl.when(pred)(lambda build=build: build().wait())


def _start_then_wait(pieces):
    _start(pieces)
    _wait(pieces)


def _scatter_kernel(tm, nts, ntiles, runs_ref, q_ref, u_ref, g_ref, su_hbm, sg_hbm,
                    stu, stg, sem_u, sem_g):
    i = pl.program_id(0)
    slot = i % 2

    def copies(tile, slot):
        return (_run_copies(runs_ref, tile, stu.at[slot], su_hbm, sem_u.at[slot], True)
                + _run_copies(runs_ref, tile, stg.at[slot], sg_hbm, sem_g.at[slot], True))

    r = lax.broadcasted_iota(jnp.int32, (_stage_rows(tm), tm), 0).astype(F32)
    perm = jnp.where(q_ref[...] == r, 1.0, 0.0).astype(BF16)
    stu[slot] = _dot(perm, u_ref[...]).astype(BF16)
    g3 = _split3(g_ref[...])
    stg[slot] = _dot(perm, g3[0]) + _dot(perm, g3[1]) + _dot(perm, g3[2])
    _start(copies(i, slot))

    @pl.when(i > 0)
    def _():
        _wait(copies(i - 1, 1 - slot))

    @pl.when(i == ntiles - 1)
    def _():
        _wait(copies(i, slot))
        stu[slot] = jnp.zeros(stu.shape[1:], stu.dtype)
        stg[slot] = jnp.zeros(stg.shape[1:], stg.dtype)
        total = runs_ref[i, RUN_TOTAL]
        tail = nts * tm - total
        nfull = tail // tm
        pieces = []
        for tile_ref, hbm, sem in ((stu.at[slot], su_hbm, sem_u.at[slot]),
                                   (stg.at[slot], sg_hbm, sem_g.at[slot])):
            for j in range(nts - ntiles):
                def full(j=j, tile_ref=tile_ref, hbm=hbm, sem=sem):
                    dst = pl.multiple_of(total + j * tm, RUN_ALIGN)
                    return pltpu.make_async_copy(tile_ref.at[pl.ds(0, tm), :], hbm.at[pl.ds(dst, tm), :], sem)
                pieces.append((j < nfull, full))
            rem = tail - nfull * tm
            for s in RUN_SIZES:
                if s >= tm:
                    continue
                def part(s=s, tile_ref=tile_ref, hbm=hbm, sem=sem):
                    dst = pl.multiple_of(total + nfull * tm + (rem & (-2 * s)), RUN_ALIGN)
                    return pltpu.make_async_copy(tile_ref.at[pl.ds(0, s), :], hbm.at[pl.ds(dst, s), :], sem)
                pieces.append(((rem & s) != 0, part))
        _start_then_wait(pieces)


def _scatter_call(runs, q3, u2, gate, tm):
    n = u2.shape[0]
    ns = _sorted_rows(n, tm)
    grid_spec = pltpu.PrefetchScalarGridSpec(
        num_scalar_prefetch=1,
        grid=(n // tm,),
        in_specs=[pl.BlockSpec((None, 1, tm), lambda i, *_: (i, 0, 0)),
                  pl.BlockSpec((tm, D_MODEL), lambda i, *_: (i, 0)),
                  pl.BlockSpec((tm, LANES), lambda i, *_: (i, 0))],
        out_specs=(pl.BlockSpec(memory_space=pl.ANY), pl.BlockSpec(memory_space=pl.ANY)),
        scratch_shapes=[pltpu.VMEM((2, _stage_rows(tm), D_MODEL), BF16),
                        pltpu.VMEM((2, _stage_rows(tm), LANES), F32),
                        pltpu.SemaphoreType.DMA((2,)), pltpu.SemaphoreType.DMA((2,))])
    return pl.pallas_call(
        functools.partial(_scatter_kernel, tm, ns // tm, n // tm),
        out_shape=(jax.ShapeDtypeStruct((ns, D_MODEL), BF16), jax.ShapeDtypeStruct((ns, LANES), F32)),
        grid_spec=grid_spec,
        compiler_params=_params(1),
        name="moe_scatter",
    )(runs, q3, u2, gate)


def _ungroup(tm, runs_ref, col_ref, ys_hbm, stage, sem):
    i = pl.program_id(0)
    slot = i % 2

    def copies(tile, slot):
        return _run_copies(runs_ref, tile, stage.at[slot], ys_hbm, sem.at[slot], False)

    @pl.when(i == 0)
    def _():
        stage[...] = jnp.zeros_like(stage)
        _start(copies(i, slot))

    @pl.when(i + 1 < pl.num_programs(0))
    def _():
        _start(copies(i + 1, 1 - slot))

    _wait(copies(i, slot))
    col = col_ref[...]
    gidx = col[:, 0:1]
    src = [runs_ref[i, RUN_SRC + g].astype(F32) for g in range(N_GROUPS)]
    first = src[N_GROUPS - 1]
    for g in range(N_GROUPS - 2, -1, -1):
        first = jnp.where(gidx == float(g), src[g], first)
    q = first + col[:, 1:2]
    lanes = lax.broadcasted_iota(jnp.int32, (tm, _stage_rows(tm)), 1).astype(F32)
    perm_t = jnp.where(q == lanes, 1.0, 0.0).astype(BF16)
    return _dot(perm_t, stage[slot])


def _moe_routed_kernel(items_ref, su_ref, sg_ref, wgu_ref, wd_ref, out_ref, acc_ref, wgu_b, wd_b):
    kk = pl.program_id(0)
    grp = items_ref[1, kk]

    @pl.when((kk == 0) | (grp != items_ref[1, jnp.maximum(kk - 1, 0)]))
    def _():
        for e in range(EXPERTS_PER_GROUP):
            wgu_b[e] = wgu_ref[e].astype(BF16)
            wd_b[e] = wd_ref[e].astype(BF16)

    @pl.when(items_ref[3, kk] == 1)
    def _():
        acc_ref[...] = jnp.zeros_like(acc_ref)

    @pl.when(items_ref[2, kk] == 1)
    def _():
        u = su_ref[...]
        gate = sg_ref[...]
        lane = lax.broadcasted_iota(jnp.int32, gate.shape, 1)
        first_lane = N_GROUPS + EXPERTS_PER_GROUP * grp
        acts = []
        for e in range(EXPERTS_PER_GROUP):
            gcol = jnp.sum(jnp.where(lane == first_lane + e, gate, 0.0), axis=-1, keepdims=True)
            hu = _dot(u, wgu_b[e])
            hg = hu[:, 0:D_EXPERT]
            acts.append((hg * jax.nn.sigmoid(hg) * hu[:, D_EXPERT:2 * D_EXPERT] * gcol).astype(BF16))
        wd_all = wd_b[...].reshape(EXPERTS_PER_GROUP * D_EXPERT, D_MODEL)
        acc_ref[...] += _dot(jnp.concatenate(acts, 1), wd_all)

    @pl.when(items_ref[4, kk] == 1)
    def _():
        out_ref[...] = acc_ref[...].astype(out_ref.dtype)


def _moe_routed_call(items, su, sg, wgu, wd, layer, tm):
    ns = su.shape[0]
    n_items = ns // tm + N_GROUPS - 1
    depth = wgu.shape[0]
    wgu5 = wgu.reshape(depth, N_GROUPS, EXPERTS_PER_GROUP, D_MODEL, 2 * D_EXPERT)
    wd5 = wd.reshape(depth, N_GROUPS, EXPERTS_PER_GROUP, D_EXPERT, D_MODEL)
    grid_spec = pltpu.PrefetchScalarGridSpec(
        num_scalar_prefetch=1,
        grid=(n_items,),
        in_specs=[pl.BlockSpec((tm, D_MODEL), lambda k, it: (it[0, k], 0)),
                  pl.BlockSpec((tm, LANES), lambda k, it: (it[0, k], 0)),
                  pl.BlockSpec((None, None, EXPERTS_PER_GROUP, D_MODEL, 2 * D_EXPERT),
                               lambda k, it: (layer, it[1, k], 0, 0, 0)),
                  pl.BlockSpec((None, None, EXPERTS_PER_GROUP, D_EXPERT, D_MODEL),
                               lambda k, it: (layer, it[1, k], 0, 0, 0))],
        out_specs=pl.BlockSpec((tm, D_MODEL), lambda k, it: (it[0, k], 0)),
        scratch_shapes=[pltpu.VMEM((tm, D_MODEL), F32),
                        pltpu.VMEM((EXPERTS_PER_GROUP, D_MODEL, 2 * D_EXPERT), BF16),
                        pltpu.VMEM((EXPERTS_PER_GROUP, D_EXPERT, D_MODEL), BF16)])
    return pl.pallas_call(
        _moe_routed_kernel,
        out_shape=jax.ShapeDtypeStruct((ns, D_MODEL), BF16),
        grid_spec=grid_spec,
        compiler_params=_params(1),
        name="moe_routed",
    )(items, su, sg, wgu5, wd5)


def _resid_gather_kernel(tm, runs_ref, x_ref, g2_ref, col_ref, ys_hbm, y_ref, stage, sem):
    y_ref[...] = x_ref[...] + g2_ref[...] * _ungroup(tm, runs_ref, col_ref, ys_hbm, stage, sem)


def _resid_gather_call(x, runs, col, ys, mod, mspec, tm):
    n = x.shape[0]
    row = lambda i, *_: (i, 0)
    grid_spec = pltpu.PrefetchScalarGridSpec(
        num_scalar_prefetch=1,
        grid=(n // tm,),
        in_specs=[pl.BlockSpec((tm, D_MODEL), row), mspec(5), pl.BlockSpec((tm, LANES), row),
                  pl.BlockSpec(memory_space=pl.ANY)],
        out_specs=pl.BlockSpec((tm, D_MODEL), row),
        scratch_shapes=[pltpu.VMEM((2, _stage_rows(tm), D_MODEL), BF16), pltpu.SemaphoreType.DMA((2,))])
    return pl.pallas_call(
        functools.partial(_resid_gather_kernel, tm),
        out_shape=jax.ShapeDtypeStruct((n, D_MODEL), F32),
        grid_spec=grid_spec,
        compiler_params=_params(1),
        name="moe_unsort_resid",
    )(runs, x, mod, col, ys)


def _moe_kernel(expand, u_ref, gate_ref, wgu_ref, wd_ref, x1_ref, g2_ref, *rest):
    rep_ref = rest[0] if expand else None
    y_ref, acc_ref = rest[1 if expand else 0:]
    e = pl.program_id(1)

    @pl.when(e == 0)
    def _():
        acc_ref[...] = jnp.zeros_like(acc_ref)

    hu = _dot(u_ref[...], wgu_ref[...].astype(BF16))
    hg = hu[:, 0:D_EXPERT]
    hv = hu[:, D_EXPERT:2 * D_EXPERT]
    gate = gate_ref[...]
    lane = lax.broadcasted_iota(jnp.int32, gate.shape, 1)
    gcol = jnp.sum(jnp.where(lane == N_GROUPS + e, gate, 0.0), axis=-1, keepdims=True)
    act = hg * jax.nn.sigmoid(hg) * hv * gcol
    acc_ref[...] += _dot(act.astype(BF16), wd_ref[...].astype(BF16))

    @pl.when(e == N_EXPERTS - 1)
    def _():
        y_ref[...] = x1_ref[...] + _mod_rows(g2_ref, rep_ref) * acc_ref[...]


def _moe_call(u2, gate, wgu, wd, layer, x1, mod, mspec, rep, tm):
    n = u2.shape[0]
    row = lambda i, e: (i, 0)
    expand = rep is not None
    return pl.pallas_call(
        functools.partial(_moe_kernel, expand),
        out_shape=jax.ShapeDtypeStruct((n, D_MODEL), F32),
        grid=(n // tm, N_EXPERTS),
        in_specs=[pl.BlockSpec((tm, D_MODEL), row), pl.BlockSpec((tm, LANES), row),
                  pl.BlockSpec((None, None, D_MODEL, 2 * D_EXPERT), lambda i, e: (layer, e, 0, 0)),
                  pl.BlockSpec((None, None, D_EXPERT, D_MODEL), lambda i, e: (layer, e, 0, 0)),
                  pl.BlockSpec((tm, D_MODEL), row), mspec(5)]
        + ([pl.BlockSpec(rep.shape, lambda i, e: (0, 0))] if expand else []),
        out_specs=pl.BlockSpec((tm, D_MODEL), row),
        scratch_shapes=[pltpu.VMEM((tm, D_MODEL), F32)],
        compiler_params=_params(2),
        name="moe_dense",
    )(u2, gate, wgu, wd, x1, mod, *([rep] if expand else []))


def _rope128(x, cos, sa, sb):
    return x * cos + pltpu.roll(x, LANES - ROT_DIM // 2, axis=1) * sa \
        + pltpu.roll(x, ROT_DIM // 2, axis=1) * sb


def _kvq_kernel(gather_tm, *refs):
    if gather_tm:
        (runs_ref, x_ref, g2_ref, col_ref, ys_hbm, kvsh_ref, kvsc_ref, sh1_ref, sc1_ref, kvn_ref,
         nm_ref, wkv_ref, wq_ref, kn_ref, qn_ref, cos_ref, sa_ref, sb_ref, g64_ref, gr_ref, gb_ref,
         q_ref, k_ref, v_ref, xa_ref, stage, sem) = refs
        x = x_ref[...] + g2_ref[...] * _ungroup(gather_tm, runs_ref, col_ref, ys_hbm, stage, sem)
        xa_ref[...] = x
        rep_ref = None
    else:
        (x_ref, kvsh_ref, kvsc_ref, sh1_ref, sc1_ref, kvn_ref, nm_ref,
         wkv_ref, wq_ref, kn_ref, qn_ref, cos_ref, sa_ref, sb_ref, g64_ref, gr_ref, gb_ref,
         rep_ref, q_ref, k_ref, v_ref) = refs
        x = x_ref[...]
    xn = _unit_rms(x)
    cos, sa, sb = cos_ref[...], sa_ref[...], sb_ref[...]

    ukv = xn * kvn_ref[...] * (1.0 + _mod_rows(kvsc_ref, rep_ref)) + _mod_rows(kvsh_ref, rep_ref)
    kv = _dot(ukv.astype(BF16), wkv_ref[...])
    k = kv[:, 0:LANES]
    v_ref[...] = kv[:, LANES:2 * LANES]
    kh, kl = _split2(k * k)
    ms = _dot(kh, g64_ref[...]) + _dot(kl, g64_ref[...])
    k_ref[...] = _rope128(k * lax.rsqrt(ms + RMS_EPS) * kn_ref[...], cos, sa, sb)

    u1 = xn * nm_ref[...] * (1.0 + _mod_rows(sc1_ref, rep_ref)) + _mod_rows(sh1_ref, rep_ref)
    q = _dot(u1.astype(BF16), wq_ref[...])
    ms16 = _dot((q * q).astype(BF16), gr_ref[...])
    rh, rl = _split2(lax.rsqrt(ms16 + RMS_EPS))
    rsb = _dot(rh, gb_ref[...]) + _dot(rl, gb_ref[...])
    qn = q * rsb * qn_ref[...]
    for i in range(D_MODEL // LANES):
        sl = slice(LANES * i, LANES * (i + 1))
        q_ref[:, sl] = (_rope128(qn[:, sl], cos, sa, sb) * (HEAD_DIM ** -0.5)).astype(BF16)


def _kvq_call(x, routed, mod0, mod1, mspec, kvmod, kvspec, rep, kvn, nm, wkv, wq, kn, qn, tabs, tab,
              g64, gr, gb, tm):
    n = x.shape[0]
    row = lambda i, *_: (i, 0)
    const = lambda i, *_: (0, 0)
    vec = pl.BlockSpec((1, D_MODEL), const)
    tok = pl.BlockSpec((tm, D_MODEL), row)
    small = pl.BlockSpec((tm, LANES), row)
    ins = [x, kvmod, kvmod, mod1, mod1, kvn, nm, wkv, wq, kn, qn, *tabs, g64, gr, gb]
    specs = [tok, kvspec(0), kvspec(1), mspec(0), mspec(1), vec, vec,
             pl.BlockSpec((D_MODEL, 2 * LANES), const), pl.BlockSpec((D_MODEL, D_MODEL), const),
             pl.BlockSpec((1, LANES), const), vec, tab, tab, tab,
             pl.BlockSpec((LANES, LANES), const), pl.BlockSpec((D_MODEL, LANES), const),
             pl.BlockSpec((LANES, D_MODEL), const)]
    out_shape = [jax.ShapeDtypeStruct((n, D_MODEL), BF16), jax.ShapeDtypeStruct((n, LANES), F32),
                 jax.ShapeDtypeStruct((n, LANES), F32)]
    out_specs = [tok, small, small]
    scratch = []
    nprefetch = 0
    if routed is not None:
        runs, col, ys = routed
        nprefetch = 1
        ins = [runs, x, mod0, col, ys] + ins[1:]
        specs = [tok, mspec(5), small, pl.BlockSpec(memory_space=pl.ANY)] + specs[1:]
        out_shape.append(jax.ShapeDtypeStruct((n, D_MODEL), F32))
        out_specs.append(tok)
        scratch = [pltpu.VMEM((2, _stage_rows(tm), D_MODEL), BF16), pltpu.SemaphoreType.DMA((2,))]
    else:
        ins.append(rep)
        specs.append(pl.BlockSpec(rep.shape, const))
    grid_spec = pltpu.PrefetchScalarGridSpec(
        num_scalar_prefetch=nprefetch, grid=(n // tm,), in_specs=specs,
        out_specs=tuple(out_specs), scratch_shapes=scratch)
    return pl.pallas_call(
        functools.partial(_kvq_kernel, tm if routed is not None else 0),
        out_shape=tuple(out_shape),
        grid_spec=grid_spec,
        compiler_params=_params(1),
        name="kv_q_proj",
    )(*ins)


def _attn_core(q, kcat, vcat, bias, sinks_ref, o_ref):
    tk = kcat.shape[0]
    pairs = N_Q_HEADS // N_KV_HEADS // 2
    lane = lax.broadcasted_iota(jnp.int32, (1, LANES), 1)
    lo = lane < HEAD_DIM
    kro = pltpu.roll(kcat, HEAD_DIM, axis=1)
    vro = pltpu.roll(vcat, HEAD_DIM, axis=1)
    one_e = jnp.broadcast_to(jnp.where(lo, 1.0, 0.0), (tk, LANES))
    one_o = 1.0 - one_e
    for g in range(N_KV_HEADS):
        if g == 0:
            ke, ko = jnp.where(lo, kcat, 0.0), jnp.where(lo, 0.0, kro)
            ve, vo = jnp.where(lo, vcat, 0.0), jnp.where(lo, 0.0, vro)
        else:
            ke, ko = jnp.where(lo, kro, 0.0), jnp.where(lo, 0.0, kcat)
            ve, vo = jnp.where(lo, vro, 0.0), jnp.where(lo, 0.0, vcat)
        k2 = jnp.concatenate([ke, ko], 0).astype(BF16)
        v2 = jnp.concatenate([jnp.concatenate([ve, one_e], 1),
                              jnp.concatenate([vo, one_o], 1)], 0).astype(BF16)
        for p in range(pairs):
            hp = g * pairs + p
            s = _dot_nt(q[:, LANES * hp:LANES * (hp + 1)], k2)
            halves, corr = [], []
            for par in range(2):
                sp = s[:, par * tk:(par + 1) * tk] + bias
                sink = sinks_ref[2 * hp + par]
                m = jnp.maximum(jnp.max(sp, axis=-1, keepdims=True), sink)
                halves.append(jnp.exp(sp - m).astype(BF16))
                corr.append(jnp.exp(sink - m))
            o2 = _dot(jnp.concatenate(halves, 1), v2)
            den = o2[:, LANES:2 * LANES] + jnp.where(lo, corr[0], corr[1])
            o_ref[:, LANES * hp:LANES * (hp + 1)] = (o2[:, 0:LANES] / den).astype(o_ref.dtype)


ATTN_BLOCKS_PER_STEP = 4


def _attn_prompt_kernel(sinks_ref, q_ref, kp_ref, kc_ref, vp_ref, vc_ref, bias_ref, o_ref):
    kall = jnp.concatenate([kp_ref[...], kc_ref[...]], 0)
    vall = jnp.concatenate([vp_ref[...], vc_ref[...]], 0)
    first = jnp.minimum(pl.program_id(1), 1)
    for j in range(ATTN_BLOCKS_PER_STEP):
        rows = slice(j * WINDOW, (j + 1) * WINDOW)
        keys = slice(j * WINDOW, (j + 2) * WINDOW)
        bias = bias_ref[first] if j == 0 else bias_ref[1]
        _attn_core(q_ref[rows, :], kall[keys, :], vall[keys, :], bias, sinks_ref, o_ref.at[rows, :])


ATTN_SEQS_PER_STEP = 8


def _attn_sample_kernel(sinks_ref, q_ref, kc_ref, kn_ref, vc_ref, vn_ref, bias_ref, o_ref, kbuf, vbuf):
    @pl.when(pl.program_id(0) == 0)
    def _():
        kbuf[...] = jnp.zeros_like(kbuf)
        vbuf[...] = jnp.zeros_like(vbuf)

    for b in range(ATTN_SEQS_PER_STEP):
        kbuf[b, 0:WINDOW, :] = kc_ref[b]
        kbuf[b, WINDOW:WINDOW + SAMPLE_PAD, :] = kn_ref[b]
        vbuf[b, 0:WINDOW, :] = vc_ref[b]
        vbuf[b, WINDOW:WINDOW + SAMPLE_PAD, :] = vn_ref[b]
    for b in range(ATTN_SEQS_PER_STEP):
        _attn_core(q_ref[b], kbuf[b], vbuf[b], bias_ref[...], sinks_ref, o_ref.at[b])


def _window_bias(tq, first):
    qi = jnp.arange(tq)[:, None]
    kj = jnp.arange(2 * WINDOW)[None, :]
    ok = (kj > qi) & (kj <= qi + WINDOW)
    if first:
        ok = ok & (kj >= WINDOW)
    return jnp.where(ok, 0.0, -jnp.inf).astype(F32)


def _attn_prompt_call(sinks, q, k, v, B, nb):
    n = q.shape[0]
    per = ATTN_BLOCKS_PER_STEP
    steps = nb // per
    cur = lambda b, i: (b * steps + i, 0)
    prev = lambda b, i: (b * nb + jnp.maximum(per * i - 1, 0), 0)
    bias = jnp.stack([_window_bias(WINDOW, True), _window_bias(WINDOW, False)])
    kv_prev = pl.BlockSpec((WINDOW, LANES), prev)
    kv_cur = pl.BlockSpec((per * WINDOW, LANES), cur)
    return pl.pallas_call(
        _attn_prompt_kernel,
        out_shape=jax.ShapeDtypeStruct((n, D_MODEL), BF16),
        grid=(B, steps),
        in_specs=[pl.BlockSpec(memory_space=pltpu.SMEM),
                  pl.BlockSpec((per * WINDOW, D_MODEL), cur), kv_prev, kv_cur, kv_prev, kv_cur,
                  pl.BlockSpec((2, WINDOW, 2 * WINDOW), lambda b, i: (0, 0, 0))],
        out_specs=pl.BlockSpec((per * WINDOW, D_MODEL), cur),
        compiler_params=_params(2),
        name="swa_prompt",
    )(sinks, q, k, k, v, v, bias)


def _attn_sample_call(sinks, q, kcache, knew, vcache, vnew):
    B = q.shape[0]
    nseq = ATTN_SEQS_PER_STEP
    b3 = lambda b: (b, 0, 0)
    cache = pl.BlockSpec((nseq, WINDOW, LANES), b3)
    new = pl.BlockSpec((nseq, SAMPLE_PAD, LANES), b3)
    return pl.pallas_call(
        _attn_sample_kernel,
        out_shape=jax.ShapeDtypeStruct(q.shape, BF16),
        grid=(B // nseq,),
        in_specs=[pl.BlockSpec(memory_space=pltpu.SMEM),
                  pl.BlockSpec((nseq, SAMPLE_PAD, D_MODEL), b3), cache, new, cache, new,
                  pl.BlockSpec((SAMPLE_PAD, 2 * WINDOW), lambda b: (0, 0))],
        out_specs=pl.BlockSpec((nseq, SAMPLE_PAD, D_MODEL), b3),
        scratch_shapes=[pltpu.VMEM((nseq, 2 * WINDOW, LANES), F32),
                        pltpu.VMEM((nseq, 2 * WINDOW, LANES), F32)],
        compiler_params=_params(1),
        name="swa_sample",
    )(sinks, q, kcache, knew, vcache, vnew, _window_bias(SAMPLE_PAD, False))


def _rope_tables(pos):
    half = ROT_DIM // 2
    inv = ROPE_THETA ** (-jnp.arange(half, dtype=F32) / half)
    ang = pos.astype(F32)[:, None] * inv[None]
    cos, sin = jnp.cos(ang), jnp.sin(ang)
    d = jnp.arange(LANES) % HEAD_DIM
    idx = d % half
    cos_t = jnp.where(d < ROT_DIM, cos[:, idx], 1.0)
    sa = jnp.where(d < half, -sin[:, idx], 0.0)
    sb = jnp.where((d >= half) & (d < ROT_DIM), sin[:, idx], 0.0)
    return cos_t, sa, sb


def _pad_lanes(a, value=0.0):
    return jnp.pad(a, ((0, 0), (0, LANES - a.shape[1])), constant_values=value)


def _prep_weights(ada_w, ada_b, norm_mix, norm_ffn, a_w_in, a_b_gates, a_head_norm, a_w_out,
                  kv_ada_w, kv_ada_b, kv_norm, w_k, w_v, k_norm, b_w_q, b_q_norm, b_sinks, b_w_o,
                  moe_w_group, moe_b_group, moe_w_expert, moe_b_expert, moe_w_gate_up, moe_w_down):
    w = {}
    g0 = QK_COLS + 2 * V_COLS
    w["w_in"] = a_w_in
    w["w_gates"] = jnp.concatenate([_pad_lanes(a_w_in[0, :, g0:g0 + M_HEADS]),
                                    _pad_lanes(a_w_in[0, :, g0 + M_HEADS:])], 1)
    w["bli"] = _pad_lanes(a_b_gates[0][None, :M_HEADS])
    w["blf"] = _pad_lanes(a_b_gates[0][None, M_HEADS:])
    w["head_norm"] = a_head_norm[0][None]
    w["w_out"] = a_w_out[0].astype(BF16)
    w["norm_mix"] = [norm_mix[l][None] for l in range(2)]
    w["norm_ffn"] = [norm_ffn[l][None] for l in range(2)]
    w["router"] = []
    for l in range(2):
        wr = _pad_lanes(jnp.concatenate([moe_w_group[l], moe_w_expert[l]], 1))
        hi = wr.astype(BF16)
        lo = (wr - hi.astype(F32)).astype(BF16)
        br = _pad_lanes(jnp.concatenate([moe_b_group[l], moe_b_expert[l]])[None])
        w["router"].append((hi, lo, br))
    w["w_gu"] = moe_w_gate_up
    w["w_d"] = moe_w_down
    w["kv_norm"] = kv_norm[None]
    w["w_kv"] = jnp.concatenate([w_k, w_v], 1).astype(BF16)
    w["k_norm"] = jnp.tile(k_norm, N_KV_HEADS)[None]
    w["w_q"] = b_w_q[0].astype(BF16)
    w["q_norm"] = jnp.tile(b_q_norm[0], N_Q_HEADS)[None]
    w["sinks"] = b_sinks[0]
    w["w_o"] = b_w_o[0].astype(BF16)
    lanes = jnp.arange(LANES)
    feat = jnp.arange(D_MODEL)
    w["g64"] = jnp.where((lanes[:, None] // HEAD_DIM) == (lanes[None, :] // HEAD_DIM),
                         1.0 / HEAD_DIM, 0.0).astype(BF16)
    w["gr"] = jnp.where((feat[:, None] // HEAD_DIM) == lanes[None, :], 1.0 / HEAD_DIM, 0.0).astype(BF16)
    w["gb"] = jnp.where(lanes[:, None] == (feat[None, :] // HEAD_DIM), 1.0, 0.0).astype(BF16)
    return w


def _trunk(x2, mods, kvmod, mspec, mspec_moe, rep, w, *, B, T, L, tm, tm_moe, tabs, tabspec, c0, n0, m0,
           cache_k=None, cache_v=None):
    sample = cache_k is not None
    nc = T // L if not sample else 1

    qk, v, o, li, lf = _inproj_call(x2, mods[0], mspec, rep, w["norm_mix"][0], w["w_in"],
                                    w["w_gates"], w["bli"], w["blf"], tm)
    if sample:
        def padtok(a, value=0.0):
            a = a.reshape(B, T, a.shape[-1])
            a = jnp.pad(a, ((0, 0), (0, L - T), (0, 0)), constant_values=value)
            return a.reshape(B * L, a.shape[-1])
        qk, v, li, lf = padtok(qk), padtok(v), padtok(li, M_EMPTY), padtok(lf)
    h, c_new, n_new, m_new = _mlstm_call(qk, v, li, lf, c0, n0, m0, B, nc, L)
    if sample:
        h = h.reshape(B, L, V_COLS)[:, :T].reshape(B * T, V_COLS)
    routed = not sample
    ntiles = (B * T) // tm

    def routed_moe(u2, gate, gid, col, layer):
        q, runs, items = _sort_meta_call(gid.reshape(ntiles, tm), tm)
        su, sg = _scatter_call(runs, q.reshape(ntiles, 1, tm), u2, gate, tm)
        ys = _moe_routed_call(items, su, sg, w["w_gu"], w["w_d"], layer, tm)
        return runs, col, ys

    rh, rl, br = w["router"][0]
    post0 = _post_call(True, routed, h, o, x2, mods[0], mspec, rep, w["norm_ffn"][0], w["head_norm"],
                       w["w_out"], rh, rl, br, tm)
    kvq_args = (mods[0], mods[1], mspec, kvmod, mspec, rep, w["kv_norm"], w["norm_mix"][1],
                w["w_kv"], w["w_q"], w["k_norm"], w["q_norm"], tabs, tabspec,
                w["g64"], w["gr"], w["gb"], tm)

    if routed:
        x1, u2, gate, gid, col = post0
        q, k, vv, xa = _kvq_call(x1, routed_moe(u2, gate, gid, col, 0), *kvq_args)
    else:
        x1, u2, gate = post0
        xa = _moe_call(u2, gate, w["w_gu"], w["w_d"], 0, x1, mods[0], mspec_moe, rep, tm_moe)
        q, k, vv = _kvq_call(xa, None, *kvq_args)
    if not sample:
        att = _attn_prompt_call(w["sinks"], q, k, vv, B, T // WINDOW)
        k_win = k.reshape(B, T, LANES)[:, T - WINDOW:].reshape(B, WINDOW, N_KV_HEADS, HEAD_DIM)
        v_win = vv.reshape(B, T, LANES)[:, T - WINDOW:].reshape(B, WINDOW, N_KV_HEADS, HEAD_DIM)
    else:
        def padseq(a):
            return jnp.pad(a.reshape(B, T, a.shape[-1]), ((0, 0), (0, SAMPLE_PAD - T), (0, 0)))
        kc = cache_k.reshape(B, WINDOW, LANES)
        vc = cache_v.reshape(B, WINDOW, LANES)
        att = _attn_sample_call(w["sinks"], padseq(q), kc, padseq(k), vc, padseq(vv))
        att = att[:, :T].reshape(B * T, D_MODEL)
        k_win = jnp.concatenate([kc[:, T:], k.reshape(B, T, LANES)], 1)
        v_win = jnp.concatenate([vc[:, T:], vv.reshape(B, T, LANES)], 1)
        k_win = k_win.reshape(B, WINDOW, N_KV_HEADS, HEAD_DIM)
        v_win = v_win.reshape(B, WINDOW, N_KV_HEADS, HEAD_DIM)
    rh, rl, br = w["router"][1]
    post1 = _post_call(False, routed, att, None, xa, mods[1], mspec, rep, w["norm_ffn"][1], None,
                       w["w_o"], rh, rl, br, tm)
    if routed:
        x3, u4, gate, gid, col = post1
        runs, col, ys = routed_moe(u4, gate, gid, col, 1)
        y = _resid_gather_call(x3, runs, col, ys, mods[1], mspec, tm)
    else:
        x3, u4, gate = post1
        y = _moe_call(u4, gate, w["w_gu"], w["w_d"], 1, x3, mods[1], mspec_moe, rep, tm_moe)
    c_out = c_new[None]
    n_out = n_new.reshape(1, B, M_HEADS, M_DK)
    m_out = m_new[:, 0, :M_HEADS][None]
    return y, c_out, n_out, m_out, k_win, v_win


def kernel(x_prompt, x_sample, c_prompt, c_sample, state_c, state_n, state_m, cache_k_win, cache_v_win, ada_w, ada_b, norm_mix, norm_ffn, a_w_in, a_b_gates, a_head_norm, a_w_out, kv_ada_w, kv_ada_b, kv_norm, w_k, w_v, k_norm, b_w_q, b_q_norm, b_sinks, b_w_o, moe_w_group, moe_b_group, moe_w_expert, moe_b_expert, moe_w_gate_up, moe_w_down):
    Bp, Tp, D = x_prompt.shape
    Bs, Ts, _ = x_sample.shape
    w = _prep_weights(ada_w, ada_b, norm_mix, norm_ffn, a_w_in, a_b_gates, a_head_norm, a_w_out,
                      kv_ada_w, kv_ada_b, kv_norm, w_k, w_v, k_norm, b_w_q, b_q_norm, b_sinks, b_w_o,
                      moe_w_group, moe_b_group, moe_w_expert, moe_b_expert, moe_w_gate_up, moe_w_down)

    rows = Bp + Bs
    rpad = -rows % 8
    c_all = jnp.concatenate([c_prompt, c_sample, jnp.zeros((rpad, D), F32)], 0)
    mod = _ada_call(c_all, ada_w, ada_b[:, None, :])
    kvm = _ada_call(c_all, kv_ada_w[None], kv_ada_b[None, None, :])

    tm_p = 512
    tiles_per_seq = Tp // tm_p
    mods_p = [mod[l, :Bp][:, None, :] for l in range(2)]
    kvmod_p = kvm[0, :Bp][:, None, :]

    def mspec_p(col):
        return pl.BlockSpec((None, 1, D_MODEL), lambda i, *_: (i // tiles_per_seq, 0, col))

    tm_moe_p = 1024
    moe_tiles_per_seq = Tp // tm_moe_p

    def mspec_moe_p(col):
        return pl.BlockSpec((None, 1, D_MODEL), lambda i, *_: (i // moe_tiles_per_seq, 0, col))

    tabs_p = _rope_tables(jnp.arange(Tp, dtype=jnp.int32))
    tabspec_p = pl.BlockSpec((tm_p, LANES), lambda i, *_: (i % tiles_per_seq, 0))
    npairs = M_HEADS // 2
    c0 = jnp.zeros((Bp, M_HEADS, M_DK, M_DV), F32)
    n0 = jnp.zeros((Bp, npairs, LANES), F32)
    m0 = jnp.pad(jnp.full((Bp, 1, M_HEADS), M_EMPTY, F32), ((0, 0), (0, 0), (0, LANES - M_HEADS)))
    yp, cp, np_, mp, kwp, vwp = _trunk(
        x_prompt.reshape(Bp * Tp, D), mods_p, kvmod_p, mspec_p, mspec_moe_p, None, w,
        B=Bp, T=Tp, L=M_CHUNK, tm=tm_p, tm_moe=tm_moe_p, tabs=tabs_p, tabspec=tabspec_p,
        c0=c0, n0=n0, m0=m0)

    ns = Bs * Ts
    mods_s = [mod[l, Bp:Bp + Bs][None] for l in range(2)]
    kvmod_s = kvm[0, Bp:Bp + Bs][None]
    rep = (jnp.arange(ns)[:, None] // Ts == jnp.arange(Bs)[None, :]).astype(BF16)

    def mspec_s(col):
        return pl.BlockSpec((None, Bs, D_MODEL), lambda i, *_: (0, 0, col))

    tabs_s = _rope_tables(PAST_LEN + jnp.arange(Ts, dtype=jnp.int32))
    tabs_s = tuple(jnp.tile(t, (Bs, 1)) for t in tabs_s)
    m0s = jnp.pad(state_m[0][:, None, :], ((0, 0), (0, 0), (0, LANES - M_HEADS)))
    ys, cs, ns_, ms, kws, vws = _trunk(
        x_sample.reshape(ns, D), mods_s, kvmod_s, mspec_s, mspec_s, rep, w,
        B=Bs, T=Ts, L=SAMPLE_PAD, tm=ns, tm_moe=ns, tabs=tabs_s,
        tabspec=pl.BlockSpec((ns, LANES), lambda i, *_: (0, 0)),
        c0=state_c[0], n0=state_n[0].reshape(Bs, npairs, LANES), m0=m0s,
        cache_k=cache_k_win, cache_v=cache_v_win)

    return (yp.reshape(Bp, Tp, D), ys.reshape(Bs, Ts, D), cp, np_, mp, kwp, vwp,
            cs, ns_, ms, kws, vws)
```

```python
import functools

import jax
import jax.numpy as jnp
from jax import lax
from jax.experimental import pallas as pl
from jax.experimental.pallas import tpu as pltpu

F32 = jnp.float32
BF16 = jnp.bfloat16

D_MODEL = 1024
PAST_LEN = 8192
M_HEADS = 8
M_DK = 64
M_DV = 128
M_CHUNK = 128
GATE_SOFTCAP = 15.0
M_EMPTY = -1e30
WINDOW = 128
HEAD_DIM = 64
N_Q_HEADS = 16
N_KV_HEADS = 2
ROPE_THETA = 500000.0
ROT_DIM = 16
N_GROUPS = 4
EXPERTS_PER_GROUP = 4
N_EXPERTS = 16
D_EXPERT = 256
RMS_EPS = 1e-6

LANES = 128
QK_COLS = 2 * M_HEADS * M_DK
V_COLS = M_HEADS * M_DV
IN_COLS = QK_COLS + 2 * V_COLS + 2 * LANES
SAMPLE_PAD = 16
VMEM_LIMIT = 52 * 1024 * 1024

NT_DIMS = (((1,), (1,)), ((), ()))


def _params(n_axes):
    return pltpu.CompilerParams(dimension_semantics=("arbitrary",) * n_axes,
                                vmem_limit_bytes=VMEM_LIMIT)


def _dot(a, b):
    return jnp.dot(a, b, preferred_element_type=F32)


def _dot_nt(a, b):
    return lax.dot_general(a, b, NT_DIMS, preferred_element_type=F32)


def _split2(x):
    hi = x.astype(BF16)
    lo = (x - hi.astype(F32)).astype(BF16)
    return hi, lo


def _split3(x):
    hi = x.astype(BF16)
    r = x - hi.astype(F32)
    mid = r.astype(BF16)
    lo = (r - mid.astype(F32)).astype(BF16)
    return hi, mid, lo


def _unit_rms(x):
    return x * lax.rsqrt(jnp.mean(x * x, axis=-1, keepdims=True) + RMS_EPS)


def _ada_kernel(c_ref, w_ref, b_ref, o_ref):
    c = c_ref[...]
    cs = (c * jax.nn.sigmoid(c)).astype(BF16)
    o_ref[...] = _dot(cs, w_ref[...].astype(BF16)) + b_ref[...]


def _ada_call(c, w, b):
    g, d, n = w.shape
    r = c.shape[0]
    tn = 1024
    return pl.pallas_call(
        _ada_kernel,
        out_shape=jax.ShapeDtypeStruct((g, r, n), F32),
        grid=(g, n // tn),
        in_specs=[pl.BlockSpec((r, d), lambda i, j: (0, 0)),
                  pl.BlockSpec((None, d, tn), lambda i, j: (i, 0, j)),
                  pl.BlockSpec((None, 1, tn), lambda i, j: (i, 0, j))],
        out_specs=pl.BlockSpec((None, r, tn), lambda i, j: (i, 0, j)),
        compiler_params=_params(2),
        name="ada_mod",
    )(c, w, b)


def _mod_rows(ref, rep_ref):
    if rep_ref is None:
        return ref[...]
    hi, mid, lo = _split3(ref[...])
    rep = rep_ref[...]
    return _dot(rep, hi) + _dot(rep, mid) + _dot(rep, lo)


def _inproj_kernel(expand, x_ref, sh_ref, sc_ref, nw_ref, w_ref, wg_ref, bli_ref, blf_ref, *rest):
    rep_ref = rest[0] if expand else None
    qk_ref, v_ref, o_ref, li_ref, lf_ref, wb_ref = rest[1 if expand else 0:]

    @pl.when(pl.program_id(0) == 0)
    def _():
        wb_ref[...] = w_ref[...].astype(BF16)

    u = _unit_rms(x_ref[...]) * (nw_ref[...] * (1.0 + _mod_rows(sc_ref, rep_ref))) \
        + _mod_rows(sh_ref, rep_ref)
    ub = u.astype(BF16)
    half = QK_COLS // 2
    q = _dot(ub, wb_ref[:, 0:half]) * (M_DK ** -0.5)
    qk_ref[:, 0:half] = q.astype(BF16)
    qk_ref[:, half:QK_COLS] = _dot(ub, wb_ref[:, half:QK_COLS]).astype(BF16)
    v_ref[...] = _dot(ub, wb_ref[:, QK_COLS:QK_COLS + V_COLS]).astype(BF16)
    o_ref[...] = jax.nn.sigmoid(_dot(ub, wb_ref[:, QK_COLS + V_COLS:QK_COLS + 2 * V_COLS])).astype(BF16)
    lane = lax.broadcasted_iota(jnp.int32, (1, LANES), 1)
    live = lane < M_HEADS
    wg = wg_ref[...].astype(BF16)
    gi = _dot(ub, wg[:, 0:LANES]) + bli_ref[...]
    gf = _dot(ub, wg[:, LANES:2 * LANES]) + blf_ref[...]
    li = GATE_SOFTCAP * jnp.tanh(gi / GATE_SOFTCAP)
    fpre = GATE_SOFTCAP * jnp.tanh(gf / GATE_SOFTCAP)
    lf = jnp.minimum(fpre, 0.0) - jnp.log1p(jnp.exp(-jnp.abs(fpre)))
    li_ref[...] = jnp.where(live, li, 0.0)
    lf_ref[...] = jnp.where(live, lf, 0.0)


def _inproj_call(x, mod, mspec, rep, nw, w_in, wg, bli, blf, tm):
    n = x.shape[0]
    row = lambda i: (i, 0)
    const = lambda i: (0, 0)
    main_cols = QK_COLS + 2 * V_COLS
    expand = rep is not None
    return pl.pallas_call(
        functools.partial(_inproj_kernel, expand),
        out_shape=(jax.ShapeDtypeStruct((n, QK_COLS), BF16),
                   jax.ShapeDtypeStruct((n, V_COLS), BF16),
                   jax.ShapeDtypeStruct((n, V_COLS), BF16),
                   jax.ShapeDtypeStruct((n, LANES), F32),
                   jax.ShapeDtypeStruct((n, LANES), F32)),
        grid=(n // tm,),
        in_specs=[pl.BlockSpec((tm, D_MODEL), row), mspec(0), mspec(1),
                  pl.BlockSpec((1, D_MODEL), const),
                  pl.BlockSpec((None, D_MODEL, main_cols), lambda i: (0, 0, 0),
                               pipeline_mode=pl.Buffered(1)),
                  pl.BlockSpec((D_MODEL, 2 * LANES), const),
                  pl.BlockSpec((1, LANES), const), pl.BlockSpec((1, LANES), const)]
        + ([pl.BlockSpec(rep.shape, const)] if expand else []),
        out_specs=(pl.BlockSpec((tm, QK_COLS), row), pl.BlockSpec((tm, V_COLS), row),
                   pl.BlockSpec((tm, V_COLS), row), pl.BlockSpec((tm, LANES), row),
                   pl.BlockSpec((tm, LANES), row)),
        scratch_shapes=[pltpu.VMEM((D_MODEL, main_cols), BF16)],
        compiler_params=_params(1),
        name="mlstm_inproj",
    )(x, mod, mod, nw, w_in, wg, bli, blf, *([rep] if expand else []))


def _mlstm_kernel(L, nseq, scale_q, *refs):
    c = pl.program_id(1)
    seqs = [[r.at[b] for r in refs] for b in range(nseq)]

    @pl.when(c == 0)
    def _():
        for s in seqs:
            _mlstm_load_state(*s[4:7], *s[11:14])

    for s in seqs:
        _mlstm_seq_step(L, scale_q, *s[0:4], s[7], *s[11:14])

    @pl.when(c == pl.num_programs(1) - 1)
    def _():
        for s in seqs:
            _mlstm_store_state(*s[8:14])


def _mlstm_load_state(c0_ref, n0_ref, m0_ref, c2_s, n_s, m_s):
    m_s[...] = m0_ref[...]
    n_s[...] = n0_ref[...]
    z = jnp.zeros((M_DK, M_DV), F32)
    for j in range(M_HEADS // 2):
        c2_s[j] = jnp.concatenate([jnp.concatenate([c0_ref[2 * j], z], 1),
                                   jnp.concatenate([z, c0_ref[2 * j + 1]], 1)], 0)


def _mlstm_seq_step(L, scale_q, qk_ref, v_ref, li_ref, lf_ref, h_ref, c2_s, n_s, m_s):
    npairs = M_HEADS // 2
    hd = M_DK

    lane = lax.broadcasted_iota(jnp.int32, (1, LANES), 1)
    lo128 = lane < hd
    lane256 = lax.broadcasted_iota(jnp.int32, (1, 2 * LANES), 1)
    lo256 = lane256 < LANES
    row128 = lax.broadcasted_iota(jnp.int32, (LANES, 1), 0)
    top = row128 < hd
    blockdiag = (top & lo256) | (jnp.logical_not(top) & jnp.logical_not(lo256))

    LI = li_ref[...]
    LF = lf_ref[...]
    rowL = lax.broadcasted_iota(jnp.int32, (L, LANES), 0)

    def prefix(x, op, ident):
        d = 1
        while d < L:
            shifted = pltpu.roll(x, d, axis=0)
            x = op(x, jnp.where(rowL >= d, shifted, ident))
            d *= 2
        return x

    Bc = prefix(LF, jnp.add, 0.0)
    Cm = LI - Bc
    mprev = m_s[...]
    Gc = jnp.maximum(mprev, prefix(Cm, jnp.maximum, -jnp.inf))
    A = jnp.exp(mprev - Gc)
    bL = Bc[L - 1:L, :]
    DL = bL + Cm
    mnew = jnp.maximum(bL + mprev, jnp.max(DL, axis=0, keepdims=True))
    ast = jnp.exp(bL + mprev - mnew)
    WST = jnp.exp(DL - mnew)

    def pad_rows(x, rows):
        if x.shape[0] == rows:
            return x
        return jnp.concatenate([x, jnp.zeros((rows - x.shape[0], x.shape[1]), x.dtype)], 0)

    kw_lanes = max(L, hd)
    assert kw_lanes in (hd, LANES) and L <= LANES
    wide = kw_lanes == LANES
    lane_s = lax.broadcasted_iota(jnp.int32, (1, 2 * kw_lanes), 1)
    row_s = lax.broadcasted_iota(jnp.int32, (2 * kw_lanes, 1), 0)
    cm_pad = pad_rows(Cm, kw_lanes)
    if wide:
        XT = cm_pad.T
    else:
        XT = jnp.concatenate([cm_pad, cm_pad], 0).T
    causal = (lane_s & (kw_lanes - 1)) <= lax.broadcasted_iota(jnp.int32, (L, 2 * kw_lanes), 0)
    top_s = row_s < kw_lanes
    J = jnp.where((top_s & lo256) | (jnp.logical_not(top_s) & jnp.logical_not(lo256)),
                  1.0, 0.0).astype(BF16)

    EXPM = jnp.exp(-(Bc + Gc))

    def bc(X, h):
        return jnp.broadcast_to(X[:, h:h + 1], X.shape)

    for j in range(npairs):
        he, ho = 2 * j, 2 * j + 1
        p128 = slice(LANES * j, LANES * (j + 1))
        p256 = slice(2 * LANES * j, 2 * LANES * (j + 1))

        def pair128(X):
            return jnp.where(lo128, bc(X, he), bc(X, ho))

        def pair256(X):
            return jnp.concatenate([bc(X, he), bc(X, ho)], 1)

        q128 = qk_ref[:, LANES * j:LANES * (j + 1)]
        k128 = qk_ref[:, QK_COLS // 2 + LANES * j:QK_COLS // 2 + LANES * (j + 1)]
        v256 = v_ref[:, 2 * LANES * j:2 * LANES * (j + 1)]
        zk = jnp.zeros_like(k128)
        zv = jnp.zeros_like(v256)
        K2t = jnp.concatenate([pad_rows(jnp.where(lo128, k128, zk), kw_lanes),
                               pad_rows(jnp.where(lo128, zk, k128), kw_lanes)], 0)
        V2 = jnp.concatenate([pad_rows(jnp.where(lo256, v256, zv), kw_lanes),
                              pad_rows(jnp.where(lo256, zv, v256), kw_lanes)], 0)
        S = _dot_nt(q128, K2t)
        if wide:
            crow = jnp.concatenate([XT[he:he + 1, :], XT[ho:ho + 1, :]], 1)
            arg = jnp.where(causal, crow - pair256(Gc), -jnp.inf)
        else:
            crow = jnp.where(lo128, XT[he:he + 1, :], XT[ho:ho + 1, :])
            arg = jnp.where(causal, crow - pair128(Gc), -jnp.inf)
        Sw = (S * jnp.exp(arg)).astype(BF16)
        num_intra = _dot(Sw, V2)
        rowsum = _dot(Sw, J)
        C2 = c2_s[j]
        npair = n_s[j:j + 1, :]
        Nrow2 = jnp.concatenate(
            [jnp.broadcast_to(jnp.where(lo128, npair, 0.0), (LANES, LANES)),
             jnp.broadcast_to(jnp.where(lo128, 0.0, npair), (LANES, LANES))], 0).astype(BF16)
        if scale_q:
            qa = (q128.astype(F32) * pair128(A)).astype(BF16)
            num = _dot(qa, C2.astype(BF16)) + num_intra
            den = _dot_nt(qa, Nrow2) + rowsum
        else:
            a256 = pair256(A)
            num = a256 * _dot(q128, C2.astype(BF16)) + num_intra
            den = a256 * _dot_nt(q128, Nrow2) + rowsum
        h = num / jnp.maximum(jnp.abs(den), pair256(EXPM))
        h_ref[:, p256] = h.astype(h_ref.dtype)

        kw = k128.astype(F32) * pair128(WST)
        n_s[j:j + 1, :] = pair128(ast) * npair + jnp.sum(kw, axis=0, keepdims=True)
        kwT = pad_rows(kw, LANES).T.astype(BF16)
        dC = _dot(kwT, pad_rows(v256, LANES))
        c2_s[j] = pair256(ast) * C2 + jnp.where(blockdiag, dC, 0.0)

    m_s[...] = mnew


def _mlstm_store_state(cout_ref, nout_ref, mout_ref, c2_s, n_s, m_s):
    for j in range(M_HEADS // 2):
        C2 = c2_s[j]
        cout_ref[2 * j] = C2[0:M_DK, 0:M_DV]
        cout_ref[2 * j + 1] = C2[M_DK:2 * M_DK, M_DV:2 * M_DV]
    nout_ref[...] = n_s[...]
    mout_ref[...] = m_s[...]


MLSTM_SEQS_PER_STEP = 4


def _mlstm_call(qk, v, li, lf, c0, n0, m0, B, nc, L):
    npairs = M_HEADS // 2
    nseq = min(B, MLSTM_SEQS_PER_STEP)
    T = nc * L
    tok3 = lambda a: a.reshape(B, T, a.shape[-1])
    tok = lambda b, c: (b, c, 0)
    st4 = lambda b, c: (b, 0, 0, 0)
    st3 = lambda b, c: (b, 0, 0)
    h, c_new, n_new, m_new = pl.pallas_call(
        functools.partial(_mlstm_kernel, L, nseq, nc == 1),
        out_shape=(jax.ShapeDtypeStruct((B, T, V_COLS), BF16),
                   jax.ShapeDtypeStruct((B, M_HEADS, M_DK, M_DV), F32),
                   jax.ShapeDtypeStruct((B, npairs, LANES), F32),
                   jax.ShapeDtypeStruct((B, 1, LANES), F32)),
        grid=(B // nseq, nc),
        in_specs=[pl.BlockSpec((nseq, L, QK_COLS), tok), pl.BlockSpec((nseq, L, V_COLS), tok),
                  pl.BlockSpec((nseq, L, LANES), tok), pl.BlockSpec((nseq, L, LANES), tok),
                  pl.BlockSpec((nseq, M_HEADS, M_DK, M_DV), st4),
                  pl.BlockSpec((nseq, npairs, LANES), st3),
                  pl.BlockSpec((nseq, 1, LANES), st3)],
        out_specs=(pl.BlockSpec((nseq, L, V_COLS), tok),
                   pl.BlockSpec((nseq, M_HEADS, M_DK, M_DV), st4),
                   pl.BlockSpec((nseq, npairs, LANES), st3),
                   pl.BlockSpec((nseq, 1, LANES), st3)),
        scratch_shapes=[pltpu.VMEM((nseq, npairs, 2 * M_DK, 2 * M_DV), F32),
                        pltpu.VMEM((nseq, npairs, LANES), F32),
                        pltpu.VMEM((nseq, 1, LANES), F32)],
        compiler_params=_params(2),
        name="mlstm_chunks",
    )(tok3(qk), tok3(v), tok3(li), tok3(lf), c0, n0, m0)
    return h.reshape(B * T, V_COLS), c_new, n_new, m_new


def _route(lg):
    lane = lax.broadcasted_iota(jnp.int32, lg.shape, 1)
    lanef = lane.astype(F32)
    neg = -jnp.inf
    far = float(LANES)
    gm = lane < N_GROUPS
    lgm = jnp.where(gm, lg, neg)
    gmax = jnp.max(lgm, axis=-1, keepdims=True)
    gsum = jnp.sum(jnp.exp(lgm - gmax), axis=-1, keepdims=True)
    g_w = 1.0 / gsum
    gidx = jnp.min(jnp.where(gm & (lg == gmax), lanef, far), axis=-1, keepdims=True)
    first = N_GROUPS + EXPERTS_PER_GROUP * gidx
    sel = (lanef >= first) & (lanef < first + EXPERTS_PER_GROUP)
    l1 = jnp.max(jnp.where(sel, lg, neg), axis=-1, keepdims=True)
    i1 = jnp.min(jnp.where(sel & (lg == l1), lanef, far), axis=-1, keepdims=True)
    sel2 = sel & (lanef != i1)
    l2 = jnp.max(jnp.where(sel2, lg, neg), axis=-1, keepdims=True)
    i2 = jnp.min(jnp.where(sel2 & (lg == l2), lanef, far), axis=-1, keepdims=True)
    r = jnp.exp(l2 - l1)
    w1 = g_w / (1.0 + r)
    w2 = w1 * r
    return jnp.where(lanef == i1, w1, jnp.where(lanef == i2, w2, 0.0)), gidx


def _post_kernel(mlstm, routed, expand, *refs):
    refs = list(refs)
    h_ref = refs.pop(0)
    o_ref = refs.pop(0) if mlstm else None
    x_ref, g1_ref, sh2_ref, sc2_ref, nf_ref = refs[:5]
    refs = refs[5:]
    hn_ref = refs.pop(0) if mlstm else None
    wout_ref, wrh_ref, wrl_ref, br_ref = refs[:4]
    refs = refs[4:]
    tri_ref = refs.pop(0) if routed else None
    rep_ref = refs.pop(0) if expand else None
    x1_ref, u2_ref, gate_ref = refs[:3]
    outs = refs[3:]
    if mlstm:
        hf = h_ref[...].astype(F32)
        parts = [_unit_rms(hf[:, M_DV * i:M_DV * (i + 1)]) for i in range(M_HEADS)]
        hn = jnp.concatenate(parts, 1) * hn_ref[...]
        hg = (hn * o_ref[...].astype(F32)).astype(BF16)
    else:
        hg = h_ref[...]
    x1 = x_ref[...] + _mod_rows(g1_ref, rep_ref) * _dot(hg, wout_ref[...])
    x1_ref[...] = x1
    u2 = _unit_rms(x1) * (nf_ref[...] * (1.0 + _mod_rows(sc2_ref, rep_ref))) + _mod_rows(sh2_ref, rep_ref)
    uh, ul = _split2(u2)
    lg = _dot(uh, wrh_ref[...]) + _dot(ul, wrh_ref[...]) + _dot(uh, wrl_ref[...]) + br_ref[...]
    gate, gidx = _route(lg)
    u2_ref[...] = uh
    gate_ref[...] = gate
    if not routed:
        return
    gid_ref, col_ref = outs
    rows = []
    for blk in range(x1.shape[0] // LANES):
        col = jnp.broadcast_to(gidx[LANES * blk:LANES * (blk + 1), :], (LANES, LANES))
        rows.append(col.T[0:1, :])
    gid_ref[...] = jnp.concatenate(rows, 1)
    lanef = lax.broadcasted_iota(jnp.int32, gate.shape, 1).astype(F32)
    onehot = jnp.where(lanef == gidx, 1.0, 0.0)
    earlier = _dot(tri_ref[...], onehot.astype(BF16))
    rank = jnp.sum(onehot * earlier, axis=-1, keepdims=True)
    col_ref[...] = jnp.where(lanef == 0.0, gidx, jnp.where(lanef == 1.0, rank, 0.0))


def _post_call(mlstm, routed, h, o, x, mod, mspec, rep, nf, hn, wout, wrh, wrl, br, tm):
    n = x.shape[0]
    expand = rep is not None
    row = lambda i: (i, 0)
    const = lambda i: (0, 0)
    tok = pl.BlockSpec((tm, D_MODEL), row)
    small = pl.BlockSpec((tm, LANES), row)
    vec = pl.BlockSpec((1, D_MODEL), const)
    ins = [h] + ([o] if mlstm else []) + [x, mod, mod, mod, nf] + ([hn] if mlstm else []) \
        + [wout, wrh, wrl, br]
    specs = [tok] + ([tok] if mlstm else []) + [tok, mspec(2), mspec(3), mspec(4), vec] \
        + ([vec] if mlstm else []) \
        + [pl.BlockSpec((D_MODEL, D_MODEL), const), pl.BlockSpec((D_MODEL, LANES), const),
           pl.BlockSpec((D_MODEL, LANES), const), pl.BlockSpec((1, LANES), const)]
    out_shape = [jax.ShapeDtypeStruct((n, D_MODEL), F32), jax.ShapeDtypeStruct((n, D_MODEL), BF16),
                 jax.ShapeDtypeStruct((n, LANES), F32)]
    out_specs = [tok, tok, small]
    if routed:
        t = jnp.arange(tm)
        ins.append((t[None, :] < t[:, None]).astype(BF16))
        specs.append(pl.BlockSpec((tm, tm), const))
        out_shape += [jax.ShapeDtypeStruct((n // tm, 1, tm), F32), jax.ShapeDtypeStruct((n, LANES), F32)]
        out_specs += [pl.BlockSpec((None, 1, tm), lambda i: (i, 0, 0)), small]
    if expand:
        ins.append(rep)
        specs.append(pl.BlockSpec(rep.shape, const))
    out_shape, out_specs = tuple(out_shape), tuple(out_specs)
    return pl.pallas_call(
        functools.partial(_post_kernel, mlstm, routed, expand),
        out_shape=out_shape,
        grid=(n // tm,),
        in_specs=specs,
        out_specs=out_specs,
        compiler_params=_params(1),
        name="post_mlstm" if mlstm else "post_attn",
    )(*ins)


ITEM_ROWS = 8
RUN_ALIGN = 16
RUN_SIZES = (512, 256, 128, 64, 32, 16)
RUN_SRC, RUN_DST, RUN_LEN, RUN_TOTAL = 0, N_GROUPS, 2 * N_GROUPS, 3 * N_GROUPS


def _stage_rows(tm):
    assert N_GROUPS * (RUN_ALIGN - 1) <= LANES
    return tm + LANES


def _sorted_rows(n, tm):
    rows = n + N_GROUPS * (n // tm) * RUN_ALIGN + tm
    return -(-rows // tm) * tm


def _sort_meta_kernel(tm, nts, gid_ref, q_ref, runs_ref, items_ref):
    ntiles = gid_ref.shape[0]
    gid = gid_ref[...]
    r = lax.broadcasted_iota(jnp.int32, (tm, tm), 0)
    c = lax.broadcasted_iota(jnp.int32, (tm, tm), 1)
    before = jnp.where(r < c, 1.0, 0.0).astype(BF16)
    trow = lax.broadcasted_iota(jnp.int32, (ntiles, LANES), 0)
    lane = lax.broadcasted_iota(jnp.int32, (ntiles, LANES), 1)
    k = lax.broadcasted_iota(jnp.int32, (1, LANES), 1).astype(F32)
    zero11 = jnp.zeros((1, 1), F32)
    q = jnp.zeros((ntiles, tm), F32)
    runs = jnp.zeros((ntiles, LANES), F32)
    src = jnp.zeros((ntiles, 1), F32)
    start, nitems = zero11, zero11
    grp = jnp.zeros((1, LANES), F32)
    tile = jnp.zeros((1, LANES), F32)
    valid = jnp.zeros((1, LANES), F32)
    for g in range(N_GROUPS):
        mask = jnp.where(gid == float(g), 1.0, 0.0)
        cnt = jnp.sum(mask, axis=1, keepdims=True)
        padded = jnp.floor((cnt + (RUN_ALIGN - 1.0)) * (1.0 / RUN_ALIGN)) * RUN_ALIGN
        incl = jnp.broadcast_to(padded, (ntiles, LANES))
        d = 1
        while d < ntiles:
            incl = incl + jnp.where(trow >= d, pltpu.roll(incl, d, axis=0), 0.0)
            d *= 2
        total = incl[ntiles - 1:ntiles, 0:1]
        dst = start + incl[:, 0:1] - padded
        q = q + mask * (src + _dot(mask.astype(BF16), before))
        runs = runs + jnp.where(lane == RUN_SRC + g, src, 0.0) + jnp.where(lane == RUN_DST + g, dst, 0.0) \
            + jnp.where(lane == RUN_LEN + g, padded, 0.0)
        src = src + padded
        end = start + total
        ft = jnp.floor(start * (1.0 / tm))
        lt = jnp.floor((end - 1.0) * (1.0 / tm))
        ni = jnp.where(total > 0.0, lt - ft + 1.0, 0.0)
        inside = (k >= nitems) & (k < nitems + ni)
        grp = grp + jnp.where(inside, float(g), 0.0)
        tile = tile + jnp.where(inside, ft + (k - nitems), 0.0)
        valid = valid + jnp.where(inside, 1.0, 0.0)
        nitems = nitems + ni
        start = end
    runs = runs + jnp.where(lane == RUN_TOTAL, start, 0.0)
    live = valid > 0.0
    prev = pltpu.roll(tile, 1, axis=1)
    nxt = pltpu.roll(tile, LANES - 1, axis=1)
    first = live & ((k == 0.0) | (tile != prev))
    last = live & ((k == nitems - 1.0) | (tile != nxt))
    spare = jnp.floor((start - 1.0) * (1.0 / tm)) + 1.0 + (k - nitems)
    fill = jnp.logical_not(live) & (spare <= nts - 1.0)
    tile = jnp.where(live, tile, jnp.minimum(spare, nts - 1.0))
    grp = jnp.where(live, grp, jnp.max(grp, axis=1, keepdims=True))
    flag = lambda m: jnp.where(m, 1.0, 0.0)
    q_ref[...] = q
    runs_ref[...] = runs.astype(jnp.int32)
    table = jnp.concatenate([tile, grp, valid, flag(first | fill), flag(last | fill),
                             jnp.zeros((ITEM_ROWS - 5, LANES), F32)], 0)
    items_ref[...] = table.astype(jnp.int32)


def _max_items(nts):
    return nts + N_GROUPS - 1


def _sort_meta_call(gid, tm):
    ntiles = gid.shape[0]
    nts = _sorted_rows(ntiles * tm, tm) // tm
    assert _max_items(nts) <= LANES and tm <= RUN_SIZES[0]
    return pl.pallas_call(
        functools.partial(_sort_meta_kernel, tm, nts),
        out_shape=(jax.ShapeDtypeStruct((ntiles, tm), F32),
                   jax.ShapeDtypeStruct((ntiles, LANES), jnp.int32),
                   jax.ShapeDtypeStruct((ITEM_ROWS, LANES), jnp.int32)),
        compiler_params=_params(0),
        name="moe_sort_meta",
    )(gid)


def _run_copies(runs_ref, i, tile_ref, sorted_hbm, sem, to_sorted):
    pieces = []
    for g in range(N_GROUPS):
        src = runs_ref[i, RUN_SRC + g]
        dst = runs_ref[i, RUN_DST + g]
        length = runs_ref[i, RUN_LEN + g]
        for s in RUN_SIZES:
            def build(src=src, dst=dst, length=length, s=s):
                off = length & (-2 * s)
                a = tile_ref.at[pl.ds(pl.multiple_of(src + off, RUN_ALIGN), s), :]
                b = sorted_hbm.at[pl.ds(pl.multiple_of(dst + off, RUN_ALIGN), s), :]
                return pltpu.make_async_copy(a, b, sem) if to_sorted else pltpu.make_async_copy(b, a, sem)
            pieces.append(((length & s) != 0, build))
    return pieces


def _start(pieces):
    for pred, build in pieces:
        pl.when(pred)(lambda build=build: build().start())


def _wait(pieces):
    for pred, build in pieces:
        pl.when(pred)(lambda build=build: build().wait())


def _start_then_wait(pieces):
    _start(pieces)
    _wait(pieces)


def _scatter_kernel(tm, nts, ntiles, runs_ref, q_ref, u_ref, g_ref, su_hbm, sg_hbm,
                    stu, stg, sem_u, sem_g):
    i = pl.program_id(0)
    slot = i % 2

    def copies(tile, slot):
        return (_run_copies(runs_ref, tile, stu.at[slot], su_hbm, sem_u.at[slot], True)
                + _run_copies(runs_ref, tile, stg.at[slot], sg_hbm, sem_g.at[slot], True))

    r = lax.broadcasted_iota(jnp.int32, (_stage_rows(tm), tm), 0).astype(F32)
    perm = jnp.where(q_ref[...] == r, 1.0, 0.0).astype(BF16)
    stu[slot] = _dot(perm, u_ref[...]).astype(BF16)
    g3 = _split3(g_ref[...])
    stg[slot] = _dot(perm, g3[0]) + _dot(perm, g3[1]) + _dot(perm, g3[2])
    _start(copies(i, slot))

    @pl.when(i > 0)
    def _():
        _wait(copies(i - 1, 1 - slot))

    @pl.when(i == ntiles - 1)
    def _():
        _wait(copies(i, slot))
        stu[slot] = jnp.zeros(stu.shape[1:], stu.dtype)
        stg[slot] = jnp.zeros(stg.shape[1:], stg.dtype)
        total = runs_ref[i, RUN_TOTAL]
        tail = nts * tm - total
        nfull = tail // tm
        pieces = []
        for tile_ref, hbm, sem in ((stu.at[slot], su_hbm, sem_u.at[slot]),
                                   (stg.at[slot], sg_hbm, sem_g.at[slot])):
            for j in range(nts - ntiles):
                def full(j=j, tile_ref=tile_ref, hbm=hbm, sem=sem):
                    dst = pl.multiple_of(total + j * tm, RUN_ALIGN)
                    return pltpu.make_async_copy(tile_ref.at[pl.ds(0, tm), :], hbm.at[pl.ds(dst, tm), :], sem)
                pieces.append((j < nfull, full))
            rem = tail - nfull * tm
            for s in RUN_SIZES:
                if s >= tm:
                    continue
                def part(s=s, tile_ref=tile_ref, hbm=hbm, sem=sem):
                    dst = pl.multiple_of(total + nfull * tm + (rem & (-2 * s)), RUN_ALIGN)
                    return pltpu.make_async_copy(tile_ref.at[pl.ds(0, s), :], hbm.at[pl.ds(dst, s), :], sem)
                pieces.append(((rem & s) != 0, part))
        _start_then_wait(pieces)


def _scatter_call(runs, q3, u2, gate, tm):
    n = u2.shape[0]
    ns = _sorted_rows(n, tm)
    grid_spec = pltpu.PrefetchScalarGridSpec(
        num_scalar_prefetch=1,
        grid=(n // tm,),
        in_specs=[pl.BlockSpec((None, 1, tm), lambda i, *_: (i, 0, 0)),
                  pl.BlockSpec((tm, D_MODEL), lambda i, *_: (i, 0)),
                  pl.BlockSpec((tm, LANES), lambda i, *_: (i, 0))],
        out_specs=(pl.BlockSpec(memory_space=pl.ANY), pl.BlockSpec(memory_space=pl.ANY)),
        scratch_shapes=[pltpu.VMEM((2, _stage_rows(tm), D_MODEL), BF16),
                        pltpu.VMEM((2, _stage_rows(tm), LANES), F32),
                        pltpu.SemaphoreType.DMA((2,)), pltpu.SemaphoreType.DMA((2,))])
    return pl.pallas_call(
        functools.partial(_scatter_kernel, tm, ns // tm, n // tm),
        out_shape=(jax.ShapeDtypeStruct((ns, D_MODEL), BF16), jax.ShapeDtypeStruct((ns, LANES), F32)),
        grid_spec=grid_spec,
        compiler_params=_params(1),
        name="moe_scatter",
    )(runs, q3, u2, gate)


def _ungroup(tm, runs_ref, col_ref, ys_hbm, stage, sem):
    i = pl.program_id(0)
    slot = i % 2

    def copies(tile, slot):
        return _run_copies(runs_ref, tile, stage.at[slot], ys_hbm, sem.at[slot], False)

    @pl.when(i == 0)
    def _():
        stage[...] = jnp.zeros_like(stage)
        _start(copies(i, slot))

    @pl.when(i + 1 < pl.num_programs(0))
    def _():
        _start(copies(i + 1, 1 - slot))

    _wait(copies(i, slot))
    col = col_ref[...]
    gidx = col[:, 0:1]
    src = [runs_ref[i, RUN_SRC + g].astype(F32) for g in range(N_GROUPS)]
    first = src[N_GROUPS - 1]
    for g in range(N_GROUPS - 2, -1, -1):
        first = jnp.where(gidx == float(g), src[g], first)
    q = first + col[:, 1:2]
    lanes = lax.broadcasted_iota(jnp.int32, (tm, _stage_rows(tm)), 1).astype(F32)
    perm_t = jnp.where(q == lanes, 1.0, 0.0).astype(BF16)
    return _dot(perm_t, stage[slot])


def _moe_routed_kernel(items_ref, su_ref, sg_ref, wgu_ref, wd_ref, out_ref, acc_ref, wgu_b, wd_b):
    kk = pl.program_id(0)
    grp = items_ref[1, kk]

    @pl.when((kk == 0) | (grp != items_ref[1, jnp.maximum(kk - 1, 0)]))
    def _():
        for e in range(EXPERTS_PER_GROUP):
            wgu_b[e] = wgu_ref[e].astype(BF16)
            wd_b[e] = wd_ref[e].astype(BF16)

    @pl.when(items_ref[3, kk] == 1)
    def _():
        acc_ref[...] = jnp.zeros_like(acc_ref)

    @pl.when(items_ref[2, kk] == 1)
    def _():
        u = su_ref[...]
        gate = sg_ref[...]
        lane = lax.broadcasted_iota(jnp.int32, gate.shape, 1)
        first_lane = N_GROUPS + EXPERTS_PER_GROUP * grp
        acts = []
        for e in range(EXPERTS_PER_GROUP):
            gcol = jnp.sum(jnp.where(lane == first_lane + e, gate, 0.0), axis=-1, keepdims=True)
            hu = _dot(u, wgu_b[e])
            hg = hu[:, 0:D_EXPERT]
            acts.append((hg * jax.nn.sigmoid(hg) * hu[:, D_EXPERT:2 * D_EXPERT] * gcol).astype(BF16))
        wd_all = wd_b[...].reshape(EXPERTS_PER_GROUP * D_EXPERT, D_MODEL)
        acc_ref[...] += _dot(jnp.concatenate(acts, 1), wd_all)

    @pl.when(items_ref[4, kk] == 1)
    def _():
        out_ref[...] = acc_ref[...].astype(out_ref.dtype)


def _moe_routed_call(items, su, sg, wgu, wd, layer, tm):
    ns = su.shape[0]
    n_items = _max_items(ns // tm)
    depth = wgu.shape[0]
    wgu5 = wgu.reshape(depth, N_GROUPS, EXPERTS_PER_GROUP, D_MODEL, 2 * D_EXPERT)
    wd5 = wd.reshape(depth, N_GROUPS, EXPERTS_PER_GROUP, D_EXPERT, D_MODEL)
    grid_spec = pltpu.PrefetchScalarGridSpec(
        num_scalar_prefetch=1,
        grid=(n_items,),
        in_specs=[pl.BlockSpec((tm, D_MODEL), lambda k, it: (it[0, k], 0)),
                  pl.BlockSpec((tm, LANES), lambda k, it: (it[0, k], 0)),
                  pl.BlockSpec((None, None, EXPERTS_PER_GROUP, D_MODEL, 2 * D_EXPERT),
                               lambda k, it: (layer, it[1, k], 0, 0, 0)),
                  pl.BlockSpec((None, None, EXPERTS_PER_GROUP, D_EXPERT, D_MODEL),
                               lambda k, it: (layer, it[1, k], 0, 0, 0))],
        out_specs=pl.BlockSpec((tm, D_MODEL), lambda k, it: (it[0, k], 0)),
        scratch_shapes=[pltpu.VMEM((tm, D_MODEL), F32),
                        pltpu.VMEM((EXPERTS_PER_GROUP, D_MODEL, 2 * D_EXPERT), BF16),
                        pltpu.VMEM((EXPERTS_PER_GROUP, D_EXPERT, D_MODEL), BF16)])
    return pl.pallas_call(
        _moe_routed_kernel,
        out_shape=jax.ShapeDtypeStruct((ns, D_MODEL), BF16),
        grid_spec=grid_spec,
        compiler_params=_params(1),
        name="moe_routed",
    )(items, su, sg, wgu5, wd5)


def _resid_gather_kernel(tm, runs_ref, x_ref, g2_ref, col_ref, ys_hbm, y_ref, stage, sem):
    y_ref[...] = x_ref[...] + g2_ref[...] * _ungroup(tm, runs_ref, col_ref, ys_hbm, stage, sem)


def _resid_gather_call(x, runs, col, ys, mod, mspec, tm):
    n = x.shape[0]
    row = lambda i, *_: (i, 0)
    grid_spec = pltpu.PrefetchScalarGridSpec(
        num_scalar_prefetch=1,
        grid=(n // tm,),
        in_specs=[pl.BlockSpec((tm, D_MODEL), row), mspec(5), pl.BlockSpec((tm, LANES), row),
                  pl.BlockSpec(memory_space=pl.ANY)],
        out_specs=pl.BlockSpec((tm, D_MODEL), row),
        scratch_shapes=[pltpu.VMEM((2, _stage_rows(tm), D_MODEL), BF16), pltpu.SemaphoreType.DMA((2,))])
    return pl.pallas_call(
        functools.partial(_resid_gather_kernel, tm),
        out_shape=jax.ShapeDtypeStruct((n, D_MODEL), F32),
        grid_spec=grid_spec,
        compiler_params=_params(1),
        name="moe_unsort_resid",
    )(runs, x, mod, col, ys)


def _moe_kernel(expand, u_ref, gate_ref, wgu_ref, wd_ref, x1_ref, g2_ref, *rest):
    rep_ref = rest[0] if expand else None
    y_ref, acc_ref = rest[1 if expand else 0:]
    e = pl.program_id(1)

    @pl.when(e == 0)
    def _():
        acc_ref[...] = jnp.zeros_like(acc_ref)

    hu = _dot(u_ref[...], wgu_ref[...].astype(BF16))
    hg = hu[:, 0:D_EXPERT]
    hv = hu[:, D_EXPERT:2 * D_EXPERT]
    gate = gate_ref[...]
    lane = lax.broadcasted_iota(jnp.int32, gate.shape, 1)
    gcol = jnp.sum(jnp.where(lane == N_GROUPS + e, gate, 0.0), axis=-1, keepdims=True)
    act = hg * jax.nn.sigmoid(hg) * hv * gcol
    acc_ref[...] += _dot(act.astype(BF16), wd_ref[...].astype(BF16))

    @pl.when(e == N_EXPERTS - 1)
    def _():
        y_ref[...] = x1_ref[...] + _mod_rows(g2_ref, rep_ref) * acc_ref[...]


def _moe_call(u2, gate, wgu, wd, layer, x1, mod, mspec, rep, tm):
    n = u2.shape[0]
    row = lambda i, e: (i, 0)
    expand = rep is not None
    return pl.pallas_call(
        functools.partial(_moe_kernel, expand),
        out_shape=jax.ShapeDtypeStruct((n, D_MODEL), F32),
        grid=(n // tm, N_EXPERTS),
        in_specs=[pl.BlockSpec((tm, D_MODEL), row), pl.BlockSpec((tm, LANES), row),
                  pl.BlockSpec((None, None, D_MODEL, 2 * D_EXPERT), lambda i, e: (layer, e, 0, 0)),
                  pl.BlockSpec((None, None, D_EXPERT, D_MODEL), lambda i, e: (layer, e, 0, 0)),
                  pl.BlockSpec((tm, D_MODEL), row), mspec(5)]
        + ([pl.BlockSpec(rep.shape, lambda i, e: (0, 0))] if expand else []),
        out_specs=pl.BlockSpec((tm, D_MODEL), row),
        scratch_shapes=[pltpu.VMEM((tm, D_MODEL), F32)],
        compiler_params=_params(2),
        name="moe_dense",
    )(u2, gate, wgu, wd, x1, mod, *([rep] if expand else []))


def _rope128(x, cos, sa, sb):
    return x * cos + pltpu.roll(x, LANES - ROT_DIM // 2, axis=1) * sa \
        + pltpu.roll(x, ROT_DIM // 2, axis=1) * sb


def _kvq_kernel(gather_tm, *refs):
    if gather_tm:
        (runs_ref, x_ref, g2_ref, col_ref, ys_hbm, kvsh_ref, kvsc_ref, sh1_ref, sc1_ref, kvn_ref,
         nm_ref, wkv_ref, wq_ref, kn_ref, qn_ref, cos_ref, sa_ref, sb_ref, g64_ref, gr_ref, gb_ref,
         q_ref, k_ref, v_ref, xa_ref, stage, sem) = refs
        x = x_ref[...] + g2_ref[...] * _ungroup(gather_tm, runs_ref, col_ref, ys_hbm, stage, sem)
        xa_ref[...] = x
        rep_ref = None
    else:
        (x_ref, kvsh_ref, kvsc_ref, sh1_ref, sc1_ref, kvn_ref, nm_ref,
         wkv_ref, wq_ref, kn_ref, qn_ref, cos_ref, sa_ref, sb_ref, g64_ref, gr_ref, gb_ref,
         rep_ref, q_ref, k_ref, v_ref) = refs
        x = x_ref[...]
    xn = _unit_rms(x)
    cos, sa, sb = cos_ref[...], sa_ref[...], sb_ref[...]

    ukv = xn * (kvn_ref[...] * (1.0 + _mod_rows(kvsc_ref, rep_ref))) + _mod_rows(kvsh_ref, rep_ref)
    kv = _dot(ukv.astype(BF16), wkv_ref[...])
    k = kv[:, 0:LANES]
    v_ref[...] = kv[:, LANES:2 * LANES]
    kh, kl = _split2(k * k)
    ms = _dot(kh, g64_ref[...]) + _dot(kl, g64_ref[...])
    k_ref[...] = _rope128(k * lax.rsqrt(ms + RMS_EPS) * kn_ref[...], cos, sa, sb)

    u1 = xn * (nm_ref[...] * (1.0 + _mod_rows(sc1_ref, rep_ref))) + _mod_rows(sh1_ref, rep_ref)
    q = _dot(u1.astype(BF16), wq_ref[...])
    ms16 = _dot((q * q).astype(BF16), gr_ref[...])
    rh, rl = _split2(lax.rsqrt(ms16 + RMS_EPS))
    rsb = _dot(rh, gb_ref[...]) + _dot(rl, gb_ref[...])
    qn = q * rsb * qn_ref[...]
    scale = HEAD_DIM ** -0.5
    cos_q, sa_q, sb_q = cos * scale, sa * scale, sb * scale
    for i in range(D_MODEL // LANES):
        sl = slice(LANES * i, LANES * (i + 1))
        q_ref[:, sl] = _rope128(qn[:, sl], cos_q, sa_q, sb_q).astype(BF16)


def _kvq_call(x, routed, mod0, mod1, mspec, kvmod, kvspec, rep, kvn, nm, wkv, wq, kn, qn, tabs, tab,
              g64, gr, gb, tm):
    n = x.shape[0]
    row = lambda i, *_: (i, 0)
    const = lambda i, *_: (0, 0)
    vec = pl.BlockSpec((1, D_MODEL), const)
    tok = pl.BlockSpec((tm, D_MODEL), row)
    small = pl.BlockSpec((tm, LANES), row)
    ins = [x, kvmod, kvmod, mod1, mod1, kvn, nm, wkv, wq, kn, qn, *tabs, g64, gr, gb]
    specs = [tok, kvspec(0), kvspec(1), mspec(0), mspec(1), vec, vec,
             pl.BlockSpec((D_MODEL, 2 * LANES), const), pl.BlockSpec((D_MODEL, D_MODEL), const),
             pl.BlockSpec((1, LANES), const), vec, tab, tab, tab,
             pl.BlockSpec((LANES, LANES), const), pl.BlockSpec((D_MODEL, LANES), const),
             pl.BlockSpec((LANES, D_MODEL), const)]
    out_shape = [jax.ShapeDtypeStruct((n, D_MODEL), BF16), jax.ShapeDtypeStruct((n, LANES), F32),
                 jax.ShapeDtypeStruct((n, LANES), F32)]
    out_specs = [tok, small, small]
    scratch = []
    nprefetch = 0
    if routed is not None:
        runs, col, ys = routed
        nprefetch = 1
        ins = [runs, x, mod0, col, ys] + ins[1:]
        specs = [tok, mspec(5), small, pl.BlockSpec(memory_space=pl.ANY)] + specs[1:]
        out_shape.append(jax.ShapeDtypeStruct((n, D_MODEL), F32))
        out_specs.append(tok)
        scratch = [pltpu.VMEM((2, _stage_rows(tm), D_MODEL), BF16), pltpu.SemaphoreType.DMA((2,))]
    else:
        ins.append(rep)
        specs.append(pl.BlockSpec(rep.shape, const))
    grid_spec = pltpu.PrefetchScalarGridSpec(
        num_scalar_prefetch=nprefetch, grid=(n // tm,), in_specs=specs,
        out_specs=tuple(out_specs), scratch_shapes=scratch)
    return pl.pallas_call(
        functools.partial(_kvq_kernel, tm if routed is not None else 0),
        out_shape=tuple(out_shape),
        grid_spec=grid_spec,
        compiler_params=_params(1),
        name="kv_q_proj",
    )(*ins)


def _attn_core(q, kcat, vcat, bias, sinks_ref, o_ref):
    tk = kcat.shape[0]
    pairs = N_Q_HEADS // N_KV_HEADS // 2
    lane = lax.broadcasted_iota(jnp.int32, (1, LANES), 1)
    lo = lane < HEAD_DIM
    kro = pltpu.roll(kcat, HEAD_DIM, axis=1)
    vro = pltpu.roll(vcat, HEAD_DIM, axis=1)
    one_e = jnp.broadcast_to(jnp.where(lo, 1.0, 0.0), (tk, LANES))
    one_o = 1.0 - one_e
    for g in range(N_KV_HEADS):
        if g == 0:
            ke, ko = jnp.where(lo, kcat, 0.0), jnp.where(lo, 0.0, kro)
            ve, vo = jnp.where(lo, vcat, 0.0), jnp.where(lo, 0.0, vro)
        else:
            ke, ko = jnp.where(lo, kro, 0.0), jnp.where(lo, 0.0, kcat)
            ve, vo = jnp.where(lo, vro, 0.0), jnp.where(lo, 0.0, vcat)
        k2 = jnp.concatenate([ke, ko], 0).astype(BF16)
        v2 = jnp.concatenate([jnp.concatenate([ve, one_e], 1),
                              jnp.concatenate([vo, one_o], 1)], 0).astype(BF16)
        for p in range(pairs):
            hp = g * pairs + p
            s = _dot_nt(q[:, LANES * hp:LANES * (hp + 1)], k2)
            halves, corr = [], []
            for par in range(2):
                sp = s[:, par * tk:(par + 1) * tk] + bias
                sink = sinks_ref[2 * hp + par]
                m = jnp.maximum(jnp.max(sp, axis=-1, keepdims=True), sink)
                halves.append(jnp.exp(sp - m).astype(BF16))
                corr.append(jnp.exp(sink - m))
            o2 = _dot(jnp.concatenate(halves, 1), v2)
            den = o2[:, LANES:2 * LANES] + jnp.where(lo, corr[0], corr[1])
            o_ref[:, LANES * hp:LANES * (hp + 1)] = (o2[:, 0:LANES] / den).astype(o_ref.dtype)


ATTN_BLOCKS_PER_STEP = 4


def _attn_prompt_kernel(sinks_ref, q_ref, kp_ref, kc_ref, vp_ref, vc_ref, bias_ref, o_ref):
    kall = jnp.concatenate([kp_ref[...], kc_ref[...]], 0)
    vall = jnp.concatenate([vp_ref[...], vc_ref[...]], 0)
    first = jnp.minimum(pl.program_id(1), 1)
    for j in range(ATTN_BLOCKS_PER_STEP):
        rows = slice(j * WINDOW, (j + 1) * WINDOW)
        keys = slice(j * WINDOW, (j + 2) * WINDOW)
        bias = bias_ref[first] if j == 0 else bias_ref[1]
        _attn_core(q_ref[rows, :], kall[keys, :], vall[keys, :], bias, sinks_ref, o_ref.at[rows, :])


ATTN_SEQS_PER_STEP = 8


def _attn_sample_kernel(sinks_ref, q_ref, kc_ref, kn_ref, vc_ref, vn_ref, bias_ref, o_ref, kbuf, vbuf):
    @pl.when(pl.program_id(0) == 0)
    def _():
        kbuf[...] = jnp.zeros_like(kbuf)
        vbuf[...] = jnp.zeros_like(vbuf)

    for b in range(ATTN_SEQS_PER_STEP):
        kbuf[b, 0:WINDOW, :] = kc_ref[b]
        kbuf[b, WINDOW:WINDOW + SAMPLE_PAD, :] = kn_ref[b]
        vbuf[b, 0:WINDOW, :] = vc_ref[b]
        vbuf[b, WINDOW:WINDOW + SAMPLE_PAD, :] = vn_ref[b]
    for b in range(ATTN_SEQS_PER_STEP):
        _attn_core(q_ref[b], kbuf[b], vbuf[b], bias_ref[...], sinks_ref, o_ref.at[b])


def _window_bias(tq, first):
    qi = jnp.arange(tq)[:, None]
    kj = jnp.arange(2 * WINDOW)[None, :]
    ok = (kj > qi) & (kj <= qi + WINDOW)
    if first:
        ok = ok & (kj >= WINDOW)
    return jnp.where(ok, 0.0, -jnp.inf).astype(F32)


def _attn_prompt_call(sinks, q, k, v, B, nb):
    n = q.shape[0]
    per = ATTN_BLOCKS_PER_STEP
    steps = nb // per
    cur = lambda b, i: (b * steps + i, 0)
    prev = lambda b, i: (b * nb + jnp.maximum(per * i - 1, 0), 0)
    bias = jnp.stack([_window_bias(WINDOW, True), _window_bias(WINDOW, False)])
    kv_prev = pl.BlockSpec((WINDOW, LANES), prev)
    kv_cur = pl.BlockSpec((per * WINDOW, LANES), cur)
    return pl.pallas_call(
        _attn_prompt_kernel,
        out_shape=jax.ShapeDtypeStruct((n, D_MODEL), BF16),
        grid=(B, steps),
        in_specs=[pl.BlockSpec(memory_space=pltpu.SMEM),
                  pl.BlockSpec((per * WINDOW, D_MODEL), cur), kv_prev, kv_cur, kv_prev, kv_cur,
                  pl.BlockSpec((2, WINDOW, 2 * WINDOW), lambda b, i: (0, 0, 0))],
        out_specs=pl.BlockSpec((per * WINDOW, D_MODEL), cur),
        compiler_params=_params(2),
        name="swa_prompt",
    )(sinks, q, k, k, v, v, bias)


def _attn_sample_call(sinks, q, kcache, knew, vcache, vnew):
    B = q.shape[0]
    nseq = ATTN_SEQS_PER_STEP
    b3 = lambda b: (b, 0, 0)
    cache = pl.BlockSpec((nseq, WINDOW, LANES), b3)
    new = pl.BlockSpec((nseq, SAMPLE_PAD, LANES), b3)
    return pl.pallas_call(
        _attn_sample_kernel,
        out_shape=jax.ShapeDtypeStruct(q.shape, BF16),
        grid=(B // nseq,),
        in_specs=[pl.BlockSpec(memory_space=pltpu.SMEM),
                  pl.BlockSpec((nseq, SAMPLE_PAD, D_MODEL), b3), cache, new, cache, new,
                  pl.BlockSpec((SAMPLE_PAD, 2 * WINDOW), lambda b: (0, 0))],
        out_specs=pl.BlockSpec((nseq, SAMPLE_PAD, D_MODEL), b3),
        scratch_shapes=[pltpu.VMEM((nseq, 2 * WINDOW, LANES), F32),
                        pltpu.VMEM((nseq, 2 * WINDOW, LANES), F32)],
        compiler_params=_params(1),
        name="swa_sample",
    )(sinks, q, kcache, knew, vcache, vnew, _window_bias(SAMPLE_PAD, False))


def _rope_tables(pos):
    half = ROT_DIM // 2
    inv = ROPE_THETA ** (-jnp.arange(half, dtype=F32) / half)
    ang = pos.astype(F32)[:, None] * inv[None]
    cos, sin = jnp.cos(ang), jnp.sin(ang)
    d = jnp.arange(LANES) % HEAD_DIM
    idx = d % half
    cos_t = jnp.where(d < ROT_DIM, cos[:, idx], 1.0)
    sa = jnp.where(d < half, -sin[:, idx], 0.0)
    sb = jnp.where((d >= half) & (d < ROT_DIM), sin[:, idx], 0.0)
    return cos_t, sa, sb


def _pad_lanes(a, value=0.0):
    return jnp.pad(a, ((0, 0), (0, LANES - a.shape[1])), constant_values=value)


def _prep_weights(ada_w, ada_b, norm_mix, norm_ffn, a_w_in, a_b_gates, a_head_norm, a_w_out,
                  kv_ada_w, kv_ada_b, kv_norm, w_k, w_v, k_norm, b_w_q, b_q_norm, b_sinks, b_w_o,
                  moe_w_group, moe_b_group, moe_w_expert, moe_b_expert, moe_w_gate_up, moe_w_down):
    w = {}
    g0 = QK_COLS + 2 * V_COLS
    w["w_in"] = a_w_in
    w["w_gates"] = jnp.concatenate([_pad_lanes(a_w_in[0, :, g0:g0 + M_HEADS]),
                                    _pad_lanes(a_w_in[0, :, g0 + M_HEADS:])], 1)
    w["bli"] = _pad_lanes(a_b_gates[0][None, :M_HEADS])
    w["blf"] = _pad_lanes(a_b_gates[0][None, M_HEADS:])
    w["head_norm"] = a_head_norm[0][None]
    w["w_out"] = a_w_out[0].astype(BF16)
    w["norm_mix"] = [norm_mix[l][None] for l in range(2)]
    w["norm_ffn"] = [norm_ffn[l][None] for l in range(2)]
    w["router"] = []
    for l in range(2):
        wr = _pad_lanes(jnp.concatenate([moe_w_group[l], moe_w_expert[l]], 1))
        hi = wr.astype(BF16)
        lo = (wr - hi.astype(F32)).astype(BF16)
        br = _pad_lanes(jnp.concatenate([moe_b_group[l], moe_b_expert[l]])[None])
        w["router"].append((hi, lo, br))
    w["w_gu"] = moe_w_gate_up
    w["w_d"] = moe_w_down
    w["kv_norm"] = kv_norm[None]
    w["w_kv"] = jnp.concatenate([w_k, w_v], 1).astype(BF16)
    w["k_norm"] = jnp.tile(k_norm, N_KV_HEADS)[None]
    w["w_q"] = b_w_q[0].astype(BF16)
    w["q_norm"] = jnp.tile(b_q_norm[0], N_Q_HEADS)[None]
    w["sinks"] = b_sinks[0]
    w["w_o"] = b_w_o[0].astype(BF16)
    lanes = jnp.arange(LANES)
    feat = jnp.arange(D_MODEL)
    w["g64"] = jnp.where((lanes[:, None] // HEAD_DIM) == (lanes[None, :] // HEAD_DIM),
                         1.0 / HEAD_DIM, 0.0).astype(BF16)
    w["gr"] = jnp.where((feat[:, None] // HEAD_DIM) == lanes[None, :], 1.0 / HEAD_DIM, 0.0).astype(BF16)
    w["gb"] = jnp.where(lanes[:, None] == (feat[None, :] // HEAD_DIM), 1.0, 0.0).astype(BF16)
    return w


def _trunk(x2, mods, kvmod, mspec, mspec_moe, rep, w, *, B, T, L, tm, tm_moe, tabs, tabspec, c0, n0, m0,
           cache_k=None, cache_v=None):
    sample = cache_k is not None
    nc = T // L if not sample else 1

    qk, v, o, li, lf = _inproj_call(x2, mods[0], mspec_moe, rep, w["norm_mix"][0], w["w_in"],
                                    w["w_gates"], w["bli"], w["blf"], tm_moe)
    if sample:
        def padtok(a, value=0.0):
            a = a.reshape(B, T, a.shape[-1])
            a = jnp.pad(a, ((0, 0), (0, L - T), (0, 0)), constant_values=value)
            return a.reshape(B * L, a.shape[-1])
        qk, v, li, lf = padtok(qk), padtok(v), padtok(li, M_EMPTY), padtok(lf)
    h, c_new, n_new, m_new = _mlstm_call(qk, v, li, lf, c0, n0, m0, B, nc, L)
    if sample:
        h = h.reshape(B, L, V_COLS)[:, :T].reshape(B * T, V_COLS)
    routed = not sample
    ntiles = (B * T) // tm

    def routed_moe(u2, gate, gid, col, layer):
        q, runs, items = _sort_meta_call(gid.reshape(ntiles, tm), tm)
        su, sg = _scatter_call(runs, q.reshape(ntiles, 1, tm), u2, gate, tm)
        ys = _moe_routed_call(items, su, sg, w["w_gu"], w["w_d"], layer, tm)
        return runs, col, ys

    rh, rl, br = w["router"][0]
    post0 = _post_call(True, routed, h, o, x2, mods[0], mspec, rep, w["norm_ffn"][0], w["head_norm"],
                       w["w_out"], rh, rl, br, tm)
    kvq_args = (mods[0], mods[1], mspec, kvmod, mspec, rep, w["kv_norm"], w["norm_mix"][1],
                w["w_kv"], w["w_q"], w["k_norm"], w["q_norm"], tabs, tabspec,
                w["g64"], w["gr"], w["gb"], tm)

    if routed:
        x1, u2, gate, gid, col = post0
        q, k, vv, xa = _kvq_call(x1, routed_moe(u2, gate, gid, col, 0), *kvq_args)
    else:
        x1, u2, gate = post0
        xa = _moe_call(u2, gate, w["w_gu"], w["w_d"], 0, x1, mods[0], mspec_moe, rep, tm_moe)
        q, k, vv = _kvq_call(xa, None, *kvq_args)
    if not sample:
        att = _attn_prompt_call(w["sinks"], q, k, vv, B, T // WINDOW)
        k_win = k.reshape(B, T, LANES)[:, T - WINDOW:].reshape(B, WINDOW, N_KV_HEADS, HEAD_DIM)
        v_win = vv.reshape(B, T, LANES)[:, T - WINDOW:].reshape(B, WINDOW, N_KV_HEADS, HEAD_DIM)
    else:
        def padseq(a):
            return jnp.pad(a.reshape(B, T, a.shape[-1]), ((0, 0), (0, SAMPLE_PAD - T), (0, 0)))
        kc = cache_k.reshape(B, WINDOW, LANES)
        vc = cache_v.reshape(B, WINDOW, LANES)
        att = _attn_sample_call(w["sinks"], padseq(q), kc, padseq(k), vc, padseq(vv))
        att = att[:, :T].reshape(B * T, D_MODEL)
        k_win = jnp.concatenate([kc[:, T:], k.reshape(B, T, LANES)], 1)
        v_win = jnp.concatenate([vc[:, T:], vv.reshape(B, T, LANES)], 1)
        k_win = k_win.reshape(B, WINDOW, N_KV_HEADS, HEAD_DIM)
        v_win = v_win.reshape(B, WINDOW, N_KV_HEADS, HEAD_DIM)
    rh, rl, br = w["router"][1]
    post1 = _post_call(False, routed, att, None, xa, mods[1], mspec, rep, w["norm_ffn"][1], None,
                       w["w_o"], rh, rl, br, tm)
    if routed:
        x3, u4, gate, gid, col = post1
        runs, col, ys = routed_moe(u4, gate, gid, col, 1)
        y = _resid_gather_call(x3, runs, col, ys, mods[1], mspec, tm)
    else:
        x3, u4, gate = post1
        y = _moe_call(u4, gate, w["w_gu"], w["w_d"], 1, x3, mods[1], mspec_moe, rep, tm_moe)
    c_out = c_new[None]
    n_out = n_new.reshape(1, B, M_HEADS, M_DK)
    m_out = m_new[:, 0, :M_HEADS][None]
    return y, c_out, n_out, m_out, k_win, v_win


def kernel(x_prompt, x_sample, c_prompt, c_sample, state_c, state_n, state_m, cache_k_win, cache_v_win, ada_w, ada_b, norm_mix, norm_ffn, a_w_in, a_b_gates, a_head_norm, a_w_out, kv_ada_w, kv_ada_b, kv_norm, w_k, w_v, k_norm, b_w_q, b_q_norm, b_sinks, b_w_o, moe_w_group, moe_b_group, moe_w_expert, moe_b_expert, moe_w_gate_up, moe_w_down):
    Bp, Tp, D = x_prompt.shape
    Bs, Ts, _ = x_sample.shape
    w = _prep_weights(ada_w, ada_b, norm_mix, norm_ffn, a_w_in, a_b_gates, a_head_norm, a_w_out,
                      kv_ada_w, kv_ada_b, kv_norm, w_k, w_v, k_norm, b_w_q, b_q_norm, b_sinks, b_w_o,
                      moe_w_group, moe_b_group, moe_w_expert, moe_b_expert, moe_w_gate_up, moe_w_down)

    rows = Bp + Bs
    rpad = -rows % 8
    c_all = jnp.concatenate([c_prompt, c_sample, jnp.zeros((rpad, D), F32)], 0)
    mod = _ada_call(c_all, ada_w, ada_b[:, None, :])
    kvm = _ada_call(c_all, kv_ada_w[None], kv_ada_b[None, None, :])

    tm_p = 512
    tiles_per_seq = Tp // tm_p
    mods_p = [mod[l, :Bp][:, None, :] for l in range(2)]
    kvmod_p = kvm[0, :Bp][:, None, :]

    def mspec_p(col):
        return pl.BlockSpec((None, 1, D_MODEL), lambda i, *_: (i // tiles_per_seq, 0, col))

    tm_moe_p = 1024
    moe_tiles_per_seq = Tp // tm_moe_p

    def mspec_moe_p(col):
        return pl.BlockSpec((None, 1, D_MODEL), lambda i, *_: (i // moe_tiles_per_seq, 0, col))

    tabs_p = _rope_tables(jnp.arange(Tp, dtype=jnp.int32))
    tabspec_p = pl.BlockSpec((tm_p, LANES), lambda i, *_: (i % tiles_per_seq, 0))
    npairs = M_HEADS // 2
    c0 = jnp.zeros((Bp, M_HEADS, M_DK, M_DV), F32)
    n0 = jnp.zeros((Bp, npairs, LANES), F32)
    m0 = jnp.pad(jnp.full((Bp, 1, M_HEADS), M_EMPTY, F32), ((0, 0), (0, 0), (0, LANES - M_HEADS)))
    yp, cp, np_, mp, kwp, vwp = _trunk(
        x_prompt.reshape(Bp * Tp, D), mods_p, kvmod_p, mspec_p, mspec_moe_p, None, w,
        B=Bp, T=Tp, L=M_CHUNK, tm=tm_p, tm_moe=tm_moe_p, tabs=tabs_p, tabspec=tabspec_p,
        c0=c0, n0=n0, m0=m0)

    ns = Bs * Ts
    mods_s = [mod[l, Bp:Bp + Bs][None] for l in range(2)]
    kvmod_s = kvm[0, Bp:Bp + Bs][None]
    rep = (jnp.arange(ns)[:, None] // Ts == jnp.arange(Bs)[None, :]).astype(BF16)

    def mspec_s(col):
        return pl.BlockSpec((None, Bs, D_MODEL), lambda i, *_: (0, 0, col))

    tabs_s = _rope_tables(PAST_LEN + jnp.arange(Ts, dtype=jnp.int32))
    tabs_s = tuple(jnp.tile(t, (Bs, 1)) for t in tabs_s)
    m0s = jnp.pad(state_m[0][:, None, :], ((0, 0), (0, 0), (0, LANES - M_HEADS)))
    ys, cs, ns_, ms, kws, vws = _trunk(
        x_sample.reshape(ns, D), mods_s, kvmod_s, mspec_s, mspec_s, rep, w,
        B=Bs, T=Ts, L=SAMPLE_PAD, tm=ns, tm_moe=ns, tabs=tabs_s,
        tabspec=pl.BlockSpec((ns, LANES), lambda i, *_: (0, 0)),
        c0=state_c[0], n0=state_n[0].reshape(Bs, npairs, LANES), m0=m0s,
        cache_k=cache_k_win, cache_v=cache_v_win)

    return (yp.reshape(Bp, Tp, D), ys.reshape(Bs, Ts, D), cp, np_, mp, kwp, vwp,
            cs, ns_, ms, kws, vws)
```

```python
import functools

import jax
import jax.numpy as jnp
from jax import lax
from jax.experimental import pallas as pl
from jax.experimental.pallas import tpu as pltpu

F32 = jnp.float32
BF16 = jnp.bfloat16

D_MODEL = 1024
PAST_LEN = 8192
M_HEADS = 8
M_DK = 64
M_DV = 128
M_CHUNK = 128
GATE_SOFTCAP = 15.0
M_EMPTY = -1e30
WINDOW = 128
HEAD_DIM = 64
N_Q_HEADS = 16
N_KV_HEADS = 2
ROPE_THETA = 500000.0
ROT_DIM = 16
N_GROUPS = 4
EXPERTS_PER_GROUP = 4
N_EXPERTS = 16
D_EXPERT = 256
RMS_EPS = 1e-6

LANES = 128
QK_COLS = 2 * M_HEADS * M_DK
V_COLS = M_HEADS * M_DV
SAMPLE_PAD = 16
TOKEN_TILE = 512
BIG_TOKEN_TILE = 1024
VMEM_LIMIT = 52 * 1024 * 1024

NT_DIMS = (((1,), (1,)), ((), ()))


def _params(n_axes):
    return pltpu.CompilerParams(dimension_semantics=("arbitrary",) * n_axes,
                                vmem_limit_bytes=VMEM_LIMIT)


def _dot(a, b):
    return jnp.dot(a, b, preferred_element_type=F32)


def _dot_nt(a, b):
    return lax.dot_general(a, b, NT_DIMS, preferred_element_type=F32)


def _split2(x):
    hi = x.astype(BF16)
    lo = (x - hi.astype(F32)).astype(BF16)
    return hi, lo


def _split3(x):
    hi = x.astype(BF16)
    r = x - hi.astype(F32)
    mid = r.astype(BF16)
    lo = (r - mid.astype(F32)).astype(BF16)
    return hi, mid, lo


def _unit_rms(x):
    return x * lax.rsqrt(jnp.mean(x * x, axis=-1, keepdims=True) + RMS_EPS)


def _ada_kernel(c_ref, w_ref, b_ref, o_ref):
    c = c_ref[...]
    cs = (c * jax.nn.sigmoid(c)).astype(BF16)
    o_ref[...] = _dot(cs, w_ref[...].astype(BF16)) + b_ref[...]


def _ada_call(c, w, b):
    g, d, n = w.shape
    r = c.shape[0]
    tn = 1024
    return pl.pallas_call(
        _ada_kernel,
        out_shape=jax.ShapeDtypeStruct((g, r, n), F32),
        grid=(g, n // tn),
        in_specs=[pl.BlockSpec((r, d), lambda i, j: (0, 0)),
                  pl.BlockSpec((None, d, tn), lambda i, j: (i, 0, j)),
                  pl.BlockSpec((None, 1, tn), lambda i, j: (i, 0, j))],
        out_specs=pl.BlockSpec((None, r, tn), lambda i, j: (i, 0, j)),
        compiler_params=_params(2),
        name="ada_mod",
    )(c, w, b)


def _mod_rows(ref, rep_ref):
    if rep_ref is None:
        return ref[...]
    hi, mid, lo = _split3(ref[...])
    rep = rep_ref[...]
    return _dot(rep, hi) + _dot(rep, mid) + _dot(rep, lo)


def _inproj_kernel(expand, x_ref, sh_ref, sc_ref, nw_ref, w_ref, wg_ref, bli_ref, blf_ref, *rest):
    rep_ref = rest[0] if expand else None
    qk_ref, v_ref, o_ref, li_ref, lf_ref, wb_ref = rest[1 if expand else 0:]

    @pl.when(pl.program_id(0) == 0)
    def _():
        wb_ref[...] = w_ref[...].astype(BF16)

    u = _unit_rms(x_ref[...]) * (nw_ref[...] * (1.0 + _mod_rows(sc_ref, rep_ref))) \
        + _mod_rows(sh_ref, rep_ref)
    ub = u.astype(BF16)
    half = QK_COLS // 2
    q = _dot(ub, wb_ref[:, 0:half]) * (M_DK ** -0.5)
    qk_ref[:, 0:half] = q.astype(BF16)
    qk_ref[:, half:QK_COLS] = _dot(ub, wb_ref[:, half:QK_COLS]).astype(BF16)
    v_ref[...] = _dot(ub, wb_ref[:, QK_COLS:QK_COLS + V_COLS]).astype(BF16)
    o_ref[...] = jax.nn.sigmoid(_dot(ub, wb_ref[:, QK_COLS + V_COLS:QK_COLS + 2 * V_COLS])).astype(BF16)
    lane = lax.broadcasted_iota(jnp.int32, (1, LANES), 1)
    live = lane < M_HEADS
    wg = wg_ref[...].astype(BF16)
    gi = _dot(ub, wg[:, 0:LANES]) + bli_ref[...]
    gf = _dot(ub, wg[:, LANES:2 * LANES]) + blf_ref[...]
    li = GATE_SOFTCAP * jnp.tanh(gi / GATE_SOFTCAP)
    fpre = GATE_SOFTCAP * jnp.tanh(gf / GATE_SOFTCAP)
    lf = jnp.minimum(fpre, 0.0) - jnp.log1p(jnp.exp(-jnp.abs(fpre)))
    li_ref[...] = jnp.where(live, li, 0.0)
    lf_ref[...] = jnp.where(live, lf, 0.0)


def _inproj_call(x, mod, mspec, rep, nw, w_in, wg, bli, blf, tm):
    n = x.shape[0]
    row = lambda i: (i, 0)
    const = lambda i: (0, 0)
    main_cols = QK_COLS + 2 * V_COLS
    expand = rep is not None
    return pl.pallas_call(
        functools.partial(_inproj_kernel, expand),
        out_shape=(jax.ShapeDtypeStruct((n, QK_COLS), BF16),
                   jax.ShapeDtypeStruct((n, V_COLS), BF16),
                   jax.ShapeDtypeStruct((n, V_COLS), BF16),
                   jax.ShapeDtypeStruct((n, LANES), F32),
                   jax.ShapeDtypeStruct((n, LANES), F32)),
        grid=(n // tm,),
        in_specs=[pl.BlockSpec((tm, D_MODEL), row), mspec(0), mspec(1),
                  pl.BlockSpec((1, D_MODEL), const),
                  pl.BlockSpec((None, D_MODEL, main_cols), lambda i: (0, 0, 0),
                               pipeline_mode=pl.Buffered(1)),
                  pl.BlockSpec((D_MODEL, 2 * LANES), const),
                  pl.BlockSpec((1, LANES), const), pl.BlockSpec((1, LANES), const)]
        + ([pl.BlockSpec(rep.shape, const)] if expand else []),
        out_specs=(pl.BlockSpec((tm, QK_COLS), row), pl.BlockSpec((tm, V_COLS), row),
                   pl.BlockSpec((tm, V_COLS), row), pl.BlockSpec((tm, LANES), row),
                   pl.BlockSpec((tm, LANES), row)),
        scratch_shapes=[pltpu.VMEM((D_MODEL, main_cols), BF16)],
        compiler_params=_params(1),
        name="mlstm_inproj",
    )(x, mod, mod, nw, w_in, wg, bli, blf, *([rep] if expand else []))


def _mlstm_kernel(L, nseq, scale_q, *refs):
    c = pl.program_id(1)
    seqs = [[r.at[b] for r in refs] for b in range(nseq)]

    @pl.when(c == 0)
    def _():
        for s in seqs:
            _mlstm_load_state(*s[4:7], *s[11:14])

    for s in seqs:
        _mlstm_seq_step(L, scale_q, *s[0:4], s[7], *s[11:14])

    @pl.when(c == pl.num_programs(1) - 1)
    def _():
        for s in seqs:
            _mlstm_store_state(*s[8:14])


def _mlstm_load_state(c0_ref, n0_ref, m0_ref, c2_s, n_s, m_s):
    m_s[...] = m0_ref[...]
    n_s[...] = n0_ref[...]
    z = jnp.zeros((M_DK, M_DV), F32)
    for j in range(M_HEADS // 2):
        c2_s[j] = jnp.concatenate([jnp.concatenate([c0_ref[2 * j], z], 1),
                                   jnp.concatenate([z, c0_ref[2 * j + 1]], 1)], 0)


def _mlstm_seq_step(L, scale_q, qk_ref, v_ref, li_ref, lf_ref, h_ref, c2_s, n_s, m_s):
    npairs = M_HEADS // 2
    hd = M_DK

    lane = lax.broadcasted_iota(jnp.int32, (1, LANES), 1)
    lo128 = lane < hd
    lane256 = lax.broadcasted_iota(jnp.int32, (1, 2 * LANES), 1)
    lo256 = lane256 < LANES
    row128 = lax.broadcasted_iota(jnp.int32, (LANES, 1), 0)
    top = row128 < hd
    blockdiag = (top & lo256) | (jnp.logical_not(top) & jnp.logical_not(lo256))

    LI = li_ref[...]
    LF = lf_ref[...]
    rowL = lax.broadcasted_iota(jnp.int32, (L, LANES), 0)

    def prefix(x, op, ident):
        d = 1
        while d < L:
            shifted = pltpu.roll(x, d, axis=0)
            x = op(x, jnp.where(rowL >= d, shifted, ident))
            d *= 2
        return x

    Bc = prefix(LF, jnp.add, 0.0)
    Cm = LI - Bc
    mprev = m_s[...]
    Gc = jnp.maximum(mprev, prefix(Cm, jnp.maximum, -jnp.inf))
    A = jnp.exp(mprev - Gc)
    bL = Bc[L - 1:L, :]
    DL = bL + Cm
    mnew = jnp.maximum(bL + mprev, jnp.max(DL, axis=0, keepdims=True))
    ast = jnp.exp(bL + mprev - mnew)
    WST = jnp.exp(DL - mnew)

    def pad_rows(x, rows):
        if x.shape[0] == rows:
            return x
        return jnp.concatenate([x, jnp.zeros((rows - x.shape[0], x.shape[1]), x.dtype)], 0)

    kw_lanes = max(L, hd)
    assert kw_lanes in (hd, LANES) and L <= LANES
    wide = kw_lanes == LANES
    lane_s = lax.broadcasted_iota(jnp.int32, (1, 2 * kw_lanes), 1)
    row_s = lax.broadcasted_iota(jnp.int32, (2 * kw_lanes, 1), 0)
    cm_pad = pad_rows(Cm, kw_lanes)
    if wide:
        XT = cm_pad.T
    else:
        XT = jnp.concatenate([cm_pad, cm_pad], 0).T
    causal = (lane_s & (kw_lanes - 1)) <= lax.broadcasted_iota(jnp.int32, (L, 2 * kw_lanes), 0)
    top_s = row_s < kw_lanes
    J = jnp.where((top_s & lo256) | (jnp.logical_not(top_s) & jnp.logical_not(lo256)),
                  1.0, 0.0).astype(BF16)

    EXPM = jnp.exp(-(Bc + Gc))

    def bc(X, h):
        return jnp.broadcast_to(X[:, h:h + 1], X.shape)

    for j in range(npairs):
        he, ho = 2 * j, 2 * j + 1
        p128 = slice(LANES * j, LANES * (j + 1))
        p256 = slice(2 * LANES * j, 2 * LANES * (j + 1))

        def pair128(X):
            return jnp.where(lo128, bc(X, he), bc(X, ho))

        def pair256(X):
            return jnp.concatenate([bc(X, he), bc(X, ho)], 1)

        q128 = qk_ref[:, LANES * j:LANES * (j + 1)]
        k128 = qk_ref[:, QK_COLS // 2 + LANES * j:QK_COLS // 2 + LANES * (j + 1)]
        v256 = v_ref[:, 2 * LANES * j:2 * LANES * (j + 1)]
        zk = jnp.zeros_like(k128)
        zv = jnp.zeros_like(v256)
        K2t = jnp.concatenate([pad_rows(jnp.where(lo128, k128, zk), kw_lanes),
                               pad_rows(jnp.where(lo128, zk, k128), kw_lanes)], 0)
        V2 = jnp.concatenate([pad_rows(jnp.where(lo256, v256, zv), kw_lanes),
                              pad_rows(jnp.where(lo256, zv, v256), kw_lanes)], 0)
        S = _dot_nt(q128, K2t)
        if wide:
            crow = jnp.concatenate([XT[he:he + 1, :], XT[ho:ho + 1, :]], 1)
            arg = jnp.where(causal, crow - pair256(Gc), -jnp.inf)
        else:
            crow = jnp.where(lo128, XT[he:he + 1, :], XT[ho:ho + 1, :])
            arg = jnp.where(causal, crow - pair128(Gc), -jnp.inf)
        Sw = (S * jnp.exp(arg)).astype(BF16)
        num_intra = _dot(Sw, V2)
        rowsum = _dot(Sw, J)
        C2 = c2_s[j]
        npair = n_s[j:j + 1, :]
        Nrow2 = jnp.concatenate(
            [jnp.broadcast_to(jnp.where(lo128, npair, 0.0), (LANES, LANES)),
             jnp.broadcast_to(jnp.where(lo128, 0.0, npair), (LANES, LANES))], 0).astype(BF16)
        if scale_q:
            qa = (q128.astype(F32) * pair128(A)).astype(BF16)
            num = _dot(qa, C2.astype(BF16)) + num_intra
            den = _dot_nt(qa, Nrow2) + rowsum
        else:
            a256 = pair256(A)
            num = a256 * _dot(q128, C2.astype(BF16)) + num_intra
            den = a256 * _dot_nt(q128, Nrow2) + rowsum
        h = num / jnp.maximum(jnp.abs(den), pair256(EXPM))
        h_ref[:, p256] = h.astype(h_ref.dtype)

        kw = k128.astype(F32) * pair128(WST)
        n_s[j:j + 1, :] = pair128(ast) * npair + jnp.sum(kw, axis=0, keepdims=True)
        kwT = pad_rows(kw, LANES).T.astype(BF16)
        dC = _dot(kwT, pad_rows(v256, LANES))
        c2_s[j] = pair256(ast) * C2 + jnp.where(blockdiag, dC, 0.0)

    m_s[...] = mnew


def _mlstm_store_state(cout_ref, nout_ref, mout_ref, c2_s, n_s, m_s):
    for j in range(M_HEADS // 2):
        C2 = c2_s[j]
        cout_ref[2 * j] = C2[0:M_DK, 0:M_DV]
        cout_ref[2 * j + 1] = C2[M_DK:2 * M_DK, M_DV:2 * M_DV]
    nout_ref[...] = n_s[...]
    mout_ref[...] = m_s[...]


MLSTM_SEQS_PER_STEP = 4


def _mlstm_call(qk, v, li, lf, c0, n0, m0, B, nc, L):
    npairs = M_HEADS // 2
    nseq = min(B, MLSTM_SEQS_PER_STEP)
    T = nc * L
    tok3 = lambda a: a.reshape(B, T, a.shape[-1])
    tok = lambda b, c: (b, c, 0)
    st4 = lambda b, c: (b, 0, 0, 0)
    st3 = lambda b, c: (b, 0, 0)
    h, c_new, n_new, m_new = pl.pallas_call(
        functools.partial(_mlstm_kernel, L, nseq, nc == 1),
        out_shape=(jax.ShapeDtypeStruct((B, T, V_COLS), BF16),
                   jax.ShapeDtypeStruct((B, M_HEADS, M_DK, M_DV), F32),
                   jax.ShapeDtypeStruct((B, npairs, LANES), F32),
                   jax.ShapeDtypeStruct((B, 1, LANES), F32)),
        grid=(B // nseq, nc),
        in_specs=[pl.BlockSpec((nseq, L, QK_COLS), tok), pl.BlockSpec((nseq, L, V_COLS), tok),
                  pl.BlockSpec((nseq, L, LANES), tok), pl.BlockSpec((nseq, L, LANES), tok),
                  pl.BlockSpec((nseq, M_HEADS, M_DK, M_DV), st4),
                  pl.BlockSpec((nseq, npairs, LANES), st3),
                  pl.BlockSpec((nseq, 1, LANES), st3)],
        out_specs=(pl.BlockSpec((nseq, L, V_COLS), tok),
                   pl.BlockSpec((nseq, M_HEADS, M_DK, M_DV), st4),
                   pl.BlockSpec((nseq, npairs, LANES), st3),
                   pl.BlockSpec((nseq, 1, LANES), st3)),
        scratch_shapes=[pltpu.VMEM((nseq, npairs, 2 * M_DK, 2 * M_DV), F32),
                        pltpu.VMEM((nseq, npairs, LANES), F32),
                        pltpu.VMEM((nseq, 1, LANES), F32)],
        compiler_params=_params(2),
        name="mlstm_chunks",
    )(tok3(qk), tok3(v), tok3(li), tok3(lf), c0, n0, m0)
    return h.reshape(B * T, V_COLS), c_new, n_new, m_new


def _route(lg):
    lane = lax.broadcasted_iota(jnp.int32, lg.shape, 1)
    lanef = lane.astype(F32)
    neg = -jnp.inf
    far = float(LANES)
    gm = lane < N_GROUPS
    lgm = jnp.where(gm, lg, neg)
    gmax = jnp.max(lgm, axis=-1, keepdims=True)
    gsum = jnp.sum(jnp.exp(lgm - gmax), axis=-1, keepdims=True)
    g_w = 1.0 / gsum
    gidx = jnp.min(jnp.where(gm & (lg == gmax), lanef, far), axis=-1, keepdims=True)
    first = N_GROUPS + EXPERTS_PER_GROUP * gidx
    sel = (lanef >= first) & (lanef < first + EXPERTS_PER_GROUP)
    l1 = jnp.max(jnp.where(sel, lg, neg), axis=-1, keepdims=True)
    i1 = jnp.min(jnp.where(sel & (lg == l1), lanef, far), axis=-1, keepdims=True)
    sel2 = sel & (lanef != i1)
    l2 = jnp.max(jnp.where(sel2, lg, neg), axis=-1, keepdims=True)
    i2 = jnp.min(jnp.where(sel2 & (lg == l2), lanef, far), axis=-1, keepdims=True)
    r = jnp.exp(l2 - l1)
    w1 = g_w / (1.0 + r)
    w2 = w1 * r
    return jnp.where(lanef == i1, w1, jnp.where(lanef == i2, w2, 0.0)), gidx


def _post_kernel(mlstm, routed, expand, *refs):
    refs = list(refs)
    h_ref = refs.pop(0)
    o_ref = refs.pop(0) if mlstm else None
    x_ref, g1_ref, sh2_ref, sc2_ref, nf_ref = refs[:5]
    refs = refs[5:]
    hn_ref = refs.pop(0) if mlstm else None
    wout_ref, wrh_ref, wrl_ref, br_ref = refs[:4]
    refs = refs[4:]
    tri_ref = refs.pop(0) if routed else None
    rep_ref = refs.pop(0) if expand else None
    x1_ref, u2_ref, gate_ref = refs[:3]
    outs = refs[3:]
    if mlstm:
        hf = h_ref[...].astype(F32)
        parts = [_unit_rms(hf[:, M_DV * i:M_DV * (i + 1)]) for i in range(M_HEADS)]
        hn = jnp.concatenate(parts, 1) * hn_ref[...]
        hg = (hn * o_ref[...].astype(F32)).astype(BF16)
    else:
        hg = h_ref[...]
    x1 = x_ref[...] + _mod_rows(g1_ref, rep_ref) * _dot(hg, wout_ref[...])
    x1_ref[...] = x1
    u2 = _unit_rms(x1) * (nf_ref[...] * (1.0 + _mod_rows(sc2_ref, rep_ref))) + _mod_rows(sh2_ref, rep_ref)
    uh, ul = _split2(u2)
    lg = _dot(uh, wrh_ref[...]) + _dot(ul, wrh_ref[...]) + _dot(uh, wrl_ref[...]) + br_ref[...]
    gate, gidx = _route(lg)
    u2_ref[...] = uh
    gate_ref[...] = gate
    if not routed:
        return
    gid_ref, col_ref = outs
    rows = []
    for blk in range(x1.shape[0] // LANES):
        col = jnp.broadcast_to(gidx[LANES * blk:LANES * (blk + 1), :], (LANES, LANES))
        rows.append(col.T[0:1, :])
    gid_ref[...] = jnp.concatenate(rows, 1)
    lanef = lax.broadcasted_iota(jnp.int32, gate.shape, 1).astype(F32)
    onehot = jnp.where(lanef == gidx, 1.0, 0.0)
    earlier = _dot(tri_ref[...], onehot.astype(BF16))
    rank = jnp.sum(onehot * earlier, axis=-1, keepdims=True)
    col_ref[...] = jnp.where(lanef == 0.0, gidx, jnp.where(lanef == 1.0, rank, 0.0))


def _post_call(mlstm, routed, h, o, x, mod, mspec, rep, nf, hn, wout, wrh, wrl, br, tm):
    n = x.shape[0]
    expand = rep is not None
    row = lambda i: (i, 0)
    const = lambda i: (0, 0)
    tok = pl.BlockSpec((tm, D_MODEL), row)
    small = pl.BlockSpec((tm, LANES), row)
    vec = pl.BlockSpec((1, D_MODEL), const)
    ins = [h] + ([o] if mlstm else []) + [x, mod, mod, mod, nf] + ([hn] if mlstm else []) \
        + [wout, wrh, wrl, br]
    specs = [tok] + ([tok] if mlstm else []) + [tok, mspec(2), mspec(3), mspec(4), vec] \
        + ([vec] if mlstm else []) \
        + [pl.BlockSpec((D_MODEL, D_MODEL), const), pl.BlockSpec((D_MODEL, LANES), const),
           pl.BlockSpec((D_MODEL, LANES), const), pl.BlockSpec((1, LANES), const)]
    out_shape = [jax.ShapeDtypeStruct((n, D_MODEL), F32), jax.ShapeDtypeStruct((n, D_MODEL), BF16),
                 jax.ShapeDtypeStruct((n, LANES), F32)]
    out_specs = [tok, tok, small]
    if routed:
        t = jnp.arange(tm)
        ins.append((t[None, :] < t[:, None]).astype(BF16))
        specs.append(pl.BlockSpec((tm, tm), const))
        out_shape += [jax.ShapeDtypeStruct((n // tm, 1, tm), F32), jax.ShapeDtypeStruct((n, LANES), F32)]
        out_specs += [pl.BlockSpec((None, 1, tm), lambda i: (i, 0, 0)), small]
    if expand:
        ins.append(rep)
        specs.append(pl.BlockSpec(rep.shape, const))
    out_shape, out_specs = tuple(out_shape), tuple(out_specs)
    return pl.pallas_call(
        functools.partial(_post_kernel, mlstm, routed, expand),
        out_shape=out_shape,
        grid=(n // tm,),
        in_specs=specs,
        out_specs=out_specs,
        compiler_params=_params(1),
        name="post_mlstm" if mlstm else "post_attn",
    )(*ins)


ITEM_ROWS = 8
RUN_ALIGN = 16
RUN_SIZES = (512, 256, 128, 64, 32, 16)
RUN_SRC, RUN_DST, RUN_LEN, RUN_TOTAL = 0, N_GROUPS, 2 * N_GROUPS, 3 * N_GROUPS


def _stage_rows(tm):
    assert N_GROUPS * (RUN_ALIGN - 1) <= LANES
    return tm + LANES


def _sorted_rows(n, tm):
    rows = n + N_GROUPS * (n // tm) * RUN_ALIGN + tm
    return -(-rows // tm) * tm


def _sort_meta_kernel(tm, nts, gid_ref, q_ref, runs_ref, items_ref):
    ntiles = gid_ref.shape[0]
    gid = gid_ref[...]
    r = lax.broadcasted_iota(jnp.int32, (tm, tm), 0)
    c = lax.broadcasted_iota(jnp.int32, (tm, tm), 1)
    before = jnp.where(r < c, 1.0, 0.0).astype(BF16)
    trow = lax.broadcasted_iota(jnp.int32, (ntiles, LANES), 0)
    lane = lax.broadcasted_iota(jnp.int32, (ntiles, LANES), 1)
    k = lax.broadcasted_iota(jnp.int32, (1, LANES), 1).astype(F32)
    zero11 = jnp.zeros((1, 1), F32)
    q = jnp.zeros((ntiles, tm), F32)
    runs = jnp.zeros((ntiles, LANES), F32)
    src = jnp.zeros((ntiles, 1), F32)
    start, nitems = zero11, zero11
    grp = jnp.zeros((1, LANES), F32)
    tile = jnp.zeros((1, LANES), F32)
    valid = jnp.zeros((1, LANES), F32)
    for g in range(N_GROUPS):
        mask = jnp.where(gid == float(g), 1.0, 0.0)
        cnt = jnp.sum(mask, axis=1, keepdims=True)
        padded = jnp.floor((cnt + (RUN_ALIGN - 1.0)) * (1.0 / RUN_ALIGN)) * RUN_ALIGN
        incl = jnp.broadcast_to(padded, (ntiles, LANES))
        d = 1
        while d < ntiles:
            incl = incl + jnp.where(trow >= d, pltpu.roll(incl, d, axis=0), 0.0)
            d *= 2
        total = incl[ntiles - 1:ntiles, 0:1]
        dst = start + incl[:, 0:1] - padded
        q = q + mask * (src + _dot(mask.astype(BF16), before))
        runs = runs + jnp.where(lane == RUN_SRC + g, src, 0.0) + jnp.where(lane == RUN_DST + g, dst, 0.0) \
            + jnp.where(lane == RUN_LEN + g, padded, 0.0)
        src = src + padded
        end = start + total
        ft = jnp.floor(start * (1.0 / tm))
        lt = jnp.floor((end - 1.0) * (1.0 / tm))
        ni = jnp.where(total > 0.0, lt - ft + 1.0, 0.0)
        inside = (k >= nitems) & (k < nitems + ni)
        grp = grp + jnp.where(inside, float(g), 0.0)
        tile = tile + jnp.where(inside, ft + (k - nitems), 0.0)
        valid = valid + jnp.where(inside, 1.0, 0.0)
        nitems = nitems + ni
        start = end
    runs = runs + jnp.where(lane == RUN_TOTAL, start, 0.0)
    live = valid > 0.0
    prev = pltpu.roll(tile, 1, axis=1)
    nxt = pltpu.roll(tile, LANES - 1, axis=1)
    first = live & ((k == 0.0) | (tile != prev))
    last = live & ((k == nitems - 1.0) | (tile != nxt))
    spare = jnp.floor((start - 1.0) * (1.0 / tm)) + 1.0 + (k - nitems)
    fill = jnp.logical_not(live) & (spare <= nts - 1.0)
    tile = jnp.where(live, tile, jnp.minimum(spare, nts - 1.0))
    grp = jnp.where(live, grp, jnp.max(grp, axis=1, keepdims=True))
    flag = lambda m: jnp.where(m, 1.0, 0.0)
    q_ref[...] = q
    runs_ref[...] = runs.astype(jnp.int32)
    table = jnp.concatenate([tile, grp, valid, flag(first | fill), flag(last | fill),
                             jnp.zeros((ITEM_ROWS - 5, LANES), F32)], 0)
    items_ref[...] = table.astype(jnp.int32)


def _max_items(nts):
    return nts + N_GROUPS - 1


def _sort_meta_call(gid, tm):
    ntiles = gid.shape[0]
    nts = _sorted_rows(ntiles * tm, tm) // tm
    assert _max_items(nts) <= LANES and tm <= RUN_SIZES[0]
    return pl.pallas_call(
        functools.partial(_sort_meta_kernel, tm, nts),
        out_shape=(jax.ShapeDtypeStruct((ntiles, tm), F32),
                   jax.ShapeDtypeStruct((ntiles, LANES), jnp.int32),
                   jax.ShapeDtypeStruct((ITEM_ROWS, LANES), jnp.int32)),
        compiler_params=_params(0),
        name="moe_sort_meta",
    )(gid)


def _run_copies(runs_ref, i, tile_ref, sorted_hbm, sem, to_sorted):
    pieces = []
    for g in range(N_GROUPS):
        src = runs_ref[i, RUN_SRC + g]
        dst = runs_ref[i, RUN_DST + g]
        length = runs_ref[i, RUN_LEN + g]
        for s in RUN_SIZES:
            def build(src=src, dst=dst, length=length, s=s):
                off = length & (-2 * s)
                a = tile_ref.at[pl.ds(pl.multiple_of(src + off, RUN_ALIGN), s), :]
                b = sorted_hbm.at[pl.ds(pl.multiple_of(dst + off, RUN_ALIGN), s), :]
                return pltpu.make_async_copy(a, b, sem) if to_sorted else pltpu.make_async_copy(b, a, sem)
            pieces.append(((length & s) != 0, build))
    return pieces


def _start(pieces):
    for pred, build in pieces:
        pl.when(pred)(lambda build=build: build().start())


def _wait(pieces):
    for pred, build in pieces:
        pl.when(pred)(lambda build=build: build().wait())


def _start_then_wait(pieces):
    _start(pieces)
    _wait(pieces)


def _scatter_kernel(tm, nts, ntiles, runs_ref, q_ref, u_ref, g_ref, su_hbm, sg_hbm,
                    stu, stg, sem_u, sem_g):
    i = pl.program_id(0)
    slot = i % 2

    def copies(tile, slot):
        return (_run_copies(runs_ref, tile, stu.at[slot], su_hbm, sem_u.at[slot], True)
                + _run_copies(runs_ref, tile, stg.at[slot], sg_hbm, sem_g.at[slot], True))

    r = lax.broadcasted_iota(jnp.int32, (_stage_rows(tm), tm), 0).astype(F32)
    perm = jnp.where(q_ref[...] == r, 1.0, 0.0).astype(BF16)
    stu[slot] = _dot(perm, u_ref[...]).astype(BF16)
    g3 = _split3(g_ref[...])
    stg[slot] = _dot(perm, g3[0]) + _dot(perm, g3[1]) + _dot(perm, g3[2])
    _start(copies(i, slot))

    @pl.when(i > 0)
    def _():
        _wait(copies(i - 1, 1 - slot))

    @pl.when(i == ntiles - 1)
    def _():
        _wait(copies(i, slot))
        stu[slot] = jnp.zeros(stu.shape[1:], stu.dtype)
        stg[slot] = jnp.zeros(stg.shape[1:], stg.dtype)
        total = runs_ref[i, RUN_TOTAL]
        tail = nts * tm - total
        nfull = tail // tm
        pieces = []
        for tile_ref, hbm, sem in ((stu.at[slot], su_hbm, sem_u.at[slot]),
                                   (stg.at[slot], sg_hbm, sem_g.at[slot])):
            for j in range(nts - ntiles):
                def full(j=j, tile_ref=tile_ref, hbm=hbm, sem=sem):
                    dst = pl.multiple_of(total + j * tm, RUN_ALIGN)
                    return pltpu.make_async_copy(tile_ref.at[pl.ds(0, tm), :], hbm.at[pl.ds(dst, tm), :], sem)
                pieces.append((j < nfull, full))
            rem = tail - nfull * tm
            for s in RUN_SIZES:
                if s >= tm:
                    continue
                def part(s=s, tile_ref=tile_ref, hbm=hbm, sem=sem):
                    dst = pl.multiple_of(total + nfull * tm + (rem & (-2 * s)), RUN_ALIGN)
                    return pltpu.make_async_copy(tile_ref.at[pl.ds(0, s), :], hbm.at[pl.ds(dst, s), :], sem)
                pieces.append(((rem & s) != 0, part))
        _start_then_wait(pieces)


def _scatter_call(runs, q3, u2, gate, tm):
    n = u2.shape[0]
    ns = _sorted_rows(n, tm)
    grid_spec = pltpu.PrefetchScalarGridSpec(
        num_scalar_prefetch=1,
        grid=(n // tm,),
        in_specs=[pl.BlockSpec((None, 1, tm), lambda i, *_: (i, 0, 0)),
                  pl.BlockSpec((tm, D_MODEL), lambda i, *_: (i, 0)),
                  pl.BlockSpec((tm, LANES), lambda i, *_: (i, 0))],
        out_specs=(pl.BlockSpec(memory_space=pl.ANY), pl.BlockSpec(memory_space=pl.ANY)),
        scratch_shapes=[pltpu.VMEM((2, _stage_rows(tm), D_MODEL), BF16),
                        pltpu.VMEM((2, _stage_rows(tm), LANES), F32),
                        pltpu.SemaphoreType.DMA((2,)), pltpu.SemaphoreType.DMA((2,))])
    return pl.pallas_call(
        functools.partial(_scatter_kernel, tm, ns // tm, n // tm),
        out_shape=(jax.ShapeDtypeStruct((ns, D_MODEL), BF16), jax.ShapeDtypeStruct((ns, LANES), F32)),
        grid_spec=grid_spec,
        compiler_params=_params(1),
        name="moe_scatter",
    )(runs, q3, u2, gate)


def _ungroup(tm, runs_ref, col_ref, ys_hbm, stage, sem):
    i = pl.program_id(0)
    slot = i % 2

    def copies(tile, slot):
        return _run_copies(runs_ref, tile, stage.at[slot], ys_hbm, sem.at[slot], False)

    @pl.when(i == 0)
    def _():
        stage[...] = jnp.zeros_like(stage)
        _start(copies(i, slot))

    @pl.when(i + 1 < pl.num_programs(0))
    def _():
        _start(copies(i + 1, 1 - slot))

    _wait(copies(i, slot))
    col = col_ref[...]
    gidx = col[:, 0:1]
    src = [runs_ref[i, RUN_SRC + g].astype(F32) for g in range(N_GROUPS)]
    first = src[N_GROUPS - 1]
    for g in range(N_GROUPS - 2, -1, -1):
        first = jnp.where(gidx == float(g), src[g], first)
    q = first + col[:, 1:2]
    lanes = lax.broadcasted_iota(jnp.int32, (tm, _stage_rows(tm)), 1).astype(F32)
    perm_t = jnp.where(q == lanes, 1.0, 0.0).astype(BF16)
    return _dot(perm_t, stage[slot])


def _moe_routed_kernel(items_ref, su_ref, sg_ref, wgu_ref, wd_ref, out_ref, acc_ref, wgu_b, wd_b):
    kk = pl.program_id(0)
    grp = items_ref[1, kk]

    @pl.when((kk == 0) | (grp != items_ref[1, jnp.maximum(kk - 1, 0)]))
    def _():
        for e in range(EXPERTS_PER_GROUP):
            wgu_b[e] = wgu_ref[e].astype(BF16)
            wd_b[e] = wd_ref[e].astype(BF16)

    @pl.when(items_ref[3, kk] == 1)
    def _():
        acc_ref[...] = jnp.zeros_like(acc_ref)

    @pl.when(items_ref[2, kk] == 1)
    def _():
        u = su_ref[...]
        gate = sg_ref[...]
        lane = lax.broadcasted_iota(jnp.int32, gate.shape, 1)
        first_lane = N_GROUPS + EXPERTS_PER_GROUP * grp
        acts = []
        for e in range(EXPERTS_PER_GROUP):
            gcol = jnp.sum(jnp.where(lane == first_lane + e, gate, 0.0), axis=-1, keepdims=True)
            hu = _dot(u, wgu_b[e])
            hg = hu[:, 0:D_EXPERT]
            acts.append((hg * jax.nn.sigmoid(hg) * hu[:, D_EXPERT:2 * D_EXPERT] * gcol).astype(BF16))
        wd_all = wd_b[...].reshape(EXPERTS_PER_GROUP * D_EXPERT, D_MODEL)
        acc_ref[...] += _dot(jnp.concatenate(acts, 1), wd_all)

    @pl.when(items_ref[4, kk] == 1)
    def _():
        out_ref[...] = acc_ref[...].astype(out_ref.dtype)


def _moe_routed_call(items, su, sg, wgu, wd, layer, tm):
    ns = su.shape[0]
    n_items = _max_items(ns // tm)
    depth = wgu.shape[0]
    wgu5 = wgu.reshape(depth, N_GROUPS, EXPERTS_PER_GROUP, D_MODEL, 2 * D_EXPERT)
    wd5 = wd.reshape(depth, N_GROUPS, EXPERTS_PER_GROUP, D_EXPERT, D_MODEL)
    grid_spec = pltpu.PrefetchScalarGridSpec(
        num_scalar_prefetch=1,
        grid=(n_items,),
        in_specs=[pl.BlockSpec((tm, D_MODEL), lambda k, it: (it[0, k], 0)),
                  pl.BlockSpec((tm, LANES), lambda k, it: (it[0, k], 0)),
                  pl.BlockSpec((None, None, EXPERTS_PER_GROUP, D_MODEL, 2 * D_EXPERT),
                               lambda k, it: (layer, it[1, k], 0, 0, 0)),
                  pl.BlockSpec((None, None, EXPERTS_PER_GROUP, D_EXPERT, D_MODEL),
                               lambda k, it: (layer, it[1, k], 0, 0, 0))],
        out_specs=pl.BlockSpec((tm, D_MODEL), lambda k, it: (it[0, k], 0)),
        scratch_shapes=[pltpu.VMEM((tm, D_MODEL), F32),
                        pltpu.VMEM((EXPERTS_PER_GROUP, D_MODEL, 2 * D_EXPERT), BF16),
                        pltpu.VMEM((EXPERTS_PER_GROUP, D_EXPERT, D_MODEL), BF16)])
    return pl.pallas_call(
        _moe_routed_kernel,
        out_shape=jax.ShapeDtypeStruct((ns, D_MODEL), BF16),
        grid_spec=grid_spec,
        compiler_params=_params(1),
        name="moe_routed",
    )(items, su, sg, wgu5, wd5)


def _resid_gather_kernel(tm, runs_ref, x_ref, g2_ref, col_ref, ys_hbm, y_ref, stage, sem):
    y_ref[...] = x_ref[...] + g2_ref[...] * _ungroup(tm, runs_ref, col_ref, ys_hbm, stage, sem)


def _resid_gather_call(x, runs, col, ys, mod, mspec, tm):
    n = x.shape[0]
    row = lambda i, *_: (i, 0)
    grid_spec = pltpu.PrefetchScalarGridSpec(
        num_scalar_prefetch=1,
        grid=(n // tm,),
        in_specs=[pl.BlockSpec((tm, D_MODEL), row), mspec(5), pl.BlockSpec((tm, LANES), row),
                  pl.BlockSpec(memory_space=pl.ANY)],
        out_specs=pl.BlockSpec((tm, D_MODEL), row),
        scratch_shapes=[pltpu.VMEM((2, _stage_rows(tm), D_MODEL), BF16), pltpu.SemaphoreType.DMA((2,))])
    return pl.pallas_call(
        functools.partial(_resid_gather_kernel, tm),
        out_shape=jax.ShapeDtypeStruct((n, D_MODEL), F32),
        grid_spec=grid_spec,
        compiler_params=_params(1),
        name="moe_unsort_resid",
    )(runs, x, mod, col, ys)


def _moe_kernel(expand, u_ref, gate_ref, wgu_ref, wd_ref, x1_ref, g2_ref, *rest):
    rep_ref = rest[0] if expand else None
    y_ref, acc_ref = rest[1 if expand else 0:]
    e = pl.program_id(1)

    @pl.when(e == 0)
    def _():
        acc_ref[...] = jnp.zeros_like(acc_ref)

    hu = _dot(u_ref[...], wgu_ref[...].astype(BF16))
    hg = hu[:, 0:D_EXPERT]
    hv = hu[:, D_EXPERT:2 * D_EXPERT]
    gate = gate_ref[...]
    lane = lax.broadcasted_iota(jnp.int32, gate.shape, 1)
    gcol = jnp.sum(jnp.where(lane == N_GROUPS + e, gate, 0.0), axis=-1, keepdims=True)
    act = hg * jax.nn.sigmoid(hg) * hv * gcol
    acc_ref[...] += _dot(act.astype(BF16), wd_ref[...].astype(BF16))

    @pl.when(e == N_EXPERTS - 1)
    def _():
        y_ref[...] = x1_ref[...] + _mod_rows(g2_ref, rep_ref) * acc_ref[...]


def _moe_call(u2, gate, wgu, wd, layer, x1, mod, mspec, rep, tm):
    n = u2.shape[0]
    row = lambda i, e: (i, 0)
    expand = rep is not None
    return pl.pallas_call(
        functools.partial(_moe_kernel, expand),
        out_shape=jax.ShapeDtypeStruct((n, D_MODEL), F32),
        grid=(n // tm, N_EXPERTS),
        in_specs=[pl.BlockSpec((tm, D_MODEL), row), pl.BlockSpec((tm, LANES), row),
                  pl.BlockSpec((None, None, D_MODEL, 2 * D_EXPERT), lambda i, e: (layer, e, 0, 0)),
                  pl.BlockSpec((None, None, D_EXPERT, D_MODEL), lambda i, e: (layer, e, 0, 0)),
                  pl.BlockSpec((tm, D_MODEL), row), mspec(5)]
        + ([pl.BlockSpec(rep.shape, lambda i, e: (0, 0))] if expand else []),
        out_specs=pl.BlockSpec((tm, D_MODEL), row),
        scratch_shapes=[pltpu.VMEM((tm, D_MODEL), F32)],
        compiler_params=_params(2),
        name="moe_dense",
    )(u2, gate, wgu, wd, x1, mod, *([rep] if expand else []))


def _rope128(x, cos, sa, sb):
    return x * cos + pltpu.roll(x, LANES - ROT_DIM // 2, axis=1) * sa \
        + pltpu.roll(x, ROT_DIM // 2, axis=1) * sb


def _kvq_kernel(gather_tm, *refs):
    if gather_tm:
        (runs_ref, x_ref, g2_ref, col_ref, ys_hbm, kvsh_ref, kvsc_ref, sh1_ref, sc1_ref, kvn_ref,
         nm_ref, wkv_ref, wq_ref, kn_ref, qn_ref, cos_ref, sa_ref, sb_ref, g64_ref, gr_ref, gb_ref,
         q_ref, k_ref, v_ref, xa_ref, stage, sem) = refs
        x = x_ref[...] + g2_ref[...] * _ungroup(gather_tm, runs_ref, col_ref, ys_hbm, stage, sem)
        xa_ref[...] = x
        rep_ref = None
    else:
        (x_ref, kvsh_ref, kvsc_ref, sh1_ref, sc1_ref, kvn_ref, nm_ref,
         wkv_ref, wq_ref, kn_ref, qn_ref, cos_ref, sa_ref, sb_ref, g64_ref, gr_ref, gb_ref,
         rep_ref, q_ref, k_ref, v_ref) = refs
        x = x_ref[...]
    xn = _unit_rms(x)
    cos, sa, sb = cos_ref[...], sa_ref[...], sb_ref[...]

    ukv = xn * (kvn_ref[...] * (1.0 + _mod_rows(kvsc_ref, rep_ref))) + _mod_rows(kvsh_ref, rep_ref)
    kv = _dot(ukv.astype(BF16), wkv_ref[...])
    k = kv[:, 0:LANES]
    v_ref[...] = kv[:, LANES:2 * LANES]
    kh, kl = _split2(k * k)
    ms = _dot(kh, g64_ref[...]) + _dot(kl, g64_ref[...])
    k_ref[...] = _rope128(k * lax.rsqrt(ms + RMS_EPS) * kn_ref[...], cos, sa, sb)

    u1 = xn * (nm_ref[...] * (1.0 + _mod_rows(sc1_ref, rep_ref))) + _mod_rows(sh1_ref, rep_ref)
    q = _dot(u1.astype(BF16), wq_ref[...])
    ms16 = _dot((q * q).astype(BF16), gr_ref[...])
    rh, rl = _split2(lax.rsqrt(ms16 + RMS_EPS))
    rsb = _dot(rh, gb_ref[...]) + _dot(rl, gb_ref[...])
    qn = q * rsb * qn_ref[...]
    scale = HEAD_DIM ** -0.5
    cos_q, sa_q, sb_q = cos * scale, sa * scale, sb * scale
    for i in range(D_MODEL // LANES):
        sl = slice(LANES * i, LANES * (i + 1))
        q_ref[:, sl] = _rope128(qn[:, sl], cos_q, sa_q, sb_q).astype(BF16)


def _kvq_call(x, routed, mod0, mod1, mspec, kvmod, kvspec, rep, kvn, nm, wkv, wq, kn, qn, tabs, tab,
              g64, gr, gb, tm):
    n = x.shape[0]
    row = lambda i, *_: (i, 0)
    const = lambda i, *_: (0, 0)
    vec = pl.BlockSpec((1, D_MODEL), const)
    tok = pl.BlockSpec((tm, D_MODEL), row)
    small = pl.BlockSpec((tm, LANES), row)
    ins = [x, kvmod, kvmod, mod1, mod1, kvn, nm, wkv, wq, kn, qn, *tabs, g64, gr, gb]
    specs = [tok, kvspec(0), kvspec(1), mspec(0), mspec(1), vec, vec,
             pl.BlockSpec((D_MODEL, 2 * LANES), const), pl.BlockSpec((D_MODEL, D_MODEL), const),
             pl.BlockSpec((1, LANES), const), vec, tab, tab, tab,
             pl.BlockSpec((LANES, LANES), const), pl.BlockSpec((D_MODEL, LANES), const),
             pl.BlockSpec((LANES, D_MODEL), const)]
    out_shape = [jax.ShapeDtypeStruct((n, D_MODEL), BF16), jax.ShapeDtypeStruct((n, LANES), F32),
                 jax.ShapeDtypeStruct((n, LANES), F32)]
    out_specs = [tok, small, small]
    scratch = []
    nprefetch = 0
    if routed is not None:
        runs, col, ys = routed
        nprefetch = 1
        ins = [runs, x, mod0, col, ys] + ins[1:]
        specs = [tok, mspec(5), small, pl.BlockSpec(memory_space=pl.ANY)] + specs[1:]
        out_shape.append(jax.ShapeDtypeStruct((n, D_MODEL), F32))
        out_specs.append(tok)
        scratch = [pltpu.VMEM((2, _stage_rows(tm), D_MODEL), BF16), pltpu.SemaphoreType.DMA((2,))]
    else:
        ins.append(rep)
        specs.append(pl.BlockSpec(rep.shape, const))
    grid_spec = pltpu.PrefetchScalarGridSpec(
        num_scalar_prefetch=nprefetch, grid=(n // tm,), in_specs=specs,
        out_specs=tuple(out_specs), scratch_shapes=scratch)
    return pl.pallas_call(
        functools.partial(_kvq_kernel, tm if routed is not None else 0),
        out_shape=tuple(out_shape),
        grid_spec=grid_spec,
        compiler_params=_params(1),
        name="kv_q_proj",
    )(*ins)


def _attn_core(q, kcat, vcat, bias, sinks_ref, o_ref):
    tk = kcat.shape[0]
    pairs = N_Q_HEADS // N_KV_HEADS // 2
    lane = lax.broadcasted_iota(jnp.int32, (1, LANES), 1)
    lo = lane < HEAD_DIM
    kro = pltpu.roll(kcat, HEAD_DIM, axis=1)
    vro = pltpu.roll(vcat, HEAD_DIM, axis=1)
    one_e = jnp.broadcast_to(jnp.where(lo, 1.0, 0.0), (tk, LANES))
    one_o = 1.0 - one_e
    for g in range(N_KV_HEADS):
        if g == 0:
            ke, ko = jnp.where(lo, kcat, 0.0), jnp.where(lo, 0.0, kro)
            ve, vo = jnp.where(lo, vcat, 0.0), jnp.where(lo, 0.0, vro)
        else:
            ke, ko = jnp.where(lo, kro, 0.0), jnp.where(lo, 0.0, kcat)
            ve, vo = jnp.where(lo, vro, 0.0), jnp.where(lo, 0.0, vcat)
        k2 = jnp.concatenate([ke, ko], 0).astype(BF16)
        v2 = jnp.concatenate([jnp.concatenate([ve, one_e], 1),
                              jnp.concatenate([vo, one_o], 1)], 0).astype(BF16)
        for p in range(pairs):
            hp = g * pairs + p
            s = _dot_nt(q[:, LANES * hp:LANES * (hp + 1)], k2)
            halves, corr = [], []
            for par in range(2):
                sp = s[:, par * tk:(par + 1) * tk] + bias
                sink = sinks_ref[2 * hp + par]
                m = jnp.maximum(jnp.max(sp, axis=-1, keepdims=True), sink)
                halves.append(jnp.exp(sp - m).astype(BF16))
                corr.append(jnp.exp(sink - m))
            o2 = _dot(jnp.concatenate(halves, 1), v2)
            den = o2[:, LANES:2 * LANES] + jnp.where(lo, corr[0], corr[1])
            o_ref[:, LANES * hp:LANES * (hp + 1)] = (o2[:, 0:LANES] / den).astype(o_ref.dtype)


ATTN_BLOCKS_PER_STEP = 4


def _attn_prompt_kernel(sinks_ref, q_ref, kp_ref, kc_ref, vp_ref, vc_ref, bias_ref, o_ref):
    kall = jnp.concatenate([kp_ref[...], kc_ref[...]], 0)
    vall = jnp.concatenate([vp_ref[...], vc_ref[...]], 0)
    first = jnp.minimum(pl.program_id(1), 1)
    for j in range(ATTN_BLOCKS_PER_STEP):
        rows = slice(j * WINDOW, (j + 1) * WINDOW)
        keys = slice(j * WINDOW, (j + 2) * WINDOW)
        bias = bias_ref[first] if j == 0 else bias_ref[1]
        _attn_core(q_ref[rows, :], kall[keys, :], vall[keys, :], bias, sinks_ref, o_ref.at[rows, :])


ATTN_SEQS_PER_STEP = 8


def _attn_sample_kernel(sinks_ref, q_ref, kc_ref, kn_ref, vc_ref, vn_ref, bias_ref, o_ref, kbuf, vbuf):
    @pl.when(pl.program_id(0) == 0)
    def _():
        kbuf[...] = jnp.zeros_like(kbuf)
        vbuf[...] = jnp.zeros_like(vbuf)

    for b in range(ATTN_SEQS_PER_STEP):
        kbuf[b, 0:WINDOW, :] = kc_ref[b]
        kbuf[b, WINDOW:WINDOW + SAMPLE_PAD, :] = kn_ref[b]
        vbuf[b, 0:WINDOW, :] = vc_ref[b]
        vbuf[b, WINDOW:WINDOW + SAMPLE_PAD, :] = vn_ref[b]
    for b in range(ATTN_SEQS_PER_STEP):
        _attn_core(q_ref[b], kbuf[b], vbuf[b], bias_ref[...], sinks_ref, o_ref.at[b])


def _window_bias(tq, first):
    qi = jnp.arange(tq)[:, None]
    kj = jnp.arange(2 * WINDOW)[None, :]
    ok = (kj > qi) & (kj <= qi + WINDOW)
    if first:
        ok = ok & (kj >= WINDOW)
    return jnp.where(ok, 0.0, -jnp.inf).astype(F32)


def _attn_prompt_call(sinks, q, k, v, B, nb):
    n = q.shape[0]
    per = ATTN_BLOCKS_PER_STEP
    steps = nb // per
    cur = lambda b, i: (b * steps + i, 0)
    prev = lambda b, i: (b * nb + jnp.maximum(per * i - 1, 0), 0)
    bias = jnp.stack([_window_bias(WINDOW, True), _window_bias(WINDOW, False)])
    kv_prev = pl.BlockSpec((WINDOW, LANES), prev)
    kv_cur = pl.BlockSpec((per * WINDOW, LANES), cur)
    return pl.pallas_call(
        _attn_prompt_kernel,
        out_shape=jax.ShapeDtypeStruct((n, D_MODEL), BF16),
        grid=(B, steps),
        in_specs=[pl.BlockSpec(memory_space=pltpu.SMEM),
                  pl.BlockSpec((per * WINDOW, D_MODEL), cur), kv_prev, kv_cur, kv_prev, kv_cur,
                  pl.BlockSpec((2, WINDOW, 2 * WINDOW), lambda b, i: (0, 0, 0))],
        out_specs=pl.BlockSpec((per * WINDOW, D_MODEL), cur),
        compiler_params=_params(2),
        name="swa_prompt",
    )(sinks, q, k, k, v, v, bias)


def _attn_sample_call(sinks, q, kcache, knew, vcache, vnew):
    B = q.shape[0]
    nseq = ATTN_SEQS_PER_STEP
    b3 = lambda b: (b, 0, 0)
    cache = pl.BlockSpec((nseq, WINDOW, LANES), b3)
    new = pl.BlockSpec((nseq, SAMPLE_PAD, LANES), b3)
    return pl.pallas_call(
        _attn_sample_kernel,
        out_shape=jax.ShapeDtypeStruct(q.shape, BF16),
        grid=(B // nseq,),
        in_specs=[pl.BlockSpec(memory_space=pltpu.SMEM),
                  pl.BlockSpec((nseq, SAMPLE_PAD, D_MODEL), b3), cache, new, cache, new,
                  pl.BlockSpec((SAMPLE_PAD, 2 * WINDOW), lambda b: (0, 0))],
        out_specs=pl.BlockSpec((nseq, SAMPLE_PAD, D_MODEL), b3),
        scratch_shapes=[pltpu.VMEM((nseq, 2 * WINDOW, LANES), F32),
                        pltpu.VMEM((nseq, 2 * WINDOW, LANES), F32)],
        compiler_params=_params(1),
        name="swa_sample",
    )(sinks, q, kcache, knew, vcache, vnew, _window_bias(SAMPLE_PAD, False))


def _rope_tables(pos):
    half = ROT_DIM // 2
    inv = ROPE_THETA ** (-jnp.arange(half, dtype=F32) / half)
    ang = pos.astype(F32)[:, None] * inv[None]
    cos, sin = jnp.cos(ang), jnp.sin(ang)
    d = jnp.arange(LANES) % HEAD_DIM
    idx = d % half
    cos_t = jnp.where(d < ROT_DIM, cos[:, idx], 1.0)
    sa = jnp.where(d < half, -sin[:, idx], 0.0)
    sb = jnp.where((d >= half) & (d < ROT_DIM), sin[:, idx], 0.0)
    return cos_t, sa, sb


def _pad_lanes(a, value=0.0):
    return jnp.pad(a, ((0, 0), (0, LANES - a.shape[1])), constant_values=value)


def _prep_weights(ada_w, ada_b, norm_mix, norm_ffn, a_w_in, a_b_gates, a_head_norm, a_w_out,
                  kv_ada_w, kv_ada_b, kv_norm, w_k, w_v, k_norm, b_w_q, b_q_norm, b_sinks, b_w_o,
                  moe_w_group, moe_b_group, moe_w_expert, moe_b_expert, moe_w_gate_up, moe_w_down):
    w = {}
    g0 = QK_COLS + 2 * V_COLS
    w["w_in"] = a_w_in
    w["w_gates"] = jnp.concatenate([_pad_lanes(a_w_in[0, :, g0:g0 + M_HEADS]),
                                    _pad_lanes(a_w_in[0, :, g0 + M_HEADS:])], 1)
    w["bli"] = _pad_lanes(a_b_gates[0][None, :M_HEADS])
    w["blf"] = _pad_lanes(a_b_gates[0][None, M_HEADS:])
    w["head_norm"] = a_head_norm[0][None]
    w["w_out"] = a_w_out[0].astype(BF16)
    w["norm_mix"] = [norm_mix[l][None] for l in range(2)]
    w["norm_ffn"] = [norm_ffn[l][None] for l in range(2)]
    w["router"] = []
    for l in range(2):
        wr = _pad_lanes(jnp.concatenate([moe_w_group[l], moe_w_expert[l]], 1))
        hi = wr.astype(BF16)
        lo = (wr - hi.astype(F32)).astype(BF16)
        br = _pad_lanes(jnp.concatenate([moe_b_group[l], moe_b_expert[l]])[None])
        w["router"].append((hi, lo, br))
    w["w_gu"] = moe_w_gate_up
    w["w_d"] = moe_w_down
    w["kv_norm"] = kv_norm[None]
    w["w_kv"] = jnp.concatenate([w_k, w_v], 1).astype(BF16)
    w["k_norm"] = jnp.tile(k_norm, N_KV_HEADS)[None]
    w["w_q"] = b_w_q[0].astype(BF16)
    w["q_norm"] = jnp.tile(b_q_norm[0], N_Q_HEADS)[None]
    w["sinks"] = b_sinks[0]
    w["w_o"] = b_w_o[0].astype(BF16)
    lanes = jnp.arange(LANES)
    feat = jnp.arange(D_MODEL)
    w["g64"] = jnp.where((lanes[:, None] // HEAD_DIM) == (lanes[None, :] // HEAD_DIM),
                         1.0 / HEAD_DIM, 0.0).astype(BF16)
    w["gr"] = jnp.where((feat[:, None] // HEAD_DIM) == lanes[None, :], 1.0 / HEAD_DIM, 0.0).astype(BF16)
    w["gb"] = jnp.where(lanes[:, None] == (feat[None, :] // HEAD_DIM), 1.0, 0.0).astype(BF16)
    return w


def _trunk(x2, mods, kvmod, mspec, mspec_big, rep, w, *, B, T, L, tm, tm_big, tabs, tabspec, c0, n0, m0,
           cache_k=None, cache_v=None):
    sample = cache_k is not None
    nc = T // L if not sample else 1

    qk, v, o, li, lf = _inproj_call(x2, mods[0], mspec_big, rep, w["norm_mix"][0], w["w_in"],
                                    w["w_gates"], w["bli"], w["blf"], tm_big)
    if sample:
        def padtok(a, value=0.0):
            a = a.reshape(B, T, a.shape[-1])
            a = jnp.pad(a, ((0, 0), (0, L - T), (0, 0)), constant_values=value)
            return a.reshape(B * L, a.shape[-1])
        qk, v, li, lf = padtok(qk), padtok(v), padtok(li, M_EMPTY), padtok(lf)
    h, c_new, n_new, m_new = _mlstm_call(qk, v, li, lf, c0, n0, m0, B, nc, L)
    if sample:
        h = h.reshape(B, L, V_COLS)[:, :T].reshape(B * T, V_COLS)
    routed = not sample
    ntiles = (B * T) // tm

    def routed_moe(u2, gate, gid, col, layer):
        q, runs, items = _sort_meta_call(gid.reshape(ntiles, tm), tm)
        su, sg = _scatter_call(runs, q.reshape(ntiles, 1, tm), u2, gate, tm)
        ys = _moe_routed_call(items, su, sg, w["w_gu"], w["w_d"], layer, tm)
        return runs, col, ys

    rh, rl, br = w["router"][0]
    post0 = _post_call(True, routed, h, o, x2, mods[0], mspec, rep, w["norm_ffn"][0], w["head_norm"],
                       w["w_out"], rh, rl, br, tm)
    kvq_args = (mods[0], mods[1], mspec, kvmod, mspec, rep, w["kv_norm"], w["norm_mix"][1],
                w["w_kv"], w["w_q"], w["k_norm"], w["q_norm"], tabs, tabspec,
                w["g64"], w["gr"], w["gb"], tm)

    if routed:
        x1, u2, gate, gid, col = post0
        q, k, vv, xa = _kvq_call(x1, routed_moe(u2, gate, gid, col, 0), *kvq_args)
    else:
        x1, u2, gate = post0
        xa = _moe_call(u2, gate, w["w_gu"], w["w_d"], 0, x1, mods[0], mspec_big, rep, tm_big)
        q, k, vv = _kvq_call(xa, None, *kvq_args)
    if not sample:
        att = _attn_prompt_call(w["sinks"], q, k, vv, B, T // WINDOW)
        k_win = k.reshape(B, T, LANES)[:, T - WINDOW:].reshape(B, WINDOW, N_KV_HEADS, HEAD_DIM)
        v_win = vv.reshape(B, T, LANES)[:, T - WINDOW:].reshape(B, WINDOW, N_KV_HEADS, HEAD_DIM)
    else:
        def padseq(a):
            return jnp.pad(a.reshape(B, T, a.shape[-1]), ((0, 0), (0, SAMPLE_PAD - T), (0, 0)))
        kc = cache_k.reshape(B, WINDOW, LANES)
        vc = cache_v.reshape(B, WINDOW, LANES)
        att = _attn_sample_call(w["sinks"], padseq(q), kc, padseq(k), vc, padseq(vv))
        att = att[:, :T].reshape(B * T, D_MODEL)
        k_win = jnp.concatenate([kc[:, T:], k.reshape(B, T, LANES)], 1)
        v_win = jnp.concatenate([vc[:, T:], vv.reshape(B, T, LANES)], 1)
        k_win = k_win.reshape(B, WINDOW, N_KV_HEADS, HEAD_DIM)
        v_win = v_win.reshape(B, WINDOW, N_KV_HEADS, HEAD_DIM)
    rh, rl, br = w["router"][1]
    post1 = _post_call(False, routed, att, None, xa, mods[1], mspec, rep, w["norm_ffn"][1], None,
                       w["w_o"], rh, rl, br, tm)
    if routed:
        x3, u4, gate, gid, col = post1
        runs, col, ys = routed_moe(u4, gate, gid, col, 1)
        y = _resid_gather_call(x3, runs, col, ys, mods[1], mspec, tm)
    else:
        x3, u4, gate = post1
        y = _moe_call(u4, gate, w["w_gu"], w["w_d"], 1, x3, mods[1], mspec_big, rep, tm_big)
    c_out = c_new[None]
    n_out = n_new.reshape(1, B, M_HEADS, M_DK)
    m_out = m_new[:, 0, :M_HEADS][None]
    return y, c_out, n_out, m_out, k_win, v_win


def kernel(x_prompt, x_sample, c_prompt, c_sample, state_c, state_n, state_m, cache_k_win, cache_v_win, ada_w, ada_b, norm_mix, norm_ffn, a_w_in, a_b_gates, a_head_norm, a_w_out, kv_ada_w, kv_ada_b, kv_norm, w_k, w_v, k_norm, b_w_q, b_q_norm, b_sinks, b_w_o, moe_w_group, moe_b_group, moe_w_expert, moe_b_expert, moe_w_gate_up, moe_w_down):
    Bp, Tp, D = x_prompt.shape
    Bs, Ts, _ = x_sample.shape
    w = _prep_weights(ada_w, ada_b, norm_mix, norm_ffn, a_w_in, a_b_gates, a_head_norm, a_w_out,
                      kv_ada_w, kv_ada_b, kv_norm, w_k, w_v, k_norm, b_w_q, b_q_norm, b_sinks, b_w_o,
                      moe_w_group, moe_b_group, moe_w_expert, moe_b_expert, moe_w_gate_up, moe_w_down)

    rows = Bp + Bs
    rpad = -rows % 8
    c_all = jnp.concatenate([c_prompt, c_sample, jnp.zeros((rpad, D), F32)], 0)
    mod = _ada_call(c_all, ada_w, ada_b[:, None, :])
    kvm = _ada_call(c_all, kv_ada_w[None], kv_ada_b[None, None, :])

    tm_p = TOKEN_TILE
    tiles_per_seq = Tp // tm_p
    mods_p = [mod[l, :Bp][:, None, :] for l in range(2)]
    kvmod_p = kvm[0, :Bp][:, None, :]

    def mspec_p(col):
        return pl.BlockSpec((None, 1, D_MODEL), lambda i, *_: (i // tiles_per_seq, 0, col))

    big_tiles_per_seq = Tp // BIG_TOKEN_TILE

    def mspec_big_p(col):
        return pl.BlockSpec((None, 1, D_MODEL), lambda i, *_: (i // big_tiles_per_seq, 0, col))

    tabs_p = _rope_tables(jnp.arange(Tp, dtype=jnp.int32))
    tabspec_p = pl.BlockSpec((tm_p, LANES), lambda i, *_: (i % tiles_per_seq, 0))
    npairs = M_HEADS // 2
    c0 = jnp.zeros((Bp, M_HEADS, M_DK, M_DV), F32)
    n0 = jnp.zeros((Bp, npairs, LANES), F32)
    m0 = jnp.pad(jnp.full((Bp, 1, M_HEADS), M_EMPTY, F32), ((0, 0), (0, 0), (0, LANES - M_HEADS)))
    yp, cp, np_, mp, kwp, vwp = _trunk(
        x_prompt.reshape(Bp * Tp, D), mods_p, kvmod_p, mspec_p, mspec_big_p, None, w,
        B=Bp, T=Tp, L=M_CHUNK, tm=tm_p, tm_big=BIG_TOKEN_TILE, tabs=tabs_p, tabspec=tabspec_p,
        c0=c0, n0=n0, m0=m0)

    ns = Bs * Ts
    mods_s = [mod[l, Bp:Bp + Bs][None] for l in range(2)]
    kvmod_s = kvm[0, Bp:Bp + Bs][None]
    rep = (jnp.arange(ns)[:, None] // Ts == jnp.arange(Bs)[None, :]).astype(BF16)

    def mspec_s(col):
        return pl.BlockSpec((None, Bs, D_MODEL), lambda i, *_: (0, 0, col))

    tabs_s = _rope_tables(PAST_LEN + jnp.arange(Ts, dtype=jnp.int32))
    tabs_s = tuple(jnp.tile(t, (Bs, 1)) for t in tabs_s)
    m0s = jnp.pad(state_m[0][:, None, :], ((0, 0), (0, 0), (0, LANES - M_HEADS)))
    ys, cs, ns_, ms, kws, vws = _trunk(
        x_sample.reshape(ns, D), mods_s, kvmod_s, mspec_s, mspec_s, rep, w,
        B=Bs, T=Ts, L=SAMPLE_PAD, tm=ns, tm_big=ns, tabs=tabs_s,
        tabspec=pl.BlockSpec((ns, LANES), lambda i, *_: (0, 0)),
        c0=state_c[0], n0=state_n[0].reshape(Bs, npairs, LANES), m0=m0s,
        cache_k=cache_k_win, cache_v=cache_v_win)

    return (yp.reshape(Bp, Tp, D), ys.reshape(Bs, Ts, D), cp, np_, mp, kwp, vwp,
            cs, ns_, ms, kws, vws)
```

```python
import functools

import jax
import jax.numpy as jnp
from jax import lax
from jax.experimental import pallas as pl
from jax.experimental.pallas import tpu as pltpu

F32 = jnp.float32
BF16 = jnp.bfloat16

D_MODEL = 1024
PAST_LEN = 8192
M_HEADS = 8
M_DK = 64
M_DV = 128
M_CHUNK = 128
GATE_SOFTCAP = 15.0
M_EMPTY = -1e30
WINDOW = 128
HEAD_DIM = 64
N_Q_HEADS = 16
N_KV_HEADS = 2
ROPE_THETA = 500000.0
ROT_DIM = 16
N_GROUPS = 4
EXPERTS_PER_GROUP = 4
N_EXPERTS = 16
D_EXPERT = 256
RMS_EPS = 1e-6

LANES = 128
QK_COLS = 2 * M_HEADS * M_DK
V_COLS = M_HEADS * M_DV
SAMPLE_PAD = 16
TOKEN_TILE = 512
BIG_TOKEN_TILE = 1024
VMEM_LIMIT = 52 * 1024 * 1024

NT_DIMS = (((1,), (1,)), ((), ()))


def _params(n_axes):
    return pltpu.CompilerParams(dimension_semantics=("arbitrary",) * n_axes,
                                vmem_limit_bytes=VMEM_LIMIT)


def _dot(a, b):
    return jnp.dot(a, b, preferred_element_type=F32)


def _dot_nt(a, b):
    return lax.dot_general(a, b, NT_DIMS, preferred_element_type=F32)


def _split2(x):
    hi = x.astype(BF16)
    lo = (x - hi.astype(F32)).astype(BF16)
    return hi, lo


def _split3(x):
    hi = x.astype(BF16)
    r = x - hi.astype(F32)
    mid = r.astype(BF16)
    lo = (r - mid.astype(F32)).astype(BF16)
    return hi, mid, lo


def _unit_rms(x):
    return x * lax.rsqrt(jnp.mean(x * x, axis=-1, keepdims=True) + RMS_EPS)


def _ada_kernel(c_ref, w_ref, b_ref, o_ref):
    c = c_ref[...]
    cs = (c * jax.nn.sigmoid(c)).astype(BF16)
    o_ref[...] = _dot(cs, w_ref[...].astype(BF16)) + b_ref[...]


def _ada_call(c, w, b):
    g, d, n = w.shape
    r = c.shape[0]
    tn = 1024
    return pl.pallas_call(
        _ada_kernel,
        out_shape=jax.ShapeDtypeStruct((g, r, n), F32),
        grid=(g, n // tn),
        in_specs=[pl.BlockSpec((r, d), lambda i, j: (0, 0)),
                  pl.BlockSpec((None, d, tn), lambda i, j: (i, 0, j)),
                  pl.BlockSpec((None, 1, tn), lambda i, j: (i, 0, j))],
        out_specs=pl.BlockSpec((None, r, tn), lambda i, j: (i, 0, j)),
        compiler_params=_params(2),
        name="ada_mod",
    )(c, w, b)


def _mod_rows(ref, rep_ref):
    if rep_ref is None:
        return ref[...]
    hi, mid, lo = _split3(ref[...])
    rep = rep_ref[...]
    return _dot(rep, hi) + _dot(rep, mid) + _dot(rep, lo)


def _inproj_kernel(expand, x_ref, sh_ref, sc_ref, nw_ref, w_ref, wg_ref, bli_ref, blf_ref, *rest):
    rep_ref = rest[0] if expand else None
    qk_ref, v_ref, o_ref, li_ref, lf_ref, wb_ref = rest[1 if expand else 0:]

    @pl.when(pl.program_id(0) == 0)
    def _():
        wb_ref[...] = w_ref[...].astype(BF16)

    u = _unit_rms(x_ref[...]) * (nw_ref[...] * (1.0 + _mod_rows(sc_ref, rep_ref))) \
        + _mod_rows(sh_ref, rep_ref)
    ub = u.astype(BF16)
    half = QK_COLS // 2
    q = _dot(ub, wb_ref[:, 0:half]) * (M_DK ** -0.5)
    qk_ref[:, 0:half] = q.astype(BF16)
    qk_ref[:, half:QK_COLS] = _dot(ub, wb_ref[:, half:QK_COLS]).astype(BF16)
    v_ref[...] = _dot(ub, wb_ref[:, QK_COLS:QK_COLS + V_COLS]).astype(BF16)
    o_ref[...] = jax.nn.sigmoid(_dot(ub, wb_ref[:, QK_COLS + V_COLS:QK_COLS + 2 * V_COLS])).astype(BF16)
    lane = lax.broadcasted_iota(jnp.int32, (1, LANES), 1)
    live = lane < M_HEADS
    wg = wg_ref[...].astype(BF16)
    gi = _dot(ub, wg[:, 0:LANES]) + bli_ref[...]
    gf = _dot(ub, wg[:, LANES:2 * LANES]) + blf_ref[...]
    li = GATE_SOFTCAP * jnp.tanh(gi / GATE_SOFTCAP)
    fpre = GATE_SOFTCAP * jnp.tanh(gf / GATE_SOFTCAP)
    lf = jnp.minimum(fpre, 0.0) - jnp.log1p(jnp.exp(-jnp.abs(fpre)))
    li_ref[...] = jnp.where(live, li, 0.0)
    lf_ref[...] = jnp.where(live, lf, 0.0)


def _inproj_call(x, mod, mspec, rep, nw, w_in, wg, bli, blf, tm):
    n = x.shape[0]
    row = lambda i: (i, 0)
    const = lambda i: (0, 0)
    main_cols = QK_COLS + 2 * V_COLS
    expand = rep is not None
    return pl.pallas_call(
        functools.partial(_inproj_kernel, expand),
        out_shape=(jax.ShapeDtypeStruct((n, QK_COLS), BF16),
                   jax.ShapeDtypeStruct((n, V_COLS), BF16),
                   jax.ShapeDtypeStruct((n, V_COLS), BF16),
                   jax.ShapeDtypeStruct((n, LANES), F32),
                   jax.ShapeDtypeStruct((n, LANES), F32)),
        grid=(n // tm,),
        in_specs=[pl.BlockSpec((tm, D_MODEL), row), mspec(0), mspec(1),
                  pl.BlockSpec((1, D_MODEL), const),
                  pl.BlockSpec((None, D_MODEL, main_cols), lambda i: (0, 0, 0),
                               pipeline_mode=pl.Buffered(1)),
                  pl.BlockSpec((D_MODEL, 2 * LANES), const),
                  pl.BlockSpec((1, LANES), const), pl.BlockSpec((1, LANES), const)]
        + ([pl.BlockSpec(rep.shape, const)] if expand else []),
        out_specs=(pl.BlockSpec((tm, QK_COLS), row), pl.BlockSpec((tm, V_COLS), row),
                   pl.BlockSpec((tm, V_COLS), row), pl.BlockSpec((tm, LANES), row),
                   pl.BlockSpec((tm, LANES), row)),
        scratch_shapes=[pltpu.VMEM((D_MODEL, main_cols), BF16)],
        compiler_params=_params(1),
        name="mlstm_inproj",
    )(x, mod, mod, nw, w_in, wg, bli, blf, *([rep] if expand else []))


def _mlstm_kernel(L, nseq, scale_q, *refs):
    c = pl.program_id(1)
    seqs = [[r.at[b] for r in refs] for b in range(nseq)]

    @pl.when(c == 0)
    def _():
        for s in seqs:
            _mlstm_load_state(*s[4:7], *s[11:14])

    for s in seqs:
        _mlstm_seq_step(L, scale_q, *s[0:4], s[7], *s[11:14])

    @pl.when(c == pl.num_programs(1) - 1)
    def _():
        for s in seqs:
            _mlstm_store_state(*s[8:14])


def _mlstm_load_state(c0_ref, n0_ref, m0_ref, c2_s, n_s, m_s):
    m_s[...] = m0_ref[...]
    n_s[...] = n0_ref[...]
    z = jnp.zeros((M_DK, M_DV), F32)
    for j in range(M_HEADS // 2):
        c2_s[j] = jnp.concatenate([jnp.concatenate([c0_ref[2 * j], z], 1),
                                   jnp.concatenate([z, c0_ref[2 * j + 1]], 1)], 0)


def _mlstm_seq_step(L, scale_q, qk_ref, v_ref, li_ref, lf_ref, h_ref, c2_s, n_s, m_s):
    npairs = M_HEADS // 2
    hd = M_DK

    lane = lax.broadcasted_iota(jnp.int32, (1, LANES), 1)
    lo128 = lane < hd
    lane256 = lax.broadcasted_iota(jnp.int32, (1, 2 * LANES), 1)
    lo256 = lane256 < LANES
    row128 = lax.broadcasted_iota(jnp.int32, (LANES, 1), 0)
    top = row128 < hd
    blockdiag = (top & lo256) | (jnp.logical_not(top) & jnp.logical_not(lo256))

    LI = li_ref[...]
    LF = lf_ref[...]
    rowL = lax.broadcasted_iota(jnp.int32, (L, LANES), 0)

    def prefix(x, op, ident):
        d = 1
        while d < L:
            shifted = pltpu.roll(x, d, axis=0)
            x = op(x, jnp.where(rowL >= d, shifted, ident))
            d *= 2
        return x

    Bc = prefix(LF, jnp.add, 0.0)
    Cm = LI - Bc
    mprev = m_s[...]
    Gc = jnp.maximum(mprev, prefix(Cm, jnp.maximum, -jnp.inf))
    A = jnp.exp(mprev - Gc)
    bL = Bc[L - 1:L, :]
    DL = bL + Cm
    mnew = jnp.maximum(bL + mprev, jnp.max(DL, axis=0, keepdims=True))
    ast = jnp.exp(bL + mprev - mnew)
    WST = jnp.exp(DL - mnew)

    def pad_rows(x, rows):
        if x.shape[0] == rows:
            return x
        return jnp.concatenate([x, jnp.zeros((rows - x.shape[0], x.shape[1]), x.dtype)], 0)

    kw_lanes = max(L, hd)
    assert kw_lanes in (hd, LANES) and L <= LANES
    wide = kw_lanes == LANES
    lane_s = lax.broadcasted_iota(jnp.int32, (1, 2 * kw_lanes), 1)
    row_s = lax.broadcasted_iota(jnp.int32, (2 * kw_lanes, 1), 0)
    cm_pad = pad_rows(Cm, kw_lanes)
    if wide:
        XT = cm_pad.T
    else:
        XT = jnp.concatenate([cm_pad, cm_pad], 0).T
    causal = (lane_s & (kw_lanes - 1)) <= lax.broadcasted_iota(jnp.int32, (L, 2 * kw_lanes), 0)
    top_s = row_s < kw_lanes
    J = jnp.where((top_s & lo256) | (jnp.logical_not(top_s) & jnp.logical_not(lo256)),
                  1.0, 0.0).astype(BF16)

    EXPM = jnp.exp(-(Bc + Gc))

    def bc(X, h):
        return jnp.broadcast_to(X[:, h:h + 1], X.shape)

    for j in range(npairs):
        he, ho = 2 * j, 2 * j + 1
        p128 = slice(LANES * j, LANES * (j + 1))
        p256 = slice(2 * LANES * j, 2 * LANES * (j + 1))

        def pair128(X):
            return jnp.where(lo128, bc(X, he), bc(X, ho))

        def pair256(X):
            return jnp.concatenate([bc(X, he), bc(X, ho)], 1)

        q128 = qk_ref[:, LANES * j:LANES * (j + 1)]
        k128 = qk_ref[:, QK_COLS // 2 + LANES * j:QK_COLS // 2 + LANES * (j + 1)]
        v256 = v_ref[:, 2 * LANES * j:2 * LANES * (j + 1)]
        zk = jnp.zeros_like(k128)
        zv = jnp.zeros_like(v256)
        K2t = jnp.concatenate([pad_rows(jnp.where(lo128, k128, zk), kw_lanes),
                               pad_rows(jnp.where(lo128, zk, k128), kw_lanes)], 0)
        V2 = jnp.concatenate([pad_rows(jnp.where(lo256, v256, zv), kw_lanes),
                              pad_rows(jnp.where(lo256, zv, v256), kw_lanes)], 0)
        S = _dot_nt(q128, K2t)
        if wide:
            crow = jnp.concatenate([XT[he:he + 1, :], XT[ho:ho + 1, :]], 1)
            arg = jnp.where(causal, crow - pair256(Gc), -jnp.inf)
        else:
            crow = jnp.where(lo128, XT[he:he + 1, :], XT[ho:ho + 1, :])
            arg = jnp.where(causal, crow - pair128(Gc), -jnp.inf)
        Sw = (S * jnp.exp(arg)).astype(BF16)
        num_intra = _dot(Sw, V2)
        rowsum = _dot(Sw, J)
        C2 = c2_s[j]
        npair = n_s[j:j + 1, :]
        Nrow2 = jnp.concatenate(
            [jnp.broadcast_to(jnp.where(lo128, npair, 0.0), (LANES, LANES)),
             jnp.broadcast_to(jnp.where(lo128, 0.0, npair), (LANES, LANES))], 0).astype(BF16)
        if scale_q:
            qa = (q128.astype(F32) * pair128(A)).astype(BF16)
            num = _dot(qa, C2.astype(BF16)) + num_intra
            den = _dot_nt(qa, Nrow2) + rowsum
        else:
            a256 = pair256(A)
            num = a256 * _dot(q128, C2.astype(BF16)) + num_intra
            den = a256 * _dot_nt(q128, Nrow2) + rowsum
        h = num / jnp.maximum(jnp.abs(den), pair256(EXPM))
        h_ref[:, p256] = h.astype(h_ref.dtype)

        kw = k128.astype(F32) * pair128(WST)
        n_s[j:j + 1, :] = pair128(ast) * npair + jnp.sum(kw, axis=0, keepdims=True)
        kwT = pad_rows(kw, LANES).T.astype(BF16)
        dC = _dot(kwT, pad_rows(v256, LANES))
        c2_s[j] = pair256(ast) * C2 + jnp.where(blockdiag, dC, 0.0)

    m_s[...] = mnew


def _mlstm_store_state(cout_ref, nout_ref, mout_ref, c2_s, n_s, m_s):
    for j in range(M_HEADS // 2):
        C2 = c2_s[j]
        cout_ref[2 * j] = C2[0:M_DK, 0:M_DV]
        cout_ref[2 * j + 1] = C2[M_DK:2 * M_DK, M_DV:2 * M_DV]
    nout_ref[...] = n_s[...]
    mout_ref[...] = m_s[...]


MLSTM_SEQS_PER_STEP = 4


def _mlstm_call(qk, v, li, lf, c0, n0, m0, B, nc, L):
    npairs = M_HEADS // 2
    nseq = min(B, MLSTM_SEQS_PER_STEP)
    T = nc * L
    tok3 = lambda a: a.reshape(B, T, a.shape[-1])
    tok = lambda b, c: (b, c, 0)
    st4 = lambda b, c: (b, 0, 0, 0)
    st3 = lambda b, c: (b, 0, 0)
    h, c_new, n_new, m_new = pl.pallas_call(
        functools.partial(_mlstm_kernel, L, nseq, nc == 1),
        out_shape=(jax.ShapeDtypeStruct((B, T, V_COLS), BF16),
                   jax.ShapeDtypeStruct((B, M_HEADS, M_DK, M_DV), F32),
                   jax.ShapeDtypeStruct((B, npairs, LANES), F32),
                   jax.ShapeDtypeStruct((B, 1, LANES), F32)),
        grid=(B // nseq, nc),
        in_specs=[pl.BlockSpec((nseq, L, QK_COLS), tok), pl.BlockSpec((nseq, L, V_COLS), tok),
                  pl.BlockSpec((nseq, L, LANES), tok), pl.BlockSpec((nseq, L, LANES), tok),
                  pl.BlockSpec((nseq, M_HEADS, M_DK, M_DV), st4),
                  pl.BlockSpec((nseq, npairs, LANES), st3),
                  pl.BlockSpec((nseq, 1, LANES), st3)],
        out_specs=(pl.BlockSpec((nseq, L, V_COLS), tok),
                   pl.BlockSpec((nseq, M_HEADS, M_DK, M_DV), st4),
                   pl.BlockSpec((nseq, npairs, LANES), st3),
                   pl.BlockSpec((nseq, 1, LANES), st3)),
        scratch_shapes=[pltpu.VMEM((nseq, npairs, 2 * M_DK, 2 * M_DV), F32),
                        pltpu.VMEM((nseq, npairs, LANES), F32),
                        pltpu.VMEM((nseq, 1, LANES), F32)],
        compiler_params=_params(2),
        name="mlstm_chunks",
    )(tok3(qk), tok3(v), tok3(li), tok3(lf), c0, n0, m0)
    return h.reshape(B * T, V_COLS), c_new, n_new, m_new


def _route(lg):
    lane = lax.broadcasted_iota(jnp.int32, lg.shape, 1)
    lanef = lane.astype(F32)
    neg = -jnp.inf
    far = float(LANES)
    gm = lane < N_GROUPS
    lgm = jnp.where(gm, lg, neg)
    gmax = jnp.max(lgm, axis=-1, keepdims=True)
    gsum = jnp.sum(jnp.exp(lgm - gmax), axis=-1, keepdims=True)
    g_w = 1.0 / gsum
    gidx = jnp.min(jnp.where(gm & (lg == gmax), lanef, far), axis=-1, keepdims=True)
    first = N_GROUPS + EXPERTS_PER_GROUP * gidx
    sel = (lanef >= first) & (lanef < first + EXPERTS_PER_GROUP)
    l1 = jnp.max(jnp.where(sel, lg, neg), axis=-1, keepdims=True)
    i1 = jnp.min(jnp.where(sel & (lg == l1), lanef, far), axis=-1, keepdims=True)
    sel2 = sel & (lanef != i1)
    l2 = jnp.max(jnp.where(sel2, lg, neg), axis=-1, keepdims=True)
    i2 = jnp.min(jnp.where(sel2 & (lg == l2), lanef, far), axis=-1, keepdims=True)
    r = jnp.exp(l2 - l1)
    w1 = g_w / (1.0 + r)
    w2 = w1 * r
    return jnp.where(lanef == i1, w1, jnp.where(lanef == i2, w2, 0.0)), gidx


def _post_kernel(mlstm, routed, expand, *refs):
    refs = list(refs)
    h_ref = refs.pop(0)
    o_ref = refs.pop(0) if mlstm else None
    x_ref, g1_ref, sh2_ref, sc2_ref, nf_ref = refs[:5]
    refs = refs[5:]
    hn_ref = refs.pop(0) if mlstm else None
    wout_ref, wrhl_ref, br_ref = refs[:3]
    refs = refs[3:]
    tri_ref = refs.pop(0) if routed else None
    rep_ref = refs.pop(0) if expand else None
    x1_ref, u2_ref, gate_ref = refs[:3]
    outs = refs[3:]
    if mlstm:
        hf = h_ref[...].astype(F32)
        parts = [_unit_rms(hf[:, M_DV * i:M_DV * (i + 1)]) for i in range(M_HEADS)]
        hn = jnp.concatenate(parts, 1) * hn_ref[...]
        hg = (hn * o_ref[...].astype(F32)).astype(BF16)
    else:
        hg = h_ref[...]
    x1 = x_ref[...] + _mod_rows(g1_ref, rep_ref) * _dot(hg, wout_ref[...])
    x1_ref[...] = x1
    u2 = _unit_rms(x1) * (nf_ref[...] * (1.0 + _mod_rows(sc2_ref, rep_ref))) + _mod_rows(sh2_ref, rep_ref)
    uh, ul = _split2(u2)
    wrh = wrhl_ref[:, 0:LANES]
    lg = _dot(uh, wrh) + _dot(ul, wrh) + _dot(uh, wrhl_ref[:, LANES:2 * LANES]) + br_ref[...]
    gate, gidx = _route(lg)
    u2_ref[...] = uh
    gate_ref[...] = gate
    if not routed:
        return
    gid_ref, col_ref = outs
    rows = []
    for blk in range(x1.shape[0] // LANES):
        col = jnp.broadcast_to(gidx[LANES * blk:LANES * (blk + 1), :], (LANES, LANES))
        rows.append(col.T[0:1, :])
    gid_ref[...] = jnp.concatenate(rows, 1)
    lanef = lax.broadcasted_iota(jnp.int32, gate.shape, 1).astype(F32)
    onehot = jnp.where(lanef == gidx, 1.0, 0.0)
    earlier = _dot(tri_ref[...], onehot.astype(BF16))
    rank = jnp.sum(onehot * earlier, axis=-1, keepdims=True)
    col_ref[...] = jnp.where(lanef == 0.0, gidx, jnp.where(lanef == 1.0, rank, 0.0))


def _post_call(mlstm, routed, h, o, x, mod, mspec, rep, nf, hn, wout, wrhl, br, tm):
    n = x.shape[0]
    expand = rep is not None
    row = lambda i: (i, 0)
    const = lambda i: (0, 0)
    tok = pl.BlockSpec((tm, D_MODEL), row)
    small = pl.BlockSpec((tm, LANES), row)
    vec = pl.BlockSpec((1, D_MODEL), const)
    ins = [h] + ([o] if mlstm else []) + [x, mod, mod, mod, nf] + ([hn] if mlstm else []) \
        + [wout, wrhl, br]
    specs = [tok] + ([tok] if mlstm else []) + [tok, mspec(2), mspec(3), mspec(4), vec] \
        + ([vec] if mlstm else []) \
        + [pl.BlockSpec((D_MODEL, D_MODEL), const), pl.BlockSpec((D_MODEL, 2 * LANES), const),
           pl.BlockSpec((1, LANES), const)]
    out_shape = [jax.ShapeDtypeStruct((n, D_MODEL), F32), jax.ShapeDtypeStruct((n, D_MODEL), BF16),
                 jax.ShapeDtypeStruct((n, LANES), F32)]
    out_specs = [tok, tok, small]
    if routed:
        t = jnp.arange(tm)
        ins.append((t[None, :] < t[:, None]).astype(BF16))
        specs.append(pl.BlockSpec((tm, tm), const))
        out_shape += [jax.ShapeDtypeStruct((n // tm, 1, tm), F32), jax.ShapeDtypeStruct((n, LANES), F32)]
        out_specs += [pl.BlockSpec((None, 1, tm), lambda i: (i, 0, 0)), small]
    if expand:
        ins.append(rep)
        specs.append(pl.BlockSpec(rep.shape, const))
    out_shape, out_specs = tuple(out_shape), tuple(out_specs)
    return pl.pallas_call(
        functools.partial(_post_kernel, mlstm, routed, expand),
        out_shape=out_shape,
        grid=(n // tm,),
        in_specs=specs,
        out_specs=out_specs,
        compiler_params=_params(1),
        name="post_mlstm" if mlstm else "post_attn",
    )(*ins)


ITEM_ROWS = 8
RUN_ALIGN = 16
RUN_SIZES = (512, 256, 128, 64, 32, 16)
RUN_SRC, RUN_DST, RUN_LEN, RUN_TOTAL = 0, N_GROUPS, 2 * N_GROUPS, 3 * N_GROUPS


def _stage_rows(tm):
    assert N_GROUPS * (RUN_ALIGN - 1) <= LANES
    return tm + LANES


def _sorted_rows(n, tm):
    rows = n + N_GROUPS * (n // tm) * RUN_ALIGN + tm
    return -(-rows // tm) * tm


def _sort_meta_kernel(tm, nts, gid_ref, q_ref, runs_ref, items_ref):
    ntiles = gid_ref.shape[0]
    gid = gid_ref[...]
    r = lax.broadcasted_iota(jnp.int32, (tm, tm), 0)
    c = lax.broadcasted_iota(jnp.int32, (tm, tm), 1)
    before = jnp.where(r < c, 1.0, 0.0).astype(BF16)
    trow = lax.broadcasted_iota(jnp.int32, (ntiles, LANES), 0)
    lane = lax.broadcasted_iota(jnp.int32, (ntiles, LANES), 1)
    k = lax.broadcasted_iota(jnp.int32, (1, LANES), 1).astype(F32)
    zero11 = jnp.zeros((1, 1), F32)
    q = jnp.zeros((ntiles, tm), F32)
    runs = jnp.zeros((ntiles, LANES), F32)
    src = jnp.zeros((ntiles, 1), F32)
    start, nitems = zero11, zero11
    grp = jnp.zeros((1, LANES), F32)
    tile = jnp.zeros((1, LANES), F32)
    valid = jnp.zeros((1, LANES), F32)
    for g in range(N_GROUPS):
        mask = jnp.where(gid == float(g), 1.0, 0.0)
        cnt = jnp.sum(mask, axis=1, keepdims=True)
        padded = jnp.floor((cnt + (RUN_ALIGN - 1.0)) * (1.0 / RUN_ALIGN)) * RUN_ALIGN
        incl = jnp.broadcast_to(padded, (ntiles, LANES))
        d = 1
        while d < ntiles:
            incl = incl + jnp.where(trow >= d, pltpu.roll(incl, d, axis=0), 0.0)
            d *= 2
        total = incl[ntiles - 1:ntiles, 0:1]
        dst = start + incl[:, 0:1] - padded
        q = q + mask * (src + _dot(mask.astype(BF16), before))
        runs = runs + jnp.where(lane == RUN_SRC + g, src, 0.0) + jnp.where(lane == RUN_DST + g, dst, 0.0) \
            + jnp.where(lane == RUN_LEN + g, padded, 0.0)
        src = src + padded
        end = start + total
        ft = jnp.floor(start * (1.0 / tm))
        lt = jnp.floor((end - 1.0) * (1.0 / tm))
        ni = jnp.where(total > 0.0, lt - ft + 1.0, 0.0)
        inside = (k >= nitems) & (k < nitems + ni)
        grp = grp + jnp.where(inside, float(g), 0.0)
        tile = tile + jnp.where(inside, ft + (k - nitems), 0.0)
        valid = valid + jnp.where(inside, 1.0, 0.0)
        nitems = nitems + ni
        start = end
    runs = runs + jnp.where(lane == RUN_TOTAL, start, 0.0)
    live = valid > 0.0
    prev = pltpu.roll(tile, 1, axis=1)
    nxt = pltpu.roll(tile, LANES - 1, axis=1)
    first = live & ((k == 0.0) | (tile != prev))
    last = live & ((k == nitems - 1.0) | (tile != nxt))
    spare = jnp.floor((start - 1.0) * (1.0 / tm)) + 1.0 + (k - nitems)
    fill = jnp.logical_not(live) & (spare <= nts - 1.0)
    tile = jnp.where(live, tile, jnp.minimum(spare, nts - 1.0))
    grp = jnp.where(live, grp, jnp.max(grp, axis=1, keepdims=True))
    flag = lambda m: jnp.where(m, 1.0, 0.0)
    q_ref[...] = q
    runs_ref[...] = runs.astype(jnp.int32)
    table = jnp.concatenate([tile, grp, valid, flag(first | fill), flag(last | fill),
                             jnp.zeros((ITEM_ROWS - 5, LANES), F32)], 0)
    items_ref[...] = table.astype(jnp.int32)


def _max_items(nts):
    return nts + N_GROUPS - 1


def _sort_meta_call(gid, tm):
    ntiles = gid.shape[0]
    nts = _sorted_rows(ntiles * tm, tm) // tm
    assert _max_items(nts) <= LANES and tm <= RUN_SIZES[0]
    return pl.pallas_call(
        functools.partial(_sort_meta_kernel, tm, nts),
        out_shape=(jax.ShapeDtypeStruct((ntiles, tm), F32),
                   jax.ShapeDtypeStruct((ntiles, LANES), jnp.int32),
                   jax.ShapeDtypeStruct((ITEM_ROWS, LANES), jnp.int32)),
        compiler_params=_params(0),
        name="moe_sort_meta",
    )(gid)


def _run_copies(runs_ref, i, tile_ref, sorted_hbm, sem, to_sorted):
    pieces = []
    for g in range(N_GROUPS):
        src = runs_ref[i, RUN_SRC + g]
        dst = runs_ref[i, RUN_DST + g]
        length = runs_ref[i, RUN_LEN + g]
        for s in RUN_SIZES:
            def build(src=src, dst=dst, length=length, s=s):
                off = length & (-2 * s)
                a = tile_ref.at[pl.ds(pl.multiple_of(src + off, RUN_ALIGN), s), :]
                b = sorted_hbm.at[pl.ds(pl.multiple_of(dst + off, RUN_ALIGN), s), :]
                return pltpu.make_async_copy(a, b, sem) if to_sorted else pltpu.make_async_copy(b, a, sem)
            pieces.append(((length & s) != 0, build))
    return pieces


def _start(pieces):
    for pred, build in pieces:
        pl.when(pred)(lambda build=build: build().start())


def _wait(pieces):
    for pred, build in pieces:
        pl.when(pred)(lambda build=build: build().wait())


def _start_then_wait(pieces):
    _start(pieces)
    _wait(pieces)


def _scatter_kernel(tm, nts, ntiles, runs_ref, q_ref, u_ref, g_ref, su_hbm, sg_hbm,
                    stu, stg, sem_u, sem_g):
    i = pl.program_id(0)
    slot = i % 2

    def copies(tile, slot):
        return (_run_copies(runs_ref, tile, stu.at[slot], su_hbm, sem_u.at[slot], True)
                + _run_copies(runs_ref, tile, stg.at[slot], sg_hbm, sem_g.at[slot], True))

    r = lax.broadcasted_iota(jnp.int32, (_stage_rows(tm), tm), 0).astype(F32)
    perm = jnp.where(q_ref[...] == r, 1.0, 0.0).astype(BF16)
    stu[slot] = _dot(perm, u_ref[...]).astype(BF16)
    both = _dot(perm, jnp.concatenate(_split2(g_ref[...]), 1))
    stg[slot] = both[:, 0:LANES] + both[:, LANES:2 * LANES]
    _start(copies(i, slot))

    @pl.when(i > 0)
    def _():
        _wait(copies(i - 1, 1 - slot))

    @pl.when(i == ntiles - 1)
    def _():
        _wait(copies(i, slot))
        stu[slot] = jnp.zeros(stu.shape[1:], stu.dtype)
        stg[slot] = jnp.zeros(stg.shape[1:], stg.dtype)
        total = runs_ref[i, RUN_TOTAL]
        tail = nts * tm - total
        nfull = tail // tm
        pieces = []
        for tile_ref, hbm, sem in ((stu.at[slot], su_hbm, sem_u.at[slot]),
                                   (stg.at[slot], sg_hbm, sem_g.at[slot])):
            for j in range(nts - ntiles):
                def full(j=j, tile_ref=tile_ref, hbm=hbm, sem=sem):
                    dst = pl.multiple_of(total + j * tm, RUN_ALIGN)
                    return pltpu.make_async_copy(tile_ref.at[pl.ds(0, tm), :], hbm.at[pl.ds(dst, tm), :], sem)
                pieces.append((j < nfull, full))
            rem = tail - nfull * tm
            for s in RUN_SIZES:
                if s >= tm:
                    continue
                def part(s=s, tile_ref=tile_ref, hbm=hbm, sem=sem):
                    dst = pl.multiple_of(total + nfull * tm + (rem & (-2 * s)), RUN_ALIGN)
                    return pltpu.make_async_copy(tile_ref.at[pl.ds(0, s), :], hbm.at[pl.ds(dst, s), :], sem)
                pieces.append(((rem & s) != 0, part))
        _start_then_wait(pieces)


def _scatter_call(runs, q3, u2, gate, tm):
    n = u2.shape[0]
    ns = _sorted_rows(n, tm)
    grid_spec = pltpu.PrefetchScalarGridSpec(
        num_scalar_prefetch=1,
        grid=(n // tm,),
        in_specs=[pl.BlockSpec((None, 1, tm), lambda i, *_: (i, 0, 0)),
                  pl.BlockSpec((tm, D_MODEL), lambda i, *_: (i, 0)),
                  pl.BlockSpec((tm, LANES), lambda i, *_: (i, 0))],
        out_specs=(pl.BlockSpec(memory_space=pl.ANY), pl.BlockSpec(memory_space=pl.ANY)),
        scratch_shapes=[pltpu.VMEM((2, _stage_rows(tm), D_MODEL), BF16),
                        pltpu.VMEM((2, _stage_rows(tm), LANES), F32),
                        pltpu.SemaphoreType.DMA((2,)), pltpu.SemaphoreType.DMA((2,))])
    return pl.pallas_call(
        functools.partial(_scatter_kernel, tm, ns // tm, n // tm),
        out_shape=(jax.ShapeDtypeStruct((ns, D_MODEL), BF16), jax.ShapeDtypeStruct((ns, LANES), F32)),
        grid_spec=grid_spec,
        compiler_params=_params(1),
        name="moe_scatter",
    )(runs, q3, u2, gate)


def _ungroup(tm, runs_ref, col_ref, ys_hbm, stage, sem):
    i = pl.program_id(0)
    slot = i % 2

    def copies(tile, slot):
        return _run_copies(runs_ref, tile, stage.at[slot], ys_hbm, sem.at[slot], False)

    @pl.when(i == 0)
    def _():
        stage[...] = jnp.zeros_like(stage)
        _start(copies(i, slot))

    @pl.when(i + 1 < pl.num_programs(0))
    def _():
        _start(copies(i + 1, 1 - slot))

    _wait(copies(i, slot))
    col = col_ref[...]
    gidx = col[:, 0:1]
    src = [runs_ref[i, RUN_SRC + g].astype(F32) for g in range(N_GROUPS)]
    first = src[N_GROUPS - 1]
    for g in range(N_GROUPS - 2, -1, -1):
        first = jnp.where(gidx == float(g), src[g], first)
    q = first + col[:, 1:2]
    lanes = lax.broadcasted_iota(jnp.int32, (tm, _stage_rows(tm)), 1).astype(F32)
    perm_t = jnp.where(q == lanes, 1.0, 0.0).astype(BF16)
    return _dot(perm_t, stage[slot])


def _moe_routed_kernel(items_ref, su_ref, sg_ref, wgu_ref, wd_ref, out_ref, acc_ref, wgu_b, wd_b):
    kk = pl.program_id(0)
    grp = items_ref[1, kk]

    @pl.when((kk == 0) | (grp != items_ref[1, jnp.maximum(kk - 1, 0)]))
    def _():
        for e in range(EXPERTS_PER_GROUP):
            wgu_b[e] = wgu_ref[e].astype(BF16)
            wd_b[e] = wd_ref[e].astype(BF16)

    @pl.when(items_ref[3, kk] == 1)
    def _():
        acc_ref[...] = jnp.zeros_like(acc_ref)

    @pl.when(items_ref[2, kk] == 1)
    def _():
        u = su_ref[...]
        gate = sg_ref[...]
        lane = lax.broadcasted_iota(jnp.int32, gate.shape, 1)
        first_lane = N_GROUPS + EXPERTS_PER_GROUP * grp
        acts = []
        for e in range(EXPERTS_PER_GROUP):
            gcol = jnp.sum(jnp.where(lane == first_lane + e, gate, 0.0), axis=-1, keepdims=True)
            hu = _dot(u, wgu_b[e])
            hg = hu[:, 0:D_EXPERT]
            acts.append((hg * jax.nn.sigmoid(hg) * hu[:, D_EXPERT:2 * D_EXPERT] * gcol).astype(BF16))
        wd_all = wd_b[...].reshape(EXPERTS_PER_GROUP * D_EXPERT, D_MODEL)
        acc_ref[...] += _dot(jnp.concatenate(acts, 1), wd_all)

    @pl.when(items_ref[4, kk] == 1)
    def _():
        out_ref[...] = acc_ref[...].astype(out_ref.dtype)


def _moe_routed_call(items, su, sg, wgu, wd, layer, tm):
    ns = su.shape[0]
    n_items = _max_items(ns // tm)
    depth = wgu.shape[0]
    wgu5 = wgu.reshape(depth, N_GROUPS, EXPERTS_PER_GROUP, D_MODEL, 2 * D_EXPERT)
    wd5 = wd.reshape(depth, N_GROUPS, EXPERTS_PER_GROUP, D_EXPERT, D_MODEL)
    grid_spec = pltpu.PrefetchScalarGridSpec(
        num_scalar_prefetch=1,
        grid=(n_items,),
        in_specs=[pl.BlockSpec((tm, D_MODEL), lambda k, it: (it[0, k], 0)),
                  pl.BlockSpec((tm, LANES), lambda k, it: (it[0, k], 0)),
                  pl.BlockSpec((None, None, EXPERTS_PER_GROUP, D_MODEL, 2 * D_EXPERT),
                               lambda k, it: (layer, it[1, k], 0, 0, 0)),
                  pl.BlockSpec((None, None, EXPERTS_PER_GROUP, D_EXPERT, D_MODEL),
                               lambda k, it: (layer, it[1, k], 0, 0, 0))],
        out_specs=pl.BlockSpec((tm, D_MODEL), lambda k, it: (it[0, k], 0)),
        scratch_shapes=[pltpu.VMEM((tm, D_MODEL), F32),
                        pltpu.VMEM((EXPERTS_PER_GROUP, D_MODEL, 2 * D_EXPERT), BF16),
                        pltpu.VMEM((EXPERTS_PER_GROUP, D_EXPERT, D_MODEL), BF16)])
    return pl.pallas_call(
        _moe_routed_kernel,
        out_shape=jax.ShapeDtypeStruct((ns, D_MODEL), BF16),
        grid_spec=grid_spec,
        compiler_params=_params(1),
        name="moe_routed",
    )(items, su, sg, wgu5, wd5)


def _resid_gather_kernel(tm, runs_ref, x_ref, g2_ref, col_ref, ys_hbm, y_ref, stage, sem):
    y_ref[...] = x_ref[...] + g2_ref[...] * _ungroup(tm, runs_ref, col_ref, ys_hbm, stage, sem)


def _resid_gather_call(x, runs, col, ys, mod, mspec, tm):
    n = x.shape[0]
    row = lambda i, *_: (i, 0)
    grid_spec = pltpu.PrefetchScalarGridSpec(
        num_scalar_prefetch=1,
        grid=(n // tm,),
        in_specs=[pl.BlockSpec((tm, D_MODEL), row), mspec(5), pl.BlockSpec((tm, LANES), row),
                  pl.BlockSpec(memory_space=pl.ANY)],
        out_specs=pl.BlockSpec((tm, D_MODEL), row),
        scratch_shapes=[pltpu.VMEM((2, _stage_rows(tm), D_MODEL), BF16), pltpu.SemaphoreType.DMA((2,))])
    return pl.pallas_call(
        functools.partial(_resid_gather_kernel, tm),
        out_shape=jax.ShapeDtypeStruct((n, D_MODEL), F32),
        grid_spec=grid_spec,
        compiler_params=_params(1),
        name="moe_unsort_resid",
    )(runs, x, mod, col, ys)


def _moe_kernel(expand, u_ref, gate_ref, wgu_ref, wd_ref, x1_ref, g2_ref, *rest):
    rep_ref = rest[0] if expand else None
    y_ref, acc_ref = rest[1 if expand else 0:]
    e = pl.program_id(1)

    @pl.when(e == 0)
    def _():
        acc_ref[...] = jnp.zeros_like(acc_ref)

    hu = _dot(u_ref[...], wgu_ref[...].astype(BF16))
    hg = hu[:, 0:D_EXPERT]
    hv = hu[:, D_EXPERT:2 * D_EXPERT]
    gate = gate_ref[...]
    lane = lax.broadcasted_iota(jnp.int32, gate.shape, 1)
    gcol = jnp.sum(jnp.where(lane == N_GROUPS + e, gate, 0.0), axis=-1, keepdims=True)
    act = hg * jax.nn.sigmoid(hg) * hv * gcol
    acc_ref[...] += _dot(act.astype(BF16), wd_ref[...].astype(BF16))

    @pl.when(e == N_EXPERTS - 1)
    def _():
        y_ref[...] = x1_ref[...] + _mod_rows(g2_ref, rep_ref) * acc_ref[...]


def _moe_call(u2, gate, wgu, wd, layer, x1, mod, mspec, rep, tm):
    n = u2.shape[0]
    row = lambda i, e: (i, 0)
    expand = rep is not None
    return pl.pallas_call(
        functools.partial(_moe_kernel, expand),
        out_shape=jax.ShapeDtypeStruct((n, D_MODEL), F32),
        grid=(n // tm, N_EXPERTS),
        in_specs=[pl.BlockSpec((tm, D_MODEL), row), pl.BlockSpec((tm, LANES), row),
                  pl.BlockSpec((None, None, D_MODEL, 2 * D_EXPERT), lambda i, e: (layer, e, 0, 0)),
                  pl.BlockSpec((None, None, D_EXPERT, D_MODEL), lambda i, e: (layer, e, 0, 0)),
                  pl.BlockSpec((tm, D_MODEL), row), mspec(5)]
        + ([pl.BlockSpec(rep.shape, lambda i, e: (0, 0))] if expand else []),
        out_specs=pl.BlockSpec((tm, D_MODEL), row),
        scratch_shapes=[pltpu.VMEM((tm, D_MODEL), F32)],
        compiler_params=_params(2),
        name="moe_dense",
    )(u2, gate, wgu, wd, x1, mod, *([rep] if expand else []))


def _rope128(x, cos, sa, sb):
    return x * cos + pltpu.roll(x, LANES - ROT_DIM // 2, axis=1) * sa \
        + pltpu.roll(x, ROT_DIM // 2, axis=1) * sb


def _kvq_kernel(gather_tm, *refs):
    if gather_tm:
        (runs_ref, x_ref, g2_ref, col_ref, ys_hbm, kvsh_ref, kvsc_ref, sh1_ref, sc1_ref, kvn_ref,
         nm_ref, wkv_ref, wq_ref, kn_ref, qn_ref, cos_ref, sa_ref, sb_ref, g64_ref, gr_ref, gb_ref,
         q_ref, k_ref, v_ref, xa_ref, stage, sem) = refs
        x = x_ref[...] + g2_ref[...] * _ungroup(gather_tm, runs_ref, col_ref, ys_hbm, stage, sem)
        xa_ref[...] = x
        rep_ref = None
    else:
        (x_ref, kvsh_ref, kvsc_ref, sh1_ref, sc1_ref, kvn_ref, nm_ref,
         wkv_ref, wq_ref, kn_ref, qn_ref, cos_ref, sa_ref, sb_ref, g64_ref, gr_ref, gb_ref,
         rep_ref, q_ref, k_ref, v_ref) = refs
        x = x_ref[...]
    xn = _unit_rms(x)
    cos, sa, sb = cos_ref[...], sa_ref[...], sb_ref[...]

    ukv = xn * (kvn_ref[...] * (1.0 + _mod_rows(kvsc_ref, rep_ref))) + _mod_rows(kvsh_ref, rep_ref)
    kv = _dot(ukv.astype(BF16), wkv_ref[...])
    k = kv[:, 0:LANES]
    v_ref[...] = kv[:, LANES:2 * LANES]
    kh, kl = _split2(k * k)
    ms = _dot(kh, g64_ref[...]) + _dot(kl, g64_ref[...])
    k_ref[...] = _rope128(k * lax.rsqrt(ms + RMS_EPS) * kn_ref[...], cos, sa, sb)

    u1 = xn * (nm_ref[...] * (1.0 + _mod_rows(sc1_ref, rep_ref))) + _mod_rows(sh1_ref, rep_ref)
    q = _dot(u1.astype(BF16), wq_ref[...])
    ms16 = _dot((q * q).astype(BF16), gr_ref[...])
    rh, rl = _split2(lax.rsqrt(ms16 + RMS_EPS))
    rsb = _dot(jnp.concatenate([rh, rl], 1), gb_ref[...])
    qn = q * rsb * qn_ref[...]
    scale = HEAD_DIM ** -0.5
    cos_q, sa_q, sb_q = cos * scale, sa * scale, sb * scale
    for i in range(D_MODEL // LANES):
        sl = slice(LANES * i, LANES * (i + 1))
        q_ref[:, sl] = _rope128(qn[:, sl], cos_q, sa_q, sb_q).astype(BF16)


def _kvq_call(x, routed, mod0, mod1, mspec, kvmod, kvspec, rep, kvn, nm, wkv, wq, kn, qn, tabs, tab,
              g64, gr, gb, tm):
    n = x.shape[0]
    row = lambda i, *_: (i, 0)
    const = lambda i, *_: (0, 0)
    vec = pl.BlockSpec((1, D_MODEL), const)
    tok = pl.BlockSpec((tm, D_MODEL), row)
    small = pl.BlockSpec((tm, LANES), row)
    ins = [x, kvmod, kvmod, mod1, mod1, kvn, nm, wkv, wq, kn, qn, *tabs, g64, gr, gb]
    specs = [tok, kvspec(0), kvspec(1), mspec(0), mspec(1), vec, vec,
             pl.BlockSpec((D_MODEL, 2 * LANES), const), pl.BlockSpec((D_MODEL, D_MODEL), const),
             pl.BlockSpec((1, LANES), const), vec, tab, tab, tab,
             pl.BlockSpec((LANES, LANES), const), pl.BlockSpec((D_MODEL, LANES), const),
             pl.BlockSpec((2 * LANES, D_MODEL), const)]
    out_shape = [jax.ShapeDtypeStruct((n, D_MODEL), BF16), jax.ShapeDtypeStruct((n, LANES), F32),
                 jax.ShapeDtypeStruct((n, LANES), F32)]
    out_specs = [tok, small, small]
    scratch = []
    nprefetch = 0
    if routed is not None:
        runs, col, ys = routed
        nprefetch = 1
        ins = [runs, x, mod0, col, ys] + ins[1:]
        specs = [tok, mspec(5), small, pl.BlockSpec(memory_space=pl.ANY)] + specs[1:]
        out_shape.append(jax.ShapeDtypeStruct((n, D_MODEL), F32))
        out_specs.append(tok)
        scratch = [pltpu.VMEM((2, _stage_rows(tm), D_MODEL), BF16), pltpu.SemaphoreType.DMA((2,))]
    else:
        ins.append(rep)
        specs.append(pl.BlockSpec(rep.shape, const))
    grid_spec = pltpu.PrefetchScalarGridSpec(
        num_scalar_prefetch=nprefetch, grid=(n // tm,), in_specs=specs,
        out_specs=tuple(out_specs), scratch_shapes=scratch)
    return pl.pallas_call(
        functools.partial(_kvq_kernel, tm if routed is not None else 0),
        out_shape=tuple(out_shape),
        grid_spec=grid_spec,
        compiler_params=_params(1),
        name="kv_q_proj",
    )(*ins)


def _attn_core(q, kcat, vcat, bias, sinks_ref, o_ref):
    tk = kcat.shape[0]
    pairs = N_Q_HEADS // N_KV_HEADS // 2
    lane = lax.broadcasted_iota(jnp.int32, (1, LANES), 1)
    lo = lane < HEAD_DIM
    kro = pltpu.roll(kcat, HEAD_DIM, axis=1)
    vro = pltpu.roll(vcat, HEAD_DIM, axis=1)
    one_e = jnp.broadcast_to(jnp.where(lo, 1.0, 0.0), (tk, LANES))
    one_o = 1.0 - one_e
    for g in range(N_KV_HEADS):
        if g == 0:
            ke, ko = jnp.where(lo, kcat, 0.0), jnp.where(lo, 0.0, kro)
            ve, vo = jnp.where(lo, vcat, 0.0), jnp.where(lo, 0.0, vro)
        else:
            ke, ko = jnp.where(lo, kro, 0.0), jnp.where(lo, 0.0, kcat)
            ve, vo = jnp.where(lo, vro, 0.0), jnp.where(lo, 0.0, vcat)
        k2 = jnp.concatenate([ke, ko], 0).astype(BF16)
        v2 = jnp.concatenate([jnp.concatenate([ve, one_e], 1),
                              jnp.concatenate([vo, one_o], 1)], 0).astype(BF16)
        for p in range(pairs):
            hp = g * pairs + p
            s = _dot_nt(q[:, LANES * hp:LANES * (hp + 1)], k2)
            halves, corr = [], []
            for par in range(2):
                sp = s[:, par * tk:(par + 1) * tk] + bias
                sink = sinks_ref[2 * hp + par]
                m = jnp.maximum(jnp.max(sp, axis=-1, keepdims=True), sink)
                halves.append(jnp.exp(sp - m).astype(BF16))
                corr.append(jnp.exp(sink - m))
            o2 = _dot(jnp.concatenate(halves, 1), v2)
            den = o2[:, LANES:2 * LANES] + jnp.where(lo, corr[0], corr[1])
            o_ref[:, LANES * hp:LANES * (hp + 1)] = (o2[:, 0:LANES] / den).astype(o_ref.dtype)


ATTN_BLOCKS_PER_STEP = 4


def _attn_prompt_kernel(sinks_ref, q_ref, kp_ref, kc_ref, vp_ref, vc_ref, bias_ref, o_ref):
    kall = jnp.concatenate([kp_ref[...], kc_ref[...]], 0)
    vall = jnp.concatenate([vp_ref[...], vc_ref[...]], 0)
    first = jnp.minimum(pl.program_id(1), 1)
    for j in range(ATTN_BLOCKS_PER_STEP):
        rows = slice(j * WINDOW, (j + 1) * WINDOW)
        keys = slice(j * WINDOW, (j + 2) * WINDOW)
        bias = bias_ref[first] if j == 0 else bias_ref[1]
        _attn_core(q_ref[rows, :], kall[keys, :], vall[keys, :], bias, sinks_ref, o_ref.at[rows, :])


ATTN_SEQS_PER_STEP = 8


def _attn_sample_kernel(sinks_ref, q_ref, kc_ref, kn_ref, vc_ref, vn_ref, bias_ref, o_ref, kbuf, vbuf):
    @pl.when(pl.program_id(0) == 0)
    def _():
        kbuf[...] = jnp.zeros_like(kbuf)
        vbuf[...] = jnp.zeros_like(vbuf)

    for b in range(ATTN_SEQS_PER_STEP):
        kbuf[b, 0:WINDOW, :] = kc_ref[b]
        kbuf[b, WINDOW:WINDOW + SAMPLE_PAD, :] = kn_ref[b]
        vbuf[b, 0:WINDOW, :] = vc_ref[b]
        vbuf[b, WINDOW:WINDOW + SAMPLE_PAD, :] = vn_ref[b]
    for b in range(ATTN_SEQS_PER_STEP):
        _attn_core(q_ref[b], kbuf[b], vbuf[b], bias_ref[...], sinks_ref, o_ref.at[b])


def _window_bias(tq, first):
    qi = jnp.arange(tq)[:, None]
    kj = jnp.arange(2 * WINDOW)[None, :]
    ok = (kj > qi) & (kj <= qi + WINDOW)
    if first:
        ok = ok & (kj >= WINDOW)
    return jnp.where(ok, 0.0, -jnp.inf).astype(F32)


def _attn_prompt_call(sinks, q, k, v, B, nb):
    n = q.shape[0]
    per = ATTN_BLOCKS_PER_STEP
    steps = nb // per
    cur = lambda b, i: (b * steps + i, 0)
    prev = lambda b, i: (b * nb + jnp.maximum(per * i - 1, 0), 0)
    bias = jnp.stack([_window_bias(WINDOW, True), _window_bias(WINDOW, False)])
    kv_prev = pl.BlockSpec((WINDOW, LANES), prev)
    kv_cur = pl.BlockSpec((per * WINDOW, LANES), cur)
    return pl.pallas_call(
        _attn_prompt_kernel,
        out_shape=jax.ShapeDtypeStruct((n, D_MODEL), BF16),
        grid=(B, steps),
        in_specs=[pl.BlockSpec(memory_space=pltpu.SMEM),
                  pl.BlockSpec((per * WINDOW, D_MODEL), cur), kv_prev, kv_cur, kv_prev, kv_cur,
                  pl.BlockSpec((2, WINDOW, 2 * WINDOW), lambda b, i: (0, 0, 0))],
        out_specs=pl.BlockSpec((per * WINDOW, D_MODEL), cur),
        compiler_params=_params(2),
        name="swa_prompt",
    )(sinks, q, k, k, v, v, bias)


def _attn_sample_call(sinks, q, kcache, knew, vcache, vnew):
    B = q.shape[0]
    nseq = ATTN_SEQS_PER_STEP
    b3 = lambda b: (b, 0, 0)
    cache = pl.BlockSpec((nseq, WINDOW, LANES), b3)
    new = pl.BlockSpec((nseq, SAMPLE_PAD, LANES), b3)
    return pl.pallas_call(
        _attn_sample_kernel,
        out_shape=jax.ShapeDtypeStruct(q.shape, BF16),
        grid=(B // nseq,),
        in_specs=[pl.BlockSpec(memory_space=pltpu.SMEM),
                  pl.BlockSpec((nseq, SAMPLE_PAD, D_MODEL), b3), cache, new, cache, new,
                  pl.BlockSpec((SAMPLE_PAD, 2 * WINDOW), lambda b: (0, 0))],
        out_specs=pl.BlockSpec((nseq, SAMPLE_PAD, D_MODEL), b3),
        scratch_shapes=[pltpu.VMEM((nseq, 2 * WINDOW, LANES), F32),
                        pltpu.VMEM((nseq, 2 * WINDOW, LANES), F32)],
        compiler_params=_params(1),
        name="swa_sample",
    )(sinks, q, kcache, knew, vcache, vnew, _window_bias(SAMPLE_PAD, False))


def _rope_tables(pos):
    half = ROT_DIM // 2
    inv = ROPE_THETA ** (-jnp.arange(half, dtype=F32) / half)
    ang = pos.astype(F32)[:, None] * inv[None]
    cos, sin = jnp.cos(ang), jnp.sin(ang)
    d = jnp.arange(LANES) % HEAD_DIM
    idx = d % half
    cos_t = jnp.where(d < ROT_DIM, cos[:, idx], 1.0)
    sa = jnp.where(d < half, -sin[:, idx], 0.0)
    sb = jnp.where((d >= half) & (d < ROT_DIM), sin[:, idx], 0.0)
    return cos_t, sa, sb


def _pad_lanes(a, value=0.0):
    return jnp.pad(a, ((0, 0), (0, LANES - a.shape[1])), constant_values=value)


def _prep_weights(ada_w, ada_b, norm_mix, norm_ffn, a_w_in, a_b_gates, a_head_norm, a_w_out,
                  kv_ada_w, kv_ada_b, kv_norm, w_k, w_v, k_norm, b_w_q, b_q_norm, b_sinks, b_w_o,
                  moe_w_group, moe_b_group, moe_w_expert, moe_b_expert, moe_w_gate_up, moe_w_down):
    w = {}
    g0 = QK_COLS + 2 * V_COLS
    w["w_in"] = a_w_in
    w["w_gates"] = jnp.concatenate([_pad_lanes(a_w_in[0, :, g0:g0 + M_HEADS]),
                                    _pad_lanes(a_w_in[0, :, g0 + M_HEADS:])], 1)
    w["bli"] = _pad_lanes(a_b_gates[0][None, :M_HEADS])
    w["blf"] = _pad_lanes(a_b_gates[0][None, M_HEADS:])
    w["head_norm"] = a_head_norm[0][None]
    w["w_out"] = a_w_out[0].astype(BF16)
    w["norm_mix"] = [norm_mix[l][None] for l in range(2)]
    w["norm_ffn"] = [norm_ffn[l][None] for l in range(2)]
    w["router"] = []
    for l in range(2):
        wr = _pad_lanes(jnp.concatenate([moe_w_group[l], moe_w_expert[l]], 1))
        hi = wr.astype(BF16)
        lo = (wr - hi.astype(F32)).astype(BF16)
        br = _pad_lanes(jnp.concatenate([moe_b_group[l], moe_b_expert[l]])[None])
        w["router"].append((jnp.concatenate([hi, lo], 1), br))
    w["w_gu"] = moe_w_gate_up
    w["w_d"] = moe_w_down
    w["kv_norm"] = kv_norm[None]
    w["w_kv"] = jnp.concatenate([w_k, w_v], 1).astype(BF16)
    w["k_norm"] = jnp.tile(k_norm, N_KV_HEADS)[None]
    w["w_q"] = b_w_q[0].astype(BF16)
    w["q_norm"] = jnp.tile(b_q_norm[0], N_Q_HEADS)[None]
    w["sinks"] = b_sinks[0]
    w["w_o"] = b_w_o[0].astype(BF16)
    lanes = jnp.arange(LANES)
    feat = jnp.arange(D_MODEL)
    w["g64"] = jnp.where((lanes[:, None] // HEAD_DIM) == (lanes[None, :] // HEAD_DIM),
                         1.0 / HEAD_DIM, 0.0).astype(BF16)
    w["gr"] = jnp.where((feat[:, None] // HEAD_DIM) == lanes[None, :], 1.0 / HEAD_DIM, 0.0).astype(BF16)
    gb = jnp.where(lanes[:, None] == (feat[None, :] // HEAD_DIM), 1.0, 0.0).astype(BF16)
    w["gb"] = jnp.concatenate([gb, gb], 0)
    return w


def _trunk(x2, mods, kvmod, mspec, mspec_big, rep, w, *, B, T, L, tm, tm_big, tabs, tabspec, c0, n0, m0,
           cache_k=None, cache_v=None):
    sample = cache_k is not None
    nc = T // L if not sample else 1

    qk, v, o, li, lf = _inproj_call(x2, mods[0], mspec_big, rep, w["norm_mix"][0], w["w_in"],
                                    w["w_gates"], w["bli"], w["blf"], tm_big)
    if sample:
        def padtok(a, value=0.0):
            a = a.reshape(B, T, a.shape[-1])
            a = jnp.pad(a, ((0, 0), (0, L - T), (0, 0)), constant_values=value)
            return a.reshape(B * L, a.shape[-1])
        qk, v, li, lf = padtok(qk), padtok(v), padtok(li, M_EMPTY), padtok(lf)
    h, c_new, n_new, m_new = _mlstm_call(qk, v, li, lf, c0, n0, m0, B, nc, L)
    if sample:
        h = h.reshape(B, L, V_COLS)[:, :T].reshape(B * T, V_COLS)
    routed = not sample
    ntiles = (B * T) // tm

    def routed_moe(u2, gate, gid, col, layer):
        q, runs, items = _sort_meta_call(gid.reshape(ntiles, tm), tm)
        su, sg = _scatter_call(runs, q.reshape(ntiles, 1, tm), u2, gate, tm)
        ys = _moe_routed_call(items, su, sg, w["w_gu"], w["w_d"], layer, tm)
        return runs, col, ys

    post0 = _post_call(True, routed, h, o, x2, mods[0], mspec, rep, w["norm_ffn"][0], w["head_norm"],
                       w["w_out"], *w["router"][0], tm)
    kvq_args = (mods[0], mods[1], mspec, kvmod, mspec, rep, w["kv_norm"], w["norm_mix"][1],
                w["w_kv"], w["w_q"], w["k_norm"], w["q_norm"], tabs, tabspec,
                w["g64"], w["gr"], w["gb"], tm)

    if routed:
        x1, u2, gate, gid, col = post0
        q, k, vv, xa = _kvq_call(x1, routed_moe(u2, gate, gid, col, 0), *kvq_args)
    else:
        x1, u2, gate = post0
        xa = _moe_call(u2, gate, w["w_gu"], w["w_d"], 0, x1, mods[0], mspec_big, rep, tm_big)
        q, k, vv = _kvq_call(xa, None, *kvq_args)
    if not sample:
        att = _attn_prompt_call(w["sinks"], q, k, vv, B, T // WINDOW)
        k_win = k.reshape(B, T, LANES)[:, T - WINDOW:].reshape(B, WINDOW, N_KV_HEADS, HEAD_DIM)
        v_win = vv.reshape(B, T, LANES)[:, T - WINDOW:].reshape(B, WINDOW, N_KV_HEADS, HEAD_DIM)
    else:
        def padseq(a):
            return jnp.pad(a.reshape(B, T, a.shape[-1]), ((0, 0), (0, SAMPLE_PAD - T), (0, 0)))
        kc = cache_k.reshape(B, WINDOW, LANES)
        vc = cache_v.reshape(B, WINDOW, LANES)
        att = _attn_sample_call(w["sinks"], padseq(q), kc, padseq(k), vc, padseq(vv))
        att = att[:, :T].reshape(B * T, D_MODEL)
        k_win = jnp.concatenate([kc[:, T:], k.reshape(B, T, LANES)], 1)
        v_win = jnp.concatenate([vc[:, T:], vv.reshape(B, T, LANES)], 1)
        k_win = k_win.reshape(B, WINDOW, N_KV_HEADS, HEAD_DIM)
        v_win = v_win.reshape(B, WINDOW, N_KV_HEADS, HEAD_DIM)
    post1 = _post_call(False, routed, att, None, xa, mods[1], mspec, rep, w["norm_ffn"][1], None,
                       w["w_o"], *w["router"][1], tm)
    if routed:
        x3, u4, gate, gid, col = post1
        runs, col, ys = routed_moe(u4, gate, gid, col, 1)
        y = _resid_gather_call(x3, runs, col, ys, mods[1], mspec, tm)
    else:
        x3, u4, gate = post1
        y = _moe_call(u4, gate, w["w_gu"], w["w_d"], 1, x3, mods[1], mspec_big, rep, tm_big)
    c_out = c_new[None]
    n_out = n_new.reshape(1, B, M_HEADS, M_DK)
    m_out = m_new[:, 0, :M_HEADS][None]
    return y, c_out, n_out, m_out, k_win, v_win


def kernel(x_prompt, x_sample, c_prompt, c_sample, state_c, state_n, state_m, cache_k_win, cache_v_win, ada_w, ada_b, norm_mix, norm_ffn, a_w_in, a_b_gates, a_head_norm, a_w_out, kv_ada_w, kv_ada_b, kv_norm, w_k, w_v, k_norm, b_w_q, b_q_norm, b_sinks, b_w_o, moe_w_group, moe_b_group, moe_w_expert, moe_b_expert, moe_w_gate_up, moe_w_down):
    Bp, Tp, D = x_prompt.shape
    Bs, Ts, _ = x_sample.shape
    w = _prep_weights(ada_w, ada_b, norm_mix, norm_ffn, a_w_in, a_b_gates, a_head_norm, a_w_out,
                      kv_ada_w, kv_ada_b, kv_norm, w_k, w_v, k_norm, b_w_q, b_q_norm, b_sinks, b_w_o,
                      moe_w_group, moe_b_group, moe_w_expert, moe_b_expert, moe_w_gate_up, moe_w_down)

    rows = Bp + Bs
    rpad = -rows % 8
    c_all = jnp.concatenate([c_prompt, c_sample, jnp.zeros((rpad, D), F32)], 0)
    mod = _ada_call(c_all, ada_w, ada_b[:, None, :])
    kvm = _ada_call(c_all, kv_ada_w[None], kv_ada_b[None, None, :])

    tm_p = TOKEN_TILE
    tiles_per_seq = Tp // tm_p
    mods_p = [mod[l, :Bp][:, None, :] for l in range(2)]
    kvmod_p = kvm[0, :Bp][:, None, :]

    def mspec_p(col):
        return pl.BlockSpec((None, 1, D_MODEL), lambda i, *_: (i // tiles_per_seq, 0, col))

    big_tiles_per_seq = Tp // BIG_TOKEN_TILE

    def mspec_big_p(col):
        return pl.BlockSpec((None, 1, D_MODEL), lambda i, *_: (i // big_tiles_per_seq, 0, col))

    tabs_p = _rope_tables(jnp.arange(Tp, dtype=jnp.int32))
    tabspec_p = pl.BlockSpec((tm_p, LANES), lambda i, *_: (i % tiles_per_seq, 0))
    npairs = M_HEADS // 2
    c0 = jnp.zeros((Bp, M_HEADS, M_DK, M_DV), F32)
    n0 = jnp.zeros((Bp, npairs, LANES), F32)
    m0 = jnp.pad(jnp.full((Bp, 1, M_HEADS), M_EMPTY, F32), ((0, 0), (0, 0), (0, LANES - M_HEADS)))
    yp, cp, np_, mp, kwp, vwp = _trunk(
        x_prompt.reshape(Bp * Tp, D), mods_p, kvmod_p, mspec_p, mspec_big_p, None, w,
        B=Bp, T=Tp, L=M_CHUNK, tm=tm_p, tm_big=BIG_TOKEN_TILE, tabs=tabs_p, tabspec=tabspec_p,
        c0=c0, n0=n0, m0=m0)

    ns = Bs * Ts
    mods_s = [mod[l, Bp:Bp + Bs][None] for l in range(2)]
    kvmod_s = kvm[0, Bp:Bp + Bs][None]
    rep = (jnp.arange(ns)[:, None] // Ts == jnp.arange(Bs)[None, :]).astype(BF16)

    def mspec_s(col):
        return pl.BlockSpec((None, Bs, D_MODEL), lambda i, *_: (0, 0, col))

    tabs_s = _rope_tables(PAST_LEN + jnp.arange(Ts, dtype=jnp.int32))
    tabs_s = tuple(jnp.tile(t, (Bs, 1)) for t in tabs_s)
    m0s = jnp.pad(state_m[0][:, None, :], ((0, 0), (0, 0), (0, LANES - M_HEADS)))
    ys, cs, ns_, ms, kws, vws = _trunk(
        x_sample.reshape(ns, D), mods_s, kvmod_s, mspec_s, mspec_s, rep, w,
        B=Bs, T=Ts, L=SAMPLE_PAD, tm=ns, tm_big=ns, tabs=tabs_s,
        tabspec=pl.BlockSpec((ns, LANES), lambda i, *_: (0, 0)),
        c0=state_c[0], n0=state_n[0].reshape(Bs, npairs, LANES), m0=m0s,
        cache_k=cache_k_win, cache_v=cache_v_win)

    return (yp.reshape(Bp, Tp, D), ys.reshape(Bs, Ts, D), cp, np_, mp, kwp, vwp,
            cs, ns_, ms, kws, vws)
```

```python
import functools

import jax
import jax.numpy as jnp
from jax import lax
from jax.experimental import pallas as pl
from jax.experimental.pallas import tpu as pltpu

F32 = jnp.float32
BF16 = jnp.bfloat16

D_MODEL = 1024
PAST_LEN = 8192
M_HEADS = 8
M_DK = 64
M_DV = 128
M_CHUNK = 128
GATE_SOFTCAP = 15.0
M_EMPTY = -1e30
WINDOW = 128
HEAD_DIM = 64
N_Q_HEADS = 16
N_KV_HEADS = 2
ROPE_THETA = 500000.0
ROT_DIM = 16
N_GROUPS = 4
EXPERTS_PER_GROUP = 4
N_EXPERTS = 16
D_EXPERT = 256
RMS_EPS = 1e-6

LANES = 128
QK_COLS = 2 * M_HEADS * M_DK
V_COLS = M_HEADS * M_DV
SAMPLE_PAD = 16
TOKEN_TILE = 512
BIG_TOKEN_TILE = 1024
VMEM_LIMIT = 52 * 1024 * 1024

NT_DIMS = (((1,), (1,)), ((), ()))


def _params(n_axes):
    return pltpu.CompilerParams(dimension_semantics=("arbitrary",) * n_axes,
                                vmem_limit_bytes=VMEM_LIMIT)


def _dot(a, b):
    return jnp.dot(a, b, preferred_element_type=F32)


def _dot_nt(a, b):
    return lax.dot_general(a, b, NT_DIMS, preferred_element_type=F32)


def _split2(x):
    hi = x.astype(BF16)
    lo = (x - hi.astype(F32)).astype(BF16)
    return hi, lo


def _split3(x):
    hi = x.astype(BF16)
    r = x - hi.astype(F32)
    mid = r.astype(BF16)
    lo = (r - mid.astype(F32)).astype(BF16)
    return hi, mid, lo


def _unit_rms(x):
    return x * lax.rsqrt(jnp.mean(x * x, axis=-1, keepdims=True) + RMS_EPS)


def _ada_kernel(c_ref, w_ref, b_ref, o_ref):
    c = c_ref[...]
    cs = (c * jax.nn.sigmoid(c)).astype(BF16)
    o_ref[...] = _dot(cs, w_ref[...].astype(BF16)) + b_ref[...]


def _ada_call(c, w, b):
    g, d, n = w.shape
    r = c.shape[0]
    tn = 1024
    return pl.pallas_call(
        _ada_kernel,
        out_shape=jax.ShapeDtypeStruct((g, r, n), F32),
        grid=(g, n // tn),
        in_specs=[pl.BlockSpec((r, d), lambda i, j: (0, 0)),
                  pl.BlockSpec((None, d, tn), lambda i, j: (i, 0, j)),
                  pl.BlockSpec((None, 1, tn), lambda i, j: (i, 0, j))],
        out_specs=pl.BlockSpec((None, r, tn), lambda i, j: (i, 0, j)),
        compiler_params=_params(2),
        name="ada_mod",
    )(c, w, b)


def _mod_rows(ref, rep_ref):
    if rep_ref is None:
        return ref[...]
    hi, mid, lo = _split3(ref[...])
    rep = rep_ref[...]
    return _dot(rep, hi) + _dot(rep, mid) + _dot(rep, lo)


def _inproj_kernel(expand, x_ref, sh_ref, sc_ref, nw_ref, w_ref, wg_ref, bli_ref, blf_ref, *rest):
    rep_ref = rest[0] if expand else None
    qk_ref, v_ref, o_ref, li_ref, lf_ref, wb_ref = rest[1 if expand else 0:]

    @pl.when(pl.program_id(0) == 0)
    def _():
        wb_ref[...] = w_ref[...].astype(BF16)

    u = _unit_rms(x_ref[...]) * (nw_ref[...] * (1.0 + _mod_rows(sc_ref, rep_ref))) \
        + _mod_rows(sh_ref, rep_ref)
    ub = u.astype(BF16)
    half = QK_COLS // 2
    q = _dot(ub, wb_ref[:, 0:half]) * (M_DK ** -0.5)
    qk_ref[:, 0:half] = q.astype(BF16)
    qk_ref[:, half:QK_COLS] = _dot(ub, wb_ref[:, half:QK_COLS]).astype(BF16)
    v_ref[...] = _dot(ub, wb_ref[:, QK_COLS:QK_COLS + V_COLS]).astype(BF16)
    o_ref[...] = jax.nn.sigmoid(_dot(ub, wb_ref[:, QK_COLS + V_COLS:QK_COLS + 2 * V_COLS])).astype(BF16)
    lane = lax.broadcasted_iota(jnp.int32, (1, LANES), 1)
    live = lane < M_HEADS
    gates = _dot(ub, wg_ref[...].astype(BF16))
    gi = gates[:, 0:LANES] + bli_ref[...]
    gf = gates[:, LANES:2 * LANES] + blf_ref[...]
    li = GATE_SOFTCAP * jnp.tanh(gi / GATE_SOFTCAP)
    fpre = GATE_SOFTCAP * jnp.tanh(gf / GATE_SOFTCAP)
    lf = jnp.minimum(fpre, 0.0) - jnp.log1p(jnp.exp(-jnp.abs(fpre)))
    li_ref[...] = jnp.where(live, li, 0.0)
    lf_ref[...] = jnp.where(live, lf, 0.0)


def _inproj_call(x, mod, mspec, rep, nw, w_in, wg, bli, blf, tm):
    n = x.shape[0]
    row = lambda i: (i, 0)
    const = lambda i: (0, 0)
    main_cols = QK_COLS + 2 * V_COLS
    expand = rep is not None
    return pl.pallas_call(
        functools.partial(_inproj_kernel, expand),
        out_shape=(jax.ShapeDtypeStruct((n, QK_COLS), BF16),
                   jax.ShapeDtypeStruct((n, V_COLS), BF16),
                   jax.ShapeDtypeStruct((n, V_COLS), BF16),
                   jax.ShapeDtypeStruct((n, LANES), F32),
                   jax.ShapeDtypeStruct((n, LANES), F32)),
        grid=(n // tm,),
        in_specs=[pl.BlockSpec((tm, D_MODEL), row), mspec(0), mspec(1),
                  pl.BlockSpec((1, D_MODEL), const),
                  pl.BlockSpec((None, D_MODEL, main_cols), lambda i: (0, 0, 0),
                               pipeline_mode=pl.Buffered(1)),
                  pl.BlockSpec((D_MODEL, 2 * LANES), const),
                  pl.BlockSpec((1, LANES), const), pl.BlockSpec((1, LANES), const)]
        + ([pl.BlockSpec(rep.shape, const)] if expand else []),
        out_specs=(pl.BlockSpec((tm, QK_COLS), row), pl.BlockSpec((tm, V_COLS), row),
                   pl.BlockSpec((tm, V_COLS), row), pl.BlockSpec((tm, LANES), row),
                   pl.BlockSpec((tm, LANES), row)),
        scratch_shapes=[pltpu.VMEM((D_MODEL, main_cols), BF16)],
        compiler_params=_params(1),
        name="mlstm_inproj",
    )(x, mod, mod, nw, w_in, wg, bli, blf, *([rep] if expand else []))


def _mlstm_kernel(L, nseq, scale_q, *refs):
    c = pl.program_id(1)
    seqs = [[r.at[b] for r in refs] for b in range(nseq)]

    @pl.when(c == 0)
    def _():
        for s in seqs:
            _mlstm_load_state(*s[4:7], *s[11:14])

    for s in seqs:
        _mlstm_seq_step(L, scale_q, *s[0:4], s[7], *s[11:14])

    @pl.when(c == pl.num_programs(1) - 1)
    def _():
        for s in seqs:
            _mlstm_store_state(*s[8:14])


def _mlstm_load_state(c0_ref, n0_ref, m0_ref, c2_s, n_s, m_s):
    m_s[...] = m0_ref[...]
    n_s[...] = n0_ref[...]
    z = jnp.zeros((M_DK, M_DV), F32)
    for j in range(M_HEADS // 2):
        c2_s[j] = jnp.concatenate([jnp.concatenate([c0_ref[2 * j], z], 1),
                                   jnp.concatenate([z, c0_ref[2 * j + 1]], 1)], 0)


def _mlstm_seq_step(L, scale_q, qk_ref, v_ref, li_ref, lf_ref, h_ref, c2_s, n_s, m_s):
    npairs = M_HEADS // 2
    hd = M_DK

    lane = lax.broadcasted_iota(jnp.int32, (1, LANES), 1)
    lo128 = lane < hd
    lane256 = lax.broadcasted_iota(jnp.int32, (1, 2 * LANES), 1)
    lo256 = lane256 < LANES
    row128 = lax.broadcasted_iota(jnp.int32, (LANES, 1), 0)
    top = row128 < hd
    blockdiag = (top & lo256) | (jnp.logical_not(top) & jnp.logical_not(lo256))

    LI = li_ref[...]
    LF = lf_ref[...]
    rowL = lax.broadcasted_iota(jnp.int32, (L, LANES), 0)

    def prefix(x, op, ident):
        d = 1
        while d < L:
            shifted = pltpu.roll(x, d, axis=0)
            x = op(x, jnp.where(rowL >= d, shifted, ident))
            d *= 2
        return x

    Bc = prefix(LF, jnp.add, 0.0)
    Cm = LI - Bc
    mprev = m_s[...]
    Gc = jnp.maximum(mprev, prefix(Cm, jnp.maximum, -jnp.inf))
    A = jnp.exp(mprev - Gc)
    bL = Bc[L - 1:L, :]
    DL = bL + Cm
    mnew = jnp.maximum(bL + mprev, jnp.max(DL, axis=0, keepdims=True))
    ast = jnp.exp(bL + mprev - mnew)
    WST = jnp.exp(DL - mnew)

    def pad_rows(x, rows):
        if x.shape[0] == rows:
            return x
        return jnp.concatenate([x, jnp.zeros((rows - x.shape[0], x.shape[1]), x.dtype)], 0)

    kw_lanes = max(L, hd)
    assert kw_lanes in (hd, LANES) and L <= LANES
    wide = kw_lanes == LANES
    lane_s = lax.broadcasted_iota(jnp.int32, (1, 2 * kw_lanes), 1)
    row_s = lax.broadcasted_iota(jnp.int32, (2 * kw_lanes, 1), 0)
    cm_pad = pad_rows(Cm, kw_lanes)
    if wide:
        XT = cm_pad.T
    else:
        XT = jnp.concatenate([cm_pad, cm_pad], 0).T
    causal = (lane_s & (kw_lanes - 1)) <= lax.broadcasted_iota(jnp.int32, (L, 2 * kw_lanes), 0)
    top_s = row_s < kw_lanes
    J = jnp.where((top_s & lo256) | (jnp.logical_not(top_s) & jnp.logical_not(lo256)),
                  1.0, 0.0).astype(BF16)

    EXPM = jnp.exp(-(Bc + Gc))

    def bc(X, h):
        return jnp.broadcast_to(X[:, h:h + 1], X.shape)

    for j in range(npairs):
        he, ho = 2 * j, 2 * j + 1
        p128 = slice(LANES * j, LANES * (j + 1))
        p256 = slice(2 * LANES * j, 2 * LANES * (j + 1))

        def pair128(X):
            return jnp.where(lo128, bc(X, he), bc(X, ho))

        def pair256(X):
            return jnp.concatenate([bc(X, he), bc(X, ho)], 1)

        q128 = qk_ref[:, LANES * j:LANES * (j + 1)]
        k128 = qk_ref[:, QK_COLS // 2 + LANES * j:QK_COLS // 2 + LANES * (j + 1)]
        v256 = v_ref[:, 2 * LANES * j:2 * LANES * (j + 1)]
        zk = jnp.zeros_like(k128)
        zv = jnp.zeros_like(v256)
        K2t = jnp.concatenate([pad_rows(jnp.where(lo128, k128, zk), kw_lanes),
                               pad_rows(jnp.where(lo128, zk, k128), kw_lanes)], 0)
        V2 = jnp.concatenate([pad_rows(jnp.where(lo256, v256, zv), kw_lanes),
                              pad_rows(jnp.where(lo256, zv, v256), kw_lanes)], 0)
        S = _dot_nt(q128, K2t)
        if wide:
            crow = jnp.concatenate([XT[he:he + 1, :], XT[ho:ho + 1, :]], 1)
            arg = jnp.where(causal, crow - pair256(Gc), -jnp.inf)
        else:
            crow = jnp.where(lo128, XT[he:he + 1, :], XT[ho:ho + 1, :])
            arg = jnp.where(causal, crow - pair128(Gc), -jnp.inf)
        Sw = (S * jnp.exp(arg)).astype(BF16)
        num_intra = _dot(Sw, V2)
        rowsum = _dot(Sw, J)
        C2 = c2_s[j]
        npair = n_s[j:j + 1, :]
        Nrow2 = jnp.concatenate(
            [jnp.broadcast_to(jnp.where(lo128, npair, 0.0), (LANES, LANES)),
             jnp.broadcast_to(jnp.where(lo128, 0.0, npair), (LANES, LANES))], 0).astype(BF16)
        if scale_q:
            qa = (q128.astype(F32) * pair128(A)).astype(BF16)
            num = _dot(qa, C2.astype(BF16)) + num_intra
            den = _dot_nt(qa, Nrow2) + rowsum
        else:
            a256 = pair256(A)
            num = a256 * _dot(q128, C2.astype(BF16)) + num_intra
            den = a256 * _dot_nt(q128, Nrow2) + rowsum
        h = num / jnp.maximum(jnp.abs(den), pair256(EXPM))
        h_ref[:, p256] = h.astype(h_ref.dtype)

        kw = k128.astype(F32) * pair128(WST)
        n_s[j:j + 1, :] = pair128(ast) * npair + jnp.sum(kw, axis=0, keepdims=True)
        kwT = pad_rows(kw, LANES).T.astype(BF16)
        dC = _dot(kwT, pad_rows(v256, LANES))
        c2_s[j] = pair256(ast) * C2 + jnp.where(blockdiag, dC, 0.0)

    m_s[...] = mnew


def _mlstm_store_state(cout_ref, nout_ref, mout_ref, c2_s, n_s, m_s):
    for j in range(M_HEADS // 2):
        C2 = c2_s[j]
        cout_ref[2 * j] = C2[0:M_DK, 0:M_DV]
        cout_ref[2 * j + 1] = C2[M_DK:2 * M_DK, M_DV:2 * M_DV]
    nout_ref[...] = n_s[...]
    mout_ref[...] = m_s[...]


MLSTM_SEQS_PER_STEP = 4


def _mlstm_call(qk, v, li, lf, c0, n0, m0, B, nc, L):
    npairs = M_HEADS // 2
    nseq = min(B, MLSTM_SEQS_PER_STEP)
    T = nc * L
    tok3 = lambda a: a.reshape(B, T, a.shape[-1])
    tok = lambda b, c: (b, c, 0)
    st4 = lambda b, c: (b, 0, 0, 0)
    st3 = lambda b, c: (b, 0, 0)
    h, c_new, n_new, m_new = pl.pallas_call(
        functools.partial(_mlstm_kernel, L, nseq, nc == 1),
        out_shape=(jax.ShapeDtypeStruct((B, T, V_COLS), BF16),
                   jax.ShapeDtypeStruct((B, M_HEADS, M_DK, M_DV), F32),
                   jax.ShapeDtypeStruct((B, npairs, LANES), F32),
                   jax.ShapeDtypeStruct((B, 1, LANES), F32)),
        grid=(B // nseq, nc),
        in_specs=[pl.BlockSpec((nseq, L, QK_COLS), tok), pl.BlockSpec((nseq, L, V_COLS), tok),
                  pl.BlockSpec((nseq, L, LANES), tok), pl.BlockSpec((nseq, L, LANES), tok),
                  pl.BlockSpec((nseq, M_HEADS, M_DK, M_DV), st4),
                  pl.BlockSpec((nseq, npairs, LANES), st3),
                  pl.BlockSpec((nseq, 1, LANES), st3)],
        out_specs=(pl.BlockSpec((nseq, L, V_COLS), tok),
                   pl.BlockSpec((nseq, M_HEADS, M_DK, M_DV), st4),
                   pl.BlockSpec((nseq, npairs, LANES), st3),
                   pl.BlockSpec((nseq, 1, LANES), st3)),
        scratch_shapes=[pltpu.VMEM((nseq, npairs, 2 * M_DK, 2 * M_DV), F32),
                        pltpu.VMEM((nseq, npairs, LANES), F32),
                        pltpu.VMEM((nseq, 1, LANES), F32)],
        compiler_params=_params(2),
        name="mlstm_chunks",
    )(tok3(qk), tok3(v), tok3(li), tok3(lf), c0, n0, m0)
    return h.reshape(B * T, V_COLS), c_new, n_new, m_new


def _route(lg):
    lane = lax.broadcasted_iota(jnp.int32, lg.shape, 1)
    lanef = lane.astype(F32)
    neg = -jnp.inf
    far = float(LANES)
    gm = lane < N_GROUPS
    lgm = jnp.where(gm, lg, neg)
    gmax = jnp.max(lgm, axis=-1, keepdims=True)
    gsum = jnp.sum(jnp.exp(lgm - gmax), axis=-1, keepdims=True)
    g_w = 1.0 / gsum
    gidx = jnp.min(jnp.where(gm & (lg == gmax), lanef, far), axis=-1, keepdims=True)
    first = N_GROUPS + EXPERTS_PER_GROUP * gidx
    sel = (lanef >= first) & (lanef < first + EXPERTS_PER_GROUP)
    l1 = jnp.max(jnp.where(sel, lg, neg), axis=-1, keepdims=True)
    i1 = jnp.min(jnp.where(sel & (lg == l1), lanef, far), axis=-1, keepdims=True)
    sel2 = sel & (lanef != i1)
    l2 = jnp.max(jnp.where(sel2, lg, neg), axis=-1, keepdims=True)
    i2 = jnp.min(jnp.where(sel2 & (lg == l2), lanef, far), axis=-1, keepdims=True)
    r = jnp.exp(l2 - l1)
    w1 = g_w / (1.0 + r)
    w2 = w1 * r
    return jnp.where(lanef == i1, w1, jnp.where(lanef == i2, w2, 0.0)), gidx


def _post_kernel(mlstm, routed, expand, *refs):
    refs = list(refs)
    h_ref = refs.pop(0)
    o_ref = refs.pop(0) if mlstm else None
    x_ref, g1_ref, sh2_ref, sc2_ref, nf_ref = refs[:5]
    refs = refs[5:]
    hn_ref = refs.pop(0) if mlstm else None
    wout_ref, wrhl_ref, br_ref = refs[:3]
    refs = refs[3:]
    tri_ref = refs.pop(0) if routed else None
    rep_ref = refs.pop(0) if expand else None
    x1_ref, u2_ref, gate_ref = refs[:3]
    outs = refs[3:]
    if mlstm:
        hf = h_ref[...].astype(F32)
        parts = [_unit_rms(hf[:, M_DV * i:M_DV * (i + 1)]) for i in range(M_HEADS)]
        hn = jnp.concatenate(parts, 1) * hn_ref[...]
        hg = (hn * o_ref[...].astype(F32)).astype(BF16)
    else:
        hg = h_ref[...]
    x1 = x_ref[...] + _mod_rows(g1_ref, rep_ref) * _dot(hg, wout_ref[...])
    x1_ref[...] = x1
    u2 = _unit_rms(x1) * (nf_ref[...] * (1.0 + _mod_rows(sc2_ref, rep_ref))) + _mod_rows(sh2_ref, rep_ref)
    uh, ul = _split2(u2)
    wrh = wrhl_ref[:, 0:LANES]
    lg = _dot(uh, wrh) + _dot(ul, wrh) + _dot(uh, wrhl_ref[:, LANES:2 * LANES]) + br_ref[...]
    gate, gidx = _route(lg)
    u2_ref[...] = uh
    gate_ref[...] = gate
    if not routed:
        return
    gid_ref, col_ref = outs
    rows = []
    for blk in range(x1.shape[0] // LANES):
        col = jnp.broadcast_to(gidx[LANES * blk:LANES * (blk + 1), :], (LANES, LANES))
        rows.append(col.T[0:1, :])
    gid_ref[...] = jnp.concatenate(rows, 1)
    lanef = lax.broadcasted_iota(jnp.int32, gate.shape, 1).astype(F32)
    onehot = jnp.where(lanef == gidx, 1.0, 0.0)
    earlier = _dot(tri_ref[...], onehot.astype(BF16))
    rank = jnp.sum(onehot * earlier, axis=-1, keepdims=True)
    col_ref[...] = jnp.where(lanef == 0.0, gidx, jnp.where(lanef == 1.0, rank, 0.0))


def _post_call(mlstm, routed, h, o, x, mod, mspec, rep, nf, hn, wout, wrhl, br, tm):
    n = x.shape[0]
    expand = rep is not None
    row = lambda i: (i, 0)
    const = lambda i: (0, 0)
    tok = pl.BlockSpec((tm, D_MODEL), row)
    small = pl.BlockSpec((tm, LANES), row)
    vec = pl.BlockSpec((1, D_MODEL), const)
    ins = [h] + ([o] if mlstm else []) + [x, mod, mod, mod, nf] + ([hn] if mlstm else []) \
        + [wout, wrhl, br]
    specs = [tok] + ([tok] if mlstm else []) + [tok, mspec(2), mspec(3), mspec(4), vec] \
        + ([vec] if mlstm else []) \
        + [pl.BlockSpec((D_MODEL, D_MODEL), const), pl.BlockSpec((D_MODEL, 2 * LANES), const),
           pl.BlockSpec((1, LANES), const)]
    out_shape = [jax.ShapeDtypeStruct((n, D_MODEL), F32), jax.ShapeDtypeStruct((n, D_MODEL), BF16),
                 jax.ShapeDtypeStruct((n, LANES), F32)]
    out_specs = [tok, tok, small]
    if routed:
        t = jnp.arange(tm)
        ins.append((t[None, :] < t[:, None]).astype(BF16))
        specs.append(pl.BlockSpec((tm, tm), const))
        out_shape += [jax.ShapeDtypeStruct((n // tm, 1, tm), F32), jax.ShapeDtypeStruct((n, LANES), F32)]
        out_specs += [pl.BlockSpec((None, 1, tm), lambda i: (i, 0, 0)), small]
    if expand:
        ins.append(rep)
        specs.append(pl.BlockSpec(rep.shape, const))
    out_shape, out_specs = tuple(out_shape), tuple(out_specs)
    return pl.pallas_call(
        functools.partial(_post_kernel, mlstm, routed, expand),
        out_shape=out_shape,
        grid=(n // tm,),
        in_specs=specs,
        out_specs=out_specs,
        compiler_params=_params(1),
        name="post_mlstm" if mlstm else "post_attn",
    )(*ins)


ITEM_ROWS = 8
RUN_ALIGN = 16
RUN_SIZES = (512, 256, 128, 64, 32, 16)
RUN_SRC, RUN_DST, RUN_LEN, RUN_TOTAL = 0, N_GROUPS, 2 * N_GROUPS, 3 * N_GROUPS


def _stage_rows(tm):
    assert N_GROUPS * (RUN_ALIGN - 1) <= LANES
    return tm + LANES


def _sorted_rows(n, tm):
    rows = n + N_GROUPS * (n // tm) * RUN_ALIGN + tm
    return -(-rows // tm) * tm


def _sort_meta_kernel(tm, nts, gid_ref, q_ref, runs_ref, items_ref):
    ntiles = gid_ref.shape[0]
    gid = gid_ref[...]
    r = lax.broadcasted_iota(jnp.int32, (tm, tm), 0)
    c = lax.broadcasted_iota(jnp.int32, (tm, tm), 1)
    before = jnp.where(r < c, 1.0, 0.0).astype(BF16)
    trow = lax.broadcasted_iota(jnp.int32, (ntiles, LANES), 0)
    lane = lax.broadcasted_iota(jnp.int32, (ntiles, LANES), 1)
    k = lax.broadcasted_iota(jnp.int32, (1, LANES), 1).astype(F32)
    zero11 = jnp.zeros((1, 1), F32)
    q = jnp.zeros((ntiles, tm), F32)
    runs = jnp.zeros((ntiles, LANES), F32)
    src = jnp.zeros((ntiles, 1), F32)
    start, nitems = zero11, zero11
    grp = jnp.zeros((1, LANES), F32)
    tile = jnp.zeros((1, LANES), F32)
    valid = jnp.zeros((1, LANES), F32)
    for g in range(N_GROUPS):
        mask = jnp.where(gid == float(g), 1.0, 0.0)
        cnt = jnp.sum(mask, axis=1, keepdims=True)
        padded = jnp.floor((cnt + (RUN_ALIGN - 1.0)) * (1.0 / RUN_ALIGN)) * RUN_ALIGN
        incl = jnp.broadcast_to(padded, (ntiles, LANES))
        d = 1
        while d < ntiles:
            incl = incl + jnp.where(trow >= d, pltpu.roll(incl, d, axis=0), 0.0)
            d *= 2
        total = incl[ntiles - 1:ntiles, 0:1]
        dst = start + incl[:, 0:1] - padded
        q = q + mask * (src + _dot(mask.astype(BF16), before))
        runs = runs + jnp.where(lane == RUN_SRC + g, src, 0.0) + jnp.where(lane == RUN_DST + g, dst, 0.0) \
            + jnp.where(lane == RUN_LEN + g, padded, 0.0)
        src = src + padded
        end = start + total
        ft = jnp.floor(start * (1.0 / tm))
        lt = jnp.floor((end - 1.0) * (1.0 / tm))
        ni = jnp.where(total > 0.0, lt - ft + 1.0, 0.0)
        inside = (k >= nitems) & (k < nitems + ni)
        grp = grp + jnp.where(inside, float(g), 0.0)
        tile = tile + jnp.where(inside, ft + (k - nitems), 0.0)
        valid = valid + jnp.where(inside, 1.0, 0.0)
        nitems = nitems + ni
        start = end
    runs = runs + jnp.where(lane == RUN_TOTAL, start, 0.0)
    live = valid > 0.0
    prev = pltpu.roll(tile, 1, axis=1)
    nxt = pltpu.roll(tile, LANES - 1, axis=1)
    first = live & ((k == 0.0) | (tile != prev))
    last = live & ((k == nitems - 1.0) | (tile != nxt))
    spare = jnp.floor((start - 1.0) * (1.0 / tm)) + 1.0 + (k - nitems)
    fill = jnp.logical_not(live) & (spare <= nts - 1.0)
    tile = jnp.where(live, tile, jnp.minimum(spare, nts - 1.0))
    grp = jnp.where(live, grp, jnp.max(grp, axis=1, keepdims=True))
    flag = lambda m: jnp.where(m, 1.0, 0.0)
    q_ref[...] = q
    runs_ref[...] = runs.astype(jnp.int32)
    table = jnp.concatenate([tile, grp, valid, flag(first | fill), flag(last | fill),
                             jnp.zeros((ITEM_ROWS - 5, LANES), F32)], 0)
    items_ref[...] = table.astype(jnp.int32)


def _max_items(nts):
    return nts + N_GROUPS - 1


def _sort_meta_call(gid, tm):
    ntiles = gid.shape[0]
    nts = _sorted_rows(ntiles * tm, tm) // tm
    assert _max_items(nts) <= LANES and tm <= RUN_SIZES[0]
    return pl.pallas_call(
        functools.partial(_sort_meta_kernel, tm, nts),
        out_shape=(jax.ShapeDtypeStruct((ntiles, tm), F32),
                   jax.ShapeDtypeStruct((ntiles, LANES), jnp.int32),
                   jax.ShapeDtypeStruct((ITEM_ROWS, LANES), jnp.int32)),
        compiler_params=_params(0),
        name="moe_sort_meta",
    )(gid)


def _run_copies(runs_ref, i, tile_ref, sorted_hbm, sem, to_sorted):
    pieces = []
    for g in range(N_GROUPS):
        src = runs_ref[i, RUN_SRC + g]
        dst = runs_ref[i, RUN_DST + g]
        length = runs_ref[i, RUN_LEN + g]
        for s in RUN_SIZES:
            def build(src=src, dst=dst, length=length, s=s):
                off = length & (-2 * s)
                a = tile_ref.at[pl.ds(pl.multiple_of(src + off, RUN_ALIGN), s), :]
                b = sorted_hbm.at[pl.ds(pl.multiple_of(dst + off, RUN_ALIGN), s), :]
                return pltpu.make_async_copy(a, b, sem) if to_sorted else pltpu.make_async_copy(b, a, sem)
            pieces.append(((length & s) != 0, build))
    return pieces


def _start(pieces):
    for pred, build in pieces:
        pl.when(pred)(lambda build=build: build().start())


def _wait(pieces):
    for pred, build in pieces:
        pl.when(pred)(lambda build=build: build().wait())


def _start_then_wait(pieces):
    _start(pieces)
    _wait(pieces)


def _scatter_kernel(tm, nts, ntiles, runs_ref, q_ref, u_ref, g_ref, su_hbm, sg_hbm,
                    stu, stg, sem_u, sem_g):
    i = pl.program_id(0)
    slot = i % 2

    def copies(tile, slot):
        return (_run_copies(runs_ref, tile, stu.at[slot], su_hbm, sem_u.at[slot], True)
                + _run_copies(runs_ref, tile, stg.at[slot], sg_hbm, sem_g.at[slot], True))

    r = lax.broadcasted_iota(jnp.int32, (_stage_rows(tm), tm), 0).astype(F32)
    perm = jnp.where(q_ref[...] == r, 1.0, 0.0).astype(BF16)
    stu[slot] = _dot(perm, u_ref[...]).astype(BF16)
    both = _dot(perm, jnp.concatenate(_split2(g_ref[...]), 1))
    stg[slot] = both[:, 0:LANES] + both[:, LANES:2 * LANES]
    _start(copies(i, slot))

    @pl.when(i > 0)
    def _():
        _wait(copies(i - 1, 1 - slot))

    @pl.when(i == ntiles - 1)
    def _():
        _wait(copies(i, slot))
        stu[slot] = jnp.zeros(stu.shape[1:], stu.dtype)
        stg[slot] = jnp.zeros(stg.shape[1:], stg.dtype)
        total = runs_ref[i, RUN_TOTAL]
        tail = nts * tm - total
        nfull = tail // tm
        pieces = []
        for tile_ref, hbm, sem in ((stu.at[slot], su_hbm, sem_u.at[slot]),
                                   (stg.at[slot], sg_hbm, sem_g.at[slot])):
            for j in range(nts - ntiles):
                def full(j=j, tile_ref=tile_ref, hbm=hbm, sem=sem):
                    dst = pl.multiple_of(total + j * tm, RUN_ALIGN)
                    return pltpu.make_async_copy(tile_ref.at[pl.ds(0, tm), :], hbm.at[pl.ds(dst, tm), :], sem)
                pieces.append((j < nfull, full))
            rem = tail - nfull * tm
            for s in RUN_SIZES:
                if s >= tm:
                    continue
                def part(s=s, tile_ref=tile_ref, hbm=hbm, sem=sem):
                    dst = pl.multiple_of(total + nfull * tm + (rem & (-2 * s)), RUN_ALIGN)
                    return pltpu.make_async_copy(tile_ref.at[pl.ds(0, s), :], hbm.at[pl.ds(dst, s), :], sem)
                pieces.append(((rem & s) != 0, part))
        _start_then_wait(pieces)


def _scatter_call(runs, q3, u2, gate, tm):
    n = u2.shape[0]
    ns = _sorted_rows(n, tm)
    grid_spec = pltpu.PrefetchScalarGridSpec(
        num_scalar_prefetch=1,
        grid=(n // tm,),
        in_specs=[pl.BlockSpec((None, 1, tm), lambda i, *_: (i, 0, 0)),
                  pl.BlockSpec((tm, D_MODEL), lambda i, *_: (i, 0)),
                  pl.BlockSpec((tm, LANES), lambda i, *_: (i, 0))],
        out_specs=(pl.BlockSpec(memory_space=pl.ANY), pl.BlockSpec(memory_space=pl.ANY)),
        scratch_shapes=[pltpu.VMEM((2, _stage_rows(tm), D_MODEL), BF16),
                        pltpu.VMEM((2, _stage_rows(tm), LANES), F32),
                        pltpu.SemaphoreType.DMA((2,)), pltpu.SemaphoreType.DMA((2,))])
    return pl.pallas_call(
        functools.partial(_scatter_kernel, tm, ns // tm, n // tm),
        out_shape=(jax.ShapeDtypeStruct((ns, D_MODEL), BF16), jax.ShapeDtypeStruct((ns, LANES), F32)),
        grid_spec=grid_spec,
        compiler_params=_params(1),
        name="moe_scatter",
    )(runs, q3, u2, gate)


def _ungroup(tm, runs_ref, col_ref, ys_hbm, stage, sem):
    i = pl.program_id(0)
    slot = i % 2

    def copies(tile, slot):
        return _run_copies(runs_ref, tile, stage.at[slot], ys_hbm, sem.at[slot], False)

    @pl.when(i == 0)
    def _():
        stage[...] = jnp.zeros_like(stage)
        _start(copies(i, slot))

    @pl.when(i + 1 < pl.num_programs(0))
    def _():
        _start(copies(i + 1, 1 - slot))

    _wait(copies(i, slot))
    col = col_ref[...]
    gidx = col[:, 0:1]
    src = [runs_ref[i, RUN_SRC + g].astype(F32) for g in range(N_GROUPS)]
    first = src[N_GROUPS - 1]
    for g in range(N_GROUPS - 2, -1, -1):
        first = jnp.where(gidx == float(g), src[g], first)
    q = first + col[:, 1:2]
    lanes = lax.broadcasted_iota(jnp.int32, (tm, _stage_rows(tm)), 1).astype(F32)
    perm_t = jnp.where(q == lanes, 1.0, 0.0).astype(BF16)
    return _dot(perm_t, stage[slot])


def _moe_routed_kernel(items_ref, su_ref, sg_ref, wgu_ref, wd_ref, out_ref, acc_ref, wgu_b, wd_b):
    kk = pl.program_id(0)
    grp = items_ref[1, kk]

    @pl.when((kk == 0) | (grp != items_ref[1, jnp.maximum(kk - 1, 0)]))
    def _():
        for e in range(EXPERTS_PER_GROUP):
            wgu_b[e] = wgu_ref[e].astype(BF16)
            wd_b[e] = wd_ref[e].astype(BF16)

    @pl.when(items_ref[3, kk] == 1)
    def _():
        acc_ref[...] = jnp.zeros_like(acc_ref)

    @pl.when(items_ref[2, kk] == 1)
    def _():
        u = su_ref[...]
        gate = sg_ref[...]
        lane = lax.broadcasted_iota(jnp.int32, gate.shape, 1)
        first_lane = N_GROUPS + EXPERTS_PER_GROUP * grp
        acts = []
        for e in range(EXPERTS_PER_GROUP):
            gcol = jnp.sum(jnp.where(lane == first_lane + e, gate, 0.0), axis=-1, keepdims=True)
            hu = _dot(u, wgu_b[e])
            hg = hu[:, 0:D_EXPERT]
            acts.append((hg * jax.nn.sigmoid(hg) * hu[:, D_EXPERT:2 * D_EXPERT] * gcol).astype(BF16))
        wd_all = wd_b[...].reshape(EXPERTS_PER_GROUP * D_EXPERT, D_MODEL)
        acc_ref[...] += _dot(jnp.concatenate(acts, 1), wd_all)

    @pl.when(items_ref[4, kk] == 1)
    def _():
        out_ref[...] = acc_ref[...].astype(out_ref.dtype)


def _moe_routed_call(items, su, sg, wgu, wd, layer, tm):
    ns = su.shape[0]
    n_items = _max_items(ns // tm)
    depth = wgu.shape[0]
    wgu5 = wgu.reshape(depth, N_GROUPS, EXPERTS_PER_GROUP, D_MODEL, 2 * D_EXPERT)
    wd5 = wd.reshape(depth, N_GROUPS, EXPERTS_PER_GROUP, D_EXPERT, D_MODEL)
    grid_spec = pltpu.PrefetchScalarGridSpec(
        num_scalar_prefetch=1,
        grid=(n_items,),
        in_specs=[pl.BlockSpec((tm, D_MODEL), lambda k, it: (it[0, k], 0)),
                  pl.BlockSpec((tm, LANES), lambda k, it: (it[0, k], 0)),
                  pl.BlockSpec((None, None, EXPERTS_PER_GROUP, D_MODEL, 2 * D_EXPERT),
                               lambda k, it: (layer, it[1, k], 0, 0, 0)),
                  pl.BlockSpec((None, None, EXPERTS_PER_GROUP, D_EXPERT, D_MODEL),
                               lambda k, it: (layer, it[1, k], 0, 0, 0))],
        out_specs=pl.BlockSpec((tm, D_MODEL), lambda k, it: (it[0, k], 0)),
        scratch_shapes=[pltpu.VMEM((tm, D_MODEL), F32),
                        pltpu.VMEM((EXPERTS_PER_GROUP, D_MODEL, 2 * D_EXPERT), BF16),
                        pltpu.VMEM((EXPERTS_PER_GROUP, D_EXPERT, D_MODEL), BF16)])
    return pl.pallas_call(
        _moe_routed_kernel,
        out_shape=jax.ShapeDtypeStruct((ns, D_MODEL), BF16),
        grid_spec=grid_spec,
        compiler_params=_params(1),
        name="moe_routed",
    )(items, su, sg, wgu5, wd5)


def _resid_gather_kernel(tm, runs_ref, x_ref, g2_ref, col_ref, ys_hbm, y_ref, stage, sem):
    y_ref[...] = x_ref[...] + g2_ref[...] * _ungroup(tm, runs_ref, col_ref, ys_hbm, stage, sem)


def _resid_gather_call(x, runs, col, ys, mod, mspec, tm):
    n = x.shape[0]
    row = lambda i, *_: (i, 0)
    grid_spec = pltpu.PrefetchScalarGridSpec(
        num_scalar_prefetch=1,
        grid=(n // tm,),
        in_specs=[pl.BlockSpec((tm, D_MODEL), row), mspec(5), pl.BlockSpec((tm, LANES), row),
                  pl.BlockSpec(memory_space=pl.ANY)],
        out_specs=pl.BlockSpec((tm, D_MODEL), row),
        scratch_shapes=[pltpu.VMEM((2, _stage_rows(tm), D_MODEL), BF16), pltpu.SemaphoreType.DMA((2,))])
    return pl.pallas_call(
        functools.partial(_resid_gather_kernel, tm),
        out_shape=jax.ShapeDtypeStruct((n, D_MODEL), F32),
        grid_spec=grid_spec,
        compiler_params=_params(1),
        name="moe_unsort_resid",
    )(runs, x, mod, col, ys)


def _moe_kernel(expand, u_ref, gate_ref, wgu_ref, wd_ref, x1_ref, g2_ref, *rest):
    rep_ref = rest[0] if expand else None
    y_ref, acc_ref = rest[1 if expand else 0:]
    e = pl.program_id(1)

    @pl.when(e == 0)
    def _():
        acc_ref[...] = jnp.zeros_like(acc_ref)

    hu = _dot(u_ref[...], wgu_ref[...].astype(BF16))
    hg = hu[:, 0:D_EXPERT]
    hv = hu[:, D_EXPERT:2 * D_EXPERT]
    gate = gate_ref[...]
    lane = lax.broadcasted_iota(jnp.int32, gate.shape, 1)
    gcol = jnp.sum(jnp.where(lane == N_GROUPS + e, gate, 0.0), axis=-1, keepdims=True)
    act = hg * jax.nn.sigmoid(hg) * hv * gcol
    acc_ref[...] += _dot(act.astype(BF16), wd_ref[...].astype(BF16))

    @pl.when(e == N_EXPERTS - 1)
    def _():
        y_ref[...] = x1_ref[...] + _mod_rows(g2_ref, rep_ref) * acc_ref[...]


def _moe_call(u2, gate, wgu, wd, layer, x1, mod, mspec, rep, tm):
    n = u2.shape[0]
    row = lambda i, e: (i, 0)
    expand = rep is not None
    return pl.pallas_call(
        functools.partial(_moe_kernel, expand),
        out_shape=jax.ShapeDtypeStruct((n, D_MODEL), F32),
        grid=(n // tm, N_EXPERTS),
        in_specs=[pl.BlockSpec((tm, D_MODEL), row), pl.BlockSpec((tm, LANES), row),
                  pl.BlockSpec((None, None, D_MODEL, 2 * D_EXPERT), lambda i, e: (layer, e, 0, 0)),
                  pl.BlockSpec((None, None, D_EXPERT, D_MODEL), lambda i, e: (layer, e, 0, 0)),
                  pl.BlockSpec((tm, D_MODEL), row), mspec(5)]
        + ([pl.BlockSpec(rep.shape, lambda i, e: (0, 0))] if expand else []),
        out_specs=pl.BlockSpec((tm, D_MODEL), row),
        scratch_shapes=[pltpu.VMEM((tm, D_MODEL), F32)],
        compiler_params=_params(2),
        name="moe_dense",
    )(u2, gate, wgu, wd, x1, mod, *([rep] if expand else []))


def _rope128(x, cos, sa, sb):
    return x * cos + pltpu.roll(x, LANES - ROT_DIM // 2, axis=1) * sa \
        + pltpu.roll(x, ROT_DIM // 2, axis=1) * sb


def _kvq_kernel(gather_tm, *refs):
    if gather_tm:
        (runs_ref, x_ref, g2_ref, col_ref, ys_hbm, kvsh_ref, kvsc_ref, sh1_ref, sc1_ref, kvn_ref,
         nm_ref, wkv_ref, wq_ref, kn_ref, qn_ref, cos_ref, sa_ref, sb_ref, g64_ref, gr_ref, gb_ref,
         q_ref, k_ref, v_ref, xa_ref, stage, sem) = refs
        x = x_ref[...] + g2_ref[...] * _ungroup(gather_tm, runs_ref, col_ref, ys_hbm, stage, sem)
        xa_ref[...] = x
        rep_ref = None
    else:
        (x_ref, kvsh_ref, kvsc_ref, sh1_ref, sc1_ref, kvn_ref, nm_ref,
         wkv_ref, wq_ref, kn_ref, qn_ref, cos_ref, sa_ref, sb_ref, g64_ref, gr_ref, gb_ref,
         rep_ref, q_ref, k_ref, v_ref) = refs
        x = x_ref[...]
    xn = _unit_rms(x)
    cos, sa, sb = cos_ref[...], sa_ref[...], sb_ref[...]

    ukv = xn * (kvn_ref[...] * (1.0 + _mod_rows(kvsc_ref, rep_ref))) + _mod_rows(kvsh_ref, rep_ref)
    kv = _dot(ukv.astype(BF16), wkv_ref[...])
    k = kv[:, 0:LANES]
    v_ref[...] = kv[:, LANES:2 * LANES]
    kh, kl = _split2(k * k)
    ms = _dot(kh, g64_ref[...]) + _dot(kl, g64_ref[...])
    k_ref[...] = _rope128(k * lax.rsqrt(ms + RMS_EPS) * kn_ref[...], cos, sa, sb)

    u1 = xn * (nm_ref[...] * (1.0 + _mod_rows(sc1_ref, rep_ref))) + _mod_rows(sh1_ref, rep_ref)
    q = _dot(u1.astype(BF16), wq_ref[...])
    ms16 = _dot((q * q).astype(BF16), gr_ref[...])
    rh, rl = _split2(lax.rsqrt(ms16 + RMS_EPS))
    rsb = _dot(jnp.concatenate([rh, rl], 1), gb_ref[...])
    qn = q * rsb * qn_ref[...]
    scale = HEAD_DIM ** -0.5
    cos_q, sa_q, sb_q = cos * scale, sa * scale, sb * scale
    for i in range(D_MODEL // LANES):
        sl = slice(LANES * i, LANES * (i + 1))
        q_ref[:, sl] = _rope128(qn[:, sl], cos_q, sa_q, sb_q).astype(BF16)


def _kvq_call(x, routed, mod0, mod1, mspec, kvmod, kvspec, rep, kvn, nm, wkv, wq, kn, qn, tabs, tab,
              g64, gr, gb, tm):
    n = x.shape[0]
    row = lambda i, *_: (i, 0)
    const = lambda i, *_: (0, 0)
    vec = pl.BlockSpec((1, D_MODEL), const)
    tok = pl.BlockSpec((tm, D_MODEL), row)
    small = pl.BlockSpec((tm, LANES), row)
    ins = [x, kvmod, kvmod, mod1, mod1, kvn, nm, wkv, wq, kn, qn, *tabs, g64, gr, gb]
    specs = [tok, kvspec(0), kvspec(1), mspec(0), mspec(1), vec, vec,
             pl.BlockSpec((D_MODEL, 2 * LANES), const), pl.BlockSpec((D_MODEL, D_MODEL), const),
             pl.BlockSpec((1, LANES), const), vec, tab, tab, tab,
             pl.BlockSpec((LANES, LANES), const), pl.BlockSpec((D_MODEL, LANES), const),
             pl.BlockSpec((2 * LANES, D_MODEL), const)]
    out_shape = [jax.ShapeDtypeStruct((n, D_MODEL), BF16), jax.ShapeDtypeStruct((n, LANES), F32),
                 jax.ShapeDtypeStruct((n, LANES), F32)]
    out_specs = [tok, small, small]
    scratch = []
    nprefetch = 0
    if routed is not None:
        runs, col, ys = routed
        nprefetch = 1
        ins = [runs, x, mod0, col, ys] + ins[1:]
        specs = [tok, mspec(5), small, pl.BlockSpec(memory_space=pl.ANY)] + specs[1:]
        out_shape.append(jax.ShapeDtypeStruct((n, D_MODEL), F32))
        out_specs.append(tok)
        scratch = [pltpu.VMEM((2, _stage_rows(tm), D_MODEL), BF16), pltpu.SemaphoreType.DMA((2,))]
    else:
        ins.append(rep)
        specs.append(pl.BlockSpec(rep.shape, const))
    grid_spec = pltpu.PrefetchScalarGridSpec(
        num_scalar_prefetch=nprefetch, grid=(n // tm,), in_specs=specs,
        out_specs=tuple(out_specs), scratch_shapes=scratch)
    return pl.pallas_call(
        functools.partial(_kvq_kernel, tm if routed is not None else 0),
        out_shape=tuple(out_shape),
        grid_spec=grid_spec,
        compiler_params=_params(1),
        name="kv_q_proj",
    )(*ins)


def _attn_core(q, kcat, vcat, bias, sinks_ref, o_ref):
    tk = kcat.shape[0]
    pairs = N_Q_HEADS // N_KV_HEADS // 2
    lane = lax.broadcasted_iota(jnp.int32, (1, LANES), 1)
    lo = lane < HEAD_DIM
    kro = pltpu.roll(kcat, HEAD_DIM, axis=1)
    vro = pltpu.roll(vcat, HEAD_DIM, axis=1)
    one_e = jnp.broadcast_to(jnp.where(lo, 1.0, 0.0), (tk, LANES))
    one_o = 1.0 - one_e
    for g in range(N_KV_HEADS):
        if g == 0:
            ke, ko = jnp.where(lo, kcat, 0.0), jnp.where(lo, 0.0, kro)
            ve, vo = jnp.where(lo, vcat, 0.0), jnp.where(lo, 0.0, vro)
        else:
            ke, ko = jnp.where(lo, kro, 0.0), jnp.where(lo, 0.0, kcat)
            ve, vo = jnp.where(lo, vro, 0.0), jnp.where(lo, 0.0, vcat)
        k2 = jnp.concatenate([ke, ko], 0).astype(BF16)
        v2 = jnp.concatenate([jnp.concatenate([ve, one_e], 1),
                              jnp.concatenate([vo, one_o], 1)], 0).astype(BF16)
        for p in range(pairs):
            hp = g * pairs + p
            s = _dot_nt(q[:, LANES * hp:LANES * (hp + 1)], k2)
            halves, corr = [], []
            for par in range(2):
                sp = s[:, par * tk:(par + 1) * tk] + bias
                sink = sinks_ref[2 * hp + par]
                m = jnp.maximum(jnp.max(sp, axis=-1, keepdims=True), sink)
                halves.append(jnp.exp(sp - m).astype(BF16))
                corr.append(jnp.exp(sink - m))
            o2 = _dot(jnp.concatenate(halves, 1), v2)
            den = o2[:, LANES:2 * LANES] + jnp.where(lo, corr[0], corr[1])
            o_ref[:, LANES * hp:LANES * (hp + 1)] = (o2[:, 0:LANES] / den).astype(o_ref.dtype)


ATTN_BLOCKS_PER_STEP = 4


def _attn_prompt_kernel(sinks_ref, q_ref, kp_ref, kc_ref, vp_ref, vc_ref, bias_ref, o_ref):
    kall = jnp.concatenate([kp_ref[...], kc_ref[...]], 0)
    vall = jnp.concatenate([vp_ref[...], vc_ref[...]], 0)
    first = jnp.minimum(pl.program_id(1), 1)
    for j in range(ATTN_BLOCKS_PER_STEP):
        rows = slice(j * WINDOW, (j + 1) * WINDOW)
        keys = slice(j * WINDOW, (j + 2) * WINDOW)
        bias = bias_ref[first] if j == 0 else bias_ref[1]
        _attn_core(q_ref[rows, :], kall[keys, :], vall[keys, :], bias, sinks_ref, o_ref.at[rows, :])


ATTN_SEQS_PER_STEP = 8


def _attn_sample_kernel(sinks_ref, q_ref, kc_ref, kn_ref, vc_ref, vn_ref, bias_ref, o_ref, kbuf, vbuf):
    @pl.when(pl.program_id(0) == 0)
    def _():
        kbuf[...] = jnp.zeros_like(kbuf)
        vbuf[...] = jnp.zeros_like(vbuf)

    for b in range(ATTN_SEQS_PER_STEP):
        kbuf[b, 0:WINDOW, :] = kc_ref[b]
        kbuf[b, WINDOW:WINDOW + SAMPLE_PAD, :] = kn_ref[b]
        vbuf[b, 0:WINDOW, :] = vc_ref[b]
        vbuf[b, WINDOW:WINDOW + SAMPLE_PAD, :] = vn_ref[b]
    for b in range(ATTN_SEQS_PER_STEP):
        _attn_core(q_ref[b], kbuf[b], vbuf[b], bias_ref[...], sinks_ref, o_ref.at[b])


def _window_bias(tq, first):
    qi = jnp.arange(tq)[:, None]
    kj = jnp.arange(2 * WINDOW)[None, :]
    ok = (kj > qi) & (kj <= qi + WINDOW)
    if first:
        ok = ok & (kj >= WINDOW)
    return jnp.where(ok, 0.0, -jnp.inf).astype(F32)


def _attn_prompt_call(sinks, q, k, v, B, nb):
    n = q.shape[0]
    per = ATTN_BLOCKS_PER_STEP
    steps = nb // per
    cur = lambda b, i: (b * steps + i, 0)
    prev = lambda b, i: (b * nb + jnp.maximum(per * i - 1, 0), 0)
    bias = jnp.stack([_window_bias(WINDOW, True), _window_bias(WINDOW, False)])
    kv_prev = pl.BlockSpec((WINDOW, LANES), prev)
    kv_cur = pl.BlockSpec((per * WINDOW, LANES), cur)
    return pl.pallas_call(
        _attn_prompt_kernel,
        out_shape=jax.ShapeDtypeStruct((n, D_MODEL), BF16),
        grid=(B, steps),
        in_specs=[pl.BlockSpec(memory_space=pltpu.SMEM),
                  pl.BlockSpec((per * WINDOW, D_MODEL), cur), kv_prev, kv_cur, kv_prev, kv_cur,
                  pl.BlockSpec((2, WINDOW, 2 * WINDOW), lambda b, i: (0, 0, 0))],
        out_specs=pl.BlockSpec((per * WINDOW, D_MODEL), cur),
        compiler_params=_params(2),
        name="swa_prompt",
    )(sinks, q, k, k, v, v, bias)


def _attn_sample_call(sinks, q, kcache, knew, vcache, vnew):
    B = q.shape[0]
    nseq = ATTN_SEQS_PER_STEP
    b3 = lambda b: (b, 0, 0)
    cache = pl.BlockSpec((nseq, WINDOW, LANES), b3)
    new = pl.BlockSpec((nseq, SAMPLE_PAD, LANES), b3)
    return pl.pallas_call(
        _attn_sample_kernel,
        out_shape=jax.ShapeDtypeStruct(q.shape, BF16),
        grid=(B // nseq,),
        in_specs=[pl.BlockSpec(memory_space=pltpu.SMEM),
                  pl.BlockSpec((nseq, SAMPLE_PAD, D_MODEL), b3), cache, new, cache, new,
                  pl.BlockSpec((SAMPLE_PAD, 2 * WINDOW), lambda b: (0, 0))],
        out_specs=pl.BlockSpec((nseq, SAMPLE_PAD, D_MODEL), b3),
        scratch_shapes=[pltpu.VMEM((nseq, 2 * WINDOW, LANES), F32),
                        pltpu.VMEM((nseq, 2 * WINDOW, LANES), F32)],
        compiler_params=_params(1),
        name="swa_sample",
    )(sinks, q, kcache, knew, vcache, vnew, _window_bias(SAMPLE_PAD, False))


def _rope_tables(pos):
    half = ROT_DIM // 2
    inv = ROPE_THETA ** (-jnp.arange(half, dtype=F32) / half)
    ang = pos.astype(F32)[:, None] * inv[None]
    cos, sin = jnp.cos(ang), jnp.sin(ang)
    d = jnp.arange(LANES) % HEAD_DIM
    idx = d % half
    cos_t = jnp.where(d < ROT_DIM, cos[:, idx], 1.0)
    sa = jnp.where(d < half, -sin[:, idx], 0.0)
    sb = jnp.where((d >= half) & (d < ROT_DIM), sin[:, idx], 0.0)
    return cos_t, sa, sb


def _pad_lanes(a, value=0.0):
    return jnp.pad(a, ((0, 0), (0, LANES - a.shape[1])), constant_values=value)


def _prep_weights(ada_w, ada_b, norm_mix, norm_ffn, a_w_in, a_b_gates, a_head_norm, a_w_out,
                  kv_ada_w, kv_ada_b, kv_norm, w_k, w_v, k_norm, b_w_q, b_q_norm, b_sinks, b_w_o,
                  moe_w_group, moe_b_group, moe_w_expert, moe_b_expert, moe_w_gate_up, moe_w_down):
    w = {}
    g0 = QK_COLS + 2 * V_COLS
    w["w_in"] = a_w_in
    w["w_gates"] = jnp.concatenate([_pad_lanes(a_w_in[0, :, g0:g0 + M_HEADS]),
                                    _pad_lanes(a_w_in[0, :, g0 + M_HEADS:])], 1)
    w["bli"] = _pad_lanes(a_b_gates[0][None, :M_HEADS])
    w["blf"] = _pad_lanes(a_b_gates[0][None, M_HEADS:])
    w["head_norm"] = a_head_norm[0][None]
    w["w_out"] = a_w_out[0].astype(BF16)
    w["norm_mix"] = [norm_mix[l][None] for l in range(2)]
    w["norm_ffn"] = [norm_ffn[l][None] for l in range(2)]
    w["router"] = []
    for l in range(2):
        wr = _pad_lanes(jnp.concatenate([moe_w_group[l], moe_w_expert[l]], 1))
        hi = wr.astype(BF16)
        lo = (wr - hi.astype(F32)).astype(BF16)
        br = _pad_lanes(jnp.concatenate([moe_b_group[l], moe_b_expert[l]])[None])
        w["router"].append((jnp.concatenate([hi, lo], 1), br))
    w["w_gu"] = moe_w_gate_up
    w["w_d"] = moe_w_down
    w["kv_norm"] = kv_norm[None]
    w["w_kv"] = jnp.concatenate([w_k, w_v], 1).astype(BF16)
    w["k_norm"] = jnp.tile(k_norm, N_KV_HEADS)[None]
    w["w_q"] = b_w_q[0].astype(BF16)
    w["q_norm"] = jnp.tile(b_q_norm[0], N_Q_HEADS)[None]
    w["sinks"] = b_sinks[0]
    w["w_o"] = b_w_o[0].astype(BF16)
    lanes = jnp.arange(LANES)
    feat = jnp.arange(D_MODEL)
    w["g64"] = jnp.where((lanes[:, None] // HEAD_DIM) == (lanes[None, :] // HEAD_DIM),
                         1.0 / HEAD_DIM, 0.0).astype(BF16)
    w["gr"] = jnp.where((feat[:, None] // HEAD_DIM) == lanes[None, :], 1.0 / HEAD_DIM, 0.0).astype(BF16)
    gb = jnp.where(lanes[:, None] == (feat[None, :] // HEAD_DIM), 1.0, 0.0).astype(BF16)
    w["gb"] = jnp.concatenate([gb, gb], 0)
    return w


def _trunk(x2, mods, kvmod, mspec, mspec_big, rep, w, *, B, T, L, tm, tm_big, tabs, tabspec, c0, n0, m0,
           cache_k=None, cache_v=None):
    sample = cache_k is not None
    nc = T // L if not sample else 1

    qk, v, o, li, lf = _inproj_call(x2, mods[0], mspec_big, rep, w["norm_mix"][0], w["w_in"],
                                    w["w_gates"], w["bli"], w["blf"], tm_big)
    if sample:
        def padtok(a, value=0.0):
            a = a.reshape(B, T, a.shape[-1])
            a = jnp.pad(a, ((0, 0), (0, L - T), (0, 0)), constant_values=value)
            return a.reshape(B * L, a.shape[-1])
        qk, v, li, lf = padtok(qk), padtok(v), padtok(li, M_EMPTY), padtok(lf)
    h, c_new, n_new, m_new = _mlstm_call(qk, v, li, lf, c0, n0, m0, B, nc, L)
    if sample:
        h = h.reshape(B, L, V_COLS)[:, :T].reshape(B * T, V_COLS)
    routed = not sample
    ntiles = (B * T) // tm

    def routed_moe(u2, gate, gid, col, layer):
        q, runs, items = _sort_meta_call(gid.reshape(ntiles, tm), tm)
        su, sg = _scatter_call(runs, q.reshape(ntiles, 1, tm), u2, gate, tm)
        ys = _moe_routed_call(items, su, sg, w["w_gu"], w["w_d"], layer, tm)
        return runs, col, ys

    post0 = _post_call(True, routed, h, o, x2, mods[0], mspec, rep, w["norm_ffn"][0], w["head_norm"],
                       w["w_out"], *w["router"][0], tm)
    kvq_args = (mods[0], mods[1], mspec, kvmod, mspec, rep, w["kv_norm"], w["norm_mix"][1],
                w["w_kv"], w["w_q"], w["k_norm"], w["q_norm"], tabs, tabspec,
                w["g64"], w["gr"], w["gb"], tm)

    if routed:
        x1, u2, gate, gid, col = post0
        q, k, vv, xa = _kvq_call(x1, routed_moe(u2, gate, gid, col, 0), *kvq_args)
    else:
        x1, u2, gate = post0
        xa = _moe_call(u2, gate, w["w_gu"], w["w_d"], 0, x1, mods[0], mspec_big, rep, tm_big)
        q, k, vv = _kvq_call(xa, None, *kvq_args)
    if not sample:
        att = _attn_prompt_call(w["sinks"], q, k, vv, B, T // WINDOW)
        k_win = k.reshape(B, T, LANES)[:, T - WINDOW:].reshape(B, WINDOW, N_KV_HEADS, HEAD_DIM)
        v_win = vv.reshape(B, T, LANES)[:, T - WINDOW:].reshape(B, WINDOW, N_KV_HEADS, HEAD_DIM)
    else:
        def padseq(a):
            return jnp.pad(a.reshape(B, T, a.shape[-1]), ((0, 0), (0, SAMPLE_PAD - T), (0, 0)))
        kc = cache_k.reshape(B, WINDOW, LANES)
        vc = cache_v.reshape(B, WINDOW, LANES)
        att = _attn_sample_call(w["sinks"], padseq(q), kc, padseq(k), vc, padseq(vv))
        att = att[:, :T].reshape(B * T, D_MODEL)
        k_win = jnp.concatenate([kc[:, T:], k.reshape(B, T, LANES)], 1)
        v_win = jnp.concatenate([vc[:, T:], vv.reshape(B, T, LANES)], 1)
        k_win = k_win.reshape(B, WINDOW, N_KV_HEADS, HEAD_DIM)
        v_win = v_win.reshape(B, WINDOW, N_KV_HEADS, HEAD_DIM)
    post1 = _post_call(False, routed, att, None, xa, mods[1], mspec, rep, w["norm_ffn"][1], None,
                       w["w_o"], *w["router"][1], tm)
    if routed:
        x3, u4, gate, gid, col = post1
        runs, col, ys = routed_moe(u4, gate, gid, col, 1)
        y = _resid_gather_call(x3, runs, col, ys, mods[1], mspec, tm)
    else:
        x3, u4, gate = post1
        y = _moe_call(u4, gate, w["w_gu"], w["w_d"], 1, x3, mods[1], mspec_big, rep, tm_big)
    c_out = c_new[None]
    n_out = n_new.reshape(1, B, M_HEADS, M_DK)
    m_out = m_new[:, 0, :M_HEADS][None]
    return y, c_out, n_out, m_out, k_win, v_win


def kernel(x_prompt, x_sample, c_prompt, c_sample, state_c, state_n, state_m, cache_k_win, cache_v_win, ada_w, ada_b, norm_mix, norm_ffn, a_w_in, a_b_gates, a_head_norm, a_w_out, kv_ada_w, kv_ada_b, kv_norm, w_k, w_v, k_norm, b_w_q, b_q_norm, b_sinks, b_w_o, moe_w_group, moe_b_group, moe_w_expert, moe_b_expert, moe_w_gate_up, moe_w_down):
    Bp, Tp, D = x_prompt.shape
    Bs, Ts, _ = x_sample.shape
    w = _prep_weights(ada_w, ada_b, norm_mix, norm_ffn, a_w_in, a_b_gates, a_head_norm, a_w_out,
                      kv_ada_w, kv_ada_b, kv_norm, w_k, w_v, k_norm, b_w_q, b_q_norm, b_sinks, b_w_o,
                      moe_w_group, moe_b_group, moe_w_expert, moe_b_expert, moe_w_gate_up, moe_w_down)

    rows = Bp + Bs
    rpad = -rows % 8
    c_all = jnp.concatenate([c_prompt, c_sample, jnp.zeros((rpad, D), F32)], 0)
    mod = _ada_call(c_all, ada_w, ada_b[:, None, :])
    kvm = _ada_call(c_all, kv_ada_w[None], kv_ada_b[None, None, :])

    tm_p = TOKEN_TILE
    tiles_per_seq = Tp // tm_p
    mods_p = [mod[l, :Bp][:, None, :] for l in range(2)]
    kvmod_p = kvm[0, :Bp][:, None, :]

    def mspec_p(col):
        return pl.BlockSpec((None, 1, D_MODEL), lambda i, *_: (i // tiles_per_seq, 0, col))

    big_tiles_per_seq = Tp // BIG_TOKEN_TILE

    def mspec_big_p(col):
        return pl.BlockSpec((None, 1, D_MODEL), lambda i, *_: (i // big_tiles_per_seq, 0, col))

    tabs_p = _rope_tables(jnp.arange(Tp, dtype=jnp.int32))
    tabspec_p = pl.BlockSpec((tm_p, LANES), lambda i, *_: (i % tiles_per_seq, 0))
    npairs = M_HEADS // 2
    c0 = jnp.zeros((Bp, M_HEADS, M_DK, M_DV), F32)
    n0 = jnp.zeros((Bp, npairs, LANES), F32)
    m0 = jnp.pad(jnp.full((Bp, 1, M_HEADS), M_EMPTY, F32), ((0, 0), (0, 0), (0, LANES - M_HEADS)))
    yp, cp, np_, mp, kwp, vwp = _trunk(
        x_prompt.reshape(Bp * Tp, D), mods_p, kvmod_p, mspec_p, mspec_big_p, None, w,
        B=Bp, T=Tp, L=M_CHUNK, tm=tm_p, tm_big=BIG_TOKEN_TILE, tabs=tabs_p, tabspec=tabspec_p,
        c0=c0, n0=n0, m0=m0)

    ns = Bs * Ts
    mods_s = [mod[l, Bp:Bp + Bs][None] for l in range(2)]
    kvmod_s = kvm[0, Bp:Bp + Bs][None]
    rep = (jnp.arange(ns)[:, None] // Ts == jnp.arange(Bs)[None, :]).astype(BF16)

    def mspec_s(col):
        return pl.BlockSpec((None, Bs, D_MODEL), lambda i, *_: (0, 0, col))

    tabs_s = _rope_tables(PAST_LEN + jnp.arange(Ts, dtype=jnp.int32))
    tabs_s = tuple(jnp.tile(t, (Bs, 1)) for t in tabs_s)
    m0s = jnp.pad(state_m[0][:, None, :], ((0, 0), (0, 0), (0, LANES - M_HEADS)))
    ys, cs, ns_, ms, kws, vws = _trunk(
        x_sample.reshape(ns, D), mods_s, kvmod_s, mspec_s, mspec_s, rep, w,
        B=Bs, T=Ts, L=SAMPLE_PAD, tm=ns, tm_big=ns, tabs=tabs_s,
        tabspec=pl.BlockSpec((ns, LANES), lambda i, *_: (0, 0)),
        c0=state_c[0], n0=state_n[0].reshape(Bs, npairs, LANES), m0=m0s,
        cache_k=cache_k_win, cache_v=cache_v_win)

    return (yp.reshape(Bp, Tp, D), ys.reshape(Bs, Ts, D), cp, np_, mp, kwp, vwp,
            cs, ns_, ms, kws, vws)
```

```python
import functools

import jax
import jax.numpy as jnp
from jax import lax
from jax.experimental import pallas as pl
from jax.experimental.pallas import tpu as pltpu

F32 = jnp.float32
BF16 = jnp.bfloat16

D_MODEL = 1024
PAST_LEN = 8192
M_HEADS = 8
M_DK = 64
M_DV = 128
M_CHUNK = 128
GATE_SOFTCAP = 15.0
M_EMPTY = -1e30
WINDOW = 128
HEAD_DIM = 64
N_Q_HEADS = 16
N_KV_HEADS = 2
ROPE_THETA = 500000.0
ROT_DIM = 16
N_GROUPS = 4
EXPERTS_PER_GROUP = 4
N_EXPERTS = 16
D_EXPERT = 256
RMS_EPS = 1e-6

LANES = 128
QK_COLS = 2 * M_HEADS * M_DK
V_COLS = M_HEADS * M_DV
SAMPLE_PAD = 16
TOKEN_TILE = 512
BIG_TOKEN_TILE = 1024
VMEM_LIMIT = 52 * 1024 * 1024

NT_DIMS = (((1,), (1,)), ((), ()))


def _params(n_axes):
    return pltpu.CompilerParams(dimension_semantics=("arbitrary",) * n_axes,
                                vmem_limit_bytes=VMEM_LIMIT)


def _dot(a, b):
    return jnp.dot(a, b, preferred_element_type=F32)


def _dot_nt(a, b):
    return lax.dot_general(a, b, NT_DIMS, preferred_element_type=F32)


def _split2(x):
    hi = x.astype(BF16)
    lo = (x - hi.astype(F32)).astype(BF16)
    return hi, lo


def _split3(x):
    hi = x.astype(BF16)
    r = x - hi.astype(F32)
    mid = r.astype(BF16)
    lo = (r - mid.astype(F32)).astype(BF16)
    return hi, mid, lo


def _unit_rms(x):
    return x * lax.rsqrt(jnp.mean(x * x, axis=-1, keepdims=True) + RMS_EPS)


def _ada_kernel(c_ref, w_ref, b_ref, o_ref):
    c = c_ref[...]
    cs = (c * jax.nn.sigmoid(c)).astype(BF16)
    o_ref[...] = _dot(cs, w_ref[...].astype(BF16)) + b_ref[...]


def _ada_call(c, w, b):
    g, d, n = w.shape
    r = c.shape[0]
    tn = 1024
    return pl.pallas_call(
        _ada_kernel,
        out_shape=jax.ShapeDtypeStruct((g, r, n), F32),
        grid=(g, n // tn),
        in_specs=[pl.BlockSpec((r, d), lambda i, j: (0, 0)),
                  pl.BlockSpec((None, d, tn), lambda i, j: (i, 0, j)),
                  pl.BlockSpec((None, 1, tn), lambda i, j: (i, 0, j))],
        out_specs=pl.BlockSpec((None, r, tn), lambda i, j: (i, 0, j)),
        compiler_params=_params(2),
        name="ada_mod",
    )(c, w, b)


def _mod_rows(ref, rep_ref):
    if rep_ref is None:
        return ref[...]
    hi, mid, lo = _split3(ref[...])
    rep = rep_ref[...]
    return _dot(rep, hi) + _dot(rep, mid) + _dot(rep, lo)


def _inproj_kernel(expand, x_ref, sh_ref, sc_ref, nw_ref, w_ref, wg_ref, bli_ref, blf_ref, *rest):
    rep_ref = rest[0] if expand else None
    qk_ref, v_ref, o_ref, li_ref, lf_ref, wb_ref = rest[1 if expand else 0:]

    @pl.when(pl.program_id(0) == 0)
    def _():
        wb_ref[...] = w_ref[...].astype(BF16)

    u = _unit_rms(x_ref[...]) * (nw_ref[...] * (1.0 + _mod_rows(sc_ref, rep_ref))) \
        + _mod_rows(sh_ref, rep_ref)
    ub = u.astype(BF16)
    half = QK_COLS // 2
    q = _dot(ub, wb_ref[:, 0:half]) * (M_DK ** -0.5)
    qk_ref[:, 0:half] = q.astype(BF16)
    qk_ref[:, half:QK_COLS] = _dot(ub, wb_ref[:, half:QK_COLS]).astype(BF16)
    v_ref[...] = _dot(ub, wb_ref[:, QK_COLS:QK_COLS + V_COLS]).astype(BF16)
    o_ref[...] = jax.nn.sigmoid(_dot(ub, wb_ref[:, QK_COLS + V_COLS:QK_COLS + 2 * V_COLS])).astype(BF16)
    lane = lax.broadcasted_iota(jnp.int32, (1, LANES), 1)
    live = lane < M_HEADS
    gates = _dot(ub, wg_ref[...].astype(BF16))
    gi = gates[:, 0:LANES] + bli_ref[...]
    gf = gates[:, LANES:2 * LANES] + blf_ref[...]
    li = GATE_SOFTCAP * jnp.tanh(gi / GATE_SOFTCAP)
    fpre = GATE_SOFTCAP * jnp.tanh(gf / GATE_SOFTCAP)
    lf = jnp.minimum(fpre, 0.0) - jnp.log1p(jnp.exp(-jnp.abs(fpre)))
    li_ref[...] = jnp.where(live, li, 0.0)
    lf_ref[...] = jnp.where(live, lf, 0.0)


def _inproj_call(x, mod, mspec, rep, nw, w_in, wg, bli, blf, tm):
    n = x.shape[0]
    row = lambda i: (i, 0)
    const = lambda i: (0, 0)
    main_cols = QK_COLS + 2 * V_COLS
    expand = rep is not None
    return pl.pallas_call(
        functools.partial(_inproj_kernel, expand),
        out_shape=(jax.ShapeDtypeStruct((n, QK_COLS), BF16),
                   jax.ShapeDtypeStruct((n, V_COLS), BF16),
                   jax.ShapeDtypeStruct((n, V_COLS), BF16),
                   jax.ShapeDtypeStruct((n, LANES), F32),
                   jax.ShapeDtypeStruct((n, LANES), F32)),
        grid=(n // tm,),
        in_specs=[pl.BlockSpec((tm, D_MODEL), row), mspec(0), mspec(1),
                  pl.BlockSpec((1, D_MODEL), const),
                  pl.BlockSpec((None, D_MODEL, main_cols), lambda i: (0, 0, 0),
                               pipeline_mode=pl.Buffered(1)),
                  pl.BlockSpec((D_MODEL, 2 * LANES), const),
                  pl.BlockSpec((1, LANES), const), pl.BlockSpec((1, LANES), const)]
        + ([pl.BlockSpec(rep.shape, const)] if expand else []),
        out_specs=(pl.BlockSpec((tm, QK_COLS), row), pl.BlockSpec((tm, V_COLS), row),
                   pl.BlockSpec((tm, V_COLS), row), pl.BlockSpec((tm, LANES), row),
                   pl.BlockSpec((tm, LANES), row)),
        scratch_shapes=[pltpu.VMEM((D_MODEL, main_cols), BF16)],
        compiler_params=_params(1),
        name="mlstm_inproj",
    )(x, mod, mod, nw, w_in, wg, bli, blf, *([rep] if expand else []))


def _mlstm_kernel(L, nseq, scale_q, *refs):
    c = pl.program_id(1)
    seqs = [[r.at[b] for r in refs] for b in range(nseq)]

    @pl.when(c == 0)
    def _():
        for s in seqs:
            _mlstm_load_state(*s[4:7], *s[11:14])

    for s in seqs:
        _mlstm_seq_step(L, scale_q, *s[0:4], s[7], *s[11:14])

    @pl.when(c == pl.num_programs(1) - 1)
    def _():
        for s in seqs:
            _mlstm_store_state(*s[8:14])


def _mlstm_load_state(c0_ref, n0_ref, m0_ref, c2_s, n_s, m_s):
    m_s[...] = m0_ref[...]
    n_s[...] = n0_ref[...]
    z = jnp.zeros((M_DK, M_DV), F32)
    for j in range(M_HEADS // 2):
        c2_s[j] = jnp.concatenate([jnp.concatenate([c0_ref[2 * j], z], 1),
                                   jnp.concatenate([z, c0_ref[2 * j + 1]], 1)], 0)


def _mlstm_seq_step(L, scale_q, qk_ref, v_ref, li_ref, lf_ref, h_ref, c2_s, n_s, m_s):
    npairs = M_HEADS // 2
    hd = M_DK

    lane = lax.broadcasted_iota(jnp.int32, (1, LANES), 1)
    lo128 = lane < hd
    lane256 = lax.broadcasted_iota(jnp.int32, (1, 2 * LANES), 1)
    lo256 = lane256 < LANES
    row128 = lax.broadcasted_iota(jnp.int32, (LANES, 1), 0)
    top = row128 < hd
    blockdiag = (top & lo256) | (jnp.logical_not(top) & jnp.logical_not(lo256))

    LI = li_ref[...]
    LF = lf_ref[...]
    rowL = lax.broadcasted_iota(jnp.int32, (L, LANES), 0)

    def prefix(x, op, ident):
        d = 1
        while d < L:
            shifted = pltpu.roll(x, d, axis=0)
            x = op(x, jnp.where(rowL >= d, shifted, ident))
            d *= 2
        return x

    Bc = prefix(LF, jnp.add, 0.0)
    Cm = LI - Bc
    mprev = m_s[...]
    Gc = jnp.maximum(mprev, prefix(Cm, jnp.maximum, -jnp.inf))
    A = jnp.exp(mprev - Gc)
    bL = Bc[L - 1:L, :]
    DL = bL + Cm
    mnew = jnp.maximum(bL + mprev, jnp.max(DL, axis=0, keepdims=True))
    ast = jnp.exp(bL + mprev - mnew)
    WST = jnp.exp(DL - mnew)

    def pad_rows(x, rows):
        if x.shape[0] == rows:
            return x
        return jnp.concatenate([x, jnp.zeros((rows - x.shape[0], x.shape[1]), x.dtype)], 0)

    kw_lanes = max(L, hd)
    assert kw_lanes in (hd, LANES) and L <= LANES
    wide = kw_lanes == LANES
    lane_s = lax.broadcasted_iota(jnp.int32, (1, 2 * kw_lanes), 1)
    row_s = lax.broadcasted_iota(jnp.int32, (2 * kw_lanes, 1), 0)
    cm_pad = pad_rows(Cm, kw_lanes)
    if wide:
        XT = cm_pad.T
    else:
        XT = jnp.concatenate([cm_pad, cm_pad], 0).T
    causal = (lane_s & (kw_lanes - 1)) <= lax.broadcasted_iota(jnp.int32, (L, 2 * kw_lanes), 0)
    top_s = row_s < kw_lanes
    J = jnp.where((top_s & lo256) | (jnp.logical_not(top_s) & jnp.logical_not(lo256)),
                  1.0, 0.0).astype(BF16)

    EXPM = jnp.exp(-(Bc + Gc))

    def bc(X, h):
        return jnp.broadcast_to(X[:, h:h + 1], X.shape)

    for j in range(npairs):
        he, ho = 2 * j, 2 * j + 1
        p128 = slice(LANES * j, LANES * (j + 1))
        p256 = slice(2 * LANES * j, 2 * LANES * (j + 1))

        def pair128(X):
            return jnp.where(lo128, bc(X, he), bc(X, ho))

        def pair256(X):
            return jnp.concatenate([bc(X, he), bc(X, ho)], 1)

        q128 = qk_ref[:, LANES * j:LANES * (j + 1)]
        k128 = qk_ref[:, QK_COLS // 2 + LANES * j:QK_COLS // 2 + LANES * (j + 1)]
        v256 = v_ref[:, 2 * LANES * j:2 * LANES * (j + 1)]
        zk = jnp.zeros_like(k128)
        zv = jnp.zeros_like(v256)
        K2t = jnp.concatenate([pad_rows(jnp.where(lo128, k128, zk), kw_lanes),
                               pad_rows(jnp.where(lo128, zk, k128), kw_lanes)], 0)
        V2 = jnp.concatenate([pad_rows(jnp.where(lo256, v256, zv), kw_lanes),
                              pad_rows(jnp.where(lo256, zv, v256), kw_lanes)], 0)
        S = _dot_nt(q128, K2t)
        if wide:
            crow = jnp.concatenate([XT[he:he + 1, :], XT[ho:ho + 1, :]], 1)
            arg = jnp.where(causal, crow - pair256(Gc), -jnp.inf)
        else:
            crow = jnp.where(lo128, XT[he:he + 1, :], XT[ho:ho + 1, :])
            arg = jnp.where(causal, crow - pair128(Gc), -jnp.inf)
        Sw = (S * jnp.exp(arg)).astype(BF16)
        num_intra = _dot(Sw, V2)
        rowsum = _dot(Sw, J)
        C2 = c2_s[j]
        npair = n_s[j:j + 1, :]
        Nrow2 = jnp.concatenate(
            [jnp.broadcast_to(jnp.where(lo128, npair, 0.0), (LANES, LANES)),
             jnp.broadcast_to(jnp.where(lo128, 0.0, npair), (LANES, LANES))], 0).astype(BF16)
        if scale_q:
            qa = (q128.astype(F32) * pair128(A)).astype(BF16)
            num = _dot(qa, C2.astype(BF16)) + num_intra
            den = _dot_nt(qa, Nrow2) + rowsum
        else:
            a256 = pair256(A)
            num = a256 * _dot(q128, C2.astype(BF16)) + num_intra
            den = a256 * _dot_nt(q128, Nrow2) + rowsum
        h = num / jnp.maximum(jnp.abs(den), pair256(EXPM))
        h_ref[:, p256] = h.astype(h_ref.dtype)

        kw = k128.astype(F32) * pair128(WST)
        n_s[j:j + 1, :] = pair128(ast) * npair + jnp.sum(kw, axis=0, keepdims=True)
        kwT = pad_rows(kw, LANES).T.astype(BF16)
        dC = _dot(kwT, pad_rows(v256, LANES))
        c2_s[j] = pair256(ast) * C2 + jnp.where(blockdiag, dC, 0.0)

    m_s[...] = mnew


def _mlstm_store_state(cout_ref, nout_ref, mout_ref, c2_s, n_s, m_s):
    for j in range(M_HEADS // 2):
        C2 = c2_s[j]
        cout_ref[2 * j] = C2[0:M_DK, 0:M_DV]
        cout_ref[2 * j + 1] = C2[M_DK:2 * M_DK, M_DV:2 * M_DV]
    nout_ref[...] = n_s[...]
    mout_ref[...] = m_s[...]


MLSTM_SEQS_PER_STEP = 8


def _mlstm_call(qk, v, li, lf, c0, n0, m0, B, nc, L):
    npairs = M_HEADS // 2
    nseq = min(B, MLSTM_SEQS_PER_STEP)
    T = nc * L
    tok3 = lambda a: a.reshape(B, T, a.shape[-1])
    tok = lambda b, c: (b, c, 0)
    st4 = lambda b, c: (b, 0, 0, 0)
    st3 = lambda b, c: (b, 0, 0)
    h, c_new, n_new, m_new = pl.pallas_call(
        functools.partial(_mlstm_kernel, L, nseq, nc == 1),
        out_shape=(jax.ShapeDtypeStruct((B, T, V_COLS), BF16),
                   jax.ShapeDtypeStruct((B, M_HEADS, M_DK, M_DV), F32),
                   jax.ShapeDtypeStruct((B, npairs, LANES), F32),
                   jax.ShapeDtypeStruct((B, 1, LANES), F32)),
        grid=(B // nseq, nc),
        in_specs=[pl.BlockSpec((nseq, L, QK_COLS), tok), pl.BlockSpec((nseq, L, V_COLS), tok),
                  pl.BlockSpec((nseq, L, LANES), tok), pl.BlockSpec((nseq, L, LANES), tok),
                  pl.BlockSpec((nseq, M_HEADS, M_DK, M_DV), st4),
                  pl.BlockSpec((nseq, npairs, LANES), st3),
                  pl.BlockSpec((nseq, 1, LANES), st3)],
        out_specs=(pl.BlockSpec((nseq, L, V_COLS), tok),
                   pl.BlockSpec((nseq, M_HEADS, M_DK, M_DV), st4),
                   pl.BlockSpec((nseq, npairs, LANES), st3),
                   pl.BlockSpec((nseq, 1, LANES), st3)),
        scratch_shapes=[pltpu.VMEM((nseq, npairs, 2 * M_DK, 2 * M_DV), F32),
                        pltpu.VMEM((nseq, npairs, LANES), F32),
                        pltpu.VMEM((nseq, 1, LANES), F32)],
        compiler_params=_params(2),
        name="mlstm_chunks",
    )(tok3(qk), tok3(v), tok3(li), tok3(lf), c0, n0, m0)
    return h.reshape(B * T, V_COLS), c_new, n_new, m_new


def _route(lg):
    lane = lax.broadcasted_iota(jnp.int32, lg.shape, 1)
    lanef = lane.astype(F32)
    neg = -jnp.inf
    far = float(LANES)
    gm = lane < N_GROUPS
    lgm = jnp.where(gm, lg, neg)
    gmax = jnp.max(lgm, axis=-1, keepdims=True)
    gsum = jnp.sum(jnp.exp(lgm - gmax), axis=-1, keepdims=True)
    g_w = 1.0 / gsum
    gidx = jnp.min(jnp.where(gm & (lg == gmax), lanef, far), axis=-1, keepdims=True)
    first = N_GROUPS + EXPERTS_PER_GROUP * gidx
    sel = (lanef >= first) & (lanef < first + EXPERTS_PER_GROUP)
    l1 = jnp.max(jnp.where(sel, lg, neg), axis=-1, keepdims=True)
    i1 = jnp.min(jnp.where(sel & (lg == l1), lanef, far), axis=-1, keepdims=True)
    sel2 = sel & (lanef != i1)
    l2 = jnp.max(jnp.where(sel2, lg, neg), axis=-1, keepdims=True)
    i2 = jnp.min(jnp.where(sel2 & (lg == l2), lanef, far), axis=-1, keepdims=True)
    r = jnp.exp(l2 - l1)
    w1 = g_w / (1.0 + r)
    w2 = w1 * r
    return jnp.where(lanef == i1, w1, jnp.where(lanef == i2, w2, 0.0)), gidx


def _post_kernel(mlstm, routed, expand, *refs):
    refs = list(refs)
    h_ref = refs.pop(0)
    o_ref = refs.pop(0) if mlstm else None
    x_ref, g1_ref, sh2_ref, sc2_ref, nf_ref = refs[:5]
    refs = refs[5:]
    hn_ref = refs.pop(0) if mlstm else None
    wout_ref, wrhl_ref, br_ref = refs[:3]
    refs = refs[3:]
    tri_ref = refs.pop(0) if routed else None
    rep_ref = refs.pop(0) if expand else None
    x1_ref, u2_ref, gate_ref = refs[:3]
    outs = refs[3:]
    if mlstm:
        hf = h_ref[...].astype(F32)
        parts = [_unit_rms(hf[:, M_DV * i:M_DV * (i + 1)]) for i in range(M_HEADS)]
        hn = jnp.concatenate(parts, 1) * hn_ref[...]
        hg = (hn * o_ref[...].astype(F32)).astype(BF16)
    else:
        hg = h_ref[...]
    x1 = x_ref[...] + _mod_rows(g1_ref, rep_ref) * _dot(hg, wout_ref[...])
    x1_ref[...] = x1
    u2 = _unit_rms(x1) * (nf_ref[...] * (1.0 + _mod_rows(sc2_ref, rep_ref))) + _mod_rows(sh2_ref, rep_ref)
    uh, ul = _split2(u2)
    wrh = wrhl_ref[:, 0:LANES]
    lg = _dot(uh, wrh) + _dot(ul, wrh) + _dot(uh, wrhl_ref[:, LANES:2 * LANES]) + br_ref[...]
    gate, gidx = _route(lg)
    u2_ref[...] = uh
    gate_ref[...] = gate
    if not routed:
        return
    gid_ref, col_ref = outs
    rows = []
    for blk in range(x1.shape[0] // LANES):
        col = jnp.broadcast_to(gidx[LANES * blk:LANES * (blk + 1), :], (LANES, LANES))
        rows.append(col.T[0:1, :])
    gid_ref[...] = jnp.concatenate(rows, 1)
    lanef = lax.broadcasted_iota(jnp.int32, gate.shape, 1).astype(F32)
    onehot = jnp.where(lanef == gidx, 1.0, 0.0)
    earlier = _dot(tri_ref[...], onehot.astype(BF16))
    rank = jnp.sum(onehot * earlier, axis=-1, keepdims=True)
    col_ref[...] = jnp.where(lanef == 0.0, gidx, jnp.where(lanef == 1.0, rank, 0.0))


def _post_call(mlstm, routed, h, o, x, mod, mspec, rep, nf, hn, wout, wrhl, br, tm):
    n = x.shape[0]
    expand = rep is not None
    row = lambda i: (i, 0)
    const = lambda i: (0, 0)
    tok = pl.BlockSpec((tm, D_MODEL), row)
    small = pl.BlockSpec((tm, LANES), row)
    vec = pl.BlockSpec((1, D_MODEL), const)
    ins = [h] + ([o] if mlstm else []) + [x, mod, mod, mod, nf] + ([hn] if mlstm else []) \
        + [wout, wrhl, br]
    specs = [tok] + ([tok] if mlstm else []) + [tok, mspec(2), mspec(3), mspec(4), vec] \
        + ([vec] if mlstm else []) \
        + [pl.BlockSpec((D_MODEL, D_MODEL), const), pl.BlockSpec((D_MODEL, 2 * LANES), const),
           pl.BlockSpec((1, LANES), const)]
    out_shape = [jax.ShapeDtypeStruct((n, D_MODEL), F32), jax.ShapeDtypeStruct((n, D_MODEL), BF16),
                 jax.ShapeDtypeStruct((n, LANES), F32)]
    out_specs = [tok, tok, small]
    if routed:
        t = jnp.arange(tm)
        ins.append((t[None, :] < t[:, None]).astype(BF16))
        specs.append(pl.BlockSpec((tm, tm), const))
        out_shape += [jax.ShapeDtypeStruct((n // tm, 1, tm), F32), jax.ShapeDtypeStruct((n, LANES), F32)]
        out_specs += [pl.BlockSpec((None, 1, tm), lambda i: (i, 0, 0)), small]
    if expand:
        ins.append(rep)
        specs.append(pl.BlockSpec(rep.shape, const))
    out_shape, out_specs = tuple(out_shape), tuple(out_specs)
    return pl.pallas_call(
        functools.partial(_post_kernel, mlstm, routed, expand),
        out_shape=out_shape,
        grid=(n // tm,),
        in_specs=specs,
        out_specs=out_specs,
        compiler_params=_params(1),
        name="post_mlstm" if mlstm else "post_attn",
    )(*ins)


ITEM_ROWS = 8
RUN_ALIGN = 16
RUN_SIZES = (512, 256, 128, 64, 32, 16)
RUN_SRC, RUN_DST, RUN_LEN, RUN_TOTAL = 0, N_GROUPS, 2 * N_GROUPS, 3 * N_GROUPS


def _stage_rows(tm):
    assert N_GROUPS * (RUN_ALIGN - 1) <= LANES
    return tm + LANES


def _sorted_rows(n, tm):
    rows = n + N_GROUPS * (n // tm) * RUN_ALIGN + tm
    return -(-rows // tm) * tm


def _sort_meta_kernel(tm, nts, gid_ref, q_ref, runs_ref, items_ref):
    ntiles = gid_ref.shape[0]
    gid = gid_ref[...]
    r = lax.broadcasted_iota(jnp.int32, (tm, tm), 0)
    c = lax.broadcasted_iota(jnp.int32, (tm, tm), 1)
    before = jnp.where(r < c, 1.0, 0.0).astype(BF16)
    trow = lax.broadcasted_iota(jnp.int32, (ntiles, LANES), 0)
    lane = lax.broadcasted_iota(jnp.int32, (ntiles, LANES), 1)
    k = lax.broadcasted_iota(jnp.int32, (1, LANES), 1).astype(F32)
    zero11 = jnp.zeros((1, 1), F32)
    q = jnp.zeros((ntiles, tm), F32)
    runs = jnp.zeros((ntiles, LANES), F32)
    src = jnp.zeros((ntiles, 1), F32)
    start, nitems = zero11, zero11
    grp = jnp.zeros((1, LANES), F32)
    tile = jnp.zeros((1, LANES), F32)
    valid = jnp.zeros((1, LANES), F32)
    for g in range(N_GROUPS):
        mask = jnp.where(gid == float(g), 1.0, 0.0)
        cnt = jnp.sum(mask, axis=1, keepdims=True)
        padded = jnp.floor((cnt + (RUN_ALIGN - 1.0)) * (1.0 / RUN_ALIGN)) * RUN_ALIGN
        incl = jnp.broadcast_to(padded, (ntiles, LANES))
        d = 1
        while d < ntiles:
            incl = incl + jnp.where(trow >= d, pltpu.roll(incl, d, axis=0), 0.0)
            d *= 2
        total = incl[ntiles - 1:ntiles, 0:1]
        dst = start + incl[:, 0:1] - padded
        q = q + mask * (src + _dot(mask.astype(BF16), before))
        runs = runs + jnp.where(lane == RUN_SRC + g, src, 0.0) + jnp.where(lane == RUN_DST + g, dst, 0.0) \
            + jnp.where(lane == RUN_LEN + g, padded, 0.0)
        src = src + padded
        end = start + total
        ft = jnp.floor(start * (1.0 / tm))
        lt = jnp.floor((end - 1.0) * (1.0 / tm))
        ni = jnp.where(total > 0.0, lt - ft + 1.0, 0.0)
        inside = (k >= nitems) & (k < nitems + ni)
        grp = grp + jnp.where(inside, float(g), 0.0)
        tile = tile + jnp.where(inside, ft + (k - nitems), 0.0)
        valid = valid + jnp.where(inside, 1.0, 0.0)
        nitems = nitems + ni
        start = end
    runs = runs + jnp.where(lane == RUN_TOTAL, start, 0.0)
    live = valid > 0.0
    prev = pltpu.roll(tile, 1, axis=1)
    nxt = pltpu.roll(tile, LANES - 1, axis=1)
    first = live & ((k == 0.0) | (tile != prev))
    last = live & ((k == nitems - 1.0) | (tile != nxt))
    spare = jnp.floor((start - 1.0) * (1.0 / tm)) + 1.0 + (k - nitems)
    fill = jnp.logical_not(live) & (spare <= nts - 1.0)
    tile = jnp.where(live, tile, jnp.minimum(spare, nts - 1.0))
    grp = jnp.where(live, grp, jnp.max(grp, axis=1, keepdims=True))
    flag = lambda m: jnp.where(m, 1.0, 0.0)
    q_ref[...] = q
    runs_ref[...] = runs.astype(jnp.int32)
    table = jnp.concatenate([tile, grp, valid, flag(first | fill), flag(last | fill),
                             jnp.zeros((ITEM_ROWS - 5, LANES), F32)], 0)
    items_ref[...] = table.astype(jnp.int32)


def _max_items(nts):
    return nts + N_GROUPS - 1


def _sort_meta_call(gid, tm):
    ntiles = gid.shape[0]
    nts = _sorted_rows(ntiles * tm, tm) // tm
    assert _max_items(nts) <= LANES and tm <= RUN_SIZES[0]
    return pl.pallas_call(
        functools.partial(_sort_meta_kernel, tm, nts),
        out_shape=(jax.ShapeDtypeStruct((ntiles, tm), F32),
                   jax.ShapeDtypeStruct((ntiles, LANES), jnp.int32),
                   jax.ShapeDtypeStruct((ITEM_ROWS, LANES), jnp.int32)),
        compiler_params=_params(0),
        name="moe_sort_meta",
    )(gid)


def _run_copies(runs_ref, i, tile_ref, sorted_hbm, sem, to_sorted):
    pieces = []
    for g in range(N_GROUPS):
        src = runs_ref[i, RUN_SRC + g]
        dst = runs_ref[i, RUN_DST + g]
        length = runs_ref[i, RUN_LEN + g]
        for s in RUN_SIZES:
            def build(src=src, dst=dst, length=length, s=s):
                off = length & (-2 * s)
                a = tile_ref.at[pl.ds(pl.multiple_of(src + off, RUN_ALIGN), s), :]
                b = sorted_hbm.at[pl.ds(pl.multiple_of(dst + off, RUN_ALIGN), s), :]
                return pltpu.make_async_copy(a, b, sem) if to_sorted else pltpu.make_async_copy(b, a, sem)
            pieces.append(((length & s) != 0, build))
    return pieces


def _start(pieces):
    for pred, build in pieces:
        pl.when(pred)(lambda build=build: build().start())


def _wait(pieces):
    for pred, build in pieces:
        pl.when(pred)(lambda build=build: build().wait())


def _start_then_wait(pieces):
    _start(pieces)
    _wait(pieces)


def _scatter_kernel(tm, nts, ntiles, runs_ref, q_ref, u_ref, g_ref, su_hbm, sg_hbm,
                    stu, stg, sem_u, sem_g):
    i = pl.program_id(0)
    slot = i % 2

    def copies(tile, slot):
        return (_run_copies(runs_ref, tile, stu.at[slot], su_hbm, sem_u.at[slot], True)
                + _run_copies(runs_ref, tile, stg.at[slot], sg_hbm, sem_g.at[slot], True))

    r = lax.broadcasted_iota(jnp.int32, (_stage_rows(tm), tm), 0).astype(F32)
    perm = jnp.where(q_ref[...] == r, 1.0, 0.0).astype(BF16)
    stu[slot] = _dot(perm, u_ref[...]).astype(BF16)
    both = _dot(perm, jnp.concatenate(_split2(g_ref[...]), 1))
    stg[slot] = both[:, 0:LANES] + both[:, LANES:2 * LANES]
    _start(copies(i, slot))

    @pl.when(i > 0)
    def _():
        _wait(copies(i - 1, 1 - slot))

    @pl.when(i == ntiles - 1)
    def _():
        _wait(copies(i, slot))
        stu[slot] = jnp.zeros(stu.shape[1:], stu.dtype)
        stg[slot] = jnp.zeros(stg.shape[1:], stg.dtype)
        total = runs_ref[i, RUN_TOTAL]
        tail = nts * tm - total
        nfull = tail // tm
        pieces = []
        for tile_ref, hbm, sem in ((stu.at[slot], su_hbm, sem_u.at[slot]),
                                   (stg.at[slot], sg_hbm, sem_g.at[slot])):
            for j in range(nts - ntiles):
                def full(j=j, tile_ref=tile_ref, hbm=hbm, sem=sem):
                    dst = pl.multiple_of(total + j * tm, RUN_ALIGN)
                    return pltpu.make_async_copy(tile_ref.at[pl.ds(0, tm), :], hbm.at[pl.ds(dst, tm), :], sem)
                pieces.append((j < nfull, full))
            rem = tail - nfull * tm
            for s in RUN_SIZES:
                if s >= tm:
                    continue
                def part(s=s, tile_ref=tile_ref, hbm=hbm, sem=sem):
                    dst = pl.multiple_of(total + nfull * tm + (rem & (-2 * s)), RUN_ALIGN)
                    return pltpu.make_async_copy(tile_ref.at[pl.ds(0, s), :], hbm.at[pl.ds(dst, s), :], sem)
                pieces.append(((rem & s) != 0, part))
        _start_then_wait(pieces)


def _scatter_call(runs, q3, u2, gate, tm):
    n = u2.shape[0]
    ns = _sorted_rows(n, tm)
    grid_spec = pltpu.PrefetchScalarGridSpec(
        num_scalar_prefetch=1,
        grid=(n // tm,),
        in_specs=[pl.BlockSpec((None, 1, tm), lambda i, *_: (i, 0, 0)),
                  pl.BlockSpec((tm, D_MODEL), lambda i, *_: (i, 0)),
                  pl.BlockSpec((tm, LANES), lambda i, *_: (i, 0))],
        out_specs=(pl.BlockSpec(memory_space=pl.ANY), pl.BlockSpec(memory_space=pl.ANY)),
        scratch_shapes=[pltpu.VMEM((2, _stage_rows(tm), D_MODEL), BF16),
                        pltpu.VMEM((2, _stage_rows(tm), LANES), F32),
                        pltpu.SemaphoreType.DMA((2,)), pltpu.SemaphoreType.DMA((2,))])
    return pl.pallas_call(
        functools.partial(_scatter_kernel, tm, ns // tm, n // tm),
        out_shape=(jax.ShapeDtypeStruct((ns, D_MODEL), BF16), jax.ShapeDtypeStruct((ns, LANES), F32)),
        grid_spec=grid_spec,
        compiler_params=_params(1),
        name="moe_scatter",
    )(runs, q3, u2, gate)


def _ungroup(tm, runs_ref, col_ref, ys_hbm, stage, sem):
    i = pl.program_id(0)
    slot = i % 2

    def copies(tile, slot):
        return _run_copies(runs_ref, tile, stage.at[slot], ys_hbm, sem.at[slot], False)

    @pl.when(i == 0)
    def _():
        stage[...] = jnp.zeros_like(stage)
        _start(copies(i, slot))

    @pl.when(i + 1 < pl.num_programs(0))
    def _():
        _start(copies(i + 1, 1 - slot))

    _wait(copies(i, slot))
    col = col_ref[...]
    gidx = col[:, 0:1]
    src = [runs_ref[i, RUN_SRC + g].astype(F32) for g in range(N_GROUPS)]
    first = src[N_GROUPS - 1]
    for g in range(N_GROUPS - 2, -1, -1):
        first = jnp.where(gidx == float(g), src[g], first)
    q = first + col[:, 1:2]
    lanes = lax.broadcasted_iota(jnp.int32, (tm, _stage_rows(tm)), 1).astype(F32)
    perm_t = jnp.where(q == lanes, 1.0, 0.0).astype(BF16)
    return _dot(perm_t, stage[slot])


def _moe_routed_kernel(items_ref, su_ref, sg_ref, wgu_ref, wd_ref, out_ref, acc_ref, wgu_b, wd_b):
    kk = pl.program_id(0)
    grp = items_ref[1, kk]

    @pl.when((kk == 0) | (grp != items_ref[1, jnp.maximum(kk - 1, 0)]))
    def _():
        for e in range(EXPERTS_PER_GROUP):
            wgu_b[e] = wgu_ref[e].astype(BF16)
            wd_b[e] = wd_ref[e].astype(BF16)

    @pl.when(items_ref[3, kk] == 1)
    def _():
        acc_ref[...] = jnp.zeros_like(acc_ref)

    @pl.when(items_ref[2, kk] == 1)
    def _():
        u = su_ref[...]
        gate = sg_ref[...]
        lane = lax.broadcasted_iota(jnp.int32, gate.shape, 1)
        first_lane = N_GROUPS + EXPERTS_PER_GROUP * grp
        acts = []
        for e in range(EXPERTS_PER_GROUP):
            gcol = jnp.sum(jnp.where(lane == first_lane + e, gate, 0.0), axis=-1, keepdims=True)
            hu = _dot(u, wgu_b[e])
            hg = hu[:, 0:D_EXPERT]
            acts.append((hg * jax.nn.sigmoid(hg) * hu[:, D_EXPERT:2 * D_EXPERT] * gcol).astype(BF16))
        wd_all = wd_b[...].reshape(EXPERTS_PER_GROUP * D_EXPERT, D_MODEL)
        acc_ref[...] += _dot(jnp.concatenate(acts, 1), wd_all)

    @pl.when(items_ref[4, kk] == 1)
    def _():
        out_ref[...] = acc_ref[...].astype(out_ref.dtype)


def _moe_routed_call(items, su, sg, wgu, wd, layer, tm):
    ns = su.shape[0]
    n_items = _max_items(ns // tm)
    depth = wgu.shape[0]
    wgu5 = wgu.reshape(depth, N_GROUPS, EXPERTS_PER_GROUP, D_MODEL, 2 * D_EXPERT)
    wd5 = wd.reshape(depth, N_GROUPS, EXPERTS_PER_GROUP, D_EXPERT, D_MODEL)
    grid_spec = pltpu.PrefetchScalarGridSpec(
        num_scalar_prefetch=1,
        grid=(n_items,),
        in_specs=[pl.BlockSpec((tm, D_MODEL), lambda k, it: (it[0, k], 0)),
                  pl.BlockSpec((tm, LANES), lambda k, it: (it[0, k], 0)),
                  pl.BlockSpec((None, None, EXPERTS_PER_GROUP, D_MODEL, 2 * D_EXPERT),
                               lambda k, it: (layer, it[1, k], 0, 0, 0)),
                  pl.BlockSpec((None, None, EXPERTS_PER_GROUP, D_EXPERT, D_MODEL),
                               lambda k, it: (layer, it[1, k], 0, 0, 0))],
        out_specs=pl.BlockSpec((tm, D_MODEL), lambda k, it: (it[0, k], 0)),
        scratch_shapes=[pltpu.VMEM((tm, D_MODEL), F32),
                        pltpu.VMEM((EXPERTS_PER_GROUP, D_MODEL, 2 * D_EXPERT), BF16),
                        pltpu.VMEM((EXPERTS_PER_GROUP, D_EXPERT, D_MODEL), BF16)])
    return pl.pallas_call(
        _moe_routed_kernel,
        out_shape=jax.ShapeDtypeStruct((ns, D_MODEL), BF16),
        grid_spec=grid_spec,
        compiler_params=_params(1),
        name="moe_routed",
    )(items, su, sg, wgu5, wd5)


def _resid_gather_kernel(tm, runs_ref, x_ref, g2_ref, col_ref, ys_hbm, y_ref, stage, sem):
    y_ref[...] = x_ref[...] + g2_ref[...] * _ungroup(tm, runs_ref, col_ref, ys_hbm, stage, sem)


def _resid_gather_call(x, runs, col, ys, mod, mspec, tm):
    n = x.shape[0]
    row = lambda i, *_: (i, 0)
    grid_spec = pltpu.PrefetchScalarGridSpec(
        num_scalar_prefetch=1,
        grid=(n // tm,),
        in_specs=[pl.BlockSpec((tm, D_MODEL), row), mspec(5), pl.BlockSpec((tm, LANES), row),
                  pl.BlockSpec(memory_space=pl.ANY)],
        out_specs=pl.BlockSpec((tm, D_MODEL), row),
        scratch_shapes=[pltpu.VMEM((2, _stage_rows(tm), D_MODEL), BF16), pltpu.SemaphoreType.DMA((2,))])
    return pl.pallas_call(
        functools.partial(_resid_gather_kernel, tm),
        out_shape=jax.ShapeDtypeStruct((n, D_MODEL), F32),
        grid_spec=grid_spec,
        compiler_params=_params(1),
        name="moe_unsort_resid",
    )(runs, x, mod, col, ys)


def _moe_kernel(expand, u_ref, gate_ref, wgu_ref, wd_ref, x1_ref, g2_ref, *rest):
    rep_ref = rest[0] if expand else None
    y_ref, acc_ref = rest[1 if expand else 0:]
    e = pl.program_id(1)

    @pl.when(e == 0)
    def _():
        acc_ref[...] = jnp.zeros_like(acc_ref)

    hu = _dot(u_ref[...], wgu_ref[...].astype(BF16))
    hg = hu[:, 0:D_EXPERT]
    hv = hu[:, D_EXPERT:2 * D_EXPERT]
    gate = gate_ref[...]
    lane = lax.broadcasted_iota(jnp.int32, gate.shape, 1)
    gcol = jnp.sum(jnp.where(lane == N_GROUPS + e, gate, 0.0), axis=-1, keepdims=True)
    act = hg * jax.nn.sigmoid(hg) * hv * gcol
    acc_ref[...] += _dot(act.astype(BF16), wd_ref[...].astype(BF16))

    @pl.when(e == N_EXPERTS - 1)
    def _():
        y_ref[...] = x1_ref[...] + _mod_rows(g2_ref, rep_ref) * acc_ref[...]


def _moe_call(u2, gate, wgu, wd, layer, x1, mod, mspec, rep, tm):
    n = u2.shape[0]
    row = lambda i, e: (i, 0)
    expand = rep is not None
    return pl.pallas_call(
        functools.partial(_moe_kernel, expand),
        out_shape=jax.ShapeDtypeStruct((n, D_MODEL), F32),
        grid=(n // tm, N_EXPERTS),
        in_specs=[pl.BlockSpec((tm, D_MODEL), row), pl.BlockSpec((tm, LANES), row),
                  pl.BlockSpec((None, None, D_MODEL, 2 * D_EXPERT), lambda i, e: (layer, e, 0, 0)),
                  pl.BlockSpec((None, None, D_EXPERT, D_MODEL), lambda i, e: (layer, e, 0, 0)),
                  pl.BlockSpec((tm, D_MODEL), row), mspec(5)]
        + ([pl.BlockSpec(rep.shape, lambda i, e: (0, 0))] if expand else []),
        out_specs=pl.BlockSpec((tm, D_MODEL), row),
        scratch_shapes=[pltpu.VMEM((tm, D_MODEL), F32)],
        compiler_params=_params(2),
        name="moe_dense",
    )(u2, gate, wgu, wd, x1, mod, *([rep] if expand else []))


def _rope128(x, cos, sa, sb):
    return x * cos + pltpu.roll(x, LANES - ROT_DIM // 2, axis=1) * sa \
        + pltpu.roll(x, ROT_DIM // 2, axis=1) * sb


def _kvq_kernel(gather_tm, *refs):
    if gather_tm:
        (runs_ref, x_ref, g2_ref, col_ref, ys_hbm, kvsh_ref, kvsc_ref, sh1_ref, sc1_ref, kvn_ref,
         nm_ref, wkv_ref, wq_ref, kn_ref, qn_ref, cos_ref, sa_ref, sb_ref, g64_ref, gr_ref, gb_ref,
         q_ref, k_ref, v_ref, xa_ref, stage, sem) = refs
        x = x_ref[...] + g2_ref[...] * _ungroup(gather_tm, runs_ref, col_ref, ys_hbm, stage, sem)
        xa_ref[...] = x
        rep_ref = None
    else:
        (x_ref, kvsh_ref, kvsc_ref, sh1_ref, sc1_ref, kvn_ref, nm_ref,
         wkv_ref, wq_ref, kn_ref, qn_ref, cos_ref, sa_ref, sb_ref, g64_ref, gr_ref, gb_ref,
         rep_ref, q_ref, k_ref, v_ref) = refs
        x = x_ref[...]
    xn = _unit_rms(x)
    cos, sa, sb = cos_ref[...], sa_ref[...], sb_ref[...]

    ukv = xn * (kvn_ref[...] * (1.0 + _mod_rows(kvsc_ref, rep_ref))) + _mod_rows(kvsh_ref, rep_ref)
    kv = _dot(ukv.astype(BF16), wkv_ref[...])
    k = kv[:, 0:LANES]
    v_ref[...] = kv[:, LANES:2 * LANES]
    kh, kl = _split2(k * k)
    ms = _dot(kh, g64_ref[...]) + _dot(kl, g64_ref[...])
    k_ref[...] = _rope128(k * lax.rsqrt(ms + RMS_EPS) * kn_ref[...], cos, sa, sb)

    u1 = xn * (nm_ref[...] * (1.0 + _mod_rows(sc1_ref, rep_ref))) + _mod_rows(sh1_ref, rep_ref)
    q = _dot(u1.astype(BF16), wq_ref[...])
    ms16 = _dot((q * q).astype(BF16), gr_ref[...])
    rh, rl = _split2(lax.rsqrt(ms16 + RMS_EPS))
    rsb = _dot(jnp.concatenate([rh, rl], 1), gb_ref[...])
    qn = q * rsb * qn_ref[...]
    scale = HEAD_DIM ** -0.5
    cos_q, sa_q, sb_q = cos * scale, sa * scale, sb * scale
    for i in range(D_MODEL // LANES):
        sl = slice(LANES * i, LANES * (i + 1))
        q_ref[:, sl] = _rope128(qn[:, sl], cos_q, sa_q, sb_q).astype(BF16)


def _kvq_call(x, routed, mod0, mod1, mspec, kvmod, kvspec, rep, kvn, nm, wkv, wq, kn, qn, tabs, tab,
              g64, gr, gb, tm):
    n = x.shape[0]
    row = lambda i, *_: (i, 0)
    const = lambda i, *_: (0, 0)
    vec = pl.BlockSpec((1, D_MODEL), const)
    tok = pl.BlockSpec((tm, D_MODEL), row)
    small = pl.BlockSpec((tm, LANES), row)
    ins = [x, kvmod, kvmod, mod1, mod1, kvn, nm, wkv, wq, kn, qn, *tabs, g64, gr, gb]
    specs = [tok, kvspec(0), kvspec(1), mspec(0), mspec(1), vec, vec,
             pl.BlockSpec((D_MODEL, 2 * LANES), const), pl.BlockSpec((D_MODEL, D_MODEL), const),
             pl.BlockSpec((1, LANES), const), vec, tab, tab, tab,
             pl.BlockSpec((LANES, LANES), const), pl.BlockSpec((D_MODEL, LANES), const),
             pl.BlockSpec((2 * LANES, D_MODEL), const)]
    out_shape = [jax.ShapeDtypeStruct((n, D_MODEL), BF16), jax.ShapeDtypeStruct((n, LANES), F32),
                 jax.ShapeDtypeStruct((n, LANES), F32)]
    out_specs = [tok, small, small]
    scratch = []
    nprefetch = 0
    if routed is not None:
        runs, col, ys = routed
        nprefetch = 1
        ins = [runs, x, mod0, col, ys] + ins[1:]
        specs = [tok, mspec(5), small, pl.BlockSpec(memory_space=pl.ANY)] + specs[1:]
        out_shape.append(jax.ShapeDtypeStruct((n, D_MODEL), F32))
        out_specs.append(tok)
        scratch = [pltpu.VMEM((2, _stage_rows(tm), D_MODEL), BF16), pltpu.SemaphoreType.DMA((2,))]
    else:
        ins.append(rep)
        specs.append(pl.BlockSpec(rep.shape, const))
    grid_spec = pltpu.PrefetchScalarGridSpec(
        num_scalar_prefetch=nprefetch, grid=(n // tm,), in_specs=specs,
        out_specs=tuple(out_specs), scratch_shapes=scratch)
    return pl.pallas_call(
        functools.partial(_kvq_kernel, tm if routed is not None else 0),
        out_shape=tuple(out_shape),
        grid_spec=grid_spec,
        compiler_params=_params(1),
        name="kv_q_proj",
    )(*ins)


def _attn_core(q, kcat, vcat, bias, sinks_ref, o_ref):
    tk = kcat.shape[0]
    pairs = N_Q_HEADS // N_KV_HEADS // 2
    lane = lax.broadcasted_iota(jnp.int32, (1, LANES), 1)
    lo = lane < HEAD_DIM
    kro = pltpu.roll(kcat, HEAD_DIM, axis=1)
    vro = pltpu.roll(vcat, HEAD_DIM, axis=1)
    one_e = jnp.broadcast_to(jnp.where(lo, 1.0, 0.0), (tk, LANES))
    one_o = 1.0 - one_e
    for g in range(N_KV_HEADS):
        if g == 0:
            ke, ko = jnp.where(lo, kcat, 0.0), jnp.where(lo, 0.0, kro)
            ve, vo = jnp.where(lo, vcat, 0.0), jnp.where(lo, 0.0, vro)
        else:
            ke, ko = jnp.where(lo, kro, 0.0), jnp.where(lo, 0.0, kcat)
            ve, vo = jnp.where(lo, vro, 0.0), jnp.where(lo, 0.0, vcat)
        k2 = jnp.concatenate([ke, ko], 0).astype(BF16)
        v2 = jnp.concatenate([jnp.concatenate([ve, one_e], 1),
                              jnp.concatenate([vo, one_o], 1)], 0).astype(BF16)
        for p in range(pairs):
            hp = g * pairs + p
            s = _dot_nt(q[:, LANES * hp:LANES * (hp + 1)], k2)
            halves, corr = [], []
            for par in range(2):
                sp = s[:, par * tk:(par + 1) * tk] + bias
                sink = sinks_ref[2 * hp + par]
                m = jnp.maximum(jnp.max(sp, axis=-1, keepdims=True), sink)
                halves.append(jnp.exp(sp - m).astype(BF16))
                corr.append(jnp.exp(sink - m))
            o2 = _dot(jnp.concatenate(halves, 1), v2)
            den = o2[:, LANES:2 * LANES] + jnp.where(lo, corr[0], corr[1])
            o_ref[:, LANES * hp:LANES * (hp + 1)] = (o2[:, 0:LANES] / den).astype(o_ref.dtype)


ATTN_BLOCKS_PER_STEP = 8


def _attn_prompt_kernel(sinks_ref, q_ref, kp_ref, kc_ref, vp_ref, vc_ref, bias_ref, o_ref):
    kall = jnp.concatenate([kp_ref[...], kc_ref[...]], 0)
    vall = jnp.concatenate([vp_ref[...], vc_ref[...]], 0)
    first = jnp.minimum(pl.program_id(1), 1)
    for j in range(ATTN_BLOCKS_PER_STEP):
        rows = slice(j * WINDOW, (j + 1) * WINDOW)
        keys = slice(j * WINDOW, (j + 2) * WINDOW)
        bias = bias_ref[first] if j == 0 else bias_ref[1]
        _attn_core(q_ref[rows, :], kall[keys, :], vall[keys, :], bias, sinks_ref, o_ref.at[rows, :])


ATTN_SEQS_PER_STEP = 8


def _attn_sample_kernel(sinks_ref, q_ref, kc_ref, kn_ref, vc_ref, vn_ref, bias_ref, o_ref, kbuf, vbuf):
    @pl.when(pl.program_id(0) == 0)
    def _():
        kbuf[...] = jnp.zeros_like(kbuf)
        vbuf[...] = jnp.zeros_like(vbuf)

    for b in range(ATTN_SEQS_PER_STEP):
        kbuf[b, 0:WINDOW, :] = kc_ref[b]
        kbuf[b, WINDOW:WINDOW + SAMPLE_PAD, :] = kn_ref[b]
        vbuf[b, 0:WINDOW, :] = vc_ref[b]
        vbuf[b, WINDOW:WINDOW + SAMPLE_PAD, :] = vn_ref[b]
    for b in range(ATTN_SEQS_PER_STEP):
        _attn_core(q_ref[b], kbuf[b], vbuf[b], bias_ref[...], sinks_ref, o_ref.at[b])


def _window_bias(tq, first):
    qi = jnp.arange(tq)[:, None]
    kj = jnp.arange(2 * WINDOW)[None, :]
    ok = (kj > qi) & (kj <= qi + WINDOW)
    if first:
        ok = ok & (kj >= WINDOW)
    return jnp.where(ok, 0.0, -jnp.inf).astype(F32)


def _attn_prompt_call(sinks, q, k, v, B, nb):
    n = q.shape[0]
    per = ATTN_BLOCKS_PER_STEP
    steps = nb // per
    cur = lambda b, i: (b * steps + i, 0)
    prev = lambda b, i: (b * nb + jnp.maximum(per * i - 1, 0), 0)
    bias = jnp.stack([_window_bias(WINDOW, True), _window_bias(WINDOW, False)])
    kv_prev = pl.BlockSpec((WINDOW, LANES), prev)
    kv_cur = pl.BlockSpec((per * WINDOW, LANES), cur)
    return pl.pallas_call(
        _attn_prompt_kernel,
        out_shape=jax.ShapeDtypeStruct((n, D_MODEL), BF16),
        grid=(B, steps),
        in_specs=[pl.BlockSpec(memory_space=pltpu.SMEM),
                  pl.BlockSpec((per * WINDOW, D_MODEL), cur), kv_prev, kv_cur, kv_prev, kv_cur,
                  pl.BlockSpec((2, WINDOW, 2 * WINDOW), lambda b, i: (0, 0, 0))],
        out_specs=pl.BlockSpec((per * WINDOW, D_MODEL), cur),
        compiler_params=_params(2),
        name="swa_prompt",
    )(sinks, q, k, k, v, v, bias)


def _attn_sample_call(sinks, q, kcache, knew, vcache, vnew):
    B = q.shape[0]
    nseq = ATTN_SEQS_PER_STEP
    b3 = lambda b: (b, 0, 0)
    cache = pl.BlockSpec((nseq, WINDOW, LANES), b3)
    new = pl.BlockSpec((nseq, SAMPLE_PAD, LANES), b3)
    return pl.pallas_call(
        _attn_sample_kernel,
        out_shape=jax.ShapeDtypeStruct(q.shape, BF16),
        grid=(B // nseq,),
        in_specs=[pl.BlockSpec(memory_space=pltpu.SMEM),
                  pl.BlockSpec((nseq, SAMPLE_PAD, D_MODEL), b3), cache, new, cache, new,
                  pl.BlockSpec((SAMPLE_PAD, 2 * WINDOW), lambda b: (0, 0))],
        out_specs=pl.BlockSpec((nseq, SAMPLE_PAD, D_MODEL), b3),
        scratch_shapes=[pltpu.VMEM((nseq, 2 * WINDOW, LANES), F32),
                        pltpu.VMEM((nseq, 2 * WINDOW, LANES), F32)],
        compiler_params=_params(1),
        name="swa_sample",
    )(sinks, q, kcache, knew, vcache, vnew, _window_bias(SAMPLE_PAD, False))


def _rope_tables(pos):
    half = ROT_DIM // 2
    inv = ROPE_THETA ** (-jnp.arange(half, dtype=F32) / half)
    ang = pos.astype(F32)[:, None] * inv[None]
    cos, sin = jnp.cos(ang), jnp.sin(ang)
    d = jnp.arange(LANES) % HEAD_DIM
    idx = d % half
    cos_t = jnp.where(d < ROT_DIM, cos[:, idx], 1.0)
    sa = jnp.where(d < half, -sin[:, idx], 0.0)
    sb = jnp.where((d >= half) & (d < ROT_DIM), sin[:, idx], 0.0)
    return cos_t, sa, sb


def _pad_lanes(a, value=0.0):
    return jnp.pad(a, ((0, 0), (0, LANES - a.shape[1])), constant_values=value)


def _prep_weights(ada_w, ada_b, norm_mix, norm_ffn, a_w_in, a_b_gates, a_head_norm, a_w_out,
                  kv_ada_w, kv_ada_b, kv_norm, w_k, w_v, k_norm, b_w_q, b_q_norm, b_sinks, b_w_o,
                  moe_w_group, moe_b_group, moe_w_expert, moe_b_expert, moe_w_gate_up, moe_w_down):
    w = {}
    g0 = QK_COLS + 2 * V_COLS
    w["w_in"] = a_w_in
    w["w_gates"] = jnp.concatenate([_pad_lanes(a_w_in[0, :, g0:g0 + M_HEADS]),
                                    _pad_lanes(a_w_in[0, :, g0 + M_HEADS:])], 1)
    w["bli"] = _pad_lanes(a_b_gates[0][None, :M_HEADS])
    w["blf"] = _pad_lanes(a_b_gates[0][None, M_HEADS:])
    w["head_norm"] = a_head_norm[0][None]
    w["w_out"] = a_w_out[0].astype(BF16)
    w["norm_mix"] = [norm_mix[l][None] for l in range(2)]
    w["norm_ffn"] = [norm_ffn[l][None] for l in range(2)]
    w["router"] = []
    for l in range(2):
        wr = _pad_lanes(jnp.concatenate([moe_w_group[l], moe_w_expert[l]], 1))
        hi = wr.astype(BF16)
        lo = (wr - hi.astype(F32)).astype(BF16)
        br = _pad_lanes(jnp.concatenate([moe_b_group[l], moe_b_expert[l]])[None])
        w["router"].append((jnp.concatenate([hi, lo], 1), br))
    w["w_gu"] = moe_w_gate_up
    w["w_d"] = moe_w_down
    w["kv_norm"] = kv_norm[None]
    w["w_kv"] = jnp.concatenate([w_k, w_v], 1).astype(BF16)
    w["k_norm"] = jnp.tile(k_norm, N_KV_HEADS)[None]
    w["w_q"] = b_w_q[0].astype(BF16)
    w["q_norm"] = jnp.tile(b_q_norm[0], N_Q_HEADS)[None]
    w["sinks"] = b_sinks[0]
    w["w_o"] = b_w_o[0].astype(BF16)
    lanes = jnp.arange(LANES)
    feat = jnp.arange(D_MODEL)
    w["g64"] = jnp.where((lanes[:, None] // HEAD_DIM) == (lanes[None, :] // HEAD_DIM),
                         1.0 / HEAD_DIM, 0.0).astype(BF16)
    w["gr"] = jnp.where((feat[:, None] // HEAD_DIM) == lanes[None, :], 1.0 / HEAD_DIM, 0.0).astype(BF16)
    gb = jnp.where(lanes[:, None] == (feat[None, :] // HEAD_DIM), 1.0, 0.0).astype(BF16)
    w["gb"] = jnp.concatenate([gb, gb], 0)
    return w


def _trunk(x2, mods, kvmod, mspec, mspec_big, rep, w, *, B, T, L, tm, tm_big, tabs, tabspec, c0, n0, m0,
           cache_k=None, cache_v=None):
    sample = cache_k is not None
    nc = T // L if not sample else 1

    qk, v, o, li, lf = _inproj_call(x2, mods[0], mspec_big, rep, w["norm_mix"][0], w["w_in"],
                                    w["w_gates"], w["bli"], w["blf"], tm_big)
    if sample:
        def padtok(a, value=0.0):
            a = a.reshape(B, T, a.shape[-1])
            a = jnp.pad(a, ((0, 0), (0, L - T), (0, 0)), constant_values=value)
            return a.reshape(B * L, a.shape[-1])
        qk, v, li, lf = padtok(qk), padtok(v), padtok(li, M_EMPTY), padtok(lf)
    h, c_new, n_new, m_new = _mlstm_call(qk, v, li, lf, c0, n0, m0, B, nc, L)
    if sample:
        h = h.reshape(B, L, V_COLS)[:, :T].reshape(B * T, V_COLS)
    routed = not sample
    ntiles = (B * T) // tm

    def routed_moe(u2, gate, gid, col, layer):
        q, runs, items = _sort_meta_call(gid.reshape(ntiles, tm), tm)
        su, sg = _scatter_call(runs, q.reshape(ntiles, 1, tm), u2, gate, tm)
        ys = _moe_routed_call(items, su, sg, w["w_gu"], w["w_d"], layer, tm)
        return runs, col, ys

    post0 = _post_call(True, routed, h, o, x2, mods[0], mspec, rep, w["norm_ffn"][0], w["head_norm"],
                       w["w_out"], *w["router"][0], tm)
    kvq_args = (mods[0], mods[1], mspec, kvmod, mspec, rep, w["kv_norm"], w["norm_mix"][1],
                w["w_kv"], w["w_q"], w["k_norm"], w["q_norm"], tabs, tabspec,
                w["g64"], w["gr"], w["gb"], tm)

    if routed:
        x1, u2, gate, gid, col = post0
        q, k, vv, xa = _kvq_call(x1, routed_moe(u2, gate, gid, col, 0), *kvq_args)
    else:
        x1, u2, gate = post0
        xa = _moe_call(u2, gate, w["w_gu"], w["w_d"], 0, x1, mods[0], mspec_big, rep, tm_big)
        q, k, vv = _kvq_call(xa, None, *kvq_args)
    if not sample:
        att = _attn_prompt_call(w["sinks"], q, k, vv, B, T // WINDOW)
        k_win = k.reshape(B, T, LANES)[:, T - WINDOW:].reshape(B, WINDOW, N_KV_HEADS, HEAD_DIM)
        v_win = vv.reshape(B, T, LANES)[:, T - WINDOW:].reshape(B, WINDOW, N_KV_HEADS, HEAD_DIM)
    else:
        def padseq(a):
            return jnp.pad(a.reshape(B, T, a.shape[-1]), ((0, 0), (0, SAMPLE_PAD - T), (0, 0)))
        kc = cache_k.reshape(B, WINDOW, LANES)
        vc = cache_v.reshape(B, WINDOW, LANES)
        att = _attn_sample_call(w["sinks"], padseq(q), kc, padseq(k), vc, padseq(vv))
        att = att[:, :T].reshape(B * T, D_MODEL)
        k_win = jnp.concatenate([kc[:, T:], k.reshape(B, T, LANES)], 1)
        v_win = jnp.concatenate([vc[:, T:], vv.reshape(B, T, LANES)], 1)
        k_win = k_win.reshape(B, WINDOW, N_KV_HEADS, HEAD_DIM)
        v_win = v_win.reshape(B, WINDOW, N_KV_HEADS, HEAD_DIM)
    post1 = _post_call(False, routed, att, None, xa, mods[1], mspec, rep, w["norm_ffn"][1], None,
                       w["w_o"], *w["router"][1], tm)
    if routed:
        x3, u4, gate, gid, col = post1
        runs, col, ys = routed_moe(u4, gate, gid, col, 1)
        y = _resid_gather_call(x3, runs, col, ys, mods[1], mspec, tm)
    else:
        x3, u4, gate = post1
        y = _moe_call(u4, gate, w["w_gu"], w["w_d"], 1, x3, mods[1], mspec_big, rep, tm_big)
    c_out = c_new[None]
    n_out = n_new.reshape(1, B, M_HEADS, M_DK)
    m_out = m_new[:, 0, :M_HEADS][None]
    return y, c_out, n_out, m_out, k_win, v_win


def kernel(x_prompt, x_sample, c_prompt, c_sample, state_c, state_n, state_m, cache_k_win, cache_v_win, ada_w, ada_b, norm_mix, norm_ffn, a_w_in, a_b_gates, a_head_norm, a_w_out, kv_ada_w, kv_ada_b, kv_norm, w_k, w_v, k_norm, b_w_q, b_q_norm, b_sinks, b_w_o, moe_w_group, moe_b_group, moe_w_expert, moe_b_expert, moe_w_gate_up, moe_w_down):
    Bp, Tp, D = x_prompt.shape
    Bs, Ts, _ = x_sample.shape
    w = _prep_weights(ada_w, ada_b, norm_mix, norm_ffn, a_w_in, a_b_gates, a_head_norm, a_w_out,
                      kv_ada_w, kv_ada_b, kv_norm, w_k, w_v, k_norm, b_w_q, b_q_norm, b_sinks, b_w_o,
                      moe_w_group, moe_b_group, moe_w_expert, moe_b_expert, moe_w_gate_up, moe_w_down)

    rows = Bp + Bs
    rpad = -rows % 8
    c_all = jnp.concatenate([c_prompt, c_sample, jnp.zeros((rpad, D), F32)], 0)
    mod = _ada_call(c_all, ada_w, ada_b[:, None, :])
    kvm = _ada_call(c_all, kv_ada_w[None], kv_ada_b[None, None, :])

    tm_p = TOKEN_TILE
    tiles_per_seq = Tp // tm_p
    mods_p = [mod[l, :Bp][:, None, :] for l in range(2)]
    kvmod_p = kvm[0, :Bp][:, None, :]

    def mspec_p(col):
        return pl.BlockSpec((None, 1, D_MODEL), lambda i, *_: (i // tiles_per_seq, 0, col))

    big_tiles_per_seq = Tp // BIG_TOKEN_TILE

    def mspec_big_p(col):
        return pl.BlockSpec((None, 1, D_MODEL), lambda i, *_: (i // big_tiles_per_seq, 0, col))

    tabs_p = _rope_tables(jnp.arange(Tp, dtype=jnp.int32))
    tabspec_p = pl.BlockSpec((tm_p, LANES), lambda i, *_: (i % tiles_per_seq, 0))
    npairs = M_HEADS // 2
    c0 = jnp.zeros((Bp, M_HEADS, M_DK, M_DV), F32)
    n0 = jnp.zeros((Bp, npairs, LANES), F32)
    m0 = jnp.pad(jnp.full((Bp, 1, M_HEADS), M_EMPTY, F32), ((0, 0), (0, 0), (0, LANES - M_HEADS)))
    yp, cp, np_, mp, kwp, vwp = _trunk(
        x_prompt.reshape(Bp * Tp, D), mods_p, kvmod_p, mspec_p, mspec_big_p, None, w,
        B=Bp, T=Tp, L=M_CHUNK, tm=tm_p, tm_big=BIG_TOKEN_TILE, tabs=tabs_p, tabspec=tabspec_p,
        c0=c0, n0=n0, m0=m0)

    ns = Bs * Ts
    mods_s = [mod[l, Bp:Bp + Bs][None] for l in range(2)]
    kvmod_s = kvm[0, Bp:Bp + Bs][None]
    rep = (jnp.arange(ns)[:, None] // Ts == jnp.arange(Bs)[None, :]).astype(BF16)

    def mspec_s(col):
        return pl.BlockSpec((None, Bs, D_MODEL), lambda i, *_: (0, 0, col))

    tabs_s = _rope_tables(PAST_LEN + jnp.arange(Ts, dtype=jnp.int32))
    tabs_s = tuple(jnp.tile(t, (Bs, 1)) for t in tabs_s)
    m0s = jnp.pad(state_m[0][:, None, :], ((0, 0), (0, 0), (0, LANES - M_HEADS)))
    ys, cs, ns_, ms, kws, vws = _trunk(
        x_sample.reshape(ns, D), mods_s, kvmod_s, mspec_s, mspec_s, rep, w,
        B=Bs, T=Ts, L=SAMPLE_PAD, tm=ns, tm_big=ns, tabs=tabs_s,
        tabspec=pl.BlockSpec((ns, LANES), lambda i, *_: (0, 0)),
        c0=state_c[0], n0=state_n[0].reshape(Bs, npairs, LANES), m0=m0s,
        cache_k=cache_k_win, cache_v=cache_v_win)

    return (yp.reshape(Bp, Tp, D), ys.reshape(Bs, Ts, D), cp, np_, mp, kwp, vwp,
            cs, ns_, ms, kws, vws)
```

```python
import functools

import jax
import jax.numpy as jnp
from jax import lax
from jax.experimental import pallas as pl
from jax.experimental.pallas import tpu as pltpu

F32 = jnp.float32
BF16 = jnp.bfloat16

D_MODEL = 1024
PAST_LEN = 8192
M_HEADS = 8
M_DK = 64
M_DV = 128
M_CHUNK = 128
GATE_SOFTCAP = 15.0
M_EMPTY = -1e30
WINDOW = 128
HEAD_DIM = 64
N_Q_HEADS = 16
N_KV_HEADS = 2
ROPE_THETA = 500000.0
ROT_DIM = 16
N_GROUPS = 4
EXPERTS_PER_GROUP = 4
N_EXPERTS = 16
D_EXPERT = 256
RMS_EPS = 1e-6

LANES = 128
QK_COLS = 2 * M_HEADS * M_DK
V_COLS = M_HEADS * M_DV
SAMPLE_PAD = 16
TOKEN_TILE = 512
BIG_TOKEN_TILE = 1024
VMEM_LIMIT = 52 * 1024 * 1024

NT_DIMS = (((1,), (1,)), ((), ()))


def _params(n_axes):
    return pltpu.CompilerParams(dimension_semantics=("arbitrary",) * n_axes,
                                vmem_limit_bytes=VMEM_LIMIT)


def _dot(a, b):
    return jnp.dot(a, b, preferred_element_type=F32)


def _dot_nt(a, b):
    return lax.dot_general(a, b, NT_DIMS, preferred_element_type=F32)


def _split2(x):
    hi = x.astype(BF16)
    lo = (x - hi.astype(F32)).astype(BF16)
    return hi, lo


def _split3(x):
    hi = x.astype(BF16)
    r = x - hi.astype(F32)
    mid = r.astype(BF16)
    lo = (r - mid.astype(F32)).astype(BF16)
    return hi, mid, lo


def _unit_rms(x):
    return x * lax.rsqrt(jnp.mean(x * x, axis=-1, keepdims=True) + RMS_EPS)


def _ada_kernel(c_ref, w_ref, b_ref, o_ref):
    c = c_ref[...]
    cs = (c * jax.nn.sigmoid(c)).astype(BF16)
    o_ref[...] = _dot(cs, w_ref[...].astype(BF16)) + b_ref[...]


def _ada_call(c, w, b):
    g, d, n = w.shape
    r = c.shape[0]
    tn = 1024
    return pl.pallas_call(
        _ada_kernel,
        out_shape=jax.ShapeDtypeStruct((g, r, n), F32),
        grid=(g, n // tn),
        in_specs=[pl.BlockSpec((r, d), lambda i, j: (0, 0)),
                  pl.BlockSpec((None, d, tn), lambda i, j: (i, 0, j)),
                  pl.BlockSpec((None, 1, tn), lambda i, j: (i, 0, j))],
        out_specs=pl.BlockSpec((None, r, tn), lambda i, j: (i, 0, j)),
        compiler_params=_params(2),
        name="ada_mod",
    )(c, w, b)


def _mod_rows(ref, rep_ref):
    if rep_ref is None:
        return ref[...]
    hi, mid, lo = _split3(ref[...])
    rep = rep_ref[...]
    return _dot(rep, hi) + _dot(rep, mid) + _dot(rep, lo)


def _inproj_kernel(expand, x_ref, sh_ref, sc_ref, nw_ref, w_ref, wg_ref, bli_ref, blf_ref, *rest):
    rep_ref = rest[0] if expand else None
    qk_ref, v_ref, o_ref, li_ref, lf_ref, wb_ref = rest[1 if expand else 0:]

    @pl.when(pl.program_id(0) == 0)
    def _():
        wb_ref[...] = w_ref[...].astype(BF16)

    u = _unit_rms(x_ref[...]) * (nw_ref[...] * (1.0 + _mod_rows(sc_ref, rep_ref))) \
        + _mod_rows(sh_ref, rep_ref)
    ub = u.astype(BF16)
    half = QK_COLS // 2
    q = _dot(ub, wb_ref[:, 0:half]) * (M_DK ** -0.5)
    qk_ref[:, 0:half] = q.astype(BF16)
    qk_ref[:, half:QK_COLS] = _dot(ub, wb_ref[:, half:QK_COLS]).astype(BF16)
    v_ref[...] = _dot(ub, wb_ref[:, QK_COLS:QK_COLS + V_COLS]).astype(BF16)
    o_ref[...] = jax.nn.sigmoid(_dot(ub, wb_ref[:, QK_COLS + V_COLS:QK_COLS + 2 * V_COLS])).astype(BF16)
    lane = lax.broadcasted_iota(jnp.int32, (1, LANES), 1)
    live = lane < M_HEADS
    gates = _dot(ub, wg_ref[...].astype(BF16))
    gi = gates[:, 0:LANES] + bli_ref[...]
    gf = gates[:, LANES:2 * LANES] + blf_ref[...]
    li = GATE_SOFTCAP * jnp.tanh(gi / GATE_SOFTCAP)
    fpre = GATE_SOFTCAP * jnp.tanh(gf / GATE_SOFTCAP)
    lf = jnp.minimum(fpre, 0.0) - jnp.log1p(jnp.exp(-jnp.abs(fpre)))
    li_ref[...] = jnp.where(live, li, 0.0)
    lf_ref[...] = jnp.where(live, lf, 0.0)


def _inproj_call(x, mod, mspec, rep, nw, w_in, wg, bli, blf, tm):
    n = x.shape[0]
    row = lambda i: (i, 0)
    const = lambda i: (0, 0)
    main_cols = QK_COLS + 2 * V_COLS
    expand = rep is not None
    return pl.pallas_call(
        functools.partial(_inproj_kernel, expand),
        out_shape=(jax.ShapeDtypeStruct((n, QK_COLS), BF16),
                   jax.ShapeDtypeStruct((n, V_COLS), BF16),
                   jax.ShapeDtypeStruct((n, V_COLS), BF16),
                   jax.ShapeDtypeStruct((n, LANES), F32),
                   jax.ShapeDtypeStruct((n, LANES), F32)),
        grid=(n // tm,),
        in_specs=[pl.BlockSpec((tm, D_MODEL), row), mspec(0), mspec(1),
                  pl.BlockSpec((1, D_MODEL), const),
                  pl.BlockSpec((None, D_MODEL, main_cols), lambda i: (0, 0, 0),
                               pipeline_mode=pl.Buffered(1)),
                  pl.BlockSpec((D_MODEL, 2 * LANES), const),
                  pl.BlockSpec((1, LANES), const), pl.BlockSpec((1, LANES), const)]
        + ([pl.BlockSpec(rep.shape, const)] if expand else []),
        out_specs=(pl.BlockSpec((tm, QK_COLS), row), pl.BlockSpec((tm, V_COLS), row),
                   pl.BlockSpec((tm, V_COLS), row), pl.BlockSpec((tm, LANES), row),
                   pl.BlockSpec((tm, LANES), row)),
        scratch_shapes=[pltpu.VMEM((D_MODEL, main_cols), BF16)],
        compiler_params=_params(1),
        name="mlstm_inproj",
    )(x, mod, mod, nw, w_in, wg, bli, blf, *([rep] if expand else []))


def _mlstm_kernel(L, nseq, scale_q, *refs):
    c = pl.program_id(1)
    seqs = [[r.at[b] for r in refs] for b in range(nseq)]

    @pl.when(c == 0)
    def _():
        for s in seqs:
            _mlstm_load_state(*s[4:7], *s[11:14])

    for s in seqs:
        _mlstm_seq_step(L, scale_q, *s[0:4], s[7], *s[11:14])

    @pl.when(c == pl.num_programs(1) - 1)
    def _():
        for s in seqs:
            _mlstm_store_state(*s[8:14])


def _mlstm_load_state(c0_ref, n0_ref, m0_ref, c2_s, n_s, m_s):
    m_s[...] = m0_ref[...]
    n_s[...] = n0_ref[...]
    z = jnp.zeros((M_DK, M_DV), F32)
    for j in range(M_HEADS // 2):
        c2_s[j] = jnp.concatenate([jnp.concatenate([c0_ref[2 * j], z], 1),
                                   jnp.concatenate([z, c0_ref[2 * j + 1]], 1)], 0)


def _mlstm_seq_step(L, scale_q, qk_ref, v_ref, li_ref, lf_ref, h_ref, c2_s, n_s, m_s):
    npairs = M_HEADS // 2
    hd = M_DK

    lane = lax.broadcasted_iota(jnp.int32, (1, LANES), 1)
    lo128 = lane < hd
    lane256 = lax.broadcasted_iota(jnp.int32, (1, 2 * LANES), 1)
    lo256 = lane256 < LANES
    row128 = lax.broadcasted_iota(jnp.int32, (LANES, 1), 0)
    top = row128 < hd
    blockdiag = (top & lo256) | (jnp.logical_not(top) & jnp.logical_not(lo256))

    LI = li_ref[...]
    LF = lf_ref[...]
    rowL = lax.broadcasted_iota(jnp.int32, (L, LANES), 0)

    def prefix(x, op, ident):
        d = 1
        while d < L:
            shifted = pltpu.roll(x, d, axis=0)
            x = op(x, jnp.where(rowL >= d, shifted, ident))
            d *= 2
        return x

    Bc = prefix(LF, jnp.add, 0.0)
    Cm = LI - Bc
    mprev = m_s[...]
    Gc = jnp.maximum(mprev, prefix(Cm, jnp.maximum, -jnp.inf))
    A = jnp.exp(mprev - Gc)
    bL = Bc[L - 1:L, :]
    DL = bL + Cm
    mnew = jnp.maximum(bL + mprev, jnp.max(DL, axis=0, keepdims=True))
    ast = jnp.exp(bL + mprev - mnew)
    WST = jnp.exp(DL - mnew)

    def pad_rows(x, rows):
        if x.shape[0] == rows:
            return x
        return jnp.concatenate([x, jnp.zeros((rows - x.shape[0], x.shape[1]), x.dtype)], 0)

    kw_lanes = max(L, hd)
    assert kw_lanes in (hd, LANES) and L <= LANES
    wide = kw_lanes == LANES
    lane_s = lax.broadcasted_iota(jnp.int32, (1, 2 * kw_lanes), 1)
    row_s = lax.broadcasted_iota(jnp.int32, (2 * kw_lanes, 1), 0)
    cm_pad = pad_rows(Cm, kw_lanes)
    if wide:
        XT = cm_pad.T
    else:
        XT = jnp.concatenate([cm_pad, cm_pad], 0).T
    causal = (lane_s & (kw_lanes - 1)) <= lax.broadcasted_iota(jnp.int32, (L, 2 * kw_lanes), 0)
    top_s = row_s < kw_lanes
    J = jnp.where((top_s & lo256) | (jnp.logical_not(top_s) & jnp.logical_not(lo256)),
                  1.0, 0.0).astype(BF16)

    EXPM = jnp.exp(-(Bc + Gc))

    def bc(X, h):
        return jnp.broadcast_to(X[:, h:h + 1], X.shape)

    for j in range(npairs):
        he, ho = 2 * j, 2 * j + 1
        p128 = slice(LANES * j, LANES * (j + 1))
        p256 = slice(2 * LANES * j, 2 * LANES * (j + 1))

        def pair128(X):
            return jnp.where(lo128, bc(X, he), bc(X, ho))

        def pair256(X):
            return jnp.concatenate([bc(X, he), bc(X, ho)], 1)

        q128 = qk_ref[:, LANES * j:LANES * (j + 1)]
        k128 = qk_ref[:, QK_COLS // 2 + LANES * j:QK_COLS // 2 + LANES * (j + 1)]
        v256 = v_ref[:, 2 * LANES * j:2 * LANES * (j + 1)]
        zk = jnp.zeros_like(k128)
        zv = jnp.zeros_like(v256)
        K2t = jnp.concatenate([pad_rows(jnp.where(lo128, k128, zk), kw_lanes),
                               pad_rows(jnp.where(lo128, zk, k128), kw_lanes)], 0)
        V2 = jnp.concatenate([pad_rows(jnp.where(lo256, v256, zv), kw_lanes),
                              pad_rows(jnp.where(lo256, zv, v256), kw_lanes)], 0)
        S = _dot_nt(q128, K2t)
        if wide:
            crow = jnp.concatenate([XT[he:he + 1, :], XT[ho:ho + 1, :]], 1)
            arg = jnp.where(causal, crow - pair256(Gc), -jnp.inf)
        else:
            crow = jnp.where(lo128, XT[he:he + 1, :], XT[ho:ho + 1, :])
            arg = jnp.where(causal, crow - pair128(Gc), -jnp.inf)
        Sw = (S * jnp.exp(arg)).astype(BF16)
        num_intra = _dot(Sw, V2)
        rowsum = _dot(Sw, J)
        C2 = c2_s[j]
        npair = n_s[j:j + 1, :]
        Nrow2 = jnp.concatenate(
            [jnp.broadcast_to(jnp.where(lo128, npair, 0.0), (LANES, LANES)),
             jnp.broadcast_to(jnp.where(lo128, 0.0, npair), (LANES, LANES))], 0).astype(BF16)
        if scale_q:
            qa = (q128.astype(F32) * pair128(A)).astype(BF16)
            num = _dot(qa, C2.astype(BF16)) + num_intra
            den = _dot_nt(qa, Nrow2) + rowsum
        else:
            a256 = pair256(A)
            num = a256 * _dot(q128, C2.astype(BF16)) + num_intra
            den = a256 * _dot_nt(q128, Nrow2) + rowsum
        h = num / jnp.maximum(jnp.abs(den), pair256(EXPM))
        h_ref[:, p256] = h.astype(h_ref.dtype)

        kw = k128.astype(F32) * pair128(WST)
        n_s[j:j + 1, :] = pair128(ast) * npair + jnp.sum(kw, axis=0, keepdims=True)
        kwT = pad_rows(kw, LANES).T.astype(BF16)
        dC = _dot(kwT, pad_rows(v256, LANES))
        c2_s[j] = pair256(ast) * C2 + jnp.where(blockdiag, dC, 0.0)

    m_s[...] = mnew


def _mlstm_store_state(cout_ref, nout_ref, mout_ref, c2_s, n_s, m_s):
    for j in range(M_HEADS // 2):
        C2 = c2_s[j]
        cout_ref[2 * j] = C2[0:M_DK, 0:M_DV]
        cout_ref[2 * j + 1] = C2[M_DK:2 * M_DK, M_DV:2 * M_DV]
    nout_ref[...] = n_s[...]
    mout_ref[...] = m_s[...]


MLSTM_SEQS_PER_STEP = 8


def _mlstm_call(qk, v, li, lf, c0, n0, m0, B, nc, L):
    npairs = M_HEADS // 2
    nseq = min(B, MLSTM_SEQS_PER_STEP)
    T = nc * L
    tok3 = lambda a: a.reshape(B, T, a.shape[-1])
    tok = lambda b, c: (b, c, 0)
    st4 = lambda b, c: (b, 0, 0, 0)
    st3 = lambda b, c: (b, 0, 0)
    h, c_new, n_new, m_new = pl.pallas_call(
        functools.partial(_mlstm_kernel, L, nseq, nc == 1),
        out_shape=(jax.ShapeDtypeStruct((B, T, V_COLS), BF16),
                   jax.ShapeDtypeStruct((B, M_HEADS, M_DK, M_DV), F32),
                   jax.ShapeDtypeStruct((B, npairs, LANES), F32),
                   jax.ShapeDtypeStruct((B, 1, LANES), F32)),
        grid=(B // nseq, nc),
        in_specs=[pl.BlockSpec((nseq, L, QK_COLS), tok), pl.BlockSpec((nseq, L, V_COLS), tok),
                  pl.BlockSpec((nseq, L, LANES), tok), pl.BlockSpec((nseq, L, LANES), tok),
                  pl.BlockSpec((nseq, M_HEADS, M_DK, M_DV), st4),
                  pl.BlockSpec((nseq, npairs, LANES), st3),
                  pl.BlockSpec((nseq, 1, LANES), st3)],
        out_specs=(pl.BlockSpec((nseq, L, V_COLS), tok),
                   pl.BlockSpec((nseq, M_HEADS, M_DK, M_DV), st4),
                   pl.BlockSpec((nseq, npairs, LANES), st3),
                   pl.BlockSpec((nseq, 1, LANES), st3)),
        scratch_shapes=[pltpu.VMEM((nseq, npairs, 2 * M_DK, 2 * M_DV), F32),
                        pltpu.VMEM((nseq, npairs, LANES), F32),
                        pltpu.VMEM((nseq, 1, LANES), F32)],
        compiler_params=_params(2),
        name="mlstm_chunks",
    )(tok3(qk), tok3(v), tok3(li), tok3(lf), c0, n0, m0)
    return h.reshape(B * T, V_COLS), c_new, n_new, m_new


def _route(lg):
    lane = lax.broadcasted_iota(jnp.int32, lg.shape, 1)
    lanef = lane.astype(F32)
    neg = -jnp.inf
    far = float(LANES)
    gm = lane < N_GROUPS
    lgm = jnp.where(gm, lg, neg)
    gmax = jnp.max(lgm, axis=-1, keepdims=True)
    gsum = jnp.sum(jnp.exp(lgm - gmax), axis=-1, keepdims=True)
    g_w = 1.0 / gsum
    gidx = jnp.min(jnp.where(gm & (lg == gmax), lanef, far), axis=-1, keepdims=True)
    first = N_GROUPS + EXPERTS_PER_GROUP * gidx
    sel = (lanef >= first) & (lanef < first + EXPERTS_PER_GROUP)
    l1 = jnp.max(jnp.where(sel, lg, neg), axis=-1, keepdims=True)
    i1 = jnp.min(jnp.where(sel & (lg == l1), lanef, far), axis=-1, keepdims=True)
    sel2 = sel & (lanef != i1)
    l2 = jnp.max(jnp.where(sel2, lg, neg), axis=-1, keepdims=True)
    i2 = jnp.min(jnp.where(sel2 & (lg == l2), lanef, far), axis=-1, keepdims=True)
    r = jnp.exp(l2 - l1)
    w1 = g_w / (1.0 + r)
    w2 = w1 * r
    return jnp.where(lanef == i1, w1, jnp.where(lanef == i2, w2, 0.0)), gidx


def _post_kernel(mlstm, routed, expand, *refs):
    refs = list(refs)
    h_ref = refs.pop(0)
    o_ref = refs.pop(0) if mlstm else None
    x_ref, g1_ref, sh2_ref, sc2_ref, nf_ref = refs[:5]
    refs = refs[5:]
    hn_ref = refs.pop(0) if mlstm else None
    wout_ref, wrhl_ref, br_ref = refs[:3]
    refs = refs[3:]
    tri_ref = refs.pop(0) if routed else None
    rep_ref = refs.pop(0) if expand else None
    x1_ref, u2_ref, gate_ref = refs[:3]
    outs = refs[3:]
    if mlstm:
        hf = h_ref[...].astype(F32)
        parts = [_unit_rms(hf[:, M_DV * i:M_DV * (i + 1)]) for i in range(M_HEADS)]
        hn = jnp.concatenate(parts, 1) * hn_ref[...]
        hg = (hn * o_ref[...].astype(F32)).astype(BF16)
    else:
        hg = h_ref[...]
    x1 = x_ref[...] + _mod_rows(g1_ref, rep_ref) * _dot(hg, wout_ref[...])
    x1_ref[...] = x1
    u2 = _unit_rms(x1) * (nf_ref[...] * (1.0 + _mod_rows(sc2_ref, rep_ref))) + _mod_rows(sh2_ref, rep_ref)
    uh, ul = _split2(u2)
    wrh = wrhl_ref[:, 0:LANES]
    lg = _dot(uh, wrh) + _dot(ul, wrh) + _dot(uh, wrhl_ref[:, LANES:2 * LANES]) + br_ref[...]
    gate, gidx = _route(lg)
    u2_ref[...] = uh
    gate_ref[...] = gate
    if not routed:
        return
    gid_ref, col_ref = outs
    rows = []
    for blk in range(x1.shape[0] // LANES):
        col = jnp.broadcast_to(gidx[LANES * blk:LANES * (blk + 1), :], (LANES, LANES))
        rows.append(col.T[0:1, :])
    gid_ref[...] = jnp.concatenate(rows, 1)
    lanef = lax.broadcasted_iota(jnp.int32, gate.shape, 1).astype(F32)
    onehot = jnp.where(lanef == gidx, 1.0, 0.0)
    earlier = _dot(tri_ref[...], onehot.astype(BF16))
    rank = jnp.sum(onehot * earlier, axis=-1, keepdims=True)
    col_ref[...] = jnp.where(lanef == 0.0, gidx, jnp.where(lanef == 1.0, rank, 0.0))


def _post_call(mlstm, routed, h, o, x, mod, mspec, rep, nf, hn, wout, wrhl, br, tm):
    n = x.shape[0]
    expand = rep is not None
    row = lambda i: (i, 0)
    const = lambda i: (0, 0)
    tok = pl.BlockSpec((tm, D_MODEL), row)
    small = pl.BlockSpec((tm, LANES), row)
    vec = pl.BlockSpec((1, D_MODEL), const)
    ins = [h] + ([o] if mlstm else []) + [x, mod, mod, mod, nf] + ([hn] if mlstm else []) \
        + [wout, wrhl, br]
    specs = [tok] + ([tok] if mlstm else []) + [tok, mspec(2), mspec(3), mspec(4), vec] \
        + ([vec] if mlstm else []) \
        + [pl.BlockSpec((D_MODEL, D_MODEL), const), pl.BlockSpec((D_MODEL, 2 * LANES), const),
           pl.BlockSpec((1, LANES), const)]
    out_shape = [jax.ShapeDtypeStruct((n, D_MODEL), F32), jax.ShapeDtypeStruct((n, D_MODEL), BF16),
                 jax.ShapeDtypeStruct((n, LANES), F32)]
    out_specs = [tok, tok, small]
    if routed:
        t = jnp.arange(tm)
        ins.append((t[None, :] < t[:, None]).astype(BF16))
        specs.append(pl.BlockSpec((tm, tm), const))
        out_shape += [jax.ShapeDtypeStruct((n // tm, 1, tm), F32), jax.ShapeDtypeStruct((n, LANES), F32)]
        out_specs += [pl.BlockSpec((None, 1, tm), lambda i: (i, 0, 0)), small]
    if expand:
        ins.append(rep)
        specs.append(pl.BlockSpec(rep.shape, const))
    out_shape, out_specs = tuple(out_shape), tuple(out_specs)
    return pl.pallas_call(
        functools.partial(_post_kernel, mlstm, routed, expand),
        out_shape=out_shape,
        grid=(n // tm,),
        in_specs=specs,
        out_specs=out_specs,
        compiler_params=_params(1),
        name="post_mlstm" if mlstm else "post_attn",
    )(*ins)


ITEM_ROWS = 8
RUN_ALIGN = 16
RUN_SIZES = (512, 256, 128, 64, 32, 16)
RUN_SRC, RUN_DST, RUN_LEN, RUN_TOTAL = 0, N_GROUPS, 2 * N_GROUPS, 3 * N_GROUPS


def _stage_rows(tm):
    assert N_GROUPS * (RUN_ALIGN - 1) <= LANES
    return tm + LANES


def _sorted_rows(n, tm):
    rows = n + N_GROUPS * (n // tm) * RUN_ALIGN + tm
    return -(-rows // tm) * tm


def _sort_meta_kernel(tm, nts, gid_ref, q_ref, runs_ref, items_ref):
    ntiles = gid_ref.shape[0]
    gid = gid_ref[...]
    r = lax.broadcasted_iota(jnp.int32, (tm, tm), 0)
    c = lax.broadcasted_iota(jnp.int32, (tm, tm), 1)
    before = jnp.where(r < c, 1.0, 0.0).astype(BF16)
    trow = lax.broadcasted_iota(jnp.int32, (ntiles, LANES), 0)
    lane = lax.broadcasted_iota(jnp.int32, (ntiles, LANES), 1)
    k = lax.broadcasted_iota(jnp.int32, (1, LANES), 1).astype(F32)
    zero11 = jnp.zeros((1, 1), F32)
    q = jnp.zeros((ntiles, tm), F32)
    runs = jnp.zeros((ntiles, LANES), F32)
    src = jnp.zeros((ntiles, 1), F32)
    start, nitems = zero11, zero11
    grp = jnp.zeros((1, LANES), F32)
    tile = jnp.zeros((1, LANES), F32)
    valid = jnp.zeros((1, LANES), F32)
    for g in range(N_GROUPS):
        mask = jnp.where(gid == float(g), 1.0, 0.0)
        cnt = jnp.sum(mask, axis=1, keepdims=True)
        padded = jnp.floor((cnt + (RUN_ALIGN - 1.0)) * (1.0 / RUN_ALIGN)) * RUN_ALIGN
        incl = jnp.broadcast_to(padded, (ntiles, LANES))
        d = 1
        while d < ntiles:
            incl = incl + jnp.where(trow >= d, pltpu.roll(incl, d, axis=0), 0.0)
            d *= 2
        total = incl[ntiles - 1:ntiles, 0:1]
        dst = start + incl[:, 0:1] - padded
        q = q + mask * (src + _dot(mask.astype(BF16), before))
        runs = runs + jnp.where(lane == RUN_SRC + g, src, 0.0) + jnp.where(lane == RUN_DST + g, dst, 0.0) \
            + jnp.where(lane == RUN_LEN + g, padded, 0.0)
        src = src + padded
        end = start + total
        ft = jnp.floor(start * (1.0 / tm))
        lt = jnp.floor((end - 1.0) * (1.0 / tm))
        ni = jnp.where(total > 0.0, lt - ft + 1.0, 0.0)
        inside = (k >= nitems) & (k < nitems + ni)
        grp = grp + jnp.where(inside, float(g), 0.0)
        tile = tile + jnp.where(inside, ft + (k - nitems), 0.0)
        valid = valid + jnp.where(inside, 1.0, 0.0)
        nitems = nitems + ni
        start = end
    runs = runs + jnp.where(lane == RUN_TOTAL, start, 0.0)
    live = valid > 0.0
    prev = pltpu.roll(tile, 1, axis=1)
    nxt = pltpu.roll(tile, LANES - 1, axis=1)
    first = live & ((k == 0.0) | (tile != prev))
    last = live & ((k == nitems - 1.0) | (tile != nxt))
    spare = jnp.floor((start - 1.0) * (1.0 / tm)) + 1.0 + (k - nitems)
    fill = jnp.logical_not(live) & (spare <= nts - 1.0)
    tile = jnp.where(live, tile, jnp.minimum(spare, nts - 1.0))
    grp = jnp.where(live, grp, jnp.max(grp, axis=1, keepdims=True))
    flag = lambda m: jnp.where(m, 1.0, 0.0)
    q_ref[...] = q
    runs_ref[...] = runs.astype(jnp.int32)
    table = jnp.concatenate([tile, grp, valid, flag(first | fill), flag(last | fill),
                             jnp.zeros((ITEM_ROWS - 5, LANES), F32)], 0)
    items_ref[...] = table.astype(jnp.int32)


def _max_items(nts):
    return nts + N_GROUPS - 1


def _sort_meta_call(gid, tm):
    ntiles = gid.shape[0]
    nts = _sorted_rows(ntiles * tm, tm) // tm
    assert _max_items(nts) <= LANES and tm <= RUN_SIZES[0]
    return pl.pallas_call(
        functools.partial(_sort_meta_kernel, tm, nts),
        out_shape=(jax.ShapeDtypeStruct((ntiles, tm), F32),
                   jax.ShapeDtypeStruct((ntiles, LANES), jnp.int32),
                   jax.ShapeDtypeStruct((ITEM_ROWS, LANES), jnp.int32)),
        compiler_params=_params(0),
        name="moe_sort_meta",
    )(gid)


def _run_copies(runs_ref, i, tile_ref, sorted_hbm, sem, to_sorted):
    pieces = []
    for g in range(N_GROUPS):
        src = runs_ref[i, RUN_SRC + g]
        dst = runs_ref[i, RUN_DST + g]
        length = runs_ref[i, RUN_LEN + g]
        for s in RUN_SIZES:
            def build(src=src, dst=dst, length=length, s=s):
                off = length & (-2 * s)
                a = tile_ref.at[pl.ds(pl.multiple_of(src + off, RUN_ALIGN), s), :]
                b = sorted_hbm.at[pl.ds(pl.multiple_of(dst + off, RUN_ALIGN), s), :]
                return pltpu.make_async_copy(a, b, sem) if to_sorted else pltpu.make_async_copy(b, a, sem)
            pieces.append(((length & s) != 0, build))
    return pieces


def _start(pieces):
    for pred, build in pieces:
        pl.when(pred)(lambda build=build: build().start())


def _wait(pieces):
    for pred, build in pieces:
        pl.when(pred)(lambda build=build: build().wait())


def _start_then_wait(pieces):
    _start(pieces)
    _wait(pieces)


def _scatter_kernel(tm, nts, ntiles, runs_ref, q_ref, u_ref, g_ref, su_hbm, sg_hbm,
                    stu, stg, sem_u, sem_g):
    i = pl.program_id(0)
    slot = i % 2

    def copies(tile, slot):
        return (_run_copies(runs_ref, tile, stu.at[slot], su_hbm, sem_u.at[slot], True)
                + _run_copies(runs_ref, tile, stg.at[slot], sg_hbm, sem_g.at[slot], True))

    r = lax.broadcasted_iota(jnp.int32, (_stage_rows(tm), tm), 0).astype(F32)
    perm = jnp.where(q_ref[...] == r, 1.0, 0.0).astype(BF16)
    stu[slot] = _dot(perm, u_ref[...]).astype(BF16)
    both = _dot(perm, jnp.concatenate(_split2(g_ref[...]), 1))
    stg[slot] = both[:, 0:LANES] + both[:, LANES:2 * LANES]
    _start(copies(i, slot))

    @pl.when(i > 0)
    def _():
        _wait(copies(i - 1, 1 - slot))

    @pl.when(i == ntiles - 1)
    def _():
        _wait(copies(i, slot))
        stu[slot] = jnp.zeros(stu.shape[1:], stu.dtype)
        stg[slot] = jnp.zeros(stg.shape[1:], stg.dtype)
        total = runs_ref[i, RUN_TOTAL]
        tail = nts * tm - total
        nfull = tail // tm
        pieces = []
        for tile_ref, hbm, sem in ((stu.at[slot], su_hbm, sem_u.at[slot]),
                                   (stg.at[slot], sg_hbm, sem_g.at[slot])):
            for j in range(nts - ntiles):
                def full(j=j, tile_ref=tile_ref, hbm=hbm, sem=sem):
                    dst = pl.multiple_of(total + j * tm, RUN_ALIGN)
                    return pltpu.make_async_copy(tile_ref.at[pl.ds(0, tm), :], hbm.at[pl.ds(dst, tm), :], sem)
                pieces.append((j < nfull, full))
            rem = tail - nfull * tm
            for s in RUN_SIZES:
                if s >= tm:
                    continue
                def part(s=s, tile_ref=tile_ref, hbm=hbm, sem=sem):
                    dst = pl.multiple_of(total + nfull * tm + (rem & (-2 * s)), RUN_ALIGN)
                    return pltpu.make_async_copy(tile_ref.at[pl.ds(0, s), :], hbm.at[pl.ds(dst, s), :], sem)
                pieces.append(((rem & s) != 0, part))
        _start_then_wait(pieces)


def _scatter_call(runs, q3, u2, gate, tm):
    n = u2.shape[0]
    ns = _sorted_rows(n, tm)
    grid_spec = pltpu.PrefetchScalarGridSpec(
        num_scalar_prefetch=1,
        grid=(n // tm,),
        in_specs=[pl.BlockSpec((None, 1, tm), lambda i, *_: (i, 0, 0)),
                  pl.BlockSpec((tm, D_MODEL), lambda i, *_: (i, 0)),
                  pl.BlockSpec((tm, LANES), lambda i, *_: (i, 0))],
        out_specs=(pl.BlockSpec(memory_space=pl.ANY), pl.BlockSpec(memory_space=pl.ANY)),
        scratch_shapes=[pltpu.VMEM((2, _stage_rows(tm), D_MODEL), BF16),
                        pltpu.VMEM((2, _stage_rows(tm), LANES), F32),
                        pltpu.SemaphoreType.DMA((2,)), pltpu.SemaphoreType.DMA((2,))])
    return pl.pallas_call(
        functools.partial(_scatter_kernel, tm, ns // tm, n // tm),
        out_shape=(jax.ShapeDtypeStruct((ns, D_MODEL), BF16), jax.ShapeDtypeStruct((ns, LANES), F32)),
        grid_spec=grid_spec,
        compiler_params=_params(1),
        name="moe_scatter",
    )(runs, q3, u2, gate)


def _ungroup(tm, runs_ref, col_ref, ys_hbm, stage, sem):
    i = pl.program_id(0)
    slot = i % 2

    def copies(tile, slot):
        return _run_copies(runs_ref, tile, stage.at[slot], ys_hbm, sem.at[slot], False)

    @pl.when(i == 0)
    def _():
        stage[...] = jnp.zeros_like(stage)
        _start(copies(i, slot))

    @pl.when(i + 1 < pl.num_programs(0))
    def _():
        _start(copies(i + 1, 1 - slot))

    _wait(copies(i, slot))
    col = col_ref[...]
    gidx = col[:, 0:1]
    src = [runs_ref[i, RUN_SRC + g].astype(F32) for g in range(N_GROUPS)]
    first = src[N_GROUPS - 1]
    for g in range(N_GROUPS - 2, -1, -1):
        first = jnp.where(gidx == float(g), src[g], first)
    q = first + col[:, 1:2]
    lanes = lax.broadcasted_iota(jnp.int32, (tm, _stage_rows(tm)), 1).astype(F32)
    perm_t = jnp.where(q == lanes, 1.0, 0.0).astype(BF16)
    return _dot(perm_t, stage[slot])


def _moe_routed_kernel(items_ref, su_ref, sg_ref, wgu_ref, wd_ref, out_ref, acc_ref, wgu_b, wd_b):
    kk = pl.program_id(0)
    grp = items_ref[1, kk]

    @pl.when((kk == 0) | (grp != items_ref[1, jnp.maximum(kk - 1, 0)]))
    def _():
        for e in range(EXPERTS_PER_GROUP):
            wgu_b[e] = wgu_ref[e].astype(BF16)
            wd_b[e] = wd_ref[e].astype(BF16)

    @pl.when(items_ref[3, kk] == 1)
    def _():
        acc_ref[...] = jnp.zeros_like(acc_ref)

    @pl.when(items_ref[2, kk] == 1)
    def _():
        u = su_ref[...]
        gate = sg_ref[...]
        lane = lax.broadcasted_iota(jnp.int32, gate.shape, 1)
        first_lane = N_GROUPS + EXPERTS_PER_GROUP * grp
        acts = []
        for e in range(EXPERTS_PER_GROUP):
            gcol = jnp.sum(jnp.where(lane == first_lane + e, gate, 0.0), axis=-1, keepdims=True)
            hu = _dot(u, wgu_b[e])
            hg = hu[:, 0:D_EXPERT]
            acts.append((hg * jax.nn.sigmoid(hg) * hu[:, D_EXPERT:2 * D_EXPERT] * gcol).astype(BF16))
        wd_all = wd_b[...].reshape(EXPERTS_PER_GROUP * D_EXPERT, D_MODEL)
        acc_ref[...] += _dot(jnp.concatenate(acts, 1), wd_all)

    @pl.when(items_ref[4, kk] == 1)
    def _():
        out_ref[...] = acc_ref[...].astype(out_ref.dtype)


def _moe_routed_call(items, su, sg, wgu, wd, layer, tm):
    ns = su.shape[0]
    n_items = _max_items(ns // tm)
    depth = wgu.shape[0]
    wgu5 = wgu.reshape(depth, N_GROUPS, EXPERTS_PER_GROUP, D_MODEL, 2 * D_EXPERT)
    wd5 = wd.reshape(depth, N_GROUPS, EXPERTS_PER_GROUP, D_EXPERT, D_MODEL)
    grid_spec = pltpu.PrefetchScalarGridSpec(
        num_scalar_prefetch=1,
        grid=(n_items,),
        in_specs=[pl.BlockSpec((tm, D_MODEL), lambda k, it: (it[0, k], 0)),
                  pl.BlockSpec((tm, LANES), lambda k, it: (it[0, k], 0)),
                  pl.BlockSpec((None, None, EXPERTS_PER_GROUP, D_MODEL, 2 * D_EXPERT),
                               lambda k, it: (layer, it[1, k], 0, 0, 0)),
                  pl.BlockSpec((None, None, EXPERTS_PER_GROUP, D_EXPERT, D_MODEL),
                               lambda k, it: (layer, it[1, k], 0, 0, 0))],
        out_specs=pl.BlockSpec((tm, D_MODEL), lambda k, it: (it[0, k], 0)),
        scratch_shapes=[pltpu.VMEM((tm, D_MODEL), F32),
                        pltpu.VMEM((EXPERTS_PER_GROUP, D_MODEL, 2 * D_EXPERT), BF16),
                        pltpu.VMEM((EXPERTS_PER_GROUP, D_EXPERT, D_MODEL), BF16)])
    return pl.pallas_call(
        _moe_routed_kernel,
        out_shape=jax.ShapeDtypeStruct((ns, D_MODEL), BF16),
        grid_spec=grid_spec,
        compiler_params=_params(1),
        name="moe_routed",
    )(items, su, sg, wgu5, wd5)


def _resid_gather_kernel(tm, runs_ref, x_ref, g2_ref, col_ref, ys_hbm, y_ref, stage, sem):
    y_ref[...] = x_ref[...] + g2_ref[...] * _ungroup(tm, runs_ref, col_ref, ys_hbm, stage, sem)


def _resid_gather_call(x, runs, col, ys, mod, mspec, tm):
    n = x.shape[0]
    row = lambda i, *_: (i, 0)
    grid_spec = pltpu.PrefetchScalarGridSpec(
        num_scalar_prefetch=1,
        grid=(n // tm,),
        in_specs=[pl.BlockSpec((tm, D_MODEL), row), mspec(5), pl.BlockSpec((tm, LANES), row),
                  pl.BlockSpec(memory_space=pl.ANY)],
        out_specs=pl.BlockSpec((tm, D_MODEL), row),
        scratch_shapes=[pltpu.VMEM((2, _stage_rows(tm), D_MODEL), BF16), pltpu.SemaphoreType.DMA((2,))])
    return pl.pallas_call(
        functools.partial(_resid_gather_kernel, tm),
        out_shape=jax.ShapeDtypeStruct((n, D_MODEL), F32),
        grid_spec=grid_spec,
        compiler_params=_params(1),
        name="moe_unsort_resid",
    )(runs, x, mod, col, ys)


def _moe_kernel(expand, u_ref, gate_ref, wgu_ref, wd_ref, x1_ref, g2_ref, *rest):
    rep_ref = rest[0] if expand else None
    y_ref, acc_ref = rest[1 if expand else 0:]
    grp = pl.program_id(1)

    @pl.when(grp == 0)
    def _():
        acc_ref[...] = jnp.zeros_like(acc_ref)

    u = u_ref[...]
    gate = gate_ref[...]
    lane = lax.broadcasted_iota(jnp.int32, gate.shape, 1)
    first_lane = N_GROUPS + EXPERTS_PER_GROUP * grp
    acts = []
    for e in range(EXPERTS_PER_GROUP):
        hu = _dot(u, wgu_ref[e].astype(BF16))
        hg = hu[:, 0:D_EXPERT]
        gcol = jnp.sum(jnp.where(lane == first_lane + e, gate, 0.0), axis=-1, keepdims=True)
        acts.append((hg * jax.nn.sigmoid(hg) * hu[:, D_EXPERT:2 * D_EXPERT] * gcol).astype(BF16))
    wd_all = wd_ref[...].astype(BF16).reshape(EXPERTS_PER_GROUP * D_EXPERT, D_MODEL)
    acc_ref[...] += _dot(jnp.concatenate(acts, 1), wd_all)

    @pl.when(grp == N_GROUPS - 1)
    def _():
        y_ref[...] = x1_ref[...] + _mod_rows(g2_ref, rep_ref) * acc_ref[...]


def _moe_call(u2, gate, wgu, wd, layer, x1, mod, mspec, rep, tm):
    n = u2.shape[0]
    row = lambda i, e: (i, 0)
    expand = rep is not None
    depth = wgu.shape[0]
    wgu5 = wgu.reshape(depth, N_GROUPS, EXPERTS_PER_GROUP, D_MODEL, 2 * D_EXPERT)
    wd5 = wd.reshape(depth, N_GROUPS, EXPERTS_PER_GROUP, D_EXPERT, D_MODEL)
    return pl.pallas_call(
        functools.partial(_moe_kernel, expand),
        out_shape=jax.ShapeDtypeStruct((n, D_MODEL), F32),
        grid=(n // tm, N_GROUPS),
        in_specs=[pl.BlockSpec((tm, D_MODEL), row), pl.BlockSpec((tm, LANES), row),
                  pl.BlockSpec((None, None, EXPERTS_PER_GROUP, D_MODEL, 2 * D_EXPERT),
                               lambda i, g: (layer, g, 0, 0, 0)),
                  pl.BlockSpec((None, None, EXPERTS_PER_GROUP, D_EXPERT, D_MODEL),
                               lambda i, g: (layer, g, 0, 0, 0)),
                  pl.BlockSpec((tm, D_MODEL), row), mspec(5)]
        + ([pl.BlockSpec(rep.shape, lambda i, e: (0, 0))] if expand else []),
        out_specs=pl.BlockSpec((tm, D_MODEL), row),
        scratch_shapes=[pltpu.VMEM((tm, D_MODEL), F32)],
        compiler_params=_params(2),
        name="moe_dense",
    )(u2, gate, wgu5, wd5, x1, mod, *([rep] if expand else []))


def _rope128(x, cos, sa, sb):
    return x * cos + pltpu.roll(x, LANES - ROT_DIM // 2, axis=1) * sa \
        + pltpu.roll(x, ROT_DIM // 2, axis=1) * sb


def _kvq_kernel(gather_tm, *refs):
    if gather_tm:
        (runs_ref, x_ref, g2_ref, col_ref, ys_hbm, kvsh_ref, kvsc_ref, sh1_ref, sc1_ref, kvn_ref,
         nm_ref, wkv_ref, wq_ref, kn_ref, qn_ref, cos_ref, sa_ref, sb_ref, g64_ref, gr_ref, gb_ref,
         q_ref, k_ref, v_ref, xa_ref, stage, sem) = refs
        x = x_ref[...] + g2_ref[...] * _ungroup(gather_tm, runs_ref, col_ref, ys_hbm, stage, sem)
        xa_ref[...] = x
        rep_ref = None
    else:
        (x_ref, kvsh_ref, kvsc_ref, sh1_ref, sc1_ref, kvn_ref, nm_ref,
         wkv_ref, wq_ref, kn_ref, qn_ref, cos_ref, sa_ref, sb_ref, g64_ref, gr_ref, gb_ref,
         rep_ref, q_ref, k_ref, v_ref) = refs
        x = x_ref[...]
    xn = _unit_rms(x)
    cos, sa, sb = cos_ref[...], sa_ref[...], sb_ref[...]

    ukv = xn * (kvn_ref[...] * (1.0 + _mod_rows(kvsc_ref, rep_ref))) + _mod_rows(kvsh_ref, rep_ref)
    kv = _dot(ukv.astype(BF16), wkv_ref[...])
    k = kv[:, 0:LANES]
    v_ref[...] = kv[:, LANES:2 * LANES]
    kh, kl = _split2(k * k)
    ms = _dot(kh, g64_ref[...]) + _dot(kl, g64_ref[...])
    k_ref[...] = _rope128(k * lax.rsqrt(ms + RMS_EPS) * kn_ref[...], cos, sa, sb)

    u1 = xn * (nm_ref[...] * (1.0 + _mod_rows(sc1_ref, rep_ref))) + _mod_rows(sh1_ref, rep_ref)
    q = _dot(u1.astype(BF16), wq_ref[...])
    ms16 = _dot((q * q).astype(BF16), gr_ref[...])
    rh, rl = _split2(lax.rsqrt(ms16 + RMS_EPS))
    rsb = _dot(jnp.concatenate([rh, rl], 1), gb_ref[...])
    qn = q * rsb * qn_ref[...]
    scale = HEAD_DIM ** -0.5
    cos_q, sa_q, sb_q = cos * scale, sa * scale, sb * scale
    for i in range(D_MODEL // LANES):
        sl = slice(LANES * i, LANES * (i + 1))
        q_ref[:, sl] = _rope128(qn[:, sl], cos_q, sa_q, sb_q).astype(BF16)


def _kvq_call(x, routed, mod0, mod1, mspec, kvmod, kvspec, rep, kvn, nm, wkv, wq, kn, qn, tabs, tab,
              g64, gr, gb, tm):
    n = x.shape[0]
    row = lambda i, *_: (i, 0)
    const = lambda i, *_: (0, 0)
    vec = pl.BlockSpec((1, D_MODEL), const)
    tok = pl.BlockSpec((tm, D_MODEL), row)
    small = pl.BlockSpec((tm, LANES), row)
    ins = [x, kvmod, kvmod, mod1, mod1, kvn, nm, wkv, wq, kn, qn, *tabs, g64, gr, gb]
    specs = [tok, kvspec(0), kvspec(1), mspec(0), mspec(1), vec, vec,
             pl.BlockSpec((D_MODEL, 2 * LANES), const), pl.BlockSpec((D_MODEL, D_MODEL), const),
             pl.BlockSpec((1, LANES), const), vec, tab, tab, tab,
             pl.BlockSpec((LANES, LANES), const), pl.BlockSpec((D_MODEL, LANES), const),
             pl.BlockSpec((2 * LANES, D_MODEL), const)]
    out_shape = [jax.ShapeDtypeStruct((n, D_MODEL), BF16), jax.ShapeDtypeStruct((n, LANES), F32),
                 jax.ShapeDtypeStruct((n, LANES), F32)]
    out_specs = [tok, small, small]
    scratch = []
    nprefetch = 0
    if routed is not None:
        runs, col, ys = routed
        nprefetch = 1
        ins = [runs, x, mod0, col, ys] + ins[1:]
        specs = [tok, mspec(5), small, pl.BlockSpec(memory_space=pl.ANY)] + specs[1:]
        out_shape.append(jax.ShapeDtypeStruct((n, D_MODEL), F32))
        out_specs.append(tok)
        scratch = [pltpu.VMEM((2, _stage_rows(tm), D_MODEL), BF16), pltpu.SemaphoreType.DMA((2,))]
    else:
        ins.append(rep)
        specs.append(pl.BlockSpec(rep.shape, const))
    grid_spec = pltpu.PrefetchScalarGridSpec(
        num_scalar_prefetch=nprefetch, grid=(n // tm,), in_specs=specs,
        out_specs=tuple(out_specs), scratch_shapes=scratch)
    return pl.pallas_call(
        functools.partial(_kvq_kernel, tm if routed is not None else 0),
        out_shape=tuple(out_shape),
        grid_spec=grid_spec,
        compiler_params=_params(1),
        name="kv_q_proj",
    )(*ins)


def _attn_core(q, kcat, vcat, bias, sinks_ref, o_ref):
    tk = kcat.shape[0]
    pairs = N_Q_HEADS // N_KV_HEADS // 2
    lane = lax.broadcasted_iota(jnp.int32, (1, LANES), 1)
    lo = lane < HEAD_DIM
    kro = pltpu.roll(kcat, HEAD_DIM, axis=1)
    vro = pltpu.roll(vcat, HEAD_DIM, axis=1)
    one_e = jnp.broadcast_to(jnp.where(lo, 1.0, 0.0), (tk, LANES))
    one_o = 1.0 - one_e
    for g in range(N_KV_HEADS):
        if g == 0:
            ke, ko = jnp.where(lo, kcat, 0.0), jnp.where(lo, 0.0, kro)
            ve, vo = jnp.where(lo, vcat, 0.0), jnp.where(lo, 0.0, vro)
        else:
            ke, ko = jnp.where(lo, kro, 0.0), jnp.where(lo, 0.0, kcat)
            ve, vo = jnp.where(lo, vro, 0.0), jnp.where(lo, 0.0, vcat)
        k2 = jnp.concatenate([ke, ko], 0).astype(BF16)
        v2 = jnp.concatenate([jnp.concatenate([ve, one_e], 1),
                              jnp.concatenate([vo, one_o], 1)], 0).astype(BF16)
        for p in range(pairs):
            hp = g * pairs + p
            s = _dot_nt(q[:, LANES * hp:LANES * (hp + 1)], k2)
            halves, corr = [], []
            for par in range(2):
                sp = s[:, par * tk:(par + 1) * tk] + bias
                sink = sinks_ref[2 * hp + par]
                m = jnp.maximum(jnp.max(sp, axis=-1, keepdims=True), sink)
                halves.append(jnp.exp(sp - m).astype(BF16))
                corr.append(jnp.exp(sink - m))
            o2 = _dot(jnp.concatenate(halves, 1), v2)
            den = o2[:, LANES:2 * LANES] + jnp.where(lo, corr[0], corr[1])
            o_ref[:, LANES * hp:LANES * (hp + 1)] = (o2[:, 0:LANES] / den).astype(o_ref.dtype)


ATTN_BLOCKS_PER_STEP = 8


def _attn_prompt_kernel(sinks_ref, q_ref, kp_ref, kc_ref, vp_ref, vc_ref, bias_ref, o_ref):
    kall = jnp.concatenate([kp_ref[...], kc_ref[...]], 0)
    vall = jnp.concatenate([vp_ref[...], vc_ref[...]], 0)
    first = jnp.minimum(pl.program_id(1), 1)
    for j in range(ATTN_BLOCKS_PER_STEP):
        rows = slice(j * WINDOW, (j + 1) * WINDOW)
        keys = slice(j * WINDOW, (j + 2) * WINDOW)
        bias = bias_ref[first] if j == 0 else bias_ref[1]
        _attn_core(q_ref[rows, :], kall[keys, :], vall[keys, :], bias, sinks_ref, o_ref.at[rows, :])


ATTN_SEQS_PER_STEP = 8


def _attn_sample_kernel(sinks_ref, q_ref, kc_ref, kn_ref, vc_ref, vn_ref, bias_ref, o_ref, kbuf, vbuf):
    @pl.when(pl.program_id(0) == 0)
    def _():
        kbuf[...] = jnp.zeros_like(kbuf)
        vbuf[...] = jnp.zeros_like(vbuf)

    for b in range(ATTN_SEQS_PER_STEP):
        kbuf[b, 0:WINDOW, :] = kc_ref[b]
        kbuf[b, WINDOW:WINDOW + SAMPLE_PAD, :] = kn_ref[b]
        vbuf[b, 0:WINDOW, :] = vc_ref[b]
        vbuf[b, WINDOW:WINDOW + SAMPLE_PAD, :] = vn_ref[b]
    for b in range(ATTN_SEQS_PER_STEP):
        _attn_core(q_ref[b], kbuf[b], vbuf[b], bias_ref[...], sinks_ref, o_ref.at[b])


def _window_bias(tq, first):
    qi = jnp.arange(tq)[:, None]
    kj = jnp.arange(2 * WINDOW)[None, :]
    ok = (kj > qi) & (kj <= qi + WINDOW)
    if first:
        ok = ok & (kj >= WINDOW)
    return jnp.where(ok, 0.0, -jnp.inf).astype(F32)


def _attn_prompt_call(sinks, q, k, v, B, nb):
    n = q.shape[0]
    per = ATTN_BLOCKS_PER_STEP
    steps = nb // per
    cur = lambda b, i: (b * steps + i, 0)
    prev = lambda b, i: (b * nb + jnp.maximum(per * i - 1, 0), 0)
    bias = jnp.stack([_window_bias(WINDOW, True), _window_bias(WINDOW, False)])
    kv_prev = pl.BlockSpec((WINDOW, LANES), prev)
    kv_cur = pl.BlockSpec((per * WINDOW, LANES), cur)
    return pl.pallas_call(
        _attn_prompt_kernel,
        out_shape=jax.ShapeDtypeStruct((n, D_MODEL), BF16),
        grid=(B, steps),
        in_specs=[pl.BlockSpec(memory_space=pltpu.SMEM),
                  pl.BlockSpec((per * WINDOW, D_MODEL), cur), kv_prev, kv_cur, kv_prev, kv_cur,
                  pl.BlockSpec((2, WINDOW, 2 * WINDOW), lambda b, i: (0, 0, 0))],
        out_specs=pl.BlockSpec((per * WINDOW, D_MODEL), cur),
        compiler_params=_params(2),
        name="swa_prompt",
    )(sinks, q, k, k, v, v, bias)


def _attn_sample_call(sinks, q, kcache, knew, vcache, vnew):
    B = q.shape[0]
    nseq = ATTN_SEQS_PER_STEP
    b3 = lambda b: (b, 0, 0)
    cache = pl.BlockSpec((nseq, WINDOW, LANES), b3)
    new = pl.BlockSpec((nseq, SAMPLE_PAD, LANES), b3)
    return pl.pallas_call(
        _attn_sample_kernel,
        out_shape=jax.ShapeDtypeStruct(q.shape, BF16),
        grid=(B // nseq,),
        in_specs=[pl.BlockSpec(memory_space=pltpu.SMEM),
                  pl.BlockSpec((nseq, SAMPLE_PAD, D_MODEL), b3), cache, new, cache, new,
                  pl.BlockSpec((SAMPLE_PAD, 2 * WINDOW), lambda b: (0, 0))],
        out_specs=pl.BlockSpec((nseq, SAMPLE_PAD, D_MODEL), b3),
        scratch_shapes=[pltpu.VMEM((nseq, 2 * WINDOW, LANES), F32),
                        pltpu.VMEM((nseq, 2 * WINDOW, LANES), F32)],
        compiler_params=_params(1),
        name="swa_sample",
    )(sinks, q, kcache, knew, vcache, vnew, _window_bias(SAMPLE_PAD, False))


def _rope_tables(pos):
    half = ROT_DIM // 2
    inv = ROPE_THETA ** (-jnp.arange(half, dtype=F32) / half)
    ang = pos.astype(F32)[:, None] * inv[None]
    cos, sin = jnp.cos(ang), jnp.sin(ang)
    d = jnp.arange(LANES) % HEAD_DIM
    idx = d % half
    cos_t = jnp.where(d < ROT_DIM, cos[:, idx], 1.0)
    sa = jnp.where(d < half, -sin[:, idx], 0.0)
    sb = jnp.where((d >= half) & (d < ROT_DIM), sin[:, idx], 0.0)
    return cos_t, sa, sb


def _pad_lanes(a, value=0.0):
    return jnp.pad(a, ((0, 0), (0, LANES - a.shape[1])), constant_values=value)


def _prep_weights(ada_w, ada_b, norm_mix, norm_ffn, a_w_in, a_b_gates, a_head_norm, a_w_out,
                  kv_ada_w, kv_ada_b, kv_norm, w_k, w_v, k_norm, b_w_q, b_q_norm, b_sinks, b_w_o,
                  moe_w_group, moe_b_group, moe_w_expert, moe_b_expert, moe_w_gate_up, moe_w_down):
    w = {}
    g0 = QK_COLS + 2 * V_COLS
    w["w_in"] = a_w_in
    w["w_gates"] = jnp.concatenate([_pad_lanes(a_w_in[0, :, g0:g0 + M_HEADS]),
                                    _pad_lanes(a_w_in[0, :, g0 + M_HEADS:])], 1)
    w["bli"] = _pad_lanes(a_b_gates[0][None, :M_HEADS])
    w["blf"] = _pad_lanes(a_b_gates[0][None, M_HEADS:])
    w["head_norm"] = a_head_norm[0][None]
    w["w_out"] = a_w_out[0].astype(BF16)
    w["norm_mix"] = [norm_mix[l][None] for l in range(2)]
    w["norm_ffn"] = [norm_ffn[l][None] for l in range(2)]
    w["router"] = []
    for l in range(2):
        wr = _pad_lanes(jnp.concatenate([moe_w_group[l], moe_w_expert[l]], 1))
        hi = wr.astype(BF16)
        lo = (wr - hi.astype(F32)).astype(BF16)
        br = _pad_lanes(jnp.concatenate([moe_b_group[l], moe_b_expert[l]])[None])
        w["router"].append((jnp.concatenate([hi, lo], 1), br))
    w["w_gu"] = moe_w_gate_up
    w["w_d"] = moe_w_down
    w["kv_norm"] = kv_norm[None]
    w["w_kv"] = jnp.concatenate([w_k, w_v], 1).astype(BF16)
    w["k_norm"] = jnp.tile(k_norm, N_KV_HEADS)[None]
    w["w_q"] = b_w_q[0].astype(BF16)
    w["q_norm"] = jnp.tile(b_q_norm[0], N_Q_HEADS)[None]
    w["sinks"] = b_sinks[0]
    w["w_o"] = b_w_o[0].astype(BF16)
    lanes = jnp.arange(LANES)
    feat = jnp.arange(D_MODEL)
    w["g64"] = jnp.where((lanes[:, None] // HEAD_DIM) == (lanes[None, :] // HEAD_DIM),
                         1.0 / HEAD_DIM, 0.0).astype(BF16)
    w["gr"] = jnp.where((feat[:, None] // HEAD_DIM) == lanes[None, :], 1.0 / HEAD_DIM, 0.0).astype(BF16)
    gb = jnp.where(lanes[:, None] == (feat[None, :] // HEAD_DIM), 1.0, 0.0).astype(BF16)
    w["gb"] = jnp.concatenate([gb, gb], 0)
    return w


def _trunk(x2, mods, kvmod, mspec, mspec_big, rep, w, *, B, T, L, tm, tm_big, tabs, tabspec, c0, n0, m0,
           cache_k=None, cache_v=None):
    sample = cache_k is not None
    nc = T // L if not sample else 1

    qk, v, o, li, lf = _inproj_call(x2, mods[0], mspec_big, rep, w["norm_mix"][0], w["w_in"],
                                    w["w_gates"], w["bli"], w["blf"], tm_big)
    if sample:
        def padtok(a, value=0.0):
            a = a.reshape(B, T, a.shape[-1])
            a = jnp.pad(a, ((0, 0), (0, L - T), (0, 0)), constant_values=value)
            return a.reshape(B * L, a.shape[-1])
        qk, v, li, lf = padtok(qk), padtok(v), padtok(li, M_EMPTY), padtok(lf)
    h, c_new, n_new, m_new = _mlstm_call(qk, v, li, lf, c0, n0, m0, B, nc, L)
    if sample:
        h = h.reshape(B, L, V_COLS)[:, :T].reshape(B * T, V_COLS)
    routed = not sample
    ntiles = (B * T) // tm

    def routed_moe(u2, gate, gid, col, layer):
        q, runs, items = _sort_meta_call(gid.reshape(ntiles, tm), tm)
        su, sg = _scatter_call(runs, q.reshape(ntiles, 1, tm), u2, gate, tm)
        ys = _moe_routed_call(items, su, sg, w["w_gu"], w["w_d"], layer, tm)
        return runs, col, ys

    post0 = _post_call(True, routed, h, o, x2, mods[0], mspec, rep, w["norm_ffn"][0], w["head_norm"],
                       w["w_out"], *w["router"][0], tm)
    kvq_args = (mods[0], mods[1], mspec, kvmod, mspec, rep, w["kv_norm"], w["norm_mix"][1],
                w["w_kv"], w["w_q"], w["k_norm"], w["q_norm"], tabs, tabspec,
                w["g64"], w["gr"], w["gb"], tm)

    if routed:
        x1, u2, gate, gid, col = post0
        q, k, vv, xa = _kvq_call(x1, routed_moe(u2, gate, gid, col, 0), *kvq_args)
    else:
        x1, u2, gate = post0
        xa = _moe_call(u2, gate, w["w_gu"], w["w_d"], 0, x1, mods[0], mspec_big, rep, tm_big)
        q, k, vv = _kvq_call(xa, None, *kvq_args)
    if not sample:
        att = _attn_prompt_call(w["sinks"], q, k, vv, B, T // WINDOW)
        k_win = k.reshape(B, T, LANES)[:, T - WINDOW:].reshape(B, WINDOW, N_KV_HEADS, HEAD_DIM)
        v_win = vv.reshape(B, T, LANES)[:, T - WINDOW:].reshape(B, WINDOW, N_KV_HEADS, HEAD_DIM)
    else:
        def padseq(a):
            return jnp.pad(a.reshape(B, T, a.shape[-1]), ((0, 0), (0, SAMPLE_PAD - T), (0, 0)))
        kc = cache_k.reshape(B, WINDOW, LANES)
        vc = cache_v.reshape(B, WINDOW, LANES)
        att = _attn_sample_call(w["sinks"], padseq(q), kc, padseq(k), vc, padseq(vv))
        att = att[:, :T].reshape(B * T, D_MODEL)
        k_win = jnp.concatenate([kc[:, T:], k.reshape(B, T, LANES)], 1)
        v_win = jnp.concatenate([vc[:, T:], vv.reshape(B, T, LANES)], 1)
        k_win = k_win.reshape(B, WINDOW, N_KV_HEADS, HEAD_DIM)
        v_win = v_win.reshape(B, WINDOW, N_KV_HEADS, HEAD_DIM)
    post1 = _post_call(False, routed, att, None, xa, mods[1], mspec, rep, w["norm_ffn"][1], None,
                       w["w_o"], *w["router"][1], tm)
    if routed:
        x3, u4, gate, gid, col = post1
        runs, col, ys = routed_moe(u4, gate, gid, col, 1)
        y = _resid_gather_call(x3, runs, col, ys, mods[1], mspec, tm)
    else:
        x3, u4, gate = post1
        y = _moe_call(u4, gate, w["w_gu"], w["w_d"], 1, x3, mods[1], mspec_big, rep, tm_big)
    c_out = c_new[None]
    n_out = n_new.reshape(1, B, M_HEADS, M_DK)
    m_out = m_new[:, 0, :M_HEADS][None]
    return y, c_out, n_out, m_out, k_win, v_win


def kernel(x_prompt, x_sample, c_prompt, c_sample, state_c, state_n, state_m, cache_k_win, cache_v_win, ada_w, ada_b, norm_mix, norm_ffn, a_w_in, a_b_gates, a_head_norm, a_w_out, kv_ada_w, kv_ada_b, kv_norm, w_k, w_v, k_norm, b_w_q, b_q_norm, b_sinks, b_w_o, moe_w_group, moe_b_group, moe_w_expert, moe_b_expert, moe_w_gate_up, moe_w_down):
    Bp, Tp, D = x_prompt.shape
    Bs, Ts, _ = x_sample.shape
    w = _prep_weights(ada_w, ada_b, norm_mix, norm_ffn, a_w_in, a_b_gates, a_head_norm, a_w_out,
                      kv_ada_w, kv_ada_b, kv_norm, w_k, w_v, k_norm, b_w_q, b_q_norm, b_sinks, b_w_o,
                      moe_w_group, moe_b_group, moe_w_expert, moe_b_expert, moe_w_gate_up, moe_w_down)

    rows = Bp + Bs
    rpad = -rows % 8
    c_all = jnp.concatenate([c_prompt, c_sample, jnp.zeros((rpad, D), F32)], 0)
    mod = _ada_call(c_all, ada_w, ada_b[:, None, :])
    kvm = _ada_call(c_all, kv_ada_w[None], kv_ada_b[None, None, :])

    tm_p = TOKEN_TILE
    tiles_per_seq = Tp // tm_p
    mods_p = [mod[l, :Bp][:, None, :] for l in range(2)]
    kvmod_p = kvm[0, :Bp][:, None, :]

    def mspec_p(col):
        return pl.BlockSpec((None, 1, D_MODEL), lambda i, *_: (i // tiles_per_seq, 0, col))

    big_tiles_per_seq = Tp // BIG_TOKEN_TILE

    def mspec_big_p(col):
        return pl.BlockSpec((None, 1, D_MODEL), lambda i, *_: (i // big_tiles_per_seq, 0, col))

    tabs_p = _rope_tables(jnp.arange(Tp, dtype=jnp.int32))
    tabspec_p = pl.BlockSpec((tm_p, LANES), lambda i, *_: (i % tiles_per_seq, 0))
    npairs = M_HEADS // 2
    c0 = jnp.zeros((Bp, M_HEADS, M_DK, M_DV), F32)
    n0 = jnp.zeros((Bp, npairs, LANES), F32)
    m0 = jnp.pad(jnp.full((Bp, 1, M_HEADS), M_EMPTY, F32), ((0, 0), (0, 0), (0, LANES - M_HEADS)))
    yp, cp, np_, mp, kwp, vwp = _trunk(
        x_prompt.reshape(Bp * Tp, D), mods_p, kvmod_p, mspec_p, mspec_big_p, None, w,
        B=Bp, T=Tp, L=M_CHUNK, tm=tm_p, tm_big=BIG_TOKEN_TILE, tabs=tabs_p, tabspec=tabspec_p,
        c0=c0, n0=n0, m0=m0)

    ns = Bs * Ts
    mods_s = [mod[l, Bp:Bp + Bs][None] for l in range(2)]
    kvmod_s = kvm[0, Bp:Bp + Bs][None]
    rep = (jnp.arange(ns)[:, None] // Ts == jnp.arange(Bs)[None, :]).astype(BF16)

    def mspec_s(col):
        return pl.BlockSpec((None, Bs, D_MODEL), lambda i, *_: (0, 0, col))

    tabs_s = _rope_tables(PAST_LEN + jnp.arange(Ts, dtype=jnp.int32))
    tabs_s = tuple(jnp.tile(t, (Bs, 1)) for t in tabs_s)
    m0s = jnp.pad(state_m[0][:, None, :], ((0, 0), (0, 0), (0, LANES - M_HEADS)))
    ys, cs, ns_, ms, kws, vws = _trunk(
        x_sample.reshape(ns, D), mods_s, kvmod_s, mspec_s, mspec_s, rep, w,
        B=Bs, T=Ts, L=SAMPLE_PAD, tm=ns, tm_big=ns, tabs=tabs_s,
        tabspec=pl.BlockSpec((ns, LANES), lambda i, *_: (0, 0)),
        c0=state_c[0], n0=state_n[0].reshape(Bs, npairs, LANES), m0=m0s,
        cache_k=cache_k_win, cache_v=cache_v_win)

    return (yp.reshape(Bp, Tp, D), ys.reshape(Bs, Ts, D), cp, np_, mp, kwp, vwp,
            cs, ns_, ms, kws, vws)
```

```python
import functools

import jax
import jax.numpy as jnp
from jax import lax
from jax.experimental import pallas as pl
from jax.experimental.pallas import tpu as pltpu

F32 = jnp.float32
BF16 = jnp.bfloat16

D_MODEL = 1024
PAST_LEN = 8192
M_HEADS = 8
M_DK = 64
M_DV = 128
M_CHUNK = 128
GATE_SOFTCAP = 15.0
M_EMPTY = -1e30
WINDOW = 128
HEAD_DIM = 64
N_Q_HEADS = 16
N_KV_HEADS = 2
ROPE_THETA = 500000.0
ROT_DIM = 16
N_GROUPS = 4
EXPERTS_PER_GROUP = 4
N_EXPERTS = 16
D_EXPERT = 256
RMS_EPS = 1e-6

LANES = 128
QK_COLS = 2 * M_HEADS * M_DK
V_COLS = M_HEADS * M_DV
SAMPLE_PAD = 16
TOKEN_TILE = 512
BIG_TOKEN_TILE = 1024
VMEM_LIMIT = 52 * 1024 * 1024

NT_DIMS = (((1,), (1,)), ((), ()))


def _params(n_axes):
    return pltpu.CompilerParams(dimension_semantics=("arbitrary",) * n_axes,
                                vmem_limit_bytes=VMEM_LIMIT)


def _dot(a, b):
    return jnp.dot(a, b, preferred_element_type=F32)


def _dot_nt(a, b):
    return lax.dot_general(a, b, NT_DIMS, preferred_element_type=F32)


def _split2(x):
    hi = x.astype(BF16)
    lo = (x - hi.astype(F32)).astype(BF16)
    return hi, lo


def _split3(x):
    hi = x.astype(BF16)
    r = x - hi.astype(F32)
    mid = r.astype(BF16)
    lo = (r - mid.astype(F32)).astype(BF16)
    return hi, mid, lo


def _unit_rms(x):
    return x * lax.rsqrt(jnp.mean(x * x, axis=-1, keepdims=True) + RMS_EPS)


def _ada_kernel(c_ref, w_ref, b_ref, o_ref):
    c = c_ref[...]
    cs = (c * jax.nn.sigmoid(c)).astype(BF16)
    o_ref[...] = _dot(cs, w_ref[...].astype(BF16)) + b_ref[...]


def _ada_call(c, w, b):
    g, d, n = w.shape
    r = c.shape[0]
    tn = 2048
    return pl.pallas_call(
        _ada_kernel,
        out_shape=jax.ShapeDtypeStruct((g, r, n), F32),
        grid=(g, n // tn),
        in_specs=[pl.BlockSpec((r, d), lambda i, j: (0, 0)),
                  pl.BlockSpec((None, d, tn), lambda i, j: (i, 0, j)),
                  pl.BlockSpec((None, 1, tn), lambda i, j: (i, 0, j))],
        out_specs=pl.BlockSpec((None, r, tn), lambda i, j: (i, 0, j)),
        compiler_params=_params(2),
        name="ada_mod",
    )(c, w, b)


def _mod_rows(ref, rep_ref):
    if rep_ref is None:
        return ref[...]
    hi, mid, lo = _split3(ref[...])
    rep = rep_ref[...]
    return _dot(rep, hi) + _dot(rep, mid) + _dot(rep, lo)


def _inproj_kernel(expand, x_ref, sh_ref, sc_ref, nw_ref, w_ref, wg_ref, bli_ref, blf_ref, *rest):
    rep_ref = rest[0] if expand else None
    qk_ref, v_ref, o_ref, li_ref, lf_ref, wb_ref = rest[1 if expand else 0:]

    @pl.when(pl.program_id(0) == 0)
    def _():
        wb_ref[...] = w_ref[...].astype(BF16)

    u = _unit_rms(x_ref[...]) * (nw_ref[...] * (1.0 + _mod_rows(sc_ref, rep_ref))) \
        + _mod_rows(sh_ref, rep_ref)
    ub = u.astype(BF16)
    half = QK_COLS // 2
    q = _dot(ub, wb_ref[:, 0:half]) * (M_DK ** -0.5)
    qk_ref[:, 0:half] = q.astype(BF16)
    qk_ref[:, half:QK_COLS] = _dot(ub, wb_ref[:, half:QK_COLS]).astype(BF16)
    v_ref[...] = _dot(ub, wb_ref[:, QK_COLS:QK_COLS + V_COLS]).astype(BF16)
    o_ref[...] = jax.nn.sigmoid(_dot(ub, wb_ref[:, QK_COLS + V_COLS:QK_COLS + 2 * V_COLS])).astype(BF16)
    lane = lax.broadcasted_iota(jnp.int32, (1, LANES), 1)
    live = lane < M_HEADS
    gates = _dot(ub, wg_ref[...].astype(BF16))
    gi = gates[:, 0:LANES] + bli_ref[...]
    gf = gates[:, LANES:2 * LANES] + blf_ref[...]
    li = GATE_SOFTCAP * jnp.tanh(gi / GATE_SOFTCAP)
    fpre = GATE_SOFTCAP * jnp.tanh(gf / GATE_SOFTCAP)
    lf = jnp.minimum(fpre, 0.0) - jnp.log1p(jnp.exp(-jnp.abs(fpre)))
    li_ref[...] = jnp.where(live, li, 0.0)
    lf_ref[...] = jnp.where(live, lf, 0.0)


def _inproj_call(x, mod, mspec, rep, nw, w_in, wg, bli, blf, tm):
    n = x.shape[0]
    row = lambda i: (i, 0)
    const = lambda i: (0, 0)
    main_cols = QK_COLS + 2 * V_COLS
    expand = rep is not None
    return pl.pallas_call(
        functools.partial(_inproj_kernel, expand),
        out_shape=(jax.ShapeDtypeStruct((n, QK_COLS), BF16),
                   jax.ShapeDtypeStruct((n, V_COLS), BF16),
                   jax.ShapeDtypeStruct((n, V_COLS), BF16),
                   jax.ShapeDtypeStruct((n, LANES), F32),
                   jax.ShapeDtypeStruct((n, LANES), F32)),
        grid=(n // tm,),
        in_specs=[pl.BlockSpec((tm, D_MODEL), row), mspec(0), mspec(1),
                  pl.BlockSpec((1, D_MODEL), const),
                  pl.BlockSpec((None, D_MODEL, main_cols), lambda i: (0, 0, 0),
                               pipeline_mode=pl.Buffered(1)),
                  pl.BlockSpec((D_MODEL, 2 * LANES), const),
                  pl.BlockSpec((1, LANES), const), pl.BlockSpec((1, LANES), const)]
        + ([pl.BlockSpec(rep.shape, const)] if expand else []),
        out_specs=(pl.BlockSpec((tm, QK_COLS), row), pl.BlockSpec((tm, V_COLS), row),
                   pl.BlockSpec((tm, V_COLS), row), pl.BlockSpec((tm, LANES), row),
                   pl.BlockSpec((tm, LANES), row)),
        scratch_shapes=[pltpu.VMEM((D_MODEL, main_cols), BF16)],
        compiler_params=_params(1),
        name="mlstm_inproj",
    )(x, mod, mod, nw, w_in, wg, bli, blf, *([rep] if expand else []))


def _mlstm_kernel(L, nseq, *refs):
    c = pl.program_id(1)
    seqs = [[r.at[b] for r in refs] for b in range(nseq)]

    @pl.when(c == 0)
    def _():
        for s in seqs:
            _mlstm_load_state(*s[4:7], *s[11:14])

    for s in seqs:
        _mlstm_seq_step(L, *s[0:4], s[7], *s[11:14])

    @pl.when(c == pl.num_programs(1) - 1)
    def _():
        for s in seqs:
            _mlstm_store_state(*s[8:14])


def _mlstm_load_state(c0_ref, n0_ref, m0_ref, c2_s, n_s, m_s):
    m_s[...] = m0_ref[...]
    n_s[...] = n0_ref[...]
    z = jnp.zeros((M_DK, M_DV), F32)
    for j in range(M_HEADS // 2):
        c2_s[j] = jnp.concatenate([jnp.concatenate([c0_ref[2 * j], z], 1),
                                   jnp.concatenate([z, c0_ref[2 * j + 1]], 1)], 0)


def _mlstm_seq_step(L, qk_ref, v_ref, li_ref, lf_ref, h_ref, c2_s, n_s, m_s):
    npairs = M_HEADS // 2
    hd = M_DK

    lane = lax.broadcasted_iota(jnp.int32, (1, LANES), 1)
    lo128 = lane < hd
    lane256 = lax.broadcasted_iota(jnp.int32, (1, 2 * LANES), 1)
    lo256 = lane256 < LANES
    row128 = lax.broadcasted_iota(jnp.int32, (LANES, 1), 0)
    top = row128 < hd
    blockdiag = (top & lo256) | (jnp.logical_not(top) & jnp.logical_not(lo256))

    LI = li_ref[...]
    LF = lf_ref[...]
    rowL = lax.broadcasted_iota(jnp.int32, (L, LANES), 0)

    def prefix(x, op, ident):
        d = 1
        while d < L:
            shifted = pltpu.roll(x, d, axis=0)
            x = op(x, jnp.where(rowL >= d, shifted, ident))
            d *= 2
        return x

    Bc = prefix(LF, jnp.add, 0.0)
    Cm = LI - Bc
    mprev = m_s[...]
    Gc = jnp.maximum(mprev, prefix(Cm, jnp.maximum, -jnp.inf))
    A = jnp.exp(mprev - Gc)
    bL = Bc[L - 1:L, :]
    DL = bL + Cm
    mnew = jnp.maximum(bL + mprev, jnp.max(DL, axis=0, keepdims=True))
    ast = jnp.exp(bL + mprev - mnew)
    WST = jnp.exp(DL - mnew)

    def pad_rows(x, rows):
        if x.shape[0] == rows:
            return x
        return jnp.concatenate([x, jnp.zeros((rows - x.shape[0], x.shape[1]), x.dtype)], 0)

    kw_lanes = max(L, hd)
    assert kw_lanes in (hd, LANES) and L <= LANES
    wide = kw_lanes == LANES
    lane_s = lax.broadcasted_iota(jnp.int32, (1, 2 * kw_lanes), 1)
    row_s = lax.broadcasted_iota(jnp.int32, (2 * kw_lanes, 1), 0)
    cm_pad = pad_rows(Cm, kw_lanes)
    if wide:
        XT = cm_pad.T
    else:
        XT = jnp.concatenate([cm_pad, cm_pad], 0).T
    causal = (lane_s & (kw_lanes - 1)) <= lax.broadcasted_iota(jnp.int32, (L, 2 * kw_lanes), 0)
    top_s = row_s < kw_lanes
    J = jnp.where((top_s & lo256) | (jnp.logical_not(top_s) & jnp.logical_not(lo256)),
                  1.0, 0.0).astype(BF16)

    EXPM = jnp.exp(-(Bc + Gc))

    def bc(X, h):
        return jnp.broadcast_to(X[:, h:h + 1], X.shape)

    for j in range(npairs):
        he, ho = 2 * j, 2 * j + 1
        p128 = slice(LANES * j, LANES * (j + 1))
        p256 = slice(2 * LANES * j, 2 * LANES * (j + 1))

        def pair128(X):
            return jnp.where(lo128, bc(X, he), bc(X, ho))

        def pair256(X):
            return jnp.concatenate([bc(X, he), bc(X, ho)], 1)

        q128 = qk_ref[:, LANES * j:LANES * (j + 1)]
        k128 = qk_ref[:, QK_COLS // 2 + LANES * j:QK_COLS // 2 + LANES * (j + 1)]
        v256 = v_ref[:, 2 * LANES * j:2 * LANES * (j + 1)]
        zk = jnp.zeros_like(k128)
        zv = jnp.zeros_like(v256)
        K2t = jnp.concatenate([pad_rows(jnp.where(lo128, k128, zk), kw_lanes),
                               pad_rows(jnp.where(lo128, zk, k128), kw_lanes)], 0)
        V2 = jnp.concatenate([pad_rows(jnp.where(lo256, v256, zv), kw_lanes),
                              pad_rows(jnp.where(lo256, zv, v256), kw_lanes)], 0)
        S = _dot_nt(q128, K2t)
        if wide:
            crow = jnp.concatenate([XT[he:he + 1, :], XT[ho:ho + 1, :]], 1)
            arg = jnp.where(causal, crow - pair256(Gc), -jnp.inf)
        else:
            crow = jnp.where(lo128, XT[he:he + 1, :], XT[ho:ho + 1, :])
            arg = jnp.where(causal, crow - pair128(Gc), -jnp.inf)
        Sw = (S * jnp.exp(arg)).astype(BF16)
        num_intra = _dot(Sw, V2)
        rowsum = _dot(Sw, J)
        C2 = c2_s[j]
        npair = n_s[j:j + 1, :]
        Nrow2 = jnp.concatenate(
            [jnp.broadcast_to(jnp.where(lo128, npair, 0.0), (LANES, LANES)),
             jnp.broadcast_to(jnp.where(lo128, 0.0, npair), (LANES, LANES))], 0).astype(BF16)
        qa = (q128.astype(F32) * pair128(A)).astype(BF16)
        num = _dot(qa, C2.astype(BF16)) + num_intra
        den = _dot_nt(qa, Nrow2) + rowsum
        h = num / jnp.maximum(jnp.abs(den), pair256(EXPM))
        h_ref[:, p256] = h.astype(h_ref.dtype)

        kw = k128.astype(F32) * pair128(WST)
        n_s[j:j + 1, :] = pair128(ast) * npair + jnp.sum(kw, axis=0, keepdims=True)
        kwT = pad_rows(kw, LANES).T.astype(BF16)
        dC = _dot(kwT, pad_rows(v256, LANES))
        c2_s[j] = pair256(ast) * C2 + jnp.where(blockdiag, dC, 0.0)

    m_s[...] = mnew


def _mlstm_store_state(cout_ref, nout_ref, mout_ref, c2_s, n_s, m_s):
    for j in range(M_HEADS // 2):
        C2 = c2_s[j]
        cout_ref[2 * j] = C2[0:M_DK, 0:M_DV]
        cout_ref[2 * j + 1] = C2[M_DK:2 * M_DK, M_DV:2 * M_DV]
    nout_ref[...] = n_s[...]
    mout_ref[...] = m_s[...]


MLSTM_SEQS_PER_STEP = 8


def _mlstm_call(qk, v, li, lf, c0, n0, m0, B, nc, L):
    npairs = M_HEADS // 2
    nseq = min(B, MLSTM_SEQS_PER_STEP)
    T = nc * L
    tok3 = lambda a: a.reshape(B, T, a.shape[-1])
    tok = lambda b, c: (b, c, 0)
    st4 = lambda b, c: (b, 0, 0, 0)
    st3 = lambda b, c: (b, 0, 0)
    h, c_new, n_new, m_new = pl.pallas_call(
        functools.partial(_mlstm_kernel, L, nseq),
        out_shape=(jax.ShapeDtypeStruct((B, T, V_COLS), BF16),
                   jax.ShapeDtypeStruct((B, M_HEADS, M_DK, M_DV), F32),
                   jax.ShapeDtypeStruct((B, npairs, LANES), F32),
                   jax.ShapeDtypeStruct((B, 1, LANES), F32)),
        grid=(B // nseq, nc),
        in_specs=[pl.BlockSpec((nseq, L, QK_COLS), tok), pl.BlockSpec((nseq, L, V_COLS), tok),
                  pl.BlockSpec((nseq, L, LANES), tok), pl.BlockSpec((nseq, L, LANES), tok),
                  pl.BlockSpec((nseq, M_HEADS, M_DK, M_DV), st4),
                  pl.BlockSpec((nseq, npairs, LANES), st3),
                  pl.BlockSpec((nseq, 1, LANES), st3)],
        out_specs=(pl.BlockSpec((nseq, L, V_COLS), tok),
                   pl.BlockSpec((nseq, M_HEADS, M_DK, M_DV), st4),
                   pl.BlockSpec((nseq, npairs, LANES), st3),
                   pl.BlockSpec((nseq, 1, LANES), st3)),
        scratch_shapes=[pltpu.VMEM((nseq, npairs, 2 * M_DK, 2 * M_DV), F32),
                        pltpu.VMEM((nseq, npairs, LANES), F32),
                        pltpu.VMEM((nseq, 1, LANES), F32)],
        compiler_params=_params(2),
        name="mlstm_chunks",
    )(tok3(qk), tok3(v), tok3(li), tok3(lf), c0, n0, m0)
    return h.reshape(B * T, V_COLS), c_new, n_new, m_new


def _route(lg):
    lane = lax.broadcasted_iota(jnp.int32, lg.shape, 1)
    lanef = lane.astype(F32)
    neg = -jnp.inf
    far = float(LANES)
    gm = lane < N_GROUPS
    lgm = jnp.where(gm, lg, neg)
    gmax = jnp.max(lgm, axis=-1, keepdims=True)
    gsum = jnp.sum(jnp.exp(lgm - gmax), axis=-1, keepdims=True)
    g_w = 1.0 / gsum
    gidx = jnp.min(jnp.where(gm & (lg == gmax), lanef, far), axis=-1, keepdims=True)
    first = N_GROUPS + EXPERTS_PER_GROUP * gidx
    sel = (lanef >= first) & (lanef < first + EXPERTS_PER_GROUP)
    l1 = jnp.max(jnp.where(sel, lg, neg), axis=-1, keepdims=True)
    i1 = jnp.min(jnp.where(sel & (lg == l1), lanef, far), axis=-1, keepdims=True)
    sel2 = sel & (lanef != i1)
    l2 = jnp.max(jnp.where(sel2, lg, neg), axis=-1, keepdims=True)
    i2 = jnp.min(jnp.where(sel2 & (lg == l2), lanef, far), axis=-1, keepdims=True)
    r = jnp.exp(l2 - l1)
    w1 = g_w / (1.0 + r)
    w2 = w1 * r
    return jnp.where(lanef == i1, w1, jnp.where(lanef == i2, w2, 0.0)), gidx


def _post_kernel(mlstm, routed, expand, *refs):
    refs = list(refs)
    h_ref = refs.pop(0)
    o_ref = refs.pop(0) if mlstm else None
    x_ref, g1_ref, sh2_ref, sc2_ref, nf_ref = refs[:5]
    refs = refs[5:]
    hn_ref = refs.pop(0) if mlstm else None
    wout_ref, wrhl_ref, br_ref = refs[:3]
    refs = refs[3:]
    tri_ref = refs.pop(0) if routed else None
    rep_ref = refs.pop(0) if expand else None
    x1_ref, u2_ref, gate_ref = refs[:3]
    outs = refs[3:]
    if mlstm:
        hf = h_ref[...].astype(F32)
        parts = [_unit_rms(hf[:, M_DV * i:M_DV * (i + 1)]) for i in range(M_HEADS)]
        hn = jnp.concatenate(parts, 1) * hn_ref[...]
        hg = (hn * o_ref[...].astype(F32)).astype(BF16)
    else:
        hg = h_ref[...]
    x1 = x_ref[...] + _mod_rows(g1_ref, rep_ref) * _dot(hg, wout_ref[...])
    x1_ref[...] = x1
    u2 = _unit_rms(x1) * (nf_ref[...] * (1.0 + _mod_rows(sc2_ref, rep_ref))) + _mod_rows(sh2_ref, rep_ref)
    uh, ul = _split2(u2)
    wrh = wrhl_ref[:, 0:LANES]
    lg = _dot(uh, wrh) + _dot(ul, wrh) + _dot(uh, wrhl_ref[:, LANES:2 * LANES]) + br_ref[...]
    gate, gidx = _route(lg)
    u2_ref[...] = uh
    gate_ref[...] = gate
    if not routed:
        return
    gid_ref, col_ref = outs
    rows = []
    for blk in range(x1.shape[0] // LANES):
        col = jnp.broadcast_to(gidx[LANES * blk:LANES * (blk + 1), :], (LANES, LANES))
        rows.append(col.T[0:1, :])
    gid_ref[...] = jnp.concatenate(rows, 1)
    lanef = lax.broadcasted_iota(jnp.int32, gate.shape, 1).astype(F32)
    onehot = jnp.where(lanef == gidx, 1.0, 0.0)
    earlier = _dot(tri_ref[...], onehot.astype(BF16))
    rank = jnp.sum(onehot * earlier, axis=-1, keepdims=True)
    col_ref[...] = jnp.where(lanef == 0.0, gidx, jnp.where(lanef == 1.0, rank, 0.0))


def _post_call(mlstm, routed, h, o, x, mod, mspec, rep, nf, hn, wout, wrhl, br, tm):
    n = x.shape[0]
    expand = rep is not None
    row = lambda i: (i, 0)
    const = lambda i: (0, 0)
    tok = pl.BlockSpec((tm, D_MODEL), row)
    small = pl.BlockSpec((tm, LANES), row)
    vec = pl.BlockSpec((1, D_MODEL), const)
    ins = [h] + ([o] if mlstm else []) + [x, mod, mod, mod, nf] + ([hn] if mlstm else []) \
        + [wout, wrhl, br]
    specs = [tok] + ([tok] if mlstm else []) + [tok, mspec(2), mspec(3), mspec(4), vec] \
        + ([vec] if mlstm else []) \
        + [pl.BlockSpec((D_MODEL, D_MODEL), const), pl.BlockSpec((D_MODEL, 2 * LANES), const),
           pl.BlockSpec((1, LANES), const)]
    out_shape = [jax.ShapeDtypeStruct((n, D_MODEL), F32), jax.ShapeDtypeStruct((n, D_MODEL), BF16),
                 jax.ShapeDtypeStruct((n, LANES), F32)]
    out_specs = [tok, tok, small]
    if routed:
        t = jnp.arange(tm)
        ins.append((t[None, :] < t[:, None]).astype(BF16))
        specs.append(pl.BlockSpec((tm, tm), const))
        out_shape += [jax.ShapeDtypeStruct((n // tm, 1, tm), F32), jax.ShapeDtypeStruct((n, LANES), F32)]
        out_specs += [pl.BlockSpec((None, 1, tm), lambda i: (i, 0, 0)), small]
    if expand:
        ins.append(rep)
        specs.append(pl.BlockSpec(rep.shape, const))
    out_shape, out_specs = tuple(out_shape), tuple(out_specs)
    return pl.pallas_call(
        functools.partial(_post_kernel, mlstm, routed, expand),
        out_shape=out_shape,
        grid=(n // tm,),
        in_specs=specs,
        out_specs=out_specs,
        compiler_params=_params(1),
        name="post_mlstm" if mlstm else "post_attn",
    )(*ins)


ITEM_ROWS = 8
RUN_ALIGN = 16
RUN_SIZES = (512, 256, 128, 64, 32, 16)
RUN_SRC, RUN_DST, RUN_LEN, RUN_TOTAL = 0, N_GROUPS, 2 * N_GROUPS, 3 * N_GROUPS


def _stage_rows(tm):
    assert N_GROUPS * (RUN_ALIGN - 1) <= LANES
    return tm + LANES


def _sorted_rows(n, tm):
    rows = n + N_GROUPS * (n // tm) * RUN_ALIGN + tm
    return -(-rows // tm) * tm


def _sort_meta_kernel(tm, nts, gid_ref, q_ref, runs_ref, items_ref):
    ntiles = gid_ref.shape[0]
    gid = gid_ref[...]
    r = lax.broadcasted_iota(jnp.int32, (tm, tm), 0)
    c = lax.broadcasted_iota(jnp.int32, (tm, tm), 1)
    before = jnp.where(r < c, 1.0, 0.0).astype(BF16)
    trow = lax.broadcasted_iota(jnp.int32, (ntiles, LANES), 0)
    lane = lax.broadcasted_iota(jnp.int32, (ntiles, LANES), 1)
    k = lax.broadcasted_iota(jnp.int32, (1, LANES), 1).astype(F32)
    zero11 = jnp.zeros((1, 1), F32)
    q = jnp.zeros((ntiles, tm), F32)
    runs = jnp.zeros((ntiles, LANES), F32)
    src = jnp.zeros((ntiles, 1), F32)
    start, nitems = zero11, zero11
    grp = jnp.zeros((1, LANES), F32)
    tile = jnp.zeros((1, LANES), F32)
    valid = jnp.zeros((1, LANES), F32)
    for g in range(N_GROUPS):
        mask = jnp.where(gid == float(g), 1.0, 0.0)
        cnt = jnp.sum(mask, axis=1, keepdims=True)
        padded = jnp.floor((cnt + (RUN_ALIGN - 1.0)) * (1.0 / RUN_ALIGN)) * RUN_ALIGN
        incl = jnp.broadcast_to(padded, (ntiles, LANES))
        d = 1
        while d < ntiles:
            incl = incl + jnp.where(trow >= d, pltpu.roll(incl, d, axis=0), 0.0)
            d *= 2
        total = incl[ntiles - 1:ntiles, 0:1]
        dst = start + incl[:, 0:1] - padded
        q = q + mask * (src + _dot(mask.astype(BF16), before))
        runs = runs + jnp.where(lane == RUN_SRC + g, src, 0.0) + jnp.where(lane == RUN_DST + g, dst, 0.0) \
            + jnp.where(lane == RUN_LEN + g, padded, 0.0)
        src = src + padded
        end = start + total
        ft = jnp.floor(start * (1.0 / tm))
        lt = jnp.floor((end - 1.0) * (1.0 / tm))
        ni = jnp.where(total > 0.0, lt - ft + 1.0, 0.0)
        inside = (k >= nitems) & (k < nitems + ni)
        grp = grp + jnp.where(inside, float(g), 0.0)
        tile = tile + jnp.where(inside, ft + (k - nitems), 0.0)
        valid = valid + jnp.where(inside, 1.0, 0.0)
        nitems = nitems + ni
        start = end
    runs = runs + jnp.where(lane == RUN_TOTAL, start, 0.0)
    live = valid > 0.0
    prev = pltpu.roll(tile, 1, axis=1)
    nxt = pltpu.roll(tile, LANES - 1, axis=1)
    first = live & ((k == 0.0) | (tile != prev))
    last = live & ((k == nitems - 1.0) | (tile != nxt))
    spare = jnp.floor((start - 1.0) * (1.0 / tm)) + 1.0 + (k - nitems)
    fill = jnp.logical_not(live) & (spare <= nts - 1.0)
    tile = jnp.where(live, tile, jnp.minimum(spare, nts - 1.0))
    grp = jnp.where(live, grp, jnp.max(grp, axis=1, keepdims=True))
    flag = lambda m: jnp.where(m, 1.0, 0.0)
    q_ref[...] = q
    runs_ref[...] = runs.astype(jnp.int32)
    table = jnp.concatenate([tile, grp, valid, flag(first | fill), flag(last | fill),
                             jnp.zeros((ITEM_ROWS - 5, LANES), F32)], 0)
    items_ref[...] = table.astype(jnp.int32)


def _max_items(nts):
    return nts + N_GROUPS - 1


def _sort_meta_call(gid, tm):
    ntiles = gid.shape[0]
    nts = _sorted_rows(ntiles * tm, tm) // tm
    assert _max_items(nts) <= LANES and tm <= RUN_SIZES[0]
    return pl.pallas_call(
        functools.partial(_sort_meta_kernel, tm, nts),
        out_shape=(jax.ShapeDtypeStruct((ntiles, tm), F32),
                   jax.ShapeDtypeStruct((ntiles, LANES), jnp.int32),
                   jax.ShapeDtypeStruct((ITEM_ROWS, LANES), jnp.int32)),
        compiler_params=_params(0),
        name="moe_sort_meta",
    )(gid)


def _run_copies(runs_ref, i, tile_ref, sorted_hbm, sem, to_sorted):
    pieces = []
    for g in range(N_GROUPS):
        src = runs_ref[i, RUN_SRC + g]
        dst = runs_ref[i, RUN_DST + g]
        length = runs_ref[i, RUN_LEN + g]
        for s in RUN_SIZES:
            def build(src=src, dst=dst, length=length, s=s):
                off = length & (-2 * s)
                a = tile_ref.at[pl.ds(pl.multiple_of(src + off, RUN_ALIGN), s), :]
                b = sorted_hbm.at[pl.ds(pl.multiple_of(dst + off, RUN_ALIGN), s), :]
                return pltpu.make_async_copy(a, b, sem) if to_sorted else pltpu.make_async_copy(b, a, sem)
            pieces.append(((length & s) != 0, build))
    return pieces


def _start(pieces):
    for pred, build in pieces:
        pl.when(pred)(lambda build=build: build().start())


def _wait(pieces):
    for pred, build in pieces:
        pl.when(pred)(lambda build=build: build().wait())


def _start_then_wait(pieces):
    _start(pieces)
    _wait(pieces)


def _scatter_kernel(tm, nts, ntiles, runs_ref, q_ref, u_ref, g_ref, su_hbm, sg_hbm,
                    stu, stg, sem_u, sem_g):
    i = pl.program_id(0)
    slot = i % 2

    def copies(tile, slot):
        return (_run_copies(runs_ref, tile, stu.at[slot], su_hbm, sem_u.at[slot], True)
                + _run_copies(runs_ref, tile, stg.at[slot], sg_hbm, sem_g.at[slot], True))

    r = lax.broadcasted_iota(jnp.int32, (_stage_rows(tm), tm), 0).astype(F32)
    perm = jnp.where(q_ref[...] == r, 1.0, 0.0).astype(BF16)
    stu[slot] = _dot(perm, u_ref[...]).astype(BF16)
    both = _dot(perm, jnp.concatenate(_split2(g_ref[...]), 1))
    stg[slot] = both[:, 0:LANES] + both[:, LANES:2 * LANES]
    _start(copies(i, slot))

    @pl.when(i > 0)
    def _():
        _wait(copies(i - 1, 1 - slot))

    @pl.when(i == ntiles - 1)
    def _():
        _wait(copies(i, slot))
        stu[slot] = jnp.zeros(stu.shape[1:], stu.dtype)
        stg[slot] = jnp.zeros(stg.shape[1:], stg.dtype)
        total = runs_ref[i, RUN_TOTAL]
        tail = nts * tm - total
        nfull = tail // tm
        pieces = []
        for tile_ref, hbm, sem in ((stu.at[slot], su_hbm, sem_u.at[slot]),
                                   (stg.at[slot], sg_hbm, sem_g.at[slot])):
            for j in range(nts - ntiles):
                def full(j=j, tile_ref=tile_ref, hbm=hbm, sem=sem):
                    dst = pl.multiple_of(total + j * tm, RUN_ALIGN)
                    return pltpu.make_async_copy(tile_ref.at[pl.ds(0, tm), :], hbm.at[pl.ds(dst, tm), :], sem)
                pieces.append((j < nfull, full))
            rem = tail - nfull * tm
            for s in RUN_SIZES:
                if s >= tm:
                    continue
                def part(s=s, tile_ref=tile_ref, hbm=hbm, sem=sem):
                    dst = pl.multiple_of(total + nfull * tm + (rem & (-2 * s)), RUN_ALIGN)
                    return pltpu.make_async_copy(tile_ref.at[pl.ds(0, s), :], hbm.at[pl.ds(dst, s), :], sem)
                pieces.append(((rem & s) != 0, part))
        _start_then_wait(pieces)


def _scatter_call(runs, q3, u2, gate, tm):
    n = u2.shape[0]
    ns = _sorted_rows(n, tm)
    grid_spec = pltpu.PrefetchScalarGridSpec(
        num_scalar_prefetch=1,
        grid=(n // tm,),
        in_specs=[pl.BlockSpec((None, 1, tm), lambda i, *_: (i, 0, 0)),
                  pl.BlockSpec((tm, D_MODEL), lambda i, *_: (i, 0)),
                  pl.BlockSpec((tm, LANES), lambda i, *_: (i, 0))],
        out_specs=(pl.BlockSpec(memory_space=pl.ANY), pl.BlockSpec(memory_space=pl.ANY)),
        scratch_shapes=[pltpu.VMEM((2, _stage_rows(tm), D_MODEL), BF16),
                        pltpu.VMEM((2, _stage_rows(tm), LANES), F32),
                        pltpu.SemaphoreType.DMA((2,)), pltpu.SemaphoreType.DMA((2,))])
    return pl.pallas_call(
        functools.partial(_scatter_kernel, tm, ns // tm, n // tm),
        out_shape=(jax.ShapeDtypeStruct((ns, D_MODEL), BF16), jax.ShapeDtypeStruct((ns, LANES), F32)),
        grid_spec=grid_spec,
        compiler_params=_params(1),
        name="moe_scatter",
    )(runs, q3, u2, gate)


def _ungroup(tm, runs_ref, col_ref, ys_hbm, stage, sem):
    i = pl.program_id(0)
    slot = i % 2

    def copies(tile, slot):
        return _run_copies(runs_ref, tile, stage.at[slot], ys_hbm, sem.at[slot], False)

    @pl.when(i == 0)
    def _():
        stage[...] = jnp.zeros_like(stage)
        _start(copies(i, slot))

    @pl.when(i + 1 < pl.num_programs(0))
    def _():
        _start(copies(i + 1, 1 - slot))

    _wait(copies(i, slot))
    col = col_ref[...]
    gidx = col[:, 0:1]
    src = [runs_ref[i, RUN_SRC + g].astype(F32) for g in range(N_GROUPS)]
    first = src[N_GROUPS - 1]
    for g in range(N_GROUPS - 2, -1, -1):
        first = jnp.where(gidx == float(g), src[g], first)
    q = first + col[:, 1:2]
    lanes = lax.broadcasted_iota(jnp.int32, (tm, _stage_rows(tm)), 1).astype(F32)
    perm_t = jnp.where(q == lanes, 1.0, 0.0).astype(BF16)
    return _dot(perm_t, stage[slot])


def _moe_routed_kernel(items_ref, su_ref, sg_ref, wgu_ref, wd_ref, out_ref, acc_ref, wgu_b, wd_b):
    kk = pl.program_id(0)
    grp = items_ref[1, kk]

    @pl.when((kk == 0) | (grp != items_ref[1, jnp.maximum(kk - 1, 0)]))
    def _():
        for e in range(EXPERTS_PER_GROUP):
            wgu_b[e] = wgu_ref[e].astype(BF16)
            wd_b[e] = wd_ref[e].astype(BF16)

    @pl.when(items_ref[3, kk] == 1)
    def _():
        acc_ref[...] = jnp.zeros_like(acc_ref)

    @pl.when(items_ref[2, kk] == 1)
    def _():
        u = su_ref[...]
        gate = sg_ref[...]
        lane = lax.broadcasted_iota(jnp.int32, gate.shape, 1)
        first_lane = N_GROUPS + EXPERTS_PER_GROUP * grp
        acts = []
        for e in range(EXPERTS_PER_GROUP):
            gcol = jnp.sum(jnp.where(lane == first_lane + e, gate, 0.0), axis=-1, keepdims=True)
            hu = _dot(u, wgu_b[e])
            hg = hu[:, 0:D_EXPERT]
            acts.append((hg * jax.nn.sigmoid(hg) * hu[:, D_EXPERT:2 * D_EXPERT] * gcol).astype(BF16))
        wd_all = wd_b[...].reshape(EXPERTS_PER_GROUP * D_EXPERT, D_MODEL)
        acc_ref[...] += _dot(jnp.concatenate(acts, 1), wd_all)

    @pl.when(items_ref[4, kk] == 1)
    def _():
        out_ref[...] = acc_ref[...].astype(out_ref.dtype)


def _moe_routed_call(items, su, sg, wgu, wd, layer, tm):
    ns = su.shape[0]
    n_items = _max_items(ns // tm)
    depth = wgu.shape[0]
    wgu5 = wgu.reshape(depth, N_GROUPS, EXPERTS_PER_GROUP, D_MODEL, 2 * D_EXPERT)
    wd5 = wd.reshape(depth, N_GROUPS, EXPERTS_PER_GROUP, D_EXPERT, D_MODEL)
    grid_spec = pltpu.PrefetchScalarGridSpec(
        num_scalar_prefetch=1,
        grid=(n_items,),
        in_specs=[pl.BlockSpec((tm, D_MODEL), lambda k, it: (it[0, k], 0)),
                  pl.BlockSpec((tm, LANES), lambda k, it: (it[0, k], 0)),
                  pl.BlockSpec((None, None, EXPERTS_PER_GROUP, D_MODEL, 2 * D_EXPERT),
                               lambda k, it: (layer, it[1, k], 0, 0, 0)),
                  pl.BlockSpec((None, None, EXPERTS_PER_GROUP, D_EXPERT, D_MODEL),
                               lambda k, it: (layer, it[1, k], 0, 0, 0))],
        out_specs=pl.BlockSpec((tm, D_MODEL), lambda k, it: (it[0, k], 0)),
        scratch_shapes=[pltpu.VMEM((tm, D_MODEL), F32),
                        pltpu.VMEM((EXPERTS_PER_GROUP, D_MODEL, 2 * D_EXPERT), BF16),
                        pltpu.VMEM((EXPERTS_PER_GROUP, D_EXPERT, D_MODEL), BF16)])
    return pl.pallas_call(
        _moe_routed_kernel,
        out_shape=jax.ShapeDtypeStruct((ns, D_MODEL), BF16),
        grid_spec=grid_spec,
        compiler_params=_params(1),
        name="moe_routed",
    )(items, su, sg, wgu5, wd5)


def _resid_gather_kernel(tm, runs_ref, x_ref, g2_ref, col_ref, ys_hbm, y_ref, stage, sem):
    y_ref[...] = x_ref[...] + g2_ref[...] * _ungroup(tm, runs_ref, col_ref, ys_hbm, stage, sem)


def _resid_gather_call(x, runs, col, ys, mod, mspec, tm):
    n = x.shape[0]
    row = lambda i, *_: (i, 0)
    grid_spec = pltpu.PrefetchScalarGridSpec(
        num_scalar_prefetch=1,
        grid=(n // tm,),
        in_specs=[pl.BlockSpec((tm, D_MODEL), row), mspec(5), pl.BlockSpec((tm, LANES), row),
                  pl.BlockSpec(memory_space=pl.ANY)],
        out_specs=pl.BlockSpec((tm, D_MODEL), row),
        scratch_shapes=[pltpu.VMEM((2, _stage_rows(tm), D_MODEL), BF16), pltpu.SemaphoreType.DMA((2,))])
    return pl.pallas_call(
        functools.partial(_resid_gather_kernel, tm),
        out_shape=jax.ShapeDtypeStruct((n, D_MODEL), F32),
        grid_spec=grid_spec,
        compiler_params=_params(1),
        name="moe_unsort_resid",
    )(runs, x, mod, col, ys)


def _moe_kernel(expand, u_ref, gate_ref, wgu_ref, wd_ref, x1_ref, g2_ref, *rest):
    rep_ref = rest[0] if expand else None
    y_ref, acc_ref = rest[1 if expand else 0:]
    grp = pl.program_id(1)

    @pl.when(grp == 0)
    def _():
        acc_ref[...] = jnp.zeros_like(acc_ref)

    u = u_ref[...]
    gate = gate_ref[...]
    lane = lax.broadcasted_iota(jnp.int32, gate.shape, 1)
    first_lane = N_GROUPS + EXPERTS_PER_GROUP * grp
    acts = []
    for e in range(EXPERTS_PER_GROUP):
        hu = _dot(u, wgu_ref[e].astype(BF16))
        hg = hu[:, 0:D_EXPERT]
        gcol = jnp.sum(jnp.where(lane == first_lane + e, gate, 0.0), axis=-1, keepdims=True)
        acts.append((hg * jax.nn.sigmoid(hg) * hu[:, D_EXPERT:2 * D_EXPERT] * gcol).astype(BF16))
    wd_all = wd_ref[...].astype(BF16).reshape(EXPERTS_PER_GROUP * D_EXPERT, D_MODEL)
    acc_ref[...] += _dot(jnp.concatenate(acts, 1), wd_all)

    @pl.when(grp == N_GROUPS - 1)
    def _():
        y_ref[...] = x1_ref[...] + _mod_rows(g2_ref, rep_ref) * acc_ref[...]


def _moe_call(u2, gate, wgu, wd, layer, x1, mod, mspec, rep, tm):
    n = u2.shape[0]
    row = lambda i, e: (i, 0)
    expand = rep is not None
    depth = wgu.shape[0]
    wgu5 = wgu.reshape(depth, N_GROUPS, EXPERTS_PER_GROUP, D_MODEL, 2 * D_EXPERT)
    wd5 = wd.reshape(depth, N_GROUPS, EXPERTS_PER_GROUP, D_EXPERT, D_MODEL)
    return pl.pallas_call(
        functools.partial(_moe_kernel, expand),
        out_shape=jax.ShapeDtypeStruct((n, D_MODEL), F32),
        grid=(n // tm, N_GROUPS),
        in_specs=[pl.BlockSpec((tm, D_MODEL), row), pl.BlockSpec((tm, LANES), row),
                  pl.BlockSpec((None, None, EXPERTS_PER_GROUP, D_MODEL, 2 * D_EXPERT),
                               lambda i, g: (layer, g, 0, 0, 0)),
                  pl.BlockSpec((None, None, EXPERTS_PER_GROUP, D_EXPERT, D_MODEL),
                               lambda i, g: (layer, g, 0, 0, 0)),
                  pl.BlockSpec((tm, D_MODEL), row), mspec(5)]
        + ([pl.BlockSpec(rep.shape, lambda i, e: (0, 0))] if expand else []),
        out_specs=pl.BlockSpec((tm, D_MODEL), row),
        scratch_shapes=[pltpu.VMEM((tm, D_MODEL), F32)],
        compiler_params=_params(2),
        name="moe_dense",
    )(u2, gate, wgu5, wd5, x1, mod, *([rep] if expand else []))


def _rope128(x, cos, sa, sb):
    return x * cos + pltpu.roll(x, LANES - ROT_DIM // 2, axis=1) * sa \
        + pltpu.roll(x, ROT_DIM // 2, axis=1) * sb


def _kvq_kernel(gather_tm, *refs):
    if gather_tm:
        (runs_ref, x_ref, g2_ref, col_ref, ys_hbm, kvsh_ref, kvsc_ref, sh1_ref, sc1_ref, kvn_ref,
         nm_ref, wkv_ref, wq_ref, kn_ref, qn_ref, cos_ref, sa_ref, sb_ref, g64_ref, gr_ref, gb_ref,
         q_ref, k_ref, v_ref, xa_ref, stage, sem) = refs
        x = x_ref[...] + g2_ref[...] * _ungroup(gather_tm, runs_ref, col_ref, ys_hbm, stage, sem)
        xa_ref[...] = x
        rep_ref = None
    else:
        (x_ref, kvsh_ref, kvsc_ref, sh1_ref, sc1_ref, kvn_ref, nm_ref,
         wkv_ref, wq_ref, kn_ref, qn_ref, cos_ref, sa_ref, sb_ref, g64_ref, gr_ref, gb_ref,
         rep_ref, q_ref, k_ref, v_ref) = refs
        x = x_ref[...]
    xn = _unit_rms(x)
    cos, sa, sb = cos_ref[...], sa_ref[...], sb_ref[...]

    ukv = xn * (kvn_ref[...] * (1.0 + _mod_rows(kvsc_ref, rep_ref))) + _mod_rows(kvsh_ref, rep_ref)
    kv = _dot(ukv.astype(BF16), wkv_ref[...])
    k = kv[:, 0:LANES]
    v_ref[...] = kv[:, LANES:2 * LANES]
    kh, kl = _split2(k * k)
    ms = _dot(kh, g64_ref[...]) + _dot(kl, g64_ref[...])
    k_ref[...] = _rope128(k * lax.rsqrt(ms + RMS_EPS) * kn_ref[...], cos, sa, sb)

    u1 = xn * (nm_ref[...] * (1.0 + _mod_rows(sc1_ref, rep_ref))) + _mod_rows(sh1_ref, rep_ref)
    q = _dot(u1.astype(BF16), wq_ref[...])
    ms16 = _dot((q * q).astype(BF16), gr_ref[...])
    rh, rl = _split2(lax.rsqrt(ms16 + RMS_EPS))
    rsb = _dot(jnp.concatenate([rh, rl], 1), gb_ref[...])
    qn = q * rsb * qn_ref[...]
    scale = HEAD_DIM ** -0.5
    cos_q, sa_q, sb_q = cos * scale, sa * scale, sb * scale
    for i in range(D_MODEL // LANES):
        sl = slice(LANES * i, LANES * (i + 1))
        q_ref[:, sl] = _rope128(qn[:, sl], cos_q, sa_q, sb_q).astype(BF16)


def _kvq_call(x, routed, mod0, mod1, mspec, kvmod, kvspec, rep, kvn, nm, wkv, wq, kn, qn, tabs, tab,
              g64, gr, gb, tm):
    n = x.shape[0]
    row = lambda i, *_: (i, 0)
    const = lambda i, *_: (0, 0)
    vec = pl.BlockSpec((1, D_MODEL), const)
    tok = pl.BlockSpec((tm, D_MODEL), row)
    small = pl.BlockSpec((tm, LANES), row)
    ins = [x, kvmod, kvmod, mod1, mod1, kvn, nm, wkv, wq, kn, qn, *tabs, g64, gr, gb]
    specs = [tok, kvspec(0), kvspec(1), mspec(0), mspec(1), vec, vec,
             pl.BlockSpec((D_MODEL, 2 * LANES), const), pl.BlockSpec((D_MODEL, D_MODEL), const),
             pl.BlockSpec((1, LANES), const), vec, tab, tab, tab,
             pl.BlockSpec((LANES, LANES), const), pl.BlockSpec((D_MODEL, LANES), const),
             pl.BlockSpec((2 * LANES, D_MODEL), const)]
    out_shape = [jax.ShapeDtypeStruct((n, D_MODEL), BF16), jax.ShapeDtypeStruct((n, LANES), F32),
                 jax.ShapeDtypeStruct((n, LANES), F32)]
    out_specs = [tok, small, small]
    scratch = []
    nprefetch = 0
    if routed is not None:
        runs, col, ys = routed
        nprefetch = 1
        ins = [runs, x, mod0, col, ys] + ins[1:]
        specs = [tok, mspec(5), small, pl.BlockSpec(memory_space=pl.ANY)] + specs[1:]
        out_shape.append(jax.ShapeDtypeStruct((n, D_MODEL), F32))
        out_specs.append(tok)
        scratch = [pltpu.VMEM((2, _stage_rows(tm), D_MODEL), BF16), pltpu.SemaphoreType.DMA((2,))]
    else:
        ins.append(rep)
        specs.append(pl.BlockSpec(rep.shape, const))
    grid_spec = pltpu.PrefetchScalarGridSpec(
        num_scalar_prefetch=nprefetch, grid=(n // tm,), in_specs=specs,
        out_specs=tuple(out_specs), scratch_shapes=scratch)
    return pl.pallas_call(
        functools.partial(_kvq_kernel, tm if routed is not None else 0),
        out_shape=tuple(out_shape),
        grid_spec=grid_spec,
        compiler_params=_params(1),
        name="kv_q_proj",
    )(*ins)


def _attn_core(q, kcat, vcat, bias, sinks_ref, o_ref):
    tk = kcat.shape[0]
    pairs = N_Q_HEADS // N_KV_HEADS // 2
    lane = lax.broadcasted_iota(jnp.int32, (1, LANES), 1)
    lo = lane < HEAD_DIM
    kro = pltpu.roll(kcat, HEAD_DIM, axis=1)
    vro = pltpu.roll(vcat, HEAD_DIM, axis=1)
    one_e = jnp.broadcast_to(jnp.where(lo, 1.0, 0.0), (tk, LANES))
    one_o = 1.0 - one_e
    for g in range(N_KV_HEADS):
        if g == 0:
            ke, ko = jnp.where(lo, kcat, 0.0), jnp.where(lo, 0.0, kro)
            ve, vo = jnp.where(lo, vcat, 0.0), jnp.where(lo, 0.0, vro)
        else:
            ke, ko = jnp.where(lo, kro, 0.0), jnp.where(lo, 0.0, kcat)
            ve, vo = jnp.where(lo, vro, 0.0), jnp.where(lo, 0.0, vcat)
        k2 = jnp.concatenate([ke, ko], 0).astype(BF16)
        v2 = jnp.concatenate([jnp.concatenate([ve, one_e], 1),
                              jnp.concatenate([vo, one_o], 1)], 0).astype(BF16)
        for p in range(pairs):
            hp = g * pairs + p
            s = _dot_nt(q[:, LANES * hp:LANES * (hp + 1)], k2)
            halves, corr = [], []
            for par in range(2):
                sp = s[:, par * tk:(par + 1) * tk] + bias
                sink = sinks_ref[2 * hp + par]
                m = jnp.maximum(jnp.max(sp, axis=-1, keepdims=True), sink)
                halves.append(jnp.exp(sp - m).astype(BF16))
                corr.append(jnp.exp(sink - m))
            o2 = _dot(jnp.concatenate(halves, 1), v2)
            den = o2[:, LANES:2 * LANES] + jnp.where(lo, corr[0], corr[1])
            o_ref[:, LANES * hp:LANES * (hp + 1)] = (o2[:, 0:LANES] / den).astype(o_ref.dtype)


ATTN_BLOCKS_PER_STEP = 8


def _attn_prompt_kernel(sinks_ref, q_ref, kp_ref, kc_ref, vp_ref, vc_ref, bias_ref, o_ref):
    kall = jnp.concatenate([kp_ref[...], kc_ref[...]], 0)
    vall = jnp.concatenate([vp_ref[...], vc_ref[...]], 0)
    first = jnp.minimum(pl.program_id(1), 1)
    for j in range(ATTN_BLOCKS_PER_STEP):
        rows = slice(j * WINDOW, (j + 1) * WINDOW)
        keys = slice(j * WINDOW, (j + 2) * WINDOW)
        bias = bias_ref[first] if j == 0 else bias_ref[1]
        _attn_core(q_ref[rows, :], kall[keys, :], vall[keys, :], bias, sinks_ref, o_ref.at[rows, :])


ATTN_SEQS_PER_STEP = 16


def _attn_sample_kernel(sinks_ref, q_ref, kc_ref, kn_ref, vc_ref, vn_ref, bias_ref, o_ref, kbuf, vbuf):
    @pl.when(pl.program_id(0) == 0)
    def _():
        kbuf[...] = jnp.zeros_like(kbuf)
        vbuf[...] = jnp.zeros_like(vbuf)

    for b in range(ATTN_SEQS_PER_STEP):
        kbuf[b, 0:WINDOW, :] = kc_ref[b]
        kbuf[b, WINDOW:WINDOW + SAMPLE_PAD, :] = kn_ref[b]
        vbuf[b, 0:WINDOW, :] = vc_ref[b]
        vbuf[b, WINDOW:WINDOW + SAMPLE_PAD, :] = vn_ref[b]
    for b in range(ATTN_SEQS_PER_STEP):
        _attn_core(q_ref[b], kbuf[b], vbuf[b], bias_ref[...], sinks_ref, o_ref.at[b])


def _window_bias(tq, first):
    qi = jnp.arange(tq)[:, None]
    kj = jnp.arange(2 * WINDOW)[None, :]
    ok = (kj > qi) & (kj <= qi + WINDOW)
    if first:
        ok = ok & (kj >= WINDOW)
    return jnp.where(ok, 0.0, -jnp.inf).astype(F32)


def _attn_prompt_call(sinks, q, k, v, B, nb):
    n = q.shape[0]
    per = ATTN_BLOCKS_PER_STEP
    steps = nb // per
    cur = lambda b, i: (b * steps + i, 0)
    prev = lambda b, i: (b * nb + jnp.maximum(per * i - 1, 0), 0)
    bias = jnp.stack([_window_bias(WINDOW, True), _window_bias(WINDOW, False)])
    kv_prev = pl.BlockSpec((WINDOW, LANES), prev)
    kv_cur = pl.BlockSpec((per * WINDOW, LANES), cur)
    return pl.pallas_call(
        _attn_prompt_kernel,
        out_shape=jax.ShapeDtypeStruct((n, D_MODEL), BF16),
        grid=(B, steps),
        in_specs=[pl.BlockSpec(memory_space=pltpu.SMEM),
                  pl.BlockSpec((per * WINDOW, D_MODEL), cur), kv_prev, kv_cur, kv_prev, kv_cur,
                  pl.BlockSpec((2, WINDOW, 2 * WINDOW), lambda b, i: (0, 0, 0))],
        out_specs=pl.BlockSpec((per * WINDOW, D_MODEL), cur),
        compiler_params=_params(2),
        name="swa_prompt",
    )(sinks, q, k, k, v, v, bias)


def _attn_sample_call(sinks, q, kcache, knew, vcache, vnew):
    B = q.shape[0]
    nseq = ATTN_SEQS_PER_STEP
    b3 = lambda b: (b, 0, 0)
    cache = pl.BlockSpec((nseq, WINDOW, LANES), b3)
    new = pl.BlockSpec((nseq, SAMPLE_PAD, LANES), b3)
    return pl.pallas_call(
        _attn_sample_kernel,
        out_shape=jax.ShapeDtypeStruct(q.shape, BF16),
        grid=(B // nseq,),
        in_specs=[pl.BlockSpec(memory_space=pltpu.SMEM),
                  pl.BlockSpec((nseq, SAMPLE_PAD, D_MODEL), b3), cache, new, cache, new,
                  pl.BlockSpec((SAMPLE_PAD, 2 * WINDOW), lambda b: (0, 0))],
        out_specs=pl.BlockSpec((nseq, SAMPLE_PAD, D_MODEL), b3),
        scratch_shapes=[pltpu.VMEM((nseq, 2 * WINDOW, LANES), F32),
                        pltpu.VMEM((nseq, 2 * WINDOW, LANES), F32)],
        compiler_params=_params(1),
        name="swa_sample",
    )(sinks, q, kcache, knew, vcache, vnew, _window_bias(SAMPLE_PAD, False))


def _rope_tables(pos):
    half = ROT_DIM // 2
    inv = ROPE_THETA ** (-jnp.arange(half, dtype=F32) / half)
    ang = pos.astype(F32)[:, None] * inv[None]
    cos, sin = jnp.cos(ang), jnp.sin(ang)
    d = jnp.arange(LANES) % HEAD_DIM
    idx = d % half
    cos_t = jnp.where(d < ROT_DIM, cos[:, idx], 1.0)
    sa = jnp.where(d < half, -sin[:, idx], 0.0)
    sb = jnp.where((d >= half) & (d < ROT_DIM), sin[:, idx], 0.0)
    return cos_t, sa, sb


def _pad_lanes(a, value=0.0):
    return jnp.pad(a, ((0, 0), (0, LANES - a.shape[1])), constant_values=value)


def _prep_weights(ada_w, ada_b, norm_mix, norm_ffn, a_w_in, a_b_gates, a_head_norm, a_w_out,
                  kv_ada_w, kv_ada_b, kv_norm, w_k, w_v, k_norm, b_w_q, b_q_norm, b_sinks, b_w_o,
                  moe_w_group, moe_b_group, moe_w_expert, moe_b_expert, moe_w_gate_up, moe_w_down):
    w = {}
    g0 = QK_COLS + 2 * V_COLS
    w["w_in"] = a_w_in
    w["w_gates"] = jnp.concatenate([_pad_lanes(a_w_in[0, :, g0:g0 + M_HEADS]),
                                    _pad_lanes(a_w_in[0, :, g0 + M_HEADS:])], 1)
    w["bli"] = _pad_lanes(a_b_gates[0][None, :M_HEADS])
    w["blf"] = _pad_lanes(a_b_gates[0][None, M_HEADS:])
    w["head_norm"] = a_head_norm[0][None]
    w["w_out"] = a_w_out[0].astype(BF16)
    w["norm_mix"] = [norm_mix[l][None] for l in range(2)]
    w["norm_ffn"] = [norm_ffn[l][None] for l in range(2)]
    w["router"] = []
    for l in range(2):
        wr = _pad_lanes(jnp.concatenate([moe_w_group[l], moe_w_expert[l]], 1))
        hi = wr.astype(BF16)
        lo = (wr - hi.astype(F32)).astype(BF16)
        br = _pad_lanes(jnp.concatenate([moe_b_group[l], moe_b_expert[l]])[None])
        w["router"].append((jnp.concatenate([hi, lo], 1), br))
    w["w_gu"] = moe_w_gate_up
    w["w_d"] = moe_w_down
    w["kv_norm"] = kv_norm[None]
    w["w_kv"] = jnp.concatenate([w_k, w_v], 1).astype(BF16)
    w["k_norm"] = jnp.tile(k_norm, N_KV_HEADS)[None]
    w["w_q"] = b_w_q[0].astype(BF16)
    w["q_norm"] = jnp.tile(b_q_norm[0], N_Q_HEADS)[None]
    w["sinks"] = b_sinks[0]
    w["w_o"] = b_w_o[0].astype(BF16)
    lanes = jnp.arange(LANES)
    feat = jnp.arange(D_MODEL)
    w["g64"] = jnp.where((lanes[:, None] // HEAD_DIM) == (lanes[None, :] // HEAD_DIM),
                         1.0 / HEAD_DIM, 0.0).astype(BF16)
    w["gr"] = jnp.where((feat[:, None] // HEAD_DIM) == lanes[None, :], 1.0 / HEAD_DIM, 0.0).astype(BF16)
    gb = jnp.where(lanes[:, None] == (feat[None, :] // HEAD_DIM), 1.0, 0.0).astype(BF16)
    w["gb"] = jnp.concatenate([gb, gb], 0)
    return w


def _trunk(x2, mods, kvmod, mspec, mspec_big, rep, w, *, B, T, L, tm, tm_big, tabs, tabspec, c0, n0, m0,
           cache_k=None, cache_v=None):
    sample = cache_k is not None
    nc = T // L if not sample else 1

    qk, v, o, li, lf = _inproj_call(x2, mods[0], mspec_big, rep, w["norm_mix"][0], w["w_in"],
                                    w["w_gates"], w["bli"], w["blf"], tm_big)
    if sample:
        def padtok(a, value=0.0):
            a = a.reshape(B, T, a.shape[-1])
            a = jnp.pad(a, ((0, 0), (0, L - T), (0, 0)), constant_values=value)
            return a.reshape(B * L, a.shape[-1])
        qk, v, li, lf = padtok(qk), padtok(v), padtok(li, M_EMPTY), padtok(lf)
    h, c_new, n_new, m_new = _mlstm_call(qk, v, li, lf, c0, n0, m0, B, nc, L)
    if sample:
        h = h.reshape(B, L, V_COLS)[:, :T].reshape(B * T, V_COLS)
    routed = not sample
    ntiles = (B * T) // tm

    def routed_moe(u2, gate, gid, col, layer):
        q, runs, items = _sort_meta_call(gid.reshape(ntiles, tm), tm)
        su, sg = _scatter_call(runs, q.reshape(ntiles, 1, tm), u2, gate, tm)
        ys = _moe_routed_call(items, su, sg, w["w_gu"], w["w_d"], layer, tm)
        return runs, col, ys

    post0 = _post_call(True, routed, h, o, x2, mods[0], mspec, rep, w["norm_ffn"][0], w["head_norm"],
                       w["w_out"], *w["router"][0], tm)
    kvq_args = (mods[0], mods[1], mspec, kvmod, mspec, rep, w["kv_norm"], w["norm_mix"][1],
                w["w_kv"], w["w_q"], w["k_norm"], w["q_norm"], tabs, tabspec,
                w["g64"], w["gr"], w["gb"], tm)

    if routed:
        x1, u2, gate, gid, col = post0
        q, k, vv, xa = _kvq_call(x1, routed_moe(u2, gate, gid, col, 0), *kvq_args)
    else:
        x1, u2, gate = post0
        xa = _moe_call(u2, gate, w["w_gu"], w["w_d"], 0, x1, mods[0], mspec_big, rep, tm_big)
        q, k, vv = _kvq_call(xa, None, *kvq_args)
    if not sample:
        att = _attn_prompt_call(w["sinks"], q, k, vv, B, T // WINDOW)
        k_win = k.reshape(B, T, LANES)[:, T - WINDOW:].reshape(B, WINDOW, N_KV_HEADS, HEAD_DIM)
        v_win = vv.reshape(B, T, LANES)[:, T - WINDOW:].reshape(B, WINDOW, N_KV_HEADS, HEAD_DIM)
    else:
        def padseq(a):
            return jnp.pad(a.reshape(B, T, a.shape[-1]), ((0, 0), (0, SAMPLE_PAD - T), (0, 0)))
        kc = cache_k.reshape(B, WINDOW, LANES)
        vc = cache_v.reshape(B, WINDOW, LANES)
        att = _attn_sample_call(w["sinks"], padseq(q), kc, padseq(k), vc, padseq(vv))
        att = att[:, :T].reshape(B * T, D_MODEL)
        k_win = jnp.concatenate([kc[:, T:], k.reshape(B, T, LANES)], 1)
        v_win = jnp.concatenate([vc[:, T:], vv.reshape(B, T, LANES)], 1)
        k_win = k_win.reshape(B, WINDOW, N_KV_HEADS, HEAD_DIM)
        v_win = v_win.reshape(B, WINDOW, N_KV_HEADS, HEAD_DIM)
    post1 = _post_call(False, routed, att, None, xa, mods[1], mspec, rep, w["norm_ffn"][1], None,
                       w["w_o"], *w["router"][1], tm)
    if routed:
        x3, u4, gate, gid, col = post1
        runs, col, ys = routed_moe(u4, gate, gid, col, 1)
        y = _resid_gather_call(x3, runs, col, ys, mods[1], mspec, tm)
    else:
        x3, u4, gate = post1
        y = _moe_call(u4, gate, w["w_gu"], w["w_d"], 1, x3, mods[1], mspec_big, rep, tm_big)
    c_out = c_new[None]
    n_out = n_new.reshape(1, B, M_HEADS, M_DK)
    m_out = m_new[:, 0, :M_HEADS][None]
    return y, c_out, n_out, m_out, k_win, v_win


def kernel(x_prompt, x_sample, c_prompt, c_sample, state_c, state_n, state_m, cache_k_win, cache_v_win, ada_w, ada_b, norm_mix, norm_ffn, a_w_in, a_b_gates, a_head_norm, a_w_out, kv_ada_w, kv_ada_b, kv_norm, w_k, w_v, k_norm, b_w_q, b_q_norm, b_sinks, b_w_o, moe_w_group, moe_b_group, moe_w_expert, moe_b_expert, moe_w_gate_up, moe_w_down):
    Bp, Tp, D = x_prompt.shape
    Bs, Ts, _ = x_sample.shape
    w = _prep_weights(ada_w, ada_b, norm_mix, norm_ffn, a_w_in, a_b_gates, a_head_norm, a_w_out,
                      kv_ada_w, kv_ada_b, kv_norm, w_k, w_v, k_norm, b_w_q, b_q_norm, b_sinks, b_w_o,
                      moe_w_group, moe_b_group, moe_w_expert, moe_b_expert, moe_w_gate_up, moe_w_down)

    rows = Bp + Bs
    rpad = -rows % 8
    c_all = jnp.concatenate([c_prompt, c_sample, jnp.zeros((rpad, D), F32)], 0)
    mod = _ada_call(c_all, ada_w, ada_b[:, None, :])
    kvm = _ada_call(c_all, kv_ada_w[None], kv_ada_b[None, None, :])

    tm_p = TOKEN_TILE
    tiles_per_seq = Tp // tm_p
    mods_p = [mod[l, :Bp][:, None, :] for l in range(2)]
    kvmod_p = kvm[0, :Bp][:, None, :]

    def mspec_p(col):
        return pl.BlockSpec((None, 1, D_MODEL), lambda i, *_: (i // tiles_per_seq, 0, col))

    big_tiles_per_seq = Tp // BIG_TOKEN_TILE

    def mspec_big_p(col):
        return pl.BlockSpec((None, 1, D_MODEL), lambda i, *_: (i // big_tiles_per_seq, 0, col))

    tabs_p = _rope_tables(jnp.arange(Tp, dtype=jnp.int32))
    tabspec_p = pl.BlockSpec((tm_p, LANES), lambda i, *_: (i % tiles_per_seq, 0))
    npairs = M_HEADS // 2
    c0 = jnp.zeros((Bp, M_HEADS, M_DK, M_DV), F32)
    n0 = jnp.zeros((Bp, npairs, LANES), F32)
    m0 = jnp.pad(jnp.full((Bp, 1, M_HEADS), M_EMPTY, F32), ((0, 0), (0, 0), (0, LANES - M_HEADS)))
    yp, cp, np_, mp, kwp, vwp = _trunk(
        x_prompt.reshape(Bp * Tp, D), mods_p, kvmod_p, mspec_p, mspec_big_p, None, w,
        B=Bp, T=Tp, L=M_CHUNK, tm=tm_p, tm_big=BIG_TOKEN_TILE, tabs=tabs_p, tabspec=tabspec_p,
        c0=c0, n0=n0, m0=m0)

    ns = Bs * Ts
    mods_s = [mod[l, Bp:Bp + Bs][None] for l in range(2)]
    kvmod_s = kvm[0, Bp:Bp + Bs][None]
    rep = (jnp.arange(ns)[:, None] // Ts == jnp.arange(Bs)[None, :]).astype(BF16)

    def mspec_s(col):
        return pl.BlockSpec((None, Bs, D_MODEL), lambda i, *_: (0, 0, col))

    tabs_s = _rope_tables(PAST_LEN + jnp.arange(Ts, dtype=jnp.int32))
    tabs_s = tuple(jnp.tile(t, (Bs, 1)) for t in tabs_s)
    m0s = jnp.pad(state_m[0][:, None, :], ((0, 0), (0, 0), (0, LANES - M_HEADS)))
    ys, cs, ns_, ms, kws, vws = _trunk(
        x_sample.reshape(ns, D), mods_s, kvmod_s, mspec_s, mspec_s, rep, w,
        B=Bs, T=Ts, L=SAMPLE_PAD, tm=ns, tm_big=ns, tabs=tabs_s,
        tabspec=pl.BlockSpec((ns, LANES), lambda i, *_: (0, 0)),
        c0=state_c[0], n0=state_n[0].reshape(Bs, npairs, LANES), m0=m0s,
        cache_k=cache_k_win, cache_v=cache_v_win)

    return (yp.reshape(Bp, Tp, D), ys.reshape(Bs, Ts, D), cp, np_, mp, kwp, vwp,
            cs, ns_, ms, kws, vws)
```

```python
import functools

import jax
import jax.numpy as jnp
from jax import lax
from jax.experimental import pallas as pl
from jax.experimental.pallas import tpu as pltpu

F32 = jnp.float32
BF16 = jnp.bfloat16

D_MODEL = 1024
PAST_LEN = 8192
M_HEADS = 8
M_DK = 64
M_DV = 128
M_CHUNK = 128
GATE_SOFTCAP = 15.0
M_EMPTY = -1e30
WINDOW = 128
HEAD_DIM = 64
N_Q_HEADS = 16
N_KV_HEADS = 2
ROPE_THETA = 500000.0
ROT_DIM = 16
N_GROUPS = 4
EXPERTS_PER_GROUP = 4
N_EXPERTS = 16
D_EXPERT = 256
RMS_EPS = 1e-6

LANES = 128
QK_COLS = 2 * M_HEADS * M_DK
V_COLS = M_HEADS * M_DV
SAMPLE_PAD = 16
TOKEN_TILE = 512
BIG_TOKEN_TILE = 1024
VMEM_LIMIT = 52 * 1024 * 1024

NT_DIMS = (((1,), (1,)), ((), ()))


def _params(n_axes):
    return pltpu.CompilerParams(dimension_semantics=("arbitrary",) * n_axes,
                                vmem_limit_bytes=VMEM_LIMIT)


def _dot(a, b):
    return jnp.dot(a, b, preferred_element_type=F32)


def _dot_nt(a, b):
    return lax.dot_general(a, b, NT_DIMS, preferred_element_type=F32)


def _split2(x):
    hi = x.astype(BF16)
    lo = (x - hi.astype(F32)).astype(BF16)
    return hi, lo


def _split3(x):
    hi = x.astype(BF16)
    r = x - hi.astype(F32)
    mid = r.astype(BF16)
    lo = (r - mid.astype(F32)).astype(BF16)
    return hi, mid, lo


def _unit_rms(x):
    return x * lax.rsqrt(jnp.mean(x * x, axis=-1, keepdims=True) + RMS_EPS)


def _ada_kernel(c_ref, w_ref, b_ref, o_ref):
    c = c_ref[...]
    cs = (c * jax.nn.sigmoid(c)).astype(BF16)
    o_ref[...] = _dot(cs, w_ref[...].astype(BF16)) + b_ref[...]


def _ada_call(c, w, b):
    g, d, n = w.shape
    r = c.shape[0]
    tn = 2048
    return pl.pallas_call(
        _ada_kernel,
        out_shape=jax.ShapeDtypeStruct((g, r, n), F32),
        grid=(g, n // tn),
        in_specs=[pl.BlockSpec((r, d), lambda i, j: (0, 0)),
                  pl.BlockSpec((None, d, tn), lambda i, j: (i, 0, j)),
                  pl.BlockSpec((None, 1, tn), lambda i, j: (i, 0, j))],
        out_specs=pl.BlockSpec((None, r, tn), lambda i, j: (i, 0, j)),
        compiler_params=_params(2),
        name="ada_mod",
    )(c, w, b)


def _mod_rows(ref, rep_ref):
    if rep_ref is None:
        return ref[...]
    hi, mid, lo = _split3(ref[...])
    rep = rep_ref[...]
    return _dot(rep, hi) + _dot(rep, mid) + _dot(rep, lo)


def _inproj_kernel(expand, x_ref, sh_ref, sc_ref, nw_ref, w_ref, wg_ref, bli_ref, blf_ref, *rest):
    rep_ref = rest[0] if expand else None
    qk_ref, v_ref, o_ref, li_ref, lf_ref, wb_ref = rest[1 if expand else 0:]

    @pl.when(pl.program_id(0) == 0)
    def _():
        wb_ref[...] = w_ref[...].astype(BF16)

    u = _unit_rms(x_ref[...]) * (nw_ref[...] * (1.0 + _mod_rows(sc_ref, rep_ref))) \
        + _mod_rows(sh_ref, rep_ref)
    ub = u.astype(BF16)
    half = QK_COLS // 2
    q = _dot(ub, wb_ref[:, 0:half]) * (M_DK ** -0.5)
    qk_ref[:, 0:half] = q.astype(BF16)
    qk_ref[:, half:QK_COLS] = _dot(ub, wb_ref[:, half:QK_COLS]).astype(BF16)
    v_ref[...] = _dot(ub, wb_ref[:, QK_COLS:QK_COLS + V_COLS]).astype(BF16)
    o_ref[...] = jax.nn.sigmoid(_dot(ub, wb_ref[:, QK_COLS + V_COLS:QK_COLS + 2 * V_COLS])).astype(BF16)
    lane = lax.broadcasted_iota(jnp.int32, (1, LANES), 1)
    live = lane < M_HEADS
    gates = _dot(ub, wg_ref[...].astype(BF16))
    gi = gates[:, 0:LANES] + bli_ref[...]
    gf = gates[:, LANES:2 * LANES] + blf_ref[...]
    li = GATE_SOFTCAP * jnp.tanh(gi / GATE_SOFTCAP)
    fpre = GATE_SOFTCAP * jnp.tanh(gf / GATE_SOFTCAP)
    lf = jnp.minimum(fpre, 0.0) - jnp.log1p(jnp.exp(-jnp.abs(fpre)))
    li_ref[...] = jnp.where(live, li, 0.0)
    lf_ref[...] = jnp.where(live, lf, 0.0)


def _inproj_call(x, mod, mspec, rep, nw, w_in, wg, bli, blf, tm):
    n = x.shape[0]
    row = lambda i: (i, 0)
    const = lambda i: (0, 0)
    main_cols = QK_COLS + 2 * V_COLS
    expand = rep is not None
    return pl.pallas_call(
        functools.partial(_inproj_kernel, expand),
        out_shape=(jax.ShapeDtypeStruct((n, QK_COLS), BF16),
                   jax.ShapeDtypeStruct((n, V_COLS), BF16),
                   jax.ShapeDtypeStruct((n, V_COLS), BF16),
                   jax.ShapeDtypeStruct((n, LANES), F32),
                   jax.ShapeDtypeStruct((n, LANES), F32)),
        grid=(n // tm,),
        in_specs=[pl.BlockSpec((tm, D_MODEL), row), mspec(0), mspec(1),
                  pl.BlockSpec((1, D_MODEL), const),
                  pl.BlockSpec((None, D_MODEL, main_cols), lambda i: (0, 0, 0),
                               pipeline_mode=pl.Buffered(1)),
                  pl.BlockSpec((D_MODEL, 2 * LANES), const),
                  pl.BlockSpec((1, LANES), const), pl.BlockSpec((1, LANES), const)]
        + ([pl.BlockSpec(rep.shape, const)] if expand else []),
        out_specs=(pl.BlockSpec((tm, QK_COLS), row), pl.BlockSpec((tm, V_COLS), row),
                   pl.BlockSpec((tm, V_COLS), row), pl.BlockSpec((tm, LANES), row),
                   pl.BlockSpec((tm, LANES), row)),
        scratch_shapes=[pltpu.VMEM((D_MODEL, main_cols), BF16)],
        compiler_params=_params(1),
        name="mlstm_inproj",
    )(x, mod, mod, nw, w_in, wg, bli, blf, *([rep] if expand else []))


def _mlstm_kernel(L, nseq, *refs):
    c = pl.program_id(1)
    seqs = [[r.at[b] for r in refs] for b in range(nseq)]

    @pl.when(c == 0)
    def _():
        for s in seqs:
            _mlstm_load_state(*s[4:7], *s[11:14])

    for s in seqs:
        _mlstm_seq_step(L, *s[0:4], s[7], *s[11:14])

    @pl.when(c == pl.num_programs(1) - 1)
    def _():
        for s in seqs:
            _mlstm_store_state(*s[8:14])


def _mlstm_load_state(c0_ref, n0_ref, m0_ref, c2_s, n_s, m_s):
    m_s[...] = m0_ref[...]
    n_s[...] = n0_ref[...]
    z = jnp.zeros((M_DK, M_DV), F32)
    for j in range(M_HEADS // 2):
        c2_s[j] = jnp.concatenate([jnp.concatenate([c0_ref[2 * j], z], 1),
                                   jnp.concatenate([z, c0_ref[2 * j + 1]], 1)], 0)


def _mlstm_seq_step(L, qk_ref, v_ref, li_ref, lf_ref, h_ref, c2_s, n_s, m_s):
    npairs = M_HEADS // 2
    hd = M_DK

    lane = lax.broadcasted_iota(jnp.int32, (1, LANES), 1)
    lo128 = lane < hd
    lane256 = lax.broadcasted_iota(jnp.int32, (1, 2 * LANES), 1)
    lo256 = lane256 < LANES
    row128 = lax.broadcasted_iota(jnp.int32, (LANES, 1), 0)
    top = row128 < hd
    blockdiag = (top & lo256) | (jnp.logical_not(top) & jnp.logical_not(lo256))

    LI = li_ref[...]
    LF = lf_ref[...]
    rowL = lax.broadcasted_iota(jnp.int32, (L, LANES), 0)

    def prefix(x, op, ident):
        d = 1
        while d < L:
            shifted = pltpu.roll(x, d, axis=0)
            x = op(x, jnp.where(rowL >= d, shifted, ident))
            d *= 2
        return x

    Bc = prefix(LF, jnp.add, 0.0)
    Cm = LI - Bc
    mprev = m_s[...]
    Gc = jnp.maximum(mprev, prefix(Cm, jnp.maximum, -jnp.inf))
    A = jnp.exp(mprev - Gc)
    bL = Bc[L - 1:L, :]
    DL = bL + Cm
    mnew = jnp.maximum(bL + mprev, jnp.max(DL, axis=0, keepdims=True))
    ast = jnp.exp(bL + mprev - mnew)
    WST = jnp.exp(DL - mnew)

    def pad_rows(x, rows):
        if x.shape[0] == rows:
            return x
        return jnp.concatenate([x, jnp.zeros((rows - x.shape[0], x.shape[1]), x.dtype)], 0)

    kw_lanes = max(L, hd)
    assert kw_lanes in (hd, LANES) and L <= LANES
    wide = kw_lanes == LANES
    lane_s = lax.broadcasted_iota(jnp.int32, (1, 2 * kw_lanes), 1)
    row_s = lax.broadcasted_iota(jnp.int32, (2 * kw_lanes, 1), 0)
    cm_pad = pad_rows(Cm, kw_lanes)
    if wide:
        XT = cm_pad.T
    else:
        XT = jnp.concatenate([cm_pad, cm_pad], 0).T
    causal = (lane_s & (kw_lanes - 1)) <= lax.broadcasted_iota(jnp.int32, (L, 2 * kw_lanes), 0)
    top_s = row_s < kw_lanes
    J = jnp.where((top_s & lo256) | (jnp.logical_not(top_s) & jnp.logical_not(lo256)),
                  1.0, 0.0).astype(BF16)

    EXPM = jnp.exp(-(Bc + Gc))

    def bc(X, h):
        return jnp.broadcast_to(X[:, h:h + 1], X.shape)

    for j in range(npairs):
        he, ho = 2 * j, 2 * j + 1
        p128 = slice(LANES * j, LANES * (j + 1))
        p256 = slice(2 * LANES * j, 2 * LANES * (j + 1))

        def pair128(X):
            return jnp.where(lo128, bc(X, he), bc(X, ho))

        def pair256(X):
            return jnp.concatenate([bc(X, he), bc(X, ho)], 1)

        q128 = qk_ref[:, LANES * j:LANES * (j + 1)]
        k128 = qk_ref[:, QK_COLS // 2 + LANES * j:QK_COLS // 2 + LANES * (j + 1)]
        v256 = v_ref[:, 2 * LANES * j:2 * LANES * (j + 1)]
        zk = jnp.zeros_like(k128)
        zv = jnp.zeros_like(v256)
        K2t = jnp.concatenate([pad_rows(jnp.where(lo128, k128, zk), kw_lanes),
                               pad_rows(jnp.where(lo128, zk, k128), kw_lanes)], 0)
        V2 = jnp.concatenate([pad_rows(jnp.where(lo256, v256, zv), kw_lanes),
                              pad_rows(jnp.where(lo256, zv, v256), kw_lanes)], 0)
        S = _dot_nt(q128, K2t)
        if wide:
            crow = jnp.concatenate([XT[he:he + 1, :], XT[ho:ho + 1, :]], 1)
            arg = jnp.where(causal, crow - pair256(Gc), -jnp.inf)
        else:
            crow = jnp.where(lo128, XT[he:he + 1, :], XT[ho:ho + 1, :])
            arg = jnp.where(causal, crow - pair128(Gc), -jnp.inf)
        Sw = (S * jnp.exp(arg)).astype(BF16)
        num_intra = _dot(Sw, V2)
        rowsum = _dot(Sw, J)
        C2 = c2_s[j]
        npair = n_s[j:j + 1, :]
        Nrow2 = jnp.concatenate(
            [jnp.broadcast_to(jnp.where(lo128, npair, 0.0), (LANES, LANES)),
             jnp.broadcast_to(jnp.where(lo128, 0.0, npair), (LANES, LANES))], 0).astype(BF16)
        qa = (q128.astype(F32) * pair128(A)).astype(BF16)
        num = _dot(qa, C2.astype(BF16)) + num_intra
        den = _dot_nt(qa, Nrow2) + rowsum
        h = num / jnp.maximum(jnp.abs(den), pair256(EXPM))
        h_ref[:, p256] = h.astype(h_ref.dtype)

        kw = k128.astype(F32) * pair128(WST)
        n_s[j:j + 1, :] = pair128(ast) * npair + jnp.sum(kw, axis=0, keepdims=True)
        kwT = pad_rows(kw, LANES).T.astype(BF16)
        dC = _dot(kwT, pad_rows(v256, LANES))
        c2_s[j] = pair256(ast) * C2 + jnp.where(blockdiag, dC, 0.0)

    m_s[...] = mnew


def _mlstm_store_state(cout_ref, nout_ref, mout_ref, c2_s, n_s, m_s):
    for j in range(M_HEADS // 2):
        C2 = c2_s[j]
        cout_ref[2 * j] = C2[0:M_DK, 0:M_DV]
        cout_ref[2 * j + 1] = C2[M_DK:2 * M_DK, M_DV:2 * M_DV]
    nout_ref[...] = n_s[...]
    mout_ref[...] = m_s[...]


MLSTM_SEQS_PER_STEP = 8


def _mlstm_call(qk, v, li, lf, c0, n0, m0, B, nc, L):
    npairs = M_HEADS // 2
    nseq = min(B, MLSTM_SEQS_PER_STEP)
    T = nc * L
    tok3 = lambda a: a.reshape(B, T, a.shape[-1])
    tok = lambda b, c: (b, c, 0)
    st4 = lambda b, c: (b, 0, 0, 0)
    st3 = lambda b, c: (b, 0, 0)
    h, c_new, n_new, m_new = pl.pallas_call(
        functools.partial(_mlstm_kernel, L, nseq),
        out_shape=(jax.ShapeDtypeStruct((B, T, V_COLS), BF16),
                   jax.ShapeDtypeStruct((B, M_HEADS, M_DK, M_DV), F32),
                   jax.ShapeDtypeStruct((B, npairs, LANES), F32),
                   jax.ShapeDtypeStruct((B, 1, LANES), F32)),
        grid=(B // nseq, nc),
        in_specs=[pl.BlockSpec((nseq, L, QK_COLS), tok), pl.BlockSpec((nseq, L, V_COLS), tok),
                  pl.BlockSpec((nseq, L, LANES), tok), pl.BlockSpec((nseq, L, LANES), tok),
                  pl.BlockSpec((nseq, M_HEADS, M_DK, M_DV), st4),
                  pl.BlockSpec((nseq, npairs, LANES), st3),
                  pl.BlockSpec((nseq, 1, LANES), st3)],
        out_specs=(pl.BlockSpec((nseq, L, V_COLS), tok),
                   pl.BlockSpec((nseq, M_HEADS, M_DK, M_DV), st4),
                   pl.BlockSpec((nseq, npairs, LANES), st3),
                   pl.BlockSpec((nseq, 1, LANES), st3)),
        scratch_shapes=[pltpu.VMEM((nseq, npairs, 2 * M_DK, 2 * M_DV), F32),
                        pltpu.VMEM((nseq, npairs, LANES), F32),
                        pltpu.VMEM((nseq, 1, LANES), F32)],
        compiler_params=_params(2),
        name="mlstm_chunks",
    )(tok3(qk), tok3(v), tok3(li), tok3(lf), c0, n0, m0)
    return h.reshape(B * T, V_COLS), c_new, n_new, m_new


def _route(lg):
    lane = lax.broadcasted_iota(jnp.int32, lg.shape, 1)
    lanef = lane.astype(F32)
    neg = -jnp.inf
    far = float(LANES)
    gm = lane < N_GROUPS
    lgm = jnp.where(gm, lg, neg)
    gmax = jnp.max(lgm, axis=-1, keepdims=True)
    gsum = jnp.sum(jnp.exp(lgm - gmax), axis=-1, keepdims=True)
    g_w = 1.0 / gsum
    gidx = jnp.min(jnp.where(gm & (lg == gmax), lanef, far), axis=-1, keepdims=True)
    first = N_GROUPS + EXPERTS_PER_GROUP * gidx
    sel = (lanef >= first) & (lanef < first + EXPERTS_PER_GROUP)
    l1 = jnp.max(jnp.where(sel, lg, neg), axis=-1, keepdims=True)
    i1 = jnp.min(jnp.where(sel & (lg == l1), lanef, far), axis=-1, keepdims=True)
    sel2 = sel & (lanef != i1)
    l2 = jnp.max(jnp.where(sel2, lg, neg), axis=-1, keepdims=True)
    i2 = jnp.min(jnp.where(sel2 & (lg == l2), lanef, far), axis=-1, keepdims=True)
    r = jnp.exp(l2 - l1)
    w1 = g_w / (1.0 + r)
    w2 = w1 * r
    return jnp.where(lanef == i1, w1, jnp.where(lanef == i2, w2, 0.0)), gidx


def _post_kernel(mlstm, routed, expand, *refs):
    refs = list(refs)
    h_ref = refs.pop(0)
    o_ref = refs.pop(0) if mlstm else None
    x_ref, g1_ref, sh2_ref, sc2_ref, nf_ref = refs[:5]
    refs = refs[5:]
    hn_ref = refs.pop(0) if mlstm else None
    wout_ref, wrhl_ref, br_ref = refs[:3]
    refs = refs[3:]
    tri_ref = refs.pop(0) if routed else None
    rep_ref = refs.pop(0) if expand else None
    x1_ref, u2_ref, gate_ref = refs[:3]
    outs = refs[3:]
    if mlstm:
        hf = h_ref[...].astype(F32)
        parts = [_unit_rms(hf[:, M_DV * i:M_DV * (i + 1)]) for i in range(M_HEADS)]
        hn = jnp.concatenate(parts, 1) * hn_ref[...]
        hg = (hn * o_ref[...].astype(F32)).astype(BF16)
    else:
        hg = h_ref[...]
    x1 = x_ref[...] + _mod_rows(g1_ref, rep_ref) * _dot(hg, wout_ref[...])
    x1_ref[...] = x1
    u2 = _unit_rms(x1) * (nf_ref[...] * (1.0 + _mod_rows(sc2_ref, rep_ref))) + _mod_rows(sh2_ref, rep_ref)
    uh, ul = _split2(u2)
    wrh = wrhl_ref[:, 0:LANES]
    lg = _dot(uh, wrh) + _dot(ul, wrh) + _dot(uh, wrhl_ref[:, LANES:2 * LANES]) + br_ref[...]
    gate, gidx = _route(lg)
    u2_ref[...] = uh
    gate_ref[...] = gate
    if not routed:
        return
    gid_ref, col_ref = outs
    rows = []
    for blk in range(x1.shape[0] // LANES):
        col = jnp.broadcast_to(gidx[LANES * blk:LANES * (blk + 1), :], (LANES, LANES))
        rows.append(col.T[0:1, :])
    gid_ref[...] = jnp.concatenate(rows, 1)
    lanef = lax.broadcasted_iota(jnp.int32, gate.shape, 1).astype(F32)
    onehot = jnp.where(lanef == gidx, 1.0, 0.0)
    earlier = _dot(tri_ref[...], onehot.astype(BF16))
    rank = jnp.sum(onehot * earlier, axis=-1, keepdims=True)
    col_ref[...] = jnp.where(lanef == 0.0, gidx, jnp.where(lanef == 1.0, rank, 0.0))


def _post_call(mlstm, routed, h, o, x, mod, mspec, rep, nf, hn, wout, wrhl, br, tm):
    n = x.shape[0]
    expand = rep is not None
    row = lambda i: (i, 0)
    const = lambda i: (0, 0)
    tok = pl.BlockSpec((tm, D_MODEL), row)
    small = pl.BlockSpec((tm, LANES), row)
    vec = pl.BlockSpec((1, D_MODEL), const)
    ins = [h] + ([o] if mlstm else []) + [x, mod, mod, mod, nf] + ([hn] if mlstm else []) \
        + [wout, wrhl, br]
    specs = [tok] + ([tok] if mlstm else []) + [tok, mspec(2), mspec(3), mspec(4), vec] \
        + ([vec] if mlstm else []) \
        + [pl.BlockSpec((D_MODEL, D_MODEL), const), pl.BlockSpec((D_MODEL, 2 * LANES), const),
           pl.BlockSpec((1, LANES), const)]
    out_shape = [jax.ShapeDtypeStruct((n, D_MODEL), F32), jax.ShapeDtypeStruct((n, D_MODEL), BF16),
                 jax.ShapeDtypeStruct((n, LANES), F32)]
    out_specs = [tok, tok, small]
    if routed:
        t = jnp.arange(tm)
        ins.append((t[None, :] < t[:, None]).astype(BF16))
        specs.append(pl.BlockSpec((tm, tm), const))
        out_shape += [jax.ShapeDtypeStruct((n // tm, 1, tm), F32), jax.ShapeDtypeStruct((n, LANES), F32)]
        out_specs += [pl.BlockSpec((None, 1, tm), lambda i: (i, 0, 0)), small]
    if expand:
        ins.append(rep)
        specs.append(pl.BlockSpec(rep.shape, const))
    out_shape, out_specs = tuple(out_shape), tuple(out_specs)
    return pl.pallas_call(
        functools.partial(_post_kernel, mlstm, routed, expand),
        out_shape=out_shape,
        grid=(n // tm,),
        in_specs=specs,
        out_specs=out_specs,
        compiler_params=_params(1),
        name="post_mlstm" if mlstm else "post_attn",
    )(*ins)


ITEM_ROWS = 8
RUN_ALIGN = 16
RUN_SIZES = (512, 256, 128, 64, 32, 16)
RUN_SRC, RUN_DST, RUN_LEN, RUN_TOTAL = 0, N_GROUPS, 2 * N_GROUPS, 3 * N_GROUPS


def _stage_rows(tm):
    assert N_GROUPS * (RUN_ALIGN - 1) <= LANES
    return tm + LANES


def _sorted_rows(n, tm):
    rows = n + N_GROUPS * (n // tm) * RUN_ALIGN + tm
    return -(-rows // tm) * tm


def _sort_meta_kernel(tm, nts, gid_ref, q_ref, runs_ref, items_ref):
    ntiles = gid_ref.shape[0]
    gid = gid_ref[...]
    r = lax.broadcasted_iota(jnp.int32, (tm, tm), 0)
    c = lax.broadcasted_iota(jnp.int32, (tm, tm), 1)
    before = jnp.where(r < c, 1.0, 0.0).astype(BF16)
    trow = lax.broadcasted_iota(jnp.int32, (ntiles, LANES), 0)
    lane = lax.broadcasted_iota(jnp.int32, (ntiles, LANES), 1)
    k = lax.broadcasted_iota(jnp.int32, (1, LANES), 1).astype(F32)
    zero11 = jnp.zeros((1, 1), F32)
    q = jnp.zeros((ntiles, tm), F32)
    runs = jnp.zeros((ntiles, LANES), F32)
    src = jnp.zeros((ntiles, 1), F32)
    start, nitems = zero11, zero11
    grp = jnp.zeros((1, LANES), F32)
    tile = jnp.zeros((1, LANES), F32)
    valid = jnp.zeros((1, LANES), F32)
    for g in range(N_GROUPS):
        mask = jnp.where(gid == float(g), 1.0, 0.0)
        cnt = jnp.sum(mask, axis=1, keepdims=True)
        padded = jnp.floor((cnt + (RUN_ALIGN - 1.0)) * (1.0 / RUN_ALIGN)) * RUN_ALIGN
        incl = jnp.broadcast_to(padded, (ntiles, LANES))
        d = 1
        while d < ntiles:
            incl = incl + jnp.where(trow >= d, pltpu.roll(incl, d, axis=0), 0.0)
            d *= 2
        total = incl[ntiles - 1:ntiles, 0:1]
        dst = start + incl[:, 0:1] - padded
        q = q + mask * (src + _dot(mask.astype(BF16), before))
        runs = runs + jnp.where(lane == RUN_SRC + g, src, 0.0) + jnp.where(lane == RUN_DST + g, dst, 0.0) \
            + jnp.where(lane == RUN_LEN + g, padded, 0.0)
        src = src + padded
        end = start + total
        ft = jnp.floor(start * (1.0 / tm))
        lt = jnp.floor((end - 1.0) * (1.0 / tm))
        ni = jnp.where(total > 0.0, lt - ft + 1.0, 0.0)
        inside = (k >= nitems) & (k < nitems + ni)
        grp = grp + jnp.where(inside, float(g), 0.0)
        tile = tile + jnp.where(inside, ft + (k - nitems), 0.0)
        valid = valid + jnp.where(inside, 1.0, 0.0)
        nitems = nitems + ni
        start = end
    runs = runs + jnp.where(lane == RUN_TOTAL, start, 0.0)
    live = valid > 0.0
    prev = pltpu.roll(tile, 1, axis=1)
    nxt = pltpu.roll(tile, LANES - 1, axis=1)
    first = live & ((k == 0.0) | (tile != prev))
    last = live & ((k == nitems - 1.0) | (tile != nxt))
    spare = jnp.floor((start - 1.0) * (1.0 / tm)) + 1.0 + (k - nitems)
    fill = jnp.logical_not(live) & (spare <= nts - 1.0)
    tile = jnp.where(live, tile, jnp.minimum(spare, nts - 1.0))
    grp = jnp.where(live, grp, jnp.max(grp, axis=1, keepdims=True))
    flag = lambda m: jnp.where(m, 1.0, 0.0)
    q_ref[...] = q
    runs_ref[...] = runs.astype(jnp.int32)
    table = jnp.concatenate([tile, grp, valid, flag(first | fill), flag(last | fill),
                             jnp.zeros((ITEM_ROWS - 5, LANES), F32)], 0)
    items_ref[...] = table.astype(jnp.int32)


def _max_items(nts):
    return nts + N_GROUPS - 1


def _sort_meta_call(gid, tm):
    ntiles = gid.shape[0]
    nts = _sorted_rows(ntiles * tm, tm) // tm
    assert _max_items(nts) <= LANES and tm <= RUN_SIZES[0]
    return pl.pallas_call(
        functools.partial(_sort_meta_kernel, tm, nts),
        out_shape=(jax.ShapeDtypeStruct((ntiles, tm), F32),
                   jax.ShapeDtypeStruct((ntiles, LANES), jnp.int32),
                   jax.ShapeDtypeStruct((ITEM_ROWS, LANES), jnp.int32)),
        compiler_params=_params(0),
        name="moe_sort_meta",
    )(gid)


def _run_copies(runs_ref, i, tile_ref, sorted_hbm, sem, to_sorted):
    pieces = []
    for g in range(N_GROUPS):
        src = runs_ref[i, RUN_SRC + g]
        dst = runs_ref[i, RUN_DST + g]
        length = runs_ref[i, RUN_LEN + g]
        for s in RUN_SIZES:
            def build(src=src, dst=dst, length=length, s=s):
                off = length & (-2 * s)
                a = tile_ref.at[pl.ds(pl.multiple_of(src + off, RUN_ALIGN), s), :]
                b = sorted_hbm.at[pl.ds(pl.multiple_of(dst + off, RUN_ALIGN), s), :]
                return pltpu.make_async_copy(a, b, sem) if to_sorted else pltpu.make_async_copy(b, a, sem)
            pieces.append(((length & s) != 0, build))
    return pieces


def _start(pieces):
    for idx, (pred, build) in enumerate(pieces):
        pl.when(pred)(lambda build=build, idx=idx: build().start(priority=idx % 2))


def _wait(pieces):
    for pred, build in pieces:
        pl.when(pred)(lambda build=build: build().wait())


def _start_then_wait(pieces):
    _start(pieces)
    _wait(pieces)


def _scatter_kernel(tm, nts, ntiles, runs_ref, q_ref, u_ref, g_ref, su_hbm, sg_hbm,
                    stu, stg, sem_u, sem_g):
    i = pl.program_id(0)
    slot = i % 2

    def copies(tile, slot):
        return (_run_copies(runs_ref, tile, stu.at[slot], su_hbm, sem_u.at[slot], True)
                + _run_copies(runs_ref, tile, stg.at[slot], sg_hbm, sem_g.at[slot], True))

    r = lax.broadcasted_iota(jnp.int32, (_stage_rows(tm), tm), 0).astype(F32)
    perm = jnp.where(q_ref[...] == r, 1.0, 0.0).astype(BF16)
    stu[slot] = _dot(perm, u_ref[...]).astype(BF16)
    both = _dot(perm, jnp.concatenate(_split2(g_ref[...]), 1))
    stg[slot] = both[:, 0:LANES] + both[:, LANES:2 * LANES]
    _start(copies(i, slot))

    @pl.when(i > 0)
    def _():
        _wait(copies(i - 1, 1 - slot))

    @pl.when(i == ntiles - 1)
    def _():
        _wait(copies(i, slot))
        stu[slot] = jnp.zeros(stu.shape[1:], stu.dtype)
        stg[slot] = jnp.zeros(stg.shape[1:], stg.dtype)
        total = runs_ref[i, RUN_TOTAL]
        tail = nts * tm - total
        nfull = tail // tm
        pieces = []
        for tile_ref, hbm, sem in ((stu.at[slot], su_hbm, sem_u.at[slot]),
                                   (stg.at[slot], sg_hbm, sem_g.at[slot])):
            for j in range(nts - ntiles):
                def full(j=j, tile_ref=tile_ref, hbm=hbm, sem=sem):
                    dst = pl.multiple_of(total + j * tm, RUN_ALIGN)
                    return pltpu.make_async_copy(tile_ref.at[pl.ds(0, tm), :], hbm.at[pl.ds(dst, tm), :], sem)
                pieces.append((j < nfull, full))
            rem = tail - nfull * tm
            for s in RUN_SIZES:
                if s >= tm:
                    continue
                def part(s=s, tile_ref=tile_ref, hbm=hbm, sem=sem):
                    dst = pl.multiple_of(total + nfull * tm + (rem & (-2 * s)), RUN_ALIGN)
                    return pltpu.make_async_copy(tile_ref.at[pl.ds(0, s), :], hbm.at[pl.ds(dst, s), :], sem)
                pieces.append(((rem & s) != 0, part))
        _start_then_wait(pieces)


def _scatter_call(runs, q3, u2, gate, tm):
    n = u2.shape[0]
    ns = _sorted_rows(n, tm)
    grid_spec = pltpu.PrefetchScalarGridSpec(
        num_scalar_prefetch=1,
        grid=(n // tm,),
        in_specs=[pl.BlockSpec((None, 1, tm), lambda i, *_: (i, 0, 0)),
                  pl.BlockSpec((tm, D_MODEL), lambda i, *_: (i, 0)),
                  pl.BlockSpec((tm, LANES), lambda i, *_: (i, 0))],
        out_specs=(pl.BlockSpec(memory_space=pl.ANY), pl.BlockSpec(memory_space=pl.ANY)),
        scratch_shapes=[pltpu.VMEM((2, _stage_rows(tm), D_MODEL), BF16),
                        pltpu.VMEM((2, _stage_rows(tm), LANES), F32),
                        pltpu.SemaphoreType.DMA((2,)), pltpu.SemaphoreType.DMA((2,))])
    return pl.pallas_call(
        functools.partial(_scatter_kernel, tm, ns // tm, n // tm),
        out_shape=(jax.ShapeDtypeStruct((ns, D_MODEL), BF16), jax.ShapeDtypeStruct((ns, LANES), F32)),
        grid_spec=grid_spec,
        compiler_params=_params(1),
        name="moe_scatter",
    )(runs, q3, u2, gate)


def _ungroup(tm, runs_ref, col_ref, ys_hbm, stage, sem):
    i = pl.program_id(0)
    slot = i % 2

    def copies(tile, slot):
        return _run_copies(runs_ref, tile, stage.at[slot], ys_hbm, sem.at[slot], False)

    @pl.when(i == 0)
    def _():
        stage[...] = jnp.zeros_like(stage)
        _start(copies(i, slot))

    @pl.when(i + 1 < pl.num_programs(0))
    def _():
        _start(copies(i + 1, 1 - slot))

    _wait(copies(i, slot))
    col = col_ref[...]
    gidx = col[:, 0:1]
    src = [runs_ref[i, RUN_SRC + g].astype(F32) for g in range(N_GROUPS)]
    first = src[N_GROUPS - 1]
    for g in range(N_GROUPS - 2, -1, -1):
        first = jnp.where(gidx == float(g), src[g], first)
    q = first + col[:, 1:2]
    lanes = lax.broadcasted_iota(jnp.int32, (tm, _stage_rows(tm)), 1).astype(F32)
    perm_t = jnp.where(q == lanes, 1.0, 0.0).astype(BF16)
    return _dot(perm_t, stage[slot])


def _moe_routed_kernel(items_ref, su_ref, sg_ref, wgu_ref, wd_ref, out_ref, acc_ref, wgu_b, wd_b):
    kk = pl.program_id(0)
    grp = items_ref[1, kk]

    @pl.when((kk == 0) | (grp != items_ref[1, jnp.maximum(kk - 1, 0)]))
    def _():
        for e in range(EXPERTS_PER_GROUP):
            wgu_b[e] = wgu_ref[e].astype(BF16)
            wd_b[e] = wd_ref[e].astype(BF16)

    @pl.when(items_ref[3, kk] == 1)
    def _():
        acc_ref[...] = jnp.zeros_like(acc_ref)

    @pl.when(items_ref[2, kk] == 1)
    def _():
        u = su_ref[...]
        gate = sg_ref[...]
        lane = lax.broadcasted_iota(jnp.int32, gate.shape, 1)
        first_lane = N_GROUPS + EXPERTS_PER_GROUP * grp
        acts = []
        for e in range(EXPERTS_PER_GROUP):
            gcol = jnp.sum(jnp.where(lane == first_lane + e, gate, 0.0), axis=-1, keepdims=True)
            hu = _dot(u, wgu_b[e])
            hg = hu[:, 0:D_EXPERT]
            acts.append((hg * jax.nn.sigmoid(hg) * hu[:, D_EXPERT:2 * D_EXPERT] * gcol).astype(BF16))
        wd_all = wd_b[...].reshape(EXPERTS_PER_GROUP * D_EXPERT, D_MODEL)
        acc_ref[...] += _dot(jnp.concatenate(acts, 1), wd_all)

    @pl.when(items_ref[4, kk] == 1)
    def _():
        out_ref[...] = acc_ref[...].astype(out_ref.dtype)


def _moe_routed_call(items, su, sg, wgu, wd, layer, tm):
    ns = su.shape[0]
    n_items = _max_items(ns // tm)
    depth = wgu.shape[0]
    wgu5 = wgu.reshape(depth, N_GROUPS, EXPERTS_PER_GROUP, D_MODEL, 2 * D_EXPERT)
    wd5 = wd.reshape(depth, N_GROUPS, EXPERTS_PER_GROUP, D_EXPERT, D_MODEL)
    grid_spec = pltpu.PrefetchScalarGridSpec(
        num_scalar_prefetch=1,
        grid=(n_items,),
        in_specs=[pl.BlockSpec((tm, D_MODEL), lambda k, it: (it[0, k], 0)),
                  pl.BlockSpec((tm, LANES), lambda k, it: (it[0, k], 0)),
                  pl.BlockSpec((None, None, EXPERTS_PER_GROUP, D_MODEL, 2 * D_EXPERT),
                               lambda k, it: (layer, it[1, k], 0, 0, 0)),
                  pl.BlockSpec((None, None, EXPERTS_PER_GROUP, D_EXPERT, D_MODEL),
                               lambda k, it: (layer, it[1, k], 0, 0, 0))],
        out_specs=pl.BlockSpec((tm, D_MODEL), lambda k, it: (it[0, k], 0)),
        scratch_shapes=[pltpu.VMEM((tm, D_MODEL), F32),
                        pltpu.VMEM((EXPERTS_PER_GROUP, D_MODEL, 2 * D_EXPERT), BF16),
                        pltpu.VMEM((EXPERTS_PER_GROUP, D_EXPERT, D_MODEL), BF16)])
    return pl.pallas_call(
        _moe_routed_kernel,
        out_shape=jax.ShapeDtypeStruct((ns, D_MODEL), BF16),
        grid_spec=grid_spec,
        compiler_params=_params(1),
        name="moe_routed",
    )(items, su, sg, wgu5, wd5)


def _resid_gather_kernel(tm, runs_ref, x_ref, g2_ref, col_ref, ys_hbm, y_ref, stage, sem):
    y_ref[...] = x_ref[...] + g2_ref[...] * _ungroup(tm, runs_ref, col_ref, ys_hbm, stage, sem)


def _resid_gather_call(x, runs, col, ys, mod, mspec, tm):
    n = x.shape[0]
    row = lambda i, *_: (i, 0)
    grid_spec = pltpu.PrefetchScalarGridSpec(
        num_scalar_prefetch=1,
        grid=(n // tm,),
        in_specs=[pl.BlockSpec((tm, D_MODEL), row), mspec(5), pl.BlockSpec((tm, LANES), row),
                  pl.BlockSpec(memory_space=pl.ANY)],
        out_specs=pl.BlockSpec((tm, D_MODEL), row),
        scratch_shapes=[pltpu.VMEM((2, _stage_rows(tm), D_MODEL), BF16), pltpu.SemaphoreType.DMA((2,))])
    return pl.pallas_call(
        functools.partial(_resid_gather_kernel, tm),
        out_shape=jax.ShapeDtypeStruct((n, D_MODEL), F32),
        grid_spec=grid_spec,
        compiler_params=_params(1),
        name="moe_unsort_resid",
    )(runs, x, mod, col, ys)


def _moe_kernel(expand, u_ref, gate_ref, wgu_ref, wd_ref, x1_ref, g2_ref, *rest):
    rep_ref = rest[0] if expand else None
    y_ref, acc_ref = rest[1 if expand else 0:]
    grp = pl.program_id(1)

    @pl.when(grp == 0)
    def _():
        acc_ref[...] = jnp.zeros_like(acc_ref)

    u = u_ref[...]
    gate = gate_ref[...]
    lane = lax.broadcasted_iota(jnp.int32, gate.shape, 1)
    first_lane = N_GROUPS + EXPERTS_PER_GROUP * grp
    acts = []
    for e in range(EXPERTS_PER_GROUP):
        hu = _dot(u, wgu_ref[e].astype(BF16))
        hg = hu[:, 0:D_EXPERT]
        gcol = jnp.sum(jnp.where(lane == first_lane + e, gate, 0.0), axis=-1, keepdims=True)
        acts.append((hg * jax.nn.sigmoid(hg) * hu[:, D_EXPERT:2 * D_EXPERT] * gcol).astype(BF16))
    wd_all = wd_ref[...].astype(BF16).reshape(EXPERTS_PER_GROUP * D_EXPERT, D_MODEL)
    acc_ref[...] += _dot(jnp.concatenate(acts, 1), wd_all)

    @pl.when(grp == N_GROUPS - 1)
    def _():
        y_ref[...] = x1_ref[...] + _mod_rows(g2_ref, rep_ref) * acc_ref[...]


def _moe_call(u2, gate, wgu, wd, layer, x1, mod, mspec, rep, tm):
    n = u2.shape[0]
    row = lambda i, e: (i, 0)
    expand = rep is not None
    depth = wgu.shape[0]
    wgu5 = wgu.reshape(depth, N_GROUPS, EXPERTS_PER_GROUP, D_MODEL, 2 * D_EXPERT)
    wd5 = wd.reshape(depth, N_GROUPS, EXPERTS_PER_GROUP, D_EXPERT, D_MODEL)
    return pl.pallas_call(
        functools.partial(_moe_kernel, expand),
        out_shape=jax.ShapeDtypeStruct((n, D_MODEL), F32),
        grid=(n // tm, N_GROUPS),
        in_specs=[pl.BlockSpec((tm, D_MODEL), row), pl.BlockSpec((tm, LANES), row),
                  pl.BlockSpec((None, None, EXPERTS_PER_GROUP, D_MODEL, 2 * D_EXPERT),
                               lambda i, g: (layer, g, 0, 0, 0)),
                  pl.BlockSpec((None, None, EXPERTS_PER_GROUP, D_EXPERT, D_MODEL),
                               lambda i, g: (layer, g, 0, 0, 0)),
                  pl.BlockSpec((tm, D_MODEL), row), mspec(5)]
        + ([pl.BlockSpec(rep.shape, lambda i, e: (0, 0))] if expand else []),
        out_specs=pl.BlockSpec((tm, D_MODEL), row),
        scratch_shapes=[pltpu.VMEM((tm, D_MODEL), F32)],
        compiler_params=_params(2),
        name="moe_dense",
    )(u2, gate, wgu5, wd5, x1, mod, *([rep] if expand else []))


def _rope128(x, cos, sa, sb):
    return x * cos + pltpu.roll(x, LANES - ROT_DIM // 2, axis=1) * sa \
        + pltpu.roll(x, ROT_DIM // 2, axis=1) * sb


def _kvq_kernel(gather_tm, *refs):
    if gather_tm:
        (runs_ref, x_ref, g2_ref, col_ref, ys_hbm, kvsh_ref, kvsc_ref, sh1_ref, sc1_ref, kvn_ref,
         nm_ref, wkv_ref, wq_ref, kn_ref, qn_ref, cos_ref, sa_ref, sb_ref, g64_ref, gr_ref, gb_ref,
         q_ref, k_ref, v_ref, xa_ref, stage, sem) = refs
        x = x_ref[...] + g2_ref[...] * _ungroup(gather_tm, runs_ref, col_ref, ys_hbm, stage, sem)
        xa_ref[...] = x
        rep_ref = None
    else:
        (x_ref, kvsh_ref, kvsc_ref, sh1_ref, sc1_ref, kvn_ref, nm_ref,
         wkv_ref, wq_ref, kn_ref, qn_ref, cos_ref, sa_ref, sb_ref, g64_ref, gr_ref, gb_ref,
         rep_ref, q_ref, k_ref, v_ref) = refs
        x = x_ref[...]
    xn = _unit_rms(x)
    cos, sa, sb = cos_ref[...], sa_ref[...], sb_ref[...]

    ukv = xn * (kvn_ref[...] * (1.0 + _mod_rows(kvsc_ref, rep_ref))) + _mod_rows(kvsh_ref, rep_ref)
    kv = _dot(ukv.astype(BF16), wkv_ref[...])
    k = kv[:, 0:LANES]
    v_ref[...] = kv[:, LANES:2 * LANES]
    kh, kl = _split2(k * k)
    ms = _dot(kh, g64_ref[...]) + _dot(kl, g64_ref[...])
    k_ref[...] = _rope128(k * lax.rsqrt(ms + RMS_EPS) * kn_ref[...], cos, sa, sb)

    u1 = xn * (nm_ref[...] * (1.0 + _mod_rows(sc1_ref, rep_ref))) + _mod_rows(sh1_ref, rep_ref)
    q = _dot(u1.astype(BF16), wq_ref[...])
    ms16 = _dot((q * q).astype(BF16), gr_ref[...])
    rh, rl = _split2(lax.rsqrt(ms16 + RMS_EPS))
    rsb = _dot(jnp.concatenate([rh, rl], 1), gb_ref[...])
    qn = q * rsb * qn_ref[...]
    scale = HEAD_DIM ** -0.5
    cos_q, sa_q, sb_q = cos * scale, sa * scale, sb * scale
    for i in range(D_MODEL // LANES):
        sl = slice(LANES * i, LANES * (i + 1))
        q_ref[:, sl] = _rope128(qn[:, sl], cos_q, sa_q, sb_q).astype(BF16)


def _kvq_call(x, routed, mod0, mod1, mspec, kvmod, kvspec, rep, kvn, nm, wkv, wq, kn, qn, tabs, tab,
              g64, gr, gb, tm):
    n = x.shape[0]
    row = lambda i, *_: (i, 0)
    const = lambda i, *_: (0, 0)
    vec = pl.BlockSpec((1, D_MODEL), const)
    tok = pl.BlockSpec((tm, D_MODEL), row)
    small = pl.BlockSpec((tm, LANES), row)
    ins = [x, kvmod, kvmod, mod1, mod1, kvn, nm, wkv, wq, kn, qn, *tabs, g64, gr, gb]
    specs = [tok, kvspec(0), kvspec(1), mspec(0), mspec(1), vec, vec,
             pl.BlockSpec((D_MODEL, 2 * LANES), const), pl.BlockSpec((D_MODEL, D_MODEL), const),
             pl.BlockSpec((1, LANES), const), vec, tab, tab, tab,
             pl.BlockSpec((LANES, LANES), const), pl.BlockSpec((D_MODEL, LANES), const),
             pl.BlockSpec((2 * LANES, D_MODEL), const)]
    out_shape = [jax.ShapeDtypeStruct((n, D_MODEL), BF16), jax.ShapeDtypeStruct((n, LANES), F32),
                 jax.ShapeDtypeStruct((n, LANES), F32)]
    out_specs = [tok, small, small]
    scratch = []
    nprefetch = 0
    if routed is not None:
        runs, col, ys = routed
        nprefetch = 1
        ins = [runs, x, mod0, col, ys] + ins[1:]
        specs = [tok, mspec(5), small, pl.BlockSpec(memory_space=pl.ANY)] + specs[1:]
        out_shape.append(jax.ShapeDtypeStruct((n, D_MODEL), F32))
        out_specs.append(tok)
        scratch = [pltpu.VMEM((2, _stage_rows(tm), D_MODEL), BF16), pltpu.SemaphoreType.DMA((2,))]
    else:
        ins.append(rep)
        specs.append(pl.BlockSpec(rep.shape, const))
    grid_spec = pltpu.PrefetchScalarGridSpec(
        num_scalar_prefetch=nprefetch, grid=(n // tm,), in_specs=specs,
        out_specs=tuple(out_specs), scratch_shapes=scratch)
    return pl.pallas_call(
        functools.partial(_kvq_kernel, tm if routed is not None else 0),
        out_shape=tuple(out_shape),
        grid_spec=grid_spec,
        compiler_params=_params(1),
        name="kv_q_proj",
    )(*ins)


def _attn_core(q, kcat, vcat, bias, sinks_ref, o_ref):
    tk = kcat.shape[0]
    pairs = N_Q_HEADS // N_KV_HEADS // 2
    lane = lax.broadcasted_iota(jnp.int32, (1, LANES), 1)
    lo = lane < HEAD_DIM
    kro = pltpu.roll(kcat, HEAD_DIM, axis=1)
    vro = pltpu.roll(vcat, HEAD_DIM, axis=1)
    one_e = jnp.broadcast_to(jnp.where(lo, 1.0, 0.0), (tk, LANES))
    one_o = 1.0 - one_e
    for g in range(N_KV_HEADS):
        if g == 0:
            ke, ko = jnp.where(lo, kcat, 0.0), jnp.where(lo, 0.0, kro)
            ve, vo = jnp.where(lo, vcat, 0.0), jnp.where(lo, 0.0, vro)
        else:
            ke, ko = jnp.where(lo, kro, 0.0), jnp.where(lo, 0.0, kcat)
            ve, vo = jnp.where(lo, vro, 0.0), jnp.where(lo, 0.0, vcat)
        k2 = jnp.concatenate([ke, ko], 0).astype(BF16)
        v2 = jnp.concatenate([jnp.concatenate([ve, one_e], 1),
                              jnp.concatenate([vo, one_o], 1)], 0).astype(BF16)
        for p in range(pairs):
            hp = g * pairs + p
            s = _dot_nt(q[:, LANES * hp:LANES * (hp + 1)], k2)
            halves, corr = [], []
            for par in range(2):
                sp = s[:, par * tk:(par + 1) * tk] + bias
                sink = sinks_ref[2 * hp + par]
                m = jnp.maximum(jnp.max(sp, axis=-1, keepdims=True), sink)
                halves.append(jnp.exp(sp - m).astype(BF16))
                corr.append(jnp.exp(sink - m))
            o2 = _dot(jnp.concatenate(halves, 1), v2)
            den = o2[:, LANES:2 * LANES] + jnp.where(lo, corr[0], corr[1])
            o_ref[:, LANES * hp:LANES * (hp + 1)] = (o2[:, 0:LANES] / den).astype(o_ref.dtype)


ATTN_BLOCKS_PER_STEP = 8


def _attn_prompt_kernel(sinks_ref, q_ref, kp_ref, kc_ref, vp_ref, vc_ref, bias_ref, o_ref):
    kall = jnp.concatenate([kp_ref[...], kc_ref[...]], 0)
    vall = jnp.concatenate([vp_ref[...], vc_ref[...]], 0)
    first = jnp.minimum(pl.program_id(1), 1)
    for j in range(ATTN_BLOCKS_PER_STEP):
        rows = slice(j * WINDOW, (j + 1) * WINDOW)
        keys = slice(j * WINDOW, (j + 2) * WINDOW)
        bias = bias_ref[first] if j == 0 else bias_ref[1]
        _attn_core(q_ref[rows, :], kall[keys, :], vall[keys, :], bias, sinks_ref, o_ref.at[rows, :])


ATTN_SEQS_PER_STEP = 16


def _attn_sample_kernel(sinks_ref, q_ref, kc_ref, kn_ref, vc_ref, vn_ref, bias_ref, o_ref, kbuf, vbuf):
    @pl.when(pl.program_id(0) == 0)
    def _():
        kbuf[...] = jnp.zeros_like(kbuf)
        vbuf[...] = jnp.zeros_like(vbuf)

    for b in range(ATTN_SEQS_PER_STEP):
        kbuf[b, 0:WINDOW, :] = kc_ref[b]
        kbuf[b, WINDOW:WINDOW + SAMPLE_PAD, :] = kn_ref[b]
        vbuf[b, 0:WINDOW, :] = vc_ref[b]
        vbuf[b, WINDOW:WINDOW + SAMPLE_PAD, :] = vn_ref[b]
    for b in range(ATTN_SEQS_PER_STEP):
        _attn_core(q_ref[b], kbuf[b], vbuf[b], bias_ref[...], sinks_ref, o_ref.at[b])


def _window_bias(tq, first):
    qi = jnp.arange(tq)[:, None]
    kj = jnp.arange(2 * WINDOW)[None, :]
    ok = (kj > qi) & (kj <= qi + WINDOW)
    if first:
        ok = ok & (kj >= WINDOW)
    return jnp.where(ok, 0.0, -jnp.inf).astype(F32)


def _attn_prompt_call(sinks, q, k, v, B, nb):
    n = q.shape[0]
    per = ATTN_BLOCKS_PER_STEP
    steps = nb // per
    cur = lambda b, i: (b * steps + i, 0)
    prev = lambda b, i: (b * nb + jnp.maximum(per * i - 1, 0), 0)
    bias = jnp.stack([_window_bias(WINDOW, True), _window_bias(WINDOW, False)])
    kv_prev = pl.BlockSpec((WINDOW, LANES), prev)
    kv_cur = pl.BlockSpec((per * WINDOW, LANES), cur)
    return pl.pallas_call(
        _attn_prompt_kernel,
        out_shape=jax.ShapeDtypeStruct((n, D_MODEL), BF16),
        grid=(B, steps),
        in_specs=[pl.BlockSpec(memory_space=pltpu.SMEM),
                  pl.BlockSpec((per * WINDOW, D_MODEL), cur), kv_prev, kv_cur, kv_prev, kv_cur,
                  pl.BlockSpec((2, WINDOW, 2 * WINDOW), lambda b, i: (0, 0, 0))],
        out_specs=pl.BlockSpec((per * WINDOW, D_MODEL), cur),
        compiler_params=_params(2),
        name="swa_prompt",
    )(sinks, q, k, k, v, v, bias)


def _attn_sample_call(sinks, q, kcache, knew, vcache, vnew):
    B = q.shape[0]
    nseq = ATTN_SEQS_PER_STEP
    b3 = lambda b: (b, 0, 0)
    cache = pl.BlockSpec((nseq, WINDOW, LANES), b3)
    new = pl.BlockSpec((nseq, SAMPLE_PAD, LANES), b3)
    return pl.pallas_call(
        _attn_sample_kernel,
        out_shape=jax.ShapeDtypeStruct(q.shape, BF16),
        grid=(B // nseq,),
        in_specs=[pl.BlockSpec(memory_space=pltpu.SMEM),
                  pl.BlockSpec((nseq, SAMPLE_PAD, D_MODEL), b3), cache, new, cache, new,
                  pl.BlockSpec((SAMPLE_PAD, 2 * WINDOW), lambda b: (0, 0))],
        out_specs=pl.BlockSpec((nseq, SAMPLE_PAD, D_MODEL), b3),
        scratch_shapes=[pltpu.VMEM((nseq, 2 * WINDOW, LANES), F32),
                        pltpu.VMEM((nseq, 2 * WINDOW, LANES), F32)],
        compiler_params=_params(1),
        name="swa_sample",
    )(sinks, q, kcache, knew, vcache, vnew, _window_bias(SAMPLE_PAD, False))


def _rope_tables(pos):
    half = ROT_DIM // 2
    inv = ROPE_THETA ** (-jnp.arange(half, dtype=F32) / half)
    ang = pos.astype(F32)[:, None] * inv[None]
    cos, sin = jnp.cos(ang), jnp.sin(ang)
    d = jnp.arange(LANES) % HEAD_DIM
    idx = d % half
    cos_t = jnp.where(d < ROT_DIM, cos[:, idx], 1.0)
    sa = jnp.where(d < half, -sin[:, idx], 0.0)
    sb = jnp.where((d >= half) & (d < ROT_DIM), sin[:, idx], 0.0)
    return cos_t, sa, sb


def _pad_lanes(a, value=0.0):
    return jnp.pad(a, ((0, 0), (0, LANES - a.shape[1])), constant_values=value)


def _prep_weights(ada_w, ada_b, norm_mix, norm_ffn, a_w_in, a_b_gates, a_head_norm, a_w_out,
                  kv_ada_w, kv_ada_b, kv_norm, w_k, w_v, k_norm, b_w_q, b_q_norm, b_sinks, b_w_o,
                  moe_w_group, moe_b_group, moe_w_expert, moe_b_expert, moe_w_gate_up, moe_w_down):
    w = {}
    g0 = QK_COLS + 2 * V_COLS
    w["w_in"] = a_w_in
    w["w_gates"] = jnp.concatenate([_pad_lanes(a_w_in[0, :, g0:g0 + M_HEADS]),
                                    _pad_lanes(a_w_in[0, :, g0 + M_HEADS:])], 1)
    w["bli"] = _pad_lanes(a_b_gates[0][None, :M_HEADS])
    w["blf"] = _pad_lanes(a_b_gates[0][None, M_HEADS:])
    w["head_norm"] = a_head_norm[0][None]
    w["w_out"] = a_w_out[0].astype(BF16)
    w["norm_mix"] = [norm_mix[l][None] for l in range(2)]
    w["norm_ffn"] = [norm_ffn[l][None] for l in range(2)]
    w["router"] = []
    for l in range(2):
        wr = _pad_lanes(jnp.concatenate([moe_w_group[l], moe_w_expert[l]], 1))
        hi = wr.astype(BF16)
        lo = (wr - hi.astype(F32)).astype(BF16)
        br = _pad_lanes(jnp.concatenate([moe_b_group[l], moe_b_expert[l]])[None])
        w["router"].append((jnp.concatenate([hi, lo], 1), br))
    w["w_gu"] = moe_w_gate_up
    w["w_d"] = moe_w_down
    w["kv_norm"] = kv_norm[None]
    w["w_kv"] = jnp.concatenate([w_k, w_v], 1).astype(BF16)
    w["k_norm"] = jnp.tile(k_norm, N_KV_HEADS)[None]
    w["w_q"] = b_w_q[0].astype(BF16)
    w["q_norm"] = jnp.tile(b_q_norm[0], N_Q_HEADS)[None]
    w["sinks"] = b_sinks[0]
    w["w_o"] = b_w_o[0].astype(BF16)
    lanes = jnp.arange(LANES)
    feat = jnp.arange(D_MODEL)
    w["g64"] = jnp.where((lanes[:, None] // HEAD_DIM) == (lanes[None, :] // HEAD_DIM),
                         1.0 / HEAD_DIM, 0.0).astype(BF16)
    w["gr"] = jnp.where((feat[:, None] // HEAD_DIM) == lanes[None, :], 1.0 / HEAD_DIM, 0.0).astype(BF16)
    gb = jnp.where(lanes[:, None] == (feat[None, :] // HEAD_DIM), 1.0, 0.0).astype(BF16)
    w["gb"] = jnp.concatenate([gb, gb], 0)
    return w


def _trunk(x2, mods, kvmod, mspec, mspec_big, rep, w, *, B, T, L, tm, tm_big, tabs, tabspec, c0, n0, m0,
           cache_k=None, cache_v=None):
    sample = cache_k is not None
    nc = T // L if not sample else 1

    qk, v, o, li, lf = _inproj_call(x2, mods[0], mspec_big, rep, w["norm_mix"][0], w["w_in"],
                                    w["w_gates"], w["bli"], w["blf"], tm_big)
    if sample:
        def padtok(a, value=0.0):
            a = a.reshape(B, T, a.shape[-1])
            a = jnp.pad(a, ((0, 0), (0, L - T), (0, 0)), constant_values=value)
            return a.reshape(B * L, a.shape[-1])
        qk, v, li, lf = padtok(qk), padtok(v), padtok(li, M_EMPTY), padtok(lf)
    h, c_new, n_new, m_new = _mlstm_call(qk, v, li, lf, c0, n0, m0, B, nc, L)
    if sample:
        h = h.reshape(B, L, V_COLS)[:, :T].reshape(B * T, V_COLS)
    routed = not sample
    ntiles = (B * T) // tm

    def routed_moe(u2, gate, gid, col, layer):
        q, runs, items = _sort_meta_call(gid.reshape(ntiles, tm), tm)
        su, sg = _scatter_call(runs, q.reshape(ntiles, 1, tm), u2, gate, tm)
        ys = _moe_routed_call(items, su, sg, w["w_gu"], w["w_d"], layer, tm)
        return runs, col, ys

    post0 = _post_call(True, routed, h, o, x2, mods[0], mspec, rep, w["norm_ffn"][0], w["head_norm"],
                       w["w_out"], *w["router"][0], tm)
    kvq_args = (mods[0], mods[1], mspec, kvmod, mspec, rep, w["kv_norm"], w["norm_mix"][1],
                w["w_kv"], w["w_q"], w["k_norm"], w["q_norm"], tabs, tabspec,
                w["g64"], w["gr"], w["gb"], tm)

    if routed:
        x1, u2, gate, gid, col = post0
        q, k, vv, xa = _kvq_call(x1, routed_moe(u2, gate, gid, col, 0), *kvq_args)
    else:
        x1, u2, gate = post0
        xa = _moe_call(u2, gate, w["w_gu"], w["w_d"], 0, x1, mods[0], mspec_big, rep, tm_big)
        q, k, vv = _kvq_call(xa, None, *kvq_args)
    if not sample:
        att = _attn_prompt_call(w["sinks"], q, k, vv, B, T // WINDOW)
        k_win = k.reshape(B, T, LANES)[:, T - WINDOW:].reshape(B, WINDOW, N_KV_HEADS, HEAD_DIM)
        v_win = vv.reshape(B, T, LANES)[:, T - WINDOW:].reshape(B, WINDOW, N_KV_HEADS, HEAD_DIM)
    else:
        def padseq(a):
            return jnp.pad(a.reshape(B, T, a.shape[-1]), ((0, 0), (0, SAMPLE_PAD - T), (0, 0)))
        kc = cache_k.reshape(B, WINDOW, LANES)
        vc = cache_v.reshape(B, WINDOW, LANES)
        att = _attn_sample_call(w["sinks"], padseq(q), kc, padseq(k), vc, padseq(vv))
        att = att[:, :T].reshape(B * T, D_MODEL)
        k_win = jnp.concatenate([kc[:, T:], k.reshape(B, T, LANES)], 1)
        v_win = jnp.concatenate([vc[:, T:], vv.reshape(B, T, LANES)], 1)
        k_win = k_win.reshape(B, WINDOW, N_KV_HEADS, HEAD_DIM)
        v_win = v_win.reshape(B, WINDOW, N_KV_HEADS, HEAD_DIM)
    post1 = _post_call(False, routed, att, None, xa, mods[1], mspec, rep, w["norm_ffn"][1], None,
                       w["w_o"], *w["router"][1], tm)
    if routed:
        x3, u4, gate, gid, col = post1
        runs, col, ys = routed_moe(u4, gate, gid, col, 1)
        y = _resid_gather_call(x3, runs, col, ys, mods[1], mspec, tm)
    else:
        x3, u4, gate = post1
        y = _moe_call(u4, gate, w["w_gu"], w["w_d"], 1, x3, mods[1], mspec_big, rep, tm_big)
    c_out = c_new[None]
    n_out = n_new.reshape(1, B, M_HEADS, M_DK)
    m_out = m_new[:, 0, :M_HEADS][None]
    return y, c_out, n_out, m_out, k_win, v_win


def kernel(x_prompt, x_sample, c_prompt, c_sample, state_c, state_n, state_m, cache_k_win, cache_v_win, ada_w, ada_b, norm_mix, norm_ffn, a_w_in, a_b_gates, a_head_norm, a_w_out, kv_ada_w, kv_ada_b, kv_norm, w_k, w_v, k_norm, b_w_q, b_q_norm, b_sinks, b_w_o, moe_w_group, moe_b_group, moe_w_expert, moe_b_expert, moe_w_gate_up, moe_w_down):
    Bp, Tp, D = x_prompt.shape
    Bs, Ts, _ = x_sample.shape
    w = _prep_weights(ada_w, ada_b, norm_mix, norm_ffn, a_w_in, a_b_gates, a_head_norm, a_w_out,
                      kv_ada_w, kv_ada_b, kv_norm, w_k, w_v, k_norm, b_w_q, b_q_norm, b_sinks, b_w_o,
                      moe_w_group, moe_b_group, moe_w_expert, moe_b_expert, moe_w_gate_up, moe_w_down)

    rows = Bp + Bs
    rpad = -rows % 8
    c_all = jnp.concatenate([c_prompt, c_sample, jnp.zeros((rpad, D), F32)], 0)
    mod = _ada_call(c_all, ada_w, ada_b[:, None, :])
    kvm = _ada_call(c_all, kv_ada_w[None], kv_ada_b[None, None, :])

    tm_p = TOKEN_TILE
    tiles_per_seq = Tp // tm_p
    mods_p = [mod[l, :Bp][:, None, :] for l in range(2)]
    kvmod_p = kvm[0, :Bp][:, None, :]

    def mspec_p(col):
        return pl.BlockSpec((None, 1, D_MODEL), lambda i, *_: (i // tiles_per_seq, 0, col))

    big_tiles_per_seq = Tp // BIG_TOKEN_TILE

    def mspec_big_p(col):
        return pl.BlockSpec((None, 1, D_MODEL), lambda i, *_: (i // big_tiles_per_seq, 0, col))

    tabs_p = _rope_tables(jnp.arange(Tp, dtype=jnp.int32))
    tabspec_p = pl.BlockSpec((tm_p, LANES), lambda i, *_: (i % tiles_per_seq, 0))
    npairs = M_HEADS // 2
    c0 = jnp.zeros((Bp, M_HEADS, M_DK, M_DV), F32)
    n0 = jnp.zeros((Bp, npairs, LANES), F32)
    m0 = jnp.pad(jnp.full((Bp, 1, M_HEADS), M_EMPTY, F32), ((0, 0), (0, 0), (0, LANES - M_HEADS)))
    yp, cp, np_, mp, kwp, vwp = _trunk(
        x_prompt.reshape(Bp * Tp, D), mods_p, kvmod_p, mspec_p, mspec_big_p, None, w,
        B=Bp, T=Tp, L=M_CHUNK, tm=tm_p, tm_big=BIG_TOKEN_TILE, tabs=tabs_p, tabspec=tabspec_p,
        c0=c0, n0=n0, m0=m0)

    ns = Bs * Ts
    mods_s = [mod[l, Bp:Bp + Bs][None] for l in range(2)]
    kvmod_s = kvm[0, Bp:Bp + Bs][None]
    rep = (jnp.arange(ns)[:, None] // Ts == jnp.arange(Bs)[None, :]).astype(BF16)

    def mspec_s(col):
        return pl.BlockSpec((None, Bs, D_MODEL), lambda i, *_: (0, 0, col))

    tabs_s = _rope_tables(PAST_LEN + jnp.arange(Ts, dtype=jnp.int32))
    tabs_s = tuple(jnp.tile(t, (Bs, 1)) for t in tabs_s)
    m0s = jnp.pad(state_m[0][:, None, :], ((0, 0), (0, 0), (0, LANES - M_HEADS)))
    ys, cs, ns_, ms, kws, vws = _trunk(
        x_sample.reshape(ns, D), mods_s, kvmod_s, mspec_s, mspec_s, rep, w,
        B=Bs, T=Ts, L=SAMPLE_PAD, tm=ns, tm_big=ns, tabs=tabs_s,
        tabspec=pl.BlockSpec((ns, LANES), lambda i, *_: (0, 0)),
        c0=state_c[0], n0=state_n[0].reshape(Bs, npairs, LANES), m0=m0s,
        cache_k=cache_k_win, cache_v=cache_v_win)

    return (yp.reshape(Bp, Tp, D), ys.reshape(Bs, Ts, D), cp, np_, mp, kwp, vwp,
            cs, ns_, ms, kws, vws)
```

```python
import functools

import jax
import jax.numpy as jnp
from jax import lax
from jax.experimental import pallas as pl
from jax.experimental.pallas import tpu as pltpu

F32 = jnp.float32
BF16 = jnp.bfloat16

D_MODEL = 1024
PAST_LEN = 8192
M_HEADS = 8
M_DK = 64
M_DV = 128
M_CHUNK = 128
GATE_SOFTCAP = 15.0
M_EMPTY = -1e30
WINDOW = 128
HEAD_DIM = 64
N_Q_HEADS = 16
N_KV_HEADS = 2
ROPE_THETA = 500000.0
ROT_DIM = 16
N_GROUPS = 4
EXPERTS_PER_GROUP = 4
N_EXPERTS = 16
D_EXPERT = 256
RMS_EPS = 1e-6

LANES = 128
QK_COLS = 2 * M_HEADS * M_DK
V_COLS = M_HEADS * M_DV
SAMPLE_PAD = 16
TOKEN_TILE = 512
BIG_TOKEN_TILE = 1024
VMEM_LIMIT = 52 * 1024 * 1024

NT_DIMS = (((1,), (1,)), ((), ()))


def _params(n_axes):
    return pltpu.CompilerParams(dimension_semantics=("arbitrary",) * n_axes,
                                vmem_limit_bytes=VMEM_LIMIT)


def _dot(a, b):
    return jnp.dot(a, b, preferred_element_type=F32)


def _dot_nt(a, b):
    return lax.dot_general(a, b, NT_DIMS, preferred_element_type=F32)


def _split2(x):
    hi = x.astype(BF16)
    lo = (x - hi.astype(F32)).astype(BF16)
    return hi, lo


def _split3(x):
    hi = x.astype(BF16)
    r = x - hi.astype(F32)
    mid = r.astype(BF16)
    lo = (r - mid.astype(F32)).astype(BF16)
    return hi, mid, lo


def _unit_rms(x):
    return x * lax.rsqrt(jnp.mean(x * x, axis=-1, keepdims=True) + RMS_EPS)


def _ada_kernel(c_ref, w_ref, b_ref, o_ref):
    c = c_ref[...]
    cs = (c * jax.nn.sigmoid(c)).astype(BF16)
    o_ref[...] = _dot(cs, w_ref[...].astype(BF16)) + b_ref[...]


def _ada_call(c, w, b):
    g, d, n = w.shape
    r = c.shape[0]
    tn = 2048
    return pl.pallas_call(
        _ada_kernel,
        out_shape=jax.ShapeDtypeStruct((g, r, n), F32),
        grid=(g, n // tn),
        in_specs=[pl.BlockSpec((r, d), lambda i, j: (0, 0)),
                  pl.BlockSpec((None, d, tn), lambda i, j: (i, 0, j)),
                  pl.BlockSpec((None, 1, tn), lambda i, j: (i, 0, j))],
        out_specs=pl.BlockSpec((None, r, tn), lambda i, j: (i, 0, j)),
        compiler_params=_params(2),
        name="ada_mod",
    )(c, w, b)


def _mod_rows(ref, rep_ref):
    if rep_ref is None:
        return ref[...]
    hi, mid, lo = _split3(ref[...])
    rep = rep_ref[...]
    return _dot(rep, hi) + _dot(rep, mid) + _dot(rep, lo)


def _inproj_kernel(expand, x_ref, sh_ref, sc_ref, nw_ref, w_ref, wg_ref, bli_ref, blf_ref, *rest):
    rep_ref = rest[0] if expand else None
    qk_ref, v_ref, o_ref, li_ref, lf_ref, wb_ref = rest[1 if expand else 0:]

    @pl.when(pl.program_id(0) == 0)
    def _():
        wb_ref[...] = w_ref[...].astype(BF16)

    u = _unit_rms(x_ref[...]) * (nw_ref[...] * (1.0 + _mod_rows(sc_ref, rep_ref))) \
        + _mod_rows(sh_ref, rep_ref)
    ub = u.astype(BF16)
    half = QK_COLS // 2
    q = _dot(ub, wb_ref[:, 0:half]) * (M_DK ** -0.5)
    qk_ref[:, 0:half] = q.astype(BF16)
    qk_ref[:, half:QK_COLS] = _dot(ub, wb_ref[:, half:QK_COLS]).astype(BF16)
    v_ref[...] = _dot(ub, wb_ref[:, QK_COLS:QK_COLS + V_COLS]).astype(BF16)
    o_ref[...] = jax.nn.sigmoid(_dot(ub, wb_ref[:, QK_COLS + V_COLS:QK_COLS + 2 * V_COLS])).astype(BF16)
    lane = lax.broadcasted_iota(jnp.int32, (1, LANES), 1)
    live = lane < M_HEADS
    gates = _dot(ub, wg_ref[...].astype(BF16))
    gi = gates[:, 0:LANES] + bli_ref[...]
    gf = gates[:, LANES:2 * LANES] + blf_ref[...]
    li = GATE_SOFTCAP * jnp.tanh(gi / GATE_SOFTCAP)
    fpre = GATE_SOFTCAP * jnp.tanh(gf / GATE_SOFTCAP)
    lf = jnp.minimum(fpre, 0.0) - jnp.log1p(jnp.exp(-jnp.abs(fpre)))
    li_ref[...] = jnp.where(live, li, 0.0)
    lf_ref[...] = jnp.where(live, lf, 0.0)


def _inproj_call(x, mod, mspec, rep, nw, w_in, wg, bli, blf, tm):
    n = x.shape[0]
    row = lambda i: (i, 0)
    const = lambda i: (0, 0)
    main_cols = QK_COLS + 2 * V_COLS
    expand = rep is not None
    return pl.pallas_call(
        functools.partial(_inproj_kernel, expand),
        out_shape=(jax.ShapeDtypeStruct((n, QK_COLS), BF16),
                   jax.ShapeDtypeStruct((n, V_COLS), BF16),
                   jax.ShapeDtypeStruct((n, V_COLS), BF16),
                   jax.ShapeDtypeStruct((n, LANES), F32),
                   jax.ShapeDtypeStruct((n, LANES), F32)),
        grid=(n // tm,),
        in_specs=[pl.BlockSpec((tm, D_MODEL), row), mspec(0), mspec(1),
                  pl.BlockSpec((1, D_MODEL), const),
                  pl.BlockSpec((None, D_MODEL, main_cols), lambda i: (0, 0, 0),
                               pipeline_mode=pl.Buffered(1)),
                  pl.BlockSpec((D_MODEL, 2 * LANES), const),
                  pl.BlockSpec((1, LANES), const), pl.BlockSpec((1, LANES), const)]
        + ([pl.BlockSpec(rep.shape, const)] if expand else []),
        out_specs=(pl.BlockSpec((tm, QK_COLS), row), pl.BlockSpec((tm, V_COLS), row),
                   pl.BlockSpec((tm, V_COLS), row), pl.BlockSpec((tm, LANES), row),
                   pl.BlockSpec((tm, LANES), row)),
        scratch_shapes=[pltpu.VMEM((D_MODEL, main_cols), BF16)],
        compiler_params=_params(1),
        name="mlstm_inproj",
    )(x, mod, mod, nw, w_in, wg, bli, blf, *([rep] if expand else []))


def _mlstm_kernel(L, nseq, *refs):
    c = pl.program_id(1)
    seqs = [[r.at[b] for r in refs] for b in range(nseq)]

    @pl.when(c == 0)
    def _():
        for s in seqs:
            _mlstm_load_state(*s[4:7], *s[11:14])

    for s in seqs:
        _mlstm_seq_step(L, *s[0:4], s[7], *s[11:14])

    @pl.when(c == pl.num_programs(1) - 1)
    def _():
        for s in seqs:
            _mlstm_store_state(*s[8:14])


def _mlstm_load_state(c0_ref, n0_ref, m0_ref, c2_s, n_s, m_s):
    m_s[...] = m0_ref[...]
    n_s[...] = n0_ref[...]
    z = jnp.zeros((M_DK, M_DV), F32)
    for j in range(M_HEADS // 2):
        c2_s[j] = jnp.concatenate([jnp.concatenate([c0_ref[2 * j], z], 1),
                                   jnp.concatenate([z, c0_ref[2 * j + 1]], 1)], 0)


def _mlstm_seq_step(L, qk_ref, v_ref, li_ref, lf_ref, h_ref, c2_s, n_s, m_s):
    npairs = M_HEADS // 2
    hd = M_DK

    lane = lax.broadcasted_iota(jnp.int32, (1, LANES), 1)
    lo128 = lane < hd
    lane256 = lax.broadcasted_iota(jnp.int32, (1, 2 * LANES), 1)
    lo256 = lane256 < LANES
    row128 = lax.broadcasted_iota(jnp.int32, (LANES, 1), 0)
    top = row128 < hd
    blockdiag = (top & lo256) | (jnp.logical_not(top) & jnp.logical_not(lo256))

    LI = li_ref[...]
    LF = lf_ref[...]
    rowL = lax.broadcasted_iota(jnp.int32, (L, LANES), 0)

    def prefix(x, op, ident):
        d = 1
        while d < L:
            shifted = pltpu.roll(x, d, axis=0)
            x = op(x, jnp.where(rowL >= d, shifted, ident))
            d *= 2
        return x

    Bc = prefix(LF, jnp.add, 0.0)
    Cm = LI - Bc
    mprev = m_s[...]
    Gc = jnp.maximum(mprev, prefix(Cm, jnp.maximum, -jnp.inf))
    A = jnp.exp(mprev - Gc)
    bL = Bc[L - 1:L, :]
    DL = bL + Cm
    mnew = jnp.maximum(bL + mprev, jnp.max(DL, axis=0, keepdims=True))
    ast = jnp.exp(bL + mprev - mnew)
    WST = jnp.exp(DL - mnew)

    def pad_rows(x, rows):
        if x.shape[0] == rows:
            return x
        return jnp.concatenate([x, jnp.zeros((rows - x.shape[0], x.shape[1]), x.dtype)], 0)

    kw_lanes = max(L, hd)
    assert kw_lanes in (hd, LANES) and L <= LANES
    wide = kw_lanes == LANES
    lane_s = lax.broadcasted_iota(jnp.int32, (1, 2 * kw_lanes), 1)
    row_s = lax.broadcasted_iota(jnp.int32, (2 * kw_lanes, 1), 0)
    cm_pad = pad_rows(Cm, kw_lanes)
    if wide:
        XT = cm_pad.T
    else:
        XT = jnp.concatenate([cm_pad, cm_pad], 0).T
    causal = (lane_s & (kw_lanes - 1)) <= lax.broadcasted_iota(jnp.int32, (L, 2 * kw_lanes), 0)
    top_s = row_s < kw_lanes
    J = jnp.where((top_s & lo256) | (jnp.logical_not(top_s) & jnp.logical_not(lo256)),
                  1.0, 0.0).astype(BF16)

    EXPM = jnp.exp(-(Bc + Gc))

    def bc(X, h):
        return jnp.broadcast_to(X[:, h:h + 1], X.shape)

    for j in range(npairs):
        he, ho = 2 * j, 2 * j + 1
        p128 = slice(LANES * j, LANES * (j + 1))
        p256 = slice(2 * LANES * j, 2 * LANES * (j + 1))

        def pair128(X):
            return jnp.where(lo128, bc(X, he), bc(X, ho))

        def pair256(X):
            return jnp.concatenate([bc(X, he), bc(X, ho)], 1)

        q128 = qk_ref[:, LANES * j:LANES * (j + 1)]
        k128 = qk_ref[:, QK_COLS // 2 + LANES * j:QK_COLS // 2 + LANES * (j + 1)]
        v256 = v_ref[:, 2 * LANES * j:2 * LANES * (j + 1)]
        zk = jnp.zeros_like(k128)
        zv = jnp.zeros_like(v256)
        K2t = jnp.concatenate([pad_rows(jnp.where(lo128, k128, zk), kw_lanes),
                               pad_rows(jnp.where(lo128, zk, k128), kw_lanes)], 0)
        V2 = jnp.concatenate([pad_rows(jnp.where(lo256, v256, zv), kw_lanes),
                              pad_rows(jnp.where(lo256, zv, v256), kw_lanes)], 0)
        S = _dot_nt(q128, K2t)
        if wide:
            crow = jnp.concatenate([XT[he:he + 1, :], XT[ho:ho + 1, :]], 1)
            arg = jnp.where(causal, crow - pair256(Gc), -jnp.inf)
        else:
            crow = jnp.where(lo128, XT[he:he + 1, :], XT[ho:ho + 1, :])
            arg = jnp.where(causal, crow - pair128(Gc), -jnp.inf)
        Sw = (S * jnp.exp(arg)).astype(BF16)
        num_intra = _dot(Sw, V2)
        rowsum = _dot(Sw, J)
        C2 = c2_s[j]
        npair = n_s[j:j + 1, :]
        Nrow2 = jnp.concatenate(
            [jnp.broadcast_to(jnp.where(lo128, npair, 0.0), (LANES, LANES)),
             jnp.broadcast_to(jnp.where(lo128, 0.0, npair), (LANES, LANES))], 0).astype(BF16)
        qa = (q128.astype(F32) * pair128(A)).astype(BF16)
        num = _dot(qa, C2.astype(BF16)) + num_intra
        den = _dot_nt(qa, Nrow2) + rowsum
        h = num / jnp.maximum(jnp.abs(den), pair256(EXPM))
        h_ref[:, p256] = h.astype(h_ref.dtype)

        kw = k128.astype(F32) * pair128(WST)
        n_s[j:j + 1, :] = pair128(ast) * npair + jnp.sum(kw, axis=0, keepdims=True)
        kwT = pad_rows(kw, LANES).T.astype(BF16)
        dC = _dot(kwT, pad_rows(v256, LANES))
        c2_s[j] = pair256(ast) * C2 + jnp.where(blockdiag, dC, 0.0)

    m_s[...] = mnew


def _mlstm_store_state(cout_ref, nout_ref, mout_ref, c2_s, n_s, m_s):
    for j in range(M_HEADS // 2):
        C2 = c2_s[j]
        cout_ref[2 * j] = C2[0:M_DK, 0:M_DV]
        cout_ref[2 * j + 1] = C2[M_DK:2 * M_DK, M_DV:2 * M_DV]
    nout_ref[...] = n_s[...]
    mout_ref[...] = m_s[...]


MLSTM_SEQS_PER_STEP = 8


def _mlstm_call(qk, v, li, lf, c0, n0, m0, B, nc, L):
    npairs = M_HEADS // 2
    nseq = min(B, MLSTM_SEQS_PER_STEP)
    T = nc * L
    tok3 = lambda a: a.reshape(B, T, a.shape[-1])
    tok = lambda b, c: (b, c, 0)
    st4 = lambda b, c: (b, 0, 0, 0)
    st3 = lambda b, c: (b, 0, 0)
    h, c_new, n_new, m_new = pl.pallas_call(
        functools.partial(_mlstm_kernel, L, nseq),
        out_shape=(jax.ShapeDtypeStruct((B, T, V_COLS), BF16),
                   jax.ShapeDtypeStruct((B, M_HEADS, M_DK, M_DV), F32),
                   jax.ShapeDtypeStruct((B, npairs, LANES), F32),
                   jax.ShapeDtypeStruct((B, 1, LANES), F32)),
        grid=(B // nseq, nc),
        in_specs=[pl.BlockSpec((nseq, L, QK_COLS), tok), pl.BlockSpec((nseq, L, V_COLS), tok),
                  pl.BlockSpec((nseq, L, LANES), tok), pl.BlockSpec((nseq, L, LANES), tok),
                  pl.BlockSpec((nseq, M_HEADS, M_DK, M_DV), st4),
                  pl.BlockSpec((nseq, npairs, LANES), st3),
                  pl.BlockSpec((nseq, 1, LANES), st3)],
        out_specs=(pl.BlockSpec((nseq, L, V_COLS), tok),
                   pl.BlockSpec((nseq, M_HEADS, M_DK, M_DV), st4),
                   pl.BlockSpec((nseq, npairs, LANES), st3),
                   pl.BlockSpec((nseq, 1, LANES), st3)),
        scratch_shapes=[pltpu.VMEM((nseq, npairs, 2 * M_DK, 2 * M_DV), F32),
                        pltpu.VMEM((nseq, npairs, LANES), F32),
                        pltpu.VMEM((nseq, 1, LANES), F32)],
        compiler_params=_params(2),
        name="mlstm_chunks",
    )(tok3(qk), tok3(v), tok3(li), tok3(lf), c0, n0, m0)
    return h.reshape(B * T, V_COLS), c_new, n_new, m_new


def _route(lg):
    lane = lax.broadcasted_iota(jnp.int32, lg.shape, 1)
    lanef = lane.astype(F32)
    neg = -jnp.inf
    far = float(LANES)
    gm = lane < N_GROUPS
    lgm = jnp.where(gm, lg, neg)
    gmax = jnp.max(lgm, axis=-1, keepdims=True)
    gsum = jnp.sum(jnp.exp(lgm - gmax), axis=-1, keepdims=True)
    g_w = 1.0 / gsum
    gidx = jnp.min(jnp.where(gm & (lg == gmax), lanef, far), axis=-1, keepdims=True)
    first = N_GROUPS + EXPERTS_PER_GROUP * gidx
    sel = (lanef >= first) & (lanef < first + EXPERTS_PER_GROUP)
    l1 = jnp.max(jnp.where(sel, lg, neg), axis=-1, keepdims=True)
    i1 = jnp.min(jnp.where(sel & (lg == l1), lanef, far), axis=-1, keepdims=True)
    sel2 = sel & (lanef != i1)
    l2 = jnp.max(jnp.where(sel2, lg, neg), axis=-1, keepdims=True)
    i2 = jnp.min(jnp.where(sel2 & (lg == l2), lanef, far), axis=-1, keepdims=True)
    r = jnp.exp(l2 - l1)
    w1 = g_w / (1.0 + r)
    w2 = w1 * r
    return jnp.where(lanef == i1, w1, jnp.where(lanef == i2, w2, 0.0)), gidx


def _post_kernel(mlstm, routed, expand, *refs):
    refs = list(refs)
    h_ref = refs.pop(0)
    o_ref = refs.pop(0) if mlstm else None
    x_ref, g1_ref, sh2_ref, sc2_ref, nf_ref = refs[:5]
    refs = refs[5:]
    hn_ref = refs.pop(0) if mlstm else None
    wout_ref, wrhl_ref, br_ref = refs[:3]
    refs = refs[3:]
    tri_ref = refs.pop(0) if routed else None
    rep_ref = refs.pop(0) if expand else None
    x1_ref, u2_ref, gate_ref = refs[:3]
    outs = refs[3:]
    if mlstm:
        hf = h_ref[...].astype(F32)
        parts = [_unit_rms(hf[:, M_DV * i:M_DV * (i + 1)]) for i in range(M_HEADS)]
        hn = jnp.concatenate(parts, 1) * hn_ref[...]
        hg = (hn * o_ref[...].astype(F32)).astype(BF16)
    else:
        hg = h_ref[...]
    x1 = x_ref[...] + _mod_rows(g1_ref, rep_ref) * _dot(hg, wout_ref[...])
    x1_ref[...] = x1
    u2 = _unit_rms(x1) * (nf_ref[...] * (1.0 + _mod_rows(sc2_ref, rep_ref))) + _mod_rows(sh2_ref, rep_ref)
    uh, ul = _split2(u2)
    wrh = wrhl_ref[:, 0:LANES]
    lg = _dot(uh, wrh) + _dot(ul, wrh) + _dot(uh, wrhl_ref[:, LANES:2 * LANES]) + br_ref[...]
    gate, gidx = _route(lg)
    u2_ref[...] = uh
    gate_ref[...] = gate
    if not routed:
        return
    gid_ref, col_ref = outs
    rows = []
    for blk in range(x1.shape[0] // LANES):
        col = jnp.broadcast_to(gidx[LANES * blk:LANES * (blk + 1), :], (LANES, LANES))
        rows.append(col.T[0:1, :])
    gid_ref[...] = jnp.concatenate(rows, 1)
    lanef = lax.broadcasted_iota(jnp.int32, gate.shape, 1).astype(F32)
    onehot = jnp.where(lanef == gidx, 1.0, 0.0)
    earlier = _dot(tri_ref[...], onehot.astype(BF16))
    rank = jnp.sum(onehot * earlier, axis=-1, keepdims=True)
    col_ref[...] = jnp.where(lanef == 0.0, gidx, jnp.where(lanef == 1.0, rank, 0.0))


def _post_call(mlstm, routed, h, o, x, mod, mspec, rep, nf, hn, wout, wrhl, br, tm):
    n = x.shape[0]
    expand = rep is not None
    row = lambda i: (i, 0)
    const = lambda i: (0, 0)
    tok = pl.BlockSpec((tm, D_MODEL), row)
    small = pl.BlockSpec((tm, LANES), row)
    vec = pl.BlockSpec((1, D_MODEL), const)
    ins = [h] + ([o] if mlstm else []) + [x, mod, mod, mod, nf] + ([hn] if mlstm else []) \
        + [wout, wrhl, br]
    specs = [tok] + ([tok] if mlstm else []) + [tok, mspec(2), mspec(3), mspec(4), vec] \
        + ([vec] if mlstm else []) \
        + [pl.BlockSpec((D_MODEL, D_MODEL), const), pl.BlockSpec((D_MODEL, 2 * LANES), const),
           pl.BlockSpec((1, LANES), const)]
    out_shape = [jax.ShapeDtypeStruct((n, D_MODEL), F32), jax.ShapeDtypeStruct((n, D_MODEL), BF16),
                 jax.ShapeDtypeStruct((n, LANES), F32)]
    out_specs = [tok, tok, small]
    if routed:
        t = jnp.arange(tm)
        ins.append((t[None, :] < t[:, None]).astype(BF16))
        specs.append(pl.BlockSpec((tm, tm), const))
        out_shape += [jax.ShapeDtypeStruct((n // tm, 1, tm), F32), jax.ShapeDtypeStruct((n, LANES), F32)]
        out_specs += [pl.BlockSpec((None, 1, tm), lambda i: (i, 0, 0)), small]
    if expand:
        ins.append(rep)
        specs.append(pl.BlockSpec(rep.shape, const))
    out_shape, out_specs = tuple(out_shape), tuple(out_specs)
    return pl.pallas_call(
        functools.partial(_post_kernel, mlstm, routed, expand),
        out_shape=out_shape,
        grid=(n // tm,),
        in_specs=specs,
        out_specs=out_specs,
        compiler_params=_params(1),
        name="post_mlstm" if mlstm else "post_attn",
    )(*ins)


ITEM_ROWS = 8
RUN_ALIGN = 16
RUN_SIZES = (512, 256, 128, 64, 32, 16)
RUN_SRC, RUN_DST, RUN_LEN, RUN_TOTAL = 0, N_GROUPS, 2 * N_GROUPS, 3 * N_GROUPS


def _stage_rows(tm):
    assert N_GROUPS * (RUN_ALIGN - 1) <= LANES
    return tm + LANES


def _sorted_rows(n, tm):
    rows = n + N_GROUPS * (n // tm) * RUN_ALIGN + tm
    return -(-rows // tm) * tm


def _sort_meta_kernel(tm, nts, gid_ref, q_ref, runs_ref, items_ref):
    ntiles = gid_ref.shape[0]
    gid = gid_ref[...]
    r = lax.broadcasted_iota(jnp.int32, (tm, tm), 0)
    c = lax.broadcasted_iota(jnp.int32, (tm, tm), 1)
    before = jnp.where(r < c, 1.0, 0.0).astype(BF16)
    trow = lax.broadcasted_iota(jnp.int32, (ntiles, LANES), 0)
    lane = lax.broadcasted_iota(jnp.int32, (ntiles, LANES), 1)
    k = lax.broadcasted_iota(jnp.int32, (1, LANES), 1).astype(F32)
    zero11 = jnp.zeros((1, 1), F32)
    q = jnp.zeros((ntiles, tm), F32)
    runs = jnp.zeros((ntiles, LANES), F32)
    src = jnp.zeros((ntiles, 1), F32)
    start, nitems = zero11, zero11
    grp = jnp.zeros((1, LANES), F32)
    tile = jnp.zeros((1, LANES), F32)
    valid = jnp.zeros((1, LANES), F32)
    for g in range(N_GROUPS):
        mask = jnp.where(gid == float(g), 1.0, 0.0)
        cnt = jnp.sum(mask, axis=1, keepdims=True)
        padded = jnp.floor((cnt + (RUN_ALIGN - 1.0)) * (1.0 / RUN_ALIGN)) * RUN_ALIGN
        incl = jnp.broadcast_to(padded, (ntiles, LANES))
        d = 1
        while d < ntiles:
            incl = incl + jnp.where(trow >= d, pltpu.roll(incl, d, axis=0), 0.0)
            d *= 2
        total = incl[ntiles - 1:ntiles, 0:1]
        dst = start + incl[:, 0:1] - padded
        q = q + mask * (src + _dot(mask.astype(BF16), before))
        runs = runs + jnp.where(lane == RUN_SRC + g, src, 0.0) + jnp.where(lane == RUN_DST + g, dst, 0.0) \
            + jnp.where(lane == RUN_LEN + g, padded, 0.0)
        src = src + padded
        end = start + total
        ft = jnp.floor(start * (1.0 / tm))
        lt = jnp.floor((end - 1.0) * (1.0 / tm))
        ni = jnp.where(total > 0.0, lt - ft + 1.0, 0.0)
        inside = (k >= nitems) & (k < nitems + ni)
        grp = grp + jnp.where(inside, float(g), 0.0)
        tile = tile + jnp.where(inside, ft + (k - nitems), 0.0)
        valid = valid + jnp.where(inside, 1.0, 0.0)
        nitems = nitems + ni
        start = end
    runs = runs + jnp.where(lane == RUN_TOTAL, start, 0.0)
    live = valid > 0.0
    prev = pltpu.roll(tile, 1, axis=1)
    nxt = pltpu.roll(tile, LANES - 1, axis=1)
    first = live & ((k == 0.0) | (tile != prev))
    last = live & ((k == nitems - 1.0) | (tile != nxt))
    spare = jnp.floor((start - 1.0) * (1.0 / tm)) + 1.0 + (k - nitems)
    fill = jnp.logical_not(live) & (spare <= nts - 1.0)
    tile = jnp.where(live, tile, jnp.minimum(spare, nts - 1.0))
    grp = jnp.where(live, grp, jnp.max(grp, axis=1, keepdims=True))
    flag = lambda m: jnp.where(m, 1.0, 0.0)
    q_ref[...] = q
    runs_ref[...] = runs.astype(jnp.int32)
    table = jnp.concatenate([tile, grp, valid, flag(first | fill), flag(last | fill),
                             jnp.zeros((ITEM_ROWS - 5, LANES), F32)], 0)
    items_ref[...] = table.astype(jnp.int32)


def _max_items(nts):
    return nts + N_GROUPS - 1


def _sort_meta_call(gid, tm):
    ntiles = gid.shape[0]
    nts = _sorted_rows(ntiles * tm, tm) // tm
    assert _max_items(nts) <= LANES and tm <= RUN_SIZES[0]
    return pl.pallas_call(
        functools.partial(_sort_meta_kernel, tm, nts),
        out_shape=(jax.ShapeDtypeStruct((ntiles, tm), F32),
                   jax.ShapeDtypeStruct((ntiles, LANES), jnp.int32),
                   jax.ShapeDtypeStruct((ITEM_ROWS, LANES), jnp.int32)),
        compiler_params=_params(0),
        name="moe_sort_meta",
    )(gid)


def _run_copies(runs_ref, i, tile_ref, sorted_hbm, sem, to_sorted):
    pieces = []
    for g in range(N_GROUPS):
        src = runs_ref[i, RUN_SRC + g]
        dst = runs_ref[i, RUN_DST + g]
        length = runs_ref[i, RUN_LEN + g]
        for s in RUN_SIZES:
            def build(src=src, dst=dst, length=length, s=s):
                off = length & (-2 * s)
                a = tile_ref.at[pl.ds(pl.multiple_of(src + off, RUN_ALIGN), s), :]
                b = sorted_hbm.at[pl.ds(pl.multiple_of(dst + off, RUN_ALIGN), s), :]
                return pltpu.make_async_copy(a, b, sem) if to_sorted else pltpu.make_async_copy(b, a, sem)
            pieces.append(((length & s) != 0, build))
    return pieces


def _start(pieces):
    for pred, build in pieces:
        pl.when(pred)(lambda build=build: build().start())


def _wait(pieces):
    for pred, build in pieces:
        pl.when(pred)(lambda build=build: build().wait())


def _start_then_wait(pieces):
    _start(pieces)
    _wait(pieces)


def _scatter_kernel(tm, nts, ntiles, runs_ref, q_ref, u_ref, g_ref, su_hbm, sg_hbm,
                    stu, stg, sem_u, sem_g):
    i = pl.program_id(0)
    slot = i % 2

    def copies(tile, slot):
        return (_run_copies(runs_ref, tile, stu.at[slot], su_hbm, sem_u.at[slot], True)
                + _run_copies(runs_ref, tile, stg.at[slot], sg_hbm, sem_g.at[slot], True))

    r = lax.broadcasted_iota(jnp.int32, (_stage_rows(tm), tm), 0).astype(F32)
    perm = jnp.where(q_ref[...] == r, 1.0, 0.0).astype(BF16)
    stu[slot] = _dot(perm, u_ref[...]).astype(BF16)
    both = _dot(perm, jnp.concatenate(_split2(g_ref[...]), 1))
    stg[slot] = both[:, 0:LANES] + both[:, LANES:2 * LANES]
    _start(copies(i, slot))

    @pl.when(i > 0)
    def _():
        _wait(copies(i - 1, 1 - slot))

    @pl.when(i == ntiles - 1)
    def _():
        _wait(copies(i, slot))
        stu[slot] = jnp.zeros(stu.shape[1:], stu.dtype)
        stg[slot] = jnp.zeros(stg.shape[1:], stg.dtype)
        total = runs_ref[i, RUN_TOTAL]
        tail = nts * tm - total
        nfull = tail // tm
        pieces = []
        for tile_ref, hbm, sem in ((stu.at[slot], su_hbm, sem_u.at[slot]),
                                   (stg.at[slot], sg_hbm, sem_g.at[slot])):
            for j in range(nts - ntiles):
                def full(j=j, tile_ref=tile_ref, hbm=hbm, sem=sem):
                    dst = pl.multiple_of(total + j * tm, RUN_ALIGN)
                    return pltpu.make_async_copy(tile_ref.at[pl.ds(0, tm), :], hbm.at[pl.ds(dst, tm), :], sem)
                pieces.append((j < nfull, full))
            rem = tail - nfull * tm
            for s in RUN_SIZES:
                if s >= tm:
                    continue
                def part(s=s, tile_ref=tile_ref, hbm=hbm, sem=sem):
                    dst = pl.multiple_of(total + nfull * tm + (rem & (-2 * s)), RUN_ALIGN)
                    return pltpu.make_async_copy(tile_ref.at[pl.ds(0, s), :], hbm.at[pl.ds(dst, s), :], sem)
                pieces.append(((rem & s) != 0, part))
        _start_then_wait(pieces)


def _scatter_call(runs, q3, u2, gate, tm):
    n = u2.shape[0]
    ns = _sorted_rows(n, tm)
    grid_spec = pltpu.PrefetchScalarGridSpec(
        num_scalar_prefetch=1,
        grid=(n // tm,),
        in_specs=[pl.BlockSpec((None, 1, tm), lambda i, *_: (i, 0, 0)),
                  pl.BlockSpec((tm, D_MODEL), lambda i, *_: (i, 0)),
                  pl.BlockSpec((tm, LANES), lambda i, *_: (i, 0))],
        out_specs=(pl.BlockSpec(memory_space=pl.ANY), pl.BlockSpec(memory_space=pl.ANY)),
        scratch_shapes=[pltpu.VMEM((2, _stage_rows(tm), D_MODEL), BF16),
                        pltpu.VMEM((2, _stage_rows(tm), LANES), F32),
                        pltpu.SemaphoreType.DMA((2,)), pltpu.SemaphoreType.DMA((2,))])
    return pl.pallas_call(
        functools.partial(_scatter_kernel, tm, ns // tm, n // tm),
        out_shape=(jax.ShapeDtypeStruct((ns, D_MODEL), BF16), jax.ShapeDtypeStruct((ns, LANES), F32)),
        grid_spec=grid_spec,
        compiler_params=_params(1),
        name="moe_scatter",
    )(runs, q3, u2, gate)


def _ungroup(tm, runs_ref, col_ref, ys_hbm, stage, sem):
    i = pl.program_id(0)
    slot = i % 2

    def copies(tile, slot):
        return _run_copies(runs_ref, tile, stage.at[slot], ys_hbm, sem.at[slot], False)

    @pl.when(i == 0)
    def _():
        stage[...] = jnp.zeros_like(stage)
        _start(copies(i, slot))

    col = col_ref[...]
    gidx = col[:, 0:1]
    src = [runs_ref[i, RUN_SRC + g].astype(F32) for g in range(N_GROUPS)]
    first = src[N_GROUPS - 1]
    for g in range(N_GROUPS - 2, -1, -1):
        first = jnp.where(gidx == float(g), src[g], first)
    q = first + col[:, 1:2]
    lanes = lax.broadcasted_iota(jnp.int32, (tm, _stage_rows(tm)), 1).astype(F32)
    perm_t = jnp.where(q == lanes, 1.0, 0.0).astype(BF16)

    @pl.when(i + 1 < pl.num_programs(0))
    def _():
        _start(copies(i + 1, 1 - slot))

    _wait(copies(i, slot))
    return _dot(perm_t, stage[slot])


def _moe_routed_kernel(items_ref, su_ref, sg_ref, wgu_ref, wd_ref, out_ref, acc_ref, wgu_b, wd_b):
    kk = pl.program_id(0)
    grp = items_ref[1, kk]

    @pl.when((kk == 0) | (grp != items_ref[1, jnp.maximum(kk - 1, 0)]))
    def _():
        for e in range(EXPERTS_PER_GROUP):
            wgu_b[e] = wgu_ref[e].astype(BF16)
            wd_b[e] = wd_ref[e].astype(BF16)

    @pl.when(items_ref[3, kk] == 1)
    def _():
        acc_ref[...] = jnp.zeros_like(acc_ref)

    @pl.when(items_ref[2, kk] == 1)
    def _():
        u = su_ref[...]
        gate = sg_ref[...]
        lane = lax.broadcasted_iota(jnp.int32, gate.shape, 1)
        first_lane = N_GROUPS + EXPERTS_PER_GROUP * grp
        acts = []
        for e in range(EXPERTS_PER_GROUP):
            gcol = jnp.sum(jnp.where(lane == first_lane + e, gate, 0.0), axis=-1, keepdims=True)
            hu = _dot(u, wgu_b[e])
            hg = hu[:, 0:D_EXPERT]
            acts.append((hg * jax.nn.sigmoid(hg) * hu[:, D_EXPERT:2 * D_EXPERT] * gcol).astype(BF16))
        wd_all = wd_b[...].reshape(EXPERTS_PER_GROUP * D_EXPERT, D_MODEL)
        acc_ref[...] += _dot(jnp.concatenate(acts, 1), wd_all)

    @pl.when(items_ref[4, kk] == 1)
    def _():
        out_ref[...] = acc_ref[...].astype(out_ref.dtype)


def _moe_routed_call(items, su, sg, wgu, wd, layer, tm):
    ns = su.shape[0]
    n_items = _max_items(ns // tm)
    depth = wgu.shape[0]
    wgu5 = wgu.reshape(depth, N_GROUPS, EXPERTS_PER_GROUP, D_MODEL, 2 * D_EXPERT)
    wd5 = wd.reshape(depth, N_GROUPS, EXPERTS_PER_GROUP, D_EXPERT, D_MODEL)
    grid_spec = pltpu.PrefetchScalarGridSpec(
        num_scalar_prefetch=1,
        grid=(n_items,),
        in_specs=[pl.BlockSpec((tm, D_MODEL), lambda k, it: (it[0, k], 0)),
                  pl.BlockSpec((tm, LANES), lambda k, it: (it[0, k], 0)),
                  pl.BlockSpec((None, None, EXPERTS_PER_GROUP, D_MODEL, 2 * D_EXPERT),
                               lambda k, it: (layer, it[1, k], 0, 0, 0)),
                  pl.BlockSpec((None, None, EXPERTS_PER_GROUP, D_EXPERT, D_MODEL),
                               lambda k, it: (layer, it[1, k], 0, 0, 0))],
        out_specs=pl.BlockSpec((tm, D_MODEL), lambda k, it: (it[0, k], 0)),
        scratch_shapes=[pltpu.VMEM((tm, D_MODEL), F32),
                        pltpu.VMEM((EXPERTS_PER_GROUP, D_MODEL, 2 * D_EXPERT), BF16),
                        pltpu.VMEM((EXPERTS_PER_GROUP, D_EXPERT, D_MODEL), BF16)])
    return pl.pallas_call(
        _moe_routed_kernel,
        out_shape=jax.ShapeDtypeStruct((ns, D_MODEL), BF16),
        grid_spec=grid_spec,
        compiler_params=_params(1),
        name="moe_routed",
    )(items, su, sg, wgu5, wd5)


def _resid_gather_kernel(tm, runs_ref, x_ref, g2_ref, col_ref, ys_hbm, y_ref, stage, sem):
    y_ref[...] = x_ref[...] + g2_ref[...] * _ungroup(tm, runs_ref, col_ref, ys_hbm, stage, sem)


def _resid_gather_call(x, runs, col, ys, mod, mspec, tm):
    n = x.shape[0]
    row = lambda i, *_: (i, 0)
    grid_spec = pltpu.PrefetchScalarGridSpec(
        num_scalar_prefetch=1,
        grid=(n // tm,),
        in_specs=[pl.BlockSpec((tm, D_MODEL), row), mspec(5), pl.BlockSpec((tm, LANES), row),
                  pl.BlockSpec(memory_space=pl.ANY)],
        out_specs=pl.BlockSpec((tm, D_MODEL), row),
        scratch_shapes=[pltpu.VMEM((2, _stage_rows(tm), D_MODEL), BF16), pltpu.SemaphoreType.DMA((2,))])
    return pl.pallas_call(
        functools.partial(_resid_gather_kernel, tm),
        out_shape=jax.ShapeDtypeStruct((n, D_MODEL), F32),
        grid_spec=grid_spec,
        compiler_params=_params(1),
        name="moe_unsort_resid",
    )(runs, x, mod, col, ys)


def _moe_kernel(expand, u_ref, gate_ref, wgu_ref, wd_ref, x1_ref, g2_ref, *rest):
    rep_ref = rest[0] if expand else None
    y_ref, acc_ref = rest[1 if expand else 0:]
    grp = pl.program_id(1)

    @pl.when(grp == 0)
    def _():
        acc_ref[...] = jnp.zeros_like(acc_ref)

    u = u_ref[...]
    gate = gate_ref[...]
    lane = lax.broadcasted_iota(jnp.int32, gate.shape, 1)
    first_lane = N_GROUPS + EXPERTS_PER_GROUP * grp
    acts = []
    for e in range(EXPERTS_PER_GROUP):
        hu = _dot(u, wgu_ref[e].astype(BF16))
        hg = hu[:, 0:D_EXPERT]
        gcol = jnp.sum(jnp.where(lane == first_lane + e, gate, 0.0), axis=-1, keepdims=True)
        acts.append((hg * jax.nn.sigmoid(hg) * hu[:, D_EXPERT:2 * D_EXPERT] * gcol).astype(BF16))
    wd_all = wd_ref[...].astype(BF16).reshape(EXPERTS_PER_GROUP * D_EXPERT, D_MODEL)
    acc_ref[...] += _dot(jnp.concatenate(acts, 1), wd_all)

    @pl.when(grp == N_GROUPS - 1)
    def _():
        y_ref[...] = x1_ref[...] + _mod_rows(g2_ref, rep_ref) * acc_ref[...]


def _moe_call(u2, gate, wgu, wd, layer, x1, mod, mspec, rep, tm):
    n = u2.shape[0]
    row = lambda i, e: (i, 0)
    expand = rep is not None
    depth = wgu.shape[0]
    wgu5 = wgu.reshape(depth, N_GROUPS, EXPERTS_PER_GROUP, D_MODEL, 2 * D_EXPERT)
    wd5 = wd.reshape(depth, N_GROUPS, EXPERTS_PER_GROUP, D_EXPERT, D_MODEL)
    return pl.pallas_call(
        functools.partial(_moe_kernel, expand),
        out_shape=jax.ShapeDtypeStruct((n, D_MODEL), F32),
        grid=(n // tm, N_GROUPS),
        in_specs=[pl.BlockSpec((tm, D_MODEL), row), pl.BlockSpec((tm, LANES), row),
                  pl.BlockSpec((None, None, EXPERTS_PER_GROUP, D_MODEL, 2 * D_EXPERT),
                               lambda i, g: (layer, g, 0, 0, 0)),
                  pl.BlockSpec((None, None, EXPERTS_PER_GROUP, D_EXPERT, D_MODEL),
                               lambda i, g: (layer, g, 0, 0, 0)),
                  pl.BlockSpec((tm, D_MODEL), row), mspec(5)]
        + ([pl.BlockSpec(rep.shape, lambda i, e: (0, 0))] if expand else []),
        out_specs=pl.BlockSpec((tm, D_MODEL), row),
        scratch_shapes=[pltpu.VMEM((tm, D_MODEL), F32)],
        compiler_params=_params(2),
        name="moe_dense",
    )(u2, gate, wgu5, wd5, x1, mod, *([rep] if expand else []))


def _rope128(x, cos, sa, sb):
    return x * cos + pltpu.roll(x, LANES - ROT_DIM // 2, axis=1) * sa \
        + pltpu.roll(x, ROT_DIM // 2, axis=1) * sb


def _kvq_kernel(gather_tm, *refs):
    if gather_tm:
        (runs_ref, x_ref, g2_ref, col_ref, ys_hbm, kvsh_ref, kvsc_ref, sh1_ref, sc1_ref, kvn_ref,
         nm_ref, wkv_ref, wq_ref, kn_ref, qn_ref, cos_ref, sa_ref, sb_ref, g64_ref, gr_ref, gb_ref,
         q_ref, k_ref, v_ref, xa_ref, stage, sem) = refs
        x = x_ref[...] + g2_ref[...] * _ungroup(gather_tm, runs_ref, col_ref, ys_hbm, stage, sem)
        xa_ref[...] = x
        rep_ref = None
    else:
        (x_ref, kvsh_ref, kvsc_ref, sh1_ref, sc1_ref, kvn_ref, nm_ref,
         wkv_ref, wq_ref, kn_ref, qn_ref, cos_ref, sa_ref, sb_ref, g64_ref, gr_ref, gb_ref,
         rep_ref, q_ref, k_ref, v_ref) = refs
        x = x_ref[...]
    xn = _unit_rms(x)
    cos, sa, sb = cos_ref[...], sa_ref[...], sb_ref[...]

    ukv = xn * (kvn_ref[...] * (1.0 + _mod_rows(kvsc_ref, rep_ref))) + _mod_rows(kvsh_ref, rep_ref)
    kv = _dot(ukv.astype(BF16), wkv_ref[...])
    k = kv[:, 0:LANES]
    v_ref[...] = kv[:, LANES:2 * LANES]
    kh, kl = _split2(k * k)
    ms = _dot(kh, g64_ref[...]) + _dot(kl, g64_ref[...])
    k_ref[...] = _rope128(k * lax.rsqrt(ms + RMS_EPS) * kn_ref[...], cos, sa, sb)

    u1 = xn * (nm_ref[...] * (1.0 + _mod_rows(sc1_ref, rep_ref))) + _mod_rows(sh1_ref, rep_ref)
    q = _dot(u1.astype(BF16), wq_ref[...])
    ms16 = _dot((q * q).astype(BF16), gr_ref[...])
    rh, rl = _split2(lax.rsqrt(ms16 + RMS_EPS))
    rsb = _dot(jnp.concatenate([rh, rl], 1), gb_ref[...])
    qn = q * rsb * qn_ref[...]
    scale = HEAD_DIM ** -0.5
    cos_q, sa_q, sb_q = cos * scale, sa * scale, sb * scale
    for i in range(D_MODEL // LANES):
        sl = slice(LANES * i, LANES * (i + 1))
        q_ref[:, sl] = _rope128(qn[:, sl], cos_q, sa_q, sb_q).astype(BF16)


def _kvq_call(x, routed, mod0, mod1, mspec, kvmod, kvspec, rep, kvn, nm, wkv, wq, kn, qn, tabs, tab,
              g64, gr, gb, tm):
    n = x.shape[0]
    row = lambda i, *_: (i, 0)
    const = lambda i, *_: (0, 0)
    vec = pl.BlockSpec((1, D_MODEL), const)
    tok = pl.BlockSpec((tm, D_MODEL), row)
    small = pl.BlockSpec((tm, LANES), row)
    ins = [x, kvmod, kvmod, mod1, mod1, kvn, nm, wkv, wq, kn, qn, *tabs, g64, gr, gb]
    specs = [tok, kvspec(0), kvspec(1), mspec(0), mspec(1), vec, vec,
             pl.BlockSpec((D_MODEL, 2 * LANES), const), pl.BlockSpec((D_MODEL, D_MODEL), const),
             pl.BlockSpec((1, LANES), const), vec, tab, tab, tab,
             pl.BlockSpec((LANES, LANES), const), pl.BlockSpec((D_MODEL, LANES), const),
             pl.BlockSpec((2 * LANES, D_MODEL), const)]
    out_shape = [jax.ShapeDtypeStruct((n, D_MODEL), BF16), jax.ShapeDtypeStruct((n, LANES), F32),
                 jax.ShapeDtypeStruct((n, LANES), F32)]
    out_specs = [tok, small, small]
    scratch = []
    nprefetch = 0
    if routed is not None:
        runs, col, ys = routed
        nprefetch = 1
        ins = [runs, x, mod0, col, ys] + ins[1:]
        specs = [tok, mspec(5), small, pl.BlockSpec(memory_space=pl.ANY)] + specs[1:]
        out_shape.append(jax.ShapeDtypeStruct((n, D_MODEL), F32))
        out_specs.append(tok)
        scratch = [pltpu.VMEM((2, _stage_rows(tm), D_MODEL), BF16), pltpu.SemaphoreType.DMA((2,))]
    else:
        ins.append(rep)
        specs.append(pl.BlockSpec(rep.shape, const))
    grid_spec = pltpu.PrefetchScalarGridSpec(
        num_scalar_prefetch=nprefetch, grid=(n // tm,), in_specs=specs,
        out_specs=tuple(out_specs), scratch_shapes=scratch)
    return pl.pallas_call(
        functools.partial(_kvq_kernel, tm if routed is not None else 0),
        out_shape=tuple(out_shape),
        grid_spec=grid_spec,
        compiler_params=_params(1),
        name="kv_q_proj",
    )(*ins)


def _attn_core(q, kcat, vcat, bias, sinks_ref, o_ref):
    tk = kcat.shape[0]
    pairs = N_Q_HEADS // N_KV_HEADS // 2
    lane = lax.broadcasted_iota(jnp.int32, (1, LANES), 1)
    lo = lane < HEAD_DIM
    kro = pltpu.roll(kcat, HEAD_DIM, axis=1)
    vro = pltpu.roll(vcat, HEAD_DIM, axis=1)
    one_e = jnp.broadcast_to(jnp.where(lo, 1.0, 0.0), (tk, LANES))
    one_o = 1.0 - one_e
    for g in range(N_KV_HEADS):
        if g == 0:
            ke, ko = jnp.where(lo, kcat, 0.0), jnp.where(lo, 0.0, kro)
            ve, vo = jnp.where(lo, vcat, 0.0), jnp.where(lo, 0.0, vro)
        else:
            ke, ko = jnp.where(lo, kro, 0.0), jnp.where(lo, 0.0, kcat)
            ve, vo = jnp.where(lo, vro, 0.0), jnp.where(lo, 0.0, vcat)
        k2 = jnp.concatenate([ke, ko], 0).astype(BF16)
        v2 = jnp.concatenate([jnp.concatenate([ve, one_e], 1),
                              jnp.concatenate([vo, one_o], 1)], 0).astype(BF16)
        for p in range(pairs):
            hp = g * pairs + p
            s = _dot_nt(q[:, LANES * hp:LANES * (hp + 1)], k2)
            halves, corr = [], []
            for par in range(2):
                sp = s[:, par * tk:(par + 1) * tk] + bias
                sink = sinks_ref[2 * hp + par]
                m = jnp.maximum(jnp.max(sp, axis=-1, keepdims=True), sink)
                halves.append(jnp.exp(sp - m).astype(BF16))
                corr.append(jnp.exp(sink - m))
            o2 = _dot(jnp.concatenate(halves, 1), v2)
            den = o2[:, LANES:2 * LANES] + jnp.where(lo, corr[0], corr[1])
            o_ref[:, LANES * hp:LANES * (hp + 1)] = (o2[:, 0:LANES] / den).astype(o_ref.dtype)


ATTN_BLOCKS_PER_STEP = 8


def _attn_prompt_kernel(sinks_ref, q_ref, kp_ref, kc_ref, vp_ref, vc_ref, bias_ref, o_ref):
    kall = jnp.concatenate([kp_ref[...], kc_ref[...]], 0)
    vall = jnp.concatenate([vp_ref[...], vc_ref[...]], 0)
    first = jnp.minimum(pl.program_id(1), 1)
    for j in range(ATTN_BLOCKS_PER_STEP):
        rows = slice(j * WINDOW, (j + 1) * WINDOW)
        keys = slice(j * WINDOW, (j + 2) * WINDOW)
        bias = bias_ref[first] if j == 0 else bias_ref[1]
        _attn_core(q_ref[rows, :], kall[keys, :], vall[keys, :], bias, sinks_ref, o_ref.at[rows, :])


ATTN_SEQS_PER_STEP = 16


def _attn_sample_kernel(sinks_ref, q_ref, kc_ref, kn_ref, vc_ref, vn_ref, bias_ref, o_ref, kbuf, vbuf):
    @pl.when(pl.program_id(0) == 0)
    def _():
        kbuf[...] = jnp.zeros_like(kbuf)
        vbuf[...] = jnp.zeros_like(vbuf)

    for b in range(ATTN_SEQS_PER_STEP):
        kbuf[b, 0:WINDOW, :] = kc_ref[b]
        kbuf[b, WINDOW:WINDOW + SAMPLE_PAD, :] = kn_ref[b]
        vbuf[b, 0:WINDOW, :] = vc_ref[b]
        vbuf[b, WINDOW:WINDOW + SAMPLE_PAD, :] = vn_ref[b]
    for b in range(ATTN_SEQS_PER_STEP):
        _attn_core(q_ref[b], kbuf[b], vbuf[b], bias_ref[...], sinks_ref, o_ref.at[b])


def _window_bias(tq, first):
    qi = jnp.arange(tq)[:, None]
    kj = jnp.arange(2 * WINDOW)[None, :]
    ok = (kj > qi) & (kj <= qi + WINDOW)
    if first:
        ok = ok & (kj >= WINDOW)
    return jnp.where(ok, 0.0, -jnp.inf).astype(F32)


def _attn_prompt_call(sinks, q, k, v, B, nb):
    n = q.shape[0]
    per = ATTN_BLOCKS_PER_STEP
    steps = nb // per
    cur = lambda b, i: (b * steps + i, 0)
    prev = lambda b, i: (b * nb + jnp.maximum(per * i - 1, 0), 0)
    bias = jnp.stack([_window_bias(WINDOW, True), _window_bias(WINDOW, False)])
    kv_prev = pl.BlockSpec((WINDOW, LANES), prev)
    kv_cur = pl.BlockSpec((per * WINDOW, LANES), cur)
    return pl.pallas_call(
        _attn_prompt_kernel,
        out_shape=jax.ShapeDtypeStruct((n, D_MODEL), BF16),
        grid=(B, steps),
        in_specs=[pl.BlockSpec(memory_space=pltpu.SMEM),
                  pl.BlockSpec((per * WINDOW, D_MODEL), cur), kv_prev, kv_cur, kv_prev, kv_cur,
                  pl.BlockSpec((2, WINDOW, 2 * WINDOW), lambda b, i: (0, 0, 0))],
        out_specs=pl.BlockSpec((per * WINDOW, D_MODEL), cur),
        compiler_params=_params(2),
        name="swa_prompt",
    )(sinks, q, k, k, v, v, bias)


def _attn_sample_call(sinks, q, kcache, knew, vcache, vnew):
    B = q.shape[0]
    nseq = ATTN_SEQS_PER_STEP
    b3 = lambda b: (b, 0, 0)
    cache = pl.BlockSpec((nseq, WINDOW, LANES), b3)
    new = pl.BlockSpec((nseq, SAMPLE_PAD, LANES), b3)
    return pl.pallas_call(
        _attn_sample_kernel,
        out_shape=jax.ShapeDtypeStruct(q.shape, BF16),
        grid=(B // nseq,),
        in_specs=[pl.BlockSpec(memory_space=pltpu.SMEM),
                  pl.BlockSpec((nseq, SAMPLE_PAD, D_MODEL), b3), cache, new, cache, new,
                  pl.BlockSpec((SAMPLE_PAD, 2 * WINDOW), lambda b: (0, 0))],
        out_specs=pl.BlockSpec((nseq, SAMPLE_PAD, D_MODEL), b3),
        scratch_shapes=[pltpu.VMEM((nseq, 2 * WINDOW, LANES), F32),
                        pltpu.VMEM((nseq, 2 * WINDOW, LANES), F32)],
        compiler_params=_params(1),
        name="swa_sample",
    )(sinks, q, kcache, knew, vcache, vnew, _window_bias(SAMPLE_PAD, False))


def _rope_tables(pos):
    half = ROT_DIM // 2
    inv = ROPE_THETA ** (-jnp.arange(half, dtype=F32) / half)
    ang = pos.astype(F32)[:, None] * inv[None]
    cos, sin = jnp.cos(ang), jnp.sin(ang)
    d = jnp.arange(LANES) % HEAD_DIM
    idx = d % half
    cos_t = jnp.where(d < ROT_DIM, cos[:, idx], 1.0)
    sa = jnp.where(d < half, -sin[:, idx], 0.0)
    sb = jnp.where((d >= half) & (d < ROT_DIM), sin[:, idx], 0.0)
    return cos_t, sa, sb


def _pad_lanes(a, value=0.0):
    return jnp.pad(a, ((0, 0), (0, LANES - a.shape[1])), constant_values=value)


def _prep_weights(ada_w, ada_b, norm_mix, norm_ffn, a_w_in, a_b_gates, a_head_norm, a_w_out,
                  kv_ada_w, kv_ada_b, kv_norm, w_k, w_v, k_norm, b_w_q, b_q_norm, b_sinks, b_w_o,
                  moe_w_group, moe_b_group, moe_w_expert, moe_b_expert, moe_w_gate_up, moe_w_down):
    w = {}
    g0 = QK_COLS + 2 * V_COLS
    w["w_in"] = a_w_in
    w["w_gates"] = jnp.concatenate([_pad_lanes(a_w_in[0, :, g0:g0 + M_HEADS]),
                                    _pad_lanes(a_w_in[0, :, g0 + M_HEADS:])], 1)
    w["bli"] = _pad_lanes(a_b_gates[0][None, :M_HEADS])
    w["blf"] = _pad_lanes(a_b_gates[0][None, M_HEADS:])
    w["head_norm"] = a_head_norm[0][None]
    w["w_out"] = a_w_out[0].astype(BF16)
    w["norm_mix"] = [norm_mix[l][None] for l in range(2)]
    w["norm_ffn"] = [norm_ffn[l][None] for l in range(2)]
    w["router"] = []
    for l in range(2):
        wr = _pad_lanes(jnp.concatenate([moe_w_group[l], moe_w_expert[l]], 1))
        hi = wr.astype(BF16)
        lo = (wr - hi.astype(F32)).astype(BF16)
        br = _pad_lanes(jnp.concatenate([moe_b_group[l], moe_b_expert[l]])[None])
        w["router"].append((jnp.concatenate([hi, lo], 1), br))
    w["w_gu"] = moe_w_gate_up
    w["w_d"] = moe_w_down
    w["kv_norm"] = kv_norm[None]
    w["w_kv"] = jnp.concatenate([w_k, w_v], 1).astype(BF16)
    w["k_norm"] = jnp.tile(k_norm, N_KV_HEADS)[None]
    w["w_q"] = b_w_q[0].astype(BF16)
    w["q_norm"] = jnp.tile(b_q_norm[0], N_Q_HEADS)[None]
    w["sinks"] = b_sinks[0]
    w["w_o"] = b_w_o[0].astype(BF16)
    lanes = jnp.arange(LANES)
    feat = jnp.arange(D_MODEL)
    w["g64"] = jnp.where((lanes[:, None] // HEAD_DIM) == (lanes[None, :] // HEAD_DIM),
                         1.0 / HEAD_DIM, 0.0).astype(BF16)
    w["gr"] = jnp.where((feat[:, None] // HEAD_DIM) == lanes[None, :], 1.0 / HEAD_DIM, 0.0).astype(BF16)
    gb = jnp.where(lanes[:, None] == (feat[None, :] // HEAD_DIM), 1.0, 0.0).astype(BF16)
    w["gb"] = jnp.concatenate([gb, gb], 0)
    return w


def _trunk(x2, mods, kvmod, mspec, mspec_big, rep, w, *, B, T, L, tm, tm_big, tabs, tabspec, c0, n0, m0,
           cache_k=None, cache_v=None):
    sample = cache_k is not None
    nc = T // L if not sample else 1

    qk, v, o, li, lf = _inproj_call(x2, mods[0], mspec_big, rep, w["norm_mix"][0], w["w_in"],
                                    w["w_gates"], w["bli"], w["blf"], tm_big)
    if sample:
        def padtok(a, value=0.0):
            a = a.reshape(B, T, a.shape[-1])
            a = jnp.pad(a, ((0, 0), (0, L - T), (0, 0)), constant_values=value)
            return a.reshape(B * L, a.shape[-1])
        qk, v, li, lf = padtok(qk), padtok(v), padtok(li, M_EMPTY), padtok(lf)
    h, c_new, n_new, m_new = _mlstm_call(qk, v, li, lf, c0, n0, m0, B, nc, L)
    if sample:
        h = h.reshape(B, L, V_COLS)[:, :T].reshape(B * T, V_COLS)
    routed = not sample
    ntiles = (B * T) // tm

    def routed_moe(u2, gate, gid, col, layer):
        q, runs, items = _sort_meta_call(gid.reshape(ntiles, tm), tm)
        su, sg = _scatter_call(runs, q.reshape(ntiles, 1, tm), u2, gate, tm)
        ys = _moe_routed_call(items, su, sg, w["w_gu"], w["w_d"], layer, tm)
        return runs, col, ys

    post0 = _post_call(True, routed, h, o, x2, mods[0], mspec, rep, w["norm_ffn"][0], w["head_norm"],
                       w["w_out"], *w["router"][0], tm)
    kvq_args = (mods[0], mods[1], mspec, kvmod, mspec, rep, w["kv_norm"], w["norm_mix"][1],
                w["w_kv"], w["w_q"], w["k_norm"], w["q_norm"], tabs, tabspec,
                w["g64"], w["gr"], w["gb"], tm)

    if routed:
        x1, u2, gate, gid, col = post0
        q, k, vv, xa = _kvq_call(x1, routed_moe(u2, gate, gid, col, 0), *kvq_args)
    else:
        x1, u2, gate = post0
        xa = _moe_call(u2, gate, w["w_gu"], w["w_d"], 0, x1, mods[0], mspec_big, rep, tm_big)
        q, k, vv = _kvq_call(xa, None, *kvq_args)
    if not sample:
        att = _attn_prompt_call(w["sinks"], q, k, vv, B, T // WINDOW)
        k_win = k.reshape(B, T, LANES)[:, T - WINDOW:].reshape(B, WINDOW, N_KV_HEADS, HEAD_DIM)
        v_win = vv.reshape(B, T, LANES)[:, T - WINDOW:].reshape(B, WINDOW, N_KV_HEADS, HEAD_DIM)
    else:
        def padseq(a):
            return jnp.pad(a.reshape(B, T, a.shape[-1]), ((0, 0), (0, SAMPLE_PAD - T), (0, 0)))
        kc = cache_k.reshape(B, WINDOW, LANES)
        vc = cache_v.reshape(B, WINDOW, LANES)
        att = _attn_sample_call(w["sinks"], padseq(q), kc, padseq(k), vc, padseq(vv))
        att = att[:, :T].reshape(B * T, D_MODEL)
        k_win = jnp.concatenate([kc[:, T:], k.reshape(B, T, LANES)], 1)
        v_win = jnp.concatenate([vc[:, T:], vv.reshape(B, T, LANES)], 1)
        k_win = k_win.reshape(B, WINDOW, N_KV_HEADS, HEAD_DIM)
        v_win = v_win.reshape(B, WINDOW, N_KV_HEADS, HEAD_DIM)
    post1 = _post_call(False, routed, att, None, xa, mods[1], mspec, rep, w["norm_ffn"][1], None,
                       w["w_o"], *w["router"][1], tm)
    if routed:
        x3, u4, gate, gid, col = post1
        runs, col, ys = routed_moe(u4, gate, gid, col, 1)
        y = _resid_gather_call(x3, runs, col, ys, mods[1], mspec, tm)
    else:
        x3, u4, gate = post1
        y = _moe_call(u4, gate, w["w_gu"], w["w_d"], 1, x3, mods[1], mspec_big, rep, tm_big)
    c_out = c_new[None]
    n_out = n_new.reshape(1, B, M_HEADS, M_DK)
    m_out = m_new[:, 0, :M_HEADS][None]
    return y, c_out, n_out, m_out, k_win, v_win


def kernel(x_prompt, x_sample, c_prompt, c_sample, state_c, state_n, state_m, cache_k_win, cache_v_win, ada_w, ada_b, norm_mix, norm_ffn, a_w_in, a_b_gates, a_head_norm, a_w_out, kv_ada_w, kv_ada_b, kv_norm, w_k, w_v, k_norm, b_w_q, b_q_norm, b_sinks, b_w_o, moe_w_group, moe_b_group, moe_w_expert, moe_b_expert, moe_w_gate_up, moe_w_down):
    Bp, Tp, D = x_prompt.shape
    Bs, Ts, _ = x_sample.shape
    w = _prep_weights(ada_w, ada_b, norm_mix, norm_ffn, a_w_in, a_b_gates, a_head_norm, a_w_out,
                      kv_ada_w, kv_ada_b, kv_norm, w_k, w_v, k_norm, b_w_q, b_q_norm, b_sinks, b_w_o,
                      moe_w_group, moe_b_group, moe_w_expert, moe_b_expert, moe_w_gate_up, moe_w_down)

    rows = Bp + Bs
    rpad = -rows % 8
    c_all = jnp.concatenate([c_prompt, c_sample, jnp.zeros((rpad, D), F32)], 0)
    mod = _ada_call(c_all, ada_w, ada_b[:, None, :])
    kvm = _ada_call(c_all, kv_ada_w[None], kv_ada_b[None, None, :])

    tm_p = TOKEN_TILE
    tiles_per_seq = Tp // tm_p
    mods_p = [mod[l, :Bp][:, None, :] for l in range(2)]
    kvmod_p = kvm[0, :Bp][:, None, :]

    def mspec_p(col):
        return pl.BlockSpec((None, 1, D_MODEL), lambda i, *_: (i // tiles_per_seq, 0, col))

    big_tiles_per_seq = Tp // BIG_TOKEN_TILE

    def mspec_big_p(col):
        return pl.BlockSpec((None, 1, D_MODEL), lambda i, *_: (i // big_tiles_per_seq, 0, col))

    tabs_p = _rope_tables(jnp.arange(Tp, dtype=jnp.int32))
    tabspec_p = pl.BlockSpec((tm_p, LANES), lambda i, *_: (i % tiles_per_seq, 0))
    npairs = M_HEADS // 2
    c0 = jnp.zeros((Bp, M_HEADS, M_DK, M_DV), F32)
    n0 = jnp.zeros((Bp, npairs, LANES), F32)
    m0 = jnp.pad(jnp.full((Bp, 1, M_HEADS), M_EMPTY, F32), ((0, 0), (0, 0), (0, LANES - M_HEADS)))
    yp, cp, np_, mp, kwp, vwp = _trunk(
        x_prompt.reshape(Bp * Tp, D), mods_p, kvmod_p, mspec_p, mspec_big_p, None, w,
        B=Bp, T=Tp, L=M_CHUNK, tm=tm_p, tm_big=BIG_TOKEN_TILE, tabs=tabs_p, tabspec=tabspec_p,
        c0=c0, n0=n0, m0=m0)

    ns = Bs * Ts
    mods_s = [mod[l, Bp:Bp + Bs][None] for l in range(2)]
    kvmod_s = kvm[0, Bp:Bp + Bs][None]
    rep = (jnp.arange(ns)[:, None] // Ts == jnp.arange(Bs)[None, :]).astype(BF16)

    def mspec_s(col):
        return pl.BlockSpec((None, Bs, D_MODEL), lambda i, *_: (0, 0, col))

    tabs_s = _rope_tables(PAST_LEN + jnp.arange(Ts, dtype=jnp.int32))
    tabs_s = tuple(jnp.tile(t, (Bs, 1)) for t in tabs_s)
    m0s = jnp.pad(state_m[0][:, None, :], ((0, 0), (0, 0), (0, LANES - M_HEADS)))
    ys, cs, ns_, ms, kws, vws = _trunk(
        x_sample.reshape(ns, D), mods_s, kvmod_s, mspec_s, mspec_s, rep, w,
        B=Bs, T=Ts, L=SAMPLE_PAD, tm=ns, tm_big=ns, tabs=tabs_s,
        tabspec=pl.BlockSpec((ns, LANES), lambda i, *_: (0, 0)),
        c0=state_c[0], n0=state_n[0].reshape(Bs, npairs, LANES), m0=m0s,
        cache_k=cache_k_win, cache_v=cache_v_win)

    return (yp.reshape(Bp, Tp, D), ys.reshape(Bs, Ts, D), cp, np_, mp, kwp, vwp,
            cs, ns_, ms, kws, vws)
```
